```python
import jax
import jax.numpy as jnp
from jax import lax
import numpy as np

D_MODEL = 1024
BATCH = 8
SEQ = 2048
DEPTH = 4

GRID_W = 64
CTX_LEN = 256
N_EVEN = (DEPTH + 1) // 2
N_ODD = DEPTH // 2
N_MOD = 6
EPS = 1e-6
NEG_BIG = -1e9
LB_MIN = 1e-30

HG_WIDTH = D_MODEL // 2
HG_HEAD_DIM = 128
HG_HEADS = HG_WIDTH // HG_HEAD_DIM
HG_CHUNK = 64
RG_WIDTH = D_MODEL - HG_WIDTH
RG_BLOCKS = 8
RG_BLOCK = RG_WIDTH // RG_BLOCKS
RG_CONV = 4
RG_C = 8.0
EVEN_SPLITS = (HG_WIDTH, HG_WIDTH, HG_WIDTH, HG_WIDTH, HG_WIDTH, RG_WIDTH, RG_WIDTH)
EVEN_IN = 5 * HG_WIDTH + 2 * RG_WIDTH
NA_HEADS = 16
NA_HEAD_DIM = D_MODEL // NA_HEADS
NA_KR = 8
NA_KC = 16
NA_QB = 16
NA_CB = NA_QB + NA_KC
D_FF = 2816
N_EXPERTS = 8
TOP_K = 2
D_FF_EXPERT = 2816

kernel_name = "hybrid_hgrn2_rglru_natten_moe_dit"


def _rmsnorm(x, g):
    xf = x.astype(jnp.float32)
    y = xf * lax.rsqrt(jnp.mean(xf * xf, axis=-1, keepdims=True) + EPS)
    return (y * g.astype(jnp.float32)).astype(x.dtype)


def _modulate(x, g, shift, scale):
    return _rmsnorm(x, g) * (1 + scale) + shift


def _adaln(cond, w_mod, b_mod):
    return jnp.split(jax.nn.silu(cond) @ w_mod + b_mod, N_MOD, axis=-1)


def _swiglu(x, w1, w3, w2):
    return (jax.nn.silu(x @ w1) * (x @ w3)) @ w2


def _moe(x, w_router, w1, w3, w2):
    logits = (x @ w_router).astype(jnp.float32)
    top_v, top_i = lax.top_k(logits, TOP_K)
    top_w = jax.nn.softmax(top_v, axis=-1)
    gates = jnp.einsum('btk,btke->bte', top_w,
                       jax.nn.one_hot(top_i, N_EXPERTS, dtype=jnp.float32)).astype(x.dtype)
    y = gates[..., 0:1] * _swiglu(x, w1[0], w3[0], w2[0])
    for e in range(1, N_EXPERTS):
        y = y + gates[..., e:e + 1] * _swiglu(x, w1[e], w3[e], w2[e])
    return y


def _flip(a):
    return jnp.flip(a, axis=1)


def _prefix_scan(scan_fn, ctx_in, lat_in, reverse):
    if reverse:
        ctx_in = [_flip(a) for a in ctx_in]
        lat_in = [_flip(a) for a in lat_in]
    o_c, s_c = scan_fn(*ctx_in, None)
    o_l, _ = scan_fn(*lat_in, s_c)
    if reverse:
        o_c, o_l = _flip(o_c), _flip(o_l)
    return o_c, o_l


def _gla_chunk_scan(q, k, v, logf, s0):
    B, T, H, K = q.shape
    V = v.shape[-1]
    n = T // HG_CHUNK
    if s0 is None:
        s0 = jnp.zeros((B, H, K, V), jnp.float32)

    def to_chunks(a):
        return a.reshape(B, n, HG_CHUNK, H, a.shape[-1]).transpose(1, 0, 3, 2, 4)

    tri = np.tril(np.ones((HG_CHUNK, HG_CHUNK), dtype=bool))[:, :, None]

    def step(s, inp):
        qc, kc, vc, gc = inp
        b = jnp.cumsum(gc, axis=2)
        o_inter = jnp.einsum('bhtk,bhkv->bhtv', qc * jnp.exp(b), s)
        diff = jnp.where(tri, b[:, :, :, None, :] - b[:, :, None, :, :], NEG_BIG)
        attn = jnp.einsum('bhtk,bhtsk,bhsk->bhts', qc, jnp.exp(diff), kc)
        o = o_inter + jnp.einsum('bhts,bhsv->bhtv', attn, vc)
        b_last = b[:, :, -1:, :]
        s_new = (jnp.exp(b_last[:, :, 0, :])[..., None] * s
                 + jnp.einsum('bhsk,bhsv->bhkv', kc * jnp.exp(b_last - b), vc))
        return s_new, o

    s_fin, o = lax.scan(step, s0, (to_chunks(q), to_chunks(k), to_chunks(v), to_chunks(logf)))
    return o.transpose(1, 0, 3, 2, 4).reshape(B, T, H, V), s_fin


def _linear_scan(a, b, h0):
    if h0 is not None:
        b = b.at[:, 0].add(a[:, 0] * h0)
    _, h = lax.associative_scan(lambda l, r: (l[0] * r[0], r[0] * l[1] + r[1]), (a, b), axis=1)
    return h, h[:, -1]


def _hg_heads(a):
    return a.astype(jnp.float32).reshape(a.shape[:2] + (HG_HEADS, HG_HEAD_DIM))


def _hg_gates(z, lb):
    z = z.astype(jnp.float32)
    k = (1 - lb) * jax.nn.sigmoid(-z)
    logf = jnp.logaddexp(jnp.log(jnp.maximum(lb, LB_MIN)), jnp.log1p(-lb) + jax.nn.log_sigmoid(z))
    return _hg_heads(k), _hg_heads(logf)


def _dwconv_centred(x, w, b):
    left = RG_CONV // 2
    y = lax.conv_general_dilated(x, w[:, None, :], window_strides=(1,),
                                 padding=[(left, RG_CONV - 1 - left)],
                                 dimension_numbers=('NWC', 'WIO', 'NWC'),
                                 feature_group_count=x.shape[-1])
    return y + b


def _blockdiag(x, w, b):
    xb = x.reshape(x.shape[:-1] + (RG_BLOCKS, RG_BLOCK))
    return jnp.einsum('btni,nij->btnj', xb, w).reshape(x.shape) + b


def _rglru_coeffs(xc, wa, ba, wx, bx, lam):
    xf = xc.astype(jnp.float32)
    r = jax.nn.sigmoid(_blockdiag(xf, wa, ba))
    ig = jax.nn.sigmoid(_blockdiag(xf, wx, bx))
    log_a = RG_C * r * jax.nn.log_sigmoid(lam.astype(jnp.float32))
    return jnp.exp(log_a), jnp.sqrt(jnp.maximum(-jnp.expm1(2 * log_a), 0.0)) * (ig * xf)


def _even_mixer(u_ctx, u_lat, w_in, w_out, lb, onorm_g, conv_w, conv_b,
                wa, ba, wx, bx, lam, with_ctx_out):
    f32 = jnp.float32
    cuts = [int(v) for v in np.cumsum(EVEN_SPLITS)[:-1]]
    pc = jnp.split(u_ctx @ w_in, cuts, axis=-1)
    pl = jnp.split(u_lat @ w_in, cuts, axis=-1)
    hq = [_hg_heads(jax.nn.silu(p[0])) for p in (pc, pl)]
    hv = [_hg_heads(p[3]) for p in (pc, pl)]
    xr = [_dwconv_centred(p[5], conv_w, conv_b) for p in (pc, pl)]
    hg_c = hg_l = rg_c = rg_l = 0.0
    for d in range(2):
        kc, lfc = _hg_gates(pc[1 + d], lb[d])
        kl, lfl = _hg_gates(pl[1 + d], lb[d])
        o_c, o_l = _prefix_scan(_gla_chunk_scan, (hq[0], kc, hv[0], lfc),
                                (hq[1], kl, hv[1], lfl), d == 1)
        hg_c, hg_l = hg_c + o_c, hg_l + o_l
        ac, bc = _rglru_coeffs(xr[0], wa[d], ba[d], wx[d], bx[d], lam[d])
        al, bl = _rglru_coeffs(xr[1], wa[d], ba[d], wx[d], bx[d], lam[d])
        h_c, h_l = _prefix_scan(_linear_scan, (ac, bc), (al, bl), d == 1)
        rg_c, rg_l = rg_c + h_c, rg_l + h_l

    def merge(hg, rg, p):
        hg = _rmsnorm(hg.reshape(hg.shape[:2] + (HG_WIDTH,)), onorm_g) * jax.nn.silu(p[4].astype(f32))
        rg = rg * jax.nn.gelu(p[6].astype(f32))
        return jnp.concatenate([hg, rg], axis=-1).astype(p[0].dtype) @ w_out

    o_lat = merge(hg_l, rg_l, pl)
    o_ctx = merge(hg_c, rg_c, pc) if with_ctx_out else None
    return o_lat, o_ctx


def _na_column_tables():
    nb = GRID_W // NA_QB
    c0 = np.arange(nb) * NA_QB
    band = np.clip(c0 - NA_KC // 2, 0, GRID_W - NA_CB)
    col_idx = band[:, None] + np.arange(NA_CB)
    qcol = c0[:, None] + np.arange(NA_QB)
    wstart = np.clip(qcol - NA_KC // 2, 0, GRID_W - NA_KC)
    kcol = col_idx[:, None, :]
    mask = (kcol >= wstart[..., None]) & (kcol < wstart[..., None] + NA_KC)
    off = np.clip(kcol - qcol[..., None] + NA_KC - 1, 0, 2 * NA_KC - 2)
    return col_idx, mask, off


def _na_mixer(u_ctx, u_lat, w_qkv, w_o, q_g, k_g, rpb, with_ctx_out):
    f32 = jnp.float32
    B, S, _ = u_lat.shape
    rows = S // GRID_W
    kr = min(NA_KR, rows)
    nb = GRID_W // NA_QB
    scale = NA_HEAD_DIM ** -0.5

    def qkv(u):
        q, k, v = jnp.split(u @ w_qkv, 3, axis=-1)
        shp = u.shape[:2] + (NA_HEADS, NA_HEAD_DIM)
        return (_rmsnorm(q.reshape(shp), q_g) * scale, _rmsnorm(k.reshape(shp), k_g), v.reshape(shp))

    qc, kc, vc = qkv(u_ctx)
    ql, kl, vl = qkv(u_lat)
    kc_h = kc.transpose(0, 2, 1, 3)
    vc_h = vc.transpose(0, 2, 1, 3)

    def to_grid(a):
        return a.reshape(B, rows, GRID_W, NA_HEADS, NA_HEAD_DIM).transpose(0, 3, 1, 2, 4)

    qg, kg, vg = to_grid(ql), to_grid(kl), to_grid(vl)
    col_idx, col_mask, col_off = _na_column_tables()
    bias = rpb.astype(f32)[:, :, col_off].transpose(0, 2, 3, 1, 4)
    bias = jnp.where(col_mask[:, :, None, :], bias, NEG_BIG)
    n_loc = kr * NA_CB

    def row_block(r):
        rs = jnp.clip(r - kr // 2, 0, rows - kr)
        q_r = lax.dynamic_index_in_dim(qg, r, axis=2, keepdims=False).reshape(
            B, NA_HEADS, nb, NA_QB, NA_HEAD_DIM)
        k_b = lax.dynamic_slice_in_dim(kg, rs, kr, axis=2)[:, :, :, col_idx]
        v_b = lax.dynamic_slice_in_dim(vg, rs, kr, axis=2)[:, :, :, col_idx]
        b_r = jnp.take(bias, rs + jnp.arange(kr) - r + (NA_KR - 1), axis=3)
        s_loc = jnp.einsum('bhnqd,bhrncd->bhnqrc', q_r, k_b).astype(f32) + b_r
        s_ctx = jnp.einsum('bhnqd,bhld->bhnql', q_r, kc_h).astype(f32)
        s = jnp.concatenate([s_loc.reshape(B, NA_HEADS, nb, NA_QB, n_loc), s_ctx], axis=-1)
        p = jax.nn.softmax(s, axis=-1).astype(vg.dtype)
        p_loc = p[..., :n_loc].reshape(B, NA_HEADS, nb, NA_QB, kr, NA_CB)
        o = (jnp.einsum('bhnqrc,bhrncd->bhnqd', p_loc, v_b)
             + jnp.einsum('bhnql,bhld->bhnqd', p[..., n_loc:], vc_h))
        return o.reshape(B, NA_HEADS, GRID_W, NA_HEAD_DIM)

    o = lax.map(row_block, jnp.arange(rows))
    o_lat = o.transpose(1, 0, 3, 2, 4).reshape(B, S, D_MODEL) @ w_o
    o_ctx = None
    if with_ctx_out:
        s = jnp.einsum('blhd,bmhd->bhlm', qc, kc).astype(f32)
        p = jax.nn.softmax(s, axis=-1).astype(vc.dtype)
        o_ctx = jnp.einsum('bhlm,bmhd->blhd', p, vc).reshape(u_ctx.shape) @ w_o
    return o_lat, o_ctx


def setup_inputs(seed: int = 0) -> dict:
    key = jax.random.key(seed)
    keys = iter(jax.random.split(key, 40))
    f32 = jnp.float32
    D = D_MODEL

    def nrm(shape, scale):
        return jax.random.normal(next(keys), shape, f32) * scale

    def gain(shape):
        return 1.0 + nrm(shape, 0.1)

    a0 = jax.random.uniform(next(keys), (N_EVEN, 2, RG_WIDTH), f32, 0.9, 0.999)
    p0 = a0 ** (1.0 / RG_C)
    return {
        "x": nrm((BATCH, SEQ, D), 1.0),
        "c": nrm((BATCH, D), 1.0),
        "ctx": nrm((BATCH, CTX_LEN, D), 1.0),
        "c_ctx": nrm((D,), 1.0),
        "w_mod": nrm((DEPTH, D, N_MOD * D), 0.5 * D ** -0.5),
        "b_mod": nrm((DEPTH, N_MOD * D), 0.05),
        "norm_mix_g": gain((DEPTH, D)),
        "norm_ffn_g": gain((DEPTH, D)),
        "ev_w_in": nrm((N_EVEN, D, EVEN_IN), D ** -0.5),
        "ev_w_out": nrm((N_EVEN, D, D), D ** -0.5),
        "hg_lb_logits": nrm((N_EVEN, 2, HG_WIDTH), 1.0),
        "hg_onorm_g": gain((N_EVEN, HG_WIDTH)),
        "rg_conv_w": nrm((N_EVEN, RG_CONV, RG_WIDTH), RG_CONV ** -0.5),
        "rg_conv_b": nrm((N_EVEN, RG_WIDTH), 0.1),
        "rg_wa": nrm((N_EVEN, 2, RG_BLOCKS, RG_BLOCK, RG_BLOCK), RG_BLOCK ** -0.5),
        "rg_ba": nrm((N_EVEN, 2, RG_WIDTH), 0.1),
        "rg_wx": nrm((N_EVEN, 2, RG_BLOCKS, RG_BLOCK, RG_BLOCK), RG_BLOCK ** -0.5),
        "rg_bx": nrm((N_EVEN, 2, RG_WIDTH), 0.1),
        "rg_lambda": jnp.log(p0) - jnp.log1p(-p0),
        "na_w_qkv": nrm((N_ODD, D, 3 * D), D ** -0.5),
        "na_w_o": nrm((N_ODD, D, D), D ** -0.5),
        "na_q_g": gain((N_ODD, NA_HEAD_DIM)),
        "na_k_g": gain((N_ODD, NA_HEAD_DIM)),
        "na_rpb": nrm((N_ODD, NA_HEADS, 2 * NA_KR - 1, 2 * NA_KC - 1), 0.5),
        "ffn_w1": nrm((N_EVEN, D, D_FF), D ** -0.5),
        "ffn_w3": nrm((N_EVEN, D, D_FF), D ** -0.5),
        "ffn_w2": nrm((N_EVEN, D_FF, D), D_FF ** -0.5),
        "moe_router": nrm((N_ODD, D, N_EXPERTS), D ** -0.5),
        "moe_w1": nrm((N_ODD, N_EXPERTS, D, D_FF_EXPERT), D ** -0.5),
        "moe_w3": nrm((N_ODD, N_EXPERTS, D, D_FF_EXPERT), D ** -0.5),
        "moe_w2": nrm((N_ODD, N_EXPERTS, D_FF_EXPERT, D), D_FF_EXPERT ** -0.5),
    }


def reference(x, c, ctx, c_ctx, w_mod, b_mod, norm_mix_g, norm_ffn_g,
              ev_w_in, ev_w_out, hg_lb_logits, hg_onorm_g, rg_conv_w, rg_conv_b,
              rg_wa, rg_ba, rg_wx, rg_bx, rg_lambda,
              na_w_qkv, na_w_o, na_q_g, na_k_g, na_rpb,
              ffn_w1, ffn_w3, ffn_w2, moe_router, moe_w1, moe_w3, moe_w2):
    n_ctx = ctx.shape[1]
    lb_p = jax.nn.softmax(hg_lb_logits.astype(jnp.float32), axis=0)
    lb_all = jnp.cumsum(lb_p, axis=0) - lb_p[0]
    h, g = x, ctx
    for layer in range(DEPTH):
        j = layer // 2
        last = layer == DEPTH - 1
        sh1, sc1, gt1, sh2, sc2, gt2 = [t[:, None, :] for t in _adaln(c, w_mod[layer], b_mod[layer])]
        csh1, csc1, cgt1, csh2, csc2, cgt2 = _adaln(c_ctx, w_mod[layer], b_mod[layer])
        u_lat = _modulate(h, norm_mix_g[layer], sh1, sc1)
        u_ctx = _modulate(g, norm_mix_g[layer], csh1, csc1)
        if layer % 2 == 0:
            o_lat, o_ctx = _even_mixer(u_ctx, u_lat, ev_w_in[j], ev_w_out[j], lb_all[j], hg_onorm_g[j],
                                       rg_conv_w[j], rg_conv_b[j], rg_wa[j], rg_ba[j],
                                       rg_wx[j], rg_bx[j], rg_lambda[j], not last)
        else:
            o_lat, o_ctx = _na_mixer(u_ctx, u_lat, na_w_qkv[j], na_w_o[j], na_q_g[j], na_k_g[j],
                                     na_rpb[j], not last)
        h = h + gt1 * o_lat
        u_lat = _modulate(h, norm_ffn_g[layer], sh2, sc2)
        if last:
            tokens = u_lat
        else:
            g = g + cgt1 * o_ctx
            tokens = jnp.concatenate([_modulate(g, norm_ffn_g[layer], csh2, csc2), u_lat], axis=1)
        if layer % 2 == 0:
            y = _swiglu(tokens, ffn_w1[j], ffn_w3[j], ffn_w2[j])
        else:
            y = _moe(tokens, moe_router[j], moe_w1[j], moe_w3[j], moe_w2[j])
        if last:
            h = h + gt2 * y
        else:
            g = g + cgt2 * y[:, :n_ctx]
            h = h + gt2 * y[:, n_ctx:]
    return h
```

```python
import functools

import numpy as np
import jax
import jax.numpy as jnp
from jax import lax
from jax.experimental import pallas as pl
from jax.experimental.pallas import tpu as pltpu

F32 = jnp.float32
BF16 = jnp.bfloat16

EPS = 1e-6
NEG_BIG = -1e9
LB_MIN = 1e-30
N_MOD = 6
GRID_W = 64
HG_HEAD_DIM = 128
RG_BLOCKS = 8
RG_CONV = 4
RG_C = 8.0
NA_HEADS = 16
NA_KR = 8
NA_KC = 16
N_EXPERTS = 8

LANES = 128
TOKEN_TILE = 256
HG_CHUNK = 128
HG_LEVELS = 7
RG_SCAN_BLOCK = 256
VMEM_LIMIT = 56 * 1024 * 1024


def _cparams(*sem):
    return pltpu.CompilerParams(dimension_semantics=sem, vmem_limit_bytes=VMEM_LIMIT)


def _dot(a, b):
    return jnp.dot(a, b, preferred_element_type=F32)


def _dot_nt(a, b):
    return lax.dot_general(a, b, (((1,), (1,)), ((), ())), preferred_element_type=F32)


def _dot_tn(a, b):
    return lax.dot_general(a, b, (((0,), (0,)), ((), ())), preferred_element_type=F32)


def _split2(x):
    hi = x.astype(BF16)
    lo = (x - hi.astype(F32)).astype(BF16)
    return hi, lo


def _split3(x):
    hi = x.astype(BF16)
    r = x - hi.astype(F32)
    mid = r.astype(BF16)
    lo = (r - mid.astype(F32)).astype(BF16)
    return hi, mid, lo


def _sigmoid(x):
    return 1.0 / (1.0 + jnp.exp(-x))


def _silu(x):
    return x * _sigmoid(x)


def _log_sigmoid(x):
    return jnp.minimum(x, 0.0) - jnp.log1p(jnp.exp(-jnp.abs(x)))


def _gelu_tanh(x):
    return 0.5 * x * (1.0 + jnp.tanh(np.sqrt(2.0 / np.pi) * (x + 0.044715 * (x * x * x))))


def _adaln_kernel(c_ref, w_ref, b_ref, o_ref):
    a_hi, a_lo = _split2(_silu(c_ref[...]))
    w_hi, w_lo = _split2(w_ref[...])
    o_ref[...] = _dot(a_hi, w_hi) + _dot(a_hi, w_lo) + _dot(a_lo, w_hi) + b_ref[...]


def _adaln(cond, w_mod, b_mod):
    depth, d, n = w_mod.shape
    r = cond.shape[0]
    tn = 1536
    return pl.pallas_call(
        _adaln_kernel,
        grid=(depth, n // tn),
        in_specs=[pl.BlockSpec((r, d), lambda l, j: (0, 0)),
                  pl.BlockSpec((None, d, tn), lambda l, j: (l, 0, j)),
                  pl.BlockSpec((None, 1, tn), lambda l, j: (l, 0, j))],
        out_specs=pl.BlockSpec((None, r, tn), lambda l, j: (l, 0, j)),
        out_shape=jax.ShapeDtypeStruct((depth, r, n), F32),
        compiler_params=_cparams("parallel", "parallel"),
        name="adaln",
    )(cond, w_mod, b_mod.reshape(depth, 1, n))


def _mod_row_spec(k, tiles_per_batch, n_batch):
    d = None

    def idx(i):
        row = jnp.where(i % tiles_per_batch == 0, n_batch, i // tiles_per_batch)
        return (row, k, 0, 0)
    return idx


def _modulated(h_ref, g_ref, sh_ref, sc_ref):
    x = h_ref[...]
    y = x * lax.rsqrt(jnp.mean(x * x, axis=-1, keepdims=True) + EPS)
    return (y * g_ref[...]) * (1.0 + sc_ref[...]) + sh_ref[...]


def _modulate_kernel(h_ref, g_ref, sh_ref, sc_ref, u_ref):
    u_ref[...] = _modulated(h_ref, g_ref, sh_ref, sc_ref).astype(u_ref.dtype)


def _modulate_router_kernel(h_ref, g_ref, sh_ref, sc_ref, wr_ref, u_ref, gates_ref):
    u = _modulated(h_ref, g_ref, sh_ref, sc_ref)
    u_ref[...] = u.astype(u_ref.dtype)
    u0, u1, u2 = _split3(u)
    w0, w1, w2 = _split3(wr_ref[...])
    logits = (_dot(u0, w0) + _dot(u0, w1) + _dot(u1, w0)
              + _dot(u1, w1) + _dot(u0, w2) + _dot(u2, w0))
    lane = lax.broadcasted_iota(jnp.int32, logits.shape, 1).astype(F32)
    lg = jnp.where(lane < N_EXPERTS, logits, -jnp.inf)
    m1 = jnp.max(lg, axis=-1, keepdims=True)
    i1 = jnp.min(jnp.where(lg == m1, lane, float(LANES)), axis=-1, keepdims=True)
    lg2 = jnp.where(lane == i1, -jnp.inf, lg)
    m2 = jnp.max(lg2, axis=-1, keepdims=True)
    i2 = jnp.min(jnp.where(lg2 == m2, lane, float(LANES)), axis=-1, keepdims=True)
    e2 = jnp.exp(m2 - m1)
    den = 1.0 + e2
    gates_ref[...] = jnp.where(lane == i1, 1.0 / den, 0.0) + jnp.where(lane == i2, e2 / den, 0.0)


def _modulate(h, g, mods, k_shift, n_batch, tiles_per_batch, w_router=None):
    n, d = h.shape
    tm = TOKEN_TILE
    row = pl.BlockSpec((tm, d), lambda i: (i, 0))
    vec = pl.BlockSpec((1, d), lambda i: (0, 0))

    def mod_spec(k):
        return pl.BlockSpec((None, None, 1, d), _mod_row_spec(k, tiles_per_batch, n_batch))

    in_specs = [row, vec, mod_spec(k_shift), mod_spec(k_shift + 1)]
    if w_router is None:
        return pl.pallas_call(
            _modulate_kernel, grid=(n // tm,), in_specs=in_specs, out_specs=row,
            out_shape=jax.ShapeDtypeStruct((n, d), BF16),
            compiler_params=_cparams("parallel"), name="modulate",
        )(h, g, mods, mods)
    wr = jnp.zeros((d, LANES), F32).at[:, :N_EXPERTS].set(w_router)
    return pl.pallas_call(
        _modulate_router_kernel, grid=(n // tm,),
        in_specs=in_specs + [pl.BlockSpec((d, LANES), lambda i: (0, 0))],
        out_specs=[row, pl.BlockSpec((tm, LANES), lambda i: (i, 0))],
        out_shape=[jax.ShapeDtypeStruct((n, d), BF16), jax.ShapeDtypeStruct((n, LANES), F32)],
        compiler_params=_cparams("parallel"), name="modulate_router",
    )(h, g, mods, mods, wr)


def _mm_kernel(x_ref, w_ref, o_ref):
    o_ref[...] = _dot(x_ref[...], w_ref[...]).astype(o_ref.dtype)


def _matmul(x, w, out_dtype, tm, tn):
    m, k = x.shape
    n = w.shape[1]
    return pl.pallas_call(
        _mm_kernel, grid=(n // tn, m // tm),
        in_specs=[pl.BlockSpec((tm, k), lambda j, i: (i, 0)),
                  pl.BlockSpec((k, tn), lambda j, i: (0, j))],
        out_specs=pl.BlockSpec((tm, tn), lambda j, i: (i, j)),
        out_shape=jax.ShapeDtypeStruct((m, n), out_dtype),
        compiler_params=_cparams("parallel", "parallel"), name="matmul",
    )(x, w)


def _mm_res_kernel(x_ref, w_ref, h_ref, gt_ref, o_ref):
    o_ref[...] = h_ref[...] + gt_ref[...] * _dot(x_ref[...], w_ref[...])


def _matmul_residual(x, w, h, mods, k_gate, n_batch, tiles_per_batch):
    n, d = h.shape
    kdim = x.shape[1]
    tm = TOKEN_TILE
    return pl.pallas_call(
        _mm_res_kernel, grid=(n // tm,),
        in_specs=[pl.BlockSpec((tm, kdim), lambda i: (i, 0)),
                  pl.BlockSpec((kdim, d), lambda i: (0, 0)),
                  pl.BlockSpec((tm, d), lambda i: (i, 0)),
                  pl.BlockSpec((None, None, 1, d), _mod_row_spec(k_gate, tiles_per_batch, n_batch))],
        out_specs=pl.BlockSpec((tm, d), lambda i: (i, 0)),
        out_shape=jax.ShapeDtypeStruct((n, d), F32),
        compiler_params=_cparams("parallel"), name="matmul_residual",
    )(x, w, h, mods)


def _hg_tables():
    c = HG_CHUNK
    t = np.arange(c)[:, None]
    u = np.arange(c)[None, :]
    mats = [(u <= t)]
    for k in range(1, HG_LEVELS + 1):
        m = ((t >> k) << k) + (1 << (k - 1)) - 1
        upper = t > m
        mats.append(np.where(upper, (u > m) & (u <= t), (u > t) & (u <= m)))
    mats.append(u > t)
    fwd = np.concatenate(mats, axis=0).astype(np.float32)
    bwd = fwd.reshape(-1, c, c)[:, ::-1, ::-1].reshape(-1, c)
    x = t ^ u
    lvl = np.where(t > u, np.floor(np.log2(np.maximum(x, 1))).astype(np.int32) + 1, -1)
    lvl = np.where(t == u, 0, lvl).astype(np.int32)
    return np.stack([fwd, bwd]), np.stack([lvl, lvl.T])


def _hg_kernel(q_ref, ff_ref, fb_ref, v_ref, lb_ref, m_ref, lvl_ref, o_ref, *, n_ctx_chunks, n_chunks):
    c = HG_CHUNK
    for d in range(2):
        z_ref = ff_ref if d == 0 else fb_ref
        lb = lb_ref[d:d + 1, :]
        log_lb = jnp.log(jnp.maximum(lb, LB_MIN))
        log1m_lb = jnp.log1p(-lb)
        tot_row = c - 1 if d == 0 else 0

        def chunk(i, st, d=d, z_ref=z_ref, lb=lb, log_lb=log_lb, log1m_lb=log1m_lb, tot_row=tot_row):
            if d == 0:
                ci = i
            else:
                ci = jnp.where(i < n_ctx_chunks, n_ctx_chunks - 1 - i, n_chunks - 1 - (i - n_ctx_chunks))
            t0 = pl.multiple_of(ci * c, c)
            q = _silu(q_ref[pl.ds(t0, c), :])
            z = z_ref[pl.ds(t0, c), :]
            v = v_ref[pl.ds(t0, c), :].astype(BF16)
            k = (1.0 - lb) * _sigmoid(-z)
            a = log_lb
            b = log1m_lb + _log_sigmoid(z)
            logf = jnp.maximum(a, b) + jnp.log1p(jnp.exp(-jnp.abs(a - b)))
            g_hi, g_lo = _split2(logf)
            mm = m_ref[d]
            e = jnp.exp(_dot(mm, g_hi) + _dot(mm, g_lo))
            lvl = lvl_ref[d]
            att = jnp.where(lvl == 0, _dot_nt(q.astype(BF16), k.astype(BF16)), 0.0)
            for lev in range(1, HG_LEVELS + 1):
                el = e[lev * c:(lev + 1) * c]
                att = jnp.where(lvl == lev, _dot_nt((q * el).astype(BF16), (k * el).astype(BF16)), att)
            e0 = e[0:c]
            o = _dot(att.astype(BF16), v) + _dot_nt((q * e0).astype(BF16), st.astype(BF16))
            if d == 0:
                o_ref[pl.ds(t0, c), :] = o
            else:
                o_ref[pl.ds(t0, c), :] += o
            k_last = (k * e[(HG_LEVELS + 1) * c:]).astype(BF16)
            return st * e0[tot_row:tot_row + 1, :] + _dot_tn(v, k_last)

        lax.fori_loop(0, n_chunks, chunk, jnp.zeros((c, c), F32))


def _hgrn2(p3, lb, n_ctx):
    bsz, t, _ = p3.shape
    width = lb.shape[1]
    heads = width // HG_HEAD_DIM
    mats, lvl = _hg_tables()
    c = HG_CHUNK

    def col(group):
        return pl.BlockSpec((None, t, c), lambda b, h: (b, 0, group * heads + h))

    kern = functools.partial(_hg_kernel, n_ctx_chunks=n_ctx // c, n_chunks=t // c)
    return pl.pallas_call(
        kern, grid=(bsz, heads),
        in_specs=[col(0), col(1), col(2), col(3),
                  pl.BlockSpec((2, c), lambda b, h: (0, h)),
                  pl.BlockSpec(mats.shape, lambda b, h: (0, 0, 0)),
                  pl.BlockSpec(lvl.shape, lambda b, h: (0, 0, 0))],
        out_specs=pl.BlockSpec((None, t, c), lambda b, h: (b, 0, h)),
        out_shape=jax.ShapeDtypeStruct((bsz, t, width), F32),
        compiler_params=_cparams("parallel", "parallel"), name="hgrn2",
    )(p3, p3, p3, p3, lb, jnp.asarray(mats, BF16), jnp.asarray(lvl))


PAD = 8


def _rg_kernel(x_ref, cw_ref, cb_ref, wa_ref, ba_ref, wx_ref, bx_ref, lam_ref, o_ref,
               xpad_ref, xc_ref, *, n_ctx):
    t, w = x_ref.shape
    n_lat = t - n_ctx
    lat0 = n_ctx + 2 * PAD
    xpad_ref[...] = jnp.zeros(xpad_ref.shape, F32)
    xpad_ref[PAD:PAD + n_ctx, :] = x_ref[0:n_ctx, :]
    xpad_ref[lat0:lat0 + n_lat, :] = x_ref[n_ctx:t, :]
    left = RG_CONV // 2
    for base, dst, n in ((PAD, 0, n_ctx), (lat0, n_ctx, n_lat)):
        acc = cb_ref[...] + cw_ref[0:1, :] * xpad_ref[base - left:base - left + n, :]
        for j in range(1, RG_CONV):
            acc = acc + cw_ref[j:j + 1, :] * xpad_ref[base - left + j:base - left + j + n, :]
        xc_ref[dst:dst + n, :] = acc

    blk = RG_SCAN_BLOCK
    n_blocks = t // blk
    n_ctx_blocks = n_ctx // blk
    rows = lax.broadcasted_iota(jnp.int32, (blk, w), 0)
    for d in range(2):
        log_sig_lam = _log_sigmoid(lam_ref[d:d + 1, :])

        def step(i, hin, d=d, log_sig_lam=log_sig_lam):
            if d == 0:
                bi = i
            else:
                bi = jnp.where(i < n_ctx_blocks, n_ctx_blocks - 1 - i, n_blocks - 1 - (i - n_ctx_blocks))
            t0 = pl.multiple_of(bi * blk, blk)
            xc = xc_ref[pl.ds(t0, blk), :]
            xb = xc.astype(BF16)
            r = _sigmoid(_dot(xb, wa_ref[d]) + ba_ref[d:d + 1, :])
            ig = _sigmoid(_dot(xb, wx_ref[d]) + bx_ref[d:d + 1, :])
            log_a = RG_C * r * log_sig_lam
            a = jnp.exp(log_a)
            b = jnp.sqrt(jnp.maximum(-jnp.tanh(log_a) * (a * a + 1.0), 0.0)) * (ig * xc)
            s = 1
            while s < blk:
                if d == 0:
                    keep = rows >= s
                    a_sh = pltpu.roll(a, s, 0)
                    b_sh = pltpu.roll(b, s, 0)
                else:
                    keep = rows < blk - s
                    a_sh = pltpu.roll(a, blk - s, 0)
                    b_sh = pltpu.roll(b, blk - s, 0)
                b = jnp.where(keep, a * b_sh, 0.0) + b
                a = jnp.where(keep, a * a_sh, a)
                s *= 2
            h = a * hin + b
            if d == 0:
                o_ref[pl.ds(t0, blk), :] = h
                return h[blk - 1:blk, :]
            o_ref[pl.ds(t0, blk), :] += h
            return h[0:1, :]

        lax.fori_loop(0, n_blocks, step, jnp.zeros((1, w), F32))


def _blockdiag_dense(w):
    nd, nb, k, _ = w.shape
    eye = jnp.eye(nb, dtype=w.dtype)
    return jnp.einsum('dnij,nm->dnimj', w, eye).reshape(nd, nb * k, nb * k)


def _rglru(p3, col_block, conv_w, conv_b, wa, ba, wx, bx, lam, n_ctx):
    bsz, t, _ = p3.shape
    w = conv_w.shape[1]
    full = lambda shape: pl.BlockSpec(shape, lambda b: (0,) * len(shape))
    kern = functools.partial(_rg_kernel, n_ctx=n_ctx)
    return pl.pallas_call(
        kern, grid=(bsz,),
        in_specs=[pl.BlockSpec((None, t, w), lambda b: (b, 0, col_block)),
                  full((RG_CONV, w)), full((1, w)),
                  full((2, w, w)), full((2, w)), full((2, w, w)), full((2, w)), full((2, w))],
        out_specs=pl.BlockSpec((None, t, w), lambda b: (b, 0, 0)),
        out_shape=jax.ShapeDtypeStruct((bsz, t, w), F32),
        scratch_shapes=[pltpu.VMEM((t + 3 * PAD, w), F32), pltpu.VMEM((t, w), F32)],
        compiler_params=_cparams("parallel"), name="rglru",
    )(p3, conv_w, conv_b.reshape(1, w), _blockdiag_dense(wa).astype(BF16), ba,
      _blockdiag_dense(wx).astype(BF16), bx, lam)


def _merge_kernel(hg_ref, rg_ref, gate_ref, y_ref, og_ref, w_ref, h_ref, gt_ref, o_ref):
    hg = hg_ref[...]
    half = hg.shape[1]
    hn = hg * lax.rsqrt(jnp.mean(hg * hg, axis=-1, keepdims=True) + EPS) * og_ref[...]
    a = (hn * _silu(gate_ref[...])).astype(BF16)
    b = (rg_ref[...] * _gelu_tanh(y_ref[...])).astype(BF16)
    o = _dot(a, w_ref[0:half, :]) + _dot(b, w_ref[half:, :])
    o_ref[...] = h_ref[...] + gt_ref[...] * o


def _merge_out(hg, rg, p, onorm_g, w_out, h, mods, n_batch, tiles_per_batch):
    n, d = h.shape
    half = hg.shape[1]
    tm = TOKEN_TILE
    blk = lambda c: pl.BlockSpec((tm, half), lambda i: (i, c))
    return pl.pallas_call(
        _merge_kernel, grid=(n // tm,),
        in_specs=[blk(0), blk(0), blk(4), blk(6),
                  pl.BlockSpec((1, half), lambda i: (0, 0)),
                  pl.BlockSpec((d, d), lambda i: (0, 0)),
                  pl.BlockSpec((tm, d), lambda i: (i, 0)),
                  pl.BlockSpec((None, None, 1, d), _mod_row_spec(2, tiles_per_batch, n_batch))],
        out_specs=pl.BlockSpec((tm, d), lambda i: (i, 0)),
        out_shape=jax.ShapeDtypeStruct((n, d), F32),
        compiler_params=_cparams("parallel"), name="merge_out",
    )(hg, rg, p, p, onorm_g.reshape(1, half), w_out, h, mods)


def _na_bias_table(rpb):
    q = np.arange(GRID_W)[:, None]
    kc = np.arange(GRID_W)[None, :]
    wstart = np.clip(q - NA_KC // 2, 0, GRID_W - NA_KC)
    mask = (kc >= wstart) & (kc < wstart + NA_KC)
    coff = np.clip(kc - q + NA_KC - 1, 0, 2 * NA_KC - 2)
    drow = np.arange(NA_KR)[:, None] + np.arange(NA_KR)[None, :]
    tab = rpb.astype(F32)[:, drow][:, :, :, coff]
    tab = jnp.where(mask[None, None, None], tab, NEG_BIG)
    tab = tab.transpose(0, 1, 3, 2, 4)
    return tab.reshape(tab.shape[0], NA_KR, GRID_W, NA_KR * GRID_W)


def _na_kernel(q_ref, k_ref, v_ref, qg_ref, kg_ref, bias_ref, o_ref, qn_ref, kn_ref, vb_ref,
               *, n_ctx, scale):
    t, w = q_ref.shape
    hd = w // 2
    rows_grid = (t - n_ctx) // GRID_W
    r_i = lax.broadcasted_iota(jnp.int32, (w, w), 0)
    c_i = lax.broadcasted_iota(jnp.int32, (w, w), 1)
    seg = jnp.where((r_i // hd) == (c_i // hd), 1.0 / hd, 0.0).astype(BF16)

    def normed(x, g):
        s_hi, s_lo = _split2(x * x)
        ms = _dot(s_hi, seg) + _dot(s_lo, seg)
        return x * lax.rsqrt(ms + EPS) * g

    qn_ref[...] = (normed(q_ref[...], qg_ref[...]) * scale).astype(BF16)
    kn_ref[...] = normed(k_ref[...], kg_ref[...]).astype(BF16)
    vb_ref[...] = v_ref[...].astype(BF16)

    lane_q = lax.broadcasted_iota(jnp.int32, (GRID_W, w), 1)
    kc = kn_ref[0:n_ctx, :]
    vc = vb_ref[0:n_ctx, :]

    qc = qn_ref[0:n_ctx, :]
    lane_c = lax.broadcasted_iota(jnp.int32, (n_ctx, w), 1)
    outs = []
    for hh in range(2):
        qh = jnp.where((lane_c // hd) == hh, qc, jnp.zeros_like(qc))
        s = _dot_nt(qh, kc)
        p = jnp.exp(s - jnp.max(s, axis=-1, keepdims=True))
        outs.append(_dot(p.astype(BF16), vc) / jnp.sum(p, axis=-1, keepdims=True))
    o_ref[0:n_ctx, :] = jnp.where(lane_c < hd, outs[0], outs[1]).astype(o_ref.dtype)

    kr = NA_KR
    n_win = kr * GRID_W

    def row(r, carry):
        rs = jnp.clip(r - kr // 2, 0, rows_grid - kr)
        off = rs - r + (NA_KR - 1) - (kr - 1) + (kr - 1)
        q0 = pl.multiple_of(n_ctx + r * GRID_W, GRID_W)
        k0 = pl.multiple_of(n_ctx + rs * GRID_W, GRID_W)
        q_r = qn_ref[pl.ds(q0, GRID_W), :]
        kw = kn_ref[pl.ds(k0, n_win), :]
        vw = vb_ref[pl.ds(k0, n_win), :]
        res = []
        for hh in range(2):
            qh = jnp.where((lane_q // hd) == hh, q_r, jnp.zeros_like(q_r))
            s_loc = _dot_nt(qh, kw) + bias_ref[hh, off]
            s_ctx = _dot_nt(qh, kc)
            m = jnp.maximum(jnp.max(s_loc, axis=-1, keepdims=True), jnp.max(s_ctx, axis=-1, keepdims=True))
            p_loc = jnp.exp(s_loc - m)
            p_ctx = jnp.exp(s_ctx - m)
            den = jnp.sum(p_loc, axis=-1, keepdims=True) + jnp.sum(p_ctx, axis=-1, keepdims=True)
            res.append((_dot(p_loc.astype(BF16), vw) + _dot(p_ctx.astype(BF16), vc)) / den)
        o_ref[pl.ds(q0, GRID_W), :] = jnp.where(lane_q < hd, res[0], res[1]).astype(o_ref.dtype)
        return carry

    lax.fori_loop(0, rows_grid, row, 0)


def _na_attention(qkv3, q_g, k_g, rpb, n_ctx):
    bsz, t, d3 = qkv3.shape
    d = d3 // 3
    hd = d // NA_HEADS
    w = 2 * hd
    pairs = d // w
    assert w == LANES and (t - n_ctx) // GRID_W >= NA_KR
    bias = _na_bias_table(rpb)
    kern = functools.partial(_na_kernel, n_ctx=n_ctx, scale=hd ** -0.5)

    def col(group):
        return pl.BlockSpec((None, t, w), lambda hp, b: (b, 0, group * pairs + hp))

    gain = lambda g: jnp.tile(g.astype(F32), 2).reshape(1, w)
    return pl.pallas_call(
        kern, grid=(pairs, bsz),
        in_specs=[col(0), col(1), col(2),
                  pl.BlockSpec((1, w), lambda hp, b: (0, 0)),
                  pl.BlockSpec((1, w), lambda hp, b: (0, 0)),
                  pl.BlockSpec((2,) + bias.shape[1:], lambda hp, b: (hp, 0, 0, 0))],
        out_specs=pl.BlockSpec((None, t, w), lambda hp, b: (b, 0, hp)),
        out_shape=jax.ShapeDtypeStruct((bsz, t, d), BF16),
        scratch_shapes=[pltpu.VMEM((t, w), BF16)] * 3,
        compiler_params=_cparams("parallel", "parallel"), name="na_attention",
    )(qkv3, qkv3, qkv3, gain(q_g), gain(k_g), bias)


FF_TILE = 256
FFN_ROWS = 768


def _row_gate(gt_ctx_ref, gt_ref, shape, tile_in_batch, n_ctx):
    rows = lax.broadcasted_iota(jnp.int32, shape, 0)
    is_ctx = (rows < n_ctx) & (tile_in_batch == 0)
    return jnp.where(is_ctx, gt_ctx_ref[...], gt_ref[...])


def _swiglu_kernel(x_ref, w1_ref, w3_ref, w2_ref, h_ref, gtc_ref, gt_ref, o_ref, acc_ref,
                   *, n_ctx, tiles_per_batch):
    f = pl.program_id(1)

    @pl.when(f == 0)
    def _():
        acc_ref[...] = jnp.zeros_like(acc_ref)

    x = x_ref[...]
    hid = _silu(_dot(x, w1_ref[...])) * _dot(x, w3_ref[...])
    acc_ref[...] += _dot(hid.astype(BF16), w2_ref[...])

    @pl.when(f == pl.num_programs(1) - 1)
    def _():
        gate = _row_gate(gtc_ref, gt_ref, acc_ref.shape, pl.program_id(0) % tiles_per_batch, n_ctx)
        o_ref[...] = h_ref[...] + gate * acc_ref[...]


def _gate_specs(d, n_batch, tiles_per_batch, k_gate, nargs):
    if nargs == 2:
        ctx = lambda i, f: (n_batch, k_gate, 0, 0)
        own = lambda i, f: (i // tiles_per_batch, k_gate, 0, 0)
    else:
        ctx = lambda i, e, f: (n_batch, k_gate, 0, 0)
        own = lambda i, e, f: (i // tiles_per_batch, k_gate, 0, 0)
    return [pl.BlockSpec((None, None, 1, d), ctx), pl.BlockSpec((None, None, 1, d), own)]


def _swiglu_residual(x, w1, w3, w2, h, mods, n_batch, t, n_ctx):
    n, d = h.shape
    dff = w1.shape[1]
    tm, tf = FFN_ROWS, FF_TILE
    tpb = t // tm
    kern = functools.partial(_swiglu_kernel, n_ctx=n_ctx, tiles_per_batch=tpb)
    return pl.pallas_call(
        kern, grid=(n // tm, dff // tf),
        in_specs=[pl.BlockSpec((tm, d), lambda i, f: (i, 0)),
                  pl.BlockSpec((d, tf), lambda i, f: (0, f)),
                  pl.BlockSpec((d, tf), lambda i, f: (0, f)),
                  pl.BlockSpec((tf, d), lambda i, f: (f, 0)),
                  pl.BlockSpec((tm, d), lambda i, f: (i, 0))] + _gate_specs(d, n_batch, tpb, 5, 2),
        out_specs=pl.BlockSpec((tm, d), lambda i, f: (i, 0)),
        out_shape=jax.ShapeDtypeStruct((n, d), F32),
        scratch_shapes=[pltpu.VMEM((tm, d), F32)],
        compiler_params=_cparams("parallel", "arbitrary"), name="swiglu",
    )(x, w1, w3, w2, h, mods, mods)


def _moe_dense_kernel(x_ref, g_ref, w1_ref, w3_ref, w2_ref, h_ref, gtc_ref, gt_ref, o_ref, acc_ref,
                      *, n_ctx, tiles_per_batch):
    e = pl.program_id(1)
    f = pl.program_id(2)

    @pl.when((e == 0) & (f == 0))
    def _():
        acc_ref[...] = jnp.zeros_like(acc_ref)

    x = x_ref[...]
    gates = g_ref[...]
    lane = lax.broadcasted_iota(jnp.int32, gates.shape, 1)
    ge = jnp.sum(jnp.where(lane == e, gates, 0.0), axis=-1, keepdims=True)
    hid = _silu(_dot(x, w1_ref[...])) * _dot(x, w3_ref[...])
    acc_ref[...] += ge * _dot(hid.astype(BF16), w2_ref[...])

    @pl.when((e == pl.num_programs(1) - 1) & (f == pl.num_programs(2) - 1))
    def _():
        gate = _row_gate(gtc_ref, gt_ref, acc_ref.shape, pl.program_id(0) % tiles_per_batch, n_ctx)
        o_ref[...] = h_ref[...] + gate * acc_ref[...]


def _moe_residual(x, gates, w1, w3, w2, h, mods, n_batch, t, n_ctx):
    n, d = h.shape
    n_exp, _, dff = w1.shape
    tm, tf = FFN_ROWS, FF_TILE
    tpb = t // tm
    kern = functools.partial(_moe_dense_kernel, n_ctx=n_ctx, tiles_per_batch=tpb)
    return pl.pallas_call(
        kern, grid=(n // tm, n_exp, dff // tf),
        in_specs=[pl.BlockSpec((tm, d), lambda i, e, f: (i, 0)),
                  pl.BlockSpec((tm, LANES), lambda i, e, f: (i, 0)),
                  pl.BlockSpec((None, d, tf), lambda i, e, f: (e, 0, f)),
                  pl.BlockSpec((None, d, tf), lambda i, e, f: (e, 0, f)),
                  pl.BlockSpec((None, tf, d), lambda i, e, f: (e, f, 0)),
                  pl.BlockSpec((tm, d), lambda i, e, f: (i, 0))] + _gate_specs(d, n_batch, tpb, 5, 3),
        out_specs=pl.BlockSpec((tm, d), lambda i, e, f: (i, 0)),
        out_shape=jax.ShapeDtypeStruct((n, d), F32),
        scratch_shapes=[pltpu.VMEM((tm, d), F32)],
        compiler_params=_cparams("parallel", "arbitrary", "arbitrary"), name="moe",
    )(x, gates, w1, w3, w2, h, mods, mods)


def kernel(x, c, ctx, c_ctx, w_mod, b_mod, norm_mix_g, norm_ffn_g, ev_w_in, ev_w_out, hg_lb_logits, hg_onorm_g, rg_conv_w, rg_conv_b, rg_wa, rg_ba, rg_wx, rg_bx, rg_lambda, na_w_qkv, na_w_o, na_q_g, na_k_g, na_rpb, ffn_w1, ffn_w3, ffn_w2, moe_router, moe_w1, moe_w3, moe_w2):
    bsz, seq, d = x.shape
    n_ctx = ctx.shape[1]
    t = n_ctx + seq
    depth = w_mod.shape[0]
    tpb = t // TOKEN_TILE
    assert n_ctx % TOKEN_TILE == 0 and seq % TOKEN_TILE == 0 and t % FFN_ROWS == 0

    rows = -(-(bsz + 1) // 8) * 8
    cond = jnp.zeros((rows, d), F32).at[:bsz].set(c).at[bsz].set(c_ctx)
    mods_all = _adaln(cond, w_mod, b_mod).reshape(depth, rows, N_MOD, 1, d)

    lb_p = jax.nn.softmax(hg_lb_logits.astype(F32), axis=0)
    lb_all = jnp.cumsum(lb_p, axis=0) - lb_p[0]

    h = jnp.concatenate([ctx, x], axis=1).reshape(bsz * t, d)
    for layer in range(depth):
        j = layer // 2
        mods = mods_all[layer]
        u = _modulate(h, norm_mix_g[layer].reshape(1, d), mods, 0, bsz, tpb)
        if layer % 2 == 0:
            p = _matmul(u, ev_w_in[j].astype(BF16), F32, 768, 512)
            p3 = p.reshape(bsz, t, -1)
            hg = _hgrn2(p3, lb_all[j], n_ctx)
            half = hg.shape[-1]
            rg = _rglru(p3, 5 * half // (d - half), rg_conv_w[j], rg_conv_b[j], rg_wa[j], rg_ba[j],
                        rg_wx[j], rg_bx[j], rg_lambda[j], n_ctx)
            h = _merge_out(hg.reshape(bsz * t, half), rg.reshape(bsz * t, d - half), p, hg_onorm_g[j],
                           ev_w_out[j].astype(BF16), h, mods, bsz, tpb)
        else:
            qkv = _matmul(u, na_w_qkv[j].astype(BF16), F32, 768, 512)
            o = _na_attention(qkv.reshape(bsz, t, 3 * d), na_q_g[j], na_k_g[j], na_rpb[j], n_ctx)
            h = _matmul_residual(o.reshape(bsz * t, d), na_w_o[j].astype(BF16), h, mods, 2, bsz, tpb)
        if layer % 2 == 0:
            u = _modulate(h, norm_ffn_g[layer].reshape(1, d), mods, 3, bsz, tpb)
            h = _swiglu_residual(u, ffn_w1[j].astype(BF16), ffn_w3[j].astype(BF16),
                                 ffn_w2[j].astype(BF16), h, mods, bsz, t, n_ctx)
        else:
            u, gates = _modulate(h, norm_ffn_g[layer].reshape(1, d), mods, 3, bsz, tpb, moe_router[j])
            h = _moe_residual(u, gates, moe_w1[j].astype(BF16), moe_w3[j].astype(BF16),
                              moe_w2[j].astype(BF16), h, mods, bsz, t, n_ctx)
    return h.reshape(bsz, t, d)[:, n_ctx:]
```

```python
import functools

import numpy as np
import jax
import jax.numpy as jnp
from jax import lax
from jax.experimental import pallas as pl
from jax.experimental.pallas import tpu as pltpu
from jax.experimental.pallas import tpu_sc as plsc

F32 = jnp.float32
BF16 = jnp.bfloat16

EPS = 1e-6
NEG_BIG = -1e9
LB_MIN = 1e-30
N_MOD = 6
GRID_W = 64
HG_HEAD_DIM = 128
RG_BLOCKS = 8
RG_CONV = 4
RG_C = 8.0
NA_HEADS = 16
NA_KR = 8
NA_KC = 16
N_EXPERTS = 8

LANES = 128
TOKEN_TILE = 256
HG_CHUNK = 128
HG_LEVELS = 7
RG_SCAN_BLOCK = 256
VMEM_LIMIT = 56 * 1024 * 1024


def _cparams(*sem):
    return pltpu.CompilerParams(dimension_semantics=sem, vmem_limit_bytes=VMEM_LIMIT)


def _dot(a, b):
    return jnp.dot(a, b, preferred_element_type=F32)


def _dot_nt(a, b):
    return lax.dot_general(a, b, (((1,), (1,)), ((), ())), preferred_element_type=F32)


def _dot_tn(a, b):
    return lax.dot_general(a, b, (((0,), (0,)), ((), ())), preferred_element_type=F32)


def _split2(x):
    hi = x.astype(BF16)
    lo = (x - hi.astype(F32)).astype(BF16)
    return hi, lo


def _split3(x):
    hi = x.astype(BF16)
    r = x - hi.astype(F32)
    mid = r.astype(BF16)
    lo = (r - mid.astype(F32)).astype(BF16)
    return hi, mid, lo


def _sigmoid(x):
    return 1.0 / (1.0 + jnp.exp(-x))


def _silu(x):
    return x * _sigmoid(x)


def _log_sigmoid(x):
    return jnp.minimum(x, 0.0) - jnp.log1p(jnp.exp(-jnp.abs(x)))


def _gelu_tanh(x):
    return 0.5 * x * (1.0 + jnp.tanh(np.sqrt(2.0 / np.pi) * (x + 0.044715 * (x * x * x))))


def _adaln_kernel(c_ref, w_ref, b_ref, o_ref):
    a_hi, a_lo = _split2(_silu(c_ref[...]))
    w_hi, w_lo = _split2(w_ref[...])
    o_ref[...] = _dot(a_hi, w_hi) + _dot(a_hi, w_lo) + _dot(a_lo, w_hi) + b_ref[...]


def _adaln(cond, w_mod, b_mod):
    depth, d, n = w_mod.shape
    r = cond.shape[0]
    tn = 1536
    return pl.pallas_call(
        _adaln_kernel,
        grid=(depth, n // tn),
        in_specs=[pl.BlockSpec((r, d), lambda l, j: (0, 0)),
                  pl.BlockSpec((None, d, tn), lambda l, j: (l, 0, j)),
                  pl.BlockSpec((None, 1, tn), lambda l, j: (l, 0, j))],
        out_specs=pl.BlockSpec((None, r, tn), lambda l, j: (l, 0, j)),
        out_shape=jax.ShapeDtypeStruct((depth, r, n), F32),
        compiler_params=_cparams("parallel", "parallel"),
        name="adaln",
    )(cond, w_mod, b_mod.reshape(depth, 1, n))


def _mod_row_spec(k, tiles_per_batch, n_batch):
    def idx(i):
        row = jnp.where(i % tiles_per_batch == 0, n_batch, i // tiles_per_batch)
        return (row, k, 0, 0)
    return idx


def _modulated(h_ref, g_ref, sh_ref, sc_ref):
    x = h_ref[...]
    y = x * lax.rsqrt(jnp.mean(x * x, axis=-1, keepdims=True) + EPS)
    return (y * g_ref[...]) * (1.0 + sc_ref[...]) + sh_ref[...]


def _modulate_kernel(h_ref, g_ref, sh_ref, sc_ref, u_ref):
    u_ref[...] = _modulated(h_ref, g_ref, sh_ref, sc_ref).astype(u_ref.dtype)


def _pack_halves(x):
    m = x.shape[1] // 2
    lo = lax.bitcast_convert_type(x[:, :m].astype(BF16).astype(F32), jnp.int32)
    hi = lax.bitcast_convert_type(x[:, m:].astype(BF16).astype(F32), jnp.int32)
    return lax.shift_right_logical(lo, 16) | (hi & jnp.int32(-65536))


def _unpack_halves(w):
    lo = lax.bitcast_convert_type(lax.shift_left(w, 16), F32)
    hi = lax.bitcast_convert_type(w & jnp.int32(-65536), F32)
    return lo, hi


def _modulate_router_kernel(h_ref, g_ref, sh_ref, sc_ref, wr_ref, u_ref, route_ref):
    u = _modulated(h_ref, g_ref, sh_ref, sc_ref)
    u_ref[...] = _pack_halves(u)
    u0, u1, u2 = _split3(u)
    w0, w1, w2 = _split3(wr_ref[...])
    logits = (_dot(u0, w0) + _dot(u0, w1) + _dot(u1, w0)
              + _dot(u1, w1) + _dot(u0, w2) + _dot(u2, w0))
    lane = lax.broadcasted_iota(jnp.int32, logits.shape, 1).astype(F32)
    lg = jnp.where(lane < N_EXPERTS, logits, -jnp.inf)
    m1 = jnp.max(lg, axis=-1, keepdims=True)
    i1 = jnp.min(jnp.where(lg == m1, lane, float(LANES)), axis=-1, keepdims=True)
    lg2 = jnp.where(lane == i1, -jnp.inf, lg)
    m2 = jnp.max(lg2, axis=-1, keepdims=True)
    i2 = jnp.min(jnp.where(lg2 == m2, lane, float(LANES)), axis=-1, keepdims=True)
    e2 = jnp.exp(m2 - m1)
    den = 1.0 + e2
    route_ref[...] = jnp.where(lane == 0, i1, jnp.where(lane == 1, i2, jnp.where(lane == 2, 1.0 / den, e2 / den)))


def _modulate(h, g, mods, k_shift, n_batch, tiles_per_batch, w_router=None):
    n, d = h.shape
    tm = TOKEN_TILE
    row = pl.BlockSpec((tm, d), lambda i: (i, 0))
    vec = pl.BlockSpec((1, d), lambda i: (0, 0))

    def mod_spec(k):
        return pl.BlockSpec((None, None, 1, d), _mod_row_spec(k, tiles_per_batch, n_batch))

    in_specs = [row, vec, mod_spec(k_shift), mod_spec(k_shift + 1)]
    if w_router is None:
        return pl.pallas_call(
            _modulate_kernel, grid=(n // tm,), in_specs=in_specs, out_specs=row,
            out_shape=jax.ShapeDtypeStruct((n, d), BF16),
            compiler_params=_cparams("parallel"), name="modulate",
        )(h, g, mods, mods)
    wr = jnp.zeros((d, LANES), F32).at[:, :N_EXPERTS].set(w_router)
    return pl.pallas_call(
        _modulate_router_kernel, grid=(n // tm,),
        in_specs=in_specs + [pl.BlockSpec((d, LANES), lambda i: (0, 0))],
        out_specs=[pl.BlockSpec((tm, d // 2), lambda i: (i, 0)), pl.BlockSpec((tm, LANES), lambda i: (i, 0))],
        out_shape=[jax.ShapeDtypeStruct((n, d // 2), jnp.int32), jax.ShapeDtypeStruct((n, LANES), F32)],
        compiler_params=_cparams("parallel"), name="modulate_router",
    )(h, g, mods, mods, wr)


def _mm_kernel(x_ref, w_ref, o_ref):
    o_ref[...] = _dot(x_ref[...], w_ref[...]).astype(o_ref.dtype)


def _matmul(x, w, out_dtype, tm, tn):
    m, k = x.shape
    n = w.shape[1]
    return pl.pallas_call(
        _mm_kernel, grid=(n // tn, m // tm),
        in_specs=[pl.BlockSpec((tm, k), lambda j, i: (i, 0)),
                  pl.BlockSpec((k, tn), lambda j, i: (0, j))],
        out_specs=pl.BlockSpec((tm, tn), lambda j, i: (i, j)),
        out_shape=jax.ShapeDtypeStruct((m, n), out_dtype),
        compiler_params=_cparams("parallel", "parallel"), name="matmul",
    )(x, w)


def _mm_res_kernel(x_ref, w_ref, h_ref, gt_ref, o_ref):
    o_ref[...] = h_ref[...] + gt_ref[...] * _dot(x_ref[...], w_ref[...])


def _matmul_residual(x, w, h, mods, k_gate, n_batch, tiles_per_batch):
    n, d = h.shape
    kdim = x.shape[1]
    tm = TOKEN_TILE
    return pl.pallas_call(
        _mm_res_kernel, grid=(n // tm,),
        in_specs=[pl.BlockSpec((tm, kdim), lambda i: (i, 0)),
                  pl.BlockSpec((kdim, d), lambda i: (0, 0)),
                  pl.BlockSpec((tm, d), lambda i: (i, 0)),
                  pl.BlockSpec((None, None, 1, d), _mod_row_spec(k_gate, tiles_per_batch, n_batch))],
        out_specs=pl.BlockSpec((tm, d), lambda i: (i, 0)),
        out_shape=jax.ShapeDtypeStruct((n, d), F32),
        compiler_params=_cparams("parallel"), name="matmul_residual",
    )(x, w, h, mods)


def _hg_tables():
    c = HG_CHUNK
    t = np.arange(c)[:, None]
    u = np.arange(c)[None, :]
    mats = [(u <= t)]
    for k in range(1, HG_LEVELS + 1):
        m = ((t >> k) << k) + (1 << (k - 1)) - 1
        upper = t > m
        mats.append(np.where(upper, (u > m) & (u <= t), (u > t) & (u <= m)))
    mats.append(u > t)
    fwd = np.concatenate(mats, axis=0).astype(np.float32)
    bwd = fwd.reshape(-1, c, c)[:, ::-1, ::-1].reshape(-1, c)
    x = t ^ u
    lvl = np.where(t > u, np.floor(np.log2(np.maximum(x, 1))).astype(np.int32) + 1, -1)
    lvl = np.where(t == u, 0, lvl).astype(np.int32)
    return np.stack([fwd, bwd]), np.stack([lvl, lvl.T])


def _hg_chunk(d, t0, st, q_ref, z_ref, v_ref, lb_ref, m_ref, lvl_ref):
    c = HG_CHUNK
    lb = lb_ref[d:d + 1, :]
    log_lb = jnp.log(jnp.maximum(lb, LB_MIN))
    q = _silu(q_ref[pl.ds(t0, c), :])
    z = z_ref[pl.ds(t0, c), :]
    v = v_ref[pl.ds(t0, c), :].astype(BF16)
    k = (1.0 - lb) * _sigmoid(-z)
    b = jnp.log1p(-lb) + _log_sigmoid(z)
    logf = jnp.maximum(log_lb, b) + jnp.log1p(jnp.exp(-jnp.abs(log_lb - b)))
    g_hi, g_lo = _split2(logf)
    mm = m_ref[d]
    e = jnp.exp(_dot(mm, g_hi) + _dot(mm, g_lo))
    lvl = lvl_ref[d]
    att = jnp.where(lvl == 0, _dot_nt(q.astype(BF16), k.astype(BF16)), 0.0)
    for lev in range(1, HG_LEVELS + 1):
        el = e[lev * c:(lev + 1) * c]
        att = jnp.where(lvl == lev, _dot_nt((q * el).astype(BF16), (k * el).astype(BF16)), att)
    e0 = e[0:c]
    o = _dot(att.astype(BF16), v) + _dot_nt((q * e0).astype(BF16), st.astype(BF16))
    k_last = (k * e[(HG_LEVELS + 1) * c:]).astype(BF16)
    tot_row = c - 1 if d == 0 else 0
    return o, st * e0[tot_row:tot_row + 1, :] + _dot_tn(v, k_last)


def _hg_kernel(q_ref, ff_ref, fb_ref, v_ref, lb_ref, m_ref, lvl_ref, o_ref, ob_ref, *, n_ctx_chunks, n_chunks):
    c = HG_CHUNK

    def step(i, carry):
        st_f, st_b = carry
        cb = jnp.where(i < n_ctx_chunks, n_ctx_chunks - 1 - i, n_chunks - 1 - (i - n_ctx_chunks))
        tf = pl.multiple_of(i * c, c)
        tb = pl.multiple_of(cb * c, c)
        o_f, st_f = _hg_chunk(0, tf, st_f, q_ref, ff_ref, v_ref, lb_ref, m_ref, lvl_ref)
        o_b, st_b = _hg_chunk(1, tb, st_b, q_ref, fb_ref, v_ref, lb_ref, m_ref, lvl_ref)
        o_ref[pl.ds(tf, c), :] = o_f
        ob_ref[pl.ds(tb, c), :] = o_b
        return st_f, st_b

    zero = jnp.zeros((c, c), F32)
    lax.fori_loop(0, n_chunks, step, (zero, zero))
    o_ref[...] += ob_ref[...]


def _hgrn2(p3, lb, n_ctx):
    bsz, t, _ = p3.shape
    width = lb.shape[1]
    heads = width // HG_HEAD_DIM
    mats, lvl = _hg_tables()
    c = HG_CHUNK

    def col(group):
        return pl.BlockSpec((None, t, c), lambda b, h: (b, 0, group * heads + h))

    kern = functools.partial(_hg_kernel, n_ctx_chunks=n_ctx // c, n_chunks=t // c)
    return pl.pallas_call(
        kern, grid=(bsz, heads),
        in_specs=[col(0), col(1), col(2), col(3),
                  pl.BlockSpec((2, c), lambda b, h: (0, h)),
                  pl.BlockSpec(mats.shape, lambda b, h: (0, 0, 0)),
                  pl.BlockSpec(lvl.shape, lambda b, h: (0, 0, 0))],
        out_specs=pl.BlockSpec((None, t, c), lambda b, h: (b, 0, h)),
        out_shape=jax.ShapeDtypeStruct((bsz, t, width), F32),
        scratch_shapes=[pltpu.VMEM((t, c), F32)],
        compiler_params=_cparams("parallel", "parallel"), name="hgrn2",
    )(p3, p3, p3, p3, lb, jnp.asarray(mats, BF16), jnp.asarray(lvl))


PAD = 8


def _rg_kernel(x_ref, cw_ref, cb_ref, wa_ref, ba_ref, wx_ref, bx_ref, lam_ref, o_ref,
               xpad_ref, xc_ref, *, n_ctx):
    t, w = x_ref.shape
    n_lat = t - n_ctx
    lat0 = n_ctx + 2 * PAD
    xpad_ref[...] = jnp.zeros(xpad_ref.shape, F32)
    xpad_ref[PAD:PAD + n_ctx, :] = x_ref[0:n_ctx, :]
    xpad_ref[lat0:lat0 + n_lat, :] = x_ref[n_ctx:t, :]
    left = RG_CONV // 2
    for base, dst, n in ((PAD, 0, n_ctx), (lat0, n_ctx, n_lat)):
        acc = cb_ref[...] + cw_ref[0:1, :] * xpad_ref[base - left:base - left + n, :]
        for j in range(1, RG_CONV):
            acc = acc + cw_ref[j:j + 1, :] * xpad_ref[base - left + j:base - left + j + n, :]
        xc_ref[dst:dst + n, :] = acc

    blk = RG_SCAN_BLOCK
    n_blocks = t // blk
    n_ctx_blocks = n_ctx // blk
    rows = lax.broadcasted_iota(jnp.int32, (blk, w), 0)
    for d in range(2):
        log_sig_lam = _log_sigmoid(lam_ref[d:d + 1, :])

        def step(i, hin, d=d, log_sig_lam=log_sig_lam):
            if d == 0:
                bi = i
            else:
                bi = jnp.where(i < n_ctx_blocks, n_ctx_blocks - 1 - i, n_blocks - 1 - (i - n_ctx_blocks))
            t0 = pl.multiple_of(bi * blk, blk)
            xc = xc_ref[pl.ds(t0, blk), :]
            xb = xc.astype(BF16)
            r = _sigmoid(_dot(xb, wa_ref[d]) + ba_ref[d:d + 1, :])
            ig = _sigmoid(_dot(xb, wx_ref[d]) + bx_ref[d:d + 1, :])
            log_a = RG_C * r * log_sig_lam
            a = jnp.exp(log_a)
            b = jnp.sqrt(jnp.maximum(-jnp.tanh(log_a) * (a * a + 1.0), 0.0)) * (ig * xc)
            s = 1
            while s < blk:
                if d == 0:
                    keep = rows >= s
                    a_sh = pltpu.roll(a, s, 0)
                    b_sh = pltpu.roll(b, s, 0)
                else:
                    keep = rows < blk - s
                    a_sh = pltpu.roll(a, blk - s, 0)
                    b_sh = pltpu.roll(b, blk - s, 0)
                b = jnp.where(keep, a * b_sh, 0.0) + b
                a = jnp.where(keep, a * a_sh, a)
                s *= 2
            h = a * hin + b
            if d == 0:
                o_ref[pl.ds(t0, blk), :] = h
                return h[blk - 1:blk, :]
            o_ref[pl.ds(t0, blk), :] += h
            return h[0:1, :]

        lax.fori_loop(0, n_blocks, step, jnp.zeros((1, w), F32))


def _blockdiag_dense(w):
    nd, nb, k, _ = w.shape
    eye = jnp.eye(nb, dtype=w.dtype)
    return jnp.einsum('dnij,nm->dnimj', w, eye).reshape(nd, nb * k, nb * k)


def _rglru(p3, col_block, conv_w, conv_b, wa, ba, wx, bx, lam, n_ctx):
    bsz, t, _ = p3.shape
    w = conv_w.shape[1]
    full = lambda shape: pl.BlockSpec(shape, lambda b: (0,) * len(shape))
    kern = functools.partial(_rg_kernel, n_ctx=n_ctx)
    return pl.pallas_call(
        kern, grid=(bsz,),
        in_specs=[pl.BlockSpec((None, t, w), lambda b: (b, 0, col_block)),
                  full((RG_CONV, w)), full((1, w)),
                  full((2, w, w)), full((2, w)), full((2, w, w)), full((2, w)), full((2, w))],
        out_specs=pl.BlockSpec((None, t, w), lambda b: (b, 0, 0)),
        out_shape=jax.ShapeDtypeStruct((bsz, t, w), F32),
        scratch_shapes=[pltpu.VMEM((t + 3 * PAD, w), F32), pltpu.VMEM((t, w), F32)],
        compiler_params=_cparams("parallel"), name="rglru",
    )(p3, conv_w, conv_b.reshape(1, w), _blockdiag_dense(wa).astype(BF16), ba,
      _blockdiag_dense(wx).astype(BF16), bx, lam)


def _merge_kernel(hg_ref, rg_ref, gate_ref, y_ref, og_ref, w_ref, h_ref, gt_ref, o_ref):
    hg = hg_ref[...]
    half = hg.shape[1]
    hn = hg * lax.rsqrt(jnp.mean(hg * hg, axis=-1, keepdims=True) + EPS) * og_ref[...]
    a = (hn * _silu(gate_ref[...])).astype(BF16)
    b = (rg_ref[...] * _gelu_tanh(y_ref[...])).astype(BF16)
    o = _dot(a, w_ref[0:half, :]) + _dot(b, w_ref[half:, :])
    o_ref[...] = h_ref[...] + gt_ref[...] * o


def _merge_out(hg, rg, p, onorm_g, w_out, h, mods, n_batch, tiles_per_batch):
    n, d = h.shape
    half = hg.shape[1]
    tm = TOKEN_TILE
    blk = lambda c: pl.BlockSpec((tm, half), lambda i: (i, c))
    return pl.pallas_call(
        _merge_kernel, grid=(n // tm,),
        in_specs=[blk(0), blk(0), blk(4), blk(6),
                  pl.BlockSpec((1, half), lambda i: (0, 0)),
                  pl.BlockSpec((d, d), lambda i: (0, 0)),
                  pl.BlockSpec((tm, d), lambda i: (i, 0)),
                  pl.BlockSpec((None, None, 1, d), _mod_row_spec(2, tiles_per_batch, n_batch))],
        out_specs=pl.BlockSpec((tm, d), lambda i: (i, 0)),
        out_shape=jax.ShapeDtypeStruct((n, d), F32),
        compiler_params=_cparams("parallel"), name="merge_out",
    )(hg, rg, p, p, onorm_g.reshape(1, half), w_out, h, mods)


def _na_bias_table(rpb):
    q = np.arange(GRID_W)[:, None]
    kc = np.arange(GRID_W)[None, :]
    wstart = np.clip(q - NA_KC // 2, 0, GRID_W - NA_KC)
    mask = (kc >= wstart) & (kc < wstart + NA_KC)
    coff = np.clip(kc - q + NA_KC - 1, 0, 2 * NA_KC - 2)
    drow = np.arange(NA_KR)[:, None] + np.arange(NA_KR)[None, :]
    tab = rpb.astype(F32)[:, drow][:, :, :, coff]
    tab = jnp.where(mask[None, None, None], tab, NEG_BIG)
    tab = tab.transpose(0, 1, 3, 2, 4)
    h = tab.shape[0]
    tab = tab.reshape(h // 2, 2, NA_KR, GRID_W, NA_KR * GRID_W).transpose(0, 2, 1, 3, 4)
    return tab.reshape(h // 2, NA_KR, 2 * GRID_W, NA_KR * GRID_W)


def _stack_heads(x, hd):
    lane = lax.broadcasted_iota(jnp.int32, x.shape, 1)
    zero = jnp.zeros_like(x)
    return jnp.concatenate([jnp.where(lane < hd, x, zero), jnp.where(lane >= hd, x, zero)], axis=0)


def _unstack_heads(o2, hd):
    n = o2.shape[0] // 2
    lane = lax.broadcasted_iota(jnp.int32, (n, o2.shape[1]), 1)
    return jnp.where(lane < hd, o2[:n], o2[n:])


def _na_kernel(q_ref, k_ref, v_ref, qg_ref, kg_ref, bias_ref, o_ref, qn_ref, kn_ref, vb_ref,
               *, n_ctx, scale):
    t, w = q_ref.shape
    hd = w // 2
    rows_grid = (t - n_ctx) // GRID_W
    r_i = lax.broadcasted_iota(jnp.int32, (w, w), 0)
    c_i = lax.broadcasted_iota(jnp.int32, (w, w), 1)
    seg = jnp.where((r_i // hd) == (c_i // hd), 1.0 / hd, 0.0).astype(BF16)

    def normed(x, g):
        s_hi, s_lo = _split2(x * x)
        ms = _dot(s_hi, seg) + _dot(s_lo, seg)
        return x * lax.rsqrt(ms + EPS) * g

    qn_ref[...] = (normed(q_ref[...], qg_ref[...]) * scale).astype(BF16)
    kn_ref[...] = normed(k_ref[...], kg_ref[...]).astype(BF16)
    vb_ref[...] = v_ref[...].astype(BF16)

    kc = kn_ref[0:n_ctx, :]
    vc = vb_ref[0:n_ctx, :]

    s = _dot_nt(_stack_heads(qn_ref[0:n_ctx, :], hd), kc)
    p = jnp.exp(s - jnp.max(s, axis=-1, keepdims=True))
    o2 = _dot(p.astype(BF16), vc) / jnp.sum(p, axis=-1, keepdims=True)
    o_ref[0:n_ctx, :] = _unstack_heads(o2, hd).astype(o_ref.dtype)

    kr = NA_KR
    n_win = kr * GRID_W

    def row(r, carry):
        rs = jnp.clip(r - kr // 2, 0, rows_grid - kr)
        off = rs - r + (kr - 1)
        q0 = pl.multiple_of(n_ctx + r * GRID_W, GRID_W)
        k0 = pl.multiple_of(n_ctx + rs * GRID_W, GRID_W)
        q2 = _stack_heads(qn_ref[pl.ds(q0, GRID_W), :], hd)
        kw = kn_ref[pl.ds(k0, n_win), :]
        vw = vb_ref[pl.ds(k0, n_win), :]
        s_loc = _dot_nt(q2, kw) + bias_ref[off]
        s_ctx = _dot_nt(q2, kc)
        m = jnp.maximum(jnp.max(s_loc, axis=-1, keepdims=True), jnp.max(s_ctx, axis=-1, keepdims=True))
        p_loc = jnp.exp(s_loc - m)
        p_ctx = jnp.exp(s_ctx - m)
        den = jnp.sum(p_loc, axis=-1, keepdims=True) + jnp.sum(p_ctx, axis=-1, keepdims=True)
        o2 = (_dot(p_loc.astype(BF16), vw) + _dot(p_ctx.astype(BF16), vc)) / den
        o_ref[pl.ds(q0, GRID_W), :] = _unstack_heads(o2, hd).astype(o_ref.dtype)
        return carry

    lax.fori_loop(0, rows_grid, row, 0, unroll=2)


def _na_attention(qkv3, q_g, k_g, rpb, n_ctx):
    bsz, t, d3 = qkv3.shape
    d = d3 // 3
    hd = d // NA_HEADS
    w = 2 * hd
    pairs = d // w
    assert w == LANES and (t - n_ctx) // GRID_W >= NA_KR
    bias = _na_bias_table(rpb)
    kern = functools.partial(_na_kernel, n_ctx=n_ctx, scale=hd ** -0.5)

    def col(group):
        return pl.BlockSpec((None, t, w), lambda hp, b: (b, 0, group * pairs + hp))

    gain = lambda g: jnp.tile(g.astype(F32), 2).reshape(1, w)
    return pl.pallas_call(
        kern, grid=(pairs, bsz),
        in_specs=[col(0), col(1), col(2),
                  pl.BlockSpec((1, w), lambda hp, b: (0, 0)),
                  pl.BlockSpec((1, w), lambda hp, b: (0, 0)),
                  pl.BlockSpec((None,) + bias.shape[1:], lambda hp, b: (hp, 0, 0, 0))],
        out_specs=pl.BlockSpec((None, t, w), lambda hp, b: (b, 0, hp)),
        out_shape=jax.ShapeDtypeStruct((bsz, t, d), BF16),
        scratch_shapes=[pltpu.VMEM((t, w), BF16)] * 3,
        compiler_params=_cparams("parallel", "parallel"), name="na_attention",
    )(qkv3, qkv3, qkv3, gain(q_g), gain(k_g), bias)


FF_TILE = 256
FFN_ROWS = 768


def _row_gate(gt_ctx_ref, gt_ref, shape, tile_in_batch, n_ctx):
    rows = lax.broadcasted_iota(jnp.int32, shape, 0)
    is_ctx = (rows < n_ctx) & (tile_in_batch == 0)
    return jnp.where(is_ctx, gt_ctx_ref[...], gt_ref[...])


def _swiglu_kernel(x_ref, w1_ref, w3_ref, w2_ref, h_ref, gtc_ref, gt_ref, o_ref, acc_ref,
                   *, n_ctx, tiles_per_batch):
    f = pl.program_id(1)

    @pl.when(f == 0)
    def _():
        acc_ref[...] = jnp.zeros_like(acc_ref)

    x = x_ref[...]
    hid = _silu(_dot(x, w1_ref[...])) * _dot(x, w3_ref[...])
    acc_ref[...] += _dot(hid.astype(BF16), w2_ref[...])

    @pl.when(f == pl.num_programs(1) - 1)
    def _():
        gate = _row_gate(gtc_ref, gt_ref, acc_ref.shape, pl.program_id(0) % tiles_per_batch, n_ctx)
        o_ref[...] = h_ref[...] + gate * acc_ref[...]


def _gate_specs(d, n_batch, tiles_per_batch, k_gate, nargs):
    if nargs == 2:
        ctx = lambda i, f: (n_batch, k_gate, 0, 0)
        own = lambda i, f: (i // tiles_per_batch, k_gate, 0, 0)
    else:
        ctx = lambda i, e, f: (n_batch, k_gate, 0, 0)
        own = lambda i, e, f: (i // tiles_per_batch, k_gate, 0, 0)
    return [pl.BlockSpec((None, None, 1, d), ctx), pl.BlockSpec((None, None, 1, d), own)]


def _swiglu_residual(x, w1, w3, w2, h, mods, n_batch, t, n_ctx):
    n, d = h.shape
    dff = w1.shape[1]
    tm, tf = FFN_ROWS, FF_TILE
    tpb = t // tm
    kern = functools.partial(_swiglu_kernel, n_ctx=n_ctx, tiles_per_batch=tpb)
    return pl.pallas_call(
        kern, grid=(n // tm, dff // tf),
        in_specs=[pl.BlockSpec((tm, d), lambda i, f: (i, 0)),
                  pl.BlockSpec((d, tf), lambda i, f: (0, f)),
                  pl.BlockSpec((d, tf), lambda i, f: (0, f)),
                  pl.BlockSpec((tf, d), lambda i, f: (f, 0)),
                  pl.BlockSpec((tm, d), lambda i, f: (i, 0))] + _gate_specs(d, n_batch, tpb, 5, 2),
        out_specs=pl.BlockSpec((tm, d), lambda i, f: (i, 0)),
        out_shape=jax.ShapeDtypeStruct((n, d), F32),
        scratch_shapes=[pltpu.VMEM((tm, d), F32)],
        compiler_params=_cparams("parallel", "arbitrary"), name="swiglu",
    )(x, w1, w3, w2, h, mods, mods)


MOE_ROWS = 512
SC_WINDOW = 128
SC_SPLIT = 2


def _route_tables(route, tm, n_exp):
    n = route.shape[0]
    e_pair = route[:, 0:2].astype(jnp.int32).reshape(-1)
    onehot = (e_pair[:, None] == jnp.arange(n_exp, dtype=jnp.int32)[None, :]).astype(jnp.int32)
    csum = jnp.cumsum(onehot, axis=0)
    rank = jnp.sum((csum - onehot) * onehot, axis=1)
    counts = csum[-1]
    padded = ((counts + tm - 1) // tm) * tm
    ends = jnp.cumsum(padded)
    slot = jnp.sum(onehot * (ends - padded)[None, :], axis=1) + rank
    p_slots = 2 * n + n_exp * tm
    tok_of_slot = jnp.zeros((p_slots,), jnp.int32).at[slot].set(jnp.arange(2 * n, dtype=jnp.int32) // 2)
    n_tiles = p_slots // tm
    n_used = ends[-1] // tm
    tile = jnp.arange(n_tiles, dtype=jnp.int32)
    te = jnp.sum(((tile * tm)[:, None] >= ends[None, :]).astype(jnp.int32), axis=1)
    te = jnp.where(tile < n_used, te, te[n_used - 1])
    return slot, tok_of_slot, te, n_used.reshape(1)


def _gather_rows(table, idx):
    v, m_full = table.shape
    p_full = idx.shape[0]
    table = table.reshape(v * SC_SPLIT, m_full // SC_SPLIT)
    idx = (idx[:, None] * SC_SPLIT + jnp.arange(SC_SPLIT, dtype=jnp.int32)[None, :]).reshape(-1)
    p = idx.shape[0]
    m = table.shape[1]
    mesh = plsc.VectorSubcoreMesh(core_axis_name="c", subcore_axis_name="s")

    @functools.partial(pl.kernel, out_type=jax.ShapeDtypeStruct((p, m), table.dtype), mesh=mesh)
    def gather(table_hbm, idx_hbm, out_hbm):
        def body(idx_vmem, out_vmem):
            pltpu.sync_copy(table_hbm.at[idx_vmem.at[0]], out_vmem)

        pltpu.emit_pipeline(
            body, grid=(p // SC_WINDOW,),
            in_specs=[pl.BlockSpec((1, SC_WINDOW), lambda i: (0, i))],
            out_specs=[pl.BlockSpec((SC_WINDOW, m), lambda i: (i, 0))],
            core_axis_name=("c", "s"), dimension_semantics=(pltpu.PARALLEL,),
        )(idx_hbm, out_hbm)

    return gather(table, idx.reshape(1, p)).reshape(p_full, m_full)


def _moe_ffn_kernel(te_ref, nu_ref, x_ref, w1_ref, w3_ref, w2_ref, y_ref, xb_ref, acc_ref):
    i = pl.program_id(0)
    f = pl.program_id(1)

    @pl.when(i < nu_ref[0])
    def _():
        half = xb_ref.shape[1] // 2

        @pl.when(f == 0)
        def _():
            lo, hi = _unpack_halves(x_ref[...])
            xb_ref[:, :half] = lo.astype(BF16)
            xb_ref[:, half:] = hi.astype(BF16)
            acc_ref[...] = jnp.zeros_like(acc_ref)

        x = xb_ref[...]
        hid = _silu(_dot(x, w1_ref[...])) * _dot(x, w3_ref[...])
        acc_ref[...] += _dot(hid.astype(BF16), w2_ref[...])

        @pl.when(f == pl.num_programs(1) - 1)
        def _():
            y_ref[...] = _pack_halves(acc_ref[...])


def _moe_ffn(xs, te, n_used, w1, w3, w2):
    p, dh = xs.shape
    d = 2 * dh
    dff = w1.shape[2]
    tm, tf = MOE_ROWS, FF_TILE
    nf = dff // tf

    def row(i, f, te_ref, nu_ref):
        return (jnp.minimum(i, nu_ref[0] - 1), 0)

    def fcol(i, f, nu_ref):
        return jnp.where(i < nu_ref[0], f, nf - 1)

    grid_spec = pltpu.PrefetchScalarGridSpec(
        num_scalar_prefetch=2, grid=(p // tm, nf),
        in_specs=[pl.BlockSpec((tm, dh), row),
                  pl.BlockSpec((None, d, tf), lambda i, f, te_ref, nu_ref: (te_ref[i], 0, fcol(i, f, nu_ref))),
                  pl.BlockSpec((None, d, tf), lambda i, f, te_ref, nu_ref: (te_ref[i], 0, fcol(i, f, nu_ref))),
                  pl.BlockSpec((None, tf, d), lambda i, f, te_ref, nu_ref: (te_ref[i], fcol(i, f, nu_ref), 0))],
        out_specs=pl.BlockSpec((tm, dh), row),
        scratch_shapes=[pltpu.VMEM((tm, d), BF16), pltpu.VMEM((tm, d), F32)])
    return pl.pallas_call(
        _moe_ffn_kernel, grid_spec=grid_spec,
        out_shape=jax.ShapeDtypeStruct((p, dh), jnp.int32),
        compiler_params=_cparams("arbitrary", "arbitrary"), name="moe_ffn",
    )(te, n_used, xs, w1, w3, w2)


def _moe_combine_kernel(y_ref, route_ref, h_ref, gt_ref, o_ref):
    half = h_ref.shape[1] // 2
    r = route_ref[...]
    w_a = r[:, 2:3]
    w_b = r[:, 3:4]
    a_lo, a_hi = _unpack_halves(y_ref[:, :half])
    b_lo, b_hi = _unpack_halves(y_ref[:, half:])
    o_ref[:, :half] = h_ref[:, :half] + gt_ref[:, :half] * (w_a * a_lo + w_b * b_lo)
    o_ref[:, half:] = h_ref[:, half:] + gt_ref[:, half:] * (w_a * a_hi + w_b * b_hi)


def _moe_combine(yg, route, h, mods, n_batch, tiles_per_batch):
    n, d = h.shape
    tm = TOKEN_TILE
    return pl.pallas_call(
        _moe_combine_kernel, grid=(n // tm,),
        in_specs=[pl.BlockSpec((tm, d), lambda i: (i, 0)),
                  pl.BlockSpec((tm, LANES), lambda i: (i, 0)),
                  pl.BlockSpec((tm, d), lambda i: (i, 0)),
                  pl.BlockSpec((None, None, 1, d), _mod_row_spec(5, tiles_per_batch, n_batch))],
        out_specs=pl.BlockSpec((tm, d), lambda i: (i, 0)),
        out_shape=jax.ShapeDtypeStruct((n, d), F32),
        compiler_params=_cparams("parallel"), name="moe_combine",
    )(yg, route, h, mods)


def _moe_residual(u_packed, route, w1, w3, w2, h, mods, n_batch, tiles_per_batch):
    n, d = h.shape
    slot, tok_of_slot, te, n_used = _route_tables(route, MOE_ROWS, w1.shape[0])
    xs = _gather_rows(u_packed, tok_of_slot)
    ys = _moe_ffn(xs, te, n_used, w1, w3, w2)
    yg = _gather_rows(ys, slot).reshape(n, d)
    return _moe_combine(yg, route, h, mods, n_batch, tiles_per_batch)


def kernel(x, c, ctx, c_ctx, w_mod, b_mod, norm_mix_g, norm_ffn_g, ev_w_in, ev_w_out, hg_lb_logits, hg_onorm_g, rg_conv_w, rg_conv_b, rg_wa, rg_ba, rg_wx, rg_bx, rg_lambda, na_w_qkv, na_w_o, na_q_g, na_k_g, na_rpb, ffn_w1, ffn_w3, ffn_w2, moe_router, moe_w1, moe_w3, moe_w2):
    bsz, seq, d = x.shape
    n_ctx = ctx.shape[1]
    t = n_ctx + seq
    depth = w_mod.shape[0]
    tpb = t // TOKEN_TILE
    assert n_ctx % TOKEN_TILE == 0 and seq % TOKEN_TILE == 0 and t % FFN_ROWS == 0

    rows = -(-(bsz + 1) // 8) * 8
    cond = jnp.zeros((rows, d), F32).at[:bsz].set(c).at[bsz].set(c_ctx)
    mods_all = _adaln(cond, w_mod, b_mod).reshape(depth, rows, N_MOD, 1, d)

    lb_p = jax.nn.softmax(hg_lb_logits.astype(F32), axis=0)
    lb_all = jnp.cumsum(lb_p, axis=0) - lb_p[0]

    h = jnp.concatenate([ctx, x], axis=1).reshape(bsz * t, d)
    for layer in range(depth):
        j = layer // 2
        mods = mods_all[layer]
        u = _modulate(h, norm_mix_g[layer].reshape(1, d), mods, 0, bsz, tpb)
        if layer % 2 == 0:
            p = _matmul(u, ev_w_in[j].astype(BF16), F32, 768, 512)
            p3 = p.reshape(bsz, t, -1)
            hg = _hgrn2(p3, lb_all[j], n_ctx)
            half = hg.shape[-1]
            rg = _rglru(p3, 5 * half // (d - half), rg_conv_w[j], rg_conv_b[j], rg_wa[j], rg_ba[j],
                        rg_wx[j], rg_bx[j], rg_lambda[j], n_ctx)
            h = _merge_out(hg.reshape(bsz * t, half), rg.reshape(bsz * t, d - half), p, hg_onorm_g[j],
                           ev_w_out[j].astype(BF16), h, mods, bsz, tpb)
        else:
            qkv = _matmul(u, na_w_qkv[j].astype(BF16), F32, 768, 512)
            o = _na_attention(qkv.reshape(bsz, t, 3 * d), na_q_g[j], na_k_g[j], na_rpb[j], n_ctx)
            h = _matmul_residual(o.reshape(bsz * t, d), na_w_o[j].astype(BF16), h, mods, 2, bsz, tpb)
        if layer % 2 == 0:
            u = _modulate(h, norm_ffn_g[layer].reshape(1, d), mods, 3, bsz, tpb)
            h = _swiglu_residual(u, ffn_w1[j].astype(BF16), ffn_w3[j].astype(BF16),
                                 ffn_w2[j].astype(BF16), h, mods, bsz, t, n_ctx)
        else:
            u, route = _modulate(h, norm_ffn_g[layer].reshape(1, d), mods, 3, bsz, tpb, moe_router[j])
            h = _moe_residual(u, route, moe_w1[j].astype(BF16), moe_w3[j].astype(BF16),
                              moe_w2[j].astype(BF16), h, mods, bsz, tpb)
    return h.reshape(bsz, t, d)[:, n_ctx:]
```

```python
import functools

import numpy as np
import jax
import jax.numpy as jnp
from jax import lax
from jax.experimental import pallas as pl
from jax.experimental.pallas import tpu as pltpu
from jax.experimental.pallas import tpu_sc as plsc

F32 = jnp.float32
BF16 = jnp.bfloat16

EPS = 1e-6
NEG_BIG = -1e9
LB_MIN = 1e-30
N_MOD = 6
GRID_W = 64
HG_HEAD_DIM = 128
RG_BLOCKS = 8
RG_CONV = 4
RG_C = 8.0
NA_HEADS = 16
NA_KR = 8
NA_KC = 16
N_EXPERTS = 8

LANES = 128
SC_WINDOW = 128
SC_PIECES = 2
TOKEN_TILE = 256
HG_CHUNK = 128
HG_LEVELS = 7
RG_SCAN_BLOCK = 256
VMEM_LIMIT = 56 * 1024 * 1024


def _cparams(*sem):
    return pltpu.CompilerParams(dimension_semantics=sem, vmem_limit_bytes=VMEM_LIMIT)


def _dot(a, b):
    return jnp.dot(a, b, preferred_element_type=F32)


def _dot_nt(a, b):
    return lax.dot_general(a, b, (((1,), (1,)), ((), ())), preferred_element_type=F32)


def _dot_tn(a, b):
    return lax.dot_general(a, b, (((0,), (0,)), ((), ())), preferred_element_type=F32)


def _split2(x):
    hi = x.astype(BF16)
    lo = (x - hi.astype(F32)).astype(BF16)
    return hi, lo


def _split3(x):
    hi = x.astype(BF16)
    r = x - hi.astype(F32)
    mid = r.astype(BF16)
    lo = (r - mid.astype(F32)).astype(BF16)
    return hi, mid, lo


def _sigmoid(x):
    return 1.0 / (1.0 + jnp.exp(-x))


def _silu(x):
    return x * _sigmoid(x)


def _log_sigmoid(x):
    return jnp.minimum(x, 0.0) - jnp.log1p(jnp.exp(-jnp.abs(x)))


def _gelu_tanh(x):
    return 0.5 * x * (1.0 + jnp.tanh(np.sqrt(2.0 / np.pi) * (x + 0.044715 * (x * x * x))))


def _adaln_kernel(c_ref, w_ref, b_ref, o_ref):
    a_hi, a_lo = _split2(_silu(c_ref[...]))
    w_hi, w_lo = _split2(w_ref[...])
    o_ref[...] = _dot(a_hi, w_hi) + _dot(a_hi, w_lo) + _dot(a_lo, w_hi) + b_ref[...]


def _adaln(cond, w_mod, b_mod):
    depth, d, n = w_mod.shape
    r = cond.shape[0]
    tn = 1536
    return pl.pallas_call(
        _adaln_kernel,
        grid=(depth, n // tn),
        in_specs=[pl.BlockSpec((r, d), lambda l, j: (0, 0)),
                  pl.BlockSpec((None, d, tn), lambda l, j: (l, 0, j)),
                  pl.BlockSpec((None, 1, tn), lambda l, j: (l, 0, j))],
        out_specs=pl.BlockSpec((None, r, tn), lambda l, j: (l, 0, j)),
        out_shape=jax.ShapeDtypeStruct((depth, r, n), F32),
        compiler_params=_cparams("parallel", "parallel"),
        name="adaln",
    )(cond, w_mod, b_mod.reshape(depth, 1, n))


def _mod_row_spec(k, tiles_per_batch, n_batch):
    def idx(i):
        row = jnp.where(i % tiles_per_batch == 0, n_batch, i // tiles_per_batch)
        return (row, k, 0, 0)
    return idx


def _modulated(h_ref, g_ref, sh_ref, sc_ref):
    x = h_ref[...]
    y = x * lax.rsqrt(jnp.mean(x * x, axis=-1, keepdims=True) + EPS)
    return (y * g_ref[...]) * (1.0 + sc_ref[...]) + sh_ref[...]


def _modulate_kernel(h_ref, g_ref, sh_ref, sc_ref, u_ref):
    u_ref[...] = _modulated(h_ref, g_ref, sh_ref, sc_ref).astype(u_ref.dtype)


def _pack_halves(x):
    m = x.shape[1] // 2
    lo = lax.bitcast_convert_type(x[:, :m].astype(BF16).astype(F32), jnp.int32)
    hi = lax.bitcast_convert_type(x[:, m:].astype(BF16).astype(F32), jnp.int32)
    return lax.shift_right_logical(lo, 16) | (hi & jnp.int32(-65536))


def _unpack_halves(w):
    lo = lax.bitcast_convert_type(lax.shift_left(w, 16), F32)
    hi = lax.bitcast_convert_type(w & jnp.int32(-65536), F32)
    return lo, hi


def _store_pieces(ref, w):
    m = w.shape[1] // SC_PIECES
    for j in range(SC_PIECES):
        ref[j] = w[:, j * m:(j + 1) * m]


def _load_unpacked(ref):
    parts = [_unpack_halves(ref[j]) for j in range(SC_PIECES)]
    return jnp.concatenate([lo for lo, _ in parts] + [hi for _, hi in parts], axis=1)


def _modulate_router_kernel(h_ref, g_ref, sh_ref, sc_ref, wr_ref, u_ref, route_ref):
    u = _modulated(h_ref, g_ref, sh_ref, sc_ref)
    _store_pieces(u_ref, _pack_halves(u))
    u0, u1, u2 = _split3(u)
    w0, w1, w2 = _split3(wr_ref[...])
    logits = (_dot(u0, w0) + _dot(u0, w1) + _dot(u1, w0)
              + _dot(u1, w1) + _dot(u0, w2) + _dot(u2, w0))
    lane = lax.broadcasted_iota(jnp.int32, logits.shape, 1).astype(F32)
    lg = jnp.where(lane < N_EXPERTS, logits, -jnp.inf)
    m1 = jnp.max(lg, axis=-1, keepdims=True)
    i1 = jnp.min(jnp.where(lg == m1, lane, float(LANES)), axis=-1, keepdims=True)
    lg2 = jnp.where(lane == i1, -jnp.inf, lg)
    m2 = jnp.max(lg2, axis=-1, keepdims=True)
    i2 = jnp.min(jnp.where(lg2 == m2, lane, float(LANES)), axis=-1, keepdims=True)
    e2 = jnp.exp(m2 - m1)
    den = 1.0 + e2
    route_ref[...] = jnp.where(lane == 0, i1, jnp.where(lane == 1, i2, jnp.where(lane == 2, 1.0 / den, e2 / den)))


def _modulate(h, g, mods, k_shift, n_batch, tiles_per_batch, w_router=None):
    n, d = h.shape
    tm = TOKEN_TILE
    row = pl.BlockSpec((tm, d), lambda i: (i, 0))
    vec = pl.BlockSpec((1, d), lambda i: (0, 0))

    def mod_spec(k):
        return pl.BlockSpec((None, None, 1, d), _mod_row_spec(k, tiles_per_batch, n_batch))

    in_specs = [row, vec, mod_spec(k_shift), mod_spec(k_shift + 1)]
    if w_router is None:
        return pl.pallas_call(
            _modulate_kernel, grid=(n // tm,), in_specs=in_specs, out_specs=row,
            out_shape=jax.ShapeDtypeStruct((n, d), BF16),
            compiler_params=_cparams("parallel"), name="modulate",
        )(h, g, mods, mods)
    wr = jnp.zeros((d, LANES), F32).at[:, :N_EXPERTS].set(w_router)
    return pl.pallas_call(
        _modulate_router_kernel, grid=(n // tm,),
        in_specs=in_specs + [pl.BlockSpec((d, LANES), lambda i: (0, 0))],
        out_specs=[pl.BlockSpec((SC_PIECES, tm, d // 2 // SC_PIECES), lambda i: (0, i, 0)),
                   pl.BlockSpec((tm, LANES), lambda i: (i, 0))],
        out_shape=[jax.ShapeDtypeStruct((SC_PIECES, n, d // 2 // SC_PIECES), jnp.int32),
                   jax.ShapeDtypeStruct((n, LANES), F32)],
        compiler_params=_cparams("parallel"), name="modulate_router",
    )(h, g, mods, mods, wr)


def _mm_kernel(x_ref, w_ref, o_ref):
    o_ref[...] = _dot(x_ref[...], w_ref[...]).astype(o_ref.dtype)


def _matmul(x, w, out_dtype, tm, tn):
    m, k = x.shape
    n = w.shape[1]
    return pl.pallas_call(
        _mm_kernel, grid=(n // tn, m // tm),
        in_specs=[pl.BlockSpec((tm, k), lambda j, i: (i, 0)),
                  pl.BlockSpec((k, tn), lambda j, i: (0, j))],
        out_specs=pl.BlockSpec((tm, tn), lambda j, i: (i, j)),
        out_shape=jax.ShapeDtypeStruct((m, n), out_dtype),
        compiler_params=_cparams("parallel", "parallel"), name="matmul",
    )(x, w)


def _mm_res_kernel(x_ref, w_ref, h_ref, gt_ref, o_ref):
    o_ref[...] = h_ref[...] + gt_ref[...] * _dot(x_ref[...], w_ref[...])


def _matmul_residual(x, w, h, mods, k_gate, n_batch, tiles_per_batch):
    n, d = h.shape
    kdim = x.shape[1]
    tm = TOKEN_TILE
    return pl.pallas_call(
        _mm_res_kernel, grid=(n // tm,),
        in_specs=[pl.BlockSpec((tm, kdim), lambda i: (i, 0)),
                  pl.BlockSpec((kdim, d), lambda i: (0, 0)),
                  pl.BlockSpec((tm, d), lambda i: (i, 0)),
                  pl.BlockSpec((None, None, 1, d), _mod_row_spec(k_gate, tiles_per_batch, n_batch))],
        out_specs=pl.BlockSpec((tm, d), lambda i: (i, 0)),
        out_shape=jax.ShapeDtypeStruct((n, d), F32),
        compiler_params=_cparams("parallel"), name="matmul_residual",
    )(x, w, h, mods)


def _hg_tables():
    c = HG_CHUNK
    t = np.arange(c)[:, None]
    u = np.arange(c)[None, :]
    mats = [(u <= t)]
    for k in range(1, HG_LEVELS + 1):
        m = ((t >> k) << k) + (1 << (k - 1)) - 1
        upper = t > m
        mats.append(np.where(upper, (u > m) & (u <= t), (u > t) & (u <= m)))
    mats.append(u > t)
    fwd = np.concatenate(mats, axis=0).astype(np.float32)
    bwd = fwd.reshape(-1, c, c)[:, ::-1, ::-1].reshape(-1, c)
    x = t ^ u
    lvl = np.where(t > u, np.floor(np.log2(np.maximum(x, 1))).astype(np.int32) + 1, -1)
    lvl = np.where(t == u, 0, lvl).astype(np.int32)
    return np.stack([fwd, bwd]), np.stack([lvl, lvl.T])


def _hg_chunk(d, t0, st, q_ref, z_ref, v_ref, lb_ref, m_ref, lvl_ref):
    c = HG_CHUNK
    lb = lb_ref[d:d + 1, :]
    log_lb = jnp.log(jnp.maximum(lb, LB_MIN))
    q = _silu(q_ref[pl.ds(t0, c), :])
    z = z_ref[pl.ds(t0, c), :]
    v = v_ref[pl.ds(t0, c), :].astype(BF16)
    k = (1.0 - lb) * _sigmoid(-z)
    b = jnp.log1p(-lb) + _log_sigmoid(z)
    logf = jnp.maximum(log_lb, b) + jnp.log1p(jnp.exp(-jnp.abs(log_lb - b)))
    g_hi, g_lo = _split2(logf)
    mm = m_ref[d]
    e = jnp.exp(_dot(mm, g_hi) + _dot(mm, g_lo))
    lvl = lvl_ref[d]
    att = jnp.where(lvl == 0, _dot_nt(q.astype(BF16), k.astype(BF16)), 0.0)
    for lev in range(1, HG_LEVELS + 1):
        el = e[lev * c:(lev + 1) * c]
        att = jnp.where(lvl == lev, _dot_nt((q * el).astype(BF16), (k * el).astype(BF16)), att)
    e0 = e[0:c]
    o = _dot(att.astype(BF16), v) + _dot_nt((q * e0).astype(BF16), st.astype(BF16))
    k_last = (k * e[(HG_LEVELS + 1) * c:]).astype(BF16)
    tot_row = c - 1 if d == 0 else 0
    return o, st * e0[tot_row:tot_row + 1, :] + _dot_tn(v, k_last)


def _hg_kernel(q_ref, ff_ref, fb_ref, v_ref, lb_ref, m_ref, lvl_ref, o_ref, ob_ref, *, n_ctx_chunks, n_chunks):
    c = HG_CHUNK

    def step(i, carry):
        st_f, st_b = carry
        cb = jnp.where(i < n_ctx_chunks, n_ctx_chunks - 1 - i, n_chunks - 1 - (i - n_ctx_chunks))
        tf = pl.multiple_of(i * c, c)
        tb = pl.multiple_of(cb * c, c)
        o_f, st_f = _hg_chunk(0, tf, st_f, q_ref, ff_ref, v_ref, lb_ref, m_ref, lvl_ref)
        o_b, st_b = _hg_chunk(1, tb, st_b, q_ref, fb_ref, v_ref, lb_ref, m_ref, lvl_ref)
        o_ref[pl.ds(tf, c), :] = o_f
        ob_ref[pl.ds(tb, c), :] = o_b
        return st_f, st_b

    zero = jnp.zeros((c, c), F32)
    lax.fori_loop(0, n_chunks, step, (zero, zero), unroll=2)
    o_ref[...] += ob_ref[...]


def _hgrn2(p3, lb, n_ctx):
    bsz, t, _ = p3.shape
    width = lb.shape[1]
    heads = width // HG_HEAD_DIM
    mats, lvl = _hg_tables()
    c = HG_CHUNK

    def col(group):
        return pl.BlockSpec((None, t, c), lambda b, h: (b, 0, group * heads + h))

    kern = functools.partial(_hg_kernel, n_ctx_chunks=n_ctx // c, n_chunks=t // c)
    return pl.pallas_call(
        kern, grid=(bsz, heads),
        in_specs=[col(0), col(1), col(2), col(3),
                  pl.BlockSpec((2, c), lambda b, h: (0, h)),
                  pl.BlockSpec(mats.shape, lambda b, h: (0, 0, 0)),
                  pl.BlockSpec(lvl.shape, lambda b, h: (0, 0, 0))],
        out_specs=pl.BlockSpec((None, t, c), lambda b, h: (b, 0, h)),
        out_shape=jax.ShapeDtypeStruct((bsz, t, width), F32),
        scratch_shapes=[pltpu.VMEM((t, c), F32)],
        compiler_params=_cparams("parallel", "parallel"), name="hgrn2",
    )(p3, p3, p3, p3, lb, jnp.asarray(mats, BF16), jnp.asarray(lvl))


PAD = 8


def _rg_kernel(x_ref, cw_ref, cb_ref, wa_ref, ba_ref, wx_ref, bx_ref, lam_ref, o_ref,
               xpad_ref, xc_ref, *, n_ctx):
    t, w = x_ref.shape
    n_lat = t - n_ctx
    lat0 = n_ctx + 2 * PAD
    xpad_ref[...] = jnp.zeros(xpad_ref.shape, F32)
    xpad_ref[PAD:PAD + n_ctx, :] = x_ref[0:n_ctx, :]
    xpad_ref[lat0:lat0 + n_lat, :] = x_ref[n_ctx:t, :]
    left = RG_CONV // 2
    for base, dst, n in ((PAD, 0, n_ctx), (lat0, n_ctx, n_lat)):
        acc = cb_ref[...] + cw_ref[0:1, :] * xpad_ref[base - left:base - left + n, :]
        for j in range(1, RG_CONV):
            acc = acc + cw_ref[j:j + 1, :] * xpad_ref[base - left + j:base - left + j + n, :]
        xc_ref[dst:dst + n, :] = acc

    blk = RG_SCAN_BLOCK
    n_blocks = t // blk
    n_ctx_blocks = n_ctx // blk
    rows = lax.broadcasted_iota(jnp.int32, (blk, w), 0)
    for d in range(2):
        log_sig_lam = _log_sigmoid(lam_ref[d:d + 1, :])

        def step(i, hin, d=d, log_sig_lam=log_sig_lam):
            if d == 0:
                bi = i
            else:
                bi = jnp.where(i < n_ctx_blocks, n_ctx_blocks - 1 - i, n_blocks - 1 - (i - n_ctx_blocks))
            t0 = pl.multiple_of(bi * blk, blk)
            xc = xc_ref[pl.ds(t0, blk), :]
            xb = xc.astype(BF16)
            r = _sigmoid(_dot(xb, wa_ref[d]) + ba_ref[d:d + 1, :])
            ig = _sigmoid(_dot(xb, wx_ref[d]) + bx_ref[d:d + 1, :])
            log_a = RG_C * r * log_sig_lam
            a = jnp.exp(log_a)
            b = jnp.sqrt(jnp.maximum(-jnp.tanh(log_a) * (a * a + 1.0), 0.0)) * (ig * xc)
            s = 1
            while s < blk:
                if d == 0:
                    keep = rows >= s
                    a_sh = pltpu.roll(a, s, 0)
                    b_sh = pltpu.roll(b, s, 0)
                else:
                    keep = rows < blk - s
                    a_sh = pltpu.roll(a, blk - s, 0)
                    b_sh = pltpu.roll(b, blk - s, 0)
                b = jnp.where(keep, a * b_sh, 0.0) + b
                a = jnp.where(keep, a * a_sh, a)
                s *= 2
            h = a * hin + b
            if d == 0:
                o_ref[pl.ds(t0, blk), :] = h
                return h[blk - 1:blk, :]
            o_ref[pl.ds(t0, blk), :] += h
            return h[0:1, :]

        lax.fori_loop(0, n_blocks, step, jnp.zeros((1, w), F32))


def _blockdiag_dense(w):
    nd, nb, k, _ = w.shape
    eye = jnp.eye(nb, dtype=w.dtype)
    return jnp.einsum('dnij,nm->dnimj', w, eye).reshape(nd, nb * k, nb * k)


def _rglru(p3, col_block, conv_w, conv_b, wa, ba, wx, bx, lam, n_ctx):
    bsz, t, _ = p3.shape
    w = conv_w.shape[1]
    full = lambda shape: pl.BlockSpec(shape, lambda b: (0,) * len(shape))
    kern = functools.partial(_rg_kernel, n_ctx=n_ctx)
    return pl.pallas_call(
        kern, grid=(bsz,),
        in_specs=[pl.BlockSpec((None, t, w), lambda b: (b, 0, col_block)),
                  full((RG_CONV, w)), full((1, w)),
                  full((2, w, w)), full((2, w)), full((2, w, w)), full((2, w)), full((2, w))],
        out_specs=pl.BlockSpec((None, t, w), lambda b: (b, 0, 0)),
        out_shape=jax.ShapeDtypeStruct((bsz, t, w), F32),
        scratch_shapes=[pltpu.VMEM((t + 3 * PAD, w), F32), pltpu.VMEM((t, w), F32)],
        compiler_params=_cparams("parallel"), name="rglru",
    )(p3, conv_w, conv_b.reshape(1, w), _blockdiag_dense(wa).astype(BF16), ba,
      _blockdiag_dense(wx).astype(BF16), bx, lam)


def _merge_kernel(hg_ref, rg_ref, gate_ref, y_ref, og_ref, w_ref, h_ref, gt_ref, o_ref):
    hg = hg_ref[...]
    half = hg.shape[1]
    hn = hg * lax.rsqrt(jnp.mean(hg * hg, axis=-1, keepdims=True) + EPS) * og_ref[...]
    a = (hn * _silu(gate_ref[...])).astype(BF16)
    b = (rg_ref[...] * _gelu_tanh(y_ref[...])).astype(BF16)
    o = _dot(a, w_ref[0:half, :]) + _dot(b, w_ref[half:, :])
    o_ref[...] = h_ref[...] + gt_ref[...] * o


def _merge_out(hg, rg, p, onorm_g, w_out, h, mods, n_batch, tiles_per_batch):
    n, d = h.shape
    half = hg.shape[1]
    tm = TOKEN_TILE
    blk = lambda c: pl.BlockSpec((tm, half), lambda i: (i, c))
    return pl.pallas_call(
        _merge_kernel, grid=(n // tm,),
        in_specs=[blk(0), blk(0), blk(4), blk(6),
                  pl.BlockSpec((1, half), lambda i: (0, 0)),
                  pl.BlockSpec((d, d), lambda i: (0, 0)),
                  pl.BlockSpec((tm, d), lambda i: (i, 0)),
                  pl.BlockSpec((None, None, 1, d), _mod_row_spec(2, tiles_per_batch, n_batch))],
        out_specs=pl.BlockSpec((tm, d), lambda i: (i, 0)),
        out_shape=jax.ShapeDtypeStruct((n, d), F32),
        compiler_params=_cparams("parallel"), name="merge_out",
    )(hg, rg, p, p, onorm_g.reshape(1, half), w_out, h, mods)


def _na_bias_table(rpb):
    q = np.arange(GRID_W)[:, None]
    kc = np.arange(GRID_W)[None, :]
    wstart = np.clip(q - NA_KC // 2, 0, GRID_W - NA_KC)
    mask = (kc >= wstart) & (kc < wstart + NA_KC)
    coff = np.clip(kc - q + NA_KC - 1, 0, 2 * NA_KC - 2)
    drow = np.arange(NA_KR)[:, None] + np.arange(NA_KR)[None, :]
    tab = rpb.astype(F32)[:, drow][:, :, :, coff]
    tab = jnp.where(mask[None, None, None], tab, NEG_BIG)
    tab = tab.transpose(0, 1, 3, 2, 4)
    h = tab.shape[0]
    tab = tab.reshape(h // 2, 2, NA_KR, GRID_W, NA_KR * GRID_W).transpose(0, 2, 1, 3, 4)
    return tab.reshape(h // 2, NA_KR, 2 * GRID_W, NA_KR * GRID_W)


def _stack_heads(x, hd):
    lane = lax.broadcasted_iota(jnp.int32, x.shape, 1)
    zero = jnp.zeros_like(x)
    return jnp.concatenate([jnp.where(lane < hd, x, zero), jnp.where(lane >= hd, x, zero)], axis=0)


def _unstack_heads(o2, hd):
    n = o2.shape[0] // 2
    lane = lax.broadcasted_iota(jnp.int32, (n, o2.shape[1]), 1)
    return jnp.where(lane < hd, o2[:n], o2[n:])


def _na_kernel(q_ref, k_ref, v_ref, qg_ref, kg_ref, bias_ref, o_ref, qn_ref, kn_ref, vb_ref,
               *, n_ctx, scale):
    t, w = q_ref.shape
    hd = w // 2
    rows_grid = (t - n_ctx) // GRID_W
    r_i = lax.broadcasted_iota(jnp.int32, (w, w), 0)
    c_i = lax.broadcasted_iota(jnp.int32, (w, w), 1)
    seg = jnp.where((r_i // hd) == (c_i // hd), 1.0 / hd, 0.0).astype(BF16)

    def normed(x, g):
        s_hi, s_lo = _split2(x * x)
        ms = _dot(s_hi, seg) + _dot(s_lo, seg)
        return x * lax.rsqrt(ms + EPS) * g

    qn_ref[...] = (normed(q_ref[...], qg_ref[...]) * scale).astype(BF16)
    kn_ref[...] = normed(k_ref[...], kg_ref[...]).astype(BF16)
    vb_ref[...] = v_ref[...].astype(BF16)

    kc = kn_ref[0:n_ctx, :]
    vc = vb_ref[0:n_ctx, :]

    s = _dot_nt(_stack_heads(qn_ref[0:n_ctx, :], hd), kc)
    p = jnp.exp(s - jnp.max(s, axis=-1, keepdims=True))
    o2 = _dot(p.astype(BF16), vc) / jnp.sum(p, axis=-1, keepdims=True)
    o_ref[0:n_ctx, :] = _unstack_heads(o2, hd).astype(o_ref.dtype)

    kr = NA_KR
    n_win = kr * GRID_W

    def row(r, carry):
        rs = jnp.clip(r - kr // 2, 0, rows_grid - kr)
        off = rs - r + (kr - 1)
        q0 = pl.multiple_of(n_ctx + r * GRID_W, GRID_W)
        k0 = pl.multiple_of(n_ctx + rs * GRID_W, GRID_W)
        q2 = _stack_heads(qn_ref[pl.ds(q0, GRID_W), :], hd)
        kw = kn_ref[pl.ds(k0, n_win), :]
        vw = vb_ref[pl.ds(k0, n_win), :]
        s_loc = _dot_nt(q2, kw) + bias_ref[off]
        s_ctx = _dot_nt(q2, kc)
        m = jnp.maximum(jnp.max(s_loc, axis=-1, keepdims=True), jnp.max(s_ctx, axis=-1, keepdims=True))
        p_loc = jnp.exp(s_loc - m)
        p_ctx = jnp.exp(s_ctx - m)
        den = jnp.sum(p_loc, axis=-1, keepdims=True) + jnp.sum(p_ctx, axis=-1, keepdims=True)
        o2 = (_dot(p_loc.astype(BF16), vw) + _dot(p_ctx.astype(BF16), vc)) / den
        o_ref[pl.ds(q0, GRID_W), :] = _unstack_heads(o2, hd).astype(o_ref.dtype)
        return carry

    lax.fori_loop(0, rows_grid, row, 0, unroll=4)


def _na_attention(qkv3, q_g, k_g, rpb, n_ctx):
    bsz, t, d3 = qkv3.shape
    d = d3 // 3
    hd = d // NA_HEADS
    w = 2 * hd
    pairs = d // w
    assert w == LANES and (t - n_ctx) // GRID_W >= NA_KR
    bias = _na_bias_table(rpb)
    kern = functools.partial(_na_kernel, n_ctx=n_ctx, scale=hd ** -0.5)

    def col(group):
        return pl.BlockSpec((None, t, w), lambda hp, b: (b, 0, group * pairs + hp))

    gain = lambda g: jnp.tile(g.astype(F32), 2).reshape(1, w)
    return pl.pallas_call(
        kern, grid=(pairs, bsz),
        in_specs=[col(0), col(1), col(2),
                  pl.BlockSpec((1, w), lambda hp, b: (0, 0)),
                  pl.BlockSpec((1, w), lambda hp, b: (0, 0)),
                  pl.BlockSpec((None,) + bias.shape[1:], lambda hp, b: (hp, 0, 0, 0))],
        out_specs=pl.BlockSpec((None, t, w), lambda hp, b: (b, 0, hp)),
        out_shape=jax.ShapeDtypeStruct((bsz, t, d), BF16),
        scratch_shapes=[pltpu.VMEM((t, w), BF16)] * 3,
        compiler_params=_cparams("parallel", "parallel"), name="na_attention",
    )(qkv3, qkv3, qkv3, gain(q_g), gain(k_g), bias)


FF_TILE = 1408
FFN_ROWS = 768


def _row_gate(gt_ctx_ref, gt_ref, shape, tile_in_batch, n_ctx):
    rows = lax.broadcasted_iota(jnp.int32, shape, 0)
    is_ctx = (rows < n_ctx) & (tile_in_batch == 0)
    return jnp.where(is_ctx, gt_ctx_ref[...], gt_ref[...])


def _swiglu_kernel(x_ref, w1_ref, w3_ref, w2_ref, h_ref, gtc_ref, gt_ref, o_ref, acc_ref,
                   *, n_ctx, tiles_per_batch):
    f = pl.program_id(1)

    @pl.when(f == 0)
    def _():
        acc_ref[...] = jnp.zeros_like(acc_ref)

    x = x_ref[...]
    hid = _silu(_dot(x, w1_ref[...])) * _dot(x, w3_ref[...])
    acc_ref[...] += _dot(hid.astype(BF16), w2_ref[...])

    @pl.when(f == pl.num_programs(1) - 1)
    def _():
        gate = _row_gate(gtc_ref, gt_ref, acc_ref.shape, pl.program_id(0) % tiles_per_batch, n_ctx)
        o_ref[...] = h_ref[...] + gate * acc_ref[...]


def _gate_specs(d, n_batch, tiles_per_batch, k_gate, nargs):
    if nargs == 2:
        ctx = lambda i, f: (n_batch, k_gate, 0, 0)
        own = lambda i, f: (i // tiles_per_batch, k_gate, 0, 0)
    else:
        ctx = lambda i, e, f: (n_batch, k_gate, 0, 0)
        own = lambda i, e, f: (i // tiles_per_batch, k_gate, 0, 0)
    return [pl.BlockSpec((None, None, 1, d), ctx), pl.BlockSpec((None, None, 1, d), own)]


def _swiglu_residual(x, w1, w3, w2, h, mods, n_batch, t, n_ctx):
    n, d = h.shape
    dff = w1.shape[1]
    tm, tf = FFN_ROWS, FF_TILE
    tpb = t // tm
    kern = functools.partial(_swiglu_kernel, n_ctx=n_ctx, tiles_per_batch=tpb)
    return pl.pallas_call(
        kern, grid=(n // tm, dff // tf),
        in_specs=[pl.BlockSpec((tm, d), lambda i, f: (i, 0)),
                  pl.BlockSpec((d, tf), lambda i, f: (0, f)),
                  pl.BlockSpec((d, tf), lambda i, f: (0, f)),
                  pl.BlockSpec((tf, d), lambda i, f: (f, 0)),
                  pl.BlockSpec((tm, d), lambda i, f: (i, 0))] + _gate_specs(d, n_batch, tpb, 5, 2),
        out_specs=pl.BlockSpec((tm, d), lambda i, f: (i, 0)),
        out_shape=jax.ShapeDtypeStruct((n, d), F32),
        scratch_shapes=[pltpu.VMEM((tm, d), F32)],
        compiler_params=_cparams("parallel", "arbitrary"), name="swiglu",
    )(x, w1, w3, w2, h, mods, mods)


MOE_ROWS = 512


def _route_tables(route, tm, n_exp):
    n = route.shape[0]
    e_pair = route[:, 0:2].astype(jnp.int32).reshape(-1)
    onehot = (e_pair[:, None] == jnp.arange(n_exp, dtype=jnp.int32)[None, :]).astype(jnp.int32)
    csum = jnp.cumsum(onehot, axis=0)
    rank = jnp.sum((csum - onehot) * onehot, axis=1)
    counts = csum[-1]
    padded = ((counts + tm - 1) // tm) * tm
    ends = jnp.cumsum(padded)
    slot = jnp.sum(onehot * (ends - padded)[None, :], axis=1) + rank
    p_slots = 2 * n + n_exp * tm
    tok_of_slot = jnp.zeros((p_slots,), jnp.int32).at[slot].set(jnp.arange(2 * n, dtype=jnp.int32) // 2)
    n_tiles = p_slots // tm
    n_used = ends[-1] // tm
    tile = jnp.arange(n_tiles, dtype=jnp.int32)
    te = jnp.sum(((tile * tm)[:, None] >= ends[None, :]).astype(jnp.int32), axis=1)
    te = jnp.where(tile < n_used, te, te[n_used - 1])
    return slot.reshape(n, 2), tok_of_slot, te, n_used.reshape(1)


def _gather_rows(table, idx):
    p = idx.shape[0]
    m = table.shape[1]
    mesh = plsc.VectorSubcoreMesh(core_axis_name="c", subcore_axis_name="s")

    @functools.partial(pl.kernel, out_type=jax.ShapeDtypeStruct((p, m), table.dtype), mesh=mesh)
    def gather(table_hbm, idx_hbm, out_hbm):
        def body(idx_vmem, out_vmem):
            pltpu.sync_copy(table_hbm.at[idx_vmem.at[0]], out_vmem)

        pltpu.emit_pipeline(
            body, grid=(p // SC_WINDOW,),
            in_specs=[pl.BlockSpec((1, SC_WINDOW), lambda i: (0, i))],
            out_specs=[pl.BlockSpec((SC_WINDOW, m), lambda i: (i, 0))],
            core_axis_name=("c", "s"), dimension_semantics=(pltpu.PARALLEL,),
        )(idx_hbm, out_hbm)

    return gather(table, idx.reshape(1, p))


def _gather_pieces(pieces, idx):
    npc, v, m = pieces.shape
    offs = (jnp.arange(npc, dtype=jnp.int32) * v).reshape((npc,) + (1,) * idx.ndim)
    flat = (idx[None] + offs).reshape(-1)
    return _gather_rows(pieces.reshape(npc * v, m), flat).reshape((npc,) + idx.shape + (m,))


def _moe_ffn_kernel(te_ref, nu_ref, x_ref, w1_ref, w3_ref, w2_ref, y_ref, xb_ref, acc_ref):
    i = pl.program_id(0)
    f = pl.program_id(1)

    @pl.when(i < nu_ref[0])
    def _():
        @pl.when(f == 0)
        def _():
            xb_ref[...] = _load_unpacked(x_ref).astype(BF16)
            acc_ref[...] = jnp.zeros_like(acc_ref)

        x = xb_ref[...]
        hid = _silu(_dot(x, w1_ref[...])) * _dot(x, w3_ref[...])
        acc_ref[...] += _dot(hid.astype(BF16), w2_ref[...])

        @pl.when(f == pl.num_programs(1) - 1)
        def _():
            _store_pieces(y_ref, _pack_halves(acc_ref[...]))


def _moe_ffn(xs, te, n_used, w1, w3, w2):
    npc, p, m = xs.shape
    d = 2 * npc * m
    dff = w1.shape[2]
    tm, tf = MOE_ROWS, FF_TILE
    nf = dff // tf

    def row(i, f, te_ref, nu_ref):
        return (0, jnp.minimum(i, nu_ref[0] - 1), 0)

    def fcol(i, f, nu_ref):
        return jnp.where(i < nu_ref[0], f, nf - 1)

    grid_spec = pltpu.PrefetchScalarGridSpec(
        num_scalar_prefetch=2, grid=(p // tm, nf),
        in_specs=[pl.BlockSpec((npc, tm, m), row),
                  pl.BlockSpec((None, d, tf), lambda i, f, te_ref, nu_ref: (te_ref[i], 0, fcol(i, f, nu_ref))),
                  pl.BlockSpec((None, d, tf), lambda i, f, te_ref, nu_ref: (te_ref[i], 0, fcol(i, f, nu_ref))),
                  pl.BlockSpec((None, tf, d), lambda i, f, te_ref, nu_ref: (te_ref[i], fcol(i, f, nu_ref), 0))],
        out_specs=pl.BlockSpec((npc, tm, m), row),
        scratch_shapes=[pltpu.VMEM((tm, d), BF16), pltpu.VMEM((tm, d), F32)])
    return pl.pallas_call(
        _moe_ffn_kernel, grid_spec=grid_spec,
        out_shape=jax.ShapeDtypeStruct((npc, p, m), jnp.int32),
        compiler_params=_cparams("arbitrary", "arbitrary"), name="moe_ffn",
    )(te, n_used, xs, w1, w3, w2)


def _moe_combine_kernel(ya_ref, yb_ref, route_ref, h_ref, gt_ref, o_ref, *, skip_first_of):
    def body():
        r = route_ref[...]
        y = r[:, 2:3] * _load_unpacked(ya_ref) + r[:, 3:4] * _load_unpacked(yb_ref)
        o_ref[...] = h_ref[...] + gt_ref[...] * y

    if skip_first_of is None:
        body()
    else:
        pl.when(pl.program_id(0) % skip_first_of != 0)(body)


def _moe_combine(yg, route, h, mods, n_batch, tiles_per_batch, latent_only):
    n, d = h.shape
    npc, _, _, m = yg.shape
    tm = TOKEN_TILE
    if latent_only:
        def out_idx(i):
            b, j = i // tiles_per_batch, i % tiles_per_batch
            return (b * (tiles_per_batch - 1) + jnp.maximum(j, 1) - 1, 0)
        n_out = n // tiles_per_batch * (tiles_per_batch - 1)
    else:
        out_idx = lambda i: (i, 0)
        n_out = n
    kern = functools.partial(_moe_combine_kernel, skip_first_of=tiles_per_batch if latent_only else None)
    return pl.pallas_call(
        kern, grid=(n // tm,),
        in_specs=[pl.BlockSpec((npc, None, tm, m), lambda i: (0, 0, i, 0)),
                  pl.BlockSpec((npc, None, tm, m), lambda i: (0, 1, i, 0)),
                  pl.BlockSpec((tm, LANES), lambda i: (i, 0)),
                  pl.BlockSpec((tm, d), lambda i: (i, 0)),
                  pl.BlockSpec((None, None, 1, d), _mod_row_spec(5, tiles_per_batch, n_batch))],
        out_specs=pl.BlockSpec((tm, d), out_idx),
        out_shape=jax.ShapeDtypeStruct((n_out, d), F32),
        compiler_params=_cparams("arbitrary"), name="moe_combine",
    )(yg, yg, route, h, mods)


def _moe_residual(u_pieces, route, w1, w3, w2, h, mods, n_batch, tiles_per_batch, latent_only):
    slot, tok_of_slot, te, n_used = _route_tables(route, MOE_ROWS, w1.shape[0])
    xs = _gather_pieces(u_pieces, tok_of_slot)
    ys = _moe_ffn(xs, te, n_used, w1, w3, w2)
    yg = _gather_pieces(ys, slot.T)
    return _moe_combine(yg, route, h, mods, n_batch, tiles_per_batch, latent_only)


def kernel(x, c, ctx, c_ctx, w_mod, b_mod, norm_mix_g, norm_ffn_g, ev_w_in, ev_w_out, hg_lb_logits, hg_onorm_g, rg_conv_w, rg_conv_b, rg_wa, rg_ba, rg_wx, rg_bx, rg_lambda, na_w_qkv, na_w_o, na_q_g, na_k_g, na_rpb, ffn_w1, ffn_w3, ffn_w2, moe_router, moe_w1, moe_w3, moe_w2):
    bsz, seq, d = x.shape
    n_ctx = ctx.shape[1]
    t = n_ctx + seq
    depth = w_mod.shape[0]
    tpb = t // TOKEN_TILE
    assert n_ctx % TOKEN_TILE == 0 and seq % TOKEN_TILE == 0 and t % FFN_ROWS == 0

    rows = -(-(bsz + 1) // 8) * 8
    cond = jnp.zeros((rows, d), F32).at[:bsz].set(c).at[bsz].set(c_ctx)
    mods_all = _adaln(cond, w_mod, b_mod).reshape(depth, rows, N_MOD, 1, d)

    lb_p = jax.nn.softmax(hg_lb_logits.astype(F32), axis=0)
    lb_all = jnp.cumsum(lb_p, axis=0) - lb_p[0]

    h = jnp.concatenate([ctx, x], axis=1).reshape(bsz * t, d)
    for layer in range(depth):
        j = layer // 2
        mods = mods_all[layer]
        u = _modulate(h, norm_mix_g[layer].reshape(1, d), mods, 0, bsz, tpb)
        if layer % 2 == 0:
            p = _matmul(u, ev_w_in[j].astype(BF16), F32, 768, 512)
            p3 = p.reshape(bsz, t, -1)
            hg = _hgrn2(p3, lb_all[j], n_ctx)
            half = hg.shape[-1]
            rg = _rglru(p3, 5 * half // (d - half), rg_conv_w[j], rg_conv_b[j], rg_wa[j], rg_ba[j],
                        rg_wx[j], rg_bx[j], rg_lambda[j], n_ctx)
            h = _merge_out(hg.reshape(bsz * t, half), rg.reshape(bsz * t, d - half), p, hg_onorm_g[j],
                           ev_w_out[j].astype(BF16), h, mods, bsz, tpb)
        else:
            qkv = _matmul(u, na_w_qkv[j].astype(BF16), F32, 768, 512)
            o = _na_attention(qkv.reshape(bsz, t, 3 * d), na_q_g[j], na_k_g[j], na_rpb[j], n_ctx)
            h = _matmul_residual(o.reshape(bsz * t, d), na_w_o[j].astype(BF16), h, mods, 2, bsz, tpb)
        if layer % 2 == 0:
            u = _modulate(h, norm_ffn_g[layer].reshape(1, d), mods, 3, bsz, tpb)
            h = _swiglu_residual(u, ffn_w1[j].astype(BF16), ffn_w3[j].astype(BF16),
                                 ffn_w2[j].astype(BF16), h, mods, bsz, t, n_ctx)
        else:
            u, route = _modulate(h, norm_ffn_g[layer].reshape(1, d), mods, 3, bsz, tpb, moe_router[j])
            h = _moe_residual(u, route, moe_w1[j].astype(BF16), moe_w3[j].astype(BF16),
                              moe_w2[j].astype(BF16), h, mods, bsz, tpb, layer == depth - 1)
    if depth % 2 == 0:
        return h.reshape(bsz, seq, d)
    return h.reshape(bsz, t, d)[:, n_ctx:]
```

```python
import functools

import numpy as np
import jax
import jax.numpy as jnp
from jax import lax
from jax.experimental import pallas as pl
from jax.experimental.pallas import tpu as pltpu
from jax.experimental.pallas import tpu_sc as plsc

F32 = jnp.float32
BF16 = jnp.bfloat16

EPS = 1e-6
NEG_BIG = -1e9
LB_MIN = 1e-30
N_MOD = 6
GRID_W = 64
HG_HEAD_DIM = 128
RG_BLOCKS = 8
RG_CONV = 4
RG_C = 8.0
NA_HEADS = 16
NA_KR = 8
NA_KC = 16
N_EXPERTS = 8

LANES = 128
SC_WINDOW = 128
SC_PIECES = 2
TOKEN_TILE = 256
HG_CHUNK = 128
HG_LEVELS = 7
RG_SCAN_BLOCK = 256
VMEM_LIMIT = 56 * 1024 * 1024


def _cparams(*sem):
    return pltpu.CompilerParams(dimension_semantics=sem, vmem_limit_bytes=VMEM_LIMIT)


def _dot(a, b):
    return jnp.dot(a, b, preferred_element_type=F32)


def _dot_nt(a, b):
    return lax.dot_general(a, b, (((1,), (1,)), ((), ())), preferred_element_type=F32)


def _dot_tn(a, b):
    return lax.dot_general(a, b, (((0,), (0,)), ((), ())), preferred_element_type=F32)


def _split2(x):
    hi = x.astype(BF16)
    lo = (x - hi.astype(F32)).astype(BF16)
    return hi, lo


def _split3(x):
    hi = x.astype(BF16)
    r = x - hi.astype(F32)
    mid = r.astype(BF16)
    lo = (r - mid.astype(F32)).astype(BF16)
    return hi, mid, lo


def _sigmoid(x):
    return 1.0 / (1.0 + jnp.exp(-x))


def _silu(x):
    return x * _sigmoid(x)


def _log_sigmoid(x):
    return jnp.minimum(x, 0.0) - jnp.log1p(jnp.exp(-jnp.abs(x)))


def _gelu_tanh(x):
    return 0.5 * x * (1.0 + jnp.tanh(np.sqrt(2.0 / np.pi) * (x + 0.044715 * (x * x * x))))


def _adaln_kernel(c_ref, w_ref, b_ref, o_ref):
    a_hi, a_lo = _split2(_silu(c_ref[...]))
    w_hi, w_lo = _split2(w_ref[...])
    o_ref[...] = _dot(a_hi, w_hi) + _dot(a_hi, w_lo) + _dot(a_lo, w_hi) + b_ref[...]


def _adaln(cond, w_mod, b_mod):
    depth, d, n = w_mod.shape
    r = cond.shape[0]
    tn = 1536
    return pl.pallas_call(
        _adaln_kernel,
        grid=(depth, n // tn),
        in_specs=[pl.BlockSpec((r, d), lambda l, j: (0, 0)),
                  pl.BlockSpec((None, d, tn), lambda l, j: (l, 0, j)),
                  pl.BlockSpec((None, 1, tn), lambda l, j: (l, 0, j))],
        out_specs=pl.BlockSpec((None, r, tn), lambda l, j: (l, 0, j)),
        out_shape=jax.ShapeDtypeStruct((depth, r, n), F32),
        compiler_params=_cparams("parallel", "parallel"),
        name="adaln",
    )(cond, w_mod, b_mod.reshape(depth, 1, n))


def _mod_row_spec(k, tiles_per_batch, n_batch):
    def idx(i):
        row = jnp.where(i % tiles_per_batch == 0, n_batch, i // tiles_per_batch)
        return (row, k, 0, 0)
    return idx


def _modulated(h_ref, g_ref, sh_ref, sc_ref):
    x = h_ref[...]
    y = x * lax.rsqrt(jnp.mean(x * x, axis=-1, keepdims=True) + EPS)
    return (y * g_ref[...]) * (1.0 + sc_ref[...]) + sh_ref[...]


def _pack_halves(x):
    m = x.shape[1] // 2
    lo = lax.bitcast_convert_type(x[:, :m].astype(BF16).astype(F32), jnp.int32)
    hi = lax.bitcast_convert_type(x[:, m:].astype(BF16).astype(F32), jnp.int32)
    return lax.shift_right_logical(lo, 16) | (hi & jnp.int32(-65536))


def _unpack_halves(w):
    lo = lax.bitcast_convert_type(lax.shift_left(w, 16), F32)
    hi = lax.bitcast_convert_type(w & jnp.int32(-65536), F32)
    return lo, hi


def _store_pieces(ref, w):
    m = w.shape[1] // SC_PIECES
    for j in range(SC_PIECES):
        ref[j] = w[:, j * m:(j + 1) * m]


def _load_unpacked(ref):
    parts = [_unpack_halves(ref[j]) for j in range(SC_PIECES)]
    return jnp.concatenate([lo for lo, _ in parts] + [hi for _, hi in parts], axis=1)


def _modulate_router_kernel(h_ref, g_ref, sh_ref, sc_ref, wr_ref, u_ref, route_ref):
    u = _modulated(h_ref, g_ref, sh_ref, sc_ref)
    _store_pieces(u_ref, _pack_halves(u))
    u0, u1, u2 = _split3(u)
    w0, w1, w2 = _split3(wr_ref[...])
    logits = (_dot(u0, w0) + _dot(u0, w1) + _dot(u1, w0)
              + _dot(u1, w1) + _dot(u0, w2) + _dot(u2, w0))
    lane = lax.broadcasted_iota(jnp.int32, logits.shape, 1).astype(F32)
    lg = jnp.where(lane < N_EXPERTS, logits, -jnp.inf)
    m1 = jnp.max(lg, axis=-1, keepdims=True)
    i1 = jnp.min(jnp.where(lg == m1, lane, float(LANES)), axis=-1, keepdims=True)
    lg2 = jnp.where(lane == i1, -jnp.inf, lg)
    m2 = jnp.max(lg2, axis=-1, keepdims=True)
    i2 = jnp.min(jnp.where(lg2 == m2, lane, float(LANES)), axis=-1, keepdims=True)
    e2 = jnp.exp(m2 - m1)
    den = 1.0 + e2
    route_ref[...] = jnp.where(lane == 0, i1, jnp.where(lane == 1, i2, jnp.where(lane == 2, 1.0 / den, e2 / den)))


def _modulate_router(h, g, mods, k_shift, n_batch, tiles_per_batch, w_router):
    n, d = h.shape
    tm = TOKEN_TILE
    row = pl.BlockSpec((tm, d), lambda i: (i, 0))
    vec = pl.BlockSpec((1, d), lambda i: (0, 0))

    def mod_spec(k):
        return pl.BlockSpec((None, None, 1, d), _mod_row_spec(k, tiles_per_batch, n_batch))

    in_specs = [row, vec, mod_spec(k_shift), mod_spec(k_shift + 1)]
    wr = jnp.zeros((d, LANES), F32).at[:, :N_EXPERTS].set(w_router)
    return pl.pallas_call(
        _modulate_router_kernel, grid=(n // tm,),
        in_specs=in_specs + [pl.BlockSpec((d, LANES), lambda i: (0, 0))],
        out_specs=[pl.BlockSpec((SC_PIECES, tm, d // 2 // SC_PIECES), lambda i: (0, i, 0)),
                   pl.BlockSpec((tm, LANES), lambda i: (i, 0))],
        out_shape=[jax.ShapeDtypeStruct((SC_PIECES, n, d // 2 // SC_PIECES), jnp.int32),
                   jax.ShapeDtypeStruct((n, LANES), F32)],
        compiler_params=_cparams("parallel"), name="modulate_router",
    )(h, g, mods, mods, wr)


ROW_TILE = 768


def _row_gate(ctx_ref, own_ref, shape, tile_in_batch, n_ctx):
    rows = lax.broadcasted_iota(jnp.int32, shape, 0)
    is_ctx = (rows < n_ctx) & (tile_in_batch == 0)
    return jnp.where(is_ctx, ctx_ref[...], own_ref[...])


def _modulated_rows(h_ref, g_ref, shc_ref, sh_ref, scc_ref, sc_ref, tile_in_batch, n_ctx):
    x = h_ref[...]
    y = x * lax.rsqrt(jnp.mean(x * x, axis=-1, keepdims=True) + EPS)
    shift = _row_gate(shc_ref, sh_ref, x.shape, tile_in_batch, n_ctx)
    scale = _row_gate(scc_ref, sc_ref, x.shape, tile_in_batch, n_ctx)
    return (y * g_ref[...]) * (1.0 + scale) + shift


def _stream_specs(d, n_batch, tiles_per_batch, k, nargs):
    if nargs == 1:
        ctx = lambda i: (n_batch, k, 0, 0)
        own = lambda i: (i // tiles_per_batch, k, 0, 0)
    else:
        ctx = lambda i, f: (n_batch, k, 0, 0)
        own = lambda i, f: (i // tiles_per_batch, k, 0, 0)
    return [pl.BlockSpec((None, None, 1, d), ctx), pl.BlockSpec((None, None, 1, d), own)]


def _mod_mm_kernel(h_ref, g_ref, shc_ref, sh_ref, scc_ref, sc_ref, w_ref, o_ref, xb_ref,
                   *, n_ctx, tiles_per_batch):
    @pl.when(pl.program_id(1) == 0)
    def _():
        xb_ref[...] = _modulated_rows(h_ref, g_ref, shc_ref, sh_ref, scc_ref, sc_ref,
                                      pl.program_id(0) % tiles_per_batch, n_ctx).astype(BF16)

    o_ref[...] = _dot(xb_ref[...], w_ref[...]).astype(o_ref.dtype)


def _modulated_matmul(h, g, mods, w, out_dtype, tn, n_batch, t, n_ctx):
    n, d = h.shape
    nout = w.shape[1]
    tm = ROW_TILE
    tpb = t // tm
    kern = functools.partial(_mod_mm_kernel, n_ctx=n_ctx, tiles_per_batch=tpb)
    return pl.pallas_call(
        kern, grid=(n // tm, nout // tn),
        in_specs=[pl.BlockSpec((tm, d), lambda i, j: (i, 0)),
                  pl.BlockSpec((1, d), lambda i, j: (0, 0))]
        + _stream_specs(d, n_batch, tpb, 0, 2) + _stream_specs(d, n_batch, tpb, 1, 2)
        + [pl.BlockSpec((d, tn), lambda i, j: (0, j))],
        out_specs=pl.BlockSpec((tm, tn), lambda i, j: (i, j)),
        out_shape=jax.ShapeDtypeStruct((n, nout), out_dtype),
        scratch_shapes=[pltpu.VMEM((tm, d), BF16)],
        compiler_params=_cparams("parallel", "arbitrary"), name="modulated_matmul",
    )(h, g, mods, mods, mods, mods, w)


def _mm_res_kernel(x_ref, w_ref, h_ref, gtc_ref, gt_ref, o_ref, *, n_ctx, tiles_per_batch):
    gate = _row_gate(gtc_ref, gt_ref, h_ref.shape, pl.program_id(0) % tiles_per_batch, n_ctx)
    o_ref[...] = h_ref[...] + gate * _dot(x_ref[...], w_ref[...])


def _matmul_residual(x, w, h, mods, k_gate, n_batch, t, n_ctx):
    n, d = h.shape
    kdim = x.shape[1]
    tm = ROW_TILE
    tpb = t // tm
    kern = functools.partial(_mm_res_kernel, n_ctx=n_ctx, tiles_per_batch=tpb)
    return pl.pallas_call(
        kern, grid=(n // tm,),
        in_specs=[pl.BlockSpec((tm, kdim), lambda i: (i, 0)),
                  pl.BlockSpec((kdim, d), lambda i: (0, 0)),
                  pl.BlockSpec((tm, d), lambda i: (i, 0))] + _stream_specs(d, n_batch, tpb, k_gate, 1),
        out_specs=pl.BlockSpec((tm, d), lambda i: (i, 0)),
        out_shape=jax.ShapeDtypeStruct((n, d), F32),
        compiler_params=_cparams("parallel"), name="matmul_residual",
    )(x, w, h, mods, mods)


def _hg_tables():
    c = HG_CHUNK
    t = np.arange(c)[:, None]
    u = np.arange(c)[None, :]
    mats = [(u <= t)]
    for k in range(1, HG_LEVELS + 1):
        m = ((t >> k) << k) + (1 << (k - 1)) - 1
        upper = t > m
        mats.append(np.where(upper, (u > m) & (u <= t), (u > t) & (u <= m)))
    mats.append(u > t)
    fwd = np.concatenate(mats, axis=0).astype(np.float32)
    bwd = fwd.reshape(-1, c, c)[:, ::-1, ::-1].reshape(-1, c)
    x = t ^ u
    lvl = np.where(t > u, np.floor(np.log2(np.maximum(x, 1))).astype(np.int32) + 1, -1)
    lvl = np.where(t == u, 0, lvl).astype(np.int32)
    return np.stack([fwd, bwd]), np.stack([lvl, lvl.T])


def _hg_chunk(d, t0, st, q_ref, z_ref, v_ref, lb_ref, m_ref, lvl_ref):
    c = HG_CHUNK
    lb = lb_ref[d:d + 1, :]
    log_lb = jnp.log(jnp.maximum(lb, LB_MIN))
    q = _silu(q_ref[pl.ds(t0, c), :])
    z = z_ref[pl.ds(t0, c), :]
    v = v_ref[pl.ds(t0, c), :].astype(BF16)
    k = (1.0 - lb) * _sigmoid(-z)
    b = jnp.log1p(-lb) + _log_sigmoid(z)
    logf = jnp.maximum(log_lb, b) + jnp.log1p(jnp.exp(-jnp.abs(log_lb - b)))
    g_hi, g_lo = _split2(logf)
    mm = m_ref[d]
    e = jnp.exp(_dot(mm, g_hi) + _dot(mm, g_lo))
    lvl = lvl_ref[d]
    att = jnp.where(lvl == 0, _dot_nt(q.astype(BF16), k.astype(BF16)), 0.0)
    for lev in range(1, HG_LEVELS + 1):
        el = e[lev * c:(lev + 1) * c]
        att = jnp.where(lvl == lev, _dot_nt((q * el).astype(BF16), (k * el).astype(BF16)), att)
    e0 = e[0:c]
    o = _dot(att.astype(BF16), v) + _dot_nt((q * e0).astype(BF16), st.astype(BF16))
    k_last = (k * e[(HG_LEVELS + 1) * c:]).astype(BF16)
    tot_row = c - 1 if d == 0 else 0
    return o, st * e0[tot_row:tot_row + 1, :] + _dot_tn(v, k_last)


def _hg_kernel(q_ref, ff_ref, fb_ref, v_ref, lb_ref, m_ref, lvl_ref, o_ref, ob_ref, *, n_ctx_chunks, n_chunks):
    c = HG_CHUNK

    def step(i, carry):
        st_f, st_b = carry
        cb = jnp.where(i < n_ctx_chunks, n_ctx_chunks - 1 - i, n_chunks - 1 - (i - n_ctx_chunks))
        tf = pl.multiple_of(i * c, c)
        tb = pl.multiple_of(cb * c, c)
        o_f, st_f = _hg_chunk(0, tf, st_f, q_ref, ff_ref, v_ref, lb_ref, m_ref, lvl_ref)
        o_b, st_b = _hg_chunk(1, tb, st_b, q_ref, fb_ref, v_ref, lb_ref, m_ref, lvl_ref)
        o_ref[pl.ds(tf, c), :] = o_f
        ob_ref[pl.ds(tb, c), :] = o_b
        return st_f, st_b

    zero = jnp.zeros((c, c), F32)
    lax.fori_loop(0, n_chunks, step, (zero, zero), unroll=2)
    o_ref[...] += ob_ref[...]


def _hgrn2(p3, lb, n_ctx):
    bsz, t, _ = p3.shape
    width = lb.shape[1]
    heads = width // HG_HEAD_DIM
    mats, lvl = _hg_tables()
    c = HG_CHUNK

    def col(group):
        return pl.BlockSpec((None, t, c), lambda b, h: (b, 0, group * heads + h))

    kern = functools.partial(_hg_kernel, n_ctx_chunks=n_ctx // c, n_chunks=t // c)
    return pl.pallas_call(
        kern, grid=(bsz, heads),
        in_specs=[col(0), col(1), col(2), col(3),
                  pl.BlockSpec((2, c), lambda b, h: (0, h)),
                  pl.BlockSpec(mats.shape, lambda b, h: (0, 0, 0)),
                  pl.BlockSpec(lvl.shape, lambda b, h: (0, 0, 0))],
        out_specs=pl.BlockSpec((None, t, c), lambda b, h: (b, 0, h)),
        out_shape=jax.ShapeDtypeStruct((bsz, t, width), F32),
        scratch_shapes=[pltpu.VMEM((t, c), F32)],
        compiler_params=_cparams("parallel", "parallel"), name="hgrn2",
    )(p3, p3, p3, p3, lb, jnp.asarray(mats, BF16), jnp.asarray(lvl))


PAD = 8


def _rg_kernel(x_ref, cw_ref, cb_ref, wa_ref, ba_ref, wx_ref, bx_ref, lam_ref, o_ref,
               xpad_ref, xc_ref, *, n_ctx):
    t, w = x_ref.shape
    n_lat = t - n_ctx
    lat0 = n_ctx + 2 * PAD
    xpad_ref[...] = jnp.zeros(xpad_ref.shape, F32)
    xpad_ref[PAD:PAD + n_ctx, :] = x_ref[0:n_ctx, :]
    xpad_ref[lat0:lat0 + n_lat, :] = x_ref[n_ctx:t, :]
    left = RG_CONV // 2
    for base, dst, n in ((PAD, 0, n_ctx), (lat0, n_ctx, n_lat)):
        acc = cb_ref[...] + cw_ref[0:1, :] * xpad_ref[base - left:base - left + n, :]
        for j in range(1, RG_CONV):
            acc = acc + cw_ref[j:j + 1, :] * xpad_ref[base - left + j:base - left + j + n, :]
        xc_ref[dst:dst + n, :] = acc

    blk = RG_SCAN_BLOCK
    n_blocks = t // blk
    n_ctx_blocks = n_ctx // blk
    rows = lax.broadcasted_iota(jnp.int32, (blk, w), 0)
    for d in range(2):
        log_sig_lam = _log_sigmoid(lam_ref[d:d + 1, :])

        def step(i, hin, d=d, log_sig_lam=log_sig_lam):
            if d == 0:
                bi = i
            else:
                bi = jnp.where(i < n_ctx_blocks, n_ctx_blocks - 1 - i, n_blocks - 1 - (i - n_ctx_blocks))
            t0 = pl.multiple_of(bi * blk, blk)
            xc = xc_ref[pl.ds(t0, blk), :]
            xb = xc.astype(BF16)
            r = _sigmoid(_dot(xb, wa_ref[d]) + ba_ref[d:d + 1, :])
            ig = _sigmoid(_dot(xb, wx_ref[d]) + bx_ref[d:d + 1, :])
            log_a = RG_C * r * log_sig_lam
            a = jnp.exp(log_a)
            b = jnp.sqrt(jnp.maximum(-jnp.tanh(log_a) * (a * a + 1.0), 0.0)) * (ig * xc)
            s = 1
            while s < blk:
                if d == 0:
                    keep = rows >= s
                    a_sh = pltpu.roll(a, s, 0)
                    b_sh = pltpu.roll(b, s, 0)
                else:
                    keep = rows < blk - s
                    a_sh = pltpu.roll(a, blk - s, 0)
                    b_sh = pltpu.roll(b, blk - s, 0)
                b = jnp.where(keep, a * b_sh, 0.0) + b
                a = jnp.where(keep, a * a_sh, a)
                s *= 2
            h = a * hin + b
            if d == 0:
                o_ref[pl.ds(t0, blk), :] = h
                return h[blk - 1:blk, :]
            o_ref[pl.ds(t0, blk), :] += h
            return h[0:1, :]

        lax.fori_loop(0, n_blocks, step, jnp.zeros((1, w), F32))


def _blockdiag_dense(w):
    nd, nb, k, _ = w.shape
    eye = jnp.eye(nb, dtype=w.dtype)
    return jnp.einsum('dnij,nm->dnimj', w, eye).reshape(nd, nb * k, nb * k)


def _rglru(p3, col_block, conv_w, conv_b, wa, ba, wx, bx, lam, n_ctx):
    bsz, t, _ = p3.shape
    w = conv_w.shape[1]
    full = lambda shape: pl.BlockSpec(shape, lambda b: (0,) * len(shape))
    kern = functools.partial(_rg_kernel, n_ctx=n_ctx)
    return pl.pallas_call(
        kern, grid=(bsz,),
        in_specs=[pl.BlockSpec((None, t, w), lambda b: (b, 0, col_block)),
                  full((RG_CONV, w)), full((1, w)),
                  full((2, w, w)), full((2, w)), full((2, w, w)), full((2, w)), full((2, w))],
        out_specs=pl.BlockSpec((None, t, w), lambda b: (b, 0, 0)),
        out_shape=jax.ShapeDtypeStruct((bsz, t, w), F32),
        scratch_shapes=[pltpu.VMEM((t + 3 * PAD, w), F32), pltpu.VMEM((t, w), F32)],
        compiler_params=_cparams("parallel"), name="rglru",
    )(p3, conv_w, conv_b.reshape(1, w), _blockdiag_dense(wa).astype(BF16), ba,
      _blockdiag_dense(wx).astype(BF16), bx, lam)


def _merge_kernel(hg_ref, rg_ref, gate_ref, y_ref, og_ref, w_ref, h_ref, gtc_ref, gt_ref, o_ref,
                  *, n_ctx, tiles_per_batch):
    hg = hg_ref[...]
    half = hg.shape[1]
    hn = hg * lax.rsqrt(jnp.mean(hg * hg, axis=-1, keepdims=True) + EPS) * og_ref[...]
    a = (hn * _silu(gate_ref[...])).astype(BF16)
    b = (rg_ref[...] * _gelu_tanh(y_ref[...])).astype(BF16)
    o = _dot(a, w_ref[0:half, :]) + _dot(b, w_ref[half:, :])
    gate = _row_gate(gtc_ref, gt_ref, h_ref.shape, pl.program_id(0) % tiles_per_batch, n_ctx)
    o_ref[...] = h_ref[...] + gate * o


def _merge_out(hg, rg, p, onorm_g, w_out, h, mods, n_batch, t, n_ctx):
    n, d = h.shape
    half = hg.shape[1]
    tm = ROW_TILE
    tpb = t // tm
    blk = lambda c: pl.BlockSpec((tm, half), lambda i: (i, c))
    kern = functools.partial(_merge_kernel, n_ctx=n_ctx, tiles_per_batch=tpb)
    return pl.pallas_call(
        kern, grid=(n // tm,),
        in_specs=[blk(0), blk(0), blk(4), blk(6),
                  pl.BlockSpec((1, half), lambda i: (0, 0)),
                  pl.BlockSpec((d, d), lambda i: (0, 0)),
                  pl.BlockSpec((tm, d), lambda i: (i, 0))] + _stream_specs(d, n_batch, tpb, 2, 1),
        out_specs=pl.BlockSpec((tm, d), lambda i: (i, 0)),
        out_shape=jax.ShapeDtypeStruct((n, d), F32),
        compiler_params=_cparams("parallel"), name="merge_out",
    )(hg, rg, p, p, onorm_g.reshape(1, half), w_out, h, mods, mods)


def _na_bias_table(rpb):
    q = np.arange(GRID_W)[:, None]
    kc = np.arange(GRID_W)[None, :]
    wstart = np.clip(q - NA_KC // 2, 0, GRID_W - NA_KC)
    mask = (kc >= wstart) & (kc < wstart + NA_KC)
    coff = np.clip(kc - q + NA_KC - 1, 0, 2 * NA_KC - 2)
    sel = (coff[None] == np.arange(2 * NA_KC - 1)[:, None, None]).astype(np.float32)
    toe = jnp.einsum('hdj,jqk->hqdk', rpb.astype(F32), sel, precision=lax.Precision.HIGHEST)
    toe = jnp.where(mask[None, :, None, :], toe, NEG_BIG)
    h = toe.shape[0]
    tab = jnp.stack([toe[:, :, off:off + NA_KR].reshape(h // 2, 2 * GRID_W, NA_KR * GRID_W)
                     for off in range(NA_KR)], axis=1)
    return tab


def _stack_heads(x, hd):
    lane = lax.broadcasted_iota(jnp.int32, x.shape, 1)
    zero = jnp.zeros_like(x)
    return jnp.concatenate([jnp.where(lane < hd, x, zero), jnp.where(lane >= hd, x, zero)], axis=0)


def _unstack_heads(o2, hd):
    n = o2.shape[0] // 2
    lane = lax.broadcasted_iota(jnp.int32, (n, o2.shape[1]), 1)
    return jnp.where(lane < hd, o2[:n], o2[n:])


def _na_kernel(q_ref, k_ref, v_ref, qg_ref, kg_ref, bias_ref, o_ref, qn_ref, kn_ref, *, n_ctx, scale):
    t, w = q_ref.shape
    hd = w // 2
    rows_grid = (t - n_ctx) // GRID_W
    r_i = lax.broadcasted_iota(jnp.int32, (w, w), 0)
    c_i = lax.broadcasted_iota(jnp.int32, (w, w), 1)
    seg = jnp.where((r_i // hd) == (c_i // hd), 1.0 / hd, 0.0).astype(BF16)

    def normed(x, g):
        s_hi, s_lo = _split2(x * x)
        ms = _dot(s_hi, seg) + _dot(s_lo, seg)
        return x * lax.rsqrt(ms + EPS) * g

    qn_ref[...] = (normed(q_ref[...].astype(F32), qg_ref[...]) * scale).astype(BF16)
    kn_ref[...] = normed(k_ref[...].astype(F32), kg_ref[...]).astype(BF16)
    vb_ref = v_ref

    kc = kn_ref[0:n_ctx, :]
    vc = vb_ref[0:n_ctx, :]

    s = _dot_nt(_stack_heads(qn_ref[0:n_ctx, :], hd), kc)
    p = jnp.exp(s - jnp.max(s, axis=-1, keepdims=True))
    o2 = _dot(p.astype(BF16), vc) / jnp.sum(p, axis=-1, keepdims=True)
    o_ref[0:n_ctx, :] = _unstack_heads(o2, hd).astype(o_ref.dtype)

    kr = NA_KR
    n_win = kr * GRID_W

    def row(r, carry):
        rs = jnp.clip(r - kr // 2, 0, rows_grid - kr)
        off = rs - r + (kr - 1)
        q0 = pl.multiple_of(n_ctx + r * GRID_W, GRID_W)
        k0 = pl.multiple_of(n_ctx + rs * GRID_W, GRID_W)
        q2 = _stack_heads(qn_ref[pl.ds(q0, GRID_W), :], hd)
        kw = kn_ref[pl.ds(k0, n_win), :]
        vw = vb_ref[pl.ds(k0, n_win), :]
        s_loc = _dot_nt(q2, kw) + bias_ref[off]
        s_ctx = _dot_nt(q2, kc)
        m = jnp.maximum(jnp.max(s_loc, axis=-1, keepdims=True), jnp.max(s_ctx, axis=-1, keepdims=True))
        p_loc = jnp.exp(s_loc - m)
        p_ctx = jnp.exp(s_ctx - m)
        den = jnp.sum(p_loc, axis=-1, keepdims=True) + jnp.sum(p_ctx, axis=-1, keepdims=True)
        o2 = (_dot(p_loc.astype(BF16), vw) + _dot(p_ctx.astype(BF16), vc)) / den
        o_ref[pl.ds(q0, GRID_W), :] = _unstack_heads(o2, hd).astype(o_ref.dtype)
        return carry

    lax.fori_loop(0, rows_grid, row, 0, unroll=4)


def _na_attention(qkv3, q_g, k_g, rpb, n_ctx):
    bsz, t, d3 = qkv3.shape
    d = d3 // 3
    hd = d // NA_HEADS
    w = 2 * hd
    pairs = d // w
    assert w == LANES and (t - n_ctx) // GRID_W >= NA_KR
    bias = _na_bias_table(rpb)
    kern = functools.partial(_na_kernel, n_ctx=n_ctx, scale=hd ** -0.5)

    def col(group):
        return pl.BlockSpec((None, t, w), lambda hp, b: (b, 0, group * pairs + hp))

    gain = lambda g: jnp.tile(g.astype(F32), 2).reshape(1, w)
    return pl.pallas_call(
        kern, grid=(pairs, bsz),
        in_specs=[col(0), col(1), col(2),
                  pl.BlockSpec((1, w), lambda hp, b: (0, 0)),
                  pl.BlockSpec((1, w), lambda hp, b: (0, 0)),
                  pl.BlockSpec((None,) + bias.shape[1:], lambda hp, b: (hp, 0, 0, 0))],
        out_specs=pl.BlockSpec((None, t, w), lambda hp, b: (b, 0, hp)),
        out_shape=jax.ShapeDtypeStruct((bsz, t, d), BF16),
        scratch_shapes=[pltpu.VMEM((t, w), BF16)] * 2,
        compiler_params=_cparams("parallel", "parallel"), name="na_attention",
    )(qkv3, qkv3, qkv3, gain(q_g), gain(k_g), bias)


FF_TILE = 1408


def _swiglu_kernel(h_ref, g_ref, shc_ref, sh_ref, scc_ref, sc_ref, w1_ref, w3_ref, w2_ref, gtc_ref, gt_ref,
                   o_ref, xb_ref, acc_ref, *, n_ctx, tiles_per_batch):
    f = pl.program_id(1)
    tile_in_batch = pl.program_id(0) % tiles_per_batch

    @pl.when(f == 0)
    def _():
        xb_ref[...] = _modulated_rows(h_ref, g_ref, shc_ref, sh_ref, scc_ref, sc_ref,
                                      tile_in_batch, n_ctx).astype(BF16)
        acc_ref[...] = jnp.zeros_like(acc_ref)

    x = xb_ref[...]
    hid = _silu(_dot(x, w1_ref[...])) * _dot(x, w3_ref[...])
    acc_ref[...] += _dot(hid.astype(BF16), w2_ref[...])

    @pl.when(f == pl.num_programs(1) - 1)
    def _():
        gate = _row_gate(gtc_ref, gt_ref, acc_ref.shape, tile_in_batch, n_ctx)
        o_ref[...] = h_ref[...] + gate * acc_ref[...]


def _swiglu_residual(h, g, w1, w3, w2, mods, n_batch, t, n_ctx):
    n, d = h.shape
    dff = w1.shape[1]
    tm, tf = ROW_TILE, FF_TILE
    tpb = t // tm
    kern = functools.partial(_swiglu_kernel, n_ctx=n_ctx, tiles_per_batch=tpb)
    return pl.pallas_call(
        kern, grid=(n // tm, dff // tf),
        in_specs=[pl.BlockSpec((tm, d), lambda i, f: (i, 0)),
                  pl.BlockSpec((1, d), lambda i, f: (0, 0))]
        + _stream_specs(d, n_batch, tpb, 3, 2) + _stream_specs(d, n_batch, tpb, 4, 2)
        + [pl.BlockSpec((d, tf), lambda i, f: (0, f)),
           pl.BlockSpec((d, tf), lambda i, f: (0, f)),
           pl.BlockSpec((tf, d), lambda i, f: (f, 0))] + _stream_specs(d, n_batch, tpb, 5, 2),
        out_specs=pl.BlockSpec((tm, d), lambda i, f: (i, 0)),
        out_shape=jax.ShapeDtypeStruct((n, d), F32),
        scratch_shapes=[pltpu.VMEM((tm, d), BF16), pltpu.VMEM((tm, d), F32)],
        compiler_params=_cparams("parallel", "arbitrary"), name="swiglu",
    )(h, g, mods, mods, mods, mods, w1, w3, w2, mods, mods)


MOE_ROWS = 512


def _route_tables(route, tm, n_exp):
    n = route.shape[0]
    e_pair = route[:, 0:2].astype(jnp.int32).reshape(-1)
    onehot = (e_pair[:, None] == jnp.arange(n_exp, dtype=jnp.int32)[None, :]).astype(jnp.int32)
    csum = jnp.cumsum(onehot, axis=0)
    rank = jnp.sum((csum - onehot) * onehot, axis=1)
    counts = csum[-1]
    padded = ((counts + tm - 1) // tm) * tm
    ends = jnp.cumsum(padded)
    slot = jnp.sum(onehot * (ends - padded)[None, :], axis=1) + rank
    p_slots = 2 * n + n_exp * tm
    tok_of_slot = jnp.zeros((p_slots,), jnp.int32).at[slot].set(jnp.arange(2 * n, dtype=jnp.int32) // 2)
    n_tiles = p_slots // tm
    n_used = ends[-1] // tm
    tile = jnp.arange(n_tiles, dtype=jnp.int32)
    te = jnp.sum(((tile * tm)[:, None] >= ends[None, :]).astype(jnp.int32), axis=1)
    te = jnp.where(tile < n_used, te, te[n_used - 1])
    return slot.reshape(n, 2), tok_of_slot, te, n_used.reshape(1)


def _gather_rows(table, idx):
    p = idx.shape[0]
    m = table.shape[1]
    mesh = plsc.VectorSubcoreMesh(core_axis_name="c", subcore_axis_name="s")

    @functools.partial(pl.kernel, out_type=jax.ShapeDtypeStruct((p, m), table.dtype), mesh=mesh)
    def gather(table_hbm, idx_hbm, out_hbm):
        def body(idx_vmem, out_vmem):
            pltpu.sync_copy(table_hbm.at[idx_vmem.at[0]], out_vmem)

        pltpu.emit_pipeline(
            body, grid=(p // SC_WINDOW,),
            in_specs=[pl.BlockSpec((1, SC_WINDOW), lambda i: (0, i))],
            out_specs=[pl.BlockSpec((SC_WINDOW, m), lambda i: (i, 0))],
            core_axis_name=("c", "s"), dimension_semantics=(pltpu.PARALLEL,),
        )(idx_hbm, out_hbm)

    return gather(table, idx.reshape(1, p))


def _gather_pieces(pieces, idx):
    npc, v, m = pieces.shape
    offs = (jnp.arange(npc, dtype=jnp.int32) * v).reshape((npc,) + (1,) * idx.ndim)
    flat = (idx[None] + offs).reshape(-1)
    return _gather_rows(pieces.reshape(npc * v, m), flat).reshape((npc,) + idx.shape + (m,))


def _moe_ffn_kernel(te_ref, nu_ref, x_ref, w1_ref, w3_ref, w2_ref, y_ref, xb_ref, acc_ref):
    i = pl.program_id(0)
    f = pl.program_id(1)

    @pl.when(i < nu_ref[0])
    def _():
        @pl.when(f == 0)
        def _():
            xb_ref[...] = _load_unpacked(x_ref).astype(BF16)
            acc_ref[...] = jnp.zeros_like(acc_ref)

        x = xb_ref[...]
        hid = _silu(_dot(x, w1_ref[...])) * _dot(x, w3_ref[...])
        acc_ref[...] += _dot(hid.astype(BF16), w2_ref[...])

        @pl.when(f == pl.num_programs(1) - 1)
        def _():
            _store_pieces(y_ref, _pack_halves(acc_ref[...]))


def _moe_ffn(xs, te, n_used, w1, w3, w2):
    npc, p, m = xs.shape
    d = 2 * npc * m
    dff = w1.shape[2]
    tm, tf = MOE_ROWS, FF_TILE
    nf = dff // tf

    def row(i, f, te_ref, nu_ref):
        return (0, jnp.minimum(i, nu_ref[0] - 1), 0)

    def fcol(i, f, nu_ref):
        return jnp.where(i < nu_ref[0], f, nf - 1)

    grid_spec = pltpu.PrefetchScalarGridSpec(
        num_scalar_prefetch=2, grid=(p // tm, nf),
        in_specs=[pl.BlockSpec((npc, tm, m), row),
                  pl.BlockSpec((None, d, tf), lambda i, f, te_ref, nu_ref: (te_ref[i], 0, fcol(i, f, nu_ref))),
                  pl.BlockSpec((None, d, tf), lambda i, f, te_ref, nu_ref: (te_ref[i], 0, fcol(i, f, nu_ref))),
                  pl.BlockSpec((None, tf, d), lambda i, f, te_ref, nu_ref: (te_ref[i], fcol(i, f, nu_ref), 0))],
        out_specs=pl.BlockSpec((npc, tm, m), row),
        scratch_shapes=[pltpu.VMEM((tm, d), BF16), pltpu.VMEM((tm, d), F32)])
    return pl.pallas_call(
        _moe_ffn_kernel, grid_spec=grid_spec,
        out_shape=jax.ShapeDtypeStruct((npc, p, m), jnp.int32),
        compiler_params=_cparams("arbitrary", "arbitrary"), name="moe_ffn",
    )(te, n_used, xs, w1, w3, w2)


def _moe_combine_kernel(ya_ref, yb_ref, route_ref, h_ref, gt_ref, o_ref, *, skip_first_of):
    def body():
        r = route_ref[...]
        y = r[:, 2:3] * _load_unpacked(ya_ref) + r[:, 3:4] * _load_unpacked(yb_ref)
        o_ref[...] = h_ref[...] + gt_ref[...] * y

    if skip_first_of is None:
        body()
    else:
        pl.when(pl.program_id(0) % skip_first_of != 0)(body)


def _moe_combine(yg, route, h, mods, n_batch, tiles_per_batch, latent_only):
    n, d = h.shape
    npc, _, _, m = yg.shape
    tm = TOKEN_TILE
    if latent_only:
        def out_idx(i):
            b, j = i // tiles_per_batch, i % tiles_per_batch
            return (b * (tiles_per_batch - 1) + jnp.maximum(j, 1) - 1, 0)
        n_out = n // tiles_per_batch * (tiles_per_batch - 1)
    else:
        out_idx = lambda i: (i, 0)
        n_out = n
    kern = functools.partial(_moe_combine_kernel, skip_first_of=tiles_per_batch if latent_only else None)
    return pl.pallas_call(
        kern, grid=(n // tm,),
        in_specs=[pl.BlockSpec((npc, None, tm, m), lambda i: (0, 0, i, 0)),
                  pl.BlockSpec((npc, None, tm, m), lambda i: (0, 1, i, 0)),
                  pl.BlockSpec((tm, LANES), lambda i: (i, 0)),
                  pl.BlockSpec((tm, d), lambda i: (i, 0)),
                  pl.BlockSpec((None, None, 1, d), _mod_row_spec(5, tiles_per_batch, n_batch))],
        out_specs=pl.BlockSpec((tm, d), out_idx),
        out_shape=jax.ShapeDtypeStruct((n_out, d), F32),
        compiler_params=_cparams("arbitrary"), name="moe_combine",
    )(yg, yg, route, h, mods)


def _moe_residual(u_pieces, route, w1, w3, w2, h, mods, n_batch, tiles_per_batch, latent_only):
    slot, tok_of_slot, te, n_used = _route_tables(route, MOE_ROWS, w1.shape[0])
    xs = _gather_pieces(u_pieces, tok_of_slot)
    ys = _moe_ffn(xs, te, n_used, w1, w3, w2)
    yg = _gather_pieces(ys, slot.T)
    return _moe_combine(yg, route, h, mods, n_batch, tiles_per_batch, latent_only)


def kernel(x, c, ctx, c_ctx, w_mod, b_mod, norm_mix_g, norm_ffn_g, ev_w_in, ev_w_out, hg_lb_logits, hg_onorm_g, rg_conv_w, rg_conv_b, rg_wa, rg_ba, rg_wx, rg_bx, rg_lambda, na_w_qkv, na_w_o, na_q_g, na_k_g, na_rpb, ffn_w1, ffn_w3, ffn_w2, moe_router, moe_w1, moe_w3, moe_w2):
    bsz, seq, d = x.shape
    n_ctx = ctx.shape[1]
    t = n_ctx + seq
    depth = w_mod.shape[0]
    tpb = t // TOKEN_TILE
    assert n_ctx % TOKEN_TILE == 0 and seq % TOKEN_TILE == 0 and t % ROW_TILE == 0 and n_ctx <= ROW_TILE

    rows = -(-(bsz + 1) // 8) * 8
    cond = jnp.zeros((rows, d), F32).at[:bsz].set(c).at[bsz].set(c_ctx)
    mods_all = _adaln(cond, w_mod, b_mod).reshape(depth, rows, N_MOD, 1, d)

    lb_p = jax.nn.softmax(hg_lb_logits.astype(F32), axis=0)
    lb_all = jnp.cumsum(lb_p, axis=0) - lb_p[0]

    h = jnp.concatenate([ctx, x], axis=1).reshape(bsz * t, d)
    for layer in range(depth):
        j = layer // 2
        mods = mods_all[layer]
        g_mix = norm_mix_g[layer].reshape(1, d)
        if layer % 2 == 0:
            w_in = ev_w_in[j].astype(BF16)
            p = _modulated_matmul(h, g_mix, mods, w_in, F32, w_in.shape[1] // 2, bsz, t, n_ctx)
            p3 = p.reshape(bsz, t, -1)
            hg = _hgrn2(p3, lb_all[j], n_ctx)
            half = hg.shape[-1]
            rg = _rglru(p3, 5 * half // (d - half), rg_conv_w[j], rg_conv_b[j], rg_wa[j], rg_ba[j],
                        rg_wx[j], rg_bx[j], rg_lambda[j], n_ctx)
            h = _merge_out(hg.reshape(bsz * t, half), rg.reshape(bsz * t, d - half), p, hg_onorm_g[j],
                           ev_w_out[j].astype(BF16), h, mods, bsz, t, n_ctx)
        else:
            qkv = _modulated_matmul(h, g_mix, mods, na_w_qkv[j].astype(BF16), BF16, 3 * d // 2, bsz, t, n_ctx)
            o = _na_attention(qkv.reshape(bsz, t, 3 * d), na_q_g[j], na_k_g[j], na_rpb[j], n_ctx)
            h = _matmul_residual(o.reshape(bsz * t, d), na_w_o[j].astype(BF16), h, mods, 2, bsz, t, n_ctx)
        if layer % 2 == 0:
            h = _swiglu_residual(h, norm_ffn_g[layer].reshape(1, d), ffn_w1[j].astype(BF16),
                                 ffn_w3[j].astype(BF16), ffn_w2[j].astype(BF16), mods, bsz, t, n_ctx)
        else:
            u, route = _modulate_router(h, norm_ffn_g[layer].reshape(1, d), mods, 3, bsz, tpb, moe_router[j])
            h = _moe_residual(u, route, moe_w1[j].astype(BF16), moe_w3[j].astype(BF16),
                              moe_w2[j].astype(BF16), h, mods, bsz, tpb, layer == depth - 1)
    if depth % 2 == 0:
        return h.reshape(bsz, seq, d)
    return h.reshape(bsz, t, d)[:, n_ctx:]
```

```python
import functools

import numpy as np
import jax
import jax.numpy as jnp
from jax import lax
from jax.experimental import pallas as pl
from jax.experimental.pallas import tpu as pltpu
from jax.experimental.pallas import tpu_sc as plsc

F32 = jnp.float32
BF16 = jnp.bfloat16

EPS = 1e-6
NEG_BIG = -1e9
LB_MIN = 1e-30
N_MOD = 6
GRID_W = 64
HG_HEAD_DIM = 128
RG_BLOCKS = 8
RG_CONV = 4
RG_C = 8.0
NA_HEADS = 16
NA_KR = 8
NA_KC = 16
N_EXPERTS = 8

LANES = 128
SC_WINDOW = 128
SC_PIECES = 2
TOKEN_TILE = 256
HG_CHUNK = 128
HG_LEVELS = 7
RG_SCAN_BLOCK = 256
RG_GROUP = 8
VMEM_LIMIT = 56 * 1024 * 1024


def _cparams(*sem):
    return pltpu.CompilerParams(dimension_semantics=sem, vmem_limit_bytes=VMEM_LIMIT)


def _dot(a, b):
    return jnp.dot(a, b, preferred_element_type=F32)


def _dot_nt(a, b):
    return lax.dot_general(a, b, (((1,), (1,)), ((), ())), preferred_element_type=F32)


def _dot_tn(a, b):
    return lax.dot_general(a, b, (((0,), (0,)), ((), ())), preferred_element_type=F32)


def _split2(x):
    hi = x.astype(BF16)
    lo = (x - hi.astype(F32)).astype(BF16)
    return hi, lo


def _split3(x):
    hi = x.astype(BF16)
    r = x - hi.astype(F32)
    mid = r.astype(BF16)
    lo = (r - mid.astype(F32)).astype(BF16)
    return hi, mid, lo


def _sigmoid(x):
    return 1.0 / (1.0 + jnp.exp(-x))


def _silu(x):
    return x * _sigmoid(x)


def _log_sigmoid(x):
    return jnp.minimum(x, 0.0) - jnp.log1p(jnp.exp(-jnp.abs(x)))


def _gelu_tanh(x):
    return 0.5 * x * (1.0 + jnp.tanh(np.sqrt(2.0 / np.pi) * (x + 0.044715 * (x * x * x))))


def _adaln_kernel(c_ref, w_ref, b_ref, o_ref):
    a_hi, a_lo = _split2(_silu(c_ref[...]))
    w_hi, w_lo = _split2(w_ref[...])
    o_ref[...] = _dot(a_hi, w_hi) + _dot(a_hi, w_lo) + _dot(a_lo, w_hi) + b_ref[...]


def _adaln(cond, w_mod, b_mod):
    depth, d, n = w_mod.shape
    r = cond.shape[0]
    tn = 1536
    return pl.pallas_call(
        _adaln_kernel,
        grid=(depth, n // tn),
        in_specs=[pl.BlockSpec((r, d), lambda l, j: (0, 0)),
                  pl.BlockSpec((None, d, tn), lambda l, j: (l, 0, j)),
                  pl.BlockSpec((None, 1, tn), lambda l, j: (l, 0, j))],
        out_specs=pl.BlockSpec((None, r, tn), lambda l, j: (l, 0, j)),
        out_shape=jax.ShapeDtypeStruct((depth, r, n), F32),
        compiler_params=_cparams("parallel", "parallel"),
        name="adaln",
    )(cond, w_mod, b_mod.reshape(depth, 1, n))


def _mod_row_spec(k, tiles_per_batch, n_batch):
    def idx(i):
        row = jnp.where(i % tiles_per_batch == 0, n_batch, i // tiles_per_batch)
        return (row, k, 0, 0)
    return idx


def _modulated(h_ref, g_ref, sh_ref, sc_ref):
    x = h_ref[...]
    y = x * lax.rsqrt(jnp.mean(x * x, axis=-1, keepdims=True) + EPS)
    return (y * g_ref[...]) * (1.0 + sc_ref[...]) + sh_ref[...]


def _pack_halves(x):
    m = x.shape[1] // 2
    lo = lax.bitcast_convert_type(x[:, :m].astype(BF16).astype(F32), jnp.int32)
    hi = lax.bitcast_convert_type(x[:, m:].astype(BF16).astype(F32), jnp.int32)
    return lax.shift_right_logical(lo, 16) | (hi & jnp.int32(-65536))


def _unpack_halves(w):
    lo = lax.bitcast_convert_type(lax.shift_left(w, 16), F32)
    hi = lax.bitcast_convert_type(w & jnp.int32(-65536), F32)
    return lo, hi


def _store_pieces(ref, w):
    m = w.shape[1] // SC_PIECES
    for j in range(SC_PIECES):
        ref[j] = w[:, j * m:(j + 1) * m]


def _load_unpacked(ref):
    parts = [_unpack_halves(ref[j]) for j in range(SC_PIECES)]
    return jnp.concatenate([lo for lo, _ in parts] + [hi for _, hi in parts], axis=1)


def _modulate_router_kernel(h_ref, g_ref, sh_ref, sc_ref, wr_ref, u_ref, route_ref):
    u = _modulated(h_ref, g_ref, sh_ref, sc_ref)
    _store_pieces(u_ref, _pack_halves(u))
    u0, u1, u2 = _split3(u)
    w0, w1, w2 = _split3(wr_ref[...])
    logits = (_dot(u0, w0) + _dot(u0, w1) + _dot(u1, w0)
              + _dot(u1, w1) + _dot(u0, w2) + _dot(u2, w0))
    lane = lax.broadcasted_iota(jnp.int32, logits.shape, 1).astype(F32)
    lg = jnp.where(lane < N_EXPERTS, logits, -jnp.inf)
    m1 = jnp.max(lg, axis=-1, keepdims=True)
    i1 = jnp.min(jnp.where(lg == m1, lane, float(LANES)), axis=-1, keepdims=True)
    lg2 = jnp.where(lane == i1, -jnp.inf, lg)
    m2 = jnp.max(lg2, axis=-1, keepdims=True)
    i2 = jnp.min(jnp.where(lg2 == m2, lane, float(LANES)), axis=-1, keepdims=True)
    e2 = jnp.exp(m2 - m1)
    den = 1.0 + e2
    route_ref[...] = jnp.where(lane == 0, i1, jnp.where(lane == 1, i2, jnp.where(lane == 2, 1.0 / den, e2 / den)))


def _modulate_router(h, g, mods, k_shift, n_batch, tiles_per_batch, w_router):
    n, d = h.shape
    tm = TOKEN_TILE
    row = pl.BlockSpec((tm, d), lambda i: (i, 0))
    vec = pl.BlockSpec((1, d), lambda i: (0, 0))

    def mod_spec(k):
        return pl.BlockSpec((None, None, 1, d), _mod_row_spec(k, tiles_per_batch, n_batch))

    in_specs = [row, vec, mod_spec(k_shift), mod_spec(k_shift + 1)]
    wr = jnp.zeros((d, LANES), F32).at[:, :N_EXPERTS].set(w_router)
    return pl.pallas_call(
        _modulate_router_kernel, grid=(n // tm,),
        in_specs=in_specs + [pl.BlockSpec((d, LANES), lambda i: (0, 0))],
        out_specs=[pl.BlockSpec((SC_PIECES, tm, d // 2 // SC_PIECES), lambda i: (0, i, 0)),
                   pl.BlockSpec((tm, LANES), lambda i: (i, 0))],
        out_shape=[jax.ShapeDtypeStruct((SC_PIECES, n, d // 2 // SC_PIECES), jnp.int32),
                   jax.ShapeDtypeStruct((n, LANES), F32)],
        compiler_params=_cparams("parallel"), name="modulate_router",
    )(h, g, mods, mods, wr)


ROW_TILE = 768


def _row_gate(ctx_ref, own_ref, shape, tile_in_batch, n_ctx):
    rows = lax.broadcasted_iota(jnp.int32, shape, 0)
    is_ctx = (rows < n_ctx) & (tile_in_batch == 0)
    return jnp.where(is_ctx, ctx_ref[...], own_ref[...])


def _modulated_rows(h_ref, g_ref, shc_ref, sh_ref, scc_ref, sc_ref, tile_in_batch, n_ctx):
    x = h_ref[...]
    y = x * lax.rsqrt(jnp.mean(x * x, axis=-1, keepdims=True) + EPS)
    shift = _row_gate(shc_ref, sh_ref, x.shape, tile_in_batch, n_ctx)
    scale = _row_gate(scc_ref, sc_ref, x.shape, tile_in_batch, n_ctx)
    return (y * g_ref[...]) * (1.0 + scale) + shift


def _stream_specs(d, n_batch, tiles_per_batch, k, nargs):
    if nargs == 1:
        ctx = lambda i: (n_batch, k, 0, 0)
        own = lambda i: (i // tiles_per_batch, k, 0, 0)
    else:
        ctx = lambda i, f: (n_batch, k, 0, 0)
        own = lambda i, f: (i // tiles_per_batch, k, 0, 0)
    return [pl.BlockSpec((None, None, 1, d), ctx), pl.BlockSpec((None, None, 1, d), own)]


def _mod_mm_kernel(h_ref, g_ref, shc_ref, sh_ref, scc_ref, sc_ref, w_ref, o_ref, xb_ref,
                   *, n_ctx, tiles_per_batch):
    @pl.when(pl.program_id(1) == 0)
    def _():
        xb_ref[...] = _modulated_rows(h_ref, g_ref, shc_ref, sh_ref, scc_ref, sc_ref,
                                      pl.program_id(0) % tiles_per_batch, n_ctx).astype(BF16)

    o_ref[...] = _dot(xb_ref[...], w_ref[...]).astype(o_ref.dtype)


def _modulated_matmul(h, g, mods, w, j, out_dtype, tn, n_batch, t, n_ctx):
    n, d = h.shape
    nout = w.shape[2]
    tm = ROW_TILE
    tpb = t // tm
    kern = functools.partial(_mod_mm_kernel, n_ctx=n_ctx, tiles_per_batch=tpb)
    return pl.pallas_call(
        kern, grid=(n // tm, nout // tn),
        in_specs=[pl.BlockSpec((tm, d), lambda i, c: (i, 0)),
                  pl.BlockSpec((1, d), lambda i, c: (0, 0))]
        + _stream_specs(d, n_batch, tpb, 0, 2) + _stream_specs(d, n_batch, tpb, 1, 2)
        + [pl.BlockSpec((None, d, tn), lambda i, c: (j, 0, c))],
        out_specs=pl.BlockSpec((tm, tn), lambda i, c: (i, c)),
        out_shape=jax.ShapeDtypeStruct((n, nout), out_dtype),
        scratch_shapes=[pltpu.VMEM((tm, d), BF16)],
        compiler_params=_cparams("parallel", "arbitrary"), name="modulated_matmul",
    )(h, g, mods, mods, mods, mods, w)


def _mm_res_kernel(x_ref, w_ref, h_ref, gtc_ref, gt_ref, o_ref, *, n_ctx, tiles_per_batch):
    gate = _row_gate(gtc_ref, gt_ref, h_ref.shape, pl.program_id(0) % tiles_per_batch, n_ctx)
    o_ref[...] = h_ref[...] + gate * _dot(x_ref[...], w_ref[...])


def _matmul_residual(x, w, j, h, mods, k_gate, n_batch, t, n_ctx):
    n, d = h.shape
    kdim = x.shape[1]
    tm = ROW_TILE
    tpb = t // tm
    kern = functools.partial(_mm_res_kernel, n_ctx=n_ctx, tiles_per_batch=tpb)
    return pl.pallas_call(
        kern, grid=(n // tm,),
        in_specs=[pl.BlockSpec((tm, kdim), lambda i: (i, 0)),
                  pl.BlockSpec((None, kdim, d), lambda i: (j, 0, 0)),
                  pl.BlockSpec((tm, d), lambda i: (i, 0))] + _stream_specs(d, n_batch, tpb, k_gate, 1),
        out_specs=pl.BlockSpec((tm, d), lambda i: (i, 0)),
        out_shape=jax.ShapeDtypeStruct((n, d), F32),
        compiler_params=_cparams("parallel"), name="matmul_residual",
    )(x, w, h, mods, mods)


def _hg_tables():
    c = HG_CHUNK
    t = np.arange(c)[:, None]
    u = np.arange(c)[None, :]
    mats = [(u <= t), (u > t)]
    for k in range(1, HG_LEVELS + 1):
        m = ((t >> k) << k) + (1 << (k - 1)) - 1
        upper = t > m
        mats.append(np.where(upper, (u > m) & (u <= t), (u > t) & (u <= m)))
    fwd = np.concatenate(mats, axis=0).astype(np.float32)
    bwd = fwd.reshape(-1, c, c)[:, ::-1, ::-1].reshape(-1, c)
    x = t ^ u
    lvl = np.where(t > u, np.floor(np.log2(np.maximum(x, 1))).astype(np.int32) + 1, -1)
    lvl = np.where(t == u, 0, lvl).astype(np.int32)
    return np.stack([fwd, bwd]), np.stack([lvl, lvl.T])


def _hg_chunk(d, t0, st, q_ref, z_ref, v_ref, lb_ref, m_ref, lvl_ref):
    c = HG_CHUNK
    lb = lb_ref[d:d + 1, :]
    log_lb = jnp.log(jnp.maximum(lb, LB_MIN))
    q = _silu(q_ref[pl.ds(t0, c), :])
    z = z_ref[pl.ds(t0, c), :]
    v = v_ref[pl.ds(t0, c), :].astype(BF16)
    k = (1.0 - lb) * _sigmoid(-z)
    b = jnp.log1p(-lb) + _log_sigmoid(z)
    logf = jnp.maximum(log_lb, b) + jnp.log1p(jnp.exp(-jnp.abs(log_lb - b)))
    g_hi, g_lo = _split2(logf)
    m_ends = m_ref[d, 0:2 * c, :]
    e_ends = jnp.exp(_dot(m_ends, g_hi) + _dot(m_ends, g_lo))
    e_lvl = jnp.exp(_dot(m_ref[d, 2 * c:, :], g_hi)).astype(BF16)
    qb = q.astype(BF16)
    kb = k.astype(BF16)
    lvl = lvl_ref[d]
    att = jnp.where(lvl == 0, _dot_nt(qb, kb), 0.0)
    for lev in range(1, HG_LEVELS + 1):
        el = e_lvl[(lev - 1) * c:lev * c]
        att = jnp.where(lvl == lev, _dot_nt(qb * el, kb * el), att)
    e0 = e_ends[0:c]
    o = _dot(att.astype(BF16), v) + _dot_nt((q * e0).astype(BF16), st.astype(BF16))
    k_last = (k * e_ends[c:]).astype(BF16)
    tot_row = c - 1 if d == 0 else 0
    return o, st * e0[tot_row:tot_row + 1, :] + _dot_tn(v, k_last)


def _hg_kernel(q_ref, ff_ref, fb_ref, v_ref, lb_ref, m_ref, lvl_ref, o_ref, ob_ref, *, n_ctx_chunks, n_chunks):
    c = HG_CHUNK

    def step(i, carry):
        st_f, st_b = carry
        cb = jnp.where(i < n_ctx_chunks, n_ctx_chunks - 1 - i, n_chunks - 1 - (i - n_ctx_chunks))
        tf = pl.multiple_of(i * c, c)
        tb = pl.multiple_of(cb * c, c)
        o_f, st_f = _hg_chunk(0, tf, st_f, q_ref, ff_ref, v_ref, lb_ref, m_ref, lvl_ref)
        o_b, st_b = _hg_chunk(1, tb, st_b, q_ref, fb_ref, v_ref, lb_ref, m_ref, lvl_ref)
        o_ref[pl.ds(tf, c), :] = o_f
        ob_ref[pl.ds(tb, c), :] = o_b
        return st_f, st_b

    zero = jnp.zeros((c, c), F32)
    lax.fori_loop(0, n_chunks, step, (zero, zero), unroll=2)
    o_ref[...] += ob_ref[...]


def _hgrn2(p3, lb, n_ctx):
    bsz, t, _ = p3.shape
    width = lb.shape[1]
    heads = width // HG_HEAD_DIM
    mats, lvl = _hg_tables()
    c = HG_CHUNK

    def col(group):
        return pl.BlockSpec((None, t, c), lambda b, h: (b, 0, group * heads + h))

    kern = functools.partial(_hg_kernel, n_ctx_chunks=n_ctx // c, n_chunks=t // c)
    return pl.pallas_call(
        kern, grid=(bsz, heads),
        in_specs=[col(0), col(1), col(2), col(3),
                  pl.BlockSpec((2, c), lambda b, h: (0, h)),
                  pl.BlockSpec(mats.shape, lambda b, h: (0, 0, 0)),
                  pl.BlockSpec(lvl.shape, lambda b, h: (0, 0, 0))],
        out_specs=pl.BlockSpec((None, t, c), lambda b, h: (b, 0, h)),
        out_shape=jax.ShapeDtypeStruct((bsz, t, width), F32),
        scratch_shapes=[pltpu.VMEM((t, c), F32)],
        compiler_params=_cparams("parallel", "parallel"), name="hgrn2",
    )(p3, p3, p3, p3, lb, jnp.asarray(mats, BF16), jnp.asarray(lvl))


PAD = 8


def _rg_kernel(x_ref, cw_ref, cb_ref, wa_ref, ba_ref, wx_ref, bx_ref, lam_ref, o_ref,
               xpad_ref, xc_ref, ob_ref, *, n_ctx):
    t, w = x_ref.shape
    n_lat = t - n_ctx
    lat0 = n_ctx + 2 * PAD
    xpad_ref[...] = jnp.zeros(xpad_ref.shape, F32)
    xpad_ref[PAD:PAD + n_ctx, :] = x_ref[0:n_ctx, :]
    xpad_ref[lat0:lat0 + n_lat, :] = x_ref[n_ctx:t, :]
    left = RG_CONV // 2
    for base, dst, n in ((PAD, 0, n_ctx), (lat0, n_ctx, n_lat)):
        acc = cb_ref[...] + cw_ref[0:1, :] * xpad_ref[base - left:base - left + n, :]
        for j in range(1, RG_CONV):
            acc = acc + cw_ref[j:j + 1, :] * xpad_ref[base - left + j:base - left + j + n, :]
        xc_ref[dst:dst + n, :] = acc

    blk = RG_SCAN_BLOCK
    n_blocks = t // blk
    n_ctx_blocks = n_ctx // blk
    sub = lax.broadcasted_iota(jnp.int32, (blk // RG_GROUP, RG_GROUP, w), 1)
    log_sig_lam = _log_sigmoid(lam_ref[...])

    def scan_block(d, t0, hin, out_ref):
        xc = xc_ref[pl.ds(t0, blk), :]
        xb = xc.astype(BF16)
        r = _sigmoid(_dot(xb, wa_ref[d]) + ba_ref[d:d + 1, :])
        ig = _sigmoid(_dot(xb, wx_ref[d]) + bx_ref[d:d + 1, :])
        log_a = RG_C * r * log_sig_lam[d:d + 1, :]
        a = jnp.exp(log_a)
        b = jnp.sqrt(jnp.maximum(-jnp.tanh(log_a) * (a * a + 1.0), 0.0)) * (ig * xc)
        n_groups = blk // RG_GROUP
        a = a.reshape(n_groups, RG_GROUP, w)
        b = b.reshape(n_groups, RG_GROUP, w)
        s = 1
        while s < RG_GROUP:
            shift = s if d == 0 else RG_GROUP - s
            keep = (sub >= s) if d == 0 else (sub < RG_GROUP - s)
            b = jnp.where(keep, a * pltpu.roll(b, shift, 1), 0.0) + b
            a = jnp.where(keep, a * pltpu.roll(a, shift, 1), a)
            s *= 2
        for gi in range(n_groups):
            gj = gi if d == 0 else n_groups - 1 - gi
            hg = a[gj] * hin + b[gj]
            out_ref[pl.ds(t0 + gj * RG_GROUP, RG_GROUP), :] = hg
            hin = hg[RG_GROUP - 1:RG_GROUP, :] if d == 0 else hg[0:1, :]
        return hin

    def step(i, carry):
        h_f, h_b = carry
        bb = jnp.where(i < n_ctx_blocks, n_ctx_blocks - 1 - i, n_blocks - 1 - (i - n_ctx_blocks))
        h_f = scan_block(0, pl.multiple_of(i * blk, blk), h_f, o_ref)
        h_b = scan_block(1, pl.multiple_of(bb * blk, blk), h_b, ob_ref)
        return h_f, h_b

    zero = jnp.zeros((1, w), F32)
    lax.fori_loop(0, n_blocks, step, (zero, zero))
    o_ref[...] += ob_ref[...]


def _blockdiag_dense(w):
    nd, nb, k, _ = w.shape
    eye = jnp.eye(nb, dtype=w.dtype)
    return jnp.einsum('dnij,nm->dnimj', w, eye).reshape(nd, nb * k, nb * k)


def _rglru(p3, col_block, conv_w, conv_b, wa, ba, wx, bx, lam, n_ctx):
    bsz, t, _ = p3.shape
    w = conv_w.shape[1]
    full = lambda shape: pl.BlockSpec(shape, lambda b: (0,) * len(shape))
    kern = functools.partial(_rg_kernel, n_ctx=n_ctx)
    return pl.pallas_call(
        kern, grid=(bsz,),
        in_specs=[pl.BlockSpec((None, t, w), lambda b: (b, 0, col_block)),
                  full((RG_CONV, w)), full((1, w)),
                  full((2, w, w)), full((2, w)), full((2, w, w)), full((2, w)), full((2, w))],
        out_specs=pl.BlockSpec((None, t, w), lambda b: (b, 0, 0)),
        out_shape=jax.ShapeDtypeStruct((bsz, t, w), F32),
        scratch_shapes=[pltpu.VMEM((t + 3 * PAD, w), F32), pltpu.VMEM((t, w), F32), pltpu.VMEM((t, w), F32)],
        compiler_params=_cparams("parallel"), name="rglru",
    )(p3, conv_w, conv_b.reshape(1, w), _blockdiag_dense(wa).astype(BF16), ba,
      _blockdiag_dense(wx).astype(BF16), bx, lam)


def _merge_kernel(hg_ref, rg_ref, gate_ref, y_ref, og_ref, w_ref, h_ref, gtc_ref, gt_ref, o_ref,
                  *, n_ctx, tiles_per_batch):
    hg = hg_ref[...]
    half = hg.shape[1]
    hn = hg * lax.rsqrt(jnp.mean(hg * hg, axis=-1, keepdims=True) + EPS) * og_ref[...]
    a = (hn * _silu(gate_ref[...])).astype(BF16)
    b = (rg_ref[...] * _gelu_tanh(y_ref[...])).astype(BF16)
    o = _dot(a, w_ref[0:half, :]) + _dot(b, w_ref[half:, :])
    gate = _row_gate(gtc_ref, gt_ref, h_ref.shape, pl.program_id(0) % tiles_per_batch, n_ctx)
    o_ref[...] = h_ref[...] + gate * o


def _merge_out(hg, rg, p, onorm_g, w_out, j, h, mods, n_batch, t, n_ctx):
    n, d = h.shape
    half = hg.shape[1]
    tm = ROW_TILE
    tpb = t // tm
    blk = lambda c: pl.BlockSpec((tm, half), lambda i: (i, c))
    kern = functools.partial(_merge_kernel, n_ctx=n_ctx, tiles_per_batch=tpb)
    return pl.pallas_call(
        kern, grid=(n // tm,),
        in_specs=[blk(0), blk(0), blk(4), blk(6),
                  pl.BlockSpec((1, half), lambda i: (0, 0)),
                  pl.BlockSpec((None, d, d), lambda i: (j, 0, 0)),
                  pl.BlockSpec((tm, d), lambda i: (i, 0))] + _stream_specs(d, n_batch, tpb, 2, 1),
        out_specs=pl.BlockSpec((tm, d), lambda i: (i, 0)),
        out_shape=jax.ShapeDtypeStruct((n, d), F32),
        compiler_params=_cparams("parallel"), name="merge_out",
    )(hg, rg, p, p, onorm_g.reshape(1, half), w_out, h, mods, mods)


def _na_bias_table(rpb):
    q = np.arange(GRID_W)[:, None]
    kc = np.arange(GRID_W)[None, :]
    wstart = np.clip(q - NA_KC // 2, 0, GRID_W - NA_KC)
    mask = (kc >= wstart) & (kc < wstart + NA_KC)
    coff = np.clip(kc - q + NA_KC - 1, 0, 2 * NA_KC - 2)
    sel = (coff[None] == np.arange(2 * NA_KC - 1)[:, None, None]).astype(np.float32)
    toe = jnp.einsum('hdj,jqk->hqdk', rpb.astype(F32), sel, precision=lax.Precision.HIGHEST)
    toe = jnp.where(mask[None, :, None, :], toe, NEG_BIG)
    h = toe.shape[0]
    tab = jnp.stack([toe[:, :, off:off + NA_KR].reshape(h // 2, 2 * GRID_W, NA_KR * GRID_W)
                     for off in range(NA_KR)], axis=1)
    return tab


def _stack_heads(x, hd):
    lane = lax.broadcasted_iota(jnp.int32, x.shape, 1)
    zero = jnp.zeros_like(x)
    return jnp.concatenate([jnp.where(lane < hd, x, zero), jnp.where(lane >= hd, x, zero)], axis=0)


def _unstack_heads(o2, hd):
    n = o2.shape[0] // 2
    lane = lax.broadcasted_iota(jnp.int32, (n, o2.shape[1]), 1)
    return jnp.where(lane < hd, o2[:n], o2[n:])


def _na_kernel(q_ref, k_ref, v_ref, qg_ref, kg_ref, bias_ref, o_ref, qn_ref, kn_ref, *, n_ctx, scale):
    t, w = q_ref.shape
    hd = w // 2
    rows_grid = (t - n_ctx) // GRID_W
    r_i = lax.broadcasted_iota(jnp.int32, (w, w), 0)
    c_i = lax.broadcasted_iota(jnp.int32, (w, w), 1)
    seg = jnp.where((r_i // hd) == (c_i // hd), 1.0 / hd, 0.0).astype(BF16)

    def normed(x, g):
        s_hi, s_lo = _split2(x * x)
        ms = _dot(s_hi, seg) + _dot(s_lo, seg)
        return x * lax.rsqrt(ms + EPS) * g

    qn_ref[...] = (normed(q_ref[...].astype(F32), qg_ref[...]) * scale).astype(BF16)
    kn_ref[...] = normed(k_ref[...].astype(F32), kg_ref[...]).astype(BF16)
    vb_ref = v_ref

    kc = kn_ref[0:n_ctx, :]
    vc = vb_ref[0:n_ctx, :]

    s = _dot_nt(_stack_heads(qn_ref[0:n_ctx, :], hd), kc)
    p = jnp.exp(s - jnp.max(s, axis=-1, keepdims=True))
    o2 = _dot(p.astype(BF16), vc) / jnp.sum(p, axis=-1, keepdims=True)
    o_ref[0:n_ctx, :] = _unstack_heads(o2, hd).astype(o_ref.dtype)

    kr = NA_KR
    n_win = kr * GRID_W

    def row(r, carry):
        rs = jnp.clip(r - kr // 2, 0, rows_grid - kr)
        off = rs - r + (kr - 1)
        q0 = pl.multiple_of(n_ctx + r * GRID_W, GRID_W)
        k0 = pl.multiple_of(n_ctx + rs * GRID_W, GRID_W)
        q2 = _stack_heads(qn_ref[pl.ds(q0, GRID_W), :], hd)
        kw = kn_ref[pl.ds(k0, n_win), :]
        vw = vb_ref[pl.ds(k0, n_win), :]
        s_loc = _dot_nt(q2, kw) + bias_ref[off]
        s_ctx = _dot_nt(q2, kc)
        m = jnp.maximum(jnp.max(s_loc, axis=-1, keepdims=True), jnp.max(s_ctx, axis=-1, keepdims=True))
        p_loc = jnp.exp(s_loc - m)
        p_ctx = jnp.exp(s_ctx - m)
        den = jnp.sum(p_loc, axis=-1, keepdims=True) + jnp.sum(p_ctx, axis=-1, keepdims=True)
        o2 = (_dot(p_loc.astype(BF16), vw) + _dot(p_ctx.astype(BF16), vc)) / den
        o_ref[pl.ds(q0, GRID_W), :] = _unstack_heads(o2, hd).astype(o_ref.dtype)
        return carry

    lax.fori_loop(0, rows_grid, row, 0, unroll=4)


def _na_attention(qkv3, q_g, k_g, rpb, n_ctx):
    bsz, t, d3 = qkv3.shape
    d = d3 // 3
    hd = d // NA_HEADS
    w = 2 * hd
    pairs = d // w
    assert w == LANES and (t - n_ctx) // GRID_W >= NA_KR
    bias = _na_bias_table(rpb)
    kern = functools.partial(_na_kernel, n_ctx=n_ctx, scale=hd ** -0.5)

    def col(group):
        return pl.BlockSpec((None, t, w), lambda hp, b: (b, 0, group * pairs + hp))

    gain = lambda g: jnp.tile(g.astype(F32), 2).reshape(1, w)
    return pl.pallas_call(
        kern, grid=(pairs, bsz),
        in_specs=[col(0), col(1), col(2),
                  pl.BlockSpec((1, w), lambda hp, b: (0, 0)),
                  pl.BlockSpec((1, w), lambda hp, b: (0, 0)),
                  pl.BlockSpec((None,) + bias.shape[1:], lambda hp, b: (hp, 0, 0, 0))],
        out_specs=pl.BlockSpec((None, t, w), lambda hp, b: (b, 0, hp)),
        out_shape=jax.ShapeDtypeStruct((bsz, t, d), BF16),
        scratch_shapes=[pltpu.VMEM((t, w), BF16)] * 2,
        compiler_params=_cparams("parallel", "parallel"), name="na_attention",
    )(qkv3, qkv3, qkv3, gain(q_g), gain(k_g), bias)


FF_TILE = 1408


def _swiglu_kernel(h_ref, g_ref, shc_ref, sh_ref, scc_ref, sc_ref, w1_ref, w3_ref, w2_ref, gtc_ref, gt_ref,
                   o_ref, xb_ref, acc_ref, *, n_ctx, tiles_per_batch):
    f = pl.program_id(1)
    tile_in_batch = pl.program_id(0) % tiles_per_batch

    @pl.when(f == 0)
    def _():
        xb_ref[...] = _modulated_rows(h_ref, g_ref, shc_ref, sh_ref, scc_ref, sc_ref,
                                      tile_in_batch, n_ctx).astype(BF16)
        acc_ref[...] = jnp.zeros_like(acc_ref)

    x = xb_ref[...]
    hid = _silu(_dot(x, w1_ref[...])) * _dot(x, w3_ref[...])
    acc_ref[...] += _dot(hid.astype(BF16), w2_ref[...])

    @pl.when(f == pl.num_programs(1) - 1)
    def _():
        gate = _row_gate(gtc_ref, gt_ref, acc_ref.shape, tile_in_batch, n_ctx)
        o_ref[...] = h_ref[...] + gate * acc_ref[...]


def _swiglu_residual(h, g, w1, w3, w2, j, mods, n_batch, t, n_ctx):
    n, d = h.shape
    dff = w1.shape[2]
    tm, tf = ROW_TILE, FF_TILE
    tpb = t // tm
    kern = functools.partial(_swiglu_kernel, n_ctx=n_ctx, tiles_per_batch=tpb)
    return pl.pallas_call(
        kern, grid=(n // tm, dff // tf),
        in_specs=[pl.BlockSpec((tm, d), lambda i, f: (i, 0)),
                  pl.BlockSpec((1, d), lambda i, f: (0, 0))]
        + _stream_specs(d, n_batch, tpb, 3, 2) + _stream_specs(d, n_batch, tpb, 4, 2)
        + [pl.BlockSpec((None, d, tf), lambda i, f: (j, 0, f)),
           pl.BlockSpec((None, d, tf), lambda i, f: (j, 0, f)),
           pl.BlockSpec((None, tf, d), lambda i, f: (j, f, 0))] + _stream_specs(d, n_batch, tpb, 5, 2),
        out_specs=pl.BlockSpec((tm, d), lambda i, f: (i, 0)),
        out_shape=jax.ShapeDtypeStruct((n, d), F32),
        scratch_shapes=[pltpu.VMEM((tm, d), BF16), pltpu.VMEM((tm, d), F32)],
        compiler_params=_cparams("parallel", "arbitrary"), name="swiglu",
    )(h, g, mods, mods, mods, mods, w1, w3, w2, mods, mods)


MOE_ROWS = 512


def _route_tables(route, tok, tm, n_exp):
    n = route.shape[0]
    e_pair = route[:, 0:2].astype(jnp.int32).reshape(-1)
    onehot = (e_pair[:, None] == jnp.arange(n_exp, dtype=jnp.int32)[None, :]).astype(jnp.int32)
    csum = jnp.cumsum(onehot, axis=0)
    rank = jnp.sum((csum - onehot) * onehot, axis=1)
    counts = csum[-1]
    padded = ((counts + tm - 1) // tm) * tm
    ends = jnp.cumsum(padded)
    slot = jnp.sum(onehot * (ends - padded)[None, :], axis=1) + rank
    p_slots = 2 * n + n_exp * tm
    tok_of_slot = jnp.zeros((p_slots,), jnp.int32).at[slot].set(jnp.repeat(tok, 2))
    n_tiles = p_slots // tm
    n_used = ends[-1] // tm
    tile = jnp.arange(n_tiles, dtype=jnp.int32)
    te = jnp.sum(((tile * tm)[:, None] >= ends[None, :]).astype(jnp.int32), axis=1)
    te = jnp.where(tile < n_used, te, te[n_used - 1])
    return slot.reshape(n, 2), tok_of_slot, te, n_used.reshape(1)


def _gather_rows(table, idx):
    p = idx.shape[0]
    m = table.shape[1]
    mesh = plsc.VectorSubcoreMesh(core_axis_name="c", subcore_axis_name="s")

    @functools.partial(pl.kernel, out_type=jax.ShapeDtypeStruct((p, m), table.dtype), mesh=mesh)
    def gather(table_hbm, idx_hbm, out_hbm):
        def body(idx_vmem, out_vmem):
            pltpu.sync_copy(table_hbm.at[idx_vmem.at[0]], out_vmem)

        pltpu.emit_pipeline(
            body, grid=(p // SC_WINDOW,),
            in_specs=[pl.BlockSpec((1, SC_WINDOW), lambda i: (0, i))],
            out_specs=[pl.BlockSpec((SC_WINDOW, m), lambda i: (i, 0))],
            core_axis_name=("c", "s"), dimension_semantics=(pltpu.PARALLEL,),
        )(idx_hbm, out_hbm)

    return gather(table, idx.reshape(1, p))


def _gather_pieces(pieces, idx):
    npc, v, m = pieces.shape
    offs = (jnp.arange(npc, dtype=jnp.int32) * v).reshape((npc,) + (1,) * idx.ndim)
    flat = (idx[None] + offs).reshape(-1)
    return _gather_rows(pieces.reshape(npc * v, m), flat).reshape((npc,) + idx.shape + (m,))


def _moe_ffn_kernel(te_ref, nu_ref, x_ref, w1_ref, w3_ref, w2_ref, y_ref, xb_ref, acc_ref):
    i = pl.program_id(0)
    f = pl.program_id(1)

    @pl.when(i < nu_ref[0])
    def _():
        @pl.when(f == 0)
        def _():
            xb_ref[...] = _load_unpacked(x_ref).astype(BF16)
            acc_ref[...] = jnp.zeros_like(acc_ref)

        x = xb_ref[...]
        hid = _silu(_dot(x, w1_ref[...])) * _dot(x, w3_ref[...])
        acc_ref[...] += _dot(hid.astype(BF16), w2_ref[...])

        @pl.when(f == pl.num_programs(1) - 1)
        def _():
            _store_pieces(y_ref, _pack_halves(acc_ref[...]))


def _moe_ffn(xs, te, n_used, w1, w3, w2, j):
    npc, p, m = xs.shape
    d = 2 * npc * m
    dff = w1.shape[3]
    tm, tf = MOE_ROWS, FF_TILE
    nf = dff // tf

    def row(i, f, te_ref, nu_ref):
        return (0, jnp.minimum(i, nu_ref[0] - 1), 0)

    def fcol(i, f, nu_ref):
        return jnp.where(i < nu_ref[0], f, nf - 1)

    grid_spec = pltpu.PrefetchScalarGridSpec(
        num_scalar_prefetch=2, grid=(p // tm, nf),
        in_specs=[pl.BlockSpec((npc, tm, m), row),
                  pl.BlockSpec((None, None, d, tf), lambda i, f, te_ref, nu_ref: (j, te_ref[i], 0, fcol(i, f, nu_ref))),
                  pl.BlockSpec((None, None, d, tf), lambda i, f, te_ref, nu_ref: (j, te_ref[i], 0, fcol(i, f, nu_ref))),
                  pl.BlockSpec((None, None, tf, d), lambda i, f, te_ref, nu_ref: (j, te_ref[i], fcol(i, f, nu_ref), 0))],
        out_specs=pl.BlockSpec((npc, tm, m), row),
        scratch_shapes=[pltpu.VMEM((tm, d), BF16), pltpu.VMEM((tm, d), F32)])
    return pl.pallas_call(
        _moe_ffn_kernel, grid_spec=grid_spec,
        out_shape=jax.ShapeDtypeStruct((npc, p, m), jnp.int32),
        compiler_params=_cparams("arbitrary", "arbitrary"), name="moe_ffn",
    )(te, n_used, xs, w1, w3, w2)


def _moe_combine_kernel(ya_ref, yb_ref, route_ref, h_ref, gt_ref, o_ref):
    r = route_ref[...]
    y = r[:, 2:3] * _load_unpacked(ya_ref) + r[:, 3:4] * _load_unpacked(yb_ref)
    o_ref[...] = h_ref[...] + gt_ref[...] * y


def _moe_combine(yg, route, h, mods, n_batch, tiles_per_batch, ctx_tiles):
    d = h.shape[1]
    npc, _, r, m = yg.shape
    tm = TOKEN_TILE
    per = tiles_per_batch - ctx_tiles
    h_idx = lambda i: ((i // per) * tiles_per_batch + ctx_tiles + i % per, 0)
    if ctx_tiles == 0:
        mod_idx = _mod_row_spec(5, tiles_per_batch, n_batch)
    else:
        mod_idx = lambda i: (i // per, 5, 0, 0)
    return pl.pallas_call(
        _moe_combine_kernel, grid=(r // tm,),
        in_specs=[pl.BlockSpec((npc, None, tm, m), lambda i: (0, 0, i, 0)),
                  pl.BlockSpec((npc, None, tm, m), lambda i: (0, 1, i, 0)),
                  pl.BlockSpec((tm, LANES), lambda i: (i, 0)),
                  pl.BlockSpec((tm, d), h_idx),
                  pl.BlockSpec((None, None, 1, d), mod_idx)],
        out_specs=pl.BlockSpec((tm, d), lambda i: (i, 0)),
        out_shape=jax.ShapeDtypeStruct((r, d), F32),
        compiler_params=_cparams("parallel"), name="moe_combine",
    )(yg, yg, route, h, mods)


def _moe_residual(u_pieces, route, w1, w3, w2, j, h, mods, n_batch, t, n_ctx, latent_only):
    n = h.shape[0]
    tok = jnp.arange(n, dtype=jnp.int32)
    if latent_only:
        tok = tok.reshape(n_batch, t)[:, n_ctx:].reshape(-1)
        route = route.reshape(n_batch, t, LANES)[:, n_ctx:].reshape(-1, LANES)
    slot, tok_of_slot, te, n_used = _route_tables(route, tok, MOE_ROWS, w1.shape[1])
    xs = _gather_pieces(u_pieces, tok_of_slot)
    ys = _moe_ffn(xs, te, n_used, w1, w3, w2, j)
    yg = _gather_pieces(ys, slot.T)
    ctx_tiles = n_ctx // TOKEN_TILE if latent_only else 0
    return _moe_combine(yg, route, h, mods, n_batch, t // TOKEN_TILE, ctx_tiles)


def kernel(x, c, ctx, c_ctx, w_mod, b_mod, norm_mix_g, norm_ffn_g, ev_w_in, ev_w_out, hg_lb_logits, hg_onorm_g, rg_conv_w, rg_conv_b, rg_wa, rg_ba, rg_wx, rg_bx, rg_lambda, na_w_qkv, na_w_o, na_q_g, na_k_g, na_rpb, ffn_w1, ffn_w3, ffn_w2, moe_router, moe_w1, moe_w3, moe_w2):
    bsz, seq, d = x.shape
    n_ctx = ctx.shape[1]
    t = n_ctx + seq
    depth = w_mod.shape[0]
    tpb = t // TOKEN_TILE
    assert n_ctx % TOKEN_TILE == 0 and seq % TOKEN_TILE == 0 and t % ROW_TILE == 0 and n_ctx <= ROW_TILE

    rows = -(-(bsz + 1) // 8) * 8
    cond = jnp.zeros((rows, d), F32).at[:bsz].set(c).at[bsz].set(c_ctx)
    mods_all = _adaln(cond, w_mod, b_mod).reshape(depth, rows, N_MOD, 1, d)

    lb_p = jax.nn.softmax(hg_lb_logits.astype(F32), axis=0)
    lb_all = jnp.cumsum(lb_p, axis=0) - lb_p[0]

    bf = lambda w: w.astype(BF16)
    w_in, w_out, w_qkv, w_o = bf(ev_w_in), bf(ev_w_out), bf(na_w_qkv), bf(na_w_o)
    f_w1, f_w3, f_w2 = bf(ffn_w1), bf(ffn_w3), bf(ffn_w2)
    m_w1, m_w3, m_w2 = bf(moe_w1), bf(moe_w3), bf(moe_w2)

    h = jnp.concatenate([ctx, x], axis=1).reshape(bsz * t, d)
    for layer in range(depth):
        j = layer // 2
        mods = mods_all[layer]
        g_mix = norm_mix_g[layer].reshape(1, d)
        if layer % 2 == 0:
            p = _modulated_matmul(h, g_mix, mods, w_in, j, F32, w_in.shape[2] // 2, bsz, t, n_ctx)
            p3 = p.reshape(bsz, t, -1)
            hg = _hgrn2(p3, lb_all[j], n_ctx)
            half = hg.shape[-1]
            rg = _rglru(p3, 5 * half // (d - half), rg_conv_w[j], rg_conv_b[j], rg_wa[j], rg_ba[j],
                        rg_wx[j], rg_bx[j], rg_lambda[j], n_ctx)
            h = _merge_out(hg.reshape(bsz * t, half), rg.reshape(bsz * t, d - half), p, hg_onorm_g[j],
                           w_out, j, h, mods, bsz, t, n_ctx)
        else:
            qkv = _modulated_matmul(h, g_mix, mods, w_qkv, j, BF16, 3 * d // 2, bsz, t, n_ctx)
            o = _na_attention(qkv.reshape(bsz, t, 3 * d), na_q_g[j], na_k_g[j], na_rpb[j], n_ctx)
            h = _matmul_residual(o.reshape(bsz * t, d), w_o, j, h, mods, 2, bsz, t, n_ctx)
        if layer % 2 == 0:
            h = _swiglu_residual(h, norm_ffn_g[layer].reshape(1, d), f_w1, f_w3, f_w2, j, mods, bsz, t, n_ctx)
        else:
            u, route = _modulate_router(h, norm_ffn_g[layer].reshape(1, d), mods, 3, bsz, tpb, moe_router[j])
            h = _moe_residual(u, route, m_w1, m_w3, m_w2, j, h, mods, bsz, t, n_ctx, layer == depth - 1)
    if depth % 2 == 0:
        return h.reshape(bsz, seq, d)
    return h.reshape(bsz, t, d)[:, n_ctx:]
```

```python
import functools

import numpy as np
import jax
import jax.numpy as jnp
from jax import lax
from jax.experimental import pallas as pl
from jax.experimental.pallas import tpu as pltpu
from jax.experimental.pallas import tpu_sc as plsc

F32 = jnp.float32
BF16 = jnp.bfloat16

EPS = 1e-6
NEG_BIG = -1e9
LB_MIN = 1e-30
LOG2_E = 1.4426950408889634
N_MOD = 6
GRID_W = 64
HG_HEAD_DIM = 128
RG_BLOCKS = 8
RG_CONV = 4
RG_C = 8.0
NA_HEADS = 16
NA_KR = 8
NA_KC = 16
N_EXPERTS = 8

LANES = 128
SC_WINDOW = 128
SC_PIECES = 2
TOKEN_TILE = 256
HG_CHUNK = 128
HG_LEVELS = 7
NA_SLOTS = 4
RG_SCAN_BLOCK = 256
RG_GROUP = 8
VMEM_LIMIT = 56 * 1024 * 1024


def _cparams(*sem):
    return pltpu.CompilerParams(dimension_semantics=sem, vmem_limit_bytes=VMEM_LIMIT)


def _dot(a, b):
    return jnp.dot(a, b, preferred_element_type=F32)


def _dot_nt(a, b):
    return lax.dot_general(a, b, (((1,), (1,)), ((), ())), preferred_element_type=F32)


def _dot_tn(a, b):
    return lax.dot_general(a, b, (((0,), (0,)), ((), ())), preferred_element_type=F32)


def _split2(x):
    hi = x.astype(BF16)
    lo = (x - hi.astype(F32)).astype(BF16)
    return hi, lo


def _split3(x):
    hi = x.astype(BF16)
    r = x - hi.astype(F32)
    mid = r.astype(BF16)
    lo = (r - mid.astype(F32)).astype(BF16)
    return hi, mid, lo


def _sigmoid(x):
    return 1.0 / (1.0 + jnp.exp(-x))


def _silu(x):
    return x * _sigmoid(x)


def _log_sigmoid(x):
    return jnp.minimum(x, 0.0) - jnp.log1p(jnp.exp(-jnp.abs(x)))


def _gelu_tanh(x):
    return 0.5 * x * (1.0 + jnp.tanh(np.sqrt(2.0 / np.pi) * (x + 0.044715 * (x * x * x))))


def _adaln_kernel(c_ref, w_ref, b_ref, o_ref):
    a_hi, a_lo = _split2(_silu(c_ref[...]))
    w_hi, w_lo = _split2(w_ref[...])
    o_ref[...] = _dot(a_hi, w_hi) + _dot(a_hi, w_lo) + _dot(a_lo, w_hi) + b_ref[...]


def _adaln(cond, w_mod, b_mod):
    depth, d, n = w_mod.shape
    r = cond.shape[0]
    tn = 1536
    return pl.pallas_call(
        _adaln_kernel,
        grid=(depth, n // tn),
        in_specs=[pl.BlockSpec((r, d), lambda l, j: (0, 0)),
                  pl.BlockSpec((None, d, tn), lambda l, j: (l, 0, j)),
                  pl.BlockSpec((None, 1, tn), lambda l, j: (l, 0, j))],
        out_specs=pl.BlockSpec((None, r, tn), lambda l, j: (l, 0, j)),
        out_shape=jax.ShapeDtypeStruct((depth, r, n), F32),
        compiler_params=_cparams("parallel", "parallel"),
        name="adaln",
    )(cond, w_mod, b_mod.reshape(depth, 1, n))


def _mod_row_spec(k, tiles_per_batch, n_batch):
    def idx(i):
        row = jnp.where(i % tiles_per_batch == 0, n_batch, i // tiles_per_batch)
        return (row, k, 0, 0)
    return idx


def _modulated(h_ref, g_ref, sh_ref, sc_ref):
    x = h_ref[...]
    y = x * lax.rsqrt(jnp.mean(x * x, axis=-1, keepdims=True) + EPS)
    return (y * g_ref[...]) * (1.0 + sc_ref[...]) + sh_ref[...]


def _pack_halves(x):
    m = x.shape[1] // 2
    lo = lax.bitcast_convert_type(x[:, :m].astype(BF16).astype(F32), jnp.int32)
    hi = lax.bitcast_convert_type(x[:, m:].astype(BF16).astype(F32), jnp.int32)
    return lax.shift_right_logical(lo, 16) | (hi & jnp.int32(-65536))


def _unpack_halves(w):
    lo = lax.bitcast_convert_type(lax.shift_left(w, 16), F32)
    hi = lax.bitcast_convert_type(w & jnp.int32(-65536), F32)
    return lo, hi


def _store_pieces(ref, w):
    m = w.shape[1] // SC_PIECES
    for j in range(SC_PIECES):
        ref[j] = w[:, j * m:(j + 1) * m]


def _load_unpacked(ref):
    parts = [_unpack_halves(ref[j]) for j in range(SC_PIECES)]
    return jnp.concatenate([lo for lo, _ in parts] + [hi for _, hi in parts], axis=1)


def _modulate_router_kernel(h_ref, g_ref, sh_ref, sc_ref, wr_ref, u_ref, route_ref):
    u = _modulated(h_ref, g_ref, sh_ref, sc_ref)
    _store_pieces(u_ref, _pack_halves(u))
    u0, u1, u2 = _split3(u)
    w0, w1, w2 = _split3(wr_ref[...])
    logits = (_dot(u0, w0) + _dot(u0, w1) + _dot(u1, w0)
              + _dot(u1, w1) + _dot(u0, w2) + _dot(u2, w0))
    lane = lax.broadcasted_iota(jnp.int32, logits.shape, 1).astype(F32)
    lg = jnp.where(lane < N_EXPERTS, logits, -jnp.inf)
    m1 = jnp.max(lg, axis=-1, keepdims=True)
    i1 = jnp.min(jnp.where(lg == m1, lane, float(LANES)), axis=-1, keepdims=True)
    lg2 = jnp.where(lane == i1, -jnp.inf, lg)
    m2 = jnp.max(lg2, axis=-1, keepdims=True)
    i2 = jnp.min(jnp.where(lg2 == m2, lane, float(LANES)), axis=-1, keepdims=True)
    e2 = jnp.exp(m2 - m1)
    den = 1.0 + e2
    route_ref[...] = jnp.where(lane == 0, i1, jnp.where(lane == 1, i2, jnp.where(lane == 2, 1.0 / den, e2 / den)))


def _modulate_router(h, g, mods, k_shift, n_batch, tiles_per_batch, w_router):
    n, d = h.shape
    tm = TOKEN_TILE
    row = pl.BlockSpec((tm, d), lambda i: (i, 0))
    vec = pl.BlockSpec((1, d), lambda i: (0, 0))

    def mod_spec(k):
        return pl.BlockSpec((None, None, 1, d), _mod_row_spec(k, tiles_per_batch, n_batch))

    in_specs = [row, vec, mod_spec(k_shift), mod_spec(k_shift + 1)]
    wr = jnp.zeros((d, LANES), F32).at[:, :N_EXPERTS].set(w_router)
    return pl.pallas_call(
        _modulate_router_kernel, grid=(n // tm,),
        in_specs=in_specs + [pl.BlockSpec((d, LANES), lambda i: (0, 0))],
        out_specs=[pl.BlockSpec((SC_PIECES, tm, d // 2 // SC_PIECES), lambda i: (0, i, 0)),
                   pl.BlockSpec((tm, LANES), lambda i: (i, 0))],
        out_shape=[jax.ShapeDtypeStruct((SC_PIECES, n, d // 2 // SC_PIECES), jnp.int32),
                   jax.ShapeDtypeStruct((n, LANES), F32)],
        compiler_params=_cparams("parallel"), name="modulate_router",
    )(h, g, mods, mods, wr)


ROW_TILE = 768


def _row_gate(ctx_ref, own_ref, shape, tile_in_batch, n_ctx):
    rows = lax.broadcasted_iota(jnp.int32, shape, 0)
    is_ctx = (rows < n_ctx) & (tile_in_batch == 0)
    return jnp.where(is_ctx, ctx_ref[...], own_ref[...])


def _modulated_rows(h_ref, g_ref, shc_ref, sh_ref, scc_ref, sc_ref, tile_in_batch, n_ctx):
    x = h_ref[...]
    y = x * lax.rsqrt(jnp.mean(x * x, axis=-1, keepdims=True) + EPS)
    shift = _row_gate(shc_ref, sh_ref, x.shape, tile_in_batch, n_ctx)
    scale = _row_gate(scc_ref, sc_ref, x.shape, tile_in_batch, n_ctx)
    return (y * g_ref[...]) * (1.0 + scale) + shift


def _stream_specs(d, n_batch, tiles_per_batch, k, nargs):
    if nargs == 1:
        ctx = lambda i: (n_batch, k, 0, 0)
        own = lambda i: (i // tiles_per_batch, k, 0, 0)
    else:
        ctx = lambda i, f: (n_batch, k, 0, 0)
        own = lambda i, f: (i // tiles_per_batch, k, 0, 0)
    return [pl.BlockSpec((None, None, 1, d), ctx), pl.BlockSpec((None, None, 1, d), own)]


def _mod_mm_kernel(h_ref, g_ref, shc_ref, sh_ref, scc_ref, sc_ref, w_ref, o_ref, xb_ref,
                   *, n_ctx, tiles_per_batch):
    @pl.when(pl.program_id(1) == 0)
    def _():
        xb_ref[...] = _modulated_rows(h_ref, g_ref, shc_ref, sh_ref, scc_ref, sc_ref,
                                      pl.program_id(0) % tiles_per_batch, n_ctx).astype(BF16)

    o_ref[...] = _dot(xb_ref[...], w_ref[...]).astype(o_ref.dtype)


def _modulated_matmul(h, g, mods, w, j, out_dtype, tn, n_batch, t, n_ctx):
    n, d = h.shape
    nout = w.shape[2]
    tm = ROW_TILE
    tpb = t // tm
    kern = functools.partial(_mod_mm_kernel, n_ctx=n_ctx, tiles_per_batch=tpb)
    return pl.pallas_call(
        kern, grid=(n // tm, nout // tn),
        in_specs=[pl.BlockSpec((tm, d), lambda i, c: (i, 0)),
                  pl.BlockSpec((1, d), lambda i, c: (0, 0))]
        + _stream_specs(d, n_batch, tpb, 0, 2) + _stream_specs(d, n_batch, tpb, 1, 2)
        + [pl.BlockSpec((None, d, tn), lambda i, c: (j, 0, c))],
        out_specs=pl.BlockSpec((tm, tn), lambda i, c: (i, c)),
        out_shape=jax.ShapeDtypeStruct((n, nout), out_dtype),
        scratch_shapes=[pltpu.VMEM((tm, d), BF16)],
        compiler_params=_cparams("parallel", "arbitrary"), name="modulated_matmul",
    )(h, g, mods, mods, mods, mods, w)


def _mm_res_kernel(x_ref, w_ref, h_ref, gtc_ref, gt_ref, o_ref, *, n_ctx, tiles_per_batch):
    gate = _row_gate(gtc_ref, gt_ref, h_ref.shape, pl.program_id(0) % tiles_per_batch, n_ctx)
    o_ref[...] = h_ref[...] + gate * _dot(x_ref[...], w_ref[...])


def _matmul_residual(x, w, j, h, mods, k_gate, n_batch, t, n_ctx):
    n, d = h.shape
    kdim = x.shape[1]
    tm = ROW_TILE
    tpb = t // tm
    kern = functools.partial(_mm_res_kernel, n_ctx=n_ctx, tiles_per_batch=tpb)
    return pl.pallas_call(
        kern, grid=(n // tm,),
        in_specs=[pl.BlockSpec((tm, kdim), lambda i: (i, 0)),
                  pl.BlockSpec((None, kdim, d), lambda i: (j, 0, 0)),
                  pl.BlockSpec((tm, d), lambda i: (i, 0))] + _stream_specs(d, n_batch, tpb, k_gate, 1),
        out_specs=pl.BlockSpec((tm, d), lambda i: (i, 0)),
        out_shape=jax.ShapeDtypeStruct((n, d), F32),
        compiler_params=_cparams("parallel"), name="matmul_residual",
    )(x, w, h, mods, mods)


def _hg_tables():
    c = HG_CHUNK
    t = np.arange(c)[:, None]
    u = np.arange(c)[None, :]
    mats = [(u <= t), (u > t)]
    for k in range(1, HG_LEVELS + 1):
        m = ((t >> k) << k) + (1 << (k - 1)) - 1
        upper = t > m
        mats.append(np.where(upper, (u > m) & (u <= t), (u > t) & (u <= m)))
    fwd = np.concatenate(mats, axis=0).astype(np.float32)
    bwd = fwd.reshape(-1, c, c)[:, ::-1, ::-1].reshape(-1, c)
    x = t ^ u
    lvl = np.where(t > u, np.floor(np.log2(np.maximum(x, 1))).astype(np.int32) + 1, -1)
    lvl = np.where(t == u, 0, lvl).astype(np.int32)
    return np.stack([fwd, bwd]), np.stack([lvl, lvl.T])


def _hg_kernel(q_ref, ff_ref, fb_ref, v_ref, lb_ref, m_ref, lvl_ref, o_ref, ob_ref,
               qk_ref, el_ref, io_ref, tot_ref, att_ref, *, n_ctx_chunks, n_chunks):
    c = HG_CHUNK
    z_refs = (ff_ref, fb_ref)
    out_refs = (o_ref, ob_ref)
    for ref in (qk_ref, el_ref, io_ref, tot_ref, att_ref):
        ref[...] = jnp.zeros(ref.shape, ref.dtype)

    def start_row(d, j):
        j = jnp.clip(j, 0, n_chunks - 1)
        if d == 1:
            j = jnp.where(j < n_ctx_chunks, n_ctx_chunks - 1 - j, n_chunks - 1 - (j - n_ctx_chunks))
        return pl.multiple_of(j * c, c)

    def stage1(d, j, slot):
        t0 = start_row(d, j)
        lb = lb_ref[d:d + 1, :]
        log_lb = jnp.log(jnp.maximum(lb, LB_MIN))
        q = _silu(q_ref[pl.ds(t0, c), :])
        z = z_refs[d][pl.ds(t0, c), :]
        k = (1.0 - lb) * _sigmoid(-z)
        b = jnp.log1p(-lb) + _log_sigmoid(z)
        logf = jnp.maximum(log_lb, b) + jnp.log1p(jnp.exp(-jnp.abs(log_lb - b)))
        g_hi, g_lo = _split2(logf * LOG2_E)
        m_ends = m_ref[d, 0:2 * c, :]
        e_ends = jnp.exp2(_dot(m_ends, g_hi) + _dot(m_ends, g_lo))
        el_ref[d, slot] = jnp.exp2(_dot(m_ref[d, 2 * c:, :], g_hi).astype(BF16))
        qk_ref[d, slot, 0] = q.astype(BF16)
        qk_ref[d, slot, 1] = k.astype(BF16)
        io_ref[d, slot, 0] = (q * e_ends[0:c]).astype(BF16)
        io_ref[d, slot, 1] = (k * e_ends[c:]).astype(BF16)
        tot_row = c - 1 if d == 0 else 0
        tot_ref[d, slot, 0:1, :] = e_ends[tot_row:tot_row + 1, :]

    def stage2(d, slot):
        qb = qk_ref[d, slot, 0]
        kb = qk_ref[d, slot, 1]
        lvl = lvl_ref[d]
        att = jnp.where(lvl == 0, _dot_nt(qb, kb), 0.0)
        for lev in range(1, HG_LEVELS + 1):
            el = el_ref[d, slot, (lev - 1) * c:lev * c, :]
            att = jnp.where(lvl == lev, _dot_nt(qb * el, kb * el), att)
        att_ref[d, slot] = att.astype(BF16)

    def stage3(d, j, slot, st):
        t0 = start_row(d, j)
        v = v_ref[pl.ds(t0, c), :].astype(BF16)
        out_refs[d][pl.ds(t0, c), :] = _dot(att_ref[d, slot], v) + _dot_nt(io_ref[d, slot, 0], st.astype(BF16))
        return st * tot_ref[d, slot, 0:1, :] + _dot_tn(v, io_ref[d, slot, 1])

    def two_steps(jj, carry):
        st = list(carry)
        for slot in range(2):
            i = 2 * jj + slot
            for d in range(2):
                st[d] = stage3(d, i - 2, slot, st[d])
            for d in range(2):
                stage2(d, 1 - slot)
            for d in range(2):
                stage1(d, i, slot)
        return tuple(st)

    zero = jnp.zeros((c, c), F32)
    lax.fori_loop(0, (n_chunks + 2) // 2, two_steps, (zero, zero))
    o_ref[...] += ob_ref[...]


def _hgrn2(p3, lb, n_ctx):
    bsz, t, _ = p3.shape
    width = lb.shape[1]
    heads = width // HG_HEAD_DIM
    mats, lvl = _hg_tables()
    c = HG_CHUNK

    def col(group):
        return pl.BlockSpec((None, t, c), lambda b, h: (b, 0, group * heads + h))

    kern = functools.partial(_hg_kernel, n_ctx_chunks=n_ctx // c, n_chunks=t // c)
    return pl.pallas_call(
        kern, grid=(bsz, heads),
        in_specs=[col(0), col(1), col(2), col(3),
                  pl.BlockSpec((2, c), lambda b, h: (0, h)),
                  pl.BlockSpec(mats.shape, lambda b, h: (0, 0, 0)),
                  pl.BlockSpec(lvl.shape, lambda b, h: (0, 0, 0))],
        out_specs=pl.BlockSpec((None, t, c), lambda b, h: (b, 0, h)),
        out_shape=jax.ShapeDtypeStruct((bsz, t, width), F32),
        scratch_shapes=[pltpu.VMEM((t, c), F32),
                        pltpu.VMEM((2, 2, 2, c, c), BF16), pltpu.VMEM((2, 2, HG_LEVELS * c, c), BF16),
                        pltpu.VMEM((2, 2, 2, c, c), BF16), pltpu.VMEM((2, 2, 8, c), F32),
                        pltpu.VMEM((2, 2, c, c), BF16)],
        compiler_params=_cparams("parallel", "parallel"), name="hgrn2",
    )(p3, p3, p3, p3, lb, jnp.asarray(mats, BF16), jnp.asarray(lvl))


PAD = 8


def _rg_kernel(x_ref, cw_ref, cb_ref, wa_ref, ba_ref, wx_ref, bx_ref, lam_ref, o_ref,
               xpad_ref, xc_ref, ob_ref, *, n_ctx):
    t, w = x_ref.shape
    n_lat = t - n_ctx
    lat0 = n_ctx + 2 * PAD
    xpad_ref[...] = jnp.zeros(xpad_ref.shape, F32)
    xpad_ref[PAD:PAD + n_ctx, :] = x_ref[0:n_ctx, :]
    xpad_ref[lat0:lat0 + n_lat, :] = x_ref[n_ctx:t, :]
    left = RG_CONV // 2
    for base, dst, n in ((PAD, 0, n_ctx), (lat0, n_ctx, n_lat)):
        acc = cb_ref[...] + cw_ref[0:1, :] * xpad_ref[base - left:base - left + n, :]
        for j in range(1, RG_CONV):
            acc = acc + cw_ref[j:j + 1, :] * xpad_ref[base - left + j:base - left + j + n, :]
        xc_ref[dst:dst + n, :] = acc

    blk = RG_SCAN_BLOCK
    n_blocks = t // blk
    n_ctx_blocks = n_ctx // blk
    sub = lax.broadcasted_iota(jnp.int32, (blk // RG_GROUP, RG_GROUP, w), 1)
    log_sig_lam = _log_sigmoid(lam_ref[...])

    def scan_block(d, t0, hin, out_ref):
        xc = xc_ref[pl.ds(t0, blk), :]
        xb = xc.astype(BF16)
        r = _sigmoid(_dot(xb, wa_ref[d]) + ba_ref[d:d + 1, :])
        ig = _sigmoid(_dot(xb, wx_ref[d]) + bx_ref[d:d + 1, :])
        log_a = RG_C * r * log_sig_lam[d:d + 1, :]
        a = jnp.exp(log_a)
        b = jnp.sqrt(jnp.maximum(-jnp.tanh(log_a) * (a * a + 1.0), 0.0)) * (ig * xc)
        n_groups = blk // RG_GROUP
        a = a.reshape(n_groups, RG_GROUP, w)
        b = b.reshape(n_groups, RG_GROUP, w)
        s = 1
        while s < RG_GROUP:
            shift = s if d == 0 else RG_GROUP - s
            keep = (sub >= s) if d == 0 else (sub < RG_GROUP - s)
            b = jnp.where(keep, a * pltpu.roll(b, shift, 1), 0.0) + b
            a = jnp.where(keep, a * pltpu.roll(a, shift, 1), a)
            s *= 2
        for gi in range(n_groups):
            gj = gi if d == 0 else n_groups - 1 - gi
            hg = a[gj] * hin + b[gj]
            out_ref[pl.ds(t0 + gj * RG_GROUP, RG_GROUP), :] = hg
            hin = hg[RG_GROUP - 1:RG_GROUP, :] if d == 0 else hg[0:1, :]
        return hin

    def step(i, carry):
        h_f, h_b = carry
        bb = jnp.where(i < n_ctx_blocks, n_ctx_blocks - 1 - i, n_blocks - 1 - (i - n_ctx_blocks))
        h_f = scan_block(0, pl.multiple_of(i * blk, blk), h_f, o_ref)
        h_b = scan_block(1, pl.multiple_of(bb * blk, blk), h_b, ob_ref)
        return h_f, h_b

    zero = jnp.zeros((1, w), F32)
    lax.fori_loop(0, n_blocks, step, (zero, zero))
    o_ref[...] += ob_ref[...]


def _blockdiag_dense(w):
    nd, nb, k, _ = w.shape
    eye = jnp.eye(nb, dtype=w.dtype)
    return jnp.einsum('dnij,nm->dnimj', w, eye).reshape(nd, nb * k, nb * k)


def _rglru(p3, col_block, conv_w, conv_b, wa, ba, wx, bx, lam, n_ctx):
    bsz, t, _ = p3.shape
    w = conv_w.shape[1]
    full = lambda shape: pl.BlockSpec(shape, lambda b: (0,) * len(shape))
    kern = functools.partial(_rg_kernel, n_ctx=n_ctx)
    return pl.pallas_call(
        kern, grid=(bsz,),
        in_specs=[pl.BlockSpec((None, t, w), lambda b: (b, 0, col_block)),
                  full((RG_CONV, w)), full((1, w)),
                  full((2, w, w)), full((2, w)), full((2, w, w)), full((2, w)), full((2, w))],
        out_specs=pl.BlockSpec((None, t, w), lambda b: (b, 0, 0)),
        out_shape=jax.ShapeDtypeStruct((bsz, t, w), F32),
        scratch_shapes=[pltpu.VMEM((t + 3 * PAD, w), F32), pltpu.VMEM((t, w), F32), pltpu.VMEM((t, w), F32)],
        compiler_params=_cparams("parallel"), name="rglru",
    )(p3, conv_w, conv_b.reshape(1, w), _blockdiag_dense(wa).astype(BF16), ba,
      _blockdiag_dense(wx).astype(BF16), bx, lam)


def _merge_kernel(hg_ref, rg_ref, gate_ref, y_ref, og_ref, w_ref, h_ref, gtc_ref, gt_ref, o_ref,
                  *, n_ctx, tiles_per_batch):
    hg = hg_ref[...]
    half = hg.shape[1]
    hn = hg * lax.rsqrt(jnp.mean(hg * hg, axis=-1, keepdims=True) + EPS) * og_ref[...]
    a = (hn * _silu(gate_ref[...])).astype(BF16)
    b = (rg_ref[...] * _gelu_tanh(y_ref[...])).astype(BF16)
    o = _dot(a, w_ref[0:half, :]) + _dot(b, w_ref[half:, :])
    gate = _row_gate(gtc_ref, gt_ref, h_ref.shape, pl.program_id(0) % tiles_per_batch, n_ctx)
    o_ref[...] = h_ref[...] + gate * o


def _merge_out(hg, rg, p, onorm_g, w_out, j, h, mods, n_batch, t, n_ctx):
    n, d = h.shape
    half = hg.shape[1]
    tm = ROW_TILE
    tpb = t // tm
    blk = lambda c: pl.BlockSpec((tm, half), lambda i: (i, c))
    kern = functools.partial(_merge_kernel, n_ctx=n_ctx, tiles_per_batch=tpb)
    return pl.pallas_call(
        kern, grid=(n // tm,),
        in_specs=[blk(0), blk(0), blk(4), blk(6),
                  pl.BlockSpec((1, half), lambda i: (0, 0)),
                  pl.BlockSpec((None, d, d), lambda i: (j, 0, 0)),
                  pl.BlockSpec((tm, d), lambda i: (i, 0))] + _stream_specs(d, n_batch, tpb, 2, 1),
        out_specs=pl.BlockSpec((tm, d), lambda i: (i, 0)),
        out_shape=jax.ShapeDtypeStruct((n, d), F32),
        compiler_params=_cparams("parallel"), name="merge_out",
    )(hg, rg, p, p, onorm_g.reshape(1, half), w_out, h, mods, mods)


def _na_bias_table(rpb):
    q = np.arange(GRID_W)[:, None]
    kc = np.arange(GRID_W)[None, :]
    wstart = np.clip(q - NA_KC // 2, 0, GRID_W - NA_KC)
    mask = (kc >= wstart) & (kc < wstart + NA_KC)
    coff = np.clip(kc - q + NA_KC - 1, 0, 2 * NA_KC - 2)
    sel = (coff[None] == np.arange(2 * NA_KC - 1)[:, None, None]).astype(np.float32)
    toe = jnp.einsum('hdj,jqk->hqdk', rpb.astype(F32), sel, precision=lax.Precision.HIGHEST)
    toe = jnp.where(mask[None, :, None, :], toe, NEG_BIG)
    h = toe.shape[0]
    tab = jnp.stack([toe[:, :, off:off + NA_KR].reshape(h // 2, 2 * GRID_W, NA_KR * GRID_W)
                     for off in range(NA_KR)], axis=1)
    return tab


def _stack_heads(x, hd):
    lane = lax.broadcasted_iota(jnp.int32, x.shape, 1)
    zero = jnp.zeros_like(x)
    return jnp.concatenate([jnp.where(lane < hd, x, zero), jnp.where(lane >= hd, x, zero)], axis=0)


def _unstack_heads(o2, hd):
    n = o2.shape[0] // 2
    lane = lax.broadcasted_iota(jnp.int32, (n, o2.shape[1]), 1)
    return jnp.where(lane < hd, o2[:n], o2[n:])


def _na_kernel(q_ref, k_ref, v_ref, qg_ref, kg_ref, bias_ref, o_ref, qn_ref, kn_ref,
               s_ref, p_ref, den_ref, *, n_ctx, scale):
    t, w = q_ref.shape
    hd = w // 2
    rows_grid = (t - n_ctx) // GRID_W
    r_i = lax.broadcasted_iota(jnp.int32, (w, w), 0)
    c_i = lax.broadcasted_iota(jnp.int32, (w, w), 1)
    seg = jnp.where((r_i // hd) == (c_i // hd), 1.0 / hd, 0.0).astype(BF16)

    def normed(x, g):
        s_hi, s_lo = _split2(x * x)
        ms = _dot(s_hi, seg) + _dot(s_lo, seg)
        return x * lax.rsqrt(ms + EPS) * g

    qn_ref[...] = (normed(q_ref[...].astype(F32), qg_ref[...]) * scale).astype(BF16)
    kn_ref[...] = normed(k_ref[...].astype(F32), kg_ref[...]).astype(BF16)
    vb_ref = v_ref

    kc = kn_ref[0:n_ctx, :]
    vc = vb_ref[0:n_ctx, :]

    s = _dot_nt(_stack_heads(qn_ref[0:n_ctx, :], hd), kc)
    p = jnp.exp(s - jnp.max(s, axis=-1, keepdims=True))
    o2 = _dot(p.astype(BF16), vc) / jnp.sum(p, axis=-1, keepdims=True)
    o_ref[0:n_ctx, :] = _unstack_heads(o2, hd).astype(o_ref.dtype)

    kr = NA_KR
    n_win = kr * GRID_W

    def window(r):
        r = jnp.clip(r, 0, rows_grid - 1)
        rs = jnp.clip(r - kr // 2, 0, rows_grid - kr)
        q0 = pl.multiple_of(n_ctx + r * GRID_W, GRID_W)
        k0 = pl.multiple_of(n_ctx + rs * GRID_W, GRID_W)
        return q0, k0, rs - r + (kr - 1)

    gap = NA_SLOTS // 2
    for slot in range(NA_SLOTS):
        s_ref[slot] = jnp.zeros(s_ref.shape[1:], F32)
        p_ref[slot] = jnp.zeros(p_ref.shape[1:], BF16)
        den_ref[slot] = jnp.ones(den_ref.shape[1:], F32)

    def step(i, slot):
        q0, k0, _ = window(i - 2 * gap)
        pr = p_ref[slot]
        o2 = (_dot(pr[:, :n_win], vb_ref[pl.ds(k0, n_win), :]) + _dot(pr[:, n_win:], vc)) / den_ref[slot]
        o_ref[pl.ds(q0, GRID_W), :] = _unstack_heads(o2, hd).astype(o_ref.dtype)

        mid = (slot + gap) % NA_SLOTS
        sc = s_ref[mid]
        pe = jnp.exp(sc - jnp.max(sc, axis=-1, keepdims=True))
        den_ref[mid] = jnp.sum(pe, axis=-1, keepdims=True)
        p_ref[mid] = pe.astype(BF16)

        q0, k0, off = window(i)
        q2 = _stack_heads(qn_ref[pl.ds(q0, GRID_W), :], hd)
        s_ref[slot, :, :n_win] = _dot_nt(q2, kn_ref[pl.ds(k0, n_win), :]) + bias_ref[off]
        s_ref[slot, :, n_win:] = _dot_nt(q2, kc)

    def steps(j, carry):
        for slot in range(NA_SLOTS):
            step(NA_SLOTS * j + slot, slot)
        return carry

    lax.fori_loop(0, (rows_grid + 2 * gap) // NA_SLOTS, steps, 0)


def _na_attention(qkv3, q_g, k_g, rpb, n_ctx):
    bsz, t, d3 = qkv3.shape
    d = d3 // 3
    hd = d // NA_HEADS
    w = 2 * hd
    pairs = d // w
    rows_grid = (t - n_ctx) // GRID_W
    assert w == LANES and rows_grid >= NA_KR and rows_grid % NA_SLOTS == 0
    n_keys = NA_KR * GRID_W + n_ctx
    bias = _na_bias_table(rpb)
    kern = functools.partial(_na_kernel, n_ctx=n_ctx, scale=hd ** -0.5)

    def col(group):
        return pl.BlockSpec((None, t, w), lambda hp, b: (b, 0, group * pairs + hp))

    gain = lambda g: jnp.tile(g.astype(F32), 2).reshape(1, w)
    return pl.pallas_call(
        kern, grid=(pairs, bsz),
        in_specs=[col(0), col(1), col(2),
                  pl.BlockSpec((1, w), lambda hp, b: (0, 0)),
                  pl.BlockSpec((1, w), lambda hp, b: (0, 0)),
                  pl.BlockSpec((None,) + bias.shape[1:], lambda hp, b: (hp, 0, 0, 0))],
        out_specs=pl.BlockSpec((None, t, w), lambda hp, b: (b, 0, hp)),
        out_shape=jax.ShapeDtypeStruct((bsz, t, d), BF16),
        scratch_shapes=[pltpu.VMEM((t, w), BF16)] * 2
        + [pltpu.VMEM((NA_SLOTS, 2 * GRID_W, n_keys), F32), pltpu.VMEM((NA_SLOTS, 2 * GRID_W, n_keys), BF16),
           pltpu.VMEM((NA_SLOTS, 2 * GRID_W, 1), F32)],
        compiler_params=_cparams("parallel", "parallel"), name="na_attention",
    )(qkv3, qkv3, qkv3, gain(q_g), gain(k_g), bias)


FF_TILE = 1408


def _swiglu_kernel(h_ref, g_ref, shc_ref, sh_ref, scc_ref, sc_ref, w1_ref, w3_ref, w2_ref, gtc_ref, gt_ref,
                   o_ref, xb_ref, acc_ref, *, n_ctx, tiles_per_batch):
    f = pl.program_id(1)
    tile_in_batch = pl.program_id(0) % tiles_per_batch

    @pl.when(f == 0)
    def _():
        xb_ref[...] = _modulated_rows(h_ref, g_ref, shc_ref, sh_ref, scc_ref, sc_ref,
                                      tile_in_batch, n_ctx).astype(BF16)
        acc_ref[...] = jnp.zeros_like(acc_ref)

    x = xb_ref[...]
    hid = _silu(_dot(x, w1_ref[...])) * _dot(x, w3_ref[...])
    acc_ref[...] += _dot(hid.astype(BF16), w2_ref[...])

    @pl.when(f == pl.num_programs(1) - 1)
    def _():
        gate = _row_gate(gtc_ref, gt_ref, acc_ref.shape, tile_in_batch, n_ctx)
        o_ref[...] = h_ref[...] + gate * acc_ref[...]


def _swiglu_residual(h, g, w1, w3, w2, j, mods, n_batch, t, n_ctx):
    n, d = h.shape
    dff = w1.shape[2]
    tm, tf = ROW_TILE, FF_TILE
    tpb = t // tm
    kern = functools.partial(_swiglu_kernel, n_ctx=n_ctx, tiles_per_batch=tpb)
    return pl.pallas_call(
        kern, grid=(n // tm, dff // tf),
        in_specs=[pl.BlockSpec((tm, d), lambda i, f: (i, 0)),
                  pl.BlockSpec((1, d), lambda i, f: (0, 0))]
        + _stream_specs(d, n_batch, tpb, 3, 2) + _stream_specs(d, n_batch, tpb, 4, 2)
        + [pl.BlockSpec((None, d, tf), lambda i, f: (j, 0, f)),
           pl.BlockSpec((None, d, tf), lambda i, f: (j, 0, f)),
           pl.BlockSpec((None, tf, d), lambda i, f: (j, f, 0))] + _stream_specs(d, n_batch, tpb, 5, 2),
        out_specs=pl.BlockSpec((tm, d), lambda i, f: (i, 0)),
        out_shape=jax.ShapeDtypeStruct((n, d), F32),
        scratch_shapes=[pltpu.VMEM((tm, d), BF16), pltpu.VMEM((tm, d), F32)],
        compiler_params=_cparams("parallel", "arbitrary"), name="swiglu",
    )(h, g, mods, mods, mods, mods, w1, w3, w2, mods, mods)


MOE_ROWS = 512


def _route_tables(route, tok, tm, n_exp):
    n = route.shape[0]
    e_pair = route[:, 0:2].astype(jnp.int32).reshape(-1)
    onehot = (e_pair[:, None] == jnp.arange(n_exp, dtype=jnp.int32)[None, :]).astype(jnp.int32)
    csum = jnp.cumsum(onehot, axis=0)
    rank = jnp.sum((csum - onehot) * onehot, axis=1)
    counts = csum[-1]
    padded = ((counts + tm - 1) // tm) * tm
    ends = jnp.cumsum(padded)
    slot = jnp.sum(onehot * (ends - padded)[None, :], axis=1) + rank
    p_slots = 2 * n + n_exp * tm
    tok_of_slot = jnp.zeros((p_slots,), jnp.int32).at[slot].set(jnp.repeat(tok, 2))
    n_tiles = p_slots // tm
    n_used = ends[-1] // tm
    tile = jnp.arange(n_tiles, dtype=jnp.int32)
    te = jnp.sum(((tile * tm)[:, None] >= ends[None, :]).astype(jnp.int32), axis=1)
    te = jnp.where(tile < n_used, te, te[n_used - 1])
    return slot.reshape(n, 2), tok_of_slot, te, n_used.reshape(1)


def _gather_rows(table, idx):
    p = idx.shape[0]
    m = table.shape[1]
    mesh = plsc.VectorSubcoreMesh(core_axis_name="c", subcore_axis_name="s")

    @functools.partial(pl.kernel, out_type=jax.ShapeDtypeStruct((p, m), table.dtype), mesh=mesh)
    def gather(table_hbm, idx_hbm, out_hbm):
        def body(idx_vmem, out_vmem):
            pltpu.sync_copy(table_hbm.at[idx_vmem.at[0]], out_vmem)

        pltpu.emit_pipeline(
            body, grid=(p // SC_WINDOW,),
            in_specs=[pl.BlockSpec((1, SC_WINDOW), lambda i: (0, i))],
            out_specs=[pl.BlockSpec((SC_WINDOW, m), lambda i: (i, 0))],
            core_axis_name=("c", "s"), dimension_semantics=(pltpu.PARALLEL,),
        )(idx_hbm, out_hbm)

    return gather(table, idx.reshape(1, p))


def _gather_pieces(pieces, idx):
    npc, v, m = pieces.shape
    offs = (jnp.arange(npc, dtype=jnp.int32) * v).reshape((npc,) + (1,) * idx.ndim)
    flat = (idx[None] + offs).reshape(-1)
    return _gather_rows(pieces.reshape(npc * v, m), flat).reshape((npc,) + idx.shape + (m,))


def _moe_ffn_kernel(te_ref, nu_ref, x_ref, w1_ref, w3_ref, w2_ref, y_ref, xb_ref, acc_ref):
    i = pl.program_id(0)
    f = pl.program_id(1)

    @pl.when(i < nu_ref[0])
    def _():
        @pl.when(f == 0)
        def _():
            xb_ref[...] = _load_unpacked(x_ref).astype(BF16)
            acc_ref[...] = jnp.zeros_like(acc_ref)

        x = xb_ref[...]
        hid = _silu(_dot(x, w1_ref[...])) * _dot(x, w3_ref[...])
        acc_ref[...] += _dot(hid.astype(BF16), w2_ref[...])

        @pl.when(f == pl.num_programs(1) - 1)
        def _():
            _store_pieces(y_ref, _pack_halves(acc_ref[...]))


def _moe_ffn(xs, te, n_used, w1, w3, w2, j):
    npc, p, m = xs.shape
    d = 2 * npc * m
    dff = w1.shape[3]
    tm, tf = MOE_ROWS, FF_TILE
    nf = dff // tf

    def row(i, f, te_ref, nu_ref):
        return (0, jnp.minimum(i, nu_ref[0] - 1), 0)

    def fcol(i, f, nu_ref):
        return jnp.where(i < nu_ref[0], f, nf - 1)

    grid_spec = pltpu.PrefetchScalarGridSpec(
        num_scalar_prefetch=2, grid=(p // tm, nf),
        in_specs=[pl.BlockSpec((npc, tm, m), row),
                  pl.BlockSpec((None, None, d, tf), lambda i, f, te_ref, nu_ref: (j, te_ref[i], 0, fcol(i, f, nu_ref))),
                  pl.BlockSpec((None, None, d, tf), lambda i, f, te_ref, nu_ref: (j, te_ref[i], 0, fcol(i, f, nu_ref))),
                  pl.BlockSpec((None, None, tf, d), lambda i, f, te_ref, nu_ref: (j, te_ref[i], fcol(i, f, nu_ref), 0))],
        out_specs=pl.BlockSpec((npc, tm, m), row),
        scratch_shapes=[pltpu.VMEM((tm, d), BF16), pltpu.VMEM((tm, d), F32)])
    return pl.pallas_call(
        _moe_ffn_kernel, grid_spec=grid_spec,
        out_shape=jax.ShapeDtypeStruct((npc, p, m), jnp.int32),
        compiler_params=_cparams("arbitrary", "arbitrary"), name="moe_ffn",
    )(te, n_used, xs, w1, w3, w2)


def _moe_combine_kernel(ya_ref, yb_ref, route_ref, h_ref, gt_ref, o_ref):
    r = route_ref[...]
    y = r[:, 2:3] * _load_unpacked(ya_ref) + r[:, 3:4] * _load_unpacked(yb_ref)
    o_ref[...] = h_ref[...] + gt_ref[...] * y


def _moe_combine(yg, route, h, mods, n_batch, tiles_per_batch, ctx_tiles):
    d = h.shape[1]
    npc, _, r, m = yg.shape
    tm = TOKEN_TILE
    per = tiles_per_batch - ctx_tiles
    h_idx = lambda i: ((i // per) * tiles_per_batch + ctx_tiles + i % per, 0)
    if ctx_tiles == 0:
        mod_idx = _mod_row_spec(5, tiles_per_batch, n_batch)
    else:
        mod_idx = lambda i: (i // per, 5, 0, 0)
    return pl.pallas_call(
        _moe_combine_kernel, grid=(r // tm,),
        in_specs=[pl.BlockSpec((npc, None, tm, m), lambda i: (0, 0, i, 0)),
                  pl.BlockSpec((npc, None, tm, m), lambda i: (0, 1, i, 0)),
                  pl.BlockSpec((tm, LANES), lambda i: (i, 0)),
                  pl.BlockSpec((tm, d), h_idx),
                  pl.BlockSpec((None, None, 1, d), mod_idx)],
        out_specs=pl.BlockSpec((tm, d), lambda i: (i, 0)),
        out_shape=jax.ShapeDtypeStruct((r, d), F32),
        compiler_params=_cparams("parallel"), name="moe_combine",
    )(yg, yg, route, h, mods)


def _moe_residual(u_pieces, route, w1, w3, w2, j, h, mods, n_batch, t, n_ctx, latent_only):
    n = h.shape[0]
    tok = jnp.arange(n, dtype=jnp.int32)
    if latent_only:
        tok = tok.reshape(n_batch, t)[:, n_ctx:].reshape(-1)
        route = route.reshape(n_batch, t, LANES)[:, n_ctx:].reshape(-1, LANES)
    slot, tok_of_slot, te, n_used = _route_tables(route, tok, MOE_ROWS, w1.shape[1])
    xs = _gather_pieces(u_pieces, tok_of_slot)
    ys = _moe_ffn(xs, te, n_used, w1, w3, w2, j)
    yg = _gather_pieces(ys, slot.T)
    ctx_tiles = n_ctx // TOKEN_TILE if latent_only else 0
    return _moe_combine(yg, route, h, mods, n_batch, t // TOKEN_TILE, ctx_tiles)


def kernel(x, c, ctx, c_ctx, w_mod, b_mod, norm_mix_g, norm_ffn_g, ev_w_in, ev_w_out, hg_lb_logits, hg_onorm_g, rg_conv_w, rg_conv_b, rg_wa, rg_ba, rg_wx, rg_bx, rg_lambda, na_w_qkv, na_w_o, na_q_g, na_k_g, na_rpb, ffn_w1, ffn_w3, ffn_w2, moe_router, moe_w1, moe_w3, moe_w2):
    bsz, seq, d = x.shape
    n_ctx = ctx.shape[1]
    t = n_ctx + seq
    depth = w_mod.shape[0]
    tpb = t // TOKEN_TILE
    assert n_ctx % TOKEN_TILE == 0 and seq % TOKEN_TILE == 0 and t % ROW_TILE == 0 and n_ctx <= ROW_TILE

    rows = -(-(bsz + 1) // 8) * 8
    cond = jnp.zeros((rows, d), F32).at[:bsz].set(c).at[bsz].set(c_ctx)
    mods_all = _adaln(cond, w_mod, b_mod).reshape(depth, rows, N_MOD, 1, d)

    lb_p = jax.nn.softmax(hg_lb_logits.astype(F32), axis=0)
    lb_all = jnp.cumsum(lb_p, axis=0) - lb_p[0]

    bf = lambda w: w.astype(BF16)
    w_in, w_out, w_qkv, w_o = bf(ev_w_in), bf(ev_w_out), bf(na_w_qkv), bf(na_w_o)
    f_w1, f_w3, f_w2 = bf(ffn_w1), bf(ffn_w3), bf(ffn_w2)
    m_w1, m_w3, m_w2 = bf(moe_w1), bf(moe_w3), bf(moe_w2)

    h = jnp.concatenate([ctx, x], axis=1).reshape(bsz * t, d)
    for layer in range(depth):
        j = layer // 2
        mods = mods_all[layer]
        g_mix = norm_mix_g[layer].reshape(1, d)
        if layer % 2 == 0:
            p = _modulated_matmul(h, g_mix, mods, w_in, j, F32, w_in.shape[2] // 2, bsz, t, n_ctx)
            p3 = p.reshape(bsz, t, -1)
            hg = _hgrn2(p3, lb_all[j], n_ctx)
            half = hg.shape[-1]
            rg = _rglru(p3, 5 * half // (d - half), rg_conv_w[j], rg_conv_b[j], rg_wa[j], rg_ba[j],
                        rg_wx[j], rg_bx[j], rg_lambda[j], n_ctx)
            h = _merge_out(hg.reshape(bsz * t, half), rg.reshape(bsz * t, d - half), p, hg_onorm_g[j],
                           w_out, j, h, mods, bsz, t, n_ctx)
        else:
            qkv = _modulated_matmul(h, g_mix, mods, w_qkv, j, BF16, 3 * d // 2, bsz, t, n_ctx)
            o = _na_attention(qkv.reshape(bsz, t, 3 * d), na_q_g[j], na_k_g[j], na_rpb[j], n_ctx)
            h = _matmul_residual(o.reshape(bsz * t, d), w_o, j, h, mods, 2, bsz, t, n_ctx)
        if layer % 2 == 0:
            h = _swiglu_residual(h, norm_ffn_g[layer].reshape(1, d), f_w1, f_w3, f_w2, j, mods, bsz, t, n_ctx)
        else:
            u, route = _modulate_router(h, norm_ffn_g[layer].reshape(1, d), mods, 3, bsz, tpb, moe_router[j])
            h = _moe_residual(u, route, m_w1, m_w3, m_w2, j, h, mods, bsz, t, n_ctx, layer == depth - 1)
    if depth % 2 == 0:
        return h.reshape(bsz, seq, d)
    return h.reshape(bsz, t, d)[:, n_ctx:]
```

```python
import functools

import numpy as np
import jax
import jax.numpy as jnp
from jax import lax
from jax.experimental import pallas as pl
from jax.experimental.pallas import tpu as pltpu
from jax.experimental.pallas import tpu_sc as plsc

F32 = jnp.float32
BF16 = jnp.bfloat16

EPS = 1e-6
NEG_BIG = -1e9
LB_MIN = 1e-30
LOG2_E = 1.4426950408889634
N_MOD = 6
GRID_W = 64
HG_HEAD_DIM = 128
RG_BLOCKS = 8
RG_CONV = 4
RG_C = 8.0
NA_HEADS = 16
NA_KR = 8
NA_KC = 16
N_EXPERTS = 8

LANES = 128
SC_WINDOW = 128
SC_PIECES = 2
TOKEN_TILE = 256
HG_CHUNK = 128
HG_LEVELS = 7
NA_SLOTS = 4
RG_SCAN_BLOCK = 256
RG_GROUP = 8
VMEM_LIMIT = 56 * 1024 * 1024


def _cparams(*sem):
    return pltpu.CompilerParams(dimension_semantics=sem, vmem_limit_bytes=VMEM_LIMIT)


def _dot(a, b):
    return jnp.dot(a, b, preferred_element_type=F32)


def _dot_nt(a, b):
    return lax.dot_general(a, b, (((1,), (1,)), ((), ())), preferred_element_type=F32)


def _dot_tn(a, b):
    return lax.dot_general(a, b, (((0,), (0,)), ((), ())), preferred_element_type=F32)


def _split2(x):
    hi = x.astype(BF16)
    lo = (x - hi.astype(F32)).astype(BF16)
    return hi, lo


def _split3(x):
    hi = x.astype(BF16)
    r = x - hi.astype(F32)
    mid = r.astype(BF16)
    lo = (r - mid.astype(F32)).astype(BF16)
    return hi, mid, lo


def _sigmoid(x):
    return 1.0 / (1.0 + jnp.exp(-x))


def _silu(x):
    return x * _sigmoid(x)


def _log_sigmoid(x):
    return jnp.minimum(x, 0.0) - jnp.log1p(jnp.exp(-jnp.abs(x)))


def _gelu_tanh(x):
    return 0.5 * x * (1.0 + jnp.tanh(np.sqrt(2.0 / np.pi) * (x + 0.044715 * (x * x * x))))


def _adaln_kernel(c_ref, w_ref, b_ref, o_ref):
    a_hi, a_lo = _split2(_silu(c_ref[...]))
    w_hi, w_lo = _split2(w_ref[...])
    o_ref[...] = _dot(a_hi, w_hi) + _dot(a_hi, w_lo) + _dot(a_lo, w_hi) + b_ref[...]


def _adaln(cond, w_mod, b_mod):
    depth, d, n = w_mod.shape
    r = cond.shape[0]
    tn = 1536
    return pl.pallas_call(
        _adaln_kernel,
        grid=(depth, n // tn),
        in_specs=[pl.BlockSpec((r, d), lambda l, j: (0, 0)),
                  pl.BlockSpec((None, d, tn), lambda l, j: (l, 0, j)),
                  pl.BlockSpec((None, 1, tn), lambda l, j: (l, 0, j))],
        out_specs=pl.BlockSpec((None, r, tn), lambda l, j: (l, 0, j)),
        out_shape=jax.ShapeDtypeStruct((depth, r, n), F32),
        compiler_params=_cparams("parallel", "parallel"),
        name="adaln",
    )(cond, w_mod, b_mod.reshape(depth, 1, n))


def _mod_row_spec(k, tiles_per_batch, n_batch):
    def idx(i):
        row = jnp.where(i % tiles_per_batch == 0, n_batch, i // tiles_per_batch)
        return (row, k, 0, 0)
    return idx


def _modulated(h_ref, g_ref, sh_ref, sc_ref):
    x = h_ref[...]
    y = x * lax.rsqrt(jnp.mean(x * x, axis=-1, keepdims=True) + EPS)
    return (y * g_ref[...]) * (1.0 + sc_ref[...]) + sh_ref[...]


def _pack_halves(x):
    m = x.shape[1] // 2
    lo = lax.bitcast_convert_type(x[:, :m].astype(BF16).astype(F32), jnp.int32)
    hi = lax.bitcast_convert_type(x[:, m:].astype(BF16).astype(F32), jnp.int32)
    return lax.shift_right_logical(lo, 16) | (hi & jnp.int32(-65536))


def _unpack_halves(w):
    lo = lax.bitcast_convert_type(lax.shift_left(w, 16), F32)
    hi = lax.bitcast_convert_type(w & jnp.int32(-65536), F32)
    return lo, hi


def _store_pieces(ref, w):
    m = w.shape[1] // SC_PIECES
    for j in range(SC_PIECES):
        ref[j] = w[:, j * m:(j + 1) * m]


def _load_unpacked(ref):
    parts = [_unpack_halves(ref[j]) for j in range(SC_PIECES)]
    return jnp.concatenate([lo for lo, _ in parts] + [hi for _, hi in parts], axis=1)


def _modulate_router_kernel(h_ref, g_ref, sh_ref, sc_ref, wr_ref, u_ref, route_ref):
    u = _modulated(h_ref, g_ref, sh_ref, sc_ref)
    _store_pieces(u_ref, _pack_halves(u))
    u0, u1, u2 = _split3(u)
    w0, w1, w2 = _split3(wr_ref[...])
    logits = (_dot(u0, w0) + _dot(u0, w1) + _dot(u1, w0)
              + _dot(u1, w1) + _dot(u0, w2) + _dot(u2, w0))
    lane = lax.broadcasted_iota(jnp.int32, logits.shape, 1).astype(F32)
    lg = jnp.where(lane < N_EXPERTS, logits, -jnp.inf)
    m1 = jnp.max(lg, axis=-1, keepdims=True)
    i1 = jnp.min(jnp.where(lg == m1, lane, float(LANES)), axis=-1, keepdims=True)
    lg2 = jnp.where(lane == i1, -jnp.inf, lg)
    m2 = jnp.max(lg2, axis=-1, keepdims=True)
    i2 = jnp.min(jnp.where(lg2 == m2, lane, float(LANES)), axis=-1, keepdims=True)
    e2 = jnp.exp(m2 - m1)
    den = 1.0 + e2
    route_ref[...] = jnp.where(lane == 0, i1, jnp.where(lane == 1, i2, jnp.where(lane == 2, 1.0 / den, e2 / den)))


def _modulate_router(h, g, mods, k_shift, n_batch, tiles_per_batch, w_router):
    n, d = h.shape
    tm = TOKEN_TILE
    row = pl.BlockSpec((tm, d), lambda i: (i, 0))
    vec = pl.BlockSpec((1, d), lambda i: (0, 0))

    def mod_spec(k):
        return pl.BlockSpec((None, None, 1, d), _mod_row_spec(k, tiles_per_batch, n_batch))

    in_specs = [row, vec, mod_spec(k_shift), mod_spec(k_shift + 1)]
    wr = jnp.zeros((d, LANES), F32).at[:, :N_EXPERTS].set(w_router)
    return pl.pallas_call(
        _modulate_router_kernel, grid=(n // tm,),
        in_specs=in_specs + [pl.BlockSpec((d, LANES), lambda i: (0, 0))],
        out_specs=[pl.BlockSpec((SC_PIECES, tm, d // 2 // SC_PIECES), lambda i: (0, i, 0)),
                   pl.BlockSpec((tm, LANES), lambda i: (i, 0))],
        out_shape=[jax.ShapeDtypeStruct((SC_PIECES, n, d // 2 // SC_PIECES), jnp.int32),
                   jax.ShapeDtypeStruct((n, LANES), F32)],
        compiler_params=_cparams("parallel"), name="modulate_router",
    )(h, g, mods, mods, wr)


ROW_TILE = 768


def _row_gate(ctx_ref, own_ref, shape, tile_in_batch, n_ctx):
    rows = lax.broadcasted_iota(jnp.int32, shape, 0)
    is_ctx = (rows < n_ctx) & (tile_in_batch == 0)
    return jnp.where(is_ctx, ctx_ref[...], own_ref[...])


def _modulated_rows(h_ref, g_ref, shc_ref, sh_ref, scc_ref, sc_ref, tile_in_batch, n_ctx):
    x = h_ref[...]
    y = x * lax.rsqrt(jnp.mean(x * x, axis=-1, keepdims=True) + EPS)
    shift = _row_gate(shc_ref, sh_ref, x.shape, tile_in_batch, n_ctx)
    scale = _row_gate(scc_ref, sc_ref, x.shape, tile_in_batch, n_ctx)
    return (y * g_ref[...]) * (1.0 + scale) + shift


def _stream_specs(d, n_batch, tiles_per_batch, k, nargs):
    if nargs == 1:
        ctx = lambda i: (n_batch, k, 0, 0)
        own = lambda i: (i // tiles_per_batch, k, 0, 0)
    else:
        ctx = lambda i, f: (n_batch, k, 0, 0)
        own = lambda i, f: (i // tiles_per_batch, k, 0, 0)
    return [pl.BlockSpec((None, None, 1, d), ctx), pl.BlockSpec((None, None, 1, d), own)]


def _mod_mm_kernel(h_ref, g_ref, shc_ref, sh_ref, scc_ref, sc_ref, w_ref, o_ref, *, n_ctx, tiles_per_batch):
    x = _modulated_rows(h_ref, g_ref, shc_ref, sh_ref, scc_ref, sc_ref,
                        pl.program_id(0) % tiles_per_batch, n_ctx).astype(BF16)
    o_ref[...] = _dot(x, w_ref[...]).astype(o_ref.dtype)


def _modulated_matmul(h, g, mods, w, j, out_dtype, tm, n_batch, t, n_ctx):
    n, d = h.shape
    nout = w.shape[2]
    tpb = t // tm
    assert t % tm == 0 and n_ctx <= tm
    kern = functools.partial(_mod_mm_kernel, n_ctx=n_ctx, tiles_per_batch=tpb)
    return pl.pallas_call(
        kern, grid=(n // tm,),
        in_specs=[pl.BlockSpec((tm, d), lambda i: (i, 0)),
                  pl.BlockSpec((1, d), lambda i: (0, 0))]
        + _stream_specs(d, n_batch, tpb, 0, 1) + _stream_specs(d, n_batch, tpb, 1, 1)
        + [pl.BlockSpec((None, d, nout), lambda i: (j, 0, 0))],
        out_specs=pl.BlockSpec((tm, nout), lambda i: (i, 0)),
        out_shape=jax.ShapeDtypeStruct((n, nout), out_dtype),
        compiler_params=_cparams("parallel"), name="modulated_matmul",
    )(h, g, mods, mods, mods, mods, w)


def _mm_res_kernel(x_ref, w_ref, h_ref, gtc_ref, gt_ref, o_ref, *, n_ctx, tiles_per_batch):
    gate = _row_gate(gtc_ref, gt_ref, h_ref.shape, pl.program_id(0) % tiles_per_batch, n_ctx)
    o_ref[...] = h_ref[...] + gate * _dot(x_ref[...], w_ref[...])


def _matmul_residual(x, w, j, h, mods, k_gate, n_batch, t, n_ctx):
    n, d = h.shape
    kdim = x.shape[1]
    tm = ROW_TILE
    tpb = t // tm
    kern = functools.partial(_mm_res_kernel, n_ctx=n_ctx, tiles_per_batch=tpb)
    return pl.pallas_call(
        kern, grid=(n // tm,),
        in_specs=[pl.BlockSpec((tm, kdim), lambda i: (i, 0)),
                  pl.BlockSpec((None, kdim, d), lambda i: (j, 0, 0)),
                  pl.BlockSpec((tm, d), lambda i: (i, 0))] + _stream_specs(d, n_batch, tpb, k_gate, 1),
        out_specs=pl.BlockSpec((tm, d), lambda i: (i, 0)),
        out_shape=jax.ShapeDtypeStruct((n, d), F32),
        compiler_params=_cparams("parallel"), name="matmul_residual",
    )(x, w, h, mods, mods)


def _hg_tables():
    c = HG_CHUNK
    t = np.arange(c)[:, None]
    u = np.arange(c)[None, :]
    mats = [(u <= t), (u > t)]
    for k in range(1, HG_LEVELS + 1):
        m = ((t >> k) << k) + (1 << (k - 1)) - 1
        upper = t > m
        mats.append(np.where(upper, (u > m) & (u <= t), (u > t) & (u <= m)))
    fwd = np.concatenate(mats, axis=0).astype(np.float32)
    bwd = fwd.reshape(-1, c, c)[:, ::-1, ::-1].reshape(-1, c)
    x = t ^ u
    lvl = np.where(t > u, np.floor(np.log2(np.maximum(x, 1))).astype(np.int32) + 1, -1)
    lvl = np.where(t == u, 0, lvl).astype(np.int32)
    return np.stack([fwd, bwd]), np.stack([lvl, lvl.T])


def _hg_kernel(q_ref, ff_ref, fb_ref, v_ref, lb_ref, m_ref, lvl_ref, o_ref, ob_ref,
               qk_ref, el_ref, io_ref, tot_ref, att_ref, *, n_ctx_chunks, n_chunks):
    c = HG_CHUNK
    z_refs = (ff_ref, fb_ref)
    out_refs = (o_ref, ob_ref)
    for ref in (qk_ref, el_ref, io_ref, tot_ref, att_ref):
        ref[...] = jnp.zeros(ref.shape, ref.dtype)

    def start_row(d, j):
        j = jnp.clip(j, 0, n_chunks - 1)
        if d == 1:
            j = jnp.where(j < n_ctx_chunks, n_ctx_chunks - 1 - j, n_chunks - 1 - (j - n_ctx_chunks))
        return pl.multiple_of(j * c, c)

    def stage1(d, j, slot):
        t0 = start_row(d, j)
        lb = lb_ref[d:d + 1, :]
        log_lb = jnp.log(jnp.maximum(lb, LB_MIN))
        q = _silu(q_ref[pl.ds(t0, c), :])
        z = z_refs[d][pl.ds(t0, c), :]
        k = (1.0 - lb) * _sigmoid(-z)
        b = jnp.log1p(-lb) + _log_sigmoid(z)
        logf = jnp.maximum(log_lb, b) + jnp.log1p(jnp.exp(-jnp.abs(log_lb - b)))
        g_hi, g_lo = _split2(logf * LOG2_E)
        m_ends = m_ref[d, 0:2 * c, :]
        e_ends = jnp.exp2(_dot(m_ends, g_hi) + _dot(m_ends, g_lo))
        el_ref[d, slot] = jnp.exp2(_dot(m_ref[d, 2 * c:, :], g_hi).astype(BF16))
        qk_ref[d, slot, 0] = q.astype(BF16)
        qk_ref[d, slot, 1] = k.astype(BF16)
        io_ref[d, slot, 0] = (q * e_ends[0:c]).astype(BF16)
        io_ref[d, slot, 1] = (k * e_ends[c:]).astype(BF16)
        tot_row = c - 1 if d == 0 else 0
        tot_ref[d, slot, 0:1, :] = e_ends[tot_row:tot_row + 1, :]

    def stage2(d, slot):
        qb = qk_ref[d, slot, 0]
        kb = qk_ref[d, slot, 1]
        lvl = lvl_ref[d]
        att = jnp.where(lvl == 0, _dot_nt(qb, kb), 0.0)
        for lev in range(1, HG_LEVELS + 1):
            el = el_ref[d, slot, (lev - 1) * c:lev * c, :]
            att = jnp.where(lvl == lev, _dot_nt(qb * el, kb * el), att)
        att_ref[d, slot] = att.astype(BF16)

    def stage3(d, j, slot, st):
        t0 = start_row(d, j)
        v = v_ref[pl.ds(t0, c), :].astype(BF16)
        out_refs[d][pl.ds(t0, c), :] = _dot(att_ref[d, slot], v) + _dot_nt(io_ref[d, slot, 0], st.astype(BF16))
        return st * tot_ref[d, slot, 0:1, :] + _dot_tn(v, io_ref[d, slot, 1])

    def two_steps(jj, carry):
        st = list(carry)
        for slot in range(2):
            i = 2 * jj + slot
            for d in range(2):
                st[d] = stage3(d, i - 2, slot, st[d])
            for d in range(2):
                stage2(d, 1 - slot)
            for d in range(2):
                stage1(d, i, slot)
        return tuple(st)

    zero = jnp.zeros((c, c), F32)
    lax.fori_loop(0, (n_chunks + 2) // 2, two_steps, (zero, zero))
    o_ref[...] += ob_ref[...]


def _hgrn2(p3, lb, n_ctx):
    bsz, t, _ = p3.shape
    width = lb.shape[1]
    heads = width // HG_HEAD_DIM
    mats, lvl = _hg_tables()
    c = HG_CHUNK

    def col(group):
        return pl.BlockSpec((None, t, c), lambda b, h: (b, 0, group * heads + h))

    kern = functools.partial(_hg_kernel, n_ctx_chunks=n_ctx // c, n_chunks=t // c)
    return pl.pallas_call(
        kern, grid=(bsz, heads),
        in_specs=[col(0), col(1), col(2), col(3),
                  pl.BlockSpec((2, c), lambda b, h: (0, h)),
                  pl.BlockSpec(mats.shape, lambda b, h: (0, 0, 0)),
                  pl.BlockSpec(lvl.shape, lambda b, h: (0, 0, 0))],
        out_specs=pl.BlockSpec((None, t, c), lambda b, h: (b, 0, h)),
        out_shape=jax.ShapeDtypeStruct((bsz, t, width), F32),
        scratch_shapes=[pltpu.VMEM((t, c), F32),
                        pltpu.VMEM((2, 2, 2, c, c), BF16), pltpu.VMEM((2, 2, HG_LEVELS * c, c), BF16),
                        pltpu.VMEM((2, 2, 2, c, c), BF16), pltpu.VMEM((2, 2, 8, c), F32),
                        pltpu.VMEM((2, 2, c, c), BF16)],
        compiler_params=_cparams("parallel", "parallel"), name="hgrn2",
    )(p3, p3, p3, p3, lb, jnp.asarray(mats, BF16), jnp.asarray(lvl))


PAD = 8


def _rg_kernel(x_ref, cw_ref, cb_ref, wa_ref, ba_ref, wx_ref, bx_ref, lam_ref, o_ref,
               xpad_ref, xc_ref, ob_ref, *, n_ctx):
    t, w = x_ref.shape
    n_lat = t - n_ctx
    lat0 = n_ctx + 2 * PAD
    xpad_ref[...] = jnp.zeros(xpad_ref.shape, F32)
    xpad_ref[PAD:PAD + n_ctx, :] = x_ref[0:n_ctx, :]
    xpad_ref[lat0:lat0 + n_lat, :] = x_ref[n_ctx:t, :]
    left = RG_CONV // 2
    for base, dst, n in ((PAD, 0, n_ctx), (lat0, n_ctx, n_lat)):
        acc = cb_ref[...] + cw_ref[0:1, :] * xpad_ref[base - left:base - left + n, :]
        for j in range(1, RG_CONV):
            acc = acc + cw_ref[j:j + 1, :] * xpad_ref[base - left + j:base - left + j + n, :]
        xc_ref[dst:dst + n, :] = acc

    blk = RG_SCAN_BLOCK
    n_blocks = t // blk
    n_ctx_blocks = n_ctx // blk
    sub = lax.broadcasted_iota(jnp.int32, (blk // RG_GROUP, RG_GROUP, w), 1)
    log_sig_lam = _log_sigmoid(lam_ref[...])

    def scan_block(d, t0, hin, out_ref):
        xc = xc_ref[pl.ds(t0, blk), :]
        xb = xc.astype(BF16)
        r = _sigmoid(_dot(xb, wa_ref[d]) + ba_ref[d:d + 1, :])
        ig = _sigmoid(_dot(xb, wx_ref[d]) + bx_ref[d:d + 1, :])
        log_a = RG_C * r * log_sig_lam[d:d + 1, :]
        a = jnp.exp(log_a)
        b = jnp.sqrt(jnp.maximum(-jnp.tanh(log_a) * (a * a + 1.0), 0.0)) * (ig * xc)
        n_groups = blk // RG_GROUP
        a = a.reshape(n_groups, RG_GROUP, w)
        b = b.reshape(n_groups, RG_GROUP, w)
        s = 1
        while s < RG_GROUP:
            shift = s if d == 0 else RG_GROUP - s
            keep = (sub >= s) if d == 0 else (sub < RG_GROUP - s)
            b = jnp.where(keep, a * pltpu.roll(b, shift, 1), 0.0) + b
            a = jnp.where(keep, a * pltpu.roll(a, shift, 1), a)
            s *= 2
        for gi in range(n_groups):
            gj = gi if d == 0 else n_groups - 1 - gi
            hg = a[gj] * hin + b[gj]
            out_ref[pl.ds(t0 + gj * RG_GROUP, RG_GROUP), :] = hg
            hin = hg[RG_GROUP - 1:RG_GROUP, :] if d == 0 else hg[0:1, :]
        return hin

    def step(i, carry):
        h_f, h_b = carry
        bb = jnp.where(i < n_ctx_blocks, n_ctx_blocks - 1 - i, n_blocks - 1 - (i - n_ctx_blocks))
        h_f = scan_block(0, pl.multiple_of(i * blk, blk), h_f, o_ref)
        h_b = scan_block(1, pl.multiple_of(bb * blk, blk), h_b, ob_ref)
        return h_f, h_b

    zero = jnp.zeros((1, w), F32)
    lax.fori_loop(0, n_blocks, step, (zero, zero))
    o_ref[...] += ob_ref[...]


def _blockdiag_dense(w):
    nd, nb, k, _ = w.shape
    eye = jnp.eye(nb, dtype=w.dtype)
    return jnp.einsum('dnij,nm->dnimj', w, eye).reshape(nd, nb * k, nb * k)


def _rglru(p3, col_block, conv_w, conv_b, wa, ba, wx, bx, lam, n_ctx):
    bsz, t, _ = p3.shape
    w = conv_w.shape[1]
    full = lambda shape: pl.BlockSpec(shape, lambda b: (0,) * len(shape))
    kern = functools.partial(_rg_kernel, n_ctx=n_ctx)
    return pl.pallas_call(
        kern, grid=(bsz,),
        in_specs=[pl.BlockSpec((None, t, w), lambda b: (b, 0, col_block)),
                  full((RG_CONV, w)), full((1, w)),
                  full((2, w, w)), full((2, w)), full((2, w, w)), full((2, w)), full((2, w))],
        out_specs=pl.BlockSpec((None, t, w), lambda b: (b, 0, 0)),
        out_shape=jax.ShapeDtypeStruct((bsz, t, w), F32),
        scratch_shapes=[pltpu.VMEM((t + 3 * PAD, w), F32), pltpu.VMEM((t, w), F32), pltpu.VMEM((t, w), F32)],
        compiler_params=_cparams("parallel"), name="rglru",
    )(p3, conv_w, conv_b.reshape(1, w), _blockdiag_dense(wa).astype(BF16), ba,
      _blockdiag_dense(wx).astype(BF16), bx, lam)


def _merge_kernel(hg_ref, rg_ref, gate_ref, y_ref, og_ref, w_ref, h_ref, gtc_ref, gt_ref, o_ref,
                  *, n_ctx, tiles_per_batch):
    hg = hg_ref[...]
    half = hg.shape[1]
    hn = hg * lax.rsqrt(jnp.mean(hg * hg, axis=-1, keepdims=True) + EPS) * og_ref[...]
    a = (hn * _silu(gate_ref[...])).astype(BF16)
    b = (rg_ref[...] * _gelu_tanh(y_ref[...])).astype(BF16)
    o = _dot(a, w_ref[0:half, :]) + _dot(b, w_ref[half:, :])
    gate = _row_gate(gtc_ref, gt_ref, h_ref.shape, pl.program_id(0) % tiles_per_batch, n_ctx)
    o_ref[...] = h_ref[...] + gate * o


def _merge_out(hg, rg, p, onorm_g, w_out, j, h, mods, n_batch, t, n_ctx):
    n, d = h.shape
    half = hg.shape[1]
    tm = ROW_TILE
    tpb = t // tm
    blk = lambda c: pl.BlockSpec((tm, half), lambda i: (i, c))
    kern = functools.partial(_merge_kernel, n_ctx=n_ctx, tiles_per_batch=tpb)
    return pl.pallas_call(
        kern, grid=(n // tm,),
        in_specs=[blk(0), blk(0), blk(4), blk(6),
                  pl.BlockSpec((1, half), lambda i: (0, 0)),
                  pl.BlockSpec((None, d, d), lambda i: (j, 0, 0)),
                  pl.BlockSpec((tm, d), lambda i: (i, 0))] + _stream_specs(d, n_batch, tpb, 2, 1),
        out_specs=pl.BlockSpec((tm, d), lambda i: (i, 0)),
        out_shape=jax.ShapeDtypeStruct((n, d), F32),
        compiler_params=_cparams("parallel"), name="merge_out",
    )(hg, rg, p, p, onorm_g.reshape(1, half), w_out, h, mods, mods)


def _na_bias_table(rpb):
    q = np.arange(GRID_W)[:, None]
    kc = np.arange(GRID_W)[None, :]
    wstart = np.clip(q - NA_KC // 2, 0, GRID_W - NA_KC)
    mask = (kc >= wstart) & (kc < wstart + NA_KC)
    coff = np.clip(kc - q + NA_KC - 1, 0, 2 * NA_KC - 2)
    sel = (coff[None] == np.arange(2 * NA_KC - 1)[:, None, None]).astype(np.float32)
    toe = jnp.einsum('hdj,jqk->hqdk', rpb.astype(F32), sel, precision=lax.Precision.HIGHEST)
    toe = jnp.where(mask[None, :, None, :], toe, NEG_BIG)
    h = toe.shape[0]
    tab = jnp.stack([toe[:, :, off:off + NA_KR].reshape(h // 2, 2 * GRID_W, NA_KR * GRID_W)
                     for off in range(NA_KR)], axis=1)
    return tab


def _stack_heads(x, hd):
    lane = lax.broadcasted_iota(jnp.int32, x.shape, 1)
    zero = jnp.zeros_like(x)
    return jnp.concatenate([jnp.where(lane < hd, x, zero), jnp.where(lane >= hd, x, zero)], axis=0)


def _unstack_heads(o2, hd):
    n = o2.shape[0] // 2
    lane = lax.broadcasted_iota(jnp.int32, (n, o2.shape[1]), 1)
    return jnp.where(lane < hd, o2[:n], o2[n:])


def _na_kernel(q_ref, k_ref, v_ref, qg_ref, kg_ref, bias_ref, o_ref, qn_ref, kn_ref,
               s_ref, p_ref, den_ref, *, n_ctx, scale):
    t, w = q_ref.shape
    hd = w // 2
    rows_grid = (t - n_ctx) // GRID_W
    r_i = lax.broadcasted_iota(jnp.int32, (w, w), 0)
    c_i = lax.broadcasted_iota(jnp.int32, (w, w), 1)
    seg = jnp.where((r_i // hd) == (c_i // hd), 1.0 / hd, 0.0).astype(BF16)

    def normed(x, g):
        ms = _dot((x * x).astype(BF16), seg)
        return (x * lax.rsqrt(ms + EPS) * g).astype(BF16)

    qn_ref[...] = normed(q_ref[...].astype(F32), qg_ref[...] * scale)
    kn_ref[...] = normed(k_ref[...].astype(F32), kg_ref[...])
    vb_ref = v_ref

    kc = kn_ref[0:n_ctx, :]
    vc = vb_ref[0:n_ctx, :]

    s = _dot_nt(_stack_heads(qn_ref[0:n_ctx, :], hd), kc)
    p = jnp.exp(s - jnp.max(s, axis=-1, keepdims=True))
    o2 = _dot(p.astype(BF16), vc) / jnp.sum(p, axis=-1, keepdims=True)
    o_ref[0:n_ctx, :] = _unstack_heads(o2, hd).astype(o_ref.dtype)

    kr = NA_KR
    n_win = kr * GRID_W

    def window(r):
        r = jnp.clip(r, 0, rows_grid - 1)
        rs = jnp.clip(r - kr // 2, 0, rows_grid - kr)
        q0 = pl.multiple_of(n_ctx + r * GRID_W, GRID_W)
        k0 = pl.multiple_of(n_ctx + rs * GRID_W, GRID_W)
        return q0, k0, rs - r + (kr - 1)

    gap = NA_SLOTS // 2
    for slot in range(NA_SLOTS):
        s_ref[slot] = jnp.zeros(s_ref.shape[1:], F32)
        p_ref[slot] = jnp.zeros(p_ref.shape[1:], BF16)
        den_ref[slot] = jnp.ones(den_ref.shape[1:], F32)

    def step(i, slot):
        q0, k0, _ = window(i - 2 * gap)
        pr = p_ref[slot]
        o2 = (_dot(pr[:, :n_win], vb_ref[pl.ds(k0, n_win), :]) + _dot(pr[:, n_win:], vc)) / den_ref[slot]
        o_ref[pl.ds(q0, GRID_W), :] = _unstack_heads(o2, hd).astype(o_ref.dtype)

        mid = (slot + gap) % NA_SLOTS
        sc = s_ref[mid]
        pe = jnp.exp(sc - jnp.max(sc, axis=-1, keepdims=True))
        den_ref[mid] = jnp.sum(pe, axis=-1, keepdims=True)
        p_ref[mid] = pe.astype(BF16)

        q0, k0, off = window(i)
        q2 = _stack_heads(qn_ref[pl.ds(q0, GRID_W), :], hd)
        s_ref[slot, :, :n_win] = _dot_nt(q2, kn_ref[pl.ds(k0, n_win), :]) + bias_ref[off]
        s_ref[slot, :, n_win:] = _dot_nt(q2, kc)

    def steps(j, carry):
        for slot in range(NA_SLOTS):
            step(NA_SLOTS * j + slot, slot)
        return carry

    lax.fori_loop(0, (rows_grid + 2 * gap) // NA_SLOTS, steps, 0)


def _na_attention(qkv3, q_g, k_g, rpb, n_ctx):
    bsz, t, d3 = qkv3.shape
    d = d3 // 3
    hd = d // NA_HEADS
    w = 2 * hd
    pairs = d // w
    rows_grid = (t - n_ctx) // GRID_W
    assert w == LANES and rows_grid >= NA_KR and rows_grid % NA_SLOTS == 0
    n_keys = NA_KR * GRID_W + n_ctx
    bias = _na_bias_table(rpb)
    kern = functools.partial(_na_kernel, n_ctx=n_ctx, scale=hd ** -0.5)

    def col(group):
        return pl.BlockSpec((None, t, w), lambda hp, b: (b, 0, group * pairs + hp))

    gain = lambda g: jnp.tile(g.astype(F32), 2).reshape(1, w)
    return pl.pallas_call(
        kern, grid=(pairs, bsz),
        in_specs=[col(0), col(1), col(2),
                  pl.BlockSpec((1, w), lambda hp, b: (0, 0)),
                  pl.BlockSpec((1, w), lambda hp, b: (0, 0)),
                  pl.BlockSpec((None,) + bias.shape[1:], lambda hp, b: (hp, 0, 0, 0))],
        out_specs=pl.BlockSpec((None, t, w), lambda hp, b: (b, 0, hp)),
        out_shape=jax.ShapeDtypeStruct((bsz, t, d), BF16),
        scratch_shapes=[pltpu.VMEM((t, w), BF16)] * 2
        + [pltpu.VMEM((NA_SLOTS, 2 * GRID_W, n_keys), F32), pltpu.VMEM((NA_SLOTS, 2 * GRID_W, n_keys), BF16),
           pltpu.VMEM((NA_SLOTS, 2 * GRID_W, 1), F32)],
        compiler_params=_cparams("parallel", "parallel"), name="na_attention",
    )(qkv3, qkv3, qkv3, gain(q_g), gain(k_g), bias)


FF_TILE = 1408


def _swiglu_kernel(h_ref, g_ref, shc_ref, sh_ref, scc_ref, sc_ref, w1_ref, w3_ref, w2_ref, gtc_ref, gt_ref,
                   o_ref, xb_ref, acc_ref, *, n_ctx, tiles_per_batch):
    f = pl.program_id(1)
    tile_in_batch = pl.program_id(0) % tiles_per_batch

    @pl.when(f == 0)
    def _():
        xb_ref[...] = _modulated_rows(h_ref, g_ref, shc_ref, sh_ref, scc_ref, sc_ref,
                                      tile_in_batch, n_ctx).astype(BF16)
        acc_ref[...] = jnp.zeros_like(acc_ref)

    x = xb_ref[...]
    hid = _silu(_dot(x, w1_ref[...])) * _dot(x, w3_ref[...])
    acc_ref[...] += _dot(hid.astype(BF16), w2_ref[...])

    @pl.when(f == pl.num_programs(1) - 1)
    def _():
        gate = _row_gate(gtc_ref, gt_ref, acc_ref.shape, tile_in_batch, n_ctx)
        o_ref[...] = h_ref[...] + gate * acc_ref[...]


def _swiglu_residual(h, g, w1, w3, w2, j, mods, n_batch, t, n_ctx):
    n, d = h.shape
    dff = w1.shape[2]
    tm, tf = ROW_TILE, FF_TILE
    tpb = t // tm
    kern = functools.partial(_swiglu_kernel, n_ctx=n_ctx, tiles_per_batch=tpb)
    return pl.pallas_call(
        kern, grid=(n // tm, dff // tf),
        in_specs=[pl.BlockSpec((tm, d), lambda i, f: (i, 0)),
                  pl.BlockSpec((1, d), lambda i, f: (0, 0))]
        + _stream_specs(d, n_batch, tpb, 3, 2) + _stream_specs(d, n_batch, tpb, 4, 2)
        + [pl.BlockSpec((None, d, tf), lambda i, f: (j, 0, f)),
           pl.BlockSpec((None, d, tf), lambda i, f: (j, 0, f)),
           pl.BlockSpec((None, tf, d), lambda i, f: (j, f, 0))] + _stream_specs(d, n_batch, tpb, 5, 2),
        out_specs=pl.BlockSpec((tm, d), lambda i, f: (i, 0)),
        out_shape=jax.ShapeDtypeStruct((n, d), F32),
        scratch_shapes=[pltpu.VMEM((tm, d), BF16), pltpu.VMEM((tm, d), F32)],
        compiler_params=_cparams("parallel", "arbitrary"), name="swiglu",
    )(h, g, mods, mods, mods, mods, w1, w3, w2, mods, mods)


MOE_ROWS = 512


def _route_tables(route, tm, n_exp):
    n = route.shape[0]
    e_pair = route[:, 0:2].astype(jnp.int32).reshape(-1)
    onehot = (e_pair[:, None] == jnp.arange(n_exp, dtype=jnp.int32)[None, :]).astype(jnp.int32)
    csum = jnp.cumsum(onehot, axis=0)
    rank = jnp.sum((csum - onehot) * onehot, axis=1)
    counts = csum[-1]
    padded = ((counts + tm - 1) // tm) * tm
    ends = jnp.cumsum(padded)
    slot = jnp.sum(onehot * (ends - padded)[None, :], axis=1) + rank
    p_slots = 2 * n + n_exp * tm
    n_tiles = p_slots // tm
    n_used = ends[-1] // tm
    tile = jnp.arange(n_tiles, dtype=jnp.int32)
    te = jnp.sum(((tile * tm)[:, None] >= ends[None, :]).astype(jnp.int32), axis=1)
    te = jnp.where(tile < n_used, te, te[n_used - 1])
    return slot.reshape(n, 2), p_slots, te, n_used.reshape(1)


def _gather_rows(table, idx):
    p = idx.shape[0]
    m = table.shape[1]
    mesh = plsc.VectorSubcoreMesh(core_axis_name="c", subcore_axis_name="s")

    @functools.partial(pl.kernel, out_type=jax.ShapeDtypeStruct((p, m), table.dtype), mesh=mesh)
    def gather(table_hbm, idx_hbm, out_hbm):
        def body(idx_vmem, out_vmem):
            pltpu.sync_copy(table_hbm.at[idx_vmem.at[0]], out_vmem)

        pltpu.emit_pipeline(
            body, grid=(p // SC_WINDOW,),
            in_specs=[pl.BlockSpec((1, SC_WINDOW), lambda i: (0, i))],
            out_specs=[pl.BlockSpec((SC_WINDOW, m), lambda i: (i, 0))],
            core_axis_name=("c", "s"), dimension_semantics=(pltpu.PARALLEL,),
        )(idx_hbm, out_hbm)

    return gather(table, idx.reshape(1, p))


def _scatter_rows(rows, idx_a, idx_b, n_out):
    q, m = rows.shape
    mesh = plsc.VectorSubcoreMesh(core_axis_name="c", subcore_axis_name="s")

    @functools.partial(pl.kernel, out_type=jax.ShapeDtypeStruct((n_out, m), rows.dtype), mesh=mesh)
    def scatter(rows_hbm, ia_hbm, ib_hbm, out_hbm):
        def body(rows_vmem, ia_vmem, ib_vmem):
            pltpu.sync_copy(rows_vmem, out_hbm.at[ia_vmem.at[0]])
            pltpu.sync_copy(rows_vmem, out_hbm.at[ib_vmem.at[0]])

        pltpu.emit_pipeline(
            body, grid=(q // SC_WINDOW,),
            in_specs=[pl.BlockSpec((SC_WINDOW, m), lambda i: (i, 0)),
                      pl.BlockSpec((1, SC_WINDOW), lambda i: (0, i)),
                      pl.BlockSpec((1, SC_WINDOW), lambda i: (0, i))],
            out_specs=[],
            core_axis_name=("c", "s"), dimension_semantics=(pltpu.PARALLEL,),
        )(rows_hbm, ia_hbm, ib_hbm)

    return scatter(rows, idx_a.reshape(1, q), idx_b.reshape(1, q))


def _scatter_pieces(pieces, slot, n_slots):
    npc, r, m = pieces.shape
    offs = jnp.arange(npc, dtype=jnp.int32)[:, None] * n_slots
    idx_a = (slot[None, :, 0] + offs).reshape(-1)
    idx_b = (slot[None, :, 1] + offs).reshape(-1)
    return _scatter_rows(pieces.reshape(npc * r, m), idx_a, idx_b, npc * n_slots).reshape(npc, n_slots, m)


def _gather_pieces(pieces, idx):
    npc, v, m = pieces.shape
    offs = (jnp.arange(npc, dtype=jnp.int32) * v).reshape((npc,) + (1,) * idx.ndim)
    flat = (idx[None] + offs).reshape(-1)
    return _gather_rows(pieces.reshape(npc * v, m), flat).reshape((npc,) + idx.shape + (m,))


def _moe_ffn_kernel(te_ref, nu_ref, x_ref, w1_ref, w3_ref, w2_ref, y_ref, xb_ref, acc_ref):
    i = pl.program_id(0)
    f = pl.program_id(1)

    @pl.when(i < nu_ref[0])
    def _():
        @pl.when(f == 0)
        def _():
            xb_ref[...] = _load_unpacked(x_ref).astype(BF16)
            acc_ref[...] = jnp.zeros_like(acc_ref)

        x = xb_ref[...]
        hid = _silu(_dot(x, w1_ref[...])) * _dot(x, w3_ref[...])
        acc_ref[...] += _dot(hid.astype(BF16), w2_ref[...])

        @pl.when(f == pl.num_programs(1) - 1)
        def _():
            _store_pieces(y_ref, _pack_halves(acc_ref[...]))


def _moe_ffn(xs, te, n_used, w1, w3, w2, j):
    npc, p, m = xs.shape
    d = 2 * npc * m
    dff = w1.shape[3]
    tm, tf = MOE_ROWS, FF_TILE
    nf = dff // tf

    def row(i, f, te_ref, nu_ref):
        return (0, jnp.minimum(i, nu_ref[0] - 1), 0)

    def fcol(i, f, nu_ref):
        return jnp.where(i < nu_ref[0], f, nf - 1)

    grid_spec = pltpu.PrefetchScalarGridSpec(
        num_scalar_prefetch=2, grid=(p // tm, nf),
        in_specs=[pl.BlockSpec((npc, tm, m), row),
                  pl.BlockSpec((None, None, d, tf), lambda i, f, te_ref, nu_ref: (j, te_ref[i], 0, fcol(i, f, nu_ref))),
                  pl.BlockSpec((None, None, d, tf), lambda i, f, te_ref, nu_ref: (j, te_ref[i], 0, fcol(i, f, nu_ref))),
                  pl.BlockSpec((None, None, tf, d), lambda i, f, te_ref, nu_ref: (j, te_ref[i], fcol(i, f, nu_ref), 0))],
        out_specs=pl.BlockSpec((npc, tm, m), row),
        scratch_shapes=[pltpu.VMEM((tm, d), BF16), pltpu.VMEM((tm, d), F32)])
    return pl.pallas_call(
        _moe_ffn_kernel, grid_spec=grid_spec,
        out_shape=jax.ShapeDtypeStruct((npc, p, m), jnp.int32),
        compiler_params=_cparams("arbitrary", "arbitrary"), name="moe_ffn",
    )(te, n_used, xs, w1, w3, w2)


def _moe_combine_kernel(ya_ref, yb_ref, route_ref, h_ref, gt_ref, o_ref):
    r = route_ref[...]
    y = r[:, 2:3] * _load_unpacked(ya_ref) + r[:, 3:4] * _load_unpacked(yb_ref)
    o_ref[...] = h_ref[...] + gt_ref[...] * y


def _moe_combine(yg, route, h, mods, n_batch, tiles_per_batch, ctx_tiles):
    d = h.shape[1]
    npc, _, r, m = yg.shape
    tm = TOKEN_TILE
    per = tiles_per_batch - ctx_tiles
    h_idx = lambda i: ((i // per) * tiles_per_batch + ctx_tiles + i % per, 0)
    if ctx_tiles == 0:
        mod_idx = _mod_row_spec(5, tiles_per_batch, n_batch)
    else:
        mod_idx = lambda i: (i // per, 5, 0, 0)
    return pl.pallas_call(
        _moe_combine_kernel, grid=(r // tm,),
        in_specs=[pl.BlockSpec((npc, None, tm, m), lambda i: (0, 0, i, 0)),
                  pl.BlockSpec((npc, None, tm, m), lambda i: (0, 1, i, 0)),
                  pl.BlockSpec((tm, LANES), lambda i: (i, 0)),
                  pl.BlockSpec((tm, d), h_idx),
                  pl.BlockSpec((None, None, 1, d), mod_idx)],
        out_specs=pl.BlockSpec((tm, d), lambda i: (i, 0)),
        out_shape=jax.ShapeDtypeStruct((r, d), F32),
        compiler_params=_cparams("parallel"), name="moe_combine",
    )(yg, yg, route, h, mods)


def _moe_residual(u_pieces, route, w1, w3, w2, j, h, mods, n_batch, t, n_ctx, latent_only):
    if latent_only:
        npc, _, m = u_pieces.shape
        u_pieces = u_pieces.reshape(npc, n_batch, t, m)[:, :, n_ctx:].reshape(npc, -1, m)
        route = route.reshape(n_batch, t, LANES)[:, n_ctx:].reshape(-1, LANES)
    slot, n_slots, te, n_used = _route_tables(route, MOE_ROWS, w1.shape[1])
    xs = _scatter_pieces(u_pieces, slot, n_slots)
    ys = _moe_ffn(xs, te, n_used, w1, w3, w2, j)
    yg = _gather_pieces(ys, slot.T)
    ctx_tiles = n_ctx // TOKEN_TILE if latent_only else 0
    return _moe_combine(yg, route, h, mods, n_batch, t // TOKEN_TILE, ctx_tiles)


def kernel(x, c, ctx, c_ctx, w_mod, b_mod, norm_mix_g, norm_ffn_g, ev_w_in, ev_w_out, hg_lb_logits, hg_onorm_g, rg_conv_w, rg_conv_b, rg_wa, rg_ba, rg_wx, rg_bx, rg_lambda, na_w_qkv, na_w_o, na_q_g, na_k_g, na_rpb, ffn_w1, ffn_w3, ffn_w2, moe_router, moe_w1, moe_w3, moe_w2):
    bsz, seq, d = x.shape
    n_ctx = ctx.shape[1]
    t = n_ctx + seq
    depth = w_mod.shape[0]
    tpb = t // TOKEN_TILE
    assert n_ctx % TOKEN_TILE == 0 and seq % TOKEN_TILE == 0 and t % ROW_TILE == 0 and n_ctx <= ROW_TILE

    rows = -(-(bsz + 1) // 8) * 8
    cond = jnp.zeros((rows, d), F32).at[:bsz].set(c).at[bsz].set(c_ctx)
    mods_all = _adaln(cond, w_mod, b_mod).reshape(depth, rows, N_MOD, 1, d)

    lb_p = jax.nn.softmax(hg_lb_logits.astype(F32), axis=0)
    lb_all = jnp.cumsum(lb_p, axis=0) - lb_p[0]

    bf = lambda w: w.astype(BF16)
    w_in, w_out, w_qkv, w_o = bf(ev_w_in), bf(ev_w_out), bf(na_w_qkv), bf(na_w_o)
    f_w1, f_w3, f_w2 = bf(ffn_w1), bf(ffn_w3), bf(ffn_w2)
    m_w1, m_w3, m_w2 = bf(moe_w1), bf(moe_w3), bf(moe_w2)

    h = jnp.concatenate([ctx, x], axis=1).reshape(bsz * t, d)
    for layer in range(depth):
        j = layer // 2
        mods = mods_all[layer]
        g_mix = norm_mix_g[layer].reshape(1, d)
        if layer % 2 == 0:
            p = _modulated_matmul(h, g_mix, mods, w_in, j, F32, ROW_TILE // 2, bsz, t, n_ctx)
            p3 = p.reshape(bsz, t, -1)
            hg = _hgrn2(p3, lb_all[j], n_ctx)
            half = hg.shape[-1]
            rg = _rglru(p3, 5 * half // (d - half), rg_conv_w[j], rg_conv_b[j], rg_wa[j], rg_ba[j],
                        rg_wx[j], rg_bx[j], rg_lambda[j], n_ctx)
            h = _merge_out(hg.reshape(bsz * t, half), rg.reshape(bsz * t, d - half), p, hg_onorm_g[j],
                           w_out, j, h, mods, bsz, t, n_ctx)
        else:
            qkv = _modulated_matmul(h, g_mix, mods, w_qkv, j, BF16, ROW_TILE, bsz, t, n_ctx)
            o = _na_attention(qkv.reshape(bsz, t, 3 * d), na_q_g[j], na_k_g[j], na_rpb[j], n_ctx)
            h = _matmul_residual(o.reshape(bsz * t, d), w_o, j, h, mods, 2, bsz, t, n_ctx)
        if layer % 2 == 0:
            h = _swiglu_residual(h, norm_ffn_g[layer].reshape(1, d), f_w1, f_w3, f_w2, j, mods, bsz, t, n_ctx)
        else:
            u, route = _modulate_router(h, norm_ffn_g[layer].reshape(1, d), mods, 3, bsz, tpb, moe_router[j])
            h = _moe_residual(u, route, m_w1, m_w3, m_w2, j, h, mods, bsz, t, n_ctx, layer == depth - 1)
    if depth % 2 == 0:
        return h.reshape(bsz, seq, d)
    return h.reshape(bsz, t, d)[:, n_ctx:]
```

```python
import functools

import numpy as np
import jax
import jax.numpy as jnp
from jax import lax
from jax.experimental import pallas as pl
from jax.experimental.pallas import tpu as pltpu
from jax.experimental.pallas import tpu_sc as plsc

F32 = jnp.float32
BF16 = jnp.bfloat16

EPS = 1e-6
NEG_BIG = -1e9
LB_MIN = 1e-30
LOG2_E = 1.4426950408889634
N_MOD = 6
GRID_W = 64
HG_HEAD_DIM = 128
RG_BLOCKS = 8
RG_CONV = 4
RG_C = 8.0
NA_HEADS = 16
NA_KR = 8
NA_KC = 16
N_EXPERTS = 8

LANES = 128
SC_WINDOW = 128
SC_PIECES = 2
TOKEN_TILE = 256
HG_CHUNK = 128
HG_LEVELS = 7
NA_SLOTS = 4
RG_SCAN_BLOCK = 256
RG_GROUP = 8
VMEM_LIMIT = 56 * 1024 * 1024


def _cparams(*sem):
    return pltpu.CompilerParams(dimension_semantics=sem, vmem_limit_bytes=VMEM_LIMIT)


def _dot(a, b):
    return jnp.dot(a, b, preferred_element_type=F32)


def _dot_nt(a, b):
    return lax.dot_general(a, b, (((1,), (1,)), ((), ())), preferred_element_type=F32)


def _dot_tn(a, b):
    return lax.dot_general(a, b, (((0,), (0,)), ((), ())), preferred_element_type=F32)


def _split2(x):
    hi = x.astype(BF16)
    lo = (x - hi.astype(F32)).astype(BF16)
    return hi, lo


def _sigmoid(x):
    return 0.5 * jnp.tanh(0.5 * x) + 0.5


def _silu(x):
    return x * _sigmoid(x)


def _log_sigmoid(x):
    return jnp.minimum(x, 0.0) - jnp.log1p(jnp.exp(-jnp.abs(x)))


def _gelu_tanh(x):
    return 0.5 * x * (1.0 + jnp.tanh(np.sqrt(2.0 / np.pi) * (x + 0.044715 * (x * x * x))))


def _adaln_kernel(c_ref, w_ref, b_ref, o_ref):
    a_hi, a_lo = _split2(_silu(c_ref[...]))
    w_hi, w_lo = _split2(w_ref[...])
    o_ref[...] = _dot(a_hi, w_hi) + _dot(a_hi, w_lo) + _dot(a_lo, w_hi) + b_ref[...]


def _adaln(cond, w_mod, b_mod):
    depth, d, n = w_mod.shape
    r = cond.shape[0]
    tn = 1536
    return pl.pallas_call(
        _adaln_kernel,
        grid=(depth, n // tn),
        in_specs=[pl.BlockSpec((r, d), lambda l, j: (0, 0)),
                  pl.BlockSpec((None, d, tn), lambda l, j: (l, 0, j)),
                  pl.BlockSpec((None, 1, tn), lambda l, j: (l, 0, j))],
        out_specs=pl.BlockSpec((None, r, tn), lambda l, j: (l, 0, j)),
        out_shape=jax.ShapeDtypeStruct((depth, r, n), F32),
        compiler_params=_cparams("parallel", "parallel"),
        name="adaln",
    )(cond, w_mod, b_mod.reshape(depth, 1, n))


def _mod_row_spec(k, tiles_per_batch, n_batch):
    def idx(i):
        row = jnp.where(i % tiles_per_batch == 0, n_batch, i // tiles_per_batch)
        return (row, k, 0, 0)
    return idx


def _modulated(h_ref, g_ref, sh_ref, sc_ref):
    x = h_ref[...]
    y = x * lax.rsqrt(jnp.mean(x * x, axis=-1, keepdims=True) + EPS)
    return (y * g_ref[...]) * (1.0 + sc_ref[...]) + sh_ref[...]


def _pack_halves(x):
    m = x.shape[1] // 2
    lo = lax.bitcast_convert_type(x[:, :m].astype(BF16).astype(F32), jnp.int32)
    hi = lax.bitcast_convert_type(x[:, m:].astype(BF16).astype(F32), jnp.int32)
    return lax.shift_right_logical(lo, 16) | (hi & jnp.int32(-65536))


def _unpack_halves(w):
    lo = lax.bitcast_convert_type(lax.shift_left(w, 16), F32)
    hi = lax.bitcast_convert_type(w & jnp.int32(-65536), F32)
    return lo, hi


def _store_pieces(ref, w):
    m = w.shape[1] // SC_PIECES
    for j in range(SC_PIECES):
        ref[j] = w[:, j * m:(j + 1) * m]


def _load_unpacked(ref):
    parts = [_unpack_halves(ref[j]) for j in range(SC_PIECES)]
    return jnp.concatenate([lo for lo, _ in parts] + [hi for _, hi in parts], axis=1)


def _modulate_router_kernel(h_ref, g_ref, sh_ref, sc_ref, wr_ref, u_ref, route_ref):
    u = _modulated(h_ref, g_ref, sh_ref, sc_ref)
    _store_pieces(u_ref, _pack_halves(u))
    u0, u1 = _split2(u)
    w0, w1 = _split2(wr_ref[...])
    logits = _dot(u0, w0) + _dot(u0, w1) + _dot(u1, w0)
    lane = lax.broadcasted_iota(jnp.int32, logits.shape, 1).astype(F32)
    lg = jnp.where(lane < N_EXPERTS, logits, -jnp.inf)
    m1 = jnp.max(lg, axis=-1, keepdims=True)
    i1 = jnp.min(jnp.where(lg == m1, lane, float(LANES)), axis=-1, keepdims=True)
    lg2 = jnp.where(lane == i1, -jnp.inf, lg)
    m2 = jnp.max(lg2, axis=-1, keepdims=True)
    i2 = jnp.min(jnp.where(lg2 == m2, lane, float(LANES)), axis=-1, keepdims=True)
    e2 = jnp.exp(m2 - m1)
    den = 1.0 + e2
    route_ref[...] = jnp.where(lane == 0, i1, jnp.where(lane == 1, i2, jnp.where(lane == 2, 1.0 / den, e2 / den)))


def _modulate_router(h, g, mods, k_shift, n_batch, tiles_per_batch, w_router):
    n, d = h.shape
    tm = TOKEN_TILE
    row = pl.BlockSpec((tm, d), lambda i: (i, 0))
    vec = pl.BlockSpec((1, d), lambda i: (0, 0))

    def mod_spec(k):
        return pl.BlockSpec((None, None, 1, d), _mod_row_spec(k, tiles_per_batch, n_batch))

    in_specs = [row, vec, mod_spec(k_shift), mod_spec(k_shift + 1)]
    wr = jnp.zeros((d, LANES), F32).at[:, :N_EXPERTS].set(w_router)
    return pl.pallas_call(
        _modulate_router_kernel, grid=(n // tm,),
        in_specs=in_specs + [pl.BlockSpec((d, LANES), lambda i: (0, 0))],
        out_specs=[pl.BlockSpec((SC_PIECES, tm, d // 2 // SC_PIECES), lambda i: (0, i, 0)),
                   pl.BlockSpec((tm, LANES), lambda i: (i, 0))],
        out_shape=[jax.ShapeDtypeStruct((SC_PIECES, n, d // 2 // SC_PIECES), jnp.int32),
                   jax.ShapeDtypeStruct((n, LANES), F32)],
        compiler_params=_cparams("parallel"), name="modulate_router",
    )(h, g, mods, mods, wr)


ROW_TILE = 768


def _row_gate(ctx_ref, own_ref, shape, tile_in_batch, n_ctx):
    rows = lax.broadcasted_iota(jnp.int32, shape, 0)
    is_ctx = (rows < n_ctx) & (tile_in_batch == 0)
    return jnp.where(is_ctx, ctx_ref[...], own_ref[...])


def _modulated_rows(h_ref, g_ref, shc_ref, sh_ref, scc_ref, sc_ref, tile_in_batch, n_ctx):
    x = h_ref[...]
    y = x * lax.rsqrt(jnp.mean(x * x, axis=-1, keepdims=True) + EPS)
    shift = _row_gate(shc_ref, sh_ref, x.shape, tile_in_batch, n_ctx)
    scale = _row_gate(scc_ref, sc_ref, x.shape, tile_in_batch, n_ctx)
    return (y * g_ref[...]) * (1.0 + scale) + shift


def _stream_specs(d, n_batch, tiles_per_batch, k, nargs):
    if nargs == 1:
        ctx = lambda i: (n_batch, k, 0, 0)
        own = lambda i: (i // tiles_per_batch, k, 0, 0)
    else:
        ctx = lambda i, f: (n_batch, k, 0, 0)
        own = lambda i, f: (i // tiles_per_batch, k, 0, 0)
    return [pl.BlockSpec((None, None, 1, d), ctx), pl.BlockSpec((None, None, 1, d), own)]


def _mod_mm_kernel(h_ref, g_ref, shc_ref, sh_ref, scc_ref, sc_ref, w_ref, o_ref, *, n_ctx, tiles_per_batch):
    x = _modulated_rows(h_ref, g_ref, shc_ref, sh_ref, scc_ref, sc_ref,
                        pl.program_id(0) % tiles_per_batch, n_ctx).astype(BF16)
    o_ref[...] = _dot(x, w_ref[...]).astype(o_ref.dtype)


def _modulated_matmul(h, g, mods, w, j, out_dtype, tm, n_batch, t, n_ctx):
    n, d = h.shape
    nout = w.shape[2]
    tpb = t // tm
    assert t % tm == 0 and n_ctx <= tm
    kern = functools.partial(_mod_mm_kernel, n_ctx=n_ctx, tiles_per_batch=tpb)
    return pl.pallas_call(
        kern, grid=(n // tm,),
        in_specs=[pl.BlockSpec((tm, d), lambda i: (i, 0)),
                  pl.BlockSpec((1, d), lambda i: (0, 0))]
        + _stream_specs(d, n_batch, tpb, 0, 1) + _stream_specs(d, n_batch, tpb, 1, 1)
        + [pl.BlockSpec((None, d, nout), lambda i: (j, 0, 0))],
        out_specs=pl.BlockSpec((tm, nout), lambda i: (i, 0)),
        out_shape=jax.ShapeDtypeStruct((n, nout), out_dtype),
        compiler_params=_cparams("parallel"), name="modulated_matmul",
    )(h, g, mods, mods, mods, mods, w)


def _mm_res_kernel(x_ref, w_ref, h_ref, gtc_ref, gt_ref, o_ref, *, n_ctx, tiles_per_batch):
    gate = _row_gate(gtc_ref, gt_ref, h_ref.shape, pl.program_id(0) % tiles_per_batch, n_ctx)
    o_ref[...] = h_ref[...] + gate * _dot(x_ref[...], w_ref[...])


def _matmul_residual(x, w, j, h, mods, k_gate, n_batch, t, n_ctx):
    n, d = h.shape
    kdim = x.shape[1]
    tm = ROW_TILE
    tpb = t // tm
    kern = functools.partial(_mm_res_kernel, n_ctx=n_ctx, tiles_per_batch=tpb)
    return pl.pallas_call(
        kern, grid=(n // tm,),
        in_specs=[pl.BlockSpec((tm, kdim), lambda i: (i, 0)),
                  pl.BlockSpec((None, kdim, d), lambda i: (j, 0, 0)),
                  pl.BlockSpec((tm, d), lambda i: (i, 0))] + _stream_specs(d, n_batch, tpb, k_gate, 1),
        out_specs=pl.BlockSpec((tm, d), lambda i: (i, 0)),
        out_shape=jax.ShapeDtypeStruct((n, d), F32),
        compiler_params=_cparams("parallel"), name="matmul_residual",
    )(x, w, h, mods, mods)


def _hg_tables():
    c = HG_CHUNK
    t = np.arange(c)[:, None]
    u = np.arange(c)[None, :]
    mats = [(u <= t), (u > t)]
    for k in range(1, HG_LEVELS + 1):
        m = ((t >> k) << k) + (1 << (k - 1)) - 1
        upper = t > m
        mats.append(np.where(upper, (u > m) & (u <= t), (u > t) & (u <= m)))
    fwd = np.concatenate(mats, axis=0).astype(np.float32)
    bwd = fwd.reshape(-1, c, c)[:, ::-1, ::-1].reshape(-1, c)
    x = t ^ u
    lvl = np.where(t > u, np.floor(np.log2(np.maximum(x, 1))).astype(np.int32) + 1, -1)
    lvl = np.where(t == u, 0, lvl).astype(np.int32)
    return np.stack([fwd, bwd]), np.stack([lvl, lvl.T])


def _hg_kernel(q_ref, ff_ref, fb_ref, v_ref, lb_ref, m_ref, lvl_ref, o_ref, ob_ref,
               qk_ref, el_ref, io_ref, tot_ref, att_ref, *, n_ctx_chunks, n_chunks):
    c = HG_CHUNK
    z_refs = (ff_ref, fb_ref)
    out_refs = (o_ref, ob_ref)
    for ref in (qk_ref, el_ref, io_ref, tot_ref, att_ref):
        ref[...] = jnp.zeros(ref.shape, ref.dtype)

    def start_row(d, j):
        j = jnp.clip(j, 0, n_chunks - 1)
        if d == 1:
            j = jnp.where(j < n_ctx_chunks, n_ctx_chunks - 1 - j, n_chunks - 1 - (j - n_ctx_chunks))
        return pl.multiple_of(j * c, c)

    def stage1(d, j, slot):
        t0 = start_row(d, j)
        lb = lb_ref[d:d + 1, :]
        log_lb = jnp.log(jnp.maximum(lb, LB_MIN))
        q = _silu(q_ref[pl.ds(t0, c), :])
        z = z_refs[d][pl.ds(t0, c), :]
        k = (1.0 - lb) * _sigmoid(-z)
        b = jnp.log1p(-lb) + _log_sigmoid(z)
        logf = jnp.maximum(log_lb, b) + jnp.log1p(jnp.exp(-jnp.abs(log_lb - b)))
        g_hi, g_lo = _split2(logf * LOG2_E)
        m_ends = m_ref[d, 0:2 * c, :]
        e_ends = jnp.exp2(_dot(m_ends, g_hi) + _dot(m_ends, g_lo))
        el_ref[d, slot] = jnp.exp2(_dot(m_ref[d, 2 * c:, :], g_hi).astype(BF16))
        qk_ref[d, slot, 0] = q.astype(BF16)
        qk_ref[d, slot, 1] = k.astype(BF16)
        io_ref[d, slot, 0] = (q * e_ends[0:c]).astype(BF16)
        io_ref[d, slot, 1] = (k * e_ends[c:]).astype(BF16)
        tot_row = c - 1 if d == 0 else 0
        tot_ref[d, slot, 0:1, :] = e_ends[tot_row:tot_row + 1, :]

    def stage2(d, slot):
        qb = qk_ref[d, slot, 0]
        kb = qk_ref[d, slot, 1]
        lvl = lvl_ref[d]
        att = jnp.where(lvl == 0, _dot_nt(qb, kb), 0.0)
        for lev in range(1, HG_LEVELS + 1):
            el = el_ref[d, slot, (lev - 1) * c:lev * c, :]
            att = jnp.where(lvl == lev, _dot_nt(qb * el, kb * el), att)
        att_ref[d, slot] = att.astype(BF16)

    def stage3(d, j, slot, st):
        t0 = start_row(d, j)
        v = v_ref[pl.ds(t0, c), :].astype(BF16)
        out_refs[d][pl.ds(t0, c), :] = _dot(att_ref[d, slot], v) + _dot_nt(io_ref[d, slot, 0], st.astype(BF16))
        return st * tot_ref[d, slot, 0:1, :] + _dot_tn(v, io_ref[d, slot, 1])

    def two_steps(jj, carry):
        st = list(carry)
        for slot in range(2):
            i = 2 * jj + slot
            for d in range(2):
                st[d] = stage3(d, i - 2, slot, st[d])
            for d in range(2):
                stage2(d, 1 - slot)
            for d in range(2):
                stage1(d, i, slot)
        return tuple(st)

    zero = jnp.zeros((c, c), F32)
    lax.fori_loop(0, (n_chunks + 2) // 2, two_steps, (zero, zero))
    o_ref[...] += ob_ref[...]


def _hgrn2(p3, lb, n_ctx):
    bsz, t, _ = p3.shape
    width = lb.shape[1]
    heads = width // HG_HEAD_DIM
    mats, lvl = _hg_tables()
    c = HG_CHUNK

    def col(group):
        return pl.BlockSpec((None, t, c), lambda b, h: (b, 0, group * heads + h))

    kern = functools.partial(_hg_kernel, n_ctx_chunks=n_ctx // c, n_chunks=t // c)
    return pl.pallas_call(
        kern, grid=(bsz, heads),
        in_specs=[col(0), col(1), col(2), col(3),
                  pl.BlockSpec((2, c), lambda b, h: (0, h)),
                  pl.BlockSpec(mats.shape, lambda b, h: (0, 0, 0)),
                  pl.BlockSpec(lvl.shape, lambda b, h: (0, 0, 0))],
        out_specs=pl.BlockSpec((None, t, c), lambda b, h: (b, 0, h)),
        out_shape=jax.ShapeDtypeStruct((bsz, t, width), F32),
        scratch_shapes=[pltpu.VMEM((t, c), F32),
                        pltpu.VMEM((2, 2, 2, c, c), BF16), pltpu.VMEM((2, 2, HG_LEVELS * c, c), BF16),
                        pltpu.VMEM((2, 2, 2, c, c), BF16), pltpu.VMEM((2, 2, 8, c), F32),
                        pltpu.VMEM((2, 2, c, c), BF16)],
        compiler_params=_cparams("parallel", "parallel"), name="hgrn2",
    )(p3, p3, p3, p3, lb, jnp.asarray(mats, BF16), jnp.asarray(lvl))


PAD = 8


def _rg_kernel(x_ref, cw_ref, cb_ref, wa_ref, ba_ref, wx_ref, bx_ref, lam_ref, o_ref,
               xpad_ref, xc_ref, ob_ref, *, n_ctx):
    t, w = x_ref.shape
    n_lat = t - n_ctx
    lat0 = n_ctx + 2 * PAD
    xpad_ref[...] = jnp.zeros(xpad_ref.shape, F32)
    xpad_ref[PAD:PAD + n_ctx, :] = x_ref[0:n_ctx, :]
    xpad_ref[lat0:lat0 + n_lat, :] = x_ref[n_ctx:t, :]
    left = RG_CONV // 2
    for base, dst, n in ((PAD, 0, n_ctx), (lat0, n_ctx, n_lat)):
        acc = cb_ref[...] + cw_ref[0:1, :] * xpad_ref[base - left:base - left + n, :]
        for j in range(1, RG_CONV):
            acc = acc + cw_ref[j:j + 1, :] * xpad_ref[base - left + j:base - left + j + n, :]
        xc_ref[dst:dst + n, :] = acc

    blk = RG_SCAN_BLOCK
    n_blocks = t // blk
    n_ctx_blocks = n_ctx // blk
    sub = lax.broadcasted_iota(jnp.int32, (blk // RG_GROUP, RG_GROUP, w), 1)
    log_sig_lam = _log_sigmoid(lam_ref[...])

    def scan_block(d, t0, hin, out_ref):
        xc = xc_ref[pl.ds(t0, blk), :]
        xb = xc.astype(BF16)
        r = _sigmoid(_dot(xb, wa_ref[d]) + ba_ref[d:d + 1, :])
        ig = _sigmoid(_dot(xb, wx_ref[d]) + bx_ref[d:d + 1, :])
        log_a = RG_C * r * log_sig_lam[d:d + 1, :]
        a = jnp.exp(log_a)
        b = jnp.sqrt(jnp.maximum(-jnp.tanh(log_a) * (a * a + 1.0), 0.0)) * (ig * xc)
        n_groups = blk // RG_GROUP
        a = a.reshape(n_groups, RG_GROUP, w)
        b = b.reshape(n_groups, RG_GROUP, w)
        s = 1
        while s < RG_GROUP:
            shift = s if d == 0 else RG_GROUP - s
            keep = (sub >= s) if d == 0 else (sub < RG_GROUP - s)
            b = jnp.where(keep, a * pltpu.roll(b, shift, 1), 0.0) + b
            a = jnp.where(keep, a * pltpu.roll(a, shift, 1), a)
            s *= 2
        for gi in range(n_groups):
            gj = gi if d == 0 else n_groups - 1 - gi
            hg = a[gj] * hin + b[gj]
            out_ref[pl.ds(t0 + gj * RG_GROUP, RG_GROUP), :] = hg
            hin = hg[RG_GROUP - 1:RG_GROUP, :] if d == 0 else hg[0:1, :]
        return hin

    def step(i, carry):
        h_f, h_b = carry
        bb = jnp.where(i < n_ctx_blocks, n_ctx_blocks - 1 - i, n_blocks - 1 - (i - n_ctx_blocks))
        h_f = scan_block(0, pl.multiple_of(i * blk, blk), h_f, o_ref)
        h_b = scan_block(1, pl.multiple_of(bb * blk, blk), h_b, ob_ref)
        return h_f, h_b

    zero = jnp.zeros((1, w), F32)
    lax.fori_loop(0, n_blocks, step, (zero, zero))
    o_ref[...] += ob_ref[...]


def _blockdiag_dense(w):
    nd, nb, k, _ = w.shape
    eye = jnp.eye(nb, dtype=w.dtype)
    return jnp.einsum('dnij,nm->dnimj', w, eye).reshape(nd, nb * k, nb * k)


def _rglru(p3, col_block, conv_w, conv_b, wa, ba, wx, bx, lam, n_ctx):
    bsz, t, _ = p3.shape
    w = conv_w.shape[1]
    full = lambda shape: pl.BlockSpec(shape, lambda b: (0,) * len(shape))
    kern = functools.partial(_rg_kernel, n_ctx=n_ctx)
    return pl.pallas_call(
        kern, grid=(bsz,),
        in_specs=[pl.BlockSpec((None, t, w), lambda b: (b, 0, col_block)),
                  full((RG_CONV, w)), full((1, w)),
                  full((2, w, w)), full((2, w)), full((2, w, w)), full((2, w)), full((2, w))],
        out_specs=pl.BlockSpec((None, t, w), lambda b: (b, 0, 0)),
        out_shape=jax.ShapeDtypeStruct((bsz, t, w), F32),
        scratch_shapes=[pltpu.VMEM((t + 3 * PAD, w), F32), pltpu.VMEM((t, w), F32), pltpu.VMEM((t, w), F32)],
        compiler_params=_cparams("parallel"), name="rglru",
    )(p3, conv_w, conv_b.reshape(1, w), _blockdiag_dense(wa).astype(BF16), ba,
      _blockdiag_dense(wx).astype(BF16), bx, lam)


def _merge_kernel(hg_ref, rg_ref, gate_ref, y_ref, og_ref, w_ref, h_ref, gtc_ref, gt_ref, o_ref,
                  *, n_ctx, tiles_per_batch):
    hg = hg_ref[...]
    half = hg.shape[1]
    hn = hg * lax.rsqrt(jnp.mean(hg * hg, axis=-1, keepdims=True) + EPS) * og_ref[...]
    a = (hn * _silu(gate_ref[...])).astype(BF16)
    b = (rg_ref[...] * _gelu_tanh(y_ref[...])).astype(BF16)
    o = _dot(a, w_ref[0:half, :]) + _dot(b, w_ref[half:, :])
    gate = _row_gate(gtc_ref, gt_ref, h_ref.shape, pl.program_id(0) % tiles_per_batch, n_ctx)
    o_ref[...] = h_ref[...] + gate * o


def _merge_out(hg, rg, p, onorm_g, w_out, j, h, mods, n_batch, t, n_ctx):
    n, d = h.shape
    half = hg.shape[1]
    tm = ROW_TILE
    tpb = t // tm
    blk = lambda c: pl.BlockSpec((tm, half), lambda i: (i, c))
    kern = functools.partial(_merge_kernel, n_ctx=n_ctx, tiles_per_batch=tpb)
    return pl.pallas_call(
        kern, grid=(n // tm,),
        in_specs=[blk(0), blk(0), blk(4), blk(6),
                  pl.BlockSpec((1, half), lambda i: (0, 0)),
                  pl.BlockSpec((None, d, d), lambda i: (j, 0, 0)),
                  pl.BlockSpec((tm, d), lambda i: (i, 0))] + _stream_specs(d, n_batch, tpb, 2, 1),
        out_specs=pl.BlockSpec((tm, d), lambda i: (i, 0)),
        out_shape=jax.ShapeDtypeStruct((n, d), F32),
        compiler_params=_cparams("parallel"), name="merge_out",
    )(hg, rg, p, p, onorm_g.reshape(1, half), w_out, h, mods, mods)


def _na_bias_table(rpb):
    q = np.arange(GRID_W)[:, None]
    kc = np.arange(GRID_W)[None, :]
    wstart = np.clip(q - NA_KC // 2, 0, GRID_W - NA_KC)
    mask = (kc >= wstart) & (kc < wstart + NA_KC)
    coff = np.clip(kc - q + NA_KC - 1, 0, 2 * NA_KC - 2)
    sel = (coff[None] == np.arange(2 * NA_KC - 1)[:, None, None]).astype(np.float32)
    toe = jnp.einsum('hdj,jqk->hqdk', rpb.astype(F32), sel, precision=lax.Precision.HIGHEST)
    toe = jnp.where(mask[None, :, None, :], toe, NEG_BIG)
    h = toe.shape[0]
    tab = jnp.stack([toe[:, :, off:off + NA_KR].reshape(h // 2, 2 * GRID_W, NA_KR * GRID_W)
                     for off in range(NA_KR)], axis=1)
    return tab


def _stack_heads(x, hd):
    lane = lax.broadcasted_iota(jnp.int32, x.shape, 1)
    zero = jnp.zeros_like(x)
    return jnp.concatenate([jnp.where(lane < hd, x, zero), jnp.where(lane >= hd, x, zero)], axis=0)


def _unstack_heads(o2, hd):
    n = o2.shape[0] // 2
    lane = lax.broadcasted_iota(jnp.int32, (n, o2.shape[1]), 1)
    return jnp.where(lane < hd, o2[:n], o2[n:])


def _na_kernel(q_ref, k_ref, v_ref, qg_ref, kg_ref, bias_ref, o_ref, qn_ref, kn_ref,
               s_ref, p_ref, den_ref, *, n_ctx, scale):
    t, w = q_ref.shape
    hd = w // 2
    rows_grid = (t - n_ctx) // GRID_W
    r_i = lax.broadcasted_iota(jnp.int32, (w, w), 0)
    c_i = lax.broadcasted_iota(jnp.int32, (w, w), 1)
    seg = jnp.where((r_i // hd) == (c_i // hd), 1.0 / hd, 0.0).astype(BF16)

    def normed(x, g):
        ms = _dot((x * x).astype(BF16), seg)
        return (x * lax.rsqrt(ms + EPS) * g).astype(BF16)

    qn_ref[...] = normed(q_ref[...].astype(F32), qg_ref[...] * scale)
    kn_ref[...] = normed(k_ref[...].astype(F32), kg_ref[...])
    vb_ref = v_ref

    kc = kn_ref[0:n_ctx, :]
    vc = vb_ref[0:n_ctx, :]

    s = _dot_nt(_stack_heads(qn_ref[0:n_ctx, :], hd), kc)
    p = jnp.exp(s - jnp.max(s, axis=-1, keepdims=True))
    o2 = _dot(p.astype(BF16), vc) / jnp.sum(p, axis=-1, keepdims=True)
    o_ref[0:n_ctx, :] = _unstack_heads(o2, hd).astype(o_ref.dtype)

    kr = NA_KR
    n_win = kr * GRID_W

    def window(r):
        r = jnp.clip(r, 0, rows_grid - 1)
        rs = jnp.clip(r - kr // 2, 0, rows_grid - kr)
        q0 = pl.multiple_of(n_ctx + r * GRID_W, GRID_W)
        k0 = pl.multiple_of(n_ctx + rs * GRID_W, GRID_W)
        return q0, k0, rs - r + (kr - 1)

    gap = NA_SLOTS // 2
    for slot in range(NA_SLOTS):
        s_ref[slot] = jnp.zeros(s_ref.shape[1:], F32)
        p_ref[slot] = jnp.zeros(p_ref.shape[1:], BF16)
        den_ref[slot] = jnp.ones(den_ref.shape[1:], F32)

    def step(i, slot):
        q0, k0, _ = window(i - 2 * gap)
        pr = p_ref[slot]
        o2 = (_dot(pr[:, :n_win], vb_ref[pl.ds(k0, n_win), :]) + _dot(pr[:, n_win:], vc)) / den_ref[slot]
        o_ref[pl.ds(q0, GRID_W), :] = _unstack_heads(o2, hd).astype(o_ref.dtype)

        mid = (slot + gap) % NA_SLOTS
        sc = s_ref[mid]
        pe = jnp.exp(sc - jnp.max(sc, axis=-1, keepdims=True))
        den_ref[mid] = jnp.sum(pe, axis=-1, keepdims=True)
        p_ref[mid] = pe.astype(BF16)

        q0, k0, off = window(i)
        q2 = _stack_heads(qn_ref[pl.ds(q0, GRID_W), :], hd)
        s_ref[slot, :, :n_win] = _dot_nt(q2, kn_ref[pl.ds(k0, n_win), :]) + bias_ref[off]
        s_ref[slot, :, n_win:] = _dot_nt(q2, kc)

    def steps(j, carry):
        for slot in range(NA_SLOTS):
            step(NA_SLOTS * j + slot, slot)
        return carry

    lax.fori_loop(0, (rows_grid + 2 * gap) // NA_SLOTS, steps, 0)


def _na_attention(qkv3, q_g, k_g, rpb, n_ctx):
    bsz, t, d3 = qkv3.shape
    d = d3 // 3
    hd = d // NA_HEADS
    w = 2 * hd
    pairs = d // w
    rows_grid = (t - n_ctx) // GRID_W
    assert w == LANES and rows_grid >= NA_KR and rows_grid % NA_SLOTS == 0
    n_keys = NA_KR * GRID_W + n_ctx
    bias = _na_bias_table(rpb)
    kern = functools.partial(_na_kernel, n_ctx=n_ctx, scale=hd ** -0.5)

    def col(group):
        return pl.BlockSpec((None, t, w), lambda hp, b: (b, 0, group * pairs + hp))

    gain = lambda g: jnp.tile(g.astype(F32), 2).reshape(1, w)
    return pl.pallas_call(
        kern, grid=(pairs, bsz),
        in_specs=[col(0), col(1), col(2),
                  pl.BlockSpec((1, w), lambda hp, b: (0, 0)),
                  pl.BlockSpec((1, w), lambda hp, b: (0, 0)),
                  pl.BlockSpec((None,) + bias.shape[1:], lambda hp, b: (hp, 0, 0, 0))],
        out_specs=pl.BlockSpec((None, t, w), lambda hp, b: (b, 0, hp)),
        out_shape=jax.ShapeDtypeStruct((bsz, t, d), BF16),
        scratch_shapes=[pltpu.VMEM((t, w), BF16)] * 2
        + [pltpu.VMEM((NA_SLOTS, 2 * GRID_W, n_keys), F32), pltpu.VMEM((NA_SLOTS, 2 * GRID_W, n_keys), BF16),
           pltpu.VMEM((NA_SLOTS, 2 * GRID_W, 1), F32)],
        compiler_params=_cparams("parallel", "parallel"), name="na_attention",
    )(qkv3, qkv3, qkv3, gain(q_g), gain(k_g), bias)


FFN_ROWS = 384


def _swiglu_kernel(h_ref, g_ref, shc_ref, sh_ref, scc_ref, sc_ref, w1_ref, w3_ref, w2_ref, gtc_ref, gt_ref,
                   o_ref, *, n_ctx, tiles_per_batch):
    tile_in_batch = pl.program_id(0) % tiles_per_batch
    x = _modulated_rows(h_ref, g_ref, shc_ref, sh_ref, scc_ref, sc_ref, tile_in_batch, n_ctx).astype(BF16)
    hid = _silu(_dot(x, w1_ref[...])) * _dot(x, w3_ref[...])
    y = _dot(hid.astype(BF16), w2_ref[...])
    o_ref[...] = h_ref[...] + _row_gate(gtc_ref, gt_ref, y.shape, tile_in_batch, n_ctx) * y


def _resident(block_shape, index_map):
    return pl.BlockSpec(block_shape, index_map, pipeline_mode=pl.Buffered(1))


def _swiglu_residual(h, g, w1, w3, w2, j, mods, n_batch, t, n_ctx):
    n, d = h.shape
    dff = w1.shape[2]
    tm = FFN_ROWS
    tpb = t // tm
    assert t % tm == 0 and n_ctx <= tm
    kern = functools.partial(_swiglu_kernel, n_ctx=n_ctx, tiles_per_batch=tpb)
    return pl.pallas_call(
        kern, grid=(n // tm,),
        in_specs=[pl.BlockSpec((tm, d), lambda i: (i, 0)),
                  pl.BlockSpec((1, d), lambda i: (0, 0))]
        + _stream_specs(d, n_batch, tpb, 3, 1) + _stream_specs(d, n_batch, tpb, 4, 1)
        + [_resident((None, d, dff), lambda i: (j, 0, 0)),
           _resident((None, d, dff), lambda i: (j, 0, 0)),
           _resident((None, dff, d), lambda i: (j, 0, 0))] + _stream_specs(d, n_batch, tpb, 5, 1),
        out_specs=pl.BlockSpec((tm, d), lambda i: (i, 0)),
        out_shape=jax.ShapeDtypeStruct((n, d), F32),
        compiler_params=_cparams("parallel"), name="swiglu",
    )(h, g, mods, mods, mods, mods, w1, w3, w2, mods, mods)


MOE_ROWS = 512


def _route_tables(route, tm, n_exp):
    n = route.shape[0]
    e_pair = route[:, 0:2].astype(jnp.int32).reshape(-1)
    onehot = (e_pair[:, None] == jnp.arange(n_exp, dtype=jnp.int32)[None, :]).astype(jnp.int32)
    csum = jnp.cumsum(onehot, axis=0)
    rank = jnp.sum((csum - onehot) * onehot, axis=1)
    counts = csum[-1]
    padded = ((counts + tm - 1) // tm) * tm
    ends = jnp.cumsum(padded)
    slot = jnp.sum(onehot * (ends - padded)[None, :], axis=1) + rank
    p_slots = 2 * n + n_exp * tm
    n_tiles = p_slots // tm
    n_used = ends[-1] // tm
    tile = jnp.arange(n_tiles, dtype=jnp.int32)
    te = jnp.sum(((tile * tm)[:, None] >= ends[None, :]).astype(jnp.int32), axis=1)
    te = jnp.where(tile < n_used, te, te[n_used - 1])
    return slot.reshape(n, 2), p_slots, te, n_used.reshape(1)


def _gather_rows(table, idx):
    p = idx.shape[0]
    m = table.shape[1]
    mesh = plsc.VectorSubcoreMesh(core_axis_name="c", subcore_axis_name="s")

    @functools.partial(pl.kernel, out_type=jax.ShapeDtypeStruct((p, m), table.dtype), mesh=mesh)
    def gather(table_hbm, idx_hbm, out_hbm):
        def body(idx_vmem, out_vmem):
            pltpu.sync_copy(table_hbm.at[idx_vmem.at[0]], out_vmem)

        pltpu.emit_pipeline(
            body, grid=(p // SC_WINDOW,),
            in_specs=[pl.BlockSpec((1, SC_WINDOW), lambda i: (0, i))],
            out_specs=[pl.BlockSpec((SC_WINDOW, m), lambda i: (i, 0))],
            core_axis_name=("c", "s"), dimension_semantics=(pltpu.PARALLEL,),
        )(idx_hbm, out_hbm)

    return gather(table, idx.reshape(1, p))


def _scatter_rows(rows, idx_a, idx_b, n_out):
    q, m = rows.shape
    mesh = plsc.VectorSubcoreMesh(core_axis_name="c", subcore_axis_name="s")

    @functools.partial(pl.kernel, out_type=jax.ShapeDtypeStruct((n_out, m), rows.dtype), mesh=mesh)
    def scatter(rows_hbm, ia_hbm, ib_hbm, out_hbm):
        def body(rows_vmem, ia_vmem, ib_vmem):
            pltpu.sync_copy(rows_vmem, out_hbm.at[ia_vmem.at[0]])
            pltpu.sync_copy(rows_vmem, out_hbm.at[ib_vmem.at[0]])

        pltpu.emit_pipeline(
            body, grid=(q // SC_WINDOW,),
            in_specs=[pl.BlockSpec((SC_WINDOW, m), lambda i: (i, 0)),
                      pl.BlockSpec((1, SC_WINDOW), lambda i: (0, i)),
                      pl.BlockSpec((1, SC_WINDOW), lambda i: (0, i))],
            out_specs=[],
            core_axis_name=("c", "s"), dimension_semantics=(pltpu.PARALLEL,),
        )(rows_hbm, ia_hbm, ib_hbm)

    return scatter(rows, idx_a.reshape(1, q), idx_b.reshape(1, q))


def _scatter_pieces(pieces, slot, n_slots):
    npc, r, m = pieces.shape
    offs = jnp.arange(npc, dtype=jnp.int32)[:, None] * n_slots
    idx_a = (slot[None, :, 0] + offs).reshape(-1)
    idx_b = (slot[None, :, 1] + offs).reshape(-1)
    return _scatter_rows(pieces.reshape(npc * r, m), idx_a, idx_b, npc * n_slots).reshape(npc, n_slots, m)


def _gather_pieces(pieces, idx):
    npc, v, m = pieces.shape
    offs = (jnp.arange(npc, dtype=jnp.int32) * v).reshape((npc,) + (1,) * idx.ndim)
    flat = (idx[None] + offs).reshape(-1)
    return _gather_rows(pieces.reshape(npc * v, m), flat).reshape((npc,) + idx.shape + (m,))


def _moe_ffn_kernel(te_ref, nu_ref, x_ref, w1_ref, w3_ref, w2_ref, y_ref):
    @pl.when(pl.program_id(0) < nu_ref[0])
    def _():
        x = _load_unpacked(x_ref).astype(BF16)
        hid = _silu(_dot(x, w1_ref[...])) * _dot(x, w3_ref[...])
        _store_pieces(y_ref, _pack_halves(_dot(hid.astype(BF16), w2_ref[...])))


def _moe_ffn(xs, te, n_used, w1, w3, w2, j):
    npc, p, m = xs.shape
    d = 2 * npc * m
    dff = w1.shape[3]
    tm = MOE_ROWS

    def row(i, te_ref, nu_ref):
        return (0, jnp.minimum(i, nu_ref[0] - 1), 0)

    def expert(i, te_ref, nu_ref):
        return (j, te_ref[i], 0, 0)

    grid_spec = pltpu.PrefetchScalarGridSpec(
        num_scalar_prefetch=2, grid=(p // tm,),
        in_specs=[pl.BlockSpec((npc, tm, m), row),
                  _resident((None, None, d, dff), expert),
                  _resident((None, None, d, dff), expert),
                  _resident((None, None, dff, d), expert)],
        out_specs=pl.BlockSpec((npc, tm, m), row))
    return pl.pallas_call(
        _moe_ffn_kernel, grid_spec=grid_spec,
        out_shape=jax.ShapeDtypeStruct((npc, p, m), jnp.int32),
        compiler_params=_cparams("arbitrary"), name="moe_ffn",
    )(te, n_used, xs, w1, w3, w2)


def _moe_combine_kernel(ya_ref, yb_ref, route_ref, h_ref, gt_ref, o_ref):
    r = route_ref[...]
    y = r[:, 2:3] * _load_unpacked(ya_ref) + r[:, 3:4] * _load_unpacked(yb_ref)
    o_ref[...] = h_ref[...] + gt_ref[...] * y


def _moe_combine(yg, route, h, mods, n_batch, tiles_per_batch, ctx_tiles):
    d = h.shape[1]
    npc, _, r, m = yg.shape
    tm = TOKEN_TILE
    per = tiles_per_batch - ctx_tiles
    h_idx = lambda i: ((i // per) * tiles_per_batch + ctx_tiles + i % per, 0)
    if ctx_tiles == 0:
        mod_idx = _mod_row_spec(5, tiles_per_batch, n_batch)
    else:
        mod_idx = lambda i: (i // per, 5, 0, 0)
    return pl.pallas_call(
        _moe_combine_kernel, grid=(r // tm,),
        in_specs=[pl.BlockSpec((npc, None, tm, m), lambda i: (0, 0, i, 0)),
                  pl.BlockSpec((npc, None, tm, m), lambda i: (0, 1, i, 0)),
                  pl.BlockSpec((tm, LANES), lambda i: (i, 0)),
                  pl.BlockSpec((tm, d), h_idx),
                  pl.BlockSpec((None, None, 1, d), mod_idx)],
        out_specs=pl.BlockSpec((tm, d), lambda i: (i, 0)),
        out_shape=jax.ShapeDtypeStruct((r, d), F32),
        compiler_params=_cparams("parallel"), name="moe_combine",
    )(yg, yg, route, h, mods)


def _moe_residual(u_pieces, route, w1, w3, w2, j, h, mods, n_batch, t, n_ctx, latent_only):
    if latent_only:
        npc, _, m = u_pieces.shape
        u_pieces = u_pieces.reshape(npc, n_batch, t, m)[:, :, n_ctx:].reshape(npc, -1, m)
        route = route.reshape(n_batch, t, LANES)[:, n_ctx:].reshape(-1, LANES)
    slot, n_slots, te, n_used = _route_tables(route, MOE_ROWS, w1.shape[1])
    xs = _scatter_pieces(u_pieces, slot, n_slots)
    ys = _moe_ffn(xs, te, n_used, w1, w3, w2, j)
    yg = _gather_pieces(ys, slot.T)
    ctx_tiles = n_ctx // TOKEN_TILE if latent_only else 0
    return _moe_combine(yg, route, h, mods, n_batch, t // TOKEN_TILE, ctx_tiles)


def kernel(x, c, ctx, c_ctx, w_mod, b_mod, norm_mix_g, norm_ffn_g, ev_w_in, ev_w_out, hg_lb_logits, hg_onorm_g, rg_conv_w, rg_conv_b, rg_wa, rg_ba, rg_wx, rg_bx, rg_lambda, na_w_qkv, na_w_o, na_q_g, na_k_g, na_rpb, ffn_w1, ffn_w3, ffn_w2, moe_router, moe_w1, moe_w3, moe_w2):
    bsz, seq, d = x.shape
    n_ctx = ctx.shape[1]
    t = n_ctx + seq
    depth = w_mod.shape[0]
    tpb = t // TOKEN_TILE
    assert n_ctx % TOKEN_TILE == 0 and seq % TOKEN_TILE == 0 and t % ROW_TILE == 0 and n_ctx <= ROW_TILE

    rows = -(-(bsz + 1) // 8) * 8
    cond = jnp.zeros((rows, d), F32).at[:bsz].set(c).at[bsz].set(c_ctx)
    mods_all = _adaln(cond, w_mod, b_mod).reshape(depth, rows, N_MOD, 1, d)

    lb_p = jax.nn.softmax(hg_lb_logits.astype(F32), axis=0)
    lb_all = jnp.cumsum(lb_p, axis=0) - lb_p[0]

    bf = lambda w: w.astype(BF16)
    w_in, w_out, w_qkv, w_o = bf(ev_w_in), bf(ev_w_out), bf(na_w_qkv), bf(na_w_o)
    f_w1, f_w3, f_w2 = bf(ffn_w1), bf(ffn_w3), bf(ffn_w2)
    m_w1, m_w3, m_w2 = bf(moe_w1), bf(moe_w3), bf(moe_w2)

    h = jnp.concatenate([ctx, x], axis=1).reshape(bsz * t, d)
    for layer in range(depth):
        j = layer // 2
        mods = mods_all[layer]
        g_mix = norm_mix_g[layer].reshape(1, d)
        if layer % 2 == 0:
            p = _modulated_matmul(h, g_mix, mods, w_in, j, F32, ROW_TILE // 2, bsz, t, n_ctx)
            p3 = p.reshape(bsz, t, -1)
            hg = _hgrn2(p3, lb_all[j], n_ctx)
            half = hg.shape[-1]
            rg = _rglru(p3, 5 * half // (d - half), rg_conv_w[j], rg_conv_b[j], rg_wa[j], rg_ba[j],
                        rg_wx[j], rg_bx[j], rg_lambda[j], n_ctx)
            h = _merge_out(hg.reshape(bsz * t, half), rg.reshape(bsz * t, d - half), p, hg_onorm_g[j],
                           w_out, j, h, mods, bsz, t, n_ctx)
        else:
            qkv = _modulated_matmul(h, g_mix, mods, w_qkv, j, BF16, ROW_TILE, bsz, t, n_ctx)
            o = _na_attention(qkv.reshape(bsz, t, 3 * d), na_q_g[j], na_k_g[j], na_rpb[j], n_ctx)
            h = _matmul_residual(o.reshape(bsz * t, d), w_o, j, h, mods, 2, bsz, t, n_ctx)
        if layer % 2 == 0:
            h = _swiglu_residual(h, norm_ffn_g[layer].reshape(1, d), f_w1, f_w3, f_w2, j, mods, bsz, t, n_ctx)
        else:
            u, route = _modulate_router(h, norm_ffn_g[layer].reshape(1, d), mods, 3, bsz, tpb, moe_router[j])
            h = _moe_residual(u, route, m_w1, m_w3, m_w2, j, h, mods, bsz, t, n_ctx, layer == depth - 1)
    if depth % 2 == 0:
        return h.reshape(bsz, seq, d)
    return h.reshape(bsz, t, d)[:, n_ctx:]
```

```python
import functools

import numpy as np
import jax
import jax.numpy as jnp
from jax import lax
from jax.experimental import pallas as pl
from jax.experimental.pallas import tpu as pltpu
from jax.experimental.pallas import tpu_sc as plsc

F32 = jnp.float32
BF16 = jnp.bfloat16

EPS = 1e-6
NEG_BIG = -1e9
LB_MIN = 1e-30
LOG2_E = 1.4426950408889634
N_MOD = 6
GRID_W = 64
HG_HEAD_DIM = 128
RG_BLOCKS = 8
RG_CONV = 4
RG_C = 8.0
NA_HEADS = 16
NA_KR = 8
NA_KC = 16
N_EXPERTS = 8

LANES = 128
SC_WINDOW = 128
SC_PIECES = 2
TOKEN_TILE = 256
HG_CHUNK = 128
HG_LEVELS = 7
HG_SLOTS = 2
NA_SLOTS = 4
RG_SCAN_BLOCK = 256
RG_GROUP = 8
VMEM_LIMIT = 56 * 1024 * 1024


def _cparams(*sem):
    return pltpu.CompilerParams(dimension_semantics=sem, vmem_limit_bytes=VMEM_LIMIT)


def _dot(a, b):
    return jnp.dot(a, b, preferred_element_type=F32)


def _dot_nt(a, b):
    return lax.dot_general(a, b, (((1,), (1,)), ((), ())), preferred_element_type=F32)


def _dot_tn(a, b):
    return lax.dot_general(a, b, (((0,), (0,)), ((), ())), preferred_element_type=F32)


def _split2(x):
    hi = x.astype(BF16)
    lo = (x - hi.astype(F32)).astype(BF16)
    return hi, lo


def _sigmoid(x):
    return 0.5 * jnp.tanh(0.5 * x) + 0.5


def _silu(x):
    return x * _sigmoid(x)


def _softplus_neg_abs(x):
    return jnp.log(1.0 + jnp.exp(-jnp.abs(x)))


def _log_sigmoid(x):
    return jnp.minimum(x, 0.0) - _softplus_neg_abs(x)


def _gelu_tanh(x):
    return 0.5 * x * (1.0 + jnp.tanh(np.sqrt(2.0 / np.pi) * (x + 0.044715 * (x * x * x))))


def _adaln_kernel(c_ref, w_ref, b_ref, o_ref):
    a_hi, a_lo = _split2(_silu(c_ref[...]))
    w_hi, w_lo = _split2(w_ref[...])
    o_ref[...] = _dot(a_hi, w_hi) + _dot(a_hi, w_lo) + _dot(a_lo, w_hi) + b_ref[...]


def _adaln(cond, w_mod, b_mod):
    depth, d, n = w_mod.shape
    r = cond.shape[0]
    tn = 1536
    return pl.pallas_call(
        _adaln_kernel,
        grid=(depth, n // tn),
        in_specs=[pl.BlockSpec((r, d), lambda l, j: (0, 0)),
                  pl.BlockSpec((None, d, tn), lambda l, j: (l, 0, j)),
                  pl.BlockSpec((None, 1, tn), lambda l, j: (l, 0, j))],
        out_specs=pl.BlockSpec((None, r, tn), lambda l, j: (l, 0, j)),
        out_shape=jax.ShapeDtypeStruct((depth, r, n), F32),
        compiler_params=_cparams("parallel", "parallel"),
        name="adaln",
    )(cond, w_mod, b_mod.reshape(depth, 1, n))


def _mod_row_spec(k, tiles_per_batch, n_batch):
    def idx(i):
        row = jnp.where(i % tiles_per_batch == 0, n_batch, i // tiles_per_batch)
        return (row, k, 0, 0)
    return idx


def _modulated(h_ref, g_ref, sh_ref, sc_ref):
    x = h_ref[...]
    y = x * lax.rsqrt(jnp.mean(x * x, axis=-1, keepdims=True) + EPS)
    return (y * g_ref[...]) * (1.0 + sc_ref[...]) + sh_ref[...]


def _pack_halves(x):
    m = x.shape[1] // 2
    lo = lax.bitcast_convert_type(x[:, :m].astype(BF16).astype(F32), jnp.int32)
    hi = lax.bitcast_convert_type(x[:, m:].astype(BF16).astype(F32), jnp.int32)
    return lax.shift_right_logical(lo, 16) | (hi & jnp.int32(-65536))


def _unpack_halves(w):
    lo = lax.bitcast_convert_type(lax.shift_left(w, 16), F32)
    hi = lax.bitcast_convert_type(w & jnp.int32(-65536), F32)
    return lo, hi


def _store_pieces(ref, w):
    m = w.shape[1] // SC_PIECES
    for j in range(SC_PIECES):
        ref[j] = w[:, j * m:(j + 1) * m]


def _load_unpacked(ref):
    parts = [_unpack_halves(ref[j]) for j in range(SC_PIECES)]
    return jnp.concatenate([lo for lo, _ in parts] + [hi for _, hi in parts], axis=1)


def _modulate_router_kernel(h_ref, g_ref, sh_ref, sc_ref, wr_ref, u_ref, route_ref):
    u = _modulated(h_ref, g_ref, sh_ref, sc_ref)
    _store_pieces(u_ref, _pack_halves(u))
    u0, u1 = _split2(u)
    w0, w1 = _split2(wr_ref[...])
    logits = _dot(u0, w0) + _dot(u0, w1) + _dot(u1, w0)
    lane = lax.broadcasted_iota(jnp.int32, logits.shape, 1).astype(F32)
    lg = jnp.where(lane < N_EXPERTS, logits, -jnp.inf)
    m1 = jnp.max(lg, axis=-1, keepdims=True)
    i1 = jnp.min(jnp.where(lg == m1, lane, float(LANES)), axis=-1, keepdims=True)
    lg2 = jnp.where(lane == i1, -jnp.inf, lg)
    m2 = jnp.max(lg2, axis=-1, keepdims=True)
    i2 = jnp.min(jnp.where(lg2 == m2, lane, float(LANES)), axis=-1, keepdims=True)
    e2 = jnp.exp(m2 - m1)
    den = 1.0 + e2
    route_ref[...] = jnp.where(lane == 0, i1, jnp.where(lane == 1, i2, jnp.where(lane == 2, 1.0 / den, e2 / den)))


def _modulate_router(h, g, mods, k_shift, n_batch, tiles_per_batch, w_router):
    n, d = h.shape
    tm = TOKEN_TILE
    row = pl.BlockSpec((tm, d), lambda i: (i, 0))
    vec = pl.BlockSpec((1, d), lambda i: (0, 0))

    def mod_spec(k):
        return pl.BlockSpec((None, None, 1, d), _mod_row_spec(k, tiles_per_batch, n_batch))

    in_specs = [row, vec, mod_spec(k_shift), mod_spec(k_shift + 1)]
    wr = jnp.zeros((d, LANES), F32).at[:, :N_EXPERTS].set(w_router)
    return pl.pallas_call(
        _modulate_router_kernel, grid=(n // tm,),
        in_specs=in_specs + [pl.BlockSpec((d, LANES), lambda i: (0, 0))],
        out_specs=[pl.BlockSpec((SC_PIECES, tm, d // 2 // SC_PIECES), lambda i: (0, i, 0)),
                   pl.BlockSpec((tm, LANES), lambda i: (i, 0))],
        out_shape=[jax.ShapeDtypeStruct((SC_PIECES, n, d // 2 // SC_PIECES), jnp.int32),
                   jax.ShapeDtypeStruct((n, LANES), F32)],
        compiler_params=_cparams("parallel"), name="modulate_router",
    )(h, g, mods, mods, wr)


ROW_TILE = 768


def _row_gate(ctx_ref, own_ref, shape, tile_in_batch, n_ctx):
    rows = lax.broadcasted_iota(jnp.int32, shape, 0)
    is_ctx = (rows < n_ctx) & (tile_in_batch == 0)
    return jnp.where(is_ctx, ctx_ref[...], own_ref[...])


def _modulated_rows(h_ref, g_ref, shc_ref, sh_ref, scc_ref, sc_ref, tile_in_batch, n_ctx):
    x = h_ref[...]
    y = x * lax.rsqrt(jnp.mean(x * x, axis=-1, keepdims=True) + EPS)
    shift = _row_gate(shc_ref, sh_ref, x.shape, tile_in_batch, n_ctx)
    scale = _row_gate(scc_ref, sc_ref, x.shape, tile_in_batch, n_ctx)
    return (y * g_ref[...]) * (1.0 + scale) + shift


def _stream_specs(d, n_batch, tiles_per_batch, k, nargs):
    if nargs == 1:
        ctx = lambda i: (n_batch, k, 0, 0)
        own = lambda i: (i // tiles_per_batch, k, 0, 0)
    else:
        ctx = lambda i, f: (n_batch, k, 0, 0)
        own = lambda i, f: (i // tiles_per_batch, k, 0, 0)
    return [pl.BlockSpec((None, None, 1, d), ctx), pl.BlockSpec((None, None, 1, d), own)]


def _mod_mm_kernel(h_ref, g_ref, shc_ref, sh_ref, scc_ref, sc_ref, w_ref, o_ref, *, n_ctx, tiles_per_batch):
    x = _modulated_rows(h_ref, g_ref, shc_ref, sh_ref, scc_ref, sc_ref,
                        pl.program_id(0) % tiles_per_batch, n_ctx).astype(BF16)
    o_ref[...] = _dot(x, w_ref[...]).astype(o_ref.dtype)


def _modulated_matmul(h, g, mods, w, j, out_dtype, tm, n_batch, t, n_ctx):
    n, d = h.shape
    nout = w.shape[2]
    tpb = t // tm
    assert t % tm == 0 and n_ctx <= tm
    kern = functools.partial(_mod_mm_kernel, n_ctx=n_ctx, tiles_per_batch=tpb)
    return pl.pallas_call(
        kern, grid=(n // tm,),
        in_specs=[pl.BlockSpec((tm, d), lambda i: (i, 0)),
                  pl.BlockSpec((1, d), lambda i: (0, 0))]
        + _stream_specs(d, n_batch, tpb, 0, 1) + _stream_specs(d, n_batch, tpb, 1, 1)
        + [pl.BlockSpec((None, d, nout), lambda i: (j, 0, 0))],
        out_specs=pl.BlockSpec((tm, nout), lambda i: (i, 0)),
        out_shape=jax.ShapeDtypeStruct((n, nout), out_dtype),
        compiler_params=_cparams("parallel"), name="modulated_matmul",
    )(h, g, mods, mods, mods, mods, w)


def _mm_res_kernel(x_ref, w_ref, h_ref, gtc_ref, gt_ref, o_ref, *, n_ctx, tiles_per_batch):
    gate = _row_gate(gtc_ref, gt_ref, h_ref.shape, pl.program_id(0) % tiles_per_batch, n_ctx)
    o_ref[...] = h_ref[...] + gate * _dot(x_ref[...], w_ref[...])


def _matmul_residual(x, w, j, h, mods, k_gate, n_batch, t, n_ctx):
    n, d = h.shape
    kdim = x.shape[1]
    tm = ROW_TILE
    tpb = t // tm
    kern = functools.partial(_mm_res_kernel, n_ctx=n_ctx, tiles_per_batch=tpb)
    return pl.pallas_call(
        kern, grid=(n // tm,),
        in_specs=[pl.BlockSpec((tm, kdim), lambda i: (i, 0)),
                  pl.BlockSpec((None, kdim, d), lambda i: (j, 0, 0)),
                  pl.BlockSpec((tm, d), lambda i: (i, 0))] + _stream_specs(d, n_batch, tpb, k_gate, 1),
        out_specs=pl.BlockSpec((tm, d), lambda i: (i, 0)),
        out_shape=jax.ShapeDtypeStruct((n, d), F32),
        compiler_params=_cparams("parallel"), name="matmul_residual",
    )(x, w, h, mods, mods)


def _hg_tables():
    c = HG_CHUNK
    t = np.arange(c)[:, None]
    u = np.arange(c)[None, :]
    mats = [(u <= t), (u > t)]
    for k in range(1, HG_LEVELS + 1):
        m = ((t >> k) << k) + (1 << (k - 1)) - 1
        upper = t > m
        mats.append(np.where(upper, (u > m) & (u <= t), (u > t) & (u <= m)))
    fwd = np.concatenate(mats, axis=0).astype(np.float32)
    bwd = fwd.reshape(-1, c, c)[:, ::-1, ::-1].reshape(-1, c)
    x = t ^ u
    lvl = np.where(t > u, np.floor(np.log2(np.maximum(x, 1))).astype(np.int32) + 1, -1)
    lvl = np.where(t == u, 0, lvl).astype(np.int32)
    return np.stack([fwd, bwd]), np.stack([lvl, lvl.T])


def _hg_kernel(q_ref, ff_ref, fb_ref, v_ref, lb_ref, m_ref, lvl_ref, o_ref, of_ref, ob_ref,
               qk_ref, el_ref, io_ref, tot_ref, att_ref, *, n_ctx_chunks, n_chunks):
    c = HG_CHUNK
    gap = HG_SLOTS // 2
    t = o_ref.shape[0]
    z_refs = (ff_ref, fb_ref)
    out_refs = (of_ref, ob_ref)
    for ref in (qk_ref, el_ref, io_ref, tot_ref, att_ref):
        ref[...] = jnp.zeros(ref.shape, ref.dtype)

    def start_row(d, j):
        j = jnp.clip(j, 0, n_chunks - 1)
        if d == 1:
            j = jnp.where(j < n_ctx_chunks, n_ctx_chunks - 1 - j, n_chunks - 1 - (j - n_ctx_chunks))
        return pl.multiple_of(j * c, c)

    def stage1(d, j, slot):
        t0 = start_row(d, j)
        lb = lb_ref[d:d + 1, :]
        log_lb = jnp.log(jnp.maximum(lb, LB_MIN))
        q = _silu(q_ref[pl.ds(t0, c), :])
        z = z_refs[d][pl.ds(t0, c), :]
        k = (1.0 - lb) * _sigmoid(-z)
        b = jnp.log1p(-lb) + _log_sigmoid(z)
        logf = jnp.maximum(log_lb, b) + _softplus_neg_abs(log_lb - b)
        g_hi, g_lo = _split2(logf * LOG2_E)
        m_ends = m_ref[d, 0:2 * c, :]
        e_ends = jnp.exp2(_dot(m_ends, g_hi) + _dot(m_ends, g_lo))
        el_ref[d, slot] = jnp.exp2(_dot(m_ref[d, 2 * c:, :], g_hi).astype(BF16))
        qk_ref[d, slot, 0] = q.astype(BF16)
        qk_ref[d, slot, 1] = k.astype(BF16)
        io_ref[d, slot, 0] = (q * e_ends[0:c]).astype(BF16)
        io_ref[d, slot, 1] = (k * e_ends[c:]).astype(BF16)
        tot_row = c - 1 if d == 0 else 0
        tot_ref[d, slot, 0:1, :] = e_ends[tot_row:tot_row + 1, :]

    def stage2(d, slot):
        qb = qk_ref[d, slot, 0]
        kb = qk_ref[d, slot, 1]
        lvl = lvl_ref[d]
        att = jnp.where(lvl == 0, _dot_nt(qb, kb), 0.0)
        for lev in range(1, HG_LEVELS + 1):
            el = el_ref[d, slot, (lev - 1) * c:lev * c, :]
            att = jnp.where(lvl == lev, _dot_nt(qb * el, kb * el), att)
        att_ref[d, slot] = att.astype(BF16)

    def stage3(d, j, slot, st):
        t0 = start_row(d, j)
        v = v_ref[pl.ds(t0, c), :].astype(BF16)
        row = jnp.where(j < n_chunks, t0, t)
        out_refs[d][pl.ds(pl.multiple_of(row, c), c), :] = (
            _dot(att_ref[d, slot], v) + _dot_nt(io_ref[d, slot, 0], st.astype(BF16)))
        return st * tot_ref[d, slot, 0:1, :] + _dot_tn(v, io_ref[d, slot, 1])

    def steps(jj, carry):
        st = list(carry)
        for slot in range(HG_SLOTS):
            i = HG_SLOTS * jj + slot
            for d in range(2):
                st[d] = stage3(d, i - 2 * gap, slot, st[d])
            for d in range(2):
                stage2(d, (slot + gap) % HG_SLOTS)
            for d in range(2):
                stage1(d, i, slot)
        return tuple(st)

    zero = jnp.zeros((c, c), F32)
    n_steps = -(-(n_chunks + 2 * gap) // HG_SLOTS)
    lax.fori_loop(0, n_steps, steps, (zero, zero))
    o_ref[...] = of_ref[0:t, :] + ob_ref[0:t, :]


def _hgrn2(p3, lb, n_ctx):
    bsz, t, _ = p3.shape
    width = lb.shape[1]
    heads = width // HG_HEAD_DIM
    mats, lvl = _hg_tables()
    c = HG_CHUNK

    def col(group):
        return pl.BlockSpec((None, t, c), lambda b, h: (b, 0, group * heads + h))

    kern = functools.partial(_hg_kernel, n_ctx_chunks=n_ctx // c, n_chunks=t // c)
    return pl.pallas_call(
        kern, grid=(bsz, heads),
        in_specs=[col(0), col(1), col(2), col(3),
                  pl.BlockSpec((2, c), lambda b, h: (0, h)),
                  pl.BlockSpec(mats.shape, lambda b, h: (0, 0, 0)),
                  pl.BlockSpec(lvl.shape, lambda b, h: (0, 0, 0))],
        out_specs=pl.BlockSpec((None, t, c), lambda b, h: (b, 0, h)),
        out_shape=jax.ShapeDtypeStruct((bsz, t, width), F32),
        scratch_shapes=[pltpu.VMEM((t + c, c), F32), pltpu.VMEM((t + c, c), F32),
                        pltpu.VMEM((2, HG_SLOTS, 2, c, c), BF16), pltpu.VMEM((2, HG_SLOTS, HG_LEVELS * c, c), BF16),
                        pltpu.VMEM((2, HG_SLOTS, 2, c, c), BF16), pltpu.VMEM((2, HG_SLOTS, 8, c), F32),
                        pltpu.VMEM((2, HG_SLOTS, c, c), BF16)],
        compiler_params=_cparams("parallel", "parallel"), name="hgrn2",
    )(p3, p3, p3, p3, lb, jnp.asarray(mats, BF16), jnp.asarray(lvl))


PAD = 8


def _rg_kernel(x_ref, cw_ref, cb_ref, wa_ref, ba_ref, wx_ref, bx_ref, lam_ref, o_ref,
               xpad_ref, xc_ref, ob_ref, *, n_ctx):
    t, w = x_ref.shape
    n_lat = t - n_ctx
    lat0 = n_ctx + 2 * PAD
    xpad_ref[...] = jnp.zeros(xpad_ref.shape, F32)
    xpad_ref[PAD:PAD + n_ctx, :] = x_ref[0:n_ctx, :]
    xpad_ref[lat0:lat0 + n_lat, :] = x_ref[n_ctx:t, :]
    left = RG_CONV // 2
    for base, dst, n in ((PAD, 0, n_ctx), (lat0, n_ctx, n_lat)):
        acc = cb_ref[...] + cw_ref[0:1, :] * xpad_ref[base - left:base - left + n, :]
        for j in range(1, RG_CONV):
            acc = acc + cw_ref[j:j + 1, :] * xpad_ref[base - left + j:base - left + j + n, :]
        xc_ref[dst:dst + n, :] = acc

    blk = RG_SCAN_BLOCK
    n_blocks = t // blk
    n_ctx_blocks = n_ctx // blk
    sub = lax.broadcasted_iota(jnp.int32, (blk // RG_GROUP, RG_GROUP, w), 1)
    log_sig_lam = _log_sigmoid(lam_ref[...])

    def scan_block(d, t0, hin, out_ref):
        xc = xc_ref[pl.ds(t0, blk), :]
        xb = xc.astype(BF16)
        r = _sigmoid(_dot(xb, wa_ref[d]) + ba_ref[d:d + 1, :])
        ig = _sigmoid(_dot(xb, wx_ref[d]) + bx_ref[d:d + 1, :])
        log_a = RG_C * r * log_sig_lam[d:d + 1, :]
        a = jnp.exp(log_a)
        b = jnp.sqrt(jnp.maximum(-jnp.tanh(log_a) * (a * a + 1.0), 0.0)) * (ig * xc)
        n_groups = blk // RG_GROUP
        a = a.reshape(n_groups, RG_GROUP, w)
        b = b.reshape(n_groups, RG_GROUP, w)
        s = 1
        while s < RG_GROUP:
            shift = s if d == 0 else RG_GROUP - s
            keep = (sub >= s) if d == 0 else (sub < RG_GROUP - s)
            b = jnp.where(keep, a * pltpu.roll(b, shift, 1), 0.0) + b
            a = jnp.where(keep, a * pltpu.roll(a, shift, 1), a)
            s *= 2
        for gi in range(n_groups):
            gj = gi if d == 0 else n_groups - 1 - gi
            hg = a[gj] * hin + b[gj]
            out_ref[pl.ds(t0 + gj * RG_GROUP, RG_GROUP), :] = hg
            hin = hg[RG_GROUP - 1:RG_GROUP, :] if d == 0 else hg[0:1, :]
        return hin

    def step(i, carry):
        h_f, h_b = carry
        bb = jnp.where(i < n_ctx_blocks, n_ctx_blocks - 1 - i, n_blocks - 1 - (i - n_ctx_blocks))
        h_f = scan_block(0, pl.multiple_of(i * blk, blk), h_f, o_ref)
        h_b = scan_block(1, pl.multiple_of(bb * blk, blk), h_b, ob_ref)
        return h_f, h_b

    zero = jnp.zeros((1, w), F32)
    lax.fori_loop(0, n_blocks, step, (zero, zero))
    o_ref[...] += ob_ref[...]


def _blockdiag_dense(w):
    nd, nb, k, _ = w.shape
    eye = jnp.eye(nb, dtype=w.dtype)
    return jnp.einsum('dnij,nm->dnimj', w, eye).reshape(nd, nb * k, nb * k)


def _rglru(p3, col_block, conv_w, conv_b, wa, ba, wx, bx, lam, n_ctx):
    bsz, t, _ = p3.shape
    w = conv_w.shape[1]
    full = lambda shape: pl.BlockSpec(shape, lambda b: (0,) * len(shape))
    kern = functools.partial(_rg_kernel, n_ctx=n_ctx)
    return pl.pallas_call(
        kern, grid=(bsz,),
        in_specs=[pl.BlockSpec((None, t, w), lambda b: (b, 0, col_block)),
                  full((RG_CONV, w)), full((1, w)),
                  full((2, w, w)), full((2, w)), full((2, w, w)), full((2, w)), full((2, w))],
        out_specs=pl.BlockSpec((None, t, w), lambda b: (b, 0, 0)),
        out_shape=jax.ShapeDtypeStruct((bsz, t, w), F32),
        scratch_shapes=[pltpu.VMEM((t + 3 * PAD, w), F32), pltpu.VMEM((t, w), F32), pltpu.VMEM((t, w), F32)],
        compiler_params=_cparams("parallel"), name="rglru",
    )(p3, conv_w, conv_b.reshape(1, w), _blockdiag_dense(wa).astype(BF16), ba,
      _blockdiag_dense(wx).astype(BF16), bx, lam)


def _merge_kernel(hg_ref, rg_ref, gate_ref, y_ref, og_ref, w_ref, h_ref, gtc_ref, gt_ref, o_ref,
                  *, n_ctx, tiles_per_batch):
    hg = hg_ref[...]
    half = hg.shape[1]
    hn = hg * lax.rsqrt(jnp.mean(hg * hg, axis=-1, keepdims=True) + EPS) * og_ref[...]
    a = (hn * _silu(gate_ref[...])).astype(BF16)
    b = (rg_ref[...] * _gelu_tanh(y_ref[...])).astype(BF16)
    o = _dot(a, w_ref[0:half, :]) + _dot(b, w_ref[half:, :])
    gate = _row_gate(gtc_ref, gt_ref, h_ref.shape, pl.program_id(0) % tiles_per_batch, n_ctx)
    o_ref[...] = h_ref[...] + gate * o


def _merge_out(hg, rg, p, onorm_g, w_out, j, h, mods, n_batch, t, n_ctx):
    n, d = h.shape
    half = hg.shape[1]
    tm = ROW_TILE
    tpb = t // tm
    blk = lambda c: pl.BlockSpec((tm, half), lambda i: (i, c))
    kern = functools.partial(_merge_kernel, n_ctx=n_ctx, tiles_per_batch=tpb)
    return pl.pallas_call(
        kern, grid=(n // tm,),
        in_specs=[blk(0), blk(0), blk(4), blk(6),
                  pl.BlockSpec((1, half), lambda i: (0, 0)),
                  pl.BlockSpec((None, d, d), lambda i: (j, 0, 0)),
                  pl.BlockSpec((tm, d), lambda i: (i, 0))] + _stream_specs(d, n_batch, tpb, 2, 1),
        out_specs=pl.BlockSpec((tm, d), lambda i: (i, 0)),
        out_shape=jax.ShapeDtypeStruct((n, d), F32),
        compiler_params=_cparams("parallel"), name="merge_out",
    )(hg, rg, p, p, onorm_g.reshape(1, half), w_out, h, mods, mods)


def _na_bias_table(rpb):
    q = np.arange(GRID_W)[:, None]
    kc = np.arange(GRID_W)[None, :]
    wstart = np.clip(q - NA_KC // 2, 0, GRID_W - NA_KC)
    mask = (kc >= wstart) & (kc < wstart + NA_KC)
    coff = np.clip(kc - q + NA_KC - 1, 0, 2 * NA_KC - 2)
    sel = (coff[None] == np.arange(2 * NA_KC - 1)[:, None, None]).astype(np.float32)
    toe = jnp.einsum('hdj,jqk->hqdk', rpb.astype(F32), sel, precision=lax.Precision.HIGHEST)
    toe = jnp.where(mask[None, :, None, :], toe, NEG_BIG)
    h = toe.shape[0]
    flat = toe.reshape(h // 2, 2 * GRID_W, -1)
    width = (2 * NA_KR) * GRID_W
    even = jnp.pad(flat, ((0, 0), (0, 0), (0, width - flat.shape[2])))
    odd = jnp.pad(flat[:, :, GRID_W:], ((0, 0), (0, 0), (0, width - flat.shape[2] + GRID_W)))
    return jnp.stack([even, odd], axis=1)


def _stack_heads(x, hd):
    lane = lax.broadcasted_iota(jnp.int32, x.shape, 1)
    zero = jnp.zeros_like(x)
    return jnp.concatenate([jnp.where(lane < hd, x, zero), jnp.where(lane >= hd, x, zero)], axis=0)


def _unstack_heads(o2, hd):
    n = o2.shape[0] // 2
    lane = lax.broadcasted_iota(jnp.int32, (n, o2.shape[1]), 1)
    return jnp.where(lane < hd, o2[:n], o2[n:])


def _na_kernel(q_ref, k_ref, v_ref, qg_ref, kg_ref, bias_ref, o_ref, qn_ref, kn_ref,
               s_ref, p_ref, den_ref, *, n_ctx, scale):
    t, w = q_ref.shape
    hd = w // 2
    rows_grid = (t - n_ctx) // GRID_W
    r_i = lax.broadcasted_iota(jnp.int32, (w, w), 0)
    c_i = lax.broadcasted_iota(jnp.int32, (w, w), 1)
    seg = jnp.where((r_i // hd) == (c_i // hd), 1.0 / hd, 0.0).astype(BF16)

    def normed(x, g):
        ms = _dot((x * x).astype(BF16), seg)
        return (x * lax.rsqrt(ms + EPS) * g).astype(BF16)

    qn_ref[...] = normed(q_ref[...].astype(F32), qg_ref[...] * scale)
    kn_ref[...] = normed(k_ref[...].astype(F32), kg_ref[...])
    vb_ref = v_ref

    kc = kn_ref[0:n_ctx, :]
    vc = vb_ref[0:n_ctx, :]

    s = _dot_nt(_stack_heads(qn_ref[0:n_ctx, :], hd), kc)
    p = jnp.exp(s - jnp.max(s, axis=-1, keepdims=True))
    o2 = _dot(p.astype(BF16), vc) / jnp.sum(p, axis=-1, keepdims=True)
    o_ref[0:n_ctx, :] = _unstack_heads(o2, hd).astype(o_ref.dtype)

    kr = NA_KR
    n_win = kr * GRID_W

    def window(r):
        r = jnp.clip(r, 0, rows_grid - 1)
        rs = jnp.clip(r - kr // 2, 0, rows_grid - kr)
        q0 = pl.multiple_of(n_ctx + r * GRID_W, GRID_W)
        k0 = pl.multiple_of(n_ctx + rs * GRID_W, GRID_W)
        return q0, k0, rs - r + (kr - 1)

    gap = NA_SLOTS // 2
    for slot in range(NA_SLOTS):
        s_ref[slot] = jnp.zeros(s_ref.shape[1:], F32)
        p_ref[slot] = jnp.zeros(p_ref.shape[1:], BF16)
        den_ref[slot] = jnp.ones(den_ref.shape[1:], F32)

    def step(i, slot):
        q0, k0, _ = window(i - 2 * gap)
        pr = p_ref[slot]
        o2 = (_dot(pr[:, :n_win], vb_ref[pl.ds(k0, n_win), :]) + _dot(pr[:, n_win:], vc)) / den_ref[slot]
        o_ref[pl.ds(q0, GRID_W), :] = _unstack_heads(o2, hd).astype(o_ref.dtype)

        mid = (slot + gap) % NA_SLOTS
        sc = s_ref[mid]
        pe = jnp.exp(sc - jnp.max(sc, axis=-1, keepdims=True))
        den_ref[mid] = jnp.sum(pe, axis=-1, keepdims=True)
        p_ref[mid] = pe.astype(BF16)

        q0, k0, off = window(i)
        q2 = _stack_heads(qn_ref[pl.ds(q0, GRID_W), :], hd)
        bias = bias_ref[off % 2, :, pl.ds(pl.multiple_of((off // 2) * (2 * GRID_W), 2 * GRID_W), n_win)]
        s_ref[slot, :, :n_win] = _dot_nt(q2, kn_ref[pl.ds(k0, n_win), :]) + bias
        s_ref[slot, :, n_win:] = _dot_nt(q2, kc)

    def steps(j, carry):
        for slot in range(NA_SLOTS):
            step(NA_SLOTS * j + slot, slot)
        return carry

    lax.fori_loop(0, (rows_grid + 2 * gap) // NA_SLOTS, steps, 0)


def _na_attention(qkv3, q_g, k_g, rpb, n_ctx):
    bsz, t, d3 = qkv3.shape
    d = d3 // 3
    hd = d // NA_HEADS
    w = 2 * hd
    pairs = d // w
    rows_grid = (t - n_ctx) // GRID_W
    assert w == LANES and rows_grid >= NA_KR and rows_grid % NA_SLOTS == 0
    n_keys = NA_KR * GRID_W + n_ctx
    bias = _na_bias_table(rpb)
    kern = functools.partial(_na_kernel, n_ctx=n_ctx, scale=hd ** -0.5)

    def col(group):
        return pl.BlockSpec((None, t, w), lambda hp, b: (b, 0, group * pairs + hp))

    gain = lambda g: jnp.tile(g.astype(F32), 2).reshape(1, w)
    return pl.pallas_call(
        kern, grid=(pairs, bsz),
        in_specs=[col(0), col(1), col(2),
                  pl.BlockSpec((1, w), lambda hp, b: (0, 0)),
                  pl.BlockSpec((1, w), lambda hp, b: (0, 0)),
                  pl.BlockSpec((None,) + bias.shape[1:], lambda hp, b: (hp, 0, 0, 0))],
        out_specs=pl.BlockSpec((None, t, w), lambda hp, b: (b, 0, hp)),
        out_shape=jax.ShapeDtypeStruct((bsz, t, d), BF16),
        scratch_shapes=[pltpu.VMEM((t, w), BF16)] * 2
        + [pltpu.VMEM((NA_SLOTS, 2 * GRID_W, n_keys), F32), pltpu.VMEM((NA_SLOTS, 2 * GRID_W, n_keys), BF16),
           pltpu.VMEM((NA_SLOTS, 2 * GRID_W, 1), F32)],
        compiler_params=_cparams("parallel", "parallel"), name="na_attention",
    )(qkv3, qkv3, qkv3, gain(q_g), gain(k_g), bias)


FFN_ROWS = 384


def _swiglu_kernel(h_ref, g_ref, shc_ref, sh_ref, scc_ref, sc_ref, w1_ref, w3_ref, w2_ref, gtc_ref, gt_ref,
                   o_ref, *, n_ctx, tiles_per_batch):
    tile_in_batch = pl.program_id(0) % tiles_per_batch
    x = _modulated_rows(h_ref, g_ref, shc_ref, sh_ref, scc_ref, sc_ref, tile_in_batch, n_ctx).astype(BF16)
    hid = _silu(_dot(x, w1_ref[...])) * _dot(x, w3_ref[...])
    y = _dot(hid.astype(BF16), w2_ref[...])
    o_ref[...] = h_ref[...] + _row_gate(gtc_ref, gt_ref, y.shape, tile_in_batch, n_ctx) * y


def _resident(block_shape, index_map):
    return pl.BlockSpec(block_shape, index_map, pipeline_mode=pl.Buffered(1))


def _swiglu_residual(h, g, w1, w3, w2, j, mods, n_batch, t, n_ctx):
    n, d = h.shape
    dff = w1.shape[2]
    tm = FFN_ROWS
    tpb = t // tm
    assert t % tm == 0 and n_ctx <= tm
    kern = functools.partial(_swiglu_kernel, n_ctx=n_ctx, tiles_per_batch=tpb)
    return pl.pallas_call(
        kern, grid=(n // tm,),
        in_specs=[pl.BlockSpec((tm, d), lambda i: (i, 0)),
                  pl.BlockSpec((1, d), lambda i: (0, 0))]
        + _stream_specs(d, n_batch, tpb, 3, 1) + _stream_specs(d, n_batch, tpb, 4, 1)
        + [_resident((None, d, dff), lambda i: (j, 0, 0)),
           _resident((None, d, dff), lambda i: (j, 0, 0)),
           _resident((None, dff, d), lambda i: (j, 0, 0))] + _stream_specs(d, n_batch, tpb, 5, 1),
        out_specs=pl.BlockSpec((tm, d), lambda i: (i, 0)),
        out_shape=jax.ShapeDtypeStruct((n, d), F32),
        compiler_params=_cparams("parallel"), name="swiglu",
    )(h, g, mods, mods, mods, mods, w1, w3, w2, mods, mods)


MOE_ROWS = 512


def _route_tables(route, tm, n_exp):
    n = route.shape[0]
    e_pair = route[:, 0:2].astype(jnp.int32).reshape(-1)
    onehot = (e_pair[:, None] == jnp.arange(n_exp, dtype=jnp.int32)[None, :]).astype(jnp.int32)
    csum = jnp.cumsum(onehot, axis=0)
    rank = jnp.sum((csum - onehot) * onehot, axis=1)
    counts = csum[-1]
    padded = ((counts + tm - 1) // tm) * tm
    ends = jnp.cumsum(padded)
    slot = jnp.sum(onehot * (ends - padded)[None, :], axis=1) + rank
    p_slots = 2 * n + n_exp * tm
    n_tiles = p_slots // tm
    n_used = ends[-1] // tm
    tile = jnp.arange(n_tiles, dtype=jnp.int32)
    te = jnp.sum(((tile * tm)[:, None] >= ends[None, :]).astype(jnp.int32), axis=1)
    te = jnp.where(tile < n_used, te, te[n_used - 1])
    return slot.reshape(n, 2), p_slots, te, n_used.reshape(1)


def _gather_rows(table, idx):
    p = idx.shape[0]
    m = table.shape[1]
    mesh = plsc.VectorSubcoreMesh(core_axis_name="c", subcore_axis_name="s")

    @functools.partial(pl.kernel, out_type=jax.ShapeDtypeStruct((p, m), table.dtype), mesh=mesh)
    def gather(table_hbm, idx_hbm, out_hbm):
        def body(idx_vmem, out_vmem):
            pltpu.sync_copy(table_hbm.at[idx_vmem.at[0]], out_vmem)

        pltpu.emit_pipeline(
            body, grid=(p // SC_WINDOW,),
            in_specs=[pl.BlockSpec((1, SC_WINDOW), lambda i: (0, i))],
            out_specs=[pl.BlockSpec((SC_WINDOW, m), lambda i: (i, 0))],
            core_axis_name=("c", "s"), dimension_semantics=(pltpu.PARALLEL,),
        )(idx_hbm, out_hbm)

    return gather(table, idx.reshape(1, p))


def _scatter_rows(rows, idx_a, idx_b, n_out):
    q, m = rows.shape
    mesh = plsc.VectorSubcoreMesh(core_axis_name="c", subcore_axis_name="s")

    @functools.partial(pl.kernel, out_type=jax.ShapeDtypeStruct((n_out, m), rows.dtype), mesh=mesh)
    def scatter(rows_hbm, ia_hbm, ib_hbm, out_hbm):
        def body(rows_vmem, ia_vmem, ib_vmem):
            pltpu.sync_copy(rows_vmem, out_hbm.at[ia_vmem.at[0]])
            pltpu.sync_copy(rows_vmem, out_hbm.at[ib_vmem.at[0]])

        pltpu.emit_pipeline(
            body, grid=(q // SC_WINDOW,),
            in_specs=[pl.BlockSpec((SC_WINDOW, m), lambda i: (i, 0)),
                      pl.BlockSpec((1, SC_WINDOW), lambda i: (0, i)),
                      pl.BlockSpec((1, SC_WINDOW), lambda i: (0, i))],
            out_specs=[],
            core_axis_name=("c", "s"), dimension_semantics=(pltpu.PARALLEL,),
        )(rows_hbm, ia_hbm, ib_hbm)

    return scatter(rows, idx_a.reshape(1, q), idx_b.reshape(1, q))


def _scatter_pieces(pieces, slot, n_slots):
    npc, r, m = pieces.shape
    offs = jnp.arange(npc, dtype=jnp.int32)[:, None] * n_slots
    idx_a = (slot[None, :, 0] + offs).reshape(-1)
    idx_b = (slot[None, :, 1] + offs).reshape(-1)
    return _scatter_rows(pieces.reshape(npc * r, m), idx_a, idx_b, npc * n_slots).reshape(npc, n_slots, m)


def _gather_pieces(pieces, idx):
    npc, v, m = pieces.shape
    offs = (jnp.arange(npc, dtype=jnp.int32) * v).reshape((npc,) + (1,) * idx.ndim)
    flat = (idx[None] + offs).reshape(-1)
    return _gather_rows(pieces.reshape(npc * v, m), flat).reshape((npc,) + idx.shape + (m,))


def _moe_ffn_kernel(te_ref, nu_ref, x_ref, w1_ref, w3_ref, w2_ref, y_ref):
    @pl.when(pl.program_id(0) < nu_ref[0])
    def _():
        x = _load_unpacked(x_ref).astype(BF16)
        hid = _silu(_dot(x, w1_ref[...])) * _dot(x, w3_ref[...])
        _store_pieces(y_ref, _pack_halves(_dot(hid.astype(BF16), w2_ref[...])))


def _moe_ffn(xs, te, n_used, w1, w3, w2, j):
    npc, p, m = xs.shape
    d = 2 * npc * m
    dff = w1.shape[3]
    tm = MOE_ROWS

    def row(i, te_ref, nu_ref):
        return (0, jnp.minimum(i, nu_ref[0] - 1), 0)

    def expert(i, te_ref, nu_ref):
        return (j, te_ref[i], 0, 0)

    grid_spec = pltpu.PrefetchScalarGridSpec(
        num_scalar_prefetch=2, grid=(p // tm,),
        in_specs=[pl.BlockSpec((npc, tm, m), row),
                  _resident((None, None, d, dff), expert),
                  _resident((None, None, d, dff), expert),
                  _resident((None, None, dff, d), expert)],
        out_specs=pl.BlockSpec((npc, tm, m), row))
    return pl.pallas_call(
        _moe_ffn_kernel, grid_spec=grid_spec,
        out_shape=jax.ShapeDtypeStruct((npc, p, m), jnp.int32),
        compiler_params=_cparams("arbitrary"), name="moe_ffn",
    )(te, n_used, xs, w1, w3, w2)


def _moe_combine_kernel(ya_ref, yb_ref, route_ref, h_ref, gt_ref, o_ref):
    r = route_ref[...]
    y = r[:, 2:3] * _load_unpacked(ya_ref) + r[:, 3:4] * _load_unpacked(yb_ref)
    o_ref[...] = h_ref[...] + gt_ref[...] * y


def _moe_combine(yg, route, h, mods, n_batch, tiles_per_batch, ctx_tiles):
    d = h.shape[1]
    npc, _, r, m = yg.shape
    tm = TOKEN_TILE
    per = tiles_per_batch - ctx_tiles
    h_idx = lambda i: ((i // per) * tiles_per_batch + ctx_tiles + i % per, 0)
    if ctx_tiles == 0:
        mod_idx = _mod_row_spec(5, tiles_per_batch, n_batch)
    else:
        mod_idx = lambda i: (i // per, 5, 0, 0)
    return pl.pallas_call(
        _moe_combine_kernel, grid=(r // tm,),
        in_specs=[pl.BlockSpec((npc, None, tm, m), lambda i: (0, 0, i, 0)),
                  pl.BlockSpec((npc, None, tm, m), lambda i: (0, 1, i, 0)),
                  pl.BlockSpec((tm, LANES), lambda i: (i, 0)),
                  pl.BlockSpec((tm, d), h_idx),
                  pl.BlockSpec((None, None, 1, d), mod_idx)],
        out_specs=pl.BlockSpec((tm, d), lambda i: (i, 0)),
        out_shape=jax.ShapeDtypeStruct((r, d), F32),
        compiler_params=_cparams("parallel"), name="moe_combine",
    )(yg, yg, route, h, mods)


def _moe_residual(u_pieces, route, w1, w3, w2, j, h, mods, n_batch, t, n_ctx, latent_only):
    if latent_only:
        npc, _, m = u_pieces.shape
        u_pieces = u_pieces.reshape(npc, n_batch, t, m)[:, :, n_ctx:].reshape(npc, -1, m)
        route = route.reshape(n_batch, t, LANES)[:, n_ctx:].reshape(-1, LANES)
    slot, n_slots, te, n_used = _route_tables(route, MOE_ROWS, w1.shape[1])
    xs = _scatter_pieces(u_pieces, slot, n_slots)
    ys = _moe_ffn(xs, te, n_used, w1, w3, w2, j)
    yg = _gather_pieces(ys, slot.T)
    ctx_tiles = n_ctx // TOKEN_TILE if latent_only else 0
    return _moe_combine(yg, route, h, mods, n_batch, t // TOKEN_TILE, ctx_tiles)


def kernel(x, c, ctx, c_ctx, w_mod, b_mod, norm_mix_g, norm_ffn_g, ev_w_in, ev_w_out, hg_lb_logits, hg_onorm_g, rg_conv_w, rg_conv_b, rg_wa, rg_ba, rg_wx, rg_bx, rg_lambda, na_w_qkv, na_w_o, na_q_g, na_k_g, na_rpb, ffn_w1, ffn_w3, ffn_w2, moe_router, moe_w1, moe_w3, moe_w2):
    bsz, seq, d = x.shape
    n_ctx = ctx.shape[1]
    t = n_ctx + seq
    depth = w_mod.shape[0]
    tpb = t // TOKEN_TILE
    assert n_ctx % TOKEN_TILE == 0 and seq % TOKEN_TILE == 0 and t % ROW_TILE == 0 and n_ctx <= ROW_TILE

    rows = -(-(bsz + 1) // 8) * 8
    cond = jnp.zeros((rows, d), F32).at[:bsz].set(c).at[bsz].set(c_ctx)
    mods_all = _adaln(cond, w_mod, b_mod).reshape(depth, rows, N_MOD, 1, d)

    lb_p = jax.nn.softmax(hg_lb_logits.astype(F32), axis=0)
    lb_all = jnp.cumsum(lb_p, axis=0) - lb_p[0]

    bf = lambda w: w.astype(BF16)
    w_in, w_out, w_qkv, w_o = bf(ev_w_in), bf(ev_w_out), bf(na_w_qkv), bf(na_w_o)
    f_w1, f_w3, f_w2 = bf(ffn_w1), bf(ffn_w3), bf(ffn_w2)
    m_w1, m_w3, m_w2 = bf(moe_w1), bf(moe_w3), bf(moe_w2)

    h = jnp.concatenate([ctx, x], axis=1).reshape(bsz * t, d)
    for layer in range(depth):
        j = layer // 2
        mods = mods_all[layer]
        g_mix = norm_mix_g[layer].reshape(1, d)
        if layer % 2 == 0:
            p = _modulated_matmul(h, g_mix, mods, w_in, j, F32, ROW_TILE // 2, bsz, t, n_ctx)
            p3 = p.reshape(bsz, t, -1)
            hg = _hgrn2(p3, lb_all[j], n_ctx)
            half = hg.shape[-1]
            rg = _rglru(p3, 5 * half // (d - half), rg_conv_w[j], rg_conv_b[j], rg_wa[j], rg_ba[j],
                        rg_wx[j], rg_bx[j], rg_lambda[j], n_ctx)
            h = _merge_out(hg.reshape(bsz * t, half), rg.reshape(bsz * t, d - half), p, hg_onorm_g[j],
                           w_out, j, h, mods, bsz, t, n_ctx)
        else:
            qkv = _modulated_matmul(h, g_mix, mods, w_qkv, j, BF16, ROW_TILE, bsz, t, n_ctx)
            o = _na_attention(qkv.reshape(bsz, t, 3 * d), na_q_g[j], na_k_g[j], na_rpb[j], n_ctx)
            h = _matmul_residual(o.reshape(bsz * t, d), w_o, j, h, mods, 2, bsz, t, n_ctx)
        if layer % 2 == 0:
            h = _swiglu_residual(h, norm_ffn_g[layer].reshape(1, d), f_w1, f_w3, f_w2, j, mods, bsz, t, n_ctx)
        else:
            u, route = _modulate_router(h, norm_ffn_g[layer].reshape(1, d), mods, 3, bsz, tpb, moe_router[j])
            h = _moe_residual(u, route, m_w1, m_w3, m_w2, j, h, mods, bsz, t, n_ctx, layer == depth - 1)
    if depth % 2 == 0:
        return h.reshape(bsz, seq, d)
    return h.reshape(bsz, t, d)[:, n_ctx:]
```

```python
import functools

import numpy as np
import jax
import jax.numpy as jnp
from jax import lax
from jax.experimental import pallas as pl
from jax.experimental.pallas import tpu as pltpu
from jax.experimental.pallas import tpu_sc as plsc

F32 = jnp.float32
BF16 = jnp.bfloat16

EPS = 1e-6
NEG_BIG = -1e9
LB_MIN = 1e-30
LOG2_E = 1.4426950408889634
N_MOD = 6
GRID_W = 64
HG_HEAD_DIM = 128
RG_BLOCKS = 8
RG_CONV = 4
RG_C = 8.0
NA_HEADS = 16
NA_KR = 8
NA_KC = 16
N_EXPERTS = 8

LANES = 128
SC_WINDOW = 128
SC_PIECES = 2
TOKEN_TILE = 256
HG_CHUNK = 128
HG_LEVELS = 7
HG_SLOTS = 2
NA_SLOTS = 4
RG_SCAN_BLOCK = 256
RG_GROUP = 8
VMEM_LIMIT = 56 * 1024 * 1024


def _cparams(*sem):
    return pltpu.CompilerParams(dimension_semantics=sem, vmem_limit_bytes=VMEM_LIMIT)


def _dot(a, b):
    return jnp.dot(a, b, preferred_element_type=F32)


def _dot_nt(a, b):
    return lax.dot_general(a, b, (((1,), (1,)), ((), ())), preferred_element_type=F32)


def _dot_tn(a, b):
    return lax.dot_general(a, b, (((0,), (0,)), ((), ())), preferred_element_type=F32)


def _split2(x):
    hi = x.astype(BF16)
    lo = (x - hi.astype(F32)).astype(BF16)
    return hi, lo


def _sigmoid(x):
    return 0.5 * jnp.tanh(0.5 * x) + 0.5


def _silu(x):
    return x * _sigmoid(x)


def _softplus_neg_abs(x):
    return jnp.log(1.0 + jnp.exp(-jnp.abs(x)))


def _log_sigmoid(x):
    return jnp.minimum(x, 0.0) - _softplus_neg_abs(x)


def _gelu_tanh(x):
    return 0.5 * x * (1.0 + jnp.tanh(np.sqrt(2.0 / np.pi) * (x + 0.044715 * (x * x * x))))


def _adaln_kernel(c_ref, w_ref, b_ref, o_ref):
    a_hi, a_lo = _split2(_silu(c_ref[...]))
    w_hi, w_lo = _split2(w_ref[...])
    o_ref[...] = _dot(a_hi, w_hi) + _dot(a_hi, w_lo) + _dot(a_lo, w_hi) + b_ref[...]


def _adaln(cond, w_mod, b_mod):
    depth, d, n = w_mod.shape
    r = cond.shape[0]
    tn = 1536
    return pl.pallas_call(
        _adaln_kernel,
        grid=(depth, n // tn),
        in_specs=[pl.BlockSpec((r, d), lambda l, j: (0, 0)),
                  pl.BlockSpec((None, d, tn), lambda l, j: (l, 0, j)),
                  pl.BlockSpec((None, 1, tn), lambda l, j: (l, 0, j))],
        out_specs=pl.BlockSpec((None, r, tn), lambda l, j: (l, 0, j)),
        out_shape=jax.ShapeDtypeStruct((depth, r, n), F32),
        compiler_params=_cparams("parallel", "parallel"),
        name="adaln",
    )(cond, w_mod, b_mod.reshape(depth, 1, n))


def _mod_row_spec(k, tiles_per_batch, n_batch):
    def idx(i):
        row = jnp.where(i % tiles_per_batch == 0, n_batch, i // tiles_per_batch)
        return (row, k, 0, 0)
    return idx


def _modulated(h_ref, g_ref, sh_ref, sc_ref):
    x = h_ref[...]
    y = x * lax.rsqrt(jnp.mean(x * x, axis=-1, keepdims=True) + EPS)
    return (y * g_ref[...]) * (1.0 + sc_ref[...]) + sh_ref[...]


def _pack_halves(x):
    m = x.shape[1] // 2
    lo = lax.bitcast_convert_type(x[:, :m].astype(BF16).astype(F32), jnp.int32)
    hi = lax.bitcast_convert_type(x[:, m:].astype(BF16).astype(F32), jnp.int32)
    return lax.shift_right_logical(lo, 16) | (hi & jnp.int32(-65536))


def _unpack_halves(w):
    lo = lax.bitcast_convert_type(lax.shift_left(w, 16), F32)
    hi = lax.bitcast_convert_type(w & jnp.int32(-65536), F32)
    return lo, hi


def _store_pieces(ref, w):
    m = w.shape[1] // SC_PIECES
    for j in range(SC_PIECES):
        ref[j] = w[:, j * m:(j + 1) * m]


def _load_unpacked(ref):
    parts = [_unpack_halves(ref[j]) for j in range(SC_PIECES)]
    return jnp.concatenate([lo for lo, _ in parts] + [hi for _, hi in parts], axis=1)


def _modulate_router_kernel(h_ref, g_ref, sh_ref, sc_ref, wr_ref, u_ref, route_ref, route_t_ref, cnt_ref,
                            *, tiles_per_batch, skip_tiles):
    i = pl.program_id(0)

    @pl.when(i == 0)
    def _():
        cnt_ref[...] = jnp.zeros_like(cnt_ref)

    u = _modulated(h_ref, g_ref, sh_ref, sc_ref)
    _store_pieces(u_ref, _pack_halves(u))
    u0, u1 = _split2(u)
    w0, w1 = _split2(wr_ref[...])
    logits = _dot(u0, w0) + _dot(u0, w1) + _dot(u1, w0)
    lane = lax.broadcasted_iota(jnp.int32, logits.shape, 1).astype(F32)
    lg = jnp.where(lane < N_EXPERTS, logits, -jnp.inf)
    m1 = jnp.max(lg, axis=-1, keepdims=True)
    i1 = jnp.min(jnp.where(lg == m1, lane, float(LANES)), axis=-1, keepdims=True)
    lg2 = jnp.where(lane == i1, -jnp.inf, lg)
    m2 = jnp.max(lg2, axis=-1, keepdims=True)
    i2 = jnp.min(jnp.where(lg2 == m2, lane, float(LANES)), axis=-1, keepdims=True)
    e2 = jnp.exp(m2 - m1)
    den = 1.0 + e2

    routed = jnp.where(i % tiles_per_batch >= skip_tiles, 1.0, 0.0)
    pick_a = jnp.where(lane == i1, routed, 0.0)
    pick_b = jnp.where(lane == i2, routed, 0.0)
    tm = pick_a.shape[0]
    earlier = jnp.where(lax.broadcasted_iota(jnp.int32, (tm, tm), 1) < lax.broadcasted_iota(jnp.int32, (tm, tm), 0),
                        1.0, 0.0).astype(BF16)
    total_a = jnp.sum(pick_a, axis=0, keepdims=True)
    base = cnt_ref[0:1, :]
    rank_a = jnp.sum(pick_a * (base + _dot(earlier, pick_a.astype(BF16))), axis=-1, keepdims=True)
    rank_b = jnp.sum(pick_b * (base + total_a + _dot(earlier, pick_b.astype(BF16))), axis=-1, keepdims=True)
    cnt_ref[0:1, :] = base + total_a + jnp.sum(pick_b, axis=0, keepdims=True)

    route = jnp.where(lane == 0, i1, jnp.where(lane == 1, i2, jnp.where(lane == 2, 1.0 / den, jnp.where(
        lane == 3, e2 / den, jnp.where(lane == 4, rank_a, rank_b)))))
    route_ref[...] = route
    route_t_ref[...] = route.T[0:route_t_ref.shape[0], :]


def _modulate_router(h, g, mods, k_shift, n_batch, tiles_per_batch, skip_tiles, w_router):
    n, d = h.shape
    tm = TOKEN_TILE
    row = pl.BlockSpec((tm, d), lambda i: (i, 0))
    vec = pl.BlockSpec((1, d), lambda i: (0, 0))

    def mod_spec(k):
        return pl.BlockSpec((None, None, 1, d), _mod_row_spec(k, tiles_per_batch, n_batch))

    in_specs = [row, vec, mod_spec(k_shift), mod_spec(k_shift + 1)]
    wr = jnp.zeros((d, LANES), F32).at[:, :N_EXPERTS].set(w_router)
    kern = functools.partial(_modulate_router_kernel, tiles_per_batch=tiles_per_batch, skip_tiles=skip_tiles)
    return pl.pallas_call(
        kern, grid=(n // tm,),
        in_specs=in_specs + [pl.BlockSpec((d, LANES), lambda i: (0, 0))],
        out_specs=[pl.BlockSpec((SC_PIECES, tm, d // 2 // SC_PIECES), lambda i: (0, i, 0)),
                   pl.BlockSpec((tm, LANES), lambda i: (i, 0)),
                   pl.BlockSpec((8, tm), lambda i: (0, i)),
                   pl.BlockSpec((8, LANES), lambda i: (0, 0))],
        out_shape=[jax.ShapeDtypeStruct((SC_PIECES, n, d // 2 // SC_PIECES), jnp.int32),
                   jax.ShapeDtypeStruct((n, LANES), F32),
                   jax.ShapeDtypeStruct((8, n), F32),
                   jax.ShapeDtypeStruct((8, LANES), F32)],
        compiler_params=_cparams("arbitrary"), name="modulate_router",
    )(h, g, mods, mods, wr)


ROW_TILE = 768


def _row_gate(ctx_ref, own_ref, shape, tile_in_batch, n_ctx):
    rows = lax.broadcasted_iota(jnp.int32, shape, 0)
    is_ctx = (rows < n_ctx) & (tile_in_batch == 0)
    return jnp.where(is_ctx, ctx_ref[...], own_ref[...])


def _modulated_rows(h_ref, g_ref, shc_ref, sh_ref, scc_ref, sc_ref, tile_in_batch, n_ctx):
    x = h_ref[...]
    y = x * lax.rsqrt(jnp.mean(x * x, axis=-1, keepdims=True) + EPS)
    shift = _row_gate(shc_ref, sh_ref, x.shape, tile_in_batch, n_ctx)
    scale = _row_gate(scc_ref, sc_ref, x.shape, tile_in_batch, n_ctx)
    return (y * g_ref[...]) * (1.0 + scale) + shift


def _stream_specs(d, n_batch, tiles_per_batch, k, nargs):
    if nargs == 1:
        ctx = lambda i: (n_batch, k, 0, 0)
        own = lambda i: (i // tiles_per_batch, k, 0, 0)
    else:
        ctx = lambda i, f: (n_batch, k, 0, 0)
        own = lambda i, f: (i // tiles_per_batch, k, 0, 0)
    return [pl.BlockSpec((None, None, 1, d), ctx), pl.BlockSpec((None, None, 1, d), own)]


def _mod_mm_kernel(h_ref, g_ref, shc_ref, sh_ref, scc_ref, sc_ref, w_ref, o_ref, *, n_ctx, tiles_per_batch):
    x = _modulated_rows(h_ref, g_ref, shc_ref, sh_ref, scc_ref, sc_ref,
                        pl.program_id(0) % tiles_per_batch, n_ctx).astype(BF16)
    o_ref[...] = _dot(x, w_ref[...]).astype(o_ref.dtype)


def _modulated_matmul(h, g, mods, w, j, out_dtype, tm, n_batch, t, n_ctx):
    n, d = h.shape
    nout = w.shape[2]
    tpb = t // tm
    assert t % tm == 0 and n_ctx <= tm
    kern = functools.partial(_mod_mm_kernel, n_ctx=n_ctx, tiles_per_batch=tpb)
    return pl.pallas_call(
        kern, grid=(n // tm,),
        in_specs=[pl.BlockSpec((tm, d), lambda i: (i, 0)),
                  pl.BlockSpec((1, d), lambda i: (0, 0))]
        + _stream_specs(d, n_batch, tpb, 0, 1) + _stream_specs(d, n_batch, tpb, 1, 1)
        + [pl.BlockSpec((None, d, nout), lambda i: (j, 0, 0))],
        out_specs=pl.BlockSpec((tm, nout), lambda i: (i, 0)),
        out_shape=jax.ShapeDtypeStruct((n, nout), out_dtype),
        compiler_params=_cparams("parallel"), name="modulated_matmul",
    )(h, g, mods, mods, mods, mods, w)


def _mm_res_kernel(x_ref, w_ref, h_ref, gtc_ref, gt_ref, o_ref, *, n_ctx, tiles_per_batch):
    gate = _row_gate(gtc_ref, gt_ref, h_ref.shape, pl.program_id(0) % tiles_per_batch, n_ctx)
    o_ref[...] = h_ref[...] + gate * _dot(x_ref[...], w_ref[...])


def _matmul_residual(x, w, j, h, mods, k_gate, n_batch, t, n_ctx):
    n, d = h.shape
    kdim = x.shape[1]
    tm = ROW_TILE
    tpb = t // tm
    kern = functools.partial(_mm_res_kernel, n_ctx=n_ctx, tiles_per_batch=tpb)
    return pl.pallas_call(
        kern, grid=(n // tm,),
        in_specs=[pl.BlockSpec((tm, kdim), lambda i: (i, 0)),
                  pl.BlockSpec((None, kdim, d), lambda i: (j, 0, 0)),
                  pl.BlockSpec((tm, d), lambda i: (i, 0))] + _stream_specs(d, n_batch, tpb, k_gate, 1),
        out_specs=pl.BlockSpec((tm, d), lambda i: (i, 0)),
        out_shape=jax.ShapeDtypeStruct((n, d), F32),
        compiler_params=_cparams("parallel"), name="matmul_residual",
    )(x, w, h, mods, mods)


def _hg_tables():
    c = HG_CHUNK
    t = np.arange(c)[:, None]
    u = np.arange(c)[None, :]
    mats = [(u <= t), (u > t)]
    for k in range(1, HG_LEVELS + 1):
        m = ((t >> k) << k) + (1 << (k - 1)) - 1
        upper = t > m
        mats.append(np.where(upper, (u > m) & (u <= t), (u > t) & (u <= m)))
    fwd = np.concatenate(mats, axis=0).astype(np.float32)
    bwd = fwd.reshape(-1, c, c)[:, ::-1, ::-1].reshape(-1, c)
    x = t ^ u
    lvl = np.where(t > u, np.floor(np.log2(np.maximum(x, 1))).astype(np.int32) + 1, -1)
    lvl = np.where(t == u, 0, lvl).astype(np.int32)
    return np.stack([fwd, bwd]), np.stack([lvl, lvl.T])


def _hg_kernel(q_ref, ff_ref, fb_ref, v_ref, lb_ref, m_ref, lvl_ref, o_ref, of_ref, ob_ref,
               qk_ref, el_ref, io_ref, tot_ref, att_ref, *, n_ctx_chunks, n_chunks):
    c = HG_CHUNK
    gap = HG_SLOTS // 2
    t = o_ref.shape[0]
    z_refs = (ff_ref, fb_ref)
    out_refs = (of_ref, ob_ref)
    for ref in (qk_ref, el_ref, io_ref, tot_ref, att_ref):
        ref[...] = jnp.zeros(ref.shape, ref.dtype)

    def start_row(d, j):
        j = jnp.clip(j, 0, n_chunks - 1)
        if d == 1:
            j = jnp.where(j < n_ctx_chunks, n_ctx_chunks - 1 - j, n_chunks - 1 - (j - n_ctx_chunks))
        return pl.multiple_of(j * c, c)

    def stage1(d, j, slot):
        t0 = start_row(d, j)
        lb = lb_ref[d:d + 1, :]
        log_lb = jnp.log(jnp.maximum(lb, LB_MIN))
        q = _silu(q_ref[pl.ds(t0, c), :])
        z = z_refs[d][pl.ds(t0, c), :]
        k = (1.0 - lb) * _sigmoid(-z)
        b = jnp.log1p(-lb) + _log_sigmoid(z)
        logf = jnp.maximum(log_lb, b) + _softplus_neg_abs(log_lb - b)
        g_hi, g_lo = _split2(logf * LOG2_E)
        m_ends = m_ref[d, 0:2 * c, :]
        e_ends = jnp.exp2(_dot(m_ends, g_hi) + _dot(m_ends, g_lo))
        el_ref[d, slot] = jnp.exp2(_dot(m_ref[d, 2 * c:, :], g_hi).astype(BF16))
        qk_ref[d, slot, 0] = q.astype(BF16)
        qk_ref[d, slot, 1] = k.astype(BF16)
        io_ref[d, slot, 0] = (q * e_ends[0:c]).astype(BF16)
        io_ref[d, slot, 1] = (k * e_ends[c:]).astype(BF16)
        tot_row = c - 1 if d == 0 else 0
        tot_ref[d, slot, 0:1, :] = e_ends[tot_row:tot_row + 1, :]

    def stage2(d, slot):
        qb = qk_ref[d, slot, 0]
        kb = qk_ref[d, slot, 1]
        lvl = lvl_ref[d]
        att = jnp.where(lvl == 0, _dot_nt(qb, kb), 0.0)
        for lev in range(1, HG_LEVELS + 1):
            el = el_ref[d, slot, (lev - 1) * c:lev * c, :]
            att = jnp.where(lvl == lev, _dot_nt(qb * el, kb * el), att)
        att_ref[d, slot] = att.astype(BF16)

    def stage3(d, j, slot, st):
        t0 = start_row(d, j)
        v = v_ref[pl.ds(t0, c), :].astype(BF16)
        row = jnp.where(j < n_chunks, t0, t)
        out_refs[d][pl.ds(pl.multiple_of(row, c), c), :] = (
            _dot(att_ref[d, slot], v) + _dot_nt(io_ref[d, slot, 0], st.astype(BF16)))
        return st * tot_ref[d, slot, 0:1, :] + _dot_tn(v, io_ref[d, slot, 1])

    def steps(jj, carry):
        st = list(carry)
        for slot in range(HG_SLOTS):
            i = HG_SLOTS * jj + slot
            for d in range(2):
                st[d] = stage3(d, i - 2 * gap, slot, st[d])
            for d in range(2):
                stage2(d, (slot + gap) % HG_SLOTS)
            for d in range(2):
                stage1(d, i, slot)
        return tuple(st)

    zero = jnp.zeros((c, c), F32)
    n_steps = -(-(n_chunks + 2 * gap) // HG_SLOTS)
    lax.fori_loop(0, n_steps, steps, (zero, zero))
    o_ref[...] = (of_ref[0:t, :] + ob_ref[0:t, :]).astype(o_ref.dtype)


def _hgrn2(p3, lb, n_ctx):
    bsz, t, _ = p3.shape
    width = lb.shape[1]
    heads = width // HG_HEAD_DIM
    mats, lvl = _hg_tables()
    c = HG_CHUNK

    def col(group):
        return pl.BlockSpec((None, t, c), lambda b, h: (b, 0, group * heads + h))

    kern = functools.partial(_hg_kernel, n_ctx_chunks=n_ctx // c, n_chunks=t // c)
    return pl.pallas_call(
        kern, grid=(bsz, heads),
        in_specs=[col(0), col(1), col(2), col(3),
                  pl.BlockSpec((2, c), lambda b, h: (0, h)),
                  pl.BlockSpec(mats.shape, lambda b, h: (0, 0, 0)),
                  pl.BlockSpec(lvl.shape, lambda b, h: (0, 0, 0))],
        out_specs=pl.BlockSpec((None, t, c), lambda b, h: (b, 0, h)),
        out_shape=jax.ShapeDtypeStruct((bsz, t, width), F32),
        scratch_shapes=[pltpu.VMEM((t + c, c), F32), pltpu.VMEM((t + c, c), F32),
                        pltpu.VMEM((2, HG_SLOTS, 2, c, c), BF16), pltpu.VMEM((2, HG_SLOTS, HG_LEVELS * c, c), BF16),
                        pltpu.VMEM((2, HG_SLOTS, 2, c, c), BF16), pltpu.VMEM((2, HG_SLOTS, 8, c), F32),
                        pltpu.VMEM((2, HG_SLOTS, c, c), BF16)],
        compiler_params=_cparams("parallel", "parallel"), name="hgrn2",
    )(p3, p3, p3, p3, lb, jnp.asarray(mats, BF16), jnp.asarray(lvl))


PAD = 8


def _rg_kernel(x_ref, cw_ref, cb_ref, wa_ref, ba_ref, wx_ref, bx_ref, lam_ref, o_ref,
               xpad_ref, xc_ref, ob_ref, *, n_ctx):
    t, w = x_ref.shape
    n_lat = t - n_ctx
    lat0 = n_ctx + 2 * PAD
    xpad_ref[...] = jnp.zeros(xpad_ref.shape, F32)
    xpad_ref[PAD:PAD + n_ctx, :] = x_ref[0:n_ctx, :]
    xpad_ref[lat0:lat0 + n_lat, :] = x_ref[n_ctx:t, :]
    left = RG_CONV // 2
    for base, dst, n in ((PAD, 0, n_ctx), (lat0, n_ctx, n_lat)):
        acc = cb_ref[...] + cw_ref[0:1, :] * xpad_ref[base - left:base - left + n, :]
        for j in range(1, RG_CONV):
            acc = acc + cw_ref[j:j + 1, :] * xpad_ref[base - left + j:base - left + j + n, :]
        xc_ref[dst:dst + n, :] = acc

    blk = RG_SCAN_BLOCK
    n_blocks = t // blk
    n_ctx_blocks = n_ctx // blk
    sub = lax.broadcasted_iota(jnp.int32, (blk // RG_GROUP, RG_GROUP, w), 1)
    log_sig_lam = _log_sigmoid(lam_ref[...])

    def scan_block(d, t0, hin, out_ref):
        xc = xc_ref[pl.ds(t0, blk), :]
        xb = xc.astype(BF16)
        r = _sigmoid(_dot(xb, wa_ref[d]) + ba_ref[d:d + 1, :])
        ig = _sigmoid(_dot(xb, wx_ref[d]) + bx_ref[d:d + 1, :])
        log_a = RG_C * r * log_sig_lam[d:d + 1, :]
        a = jnp.exp(log_a)
        b = jnp.sqrt(jnp.maximum(-jnp.tanh(log_a) * (a * a + 1.0), 0.0)) * (ig * xc)
        n_groups = blk // RG_GROUP
        a = a.reshape(n_groups, RG_GROUP, w)
        b = b.reshape(n_groups, RG_GROUP, w)
        s = 1
        while s < RG_GROUP:
            shift = s if d == 0 else RG_GROUP - s
            keep = (sub >= s) if d == 0 else (sub < RG_GROUP - s)
            b = jnp.where(keep, a * pltpu.roll(b, shift, 1), 0.0) + b
            a = jnp.where(keep, a * pltpu.roll(a, shift, 1), a)
            s *= 2
        for gi in range(n_groups):
            gj = gi if d == 0 else n_groups - 1 - gi
            hg = a[gj] * hin + b[gj]
            out_ref[pl.ds(t0 + gj * RG_GROUP, RG_GROUP), :] = hg
            hin = hg[RG_GROUP - 1:RG_GROUP, :] if d == 0 else hg[0:1, :]
        return hin

    def step(i, carry):
        h_f, h_b = carry
        bb = jnp.where(i < n_ctx_blocks, n_ctx_blocks - 1 - i, n_blocks - 1 - (i - n_ctx_blocks))
        h_f = scan_block(0, pl.multiple_of(i * blk, blk), h_f, o_ref)
        h_b = scan_block(1, pl.multiple_of(bb * blk, blk), h_b, ob_ref)
        return h_f, h_b

    zero = jnp.zeros((1, w), F32)
    lax.fori_loop(0, n_blocks, step, (zero, zero))
    o_ref[...] += ob_ref[...]


def _blockdiag_dense(w):
    nd, nb, k, _ = w.shape
    eye = jnp.eye(nb, dtype=w.dtype)
    return jnp.einsum('dnij,nm->dnimj', w, eye).reshape(nd, nb * k, nb * k)


def _rglru(p3, col_block, conv_w, conv_b, wa, ba, wx, bx, lam, n_ctx):
    bsz, t, _ = p3.shape
    w = conv_w.shape[1]
    full = lambda shape: pl.BlockSpec(shape, lambda b: (0,) * len(shape))
    kern = functools.partial(_rg_kernel, n_ctx=n_ctx)
    return pl.pallas_call(
        kern, grid=(bsz,),
        in_specs=[pl.BlockSpec((None, t, w), lambda b: (b, 0, col_block)),
                  full((RG_CONV, w)), full((1, w)),
                  full((2, w, w)), full((2, w)), full((2, w, w)), full((2, w)), full((2, w))],
        out_specs=pl.BlockSpec((None, t, w), lambda b: (b, 0, 0)),
        out_shape=jax.ShapeDtypeStruct((bsz, t, w), F32),
        scratch_shapes=[pltpu.VMEM((t + 3 * PAD, w), F32)] + [pltpu.VMEM((t, w), F32)] * 2,
        compiler_params=_cparams("parallel"), name="rglru",
    )(p3, conv_w, conv_b.reshape(1, w), _blockdiag_dense(wa).astype(BF16), ba,
      _blockdiag_dense(wx).astype(BF16), bx, lam)


def _merge_kernel(hg_ref, rg_ref, gate_ref, y_ref, og_ref, w_ref, h_ref, gtc_ref, gt_ref, o_ref,
                  *, n_ctx, tiles_per_batch):
    hg = hg_ref[...]
    half = hg.shape[1]
    hn = hg * lax.rsqrt(jnp.mean(hg * hg, axis=-1, keepdims=True) + EPS) * og_ref[...]
    a = (hn * _silu(gate_ref[...])).astype(BF16)
    b = (rg_ref[...] * _gelu_tanh(y_ref[...])).astype(BF16)
    o = _dot(a, w_ref[0:half, :]) + _dot(b, w_ref[half:, :])
    gate = _row_gate(gtc_ref, gt_ref, h_ref.shape, pl.program_id(0) % tiles_per_batch, n_ctx)
    o_ref[...] = h_ref[...] + gate * o


def _merge_out(hg, rg, p, onorm_g, w_out, j, h, mods, n_batch, t, n_ctx):
    n, d = h.shape
    half = hg.shape[1]
    tm = ROW_TILE
    tpb = t // tm
    blk = lambda c: pl.BlockSpec((tm, half), lambda i: (i, c))
    kern = functools.partial(_merge_kernel, n_ctx=n_ctx, tiles_per_batch=tpb)
    return pl.pallas_call(
        kern, grid=(n // tm,),
        in_specs=[blk(0), blk(0), blk(4), blk(6),
                  pl.BlockSpec((1, half), lambda i: (0, 0)),
                  pl.BlockSpec((None, d, d), lambda i: (j, 0, 0)),
                  pl.BlockSpec((tm, d), lambda i: (i, 0))] + _stream_specs(d, n_batch, tpb, 2, 1),
        out_specs=pl.BlockSpec((tm, d), lambda i: (i, 0)),
        out_shape=jax.ShapeDtypeStruct((n, d), F32),
        compiler_params=_cparams("parallel"), name="merge_out",
    )(hg, rg, p, p, onorm_g.reshape(1, half), w_out, h, mods, mods)


def _na_bias_table(rpb):
    q = np.arange(GRID_W)[:, None]
    kc = np.arange(GRID_W)[None, :]
    wstart = np.clip(q - NA_KC // 2, 0, GRID_W - NA_KC)
    mask = (kc >= wstart) & (kc < wstart + NA_KC)
    coff = np.clip(kc - q + NA_KC - 1, 0, 2 * NA_KC - 2)
    sel = (coff[None] == np.arange(2 * NA_KC - 1)[:, None, None]).astype(np.float32)
    toe = jnp.einsum('hdj,jqk->hqdk', rpb.astype(F32), sel, precision=lax.Precision.HIGHEST)
    toe = jnp.where(mask[None, :, None, :], toe, NEG_BIG)
    h = toe.shape[0]
    flat = toe.reshape(h // 2, 2 * GRID_W, -1)
    width = (2 * NA_KR) * GRID_W
    even = jnp.pad(flat, ((0, 0), (0, 0), (0, width - flat.shape[2])))
    odd = jnp.pad(flat[:, :, GRID_W:], ((0, 0), (0, 0), (0, width - flat.shape[2] + GRID_W)))
    return jnp.stack([even, odd], axis=1)


def _stack_heads(x, hd):
    lane = lax.broadcasted_iota(jnp.int32, x.shape, 1)
    zero = jnp.zeros_like(x)
    return jnp.concatenate([jnp.where(lane < hd, x, zero), jnp.where(lane >= hd, x, zero)], axis=0)


def _unstack_heads(o2, hd):
    n = o2.shape[0] // 2
    lane = lax.broadcasted_iota(jnp.int32, (n, o2.shape[1]), 1)
    return jnp.where(lane < hd, o2[:n], o2[n:])


def _na_kernel(q_ref, k_ref, v_ref, qg_ref, kg_ref, bias_ref, o_ref, qn_ref, kn_ref,
               s_ref, p_ref, den_ref, *, n_ctx, scale):
    t, w = q_ref.shape
    hd = w // 2
    rows_grid = (t - n_ctx) // GRID_W
    r_i = lax.broadcasted_iota(jnp.int32, (w, w), 0)
    c_i = lax.broadcasted_iota(jnp.int32, (w, w), 1)
    seg = jnp.where((r_i // hd) == (c_i // hd), 1.0 / hd, 0.0).astype(BF16)

    def normed(x, g):
        ms = _dot((x * x).astype(BF16), seg)
        return (x * lax.rsqrt(ms + EPS) * g).astype(BF16)

    qn_ref[...] = normed(q_ref[...].astype(F32), qg_ref[...] * scale)
    kn_ref[...] = normed(k_ref[...].astype(F32), kg_ref[...])
    vb_ref = v_ref

    kc = kn_ref[0:n_ctx, :]
    vc = vb_ref[0:n_ctx, :]

    s = _dot_nt(_stack_heads(qn_ref[0:n_ctx, :], hd), kc)
    p = jnp.exp(s - jnp.max(s, axis=-1, keepdims=True))
    o2 = _dot(p.astype(BF16), vc) / jnp.sum(p, axis=-1, keepdims=True)
    o_ref[0:n_ctx, :] = _unstack_heads(o2, hd).astype(o_ref.dtype)

    kr = NA_KR
    n_win = kr * GRID_W

    def window(r):
        r = jnp.clip(r, 0, rows_grid - 1)
        rs = jnp.clip(r - kr // 2, 0, rows_grid - kr)
        q0 = pl.multiple_of(n_ctx + r * GRID_W, GRID_W)
        k0 = pl.multiple_of(n_ctx + rs * GRID_W, GRID_W)
        return q0, k0, rs - r + (kr - 1)

    gap = NA_SLOTS // 2
    for slot in range(NA_SLOTS):
        s_ref[slot] = jnp.zeros(s_ref.shape[1:], F32)
        p_ref[slot] = jnp.zeros(p_ref.shape[1:], BF16)
        den_ref[slot] = jnp.ones(den_ref.shape[1:], F32)

    def step(i, slot):
        q0, k0, _ = window(i - 2 * gap)
        pr = p_ref[slot]
        o2 = (_dot(pr[:, :n_win], vb_ref[pl.ds(k0, n_win), :]) + _dot(pr[:, n_win:], vc)) / den_ref[slot]
        o_ref[pl.ds(q0, GRID_W), :] = _unstack_heads(o2, hd).astype(o_ref.dtype)

        mid = (slot + gap) % NA_SLOTS
        sc = s_ref[mid]
        pe = jnp.exp(sc - jnp.max(sc, axis=-1, keepdims=True))
        den_ref[mid] = jnp.sum(pe, axis=-1, keepdims=True)
        p_ref[mid] = pe.astype(BF16)

        q0, k0, off = window(i)
        q2 = _stack_heads(qn_ref[pl.ds(q0, GRID_W), :], hd)
        bias = bias_ref[off % 2, :, pl.ds(pl.multiple_of((off // 2) * (2 * GRID_W), 2 * GRID_W), n_win)]
        s_ref[slot, :, :n_win] = _dot_nt(q2, kn_ref[pl.ds(k0, n_win), :]) + bias
        s_ref[slot, :, n_win:] = _dot_nt(q2, kc)

    def steps(j, carry):
        for slot in range(NA_SLOTS):
            step(NA_SLOTS * j + slot, slot)
        return carry

    lax.fori_loop(0, (rows_grid + 2 * gap) // NA_SLOTS, steps, 0)


def _na_attention(qkv3, q_g, k_g, rpb, n_ctx):
    bsz, t, d3 = qkv3.shape
    d = d3 // 3
    hd = d // NA_HEADS
    w = 2 * hd
    pairs = d // w
    rows_grid = (t - n_ctx) // GRID_W
    assert w == LANES and rows_grid >= NA_KR and rows_grid % NA_SLOTS == 0
    n_keys = NA_KR * GRID_W + n_ctx
    bias = _na_bias_table(rpb)
    kern = functools.partial(_na_kernel, n_ctx=n_ctx, scale=hd ** -0.5)

    def col(group):
        return pl.BlockSpec((None, t, w), lambda hp, b: (b, 0, group * pairs + hp))

    gain = lambda g: jnp.tile(g.astype(F32), 2).reshape(1, w)
    return pl.pallas_call(
        kern, grid=(pairs, bsz),
        in_specs=[col(0), col(1), col(2),
                  pl.BlockSpec((1, w), lambda hp, b: (0, 0)),
                  pl.BlockSpec((1, w), lambda hp, b: (0, 0)),
                  pl.BlockSpec((None,) + bias.shape[1:], lambda hp, b: (hp, 0, 0, 0))],
        out_specs=pl.BlockSpec((None, t, w), lambda hp, b: (b, 0, hp)),
        out_shape=jax.ShapeDtypeStruct((bsz, t, d), BF16),
        scratch_shapes=[pltpu.VMEM((t, w), BF16)] * 2
        + [pltpu.VMEM((NA_SLOTS, 2 * GRID_W, n_keys), F32), pltpu.VMEM((NA_SLOTS, 2 * GRID_W, n_keys), BF16),
           pltpu.VMEM((NA_SLOTS, 2 * GRID_W, 1), F32)],
        compiler_params=_cparams("parallel", "parallel"), name="na_attention",
    )(qkv3, qkv3, qkv3, gain(q_g), gain(k_g), bias)


FFN_ROWS = 384


def _swiglu_kernel(h_ref, g_ref, shc_ref, sh_ref, scc_ref, sc_ref, w1_ref, w3_ref, w2_ref, gtc_ref, gt_ref,
                   o_ref, *, n_ctx, tiles_per_batch):
    tile_in_batch = pl.program_id(0) % tiles_per_batch
    x = _modulated_rows(h_ref, g_ref, shc_ref, sh_ref, scc_ref, sc_ref, tile_in_batch, n_ctx).astype(BF16)
    hid = _silu(_dot(x, w1_ref[...])) * _dot(x, w3_ref[...])
    y = _dot(hid.astype(BF16), w2_ref[...])
    o_ref[...] = h_ref[...] + _row_gate(gtc_ref, gt_ref, y.shape, tile_in_batch, n_ctx) * y


def _resident(block_shape, index_map):
    return pl.BlockSpec(block_shape, index_map, pipeline_mode=pl.Buffered(1))


def _swiglu_residual(h, g, w1, w3, w2, j, mods, n_batch, t, n_ctx):
    n, d = h.shape
    dff = w1.shape[2]
    tm = FFN_ROWS
    tpb = t // tm
    assert t % tm == 0 and n_ctx <= tm
    kern = functools.partial(_swiglu_kernel, n_ctx=n_ctx, tiles_per_batch=tpb)
    return pl.pallas_call(
        kern, grid=(n // tm,),
        in_specs=[pl.BlockSpec((tm, d), lambda i: (i, 0)),
                  pl.BlockSpec((1, d), lambda i: (0, 0))]
        + _stream_specs(d, n_batch, tpb, 3, 1) + _stream_specs(d, n_batch, tpb, 4, 1)
        + [_resident((None, d, dff), lambda i: (j, 0, 0)),
           _resident((None, d, dff), lambda i: (j, 0, 0)),
           _resident((None, dff, d), lambda i: (j, 0, 0))] + _stream_specs(d, n_batch, tpb, 5, 1),
        out_specs=pl.BlockSpec((tm, d), lambda i: (i, 0)),
        out_shape=jax.ShapeDtypeStruct((n, d), F32),
        compiler_params=_cparams("parallel"), name="swiglu",
    )(h, g, mods, mods, mods, mods, w1, w3, w2, mods, mods)


MOE_ROWS = 512


def _route_tables(route_t, counts, tm, n_exp):
    r = route_t.shape[1]
    experts = route_t[0:2].astype(jnp.int32)
    ranks = route_t[4:6].astype(jnp.int32)
    padded = ((counts + tm - 1) // tm) * tm
    ends = jnp.cumsum(padded)
    starts = ends - padded
    slot = ranks
    for e in range(n_exp):
        slot = slot + jnp.where(experts == e, starts[e], 0)
    p_slots = 2 * r + n_exp * tm
    n_tiles = p_slots // tm
    n_used = ends[-1] // tm
    tile = jnp.arange(n_tiles, dtype=jnp.int32)
    te = jnp.sum(((tile * tm)[:, None] >= ends[None, :]).astype(jnp.int32), axis=1)
    te = jnp.where(tile < n_used, te, te[n_used - 1])
    return slot, p_slots, te, n_used.reshape(1)


def _gather_rows(table, idx):
    p = idx.shape[0]
    m = table.shape[1]
    mesh = plsc.VectorSubcoreMesh(core_axis_name="c", subcore_axis_name="s")

    @functools.partial(pl.kernel, out_type=jax.ShapeDtypeStruct((p, m), table.dtype), mesh=mesh)
    def gather(table_hbm, idx_hbm, out_hbm):
        def body(idx_vmem, out_vmem):
            pltpu.sync_copy(table_hbm.at[idx_vmem.at[0]], out_vmem)

        pltpu.emit_pipeline(
            body, grid=(p // SC_WINDOW,),
            in_specs=[pl.BlockSpec((1, SC_WINDOW), lambda i: (0, i))],
            out_specs=[pl.BlockSpec((SC_WINDOW, m), lambda i: (i, 0))],
            core_axis_name=("c", "s"), dimension_semantics=(pltpu.PARALLEL,),
        )(idx_hbm, out_hbm)

    return gather(table, idx.reshape(1, p))


def _scatter_rows(rows, idx_a, idx_b, n_out):
    q, m = rows.shape
    mesh = plsc.VectorSubcoreMesh(core_axis_name="c", subcore_axis_name="s")

    @functools.partial(pl.kernel, out_type=jax.ShapeDtypeStruct((n_out, m), rows.dtype), mesh=mesh)
    def scatter(rows_hbm, ia_hbm, ib_hbm, out_hbm):
        def body(rows_vmem, ia_vmem, ib_vmem):
            pltpu.sync_copy(rows_vmem, out_hbm.at[ia_vmem.at[0]])
            pltpu.sync_copy(rows_vmem, out_hbm.at[ib_vmem.at[0]])

        pltpu.emit_pipeline(
            body, grid=(q // SC_WINDOW,),
            in_specs=[pl.BlockSpec((SC_WINDOW, m), lambda i: (i, 0)),
                      pl.BlockSpec((1, SC_WINDOW), lambda i: (0, i)),
                      pl.BlockSpec((1, SC_WINDOW), lambda i: (0, i))],
            out_specs=[],
            core_axis_name=("c", "s"), dimension_semantics=(pltpu.PARALLEL,),
        )(rows_hbm, ia_hbm, ib_hbm)

    return scatter(rows, idx_a.reshape(1, q), idx_b.reshape(1, q))


def _scatter_pieces(pieces, slot, n_slots):
    npc, r, m = pieces.shape
    offs = jnp.arange(npc, dtype=jnp.int32)[:, None] * n_slots
    idx_a = (slot[0][None, :] + offs).reshape(-1)
    idx_b = (slot[1][None, :] + offs).reshape(-1)
    return _scatter_rows(pieces.reshape(npc * r, m), idx_a, idx_b, npc * n_slots).reshape(npc, n_slots, m)


def _gather_pieces(pieces, idx):
    npc, v, m = pieces.shape
    offs = (jnp.arange(npc, dtype=jnp.int32) * v).reshape((npc,) + (1,) * idx.ndim)
    flat = (idx[None] + offs).reshape(-1)
    return _gather_rows(pieces.reshape(npc * v, m), flat).reshape((npc,) + idx.shape + (m,))


def _moe_ffn_kernel(te_ref, nu_ref, x_ref, w1_ref, w3_ref, w2_ref, y_ref):
    @pl.when(pl.program_id(0) < nu_ref[0])
    def _():
        x = _load_unpacked(x_ref).astype(BF16)
        hid = _silu(_dot(x, w1_ref[...])) * _dot(x, w3_ref[...])
        _store_pieces(y_ref, _pack_halves(_dot(hid.astype(BF16), w2_ref[...])))


def _moe_ffn(xs, te, n_used, w1, w3, w2, j):
    npc, p, m = xs.shape
    d = 2 * npc * m
    dff = w1.shape[3]
    tm = MOE_ROWS

    def row(i, te_ref, nu_ref):
        return (0, jnp.minimum(i, nu_ref[0] - 1), 0)

    def expert(i, te_ref, nu_ref):
        return (j, te_ref[i], 0, 0)

    grid_spec = pltpu.PrefetchScalarGridSpec(
        num_scalar_prefetch=2, grid=(p // tm,),
        in_specs=[pl.BlockSpec((npc, tm, m), row),
                  _resident((None, None, d, dff), expert),
                  _resident((None, None, d, dff), expert),
                  _resident((None, None, dff, d), expert)],
        out_specs=pl.BlockSpec((npc, tm, m), row))
    return pl.pallas_call(
        _moe_ffn_kernel, grid_spec=grid_spec,
        out_shape=jax.ShapeDtypeStruct((npc, p, m), jnp.int32),
        compiler_params=_cparams("arbitrary"), name="moe_ffn",
    )(te, n_used, xs, w1, w3, w2)


def _moe_combine_kernel(ya_ref, yb_ref, route_ref, h_ref, gt_ref, o_ref):
    r = route_ref[...]
    y = r[:, 2:3] * _load_unpacked(ya_ref) + r[:, 3:4] * _load_unpacked(yb_ref)
    o_ref[...] = h_ref[...] + gt_ref[...] * y


def _moe_combine(yg, route, h, mods, n_batch, tiles_per_batch, ctx_tiles):
    d = h.shape[1]
    npc, _, r, m = yg.shape
    tm = TOKEN_TILE
    per = tiles_per_batch - ctx_tiles
    h_idx = lambda i: ((i // per) * tiles_per_batch + ctx_tiles + i % per, 0)
    if ctx_tiles == 0:
        mod_idx = _mod_row_spec(5, tiles_per_batch, n_batch)
    else:
        mod_idx = lambda i: (i // per, 5, 0, 0)
    return pl.pallas_call(
        _moe_combine_kernel, grid=(r // tm,),
        in_specs=[pl.BlockSpec((npc, None, tm, m), lambda i: (0, 0, i, 0)),
                  pl.BlockSpec((npc, None, tm, m), lambda i: (0, 1, i, 0)),
                  pl.BlockSpec((tm, LANES), lambda i: (i, 0)),
                  pl.BlockSpec((tm, d), h_idx),
                  pl.BlockSpec((None, None, 1, d), mod_idx)],
        out_specs=pl.BlockSpec((tm, d), lambda i: (i, 0)),
        out_shape=jax.ShapeDtypeStruct((r, d), F32),
        compiler_params=_cparams("parallel"), name="moe_combine",
    )(yg, yg, route, h, mods)


def _moe_residual(h, g, w_router, w1, w3, w2, j, mods, n_batch, t, n_ctx, latent_only):
    tpb = t // TOKEN_TILE
    ctx_tiles = n_ctx // TOKEN_TILE if latent_only else 0
    u_pieces, route, route_t, counts = _modulate_router(h, g, mods, 3, n_batch, tpb, ctx_tiles, w_router)
    if latent_only:
        npc, _, m = u_pieces.shape
        u_pieces = u_pieces.reshape(npc, n_batch, t, m)[:, :, n_ctx:].reshape(npc, -1, m)
        route = route.reshape(n_batch, t, LANES)[:, n_ctx:].reshape(-1, LANES)
        route_t = route_t.reshape(-1, n_batch, t)[:, :, n_ctx:].reshape(route_t.shape[0], -1)
    n_exp = w1.shape[1]
    slot, n_slots, te, n_used = _route_tables(route_t, counts[0, :n_exp].astype(jnp.int32), MOE_ROWS, n_exp)
    xs = _scatter_pieces(u_pieces, slot, n_slots)
    ys = _moe_ffn(xs, te, n_used, w1, w3, w2, j)
    yg = _gather_pieces(ys, slot)
    return _moe_combine(yg, route, h, mods, n_batch, tpb, ctx_tiles)


def kernel(x, c, ctx, c_ctx, w_mod, b_mod, norm_mix_g, norm_ffn_g, ev_w_in, ev_w_out, hg_lb_logits, hg_onorm_g, rg_conv_w, rg_conv_b, rg_wa, rg_ba, rg_wx, rg_bx, rg_lambda, na_w_qkv, na_w_o, na_q_g, na_k_g, na_rpb, ffn_w1, ffn_w3, ffn_w2, moe_router, moe_w1, moe_w3, moe_w2):
    bsz, seq, d = x.shape
    n_ctx = ctx.shape[1]
    t = n_ctx + seq
    depth = w_mod.shape[0]
    assert n_ctx % TOKEN_TILE == 0 and seq % TOKEN_TILE == 0 and t % ROW_TILE == 0 and n_ctx <= ROW_TILE

    rows = -(-(bsz + 1) // 8) * 8
    cond = jnp.zeros((rows, d), F32).at[:bsz].set(c).at[bsz].set(c_ctx)
    mods_all = _adaln(cond, w_mod, b_mod).reshape(depth, rows, N_MOD, 1, d)

    lb_p = jax.nn.softmax(hg_lb_logits.astype(F32), axis=0)
    lb_all = jnp.cumsum(lb_p, axis=0) - lb_p[0]

    bf = lambda w: w.astype(BF16)
    w_in, w_out, w_qkv, w_o = bf(ev_w_in), bf(ev_w_out), bf(na_w_qkv), bf(na_w_o)
    f_w1, f_w3, f_w2 = bf(ffn_w1), bf(ffn_w3), bf(ffn_w2)
    m_w1, m_w3, m_w2 = bf(moe_w1), bf(moe_w3), bf(moe_w2)

    h = jnp.concatenate([ctx, x], axis=1).reshape(bsz * t, d)
    for layer in range(depth):
        j = layer // 2
        mods = mods_all[layer]
        g_mix = norm_mix_g[layer].reshape(1, d)
        if layer % 2 == 0:
            p = _modulated_matmul(h, g_mix, mods, w_in, j, F32, ROW_TILE // 2, bsz, t, n_ctx)
            p3 = p.reshape(bsz, t, -1)
            hg = _hgrn2(p3, lb_all[j], n_ctx)
            half = hg.shape[-1]
            rg = _rglru(p3, 5 * half // (d - half), rg_conv_w[j], rg_conv_b[j], rg_wa[j], rg_ba[j],
                        rg_wx[j], rg_bx[j], rg_lambda[j], n_ctx)
            h = _merge_out(hg.reshape(bsz * t, half), rg.reshape(bsz * t, d - half), p, hg_onorm_g[j],
                           w_out, j, h, mods, bsz, t, n_ctx)
        else:
            qkv = _modulated_matmul(h, g_mix, mods, w_qkv, j, BF16, ROW_TILE, bsz, t, n_ctx)
            o = _na_attention(qkv.reshape(bsz, t, 3 * d), na_q_g[j], na_k_g[j], na_rpb[j], n_ctx)
            h = _matmul_residual(o.reshape(bsz * t, d), w_o, j, h, mods, 2, bsz, t, n_ctx)
        if layer % 2 == 0:
            h = _swiglu_residual(h, norm_ffn_g[layer].reshape(1, d), f_w1, f_w3, f_w2, j, mods, bsz, t, n_ctx)
        else:
            h = _moe_residual(h, norm_ffn_g[layer].reshape(1, d), moe_router[j], m_w1, m_w3, m_w2, j, mods,
                              bsz, t, n_ctx, layer == depth - 1)
    if depth % 2 == 0:
        return h.reshape(bsz, seq, d)
    return h.reshape(bsz, t, d)[:, n_ctx:]
```

```python
import functools

import numpy as np
import jax
import jax.numpy as jnp
from jax import lax
from jax.experimental import pallas as pl
from jax.experimental.pallas import tpu as pltpu
from jax.experimental.pallas import tpu_sc as plsc

F32 = jnp.float32
BF16 = jnp.bfloat16

EPS = 1e-6
NEG_BIG = -1e9
LB_MIN = 1e-30
LOG2_E = 1.4426950408889634
N_MOD = 6
GRID_W = 64
HG_HEAD_DIM = 128
RG_BLOCKS = 8
RG_CONV = 4
RG_C = 8.0
NA_HEADS = 16
NA_KR = 8
NA_KC = 16
N_EXPERTS = 8

LANES = 128
SC_WINDOW = 128
SC_PIECES = 2
TOKEN_TILE = 256
HG_CHUNK = 128
HG_LEVELS = 7
HG_SLOTS = 2
NA_SLOTS = 4
RG_SCAN_BLOCK = 256
RG_GROUP = 8
VMEM_LIMIT = 56 * 1024 * 1024


def _cparams(*sem):
    return pltpu.CompilerParams(dimension_semantics=sem, vmem_limit_bytes=VMEM_LIMIT)


def _dot(a, b):
    return jnp.dot(a, b, preferred_element_type=F32)


def _dot_nt(a, b):
    return lax.dot_general(a, b, (((1,), (1,)), ((), ())), preferred_element_type=F32)


def _dot_tn(a, b):
    return lax.dot_general(a, b, (((0,), (0,)), ((), ())), preferred_element_type=F32)


def _split2(x):
    hi = x.astype(BF16)
    lo = (x - hi.astype(F32)).astype(BF16)
    return hi, lo


def _sigmoid(x):
    return 0.5 * jnp.tanh(0.5 * x) + 0.5


def _silu(x):
    return x * _sigmoid(x)


def _softplus_neg_abs(x):
    return jnp.log(1.0 + jnp.exp(-jnp.abs(x)))


def _log_sigmoid(x):
    return jnp.minimum(x, 0.0) - _softplus_neg_abs(x)


def _gelu_tanh(x):
    return 0.5 * x * (1.0 + jnp.tanh(np.sqrt(2.0 / np.pi) * (x + 0.044715 * (x * x * x))))


def _adaln_kernel(c_ref, w_ref, b_ref, o_ref):
    a_hi, a_lo = _split2(_silu(c_ref[...]))
    w_hi, w_lo = _split2(w_ref[...])
    o_ref[...] = _dot(a_hi, w_hi) + _dot(a_hi, w_lo) + _dot(a_lo, w_hi) + b_ref[...]


def _adaln(cond, w_mod, b_mod):
    depth, d, n = w_mod.shape
    r = cond.shape[0]
    tn = 1536
    return pl.pallas_call(
        _adaln_kernel,
        grid=(depth, n // tn),
        in_specs=[pl.BlockSpec((r, d), lambda l, j: (0, 0)),
                  pl.BlockSpec((None, d, tn), lambda l, j: (l, 0, j)),
                  pl.BlockSpec((None, 1, tn), lambda l, j: (l, 0, j))],
        out_specs=pl.BlockSpec((None, r, tn), lambda l, j: (l, 0, j)),
        out_shape=jax.ShapeDtypeStruct((depth, r, n), F32),
        compiler_params=_cparams("parallel", "parallel"),
        name="adaln",
    )(cond, w_mod, b_mod.reshape(depth, 1, n))


def _mod_row_spec(k, tiles_per_batch, n_batch):
    def idx(i):
        row = jnp.where(i % tiles_per_batch == 0, n_batch, i // tiles_per_batch)
        return (row, k, 0, 0)
    return idx


def _modulated(h_ref, g_ref, sh_ref, sc_ref):
    x = h_ref[...]
    y = x * lax.rsqrt(jnp.mean(x * x, axis=-1, keepdims=True) + EPS)
    return (y * g_ref[...]) * (1.0 + sc_ref[...]) + sh_ref[...]


def _pack_halves(x):
    m = x.shape[1] // 2
    lo = lax.bitcast_convert_type(x[:, :m].astype(BF16).astype(F32), jnp.int32)
    hi = lax.bitcast_convert_type(x[:, m:].astype(BF16).astype(F32), jnp.int32)
    return lax.shift_right_logical(lo, 16) | (hi & jnp.int32(-65536))


def _unpack_halves(w):
    lo = lax.bitcast_convert_type(lax.shift_left(w, 16), F32)
    hi = lax.bitcast_convert_type(w & jnp.int32(-65536), F32)
    return lo, hi


def _store_pieces(ref, w):
    m = w.shape[1] // SC_PIECES
    for j in range(SC_PIECES):
        ref[j] = w[:, j * m:(j + 1) * m]


def _load_unpacked(ref):
    parts = [_unpack_halves(ref[j]) for j in range(SC_PIECES)]
    return jnp.concatenate([lo for lo, _ in parts] + [hi for _, hi in parts], axis=1)


def _modulate_router_kernel(h_ref, g_ref, sh_ref, sc_ref, wr_ref, u_ref, route_ref, route_t_ref, cnt_ref,
                            *, tiles_per_batch, skip_tiles):
    i = pl.program_id(0)

    @pl.when(i == 0)
    def _():
        cnt_ref[...] = jnp.zeros_like(cnt_ref)

    u = _modulated(h_ref, g_ref, sh_ref, sc_ref)
    _store_pieces(u_ref, _pack_halves(u))
    u0, u1 = _split2(u)
    w0, w1 = _split2(wr_ref[...])
    logits = _dot(u0, w0) + _dot(u0, w1) + _dot(u1, w0)
    lane = lax.broadcasted_iota(jnp.int32, logits.shape, 1).astype(F32)
    lg = jnp.where(lane < N_EXPERTS, logits, -jnp.inf)
    m1 = jnp.max(lg, axis=-1, keepdims=True)
    i1 = jnp.min(jnp.where(lg == m1, lane, float(LANES)), axis=-1, keepdims=True)
    lg2 = jnp.where(lane == i1, -jnp.inf, lg)
    m2 = jnp.max(lg2, axis=-1, keepdims=True)
    i2 = jnp.min(jnp.where(lg2 == m2, lane, float(LANES)), axis=-1, keepdims=True)
    e2 = jnp.exp(m2 - m1)
    den = 1.0 + e2

    routed = jnp.where(i % tiles_per_batch >= skip_tiles, 1.0, 0.0)
    pick_a = jnp.where(lane == i1, routed, 0.0)
    pick_b = jnp.where(lane == i2, routed, 0.0)
    tm = pick_a.shape[0]
    earlier = jnp.where(lax.broadcasted_iota(jnp.int32, (tm, tm), 1) < lax.broadcasted_iota(jnp.int32, (tm, tm), 0),
                        1.0, 0.0).astype(BF16)
    total_a = jnp.sum(pick_a, axis=0, keepdims=True)
    base = cnt_ref[0:1, :]
    rank_a = jnp.sum(pick_a * (base + _dot(earlier, pick_a.astype(BF16))), axis=-1, keepdims=True)
    rank_b = jnp.sum(pick_b * (base + total_a + _dot(earlier, pick_b.astype(BF16))), axis=-1, keepdims=True)
    cnt_ref[0:1, :] = base + total_a + jnp.sum(pick_b, axis=0, keepdims=True)

    route = jnp.where(lane == 0, i1, jnp.where(lane == 1, i2, jnp.where(lane == 2, 1.0 / den, jnp.where(
        lane == 3, e2 / den, jnp.where(lane == 4, rank_a, rank_b)))))
    route_ref[...] = route
    route_t_ref[...] = route.T[0:route_t_ref.shape[0], :]


def _modulate_router(h, g, mods, k_shift, n_batch, tiles_per_batch, skip_tiles, w_router):
    n, d = h.shape
    tm = TOKEN_TILE
    row = pl.BlockSpec((tm, d), lambda i: (i, 0))
    vec = pl.BlockSpec((1, d), lambda i: (0, 0))

    def mod_spec(k):
        return pl.BlockSpec((None, None, 1, d), _mod_row_spec(k, tiles_per_batch, n_batch))

    in_specs = [row, vec, mod_spec(k_shift), mod_spec(k_shift + 1)]
    wr = jnp.zeros((d, LANES), F32).at[:, :N_EXPERTS].set(w_router)
    kern = functools.partial(_modulate_router_kernel, tiles_per_batch=tiles_per_batch, skip_tiles=skip_tiles)
    return pl.pallas_call(
        kern, grid=(n // tm,),
        in_specs=in_specs + [pl.BlockSpec((d, LANES), lambda i: (0, 0))],
        out_specs=[pl.BlockSpec((SC_PIECES, tm, d // 2 // SC_PIECES), lambda i: (0, i, 0)),
                   pl.BlockSpec((tm, LANES), lambda i: (i, 0)),
                   pl.BlockSpec((8, tm), lambda i: (0, i)),
                   pl.BlockSpec((8, LANES), lambda i: (0, 0))],
        out_shape=[jax.ShapeDtypeStruct((SC_PIECES, n, d // 2 // SC_PIECES), jnp.int32),
                   jax.ShapeDtypeStruct((n, LANES), F32),
                   jax.ShapeDtypeStruct((8, n), F32),
                   jax.ShapeDtypeStruct((8, LANES), F32)],
        compiler_params=_cparams("arbitrary"), name="modulate_router",
    )(h, g, mods, mods, wr)


ROW_TILE = 768


def _row_gate(ctx_ref, own_ref, shape, tile_in_batch, n_ctx):
    rows = lax.broadcasted_iota(jnp.int32, shape, 0)
    is_ctx = (rows < n_ctx) & (tile_in_batch == 0)
    return jnp.where(is_ctx, ctx_ref[...], own_ref[...])


def _modulated_rows(h_ref, g_ref, shc_ref, sh_ref, scc_ref, sc_ref, tile_in_batch, n_ctx):
    x = h_ref[...]
    y = x * lax.rsqrt(jnp.mean(x * x, axis=-1, keepdims=True) + EPS)
    shift = _row_gate(shc_ref, sh_ref, x.shape, tile_in_batch, n_ctx)
    scale = _row_gate(scc_ref, sc_ref, x.shape, tile_in_batch, n_ctx)
    return (y * g_ref[...]) * (1.0 + scale) + shift


def _stream_specs(d, n_batch, tiles_per_batch, k, nargs):
    if nargs == 1:
        ctx = lambda i: (n_batch, k, 0, 0)
        own = lambda i: (i // tiles_per_batch, k, 0, 0)
    else:
        ctx = lambda i, f: (n_batch, k, 0, 0)
        own = lambda i, f: (i // tiles_per_batch, k, 0, 0)
    return [pl.BlockSpec((None, None, 1, d), ctx), pl.BlockSpec((None, None, 1, d), own)]


def _mod_mm_kernel(h_ref, g_ref, shc_ref, sh_ref, scc_ref, sc_ref, w_ref, o_ref, *, n_ctx, tiles_per_batch):
    x = _modulated_rows(h_ref, g_ref, shc_ref, sh_ref, scc_ref, sc_ref,
                        pl.program_id(0) % tiles_per_batch, n_ctx).astype(BF16)
    o_ref[...] = _dot(x, w_ref[...]).astype(o_ref.dtype)


def _modulated_matmul(h, g, mods, w, j, out_dtype, tm, n_batch, t, n_ctx):
    n, d = h.shape
    nout = w.shape[2]
    tpb = t // tm
    assert t % tm == 0 and n_ctx <= tm
    kern = functools.partial(_mod_mm_kernel, n_ctx=n_ctx, tiles_per_batch=tpb)
    return pl.pallas_call(
        kern, grid=(n // tm,),
        in_specs=[pl.BlockSpec((tm, d), lambda i: (i, 0)),
                  pl.BlockSpec((1, d), lambda i: (0, 0))]
        + _stream_specs(d, n_batch, tpb, 0, 1) + _stream_specs(d, n_batch, tpb, 1, 1)
        + [pl.BlockSpec((None, d, nout), lambda i: (j, 0, 0))],
        out_specs=pl.BlockSpec((tm, nout), lambda i: (i, 0)),
        out_shape=jax.ShapeDtypeStruct((n, nout), out_dtype),
        compiler_params=_cparams("parallel"), name="modulated_matmul",
    )(h, g, mods, mods, mods, mods, w)


def _mm_res_kernel(x_ref, w_ref, h_ref, gtc_ref, gt_ref, o_ref, *, n_ctx, tiles_per_batch):
    gate = _row_gate(gtc_ref, gt_ref, h_ref.shape, pl.program_id(0) % tiles_per_batch, n_ctx)
    o_ref[...] = h_ref[...] + gate * _dot(x_ref[...], w_ref[...])


def _matmul_residual(x, w, j, h, mods, k_gate, n_batch, t, n_ctx):
    n, d = h.shape
    kdim = x.shape[1]
    tm = ROW_TILE
    tpb = t // tm
    kern = functools.partial(_mm_res_kernel, n_ctx=n_ctx, tiles_per_batch=tpb)
    return pl.pallas_call(
        kern, grid=(n // tm,),
        in_specs=[pl.BlockSpec((tm, kdim), lambda i: (i, 0)),
                  pl.BlockSpec((None, kdim, d), lambda i: (j, 0, 0)),
                  pl.BlockSpec((tm, d), lambda i: (i, 0))] + _stream_specs(d, n_batch, tpb, k_gate, 1),
        out_specs=pl.BlockSpec((tm, d), lambda i: (i, 0)),
        out_shape=jax.ShapeDtypeStruct((n, d), F32),
        compiler_params=_cparams("parallel"), name="matmul_residual",
    )(x, w, h, mods, mods)


def _hg_tables():
    c = HG_CHUNK
    t = np.arange(c)[:, None]
    u = np.arange(c)[None, :]
    mats = [(u <= t), (u > t)]
    for k in range(1, HG_LEVELS + 1):
        m = ((t >> k) << k) + (1 << (k - 1)) - 1
        upper = t > m
        mats.append(np.where(upper, (u > m) & (u <= t), (u > t) & (u <= m)))
    fwd = np.concatenate(mats, axis=0).astype(np.float32)
    bwd = fwd.reshape(-1, c, c)[:, ::-1, ::-1].reshape(-1, c)
    x = t ^ u
    lvl = np.where(t > u, np.floor(np.log2(np.maximum(x, 1))).astype(np.int32) + 1, -1)
    lvl = np.where(t == u, 0, lvl).astype(np.int32)
    return np.stack([fwd, bwd]), np.stack([lvl, lvl.T])


def _hg_kernel(q_ref, ff_ref, fb_ref, v_ref, lb_ref, m_ref, lvl_ref, o_ref, of_ref, ob_ref,
               qk_ref, el_ref, io_ref, tot_ref, att_ref, *, n_ctx_chunks, n_chunks):
    c = HG_CHUNK
    gap = HG_SLOTS // 2
    t = o_ref.shape[0]
    z_refs = (ff_ref, fb_ref)
    out_refs = (of_ref, ob_ref)
    for ref in (qk_ref, el_ref, io_ref, tot_ref, att_ref):
        ref[...] = jnp.zeros(ref.shape, ref.dtype)

    def start_row(d, j):
        j = jnp.clip(j, 0, n_chunks - 1)
        if d == 1:
            j = jnp.where(j < n_ctx_chunks, n_ctx_chunks - 1 - j, n_chunks - 1 - (j - n_ctx_chunks))
        return pl.multiple_of(j * c, c)

    def stage1(d, j, slot):
        t0 = start_row(d, j)
        lb = lb_ref[d:d + 1, :]
        log_lb = jnp.log(jnp.maximum(lb, LB_MIN))
        q = _silu(q_ref[pl.ds(t0, c), :])
        z = z_refs[d][pl.ds(t0, c), :]
        k = (1.0 - lb) * _sigmoid(-z)
        b = jnp.log1p(-lb) + _log_sigmoid(z)
        logf = jnp.maximum(log_lb, b) + _softplus_neg_abs(log_lb - b)
        g_hi, g_lo = _split2(logf * LOG2_E)
        m_ends = m_ref[d, 0:2 * c, :]
        e_ends = jnp.exp2(_dot(m_ends, g_hi) + _dot(m_ends, g_lo))
        el_ref[d, slot] = jnp.exp2(_dot(m_ref[d, 2 * c:, :], g_hi).astype(BF16))
        qk_ref[d, slot, 0] = q.astype(BF16)
        qk_ref[d, slot, 1] = k.astype(BF16)
        io_ref[d, slot, 0] = (q * e_ends[0:c]).astype(BF16)
        io_ref[d, slot, 1] = (k * e_ends[c:]).astype(BF16)
        tot_row = c - 1 if d == 0 else 0
        tot_ref[d, slot, 0:1, :] = e_ends[tot_row:tot_row + 1, :]

    def stage2(d, slot):
        qb = qk_ref[d, slot, 0]
        kb = qk_ref[d, slot, 1]
        lvl = lvl_ref[d]
        att = jnp.where(lvl == 0, _dot_nt(qb, kb), 0.0)
        for lev in range(1, HG_LEVELS + 1):
            el = el_ref[d, slot, (lev - 1) * c:lev * c, :]
            att = jnp.where(lvl == lev, _dot_nt(qb * el, kb * el), att)
        att_ref[d, slot] = att.astype(BF16)

    def stage3(d, j, slot, st):
        t0 = start_row(d, j)
        v = v_ref[pl.ds(t0, c), :].astype(BF16)
        row = jnp.where(j < n_chunks, t0, t)
        out_refs[d][pl.ds(pl.multiple_of(row, c), c), :] = (
            _dot(att_ref[d, slot], v) + _dot_nt(io_ref[d, slot, 0], st.astype(BF16)))
        return st * tot_ref[d, slot, 0:1, :] + _dot_tn(v, io_ref[d, slot, 1])

    def steps(jj, carry):
        st = list(carry)
        for slot in range(HG_SLOTS):
            i = HG_SLOTS * jj + slot
            for d in range(2):
                st[d] = stage3(d, i - 2 * gap, slot, st[d])
            for d in range(2):
                stage2(d, (slot + gap) % HG_SLOTS)
            for d in range(2):
                stage1(d, i, slot)
        return tuple(st)

    zero = jnp.zeros((c, c), F32)
    n_steps = -(-(n_chunks + 2 * gap) // HG_SLOTS)
    lax.fori_loop(0, n_steps, steps, (zero, zero))
    o_ref[...] = (of_ref[0:t, :] + ob_ref[0:t, :]).astype(o_ref.dtype)


def _hgrn2(p3, lb, n_ctx):
    bsz, t, _ = p3.shape
    width = lb.shape[1]
    heads = width // HG_HEAD_DIM
    mats, lvl = _hg_tables()
    c = HG_CHUNK

    def col(group):
        return pl.BlockSpec((None, t, c), lambda b, h: (b, 0, group * heads + h))

    kern = functools.partial(_hg_kernel, n_ctx_chunks=n_ctx // c, n_chunks=t // c)
    return pl.pallas_call(
        kern, grid=(bsz, heads),
        in_specs=[col(0), col(1), col(2), col(3),
                  pl.BlockSpec((2, c), lambda b, h: (0, h)),
                  pl.BlockSpec(mats.shape, lambda b, h: (0, 0, 0)),
                  pl.BlockSpec(lvl.shape, lambda b, h: (0, 0, 0))],
        out_specs=pl.BlockSpec((None, t, c), lambda b, h: (b, 0, h)),
        out_shape=jax.ShapeDtypeStruct((bsz, t, width), F32),
        scratch_shapes=[pltpu.VMEM((t + c, c), F32), pltpu.VMEM((t + c, c), F32),
                        pltpu.VMEM((2, HG_SLOTS, 2, c, c), BF16), pltpu.VMEM((2, HG_SLOTS, HG_LEVELS * c, c), BF16),
                        pltpu.VMEM((2, HG_SLOTS, 2, c, c), BF16), pltpu.VMEM((2, HG_SLOTS, 8, c), F32),
                        pltpu.VMEM((2, HG_SLOTS, c, c), BF16)],
        compiler_params=_cparams("parallel", "parallel"), name="hgrn2",
    )(p3, p3, p3, p3, lb, jnp.asarray(mats, BF16), jnp.asarray(lvl))


PAD = 8


def _rg_kernel(x_ref, cw_ref, cb_ref, wa_ref, ba_ref, wx_ref, bx_ref, lam_ref, o_ref,
               xpad_ref, xc_ref, ob_ref, *, n_ctx):
    t, w = x_ref.shape
    n_lat = t - n_ctx
    lat0 = n_ctx + 2 * PAD
    xpad_ref[...] = jnp.zeros(xpad_ref.shape, F32)
    xpad_ref[PAD:PAD + n_ctx, :] = x_ref[0:n_ctx, :]
    xpad_ref[lat0:lat0 + n_lat, :] = x_ref[n_ctx:t, :]
    left = RG_CONV // 2
    for base, dst, n in ((PAD, 0, n_ctx), (lat0, n_ctx, n_lat)):
        acc = cb_ref[...] + cw_ref[0:1, :] * xpad_ref[base - left:base - left + n, :]
        for j in range(1, RG_CONV):
            acc = acc + cw_ref[j:j + 1, :] * xpad_ref[base - left + j:base - left + j + n, :]
        xc_ref[dst:dst + n, :] = acc

    blk = RG_SCAN_BLOCK
    n_blocks = t // blk
    n_ctx_blocks = n_ctx // blk
    sub = lax.broadcasted_iota(jnp.int32, (blk // RG_GROUP, RG_GROUP, w), 1)
    c_log_sig_lam = RG_C * _log_sigmoid(lam_ref[...])

    def scan_block(d, t0, hin, out_ref):
        xc = xc_ref[pl.ds(t0, blk), :]
        xb = xc.astype(BF16)
        r = _sigmoid(_dot(xb, wa_ref[d]) + ba_ref[d:d + 1, :])
        ig = _sigmoid(_dot(xb, wx_ref[d]) + bx_ref[d:d + 1, :])
        log_a = r * c_log_sig_lam[d:d + 1, :]
        a = jnp.exp(log_a)
        b = jnp.sqrt(jnp.maximum(1.0 - a * a, 0.0)) * (ig * xc)
        n_groups = blk // RG_GROUP
        a = a.reshape(n_groups, RG_GROUP, w)
        b = b.reshape(n_groups, RG_GROUP, w)
        s = 1
        while s < RG_GROUP:
            shift = s if d == 0 else RG_GROUP - s
            keep = (sub >= s) if d == 0 else (sub < RG_GROUP - s)
            b = jnp.where(keep, a * pltpu.roll(b, shift, 1), 0.0) + b
            a = jnp.where(keep, a * pltpu.roll(a, shift, 1), a)
            s *= 2
        for gi in range(n_groups):
            gj = gi if d == 0 else n_groups - 1 - gi
            hg = a[gj] * hin + b[gj]
            out_ref[pl.ds(t0 + gj * RG_GROUP, RG_GROUP), :] = hg
            hin = hg[RG_GROUP - 1:RG_GROUP, :] if d == 0 else hg[0:1, :]
        return hin

    def step(i, carry):
        h_f, h_b = carry
        bb = jnp.where(i < n_ctx_blocks, n_ctx_blocks - 1 - i, n_blocks - 1 - (i - n_ctx_blocks))
        h_f = scan_block(0, pl.multiple_of(i * blk, blk), h_f, o_ref)
        h_b = scan_block(1, pl.multiple_of(bb * blk, blk), h_b, ob_ref)
        return h_f, h_b

    zero = jnp.zeros((1, w), F32)
    lax.fori_loop(0, n_blocks, step, (zero, zero))
    o_ref[...] += ob_ref[...]


def _blockdiag_dense(w):
    nd, nb, k, _ = w.shape
    eye = jnp.eye(nb, dtype=w.dtype)
    return jnp.einsum('dnij,nm->dnimj', w, eye).reshape(nd, nb * k, nb * k)


def _rglru(p3, col_block, conv_w, conv_b, wa, ba, wx, bx, lam, n_ctx):
    bsz, t, _ = p3.shape
    w = conv_w.shape[1]
    full = lambda shape: pl.BlockSpec(shape, lambda b: (0,) * len(shape))
    kern = functools.partial(_rg_kernel, n_ctx=n_ctx)
    return pl.pallas_call(
        kern, grid=(bsz,),
        in_specs=[pl.BlockSpec((None, t, w), lambda b: (b, 0, col_block)),
                  full((RG_CONV, w)), full((1, w)),
                  full((2, w, w)), full((2, w)), full((2, w, w)), full((2, w)), full((2, w))],
        out_specs=pl.BlockSpec((None, t, w), lambda b: (b, 0, 0)),
        out_shape=jax.ShapeDtypeStruct((bsz, t, w), F32),
        scratch_shapes=[pltpu.VMEM((t + 3 * PAD, w), F32)] + [pltpu.VMEM((t, w), F32)] * 2,
        compiler_params=_cparams("parallel"), name="rglru",
    )(p3, conv_w, conv_b.reshape(1, w), _blockdiag_dense(wa).astype(BF16), ba,
      _blockdiag_dense(wx).astype(BF16), bx, lam)


def _merge_kernel(hg_ref, rg_ref, gate_ref, y_ref, og_ref, w_ref, h_ref, gtc_ref, gt_ref, o_ref,
                  *, n_ctx, tiles_per_batch):
    hg = hg_ref[...]
    half = hg.shape[1]
    hn = hg * lax.rsqrt(jnp.mean(hg * hg, axis=-1, keepdims=True) + EPS) * og_ref[...]
    a = (hn * _silu(gate_ref[...])).astype(BF16)
    b = (rg_ref[...] * _gelu_tanh(y_ref[...])).astype(BF16)
    o = _dot(a, w_ref[0:half, :]) + _dot(b, w_ref[half:, :])
    gate = _row_gate(gtc_ref, gt_ref, h_ref.shape, pl.program_id(0) % tiles_per_batch, n_ctx)
    o_ref[...] = h_ref[...] + gate * o


def _merge_out(hg, rg, p, onorm_g, w_out, j, h, mods, n_batch, t, n_ctx):
    n, d = h.shape
    half = hg.shape[1]
    tm = ROW_TILE
    tpb = t // tm
    blk = lambda c: pl.BlockSpec((tm, half), lambda i: (i, c))
    kern = functools.partial(_merge_kernel, n_ctx=n_ctx, tiles_per_batch=tpb)
    return pl.pallas_call(
        kern, grid=(n // tm,),
        in_specs=[blk(0), blk(0), blk(4), blk(6),
                  pl.BlockSpec((1, half), lambda i: (0, 0)),
                  pl.BlockSpec((None, d, d), lambda i: (j, 0, 0)),
                  pl.BlockSpec((tm, d), lambda i: (i, 0))] + _stream_specs(d, n_batch, tpb, 2, 1),
        out_specs=pl.BlockSpec((tm, d), lambda i: (i, 0)),
        out_shape=jax.ShapeDtypeStruct((n, d), F32),
        compiler_params=_cparams("parallel"), name="merge_out",
    )(hg, rg, p, p, onorm_g.reshape(1, half), w_out, h, mods, mods)


def _na_bias_table(rpb):
    q = np.arange(GRID_W)[:, None]
    kc = np.arange(GRID_W)[None, :]
    wstart = np.clip(q - NA_KC // 2, 0, GRID_W - NA_KC)
    mask = (kc >= wstart) & (kc < wstart + NA_KC)
    coff = np.clip(kc - q + NA_KC - 1, 0, 2 * NA_KC - 2)
    sel = (coff[None] == np.arange(2 * NA_KC - 1)[:, None, None]).astype(np.float32)
    toe = jnp.einsum('hdj,jqk->hqdk', rpb.astype(F32), sel, precision=lax.Precision.HIGHEST)
    toe = jnp.where(mask[None, :, None, :], toe, NEG_BIG)
    h = toe.shape[0]
    flat = toe.reshape(h // 2, 2 * GRID_W, -1)
    width = (2 * NA_KR) * GRID_W
    even = jnp.pad(flat, ((0, 0), (0, 0), (0, width - flat.shape[2])))
    odd = jnp.pad(flat[:, :, GRID_W:], ((0, 0), (0, 0), (0, width - flat.shape[2] + GRID_W)))
    return jnp.stack([even, odd], axis=1)


def _stack_heads(x, hd):
    lane = lax.broadcasted_iota(jnp.int32, x.shape, 1)
    zero = jnp.zeros_like(x)
    return jnp.concatenate([jnp.where(lane < hd, x, zero), jnp.where(lane >= hd, x, zero)], axis=0)


def _unstack_heads(o2, hd):
    n = o2.shape[0] // 2
    lane = lax.broadcasted_iota(jnp.int32, (n, o2.shape[1]), 1)
    return jnp.where(lane < hd, o2[:n], o2[n:])


def _na_kernel(q_ref, k_ref, v_ref, qg_ref, kg_ref, bias_ref, o_ref, qn_ref, kn_ref,
               s_ref, p_ref, den_ref, *, n_ctx, scale):
    t, w = q_ref.shape
    hd = w // 2
    rows_grid = (t - n_ctx) // GRID_W
    r_i = lax.broadcasted_iota(jnp.int32, (w, w), 0)
    c_i = lax.broadcasted_iota(jnp.int32, (w, w), 1)
    seg = jnp.where((r_i // hd) == (c_i // hd), 1.0 / hd, 0.0).astype(BF16)

    def normed(x, g):
        ms = _dot((x * x).astype(BF16), seg)
        return (x * lax.rsqrt(ms + EPS) * g).astype(BF16)

    qn_ref[...] = normed(q_ref[...].astype(F32), qg_ref[...] * scale)
    kn_ref[...] = normed(k_ref[...].astype(F32), kg_ref[...])
    vb_ref = v_ref

    kc = kn_ref[0:n_ctx, :]
    vc = vb_ref[0:n_ctx, :]

    s = _dot_nt(_stack_heads(qn_ref[0:n_ctx, :], hd), kc)
    p = jnp.exp(s - jnp.max(s, axis=-1, keepdims=True))
    o2 = _dot(p.astype(BF16), vc) / jnp.sum(p, axis=-1, keepdims=True)
    o_ref[0:n_ctx, :] = _unstack_heads(o2, hd).astype(o_ref.dtype)

    kr = NA_KR
    n_win = kr * GRID_W

    def window(r):
        r = jnp.clip(r, 0, rows_grid - 1)
        rs = jnp.clip(r - kr // 2, 0, rows_grid - kr)
        q0 = pl.multiple_of(n_ctx + r * GRID_W, GRID_W)
        k0 = pl.multiple_of(n_ctx + rs * GRID_W, GRID_W)
        return q0, k0, rs - r + (kr - 1)

    gap = NA_SLOTS // 2
    for slot in range(NA_SLOTS):
        s_ref[slot] = jnp.zeros(s_ref.shape[1:], F32)
        p_ref[slot] = jnp.zeros(p_ref.shape[1:], BF16)
        den_ref[slot] = jnp.ones(den_ref.shape[1:], F32)

    def step(i, slot):
        q0, k0, _ = window(i - 2 * gap)
        pr = p_ref[slot]
        o2 = (_dot(pr[:, :n_win], vb_ref[pl.ds(k0, n_win), :]) + _dot(pr[:, n_win:], vc)) / den_ref[slot]
        o_ref[pl.ds(q0, GRID_W), :] = _unstack_heads(o2, hd).astype(o_ref.dtype)

        mid = (slot + gap) % NA_SLOTS
        sc = s_ref[mid]
        pe = jnp.exp(sc - jnp.max(sc, axis=-1, keepdims=True))
        den_ref[mid] = jnp.sum(pe, axis=-1, keepdims=True)
        p_ref[mid] = pe.astype(BF16)

        q0, k0, off = window(i)
        q2 = _stack_heads(qn_ref[pl.ds(q0, GRID_W), :], hd)
        bias = bias_ref[off % 2, :, pl.ds(pl.multiple_of((off // 2) * (2 * GRID_W), 2 * GRID_W), n_win)]
        s_ref[slot, :, :n_win] = _dot_nt(q2, kn_ref[pl.ds(k0, n_win), :]) + bias
        s_ref[slot, :, n_win:] = _dot_nt(q2, kc)

    def steps(j, carry):
        for slot in range(NA_SLOTS):
            step(NA_SLOTS * j + slot, slot)
        return carry

    lax.fori_loop(0, (rows_grid + 2 * gap) // NA_SLOTS, steps, 0)


def _na_attention(qkv3, q_g, k_g, rpb, n_ctx):
    bsz, t, d3 = qkv3.shape
    d = d3 // 3
    hd = d // NA_HEADS
    w = 2 * hd
    pairs = d // w
    rows_grid = (t - n_ctx) // GRID_W
    assert w == LANES and rows_grid >= NA_KR and rows_grid % NA_SLOTS == 0
    n_keys = NA_KR * GRID_W + n_ctx
    bias = _na_bias_table(rpb)
    kern = functools.partial(_na_kernel, n_ctx=n_ctx, scale=hd ** -0.5)

    def col(group):
        return pl.BlockSpec((None, t, w), lambda hp, b: (b, 0, group * pairs + hp))

    gain = lambda g: jnp.tile(g.astype(F32), 2).reshape(1, w)
    return pl.pallas_call(
        kern, grid=(pairs, bsz),
        in_specs=[col(0), col(1), col(2),
                  pl.BlockSpec((1, w), lambda hp, b: (0, 0)),
                  pl.BlockSpec((1, w), lambda hp, b: (0, 0)),
                  pl.BlockSpec((None,) + bias.shape[1:], lambda hp, b: (hp, 0, 0, 0))],
        out_specs=pl.BlockSpec((None, t, w), lambda hp, b: (b, 0, hp)),
        out_shape=jax.ShapeDtypeStruct((bsz, t, d), BF16),
        scratch_shapes=[pltpu.VMEM((t, w), BF16)] * 2
        + [pltpu.VMEM((NA_SLOTS, 2 * GRID_W, n_keys), F32), pltpu.VMEM((NA_SLOTS, 2 * GRID_W, n_keys), BF16),
           pltpu.VMEM((NA_SLOTS, 2 * GRID_W, 1), F32)],
        compiler_params=_cparams("parallel", "parallel"), name="na_attention",
    )(qkv3, qkv3, qkv3, gain(q_g), gain(k_g), bias)


FFN_ROWS = 384


def _swiglu_kernel(h_ref, g_ref, shc_ref, sh_ref, scc_ref, sc_ref, w1_ref, w3_ref, w2_ref, gtc_ref, gt_ref,
                   o_ref, *, n_ctx, tiles_per_batch):
    tile_in_batch = pl.program_id(0) % tiles_per_batch
    x = _modulated_rows(h_ref, g_ref, shc_ref, sh_ref, scc_ref, sc_ref, tile_in_batch, n_ctx).astype(BF16)
    hid = _silu(_dot(x, w1_ref[...])) * _dot(x, w3_ref[...])
    y = _dot(hid.astype(BF16), w2_ref[...])
    o_ref[...] = h_ref[...] + _row_gate(gtc_ref, gt_ref, y.shape, tile_in_batch, n_ctx) * y


def _resident(block_shape, index_map):
    return pl.BlockSpec(block_shape, index_map, pipeline_mode=pl.Buffered(1))


def _swiglu_residual(h, g, w1, w3, w2, j, mods, n_batch, t, n_ctx):
    n, d = h.shape
    dff = w1.shape[2]
    tm = FFN_ROWS
    tpb = t // tm
    assert t % tm == 0 and n_ctx <= tm
    kern = functools.partial(_swiglu_kernel, n_ctx=n_ctx, tiles_per_batch=tpb)
    return pl.pallas_call(
        kern, grid=(n // tm,),
        in_specs=[pl.BlockSpec((tm, d), lambda i: (i, 0)),
                  pl.BlockSpec((1, d), lambda i: (0, 0))]
        + _stream_specs(d, n_batch, tpb, 3, 1) + _stream_specs(d, n_batch, tpb, 4, 1)
        + [_resident((None, d, dff), lambda i: (j, 0, 0)),
           _resident((None, d, dff), lambda i: (j, 0, 0)),
           _resident((None, dff, d), lambda i: (j, 0, 0))] + _stream_specs(d, n_batch, tpb, 5, 1),
        out_specs=pl.BlockSpec((tm, d), lambda i: (i, 0)),
        out_shape=jax.ShapeDtypeStruct((n, d), F32),
        compiler_params=_cparams("parallel"), name="swiglu",
    )(h, g, mods, mods, mods, mods, w1, w3, w2, mods, mods)


MOE_ROWS = 512


def _route_tables(route_t, counts, tm, n_exp):
    r = route_t.shape[1]
    experts = route_t[0:2].astype(jnp.int32)
    ranks = route_t[4:6].astype(jnp.int32)
    padded = ((counts + tm - 1) // tm) * tm
    ends = jnp.cumsum(padded)
    starts = ends - padded
    slot = ranks
    for e in range(n_exp):
        slot = slot + jnp.where(experts == e, starts[e], 0)
    p_slots = 2 * r + n_exp * tm
    n_tiles = p_slots // tm
    n_used = ends[-1] // tm
    tile = jnp.arange(n_tiles, dtype=jnp.int32)
    te = jnp.sum(((tile * tm)[:, None] >= ends[None, :]).astype(jnp.int32), axis=1)
    te = jnp.where(tile < n_used, te, te[n_used - 1])
    return slot, p_slots, te, n_used.reshape(1)


def _gather_rows(table, idx):
    p = idx.shape[0]
    m = table.shape[1]
    mesh = plsc.VectorSubcoreMesh(core_axis_name="c", subcore_axis_name="s")

    @functools.partial(pl.kernel, out_type=jax.ShapeDtypeStruct((p, m), table.dtype), mesh=mesh)
    def gather(table_hbm, idx_hbm, out_hbm):
        def body(idx_vmem, out_vmem):
            pltpu.sync_copy(table_hbm.at[idx_vmem.at[0]], out_vmem)

        pltpu.emit_pipeline(
            body, grid=(p // SC_WINDOW,),
            in_specs=[pl.BlockSpec((1, SC_WINDOW), lambda i: (0, i))],
            out_specs=[pl.BlockSpec((SC_WINDOW, m), lambda i: (i, 0))],
            core_axis_name=("c", "s"), dimension_semantics=(pltpu.PARALLEL,),
        )(idx_hbm, out_hbm)

    return gather(table, idx.reshape(1, p))


def _scatter_rows(rows, idx_a, idx_b, n_out):
    q, m = rows.shape
    mesh = plsc.VectorSubcoreMesh(core_axis_name="c", subcore_axis_name="s")

    @functools.partial(pl.kernel, out_type=jax.ShapeDtypeStruct((n_out, m), rows.dtype), mesh=mesh)
    def scatter(rows_hbm, ia_hbm, ib_hbm, out_hbm):
        def body(rows_vmem, ia_vmem, ib_vmem):
            pltpu.sync_copy(rows_vmem, out_hbm.at[ia_vmem.at[0]])
            pltpu.sync_copy(rows_vmem, out_hbm.at[ib_vmem.at[0]])

        pltpu.emit_pipeline(
            body, grid=(q // SC_WINDOW,),
            in_specs=[pl.BlockSpec((SC_WINDOW, m), lambda i: (i, 0)),
                      pl.BlockSpec((1, SC_WINDOW), lambda i: (0, i)),
                      pl.BlockSpec((1, SC_WINDOW), lambda i: (0, i))],
            out_specs=[],
            core_axis_name=("c", "s"), dimension_semantics=(pltpu.PARALLEL,),
        )(rows_hbm, ia_hbm, ib_hbm)

    return scatter(rows, idx_a.reshape(1, q), idx_b.reshape(1, q))


def _scatter_pieces(pieces, slot, n_slots):
    npc, r, m = pieces.shape
    offs = jnp.arange(npc, dtype=jnp.int32)[:, None] * n_slots
    idx_a = (slot[0][None, :] + offs).reshape(-1)
    idx_b = (slot[1][None, :] + offs).reshape(-1)
    return _scatter_rows(pieces.reshape(npc * r, m), idx_a, idx_b, npc * n_slots).reshape(npc, n_slots, m)


def _gather_pieces(pieces, idx):
    npc, v, m = pieces.shape
    offs = (jnp.arange(npc, dtype=jnp.int32) * v).reshape((npc,) + (1,) * idx.ndim)
    flat = (idx[None] + offs).reshape(-1)
    return _gather_rows(pieces.reshape(npc * v, m), flat).reshape((npc,) + idx.shape + (m,))


def _moe_ffn_kernel(te_ref, nu_ref, x_ref, w1_ref, w3_ref, w2_ref, y_ref):
    @pl.when(pl.program_id(0) < nu_ref[0])
    def _():
        x = _load_unpacked(x_ref).astype(BF16)
        hid = _silu(_dot(x, w1_ref[...])) * _dot(x, w3_ref[...])
        _store_pieces(y_ref, _pack_halves(_dot(hid.astype(BF16), w2_ref[...])))


def _moe_ffn(xs, te, n_used, w1, w3, w2, j):
    npc, p, m = xs.shape
    d = 2 * npc * m
    dff = w1.shape[3]
    tm = MOE_ROWS

    def row(i, te_ref, nu_ref):
        return (0, jnp.minimum(i, nu_ref[0] - 1), 0)

    def expert(i, te_ref, nu_ref):
        return (j, te_ref[i], 0, 0)

    grid_spec = pltpu.PrefetchScalarGridSpec(
        num_scalar_prefetch=2, grid=(p // tm,),
        in_specs=[pl.BlockSpec((npc, tm, m), row),
                  pl.BlockSpec((None, None, d, dff), expert),
                  pl.BlockSpec((None, None, d, dff), expert),
                  pl.BlockSpec((None, None, dff, d), expert)],
        out_specs=pl.BlockSpec((npc, tm, m), row))
    return pl.pallas_call(
        _moe_ffn_kernel, grid_spec=grid_spec,
        out_shape=jax.ShapeDtypeStruct((npc, p, m), jnp.int32),
        compiler_params=_cparams("arbitrary"), name="moe_ffn",
    )(te, n_used, xs, w1, w3, w2)


def _moe_combine_kernel(ya_ref, yb_ref, route_ref, h_ref, gt_ref, o_ref):
    r = route_ref[...]
    y = r[:, 2:3] * _load_unpacked(ya_ref) + r[:, 3:4] * _load_unpacked(yb_ref)
    o_ref[...] = h_ref[...] + gt_ref[...] * y


def _moe_combine(yg, route, h, mods, n_batch, tiles_per_batch, ctx_tiles):
    d = h.shape[1]
    npc, _, r, m = yg.shape
    tm = TOKEN_TILE
    per = tiles_per_batch - ctx_tiles
    h_idx = lambda i: ((i // per) * tiles_per_batch + ctx_tiles + i % per, 0)
    if ctx_tiles == 0:
        mod_idx = _mod_row_spec(5, tiles_per_batch, n_batch)
    else:
        mod_idx = lambda i: (i // per, 5, 0, 0)
    return pl.pallas_call(
        _moe_combine_kernel, grid=(r // tm,),
        in_specs=[pl.BlockSpec((npc, None, tm, m), lambda i: (0, 0, i, 0)),
                  pl.BlockSpec((npc, None, tm, m), lambda i: (0, 1, i, 0)),
                  pl.BlockSpec((tm, LANES), lambda i: (i, 0)),
                  pl.BlockSpec((tm, d), h_idx),
                  pl.BlockSpec((None, None, 1, d), mod_idx)],
        out_specs=pl.BlockSpec((tm, d), lambda i: (i, 0)),
        out_shape=jax.ShapeDtypeStruct((r, d), F32),
        compiler_params=_cparams("parallel"), name="moe_combine",
    )(yg, yg, route, h, mods)


def _moe_residual(h, g, w_router, w1, w3, w2, j, mods, n_batch, t, n_ctx, latent_only):
    tpb = t // TOKEN_TILE
    ctx_tiles = n_ctx // TOKEN_TILE if latent_only else 0
    u_pieces, route, route_t, counts = _modulate_router(h, g, mods, 3, n_batch, tpb, ctx_tiles, w_router)
    if latent_only:
        npc, _, m = u_pieces.shape
        u_pieces = u_pieces.reshape(npc, n_batch, t, m)[:, :, n_ctx:].reshape(npc, -1, m)
        route = route.reshape(n_batch, t, LANES)[:, n_ctx:].reshape(-1, LANES)
        route_t = route_t.reshape(-1, n_batch, t)[:, :, n_ctx:].reshape(route_t.shape[0], -1)
    n_exp = w1.shape[1]
    slot, n_slots, te, n_used = _route_tables(route_t, counts[0, :n_exp].astype(jnp.int32), MOE_ROWS, n_exp)
    xs = _scatter_pieces(u_pieces, slot, n_slots)
    ys = _moe_ffn(xs, te, n_used, w1, w3, w2, j)
    yg = _gather_pieces(ys, slot)
    return _moe_combine(yg, route, h, mods, n_batch, tpb, ctx_tiles)


def kernel(x, c, ctx, c_ctx, w_mod, b_mod, norm_mix_g, norm_ffn_g, ev_w_in, ev_w_out, hg_lb_logits, hg_onorm_g, rg_conv_w, rg_conv_b, rg_wa, rg_ba, rg_wx, rg_bx, rg_lambda, na_w_qkv, na_w_o, na_q_g, na_k_g, na_rpb, ffn_w1, ffn_w3, ffn_w2, moe_router, moe_w1, moe_w3, moe_w2):
    bsz, seq, d = x.shape
    n_ctx = ctx.shape[1]
    t = n_ctx + seq
    depth = w_mod.shape[0]
    assert n_ctx % TOKEN_TILE == 0 and seq % TOKEN_TILE == 0 and t % ROW_TILE == 0 and n_ctx <= ROW_TILE

    rows = -(-(bsz + 1) // 8) * 8
    cond = jnp.zeros((rows, d), F32).at[:bsz].set(c).at[bsz].set(c_ctx)
    mods_all = _adaln(cond, w_mod, b_mod).reshape(depth, rows, N_MOD, 1, d)

    lb_p = jax.nn.softmax(hg_lb_logits.astype(F32), axis=0)
    lb_all = jnp.cumsum(lb_p, axis=0) - lb_p[0]

    bf = lambda w: w.astype(BF16)
    w_in, w_out, w_qkv, w_o = bf(ev_w_in), bf(ev_w_out), bf(na_w_qkv), bf(na_w_o)
    f_w1, f_w3, f_w2 = bf(ffn_w1), bf(ffn_w3), bf(ffn_w2)
    m_w1, m_w3, m_w2 = bf(moe_w1), bf(moe_w3), bf(moe_w2)

    h = jnp.concatenate([ctx, x], axis=1).reshape(bsz * t, d)
    for layer in range(depth):
        j = layer // 2
        mods = mods_all[layer]
        g_mix = norm_mix_g[layer].reshape(1, d)
        if layer % 2 == 0:
            p = _modulated_matmul(h, g_mix, mods, w_in, j, F32, ROW_TILE // 2, bsz, t, n_ctx)
            p3 = p.reshape(bsz, t, -1)
            hg = _hgrn2(p3, lb_all[j], n_ctx)
            half = hg.shape[-1]
            rg = _rglru(p3, 5 * half // (d - half), rg_conv_w[j], rg_conv_b[j], rg_wa[j], rg_ba[j],
                        rg_wx[j], rg_bx[j], rg_lambda[j], n_ctx)
            h = _merge_out(hg.reshape(bsz * t, half), rg.reshape(bsz * t, d - half), p, hg_onorm_g[j],
                           w_out, j, h, mods, bsz, t, n_ctx)
        else:
            qkv = _modulated_matmul(h, g_mix, mods, w_qkv, j, BF16, ROW_TILE, bsz, t, n_ctx)
            o = _na_attention(qkv.reshape(bsz, t, 3 * d), na_q_g[j], na_k_g[j], na_rpb[j], n_ctx)
            h = _matmul_residual(o.reshape(bsz * t, d), w_o, j, h, mods, 2, bsz, t, n_ctx)
        if layer % 2 == 0:
            h = _swiglu_residual(h, norm_ffn_g[layer].reshape(1, d), f_w1, f_w3, f_w2, j, mods, bsz, t, n_ctx)
        else:
            h = _moe_residual(h, norm_ffn_g[layer].reshape(1, d), moe_router[j], m_w1, m_w3, m_w2, j, mods,
                              bsz, t, n_ctx, layer == depth - 1)
    if depth % 2 == 0:
        return h.reshape(bsz, seq, d)
    return h.reshape(bsz, t, d)[:, n_ctx:]
```

```python
import functools

import numpy as np
import jax
import jax.numpy as jnp
from jax import lax
from jax.experimental import pallas as pl
from jax.experimental.pallas import tpu as pltpu
from jax.experimental.pallas import tpu_sc as plsc

F32 = jnp.float32
BF16 = jnp.bfloat16

EPS = 1e-6
NEG_BIG = -1e9
LB_MIN = 1e-30
LOG2_E = 1.4426950408889634
N_MOD = 6
GRID_W = 64
HG_HEAD_DIM = 128
RG_BLOCKS = 8
RG_CONV = 4
RG_C = 8.0
NA_HEADS = 16
NA_KR = 8
NA_KC = 16
N_EXPERTS = 8

LANES = 128
SC_WINDOW = 128
SC_PIECES = 2
TOKEN_TILE = 256
HG_CHUNK = 128
HG_LEVELS = 7
HG_SLOTS = 2
NA_SLOTS = 4
RG_SCAN_BLOCK = 256
RG_GROUP = 8
VMEM_LIMIT = 56 * 1024 * 1024


def _cparams(*sem):
    return pltpu.CompilerParams(dimension_semantics=sem, vmem_limit_bytes=VMEM_LIMIT)


def _dot(a, b):
    return jnp.dot(a, b, preferred_element_type=F32)


def _dot_nt(a, b):
    return lax.dot_general(a, b, (((1,), (1,)), ((), ())), preferred_element_type=F32)


def _dot_tn(a, b):
    return lax.dot_general(a, b, (((0,), (0,)), ((), ())), preferred_element_type=F32)


def _split2(x):
    hi = x.astype(BF16)
    lo = (x - hi.astype(F32)).astype(BF16)
    return hi, lo


def _sigmoid(x):
    return 0.5 * jnp.tanh(0.5 * x) + 0.5


def _silu(x):
    return x * _sigmoid(x)


def _softplus_neg_abs(x):
    return jnp.log(1.0 + jnp.exp(-jnp.abs(x)))


def _log_sigmoid(x):
    return jnp.minimum(x, 0.0) - _softplus_neg_abs(x)


def _gelu_tanh(x):
    return 0.5 * x * (1.0 + jnp.tanh(np.sqrt(2.0 / np.pi) * (x + 0.044715 * (x * x * x))))


def _adaln_kernel(c_ref, w_ref, b_ref, o_ref):
    a_hi, a_lo = _split2(_silu(c_ref[...]))
    w_hi, w_lo = _split2(w_ref[...])
    o_ref[...] = _dot(a_hi, w_hi) + _dot(a_hi, w_lo) + _dot(a_lo, w_hi) + b_ref[...]


def _adaln(cond, w_mod, b_mod):
    depth, d, n = w_mod.shape
    r = cond.shape[0]
    tn = 1536
    return pl.pallas_call(
        _adaln_kernel,
        grid=(depth, n // tn),
        in_specs=[pl.BlockSpec((r, d), lambda l, j: (0, 0)),
                  pl.BlockSpec((None, d, tn), lambda l, j: (l, 0, j)),
                  pl.BlockSpec((None, 1, tn), lambda l, j: (l, 0, j))],
        out_specs=pl.BlockSpec((None, r, tn), lambda l, j: (l, 0, j)),
        out_shape=jax.ShapeDtypeStruct((depth, r, n), F32),
        compiler_params=_cparams("parallel", "parallel"),
        name="adaln",
    )(cond, w_mod, b_mod.reshape(depth, 1, n))


def _mod_row_spec(k, tiles_per_batch, n_batch):
    def idx(i):
        row = jnp.where(i % tiles_per_batch == 0, n_batch, i // tiles_per_batch)
        return (row, k, 0, 0)
    return idx


def _modulated(h_ref, g_ref, sh_ref, sc_ref):
    x = h_ref[...]
    y = x * lax.rsqrt(jnp.mean(x * x, axis=-1, keepdims=True) + EPS)
    return (y * g_ref[...]) * (1.0 + sc_ref[...]) + sh_ref[...]


def _pack_halves(x):
    m = x.shape[1] // 2
    lo = lax.bitcast_convert_type(x[:, :m].astype(BF16).astype(F32), jnp.int32)
    hi = lax.bitcast_convert_type(x[:, m:].astype(BF16).astype(F32), jnp.int32)
    return lax.shift_right_logical(lo, 16) | (hi & jnp.int32(-65536))


def _unpack_halves(w):
    lo = lax.bitcast_convert_type(lax.shift_left(w, 16), F32)
    hi = lax.bitcast_convert_type(w & jnp.int32(-65536), F32)
    return lo, hi


def _store_pieces(ref, w):
    m = w.shape[1] // SC_PIECES
    for j in range(SC_PIECES):
        ref[j] = w[:, j * m:(j + 1) * m]


def _load_unpacked(ref):
    parts = [_unpack_halves(ref[j]) for j in range(SC_PIECES)]
    return jnp.concatenate([lo for lo, _ in parts] + [hi for _, hi in parts], axis=1)


def _modulate_router_kernel(h_ref, g_ref, sh_ref, sc_ref, wr_ref, u_ref, route_ref, route_t_ref, cnt_ref,
                            *, tiles_per_batch, skip_tiles):
    i = pl.program_id(0)

    @pl.when(i == 0)
    def _():
        cnt_ref[...] = jnp.zeros_like(cnt_ref)

    u = _modulated(h_ref, g_ref, sh_ref, sc_ref)
    _store_pieces(u_ref, _pack_halves(u))
    u0, u1 = _split2(u)
    w0, w1 = _split2(wr_ref[...])
    logits = _dot(u0, w0) + _dot(u0, w1) + _dot(u1, w0)
    lane = lax.broadcasted_iota(jnp.int32, logits.shape, 1).astype(F32)
    lg = jnp.where(lane < N_EXPERTS, logits, -jnp.inf)
    m1 = jnp.max(lg, axis=-1, keepdims=True)
    i1 = jnp.min(jnp.where(lg == m1, lane, float(LANES)), axis=-1, keepdims=True)
    lg2 = jnp.where(lane == i1, -jnp.inf, lg)
    m2 = jnp.max(lg2, axis=-1, keepdims=True)
    i2 = jnp.min(jnp.where(lg2 == m2, lane, float(LANES)), axis=-1, keepdims=True)
    e2 = jnp.exp(m2 - m1)
    den = 1.0 + e2

    routed = jnp.where(i % tiles_per_batch >= skip_tiles, 1.0, 0.0)
    pick_a = jnp.where(lane == i1, routed, 0.0)
    pick_b = jnp.where(lane == i2, routed, 0.0)
    tm = pick_a.shape[0]
    earlier = jnp.where(lax.broadcasted_iota(jnp.int32, (tm, tm), 1) < lax.broadcasted_iota(jnp.int32, (tm, tm), 0),
                        1.0, 0.0).astype(BF16)
    total_a = jnp.sum(pick_a, axis=0, keepdims=True)
    base = cnt_ref[0:1, :]
    rank_a = jnp.sum(pick_a * (base + _dot(earlier, pick_a.astype(BF16))), axis=-1, keepdims=True)
    rank_b = jnp.sum(pick_b * (base + total_a + _dot(earlier, pick_b.astype(BF16))), axis=-1, keepdims=True)
    cnt_ref[0:1, :] = base + total_a + jnp.sum(pick_b, axis=0, keepdims=True)

    route = jnp.where(lane == 0, i1, jnp.where(lane == 1, i2, jnp.where(lane == 2, 1.0 / den, jnp.where(
        lane == 3, e2 / den, jnp.where(lane == 4, rank_a, rank_b)))))
    route_ref[...] = route
    route_t_ref[...] = route.T[0:route_t_ref.shape[0], :]


def _modulate_router(h, g, mods, k_shift, n_batch, tiles_per_batch, skip_tiles, w_router):
    n, d = h.shape
    tm = TOKEN_TILE
    row = pl.BlockSpec((tm, d), lambda i: (i, 0))
    vec = pl.BlockSpec((1, d), lambda i: (0, 0))

    def mod_spec(k):
        return pl.BlockSpec((None, None, 1, d), _mod_row_spec(k, tiles_per_batch, n_batch))

    in_specs = [row, vec, mod_spec(k_shift), mod_spec(k_shift + 1)]
    wr = jnp.zeros((d, LANES), F32).at[:, :N_EXPERTS].set(w_router)
    kern = functools.partial(_modulate_router_kernel, tiles_per_batch=tiles_per_batch, skip_tiles=skip_tiles)
    return pl.pallas_call(
        kern, grid=(n // tm,),
        in_specs=in_specs + [pl.BlockSpec((d, LANES), lambda i: (0, 0))],
        out_specs=[pl.BlockSpec((SC_PIECES, tm, d // 2 // SC_PIECES), lambda i: (0, i, 0)),
                   pl.BlockSpec((tm, LANES), lambda i: (i, 0)),
                   pl.BlockSpec((8, tm), lambda i: (0, i)),
                   pl.BlockSpec((8, LANES), lambda i: (0, 0))],
        out_shape=[jax.ShapeDtypeStruct((SC_PIECES, n, d // 2 // SC_PIECES), jnp.int32),
                   jax.ShapeDtypeStruct((n, LANES), F32),
                   jax.ShapeDtypeStruct((8, n), F32),
                   jax.ShapeDtypeStruct((8, LANES), F32)],
        compiler_params=_cparams("arbitrary"), name="modulate_router",
    )(h, g, mods, mods, wr)


ROW_TILE = 768


def _row_gate(ctx_ref, own_ref, shape, tile_in_batch, n_ctx):
    rows = lax.broadcasted_iota(jnp.int32, shape, 0)
    is_ctx = (rows < n_ctx) & (tile_in_batch == 0)
    return jnp.where(is_ctx, ctx_ref[...], own_ref[...])


def _modulated_rows(h_ref, g_ref, shc_ref, sh_ref, scc_ref, sc_ref, tile_in_batch, n_ctx):
    x = h_ref[...]
    y = x * lax.rsqrt(jnp.mean(x * x, axis=-1, keepdims=True) + EPS)
    shift = _row_gate(shc_ref, sh_ref, x.shape, tile_in_batch, n_ctx)
    scale = _row_gate(scc_ref, sc_ref, x.shape, tile_in_batch, n_ctx)
    return (y * g_ref[...]) * (1.0 + scale) + shift


def _stream_specs(d, n_batch, tiles_per_batch, k, nargs):
    if nargs == 1:
        ctx = lambda i: (n_batch, k, 0, 0)
        own = lambda i: (i // tiles_per_batch, k, 0, 0)
    else:
        ctx = lambda i, f: (n_batch, k, 0, 0)
        own = lambda i, f: (i // tiles_per_batch, k, 0, 0)
    return [pl.BlockSpec((None, None, 1, d), ctx), pl.BlockSpec((None, None, 1, d), own)]


def _mod_mm_kernel(h_ref, g_ref, shc_ref, sh_ref, scc_ref, sc_ref, w_ref, slab_ref, *rest_ref,
                   n_ctx, tiles_per_batch):
    x = _modulated_rows(h_ref, g_ref, shc_ref, sh_ref, scc_ref, sc_ref,
                        pl.program_id(0) % tiles_per_batch, n_ctx).astype(BF16)
    y = _dot(x, w_ref[...])
    n_slabs = slab_ref.shape[0]
    for s in range(n_slabs):
        slab_ref[s] = y[:, s * LANES:(s + 1) * LANES].astype(slab_ref.dtype)
    if rest_ref:
        rest_ref[0][...] = y[:, n_slabs * LANES:].astype(rest_ref[0].dtype)


def _modulated_matmul(h, g, mods, w, j, out_dtype, n_slabs, tm, n_batch, t, n_ctx):
    n, d = h.shape
    nout = w.shape[2]
    tpb = t // tm
    n_rest = nout - n_slabs * LANES
    assert t % tm == 0 and n_ctx <= tm
    kern = functools.partial(_mod_mm_kernel, n_ctx=n_ctx, tiles_per_batch=tpb)
    out_specs = [pl.BlockSpec((n_slabs, tm, LANES), lambda i: (0, i, 0))]
    out_shape = [jax.ShapeDtypeStruct((n_slabs, n, LANES), out_dtype)]
    if n_rest:
        out_specs.append(pl.BlockSpec((tm, n_rest), lambda i: (i, 0)))
        out_shape.append(jax.ShapeDtypeStruct((n, n_rest), out_dtype))
    return pl.pallas_call(
        kern, grid=(n // tm,),
        in_specs=[pl.BlockSpec((tm, d), lambda i: (i, 0)),
                  pl.BlockSpec((1, d), lambda i: (0, 0))]
        + _stream_specs(d, n_batch, tpb, 0, 1) + _stream_specs(d, n_batch, tpb, 1, 1)
        + [pl.BlockSpec((None, d, nout), lambda i: (j, 0, 0))],
        out_specs=out_specs, out_shape=out_shape,
        compiler_params=_cparams("parallel"), name="modulated_matmul",
    )(h, g, mods, mods, mods, mods, w)


def _mm_res_kernel(x_ref, w_ref, h_ref, gtc_ref, gt_ref, o_ref, *, n_ctx, tiles_per_batch):
    gate = _row_gate(gtc_ref, gt_ref, h_ref.shape, pl.program_id(0) % tiles_per_batch, n_ctx)
    o_ref[...] = h_ref[...] + gate * _dot(x_ref[...], w_ref[...])


def _matmul_residual(x, w, j, h, mods, k_gate, n_batch, t, n_ctx):
    n, d = h.shape
    kdim = x.shape[1]
    tm = ROW_TILE
    tpb = t // tm
    kern = functools.partial(_mm_res_kernel, n_ctx=n_ctx, tiles_per_batch=tpb)
    return pl.pallas_call(
        kern, grid=(n // tm,),
        in_specs=[pl.BlockSpec((tm, kdim), lambda i: (i, 0)),
                  pl.BlockSpec((None, kdim, d), lambda i: (j, 0, 0)),
                  pl.BlockSpec((tm, d), lambda i: (i, 0))] + _stream_specs(d, n_batch, tpb, k_gate, 1),
        out_specs=pl.BlockSpec((tm, d), lambda i: (i, 0)),
        out_shape=jax.ShapeDtypeStruct((n, d), F32),
        compiler_params=_cparams("parallel"), name="matmul_residual",
    )(x, w, h, mods, mods)


def _hg_tables():
    c = HG_CHUNK
    t = np.arange(c)[:, None]
    u = np.arange(c)[None, :]
    mats = [(u <= t), (u > t)]
    for k in range(1, HG_LEVELS + 1):
        m = ((t >> k) << k) + (1 << (k - 1)) - 1
        upper = t > m
        mats.append(np.where(upper, (u > m) & (u <= t), (u > t) & (u <= m)))
    fwd = np.concatenate(mats, axis=0).astype(np.float32)
    bwd = fwd.reshape(-1, c, c)[:, ::-1, ::-1].reshape(-1, c)
    x = t ^ u
    lvl = np.where(t > u, np.floor(np.log2(np.maximum(x, 1))).astype(np.int32) + 1, -1)
    lvl = np.where(t == u, 0, lvl).astype(np.int32)
    return np.stack([fwd, bwd]), np.stack([lvl, lvl.T])


def _hg_kernel(q_ref, ff_ref, fb_ref, v_ref, lb_ref, m_ref, lvl_ref, o_ref, of_ref, ob_ref,
               qk_ref, el_ref, io_ref, tot_ref, att_ref, *, n_ctx_chunks, n_chunks):
    c = HG_CHUNK
    gap = HG_SLOTS // 2
    t = o_ref.shape[0]
    z_refs = (ff_ref, fb_ref)
    out_refs = (of_ref, ob_ref)
    for ref in (qk_ref, el_ref, io_ref, tot_ref, att_ref):
        ref[...] = jnp.zeros(ref.shape, ref.dtype)

    def start_row(d, j):
        j = jnp.clip(j, 0, n_chunks - 1)
        if d == 1:
            j = jnp.where(j < n_ctx_chunks, n_ctx_chunks - 1 - j, n_chunks - 1 - (j - n_ctx_chunks))
        return pl.multiple_of(j * c, c)

    def stage1(d, j, slot):
        t0 = start_row(d, j)
        lb = lb_ref[d:d + 1, :]
        log_lb = jnp.log(jnp.maximum(lb, LB_MIN))
        q = _silu(q_ref[pl.ds(t0, c), :])
        z = z_refs[d][pl.ds(t0, c), :]
        k = (1.0 - lb) * _sigmoid(-z)
        b = jnp.log1p(-lb) + _log_sigmoid(z)
        logf = jnp.maximum(log_lb, b) + _softplus_neg_abs(log_lb - b)
        g_hi, g_lo = _split2(logf * LOG2_E)
        m_ends = m_ref[d, 0:2 * c, :]
        e_ends = jnp.exp2(_dot(m_ends, g_hi) + _dot(m_ends, g_lo))
        el_ref[d, slot] = jnp.exp2(_dot(m_ref[d, 2 * c:, :], g_hi).astype(BF16))
        qk_ref[d, slot, 0] = q.astype(BF16)
        qk_ref[d, slot, 1] = k.astype(BF16)
        io_ref[d, slot, 0] = (q * e_ends[0:c]).astype(BF16)
        io_ref[d, slot, 1] = (k * e_ends[c:]).astype(BF16)
        tot_row = c - 1 if d == 0 else 0
        tot_ref[d, slot, 0:1, :] = e_ends[tot_row:tot_row + 1, :]

    def stage2(d, slot):
        qb = qk_ref[d, slot, 0]
        kb = qk_ref[d, slot, 1]
        lvl = lvl_ref[d]
        att = jnp.where(lvl == 0, _dot_nt(qb, kb), 0.0)
        for lev in range(1, HG_LEVELS + 1):
            el = el_ref[d, slot, (lev - 1) * c:lev * c, :]
            att = jnp.where(lvl == lev, _dot_nt(qb * el, kb * el), att)
        att_ref[d, slot] = att.astype(BF16)

    def stage3(d, j, slot, st):
        t0 = start_row(d, j)
        v = v_ref[pl.ds(t0, c), :].astype(BF16)
        row = jnp.where(j < n_chunks, t0, t)
        out_refs[d][pl.ds(pl.multiple_of(row, c), c), :] = (
            _dot(att_ref[d, slot], v) + _dot_nt(io_ref[d, slot, 0], st.astype(BF16)))
        return st * tot_ref[d, slot, 0:1, :] + _dot_tn(v, io_ref[d, slot, 1])

    def steps(jj, carry):
        st = list(carry)
        for slot in range(HG_SLOTS):
            i = HG_SLOTS * jj + slot
            for d in range(2):
                st[d] = stage3(d, i - 2 * gap, slot, st[d])
            for d in range(2):
                stage2(d, (slot + gap) % HG_SLOTS)
            for d in range(2):
                stage1(d, i, slot)
        return tuple(st)

    zero = jnp.zeros((c, c), F32)
    n_steps = -(-(n_chunks + 2 * gap) // HG_SLOTS)
    lax.fori_loop(0, n_steps, steps, (zero, zero))
    o_ref[...] = (of_ref[0:t, :] + ob_ref[0:t, :]).astype(o_ref.dtype)


def _hgrn2(slabs, lb, n_ctx):
    _, bsz, t, _ = slabs.shape
    width = lb.shape[1]
    heads = width // HG_HEAD_DIM
    mats, lvl = _hg_tables()
    c = HG_CHUNK

    def col(group):
        return pl.BlockSpec((None, None, t, c), lambda b, h: (group * heads + h, b, 0, 0))

    kern = functools.partial(_hg_kernel, n_ctx_chunks=n_ctx // c, n_chunks=t // c)
    return pl.pallas_call(
        kern, grid=(bsz, heads),
        in_specs=[col(0), col(1), col(2), col(3),
                  pl.BlockSpec((2, c), lambda b, h: (0, h)),
                  pl.BlockSpec(mats.shape, lambda b, h: (0, 0, 0)),
                  pl.BlockSpec(lvl.shape, lambda b, h: (0, 0, 0))],
        out_specs=pl.BlockSpec((None, t, c), lambda b, h: (b, 0, h)),
        out_shape=jax.ShapeDtypeStruct((bsz, t, width), F32),
        scratch_shapes=[pltpu.VMEM((t + c, c), F32), pltpu.VMEM((t + c, c), F32),
                        pltpu.VMEM((2, HG_SLOTS, 2, c, c), BF16), pltpu.VMEM((2, HG_SLOTS, HG_LEVELS * c, c), BF16),
                        pltpu.VMEM((2, HG_SLOTS, 2, c, c), BF16), pltpu.VMEM((2, HG_SLOTS, 8, c), F32),
                        pltpu.VMEM((2, HG_SLOTS, c, c), BF16)],
        compiler_params=_cparams("parallel", "parallel"), name="hgrn2",
    )(slabs, slabs, slabs, slabs, lb, jnp.asarray(mats, BF16), jnp.asarray(lvl))


PAD = 8


def _rg_kernel(x_ref, cw_ref, cb_ref, wa_ref, ba_ref, wx_ref, bx_ref, lam_ref, o_ref,
               xpad_ref, xc_ref, ob_ref, *, n_ctx):
    t, w = x_ref.shape
    n_lat = t - n_ctx
    lat0 = n_ctx + 2 * PAD
    xpad_ref[...] = jnp.zeros(xpad_ref.shape, F32)
    xpad_ref[PAD:PAD + n_ctx, :] = x_ref[0:n_ctx, :]
    xpad_ref[lat0:lat0 + n_lat, :] = x_ref[n_ctx:t, :]
    left = RG_CONV // 2
    for base, dst, n in ((PAD, 0, n_ctx), (lat0, n_ctx, n_lat)):
        acc = cb_ref[...] + cw_ref[0:1, :] * xpad_ref[base - left:base - left + n, :]
        for j in range(1, RG_CONV):
            acc = acc + cw_ref[j:j + 1, :] * xpad_ref[base - left + j:base - left + j + n, :]
        xc_ref[dst:dst + n, :] = acc

    blk = RG_SCAN_BLOCK
    n_blocks = t // blk
    n_ctx_blocks = n_ctx // blk
    sub = lax.broadcasted_iota(jnp.int32, (blk // RG_GROUP, RG_GROUP, w), 1)
    c_log_sig_lam = RG_C * _log_sigmoid(lam_ref[...])

    def scan_block(d, t0, hin, out_ref):
        xc = xc_ref[pl.ds(t0, blk), :]
        xb = xc.astype(BF16)
        r = _sigmoid(_dot(xb, wa_ref[d]) + ba_ref[d:d + 1, :])
        ig = _sigmoid(_dot(xb, wx_ref[d]) + bx_ref[d:d + 1, :])
        log_a = r * c_log_sig_lam[d:d + 1, :]
        a = jnp.exp(log_a)
        b = jnp.sqrt(jnp.maximum(1.0 - a * a, 0.0)) * (ig * xc)
        n_groups = blk // RG_GROUP
        a = a.reshape(n_groups, RG_GROUP, w)
        b = b.reshape(n_groups, RG_GROUP, w)
        s = 1
        while s < RG_GROUP:
            shift = s if d == 0 else RG_GROUP - s
            keep = (sub >= s) if d == 0 else (sub < RG_GROUP - s)
            b = jnp.where(keep, a * pltpu.roll(b, shift, 1), 0.0) + b
            a = jnp.where(keep, a * pltpu.roll(a, shift, 1), a)
            s *= 2
        for gi in range(n_groups):
            gj = gi if d == 0 else n_groups - 1 - gi
            hg = a[gj] * hin + b[gj]
            out_ref[pl.ds(t0 + gj * RG_GROUP, RG_GROUP), :] = hg
            hin = hg[RG_GROUP - 1:RG_GROUP, :] if d == 0 else hg[0:1, :]
        return hin

    def step(i, carry):
        h_f, h_b = carry
        bb = jnp.where(i < n_ctx_blocks, n_ctx_blocks - 1 - i, n_blocks - 1 - (i - n_ctx_blocks))
        h_f = scan_block(0, pl.multiple_of(i * blk, blk), h_f, o_ref)
        h_b = scan_block(1, pl.multiple_of(bb * blk, blk), h_b, ob_ref)
        return h_f, h_b

    zero = jnp.zeros((1, w), F32)
    lax.fori_loop(0, n_blocks, step, (zero, zero))
    o_ref[...] += ob_ref[...]


def _blockdiag_dense(w):
    nd, nb, k, _ = w.shape
    eye = jnp.eye(nb, dtype=w.dtype)
    return jnp.einsum('dnij,nm->dnimj', w, eye).reshape(nd, nb * k, nb * k)


def _rglru(p3, col_block, conv_w, conv_b, wa, ba, wx, bx, lam, n_ctx):
    bsz, t, _ = p3.shape
    w = conv_w.shape[1]
    full = lambda shape: pl.BlockSpec(shape, lambda b: (0,) * len(shape))
    kern = functools.partial(_rg_kernel, n_ctx=n_ctx)
    return pl.pallas_call(
        kern, grid=(bsz,),
        in_specs=[pl.BlockSpec((None, t, w), lambda b: (b, 0, col_block)),
                  full((RG_CONV, w)), full((1, w)),
                  full((2, w, w)), full((2, w)), full((2, w, w)), full((2, w)), full((2, w))],
        out_specs=pl.BlockSpec((None, t, w), lambda b: (b, 0, 0)),
        out_shape=jax.ShapeDtypeStruct((bsz, t, w), F32),
        scratch_shapes=[pltpu.VMEM((t + 3 * PAD, w), F32)] + [pltpu.VMEM((t, w), F32)] * 2,
        compiler_params=_cparams("parallel"), name="rglru",
    )(p3, conv_w, conv_b.reshape(1, w), _blockdiag_dense(wa).astype(BF16), ba,
      _blockdiag_dense(wx).astype(BF16), bx, lam)


def _merge_kernel(hg_ref, rg_ref, gate_ref, y_ref, og_ref, w_ref, h_ref, gtc_ref, gt_ref, o_ref,
                  *, n_ctx, tiles_per_batch):
    hg = hg_ref[...]
    half = hg.shape[1]
    hn = hg * lax.rsqrt(jnp.mean(hg * hg, axis=-1, keepdims=True) + EPS) * og_ref[...]
    a = (hn * _silu(gate_ref[...])).astype(BF16)
    b = (rg_ref[...] * _gelu_tanh(y_ref[...])).astype(BF16)
    o = _dot(a, w_ref[0:half, :]) + _dot(b, w_ref[half:, :])
    gate = _row_gate(gtc_ref, gt_ref, h_ref.shape, pl.program_id(0) % tiles_per_batch, n_ctx)
    o_ref[...] = h_ref[...] + gate * o


def _merge_out(hg, rg, p, onorm_g, w_out, j, h, mods, n_batch, t, n_ctx):
    n, d = h.shape
    half = hg.shape[1]
    tm = ROW_TILE
    tpb = t // tm
    blk = lambda c: pl.BlockSpec((tm, half), lambda i: (i, c))
    kern = functools.partial(_merge_kernel, n_ctx=n_ctx, tiles_per_batch=tpb)
    return pl.pallas_call(
        kern, grid=(n // tm,),
        in_specs=[blk(0), blk(0), blk(0), blk(2),
                  pl.BlockSpec((1, half), lambda i: (0, 0)),
                  pl.BlockSpec((None, d, d), lambda i: (j, 0, 0)),
                  pl.BlockSpec((tm, d), lambda i: (i, 0))] + _stream_specs(d, n_batch, tpb, 2, 1),
        out_specs=pl.BlockSpec((tm, d), lambda i: (i, 0)),
        out_shape=jax.ShapeDtypeStruct((n, d), F32),
        compiler_params=_cparams("parallel"), name="merge_out",
    )(hg, rg, p, p, onorm_g.reshape(1, half), w_out, h, mods, mods)


def _na_bias_table(rpb):
    q = np.arange(GRID_W)[:, None]
    kc = np.arange(GRID_W)[None, :]
    wstart = np.clip(q - NA_KC // 2, 0, GRID_W - NA_KC)
    mask = (kc >= wstart) & (kc < wstart + NA_KC)
    coff = np.clip(kc - q + NA_KC - 1, 0, 2 * NA_KC - 2)
    sel = (coff[None] == np.arange(2 * NA_KC - 1)[:, None, None]).astype(np.float32)
    toe = jnp.einsum('hdj,jqk->hqdk', rpb.astype(F32), sel, precision=lax.Precision.HIGHEST)
    toe = jnp.where(mask[None, :, None, :], toe, NEG_BIG)
    h = toe.shape[0]
    flat = toe.reshape(h // 2, 2 * GRID_W, -1)
    width = (2 * NA_KR) * GRID_W
    even = jnp.pad(flat, ((0, 0), (0, 0), (0, width - flat.shape[2])))
    odd = jnp.pad(flat[:, :, GRID_W:], ((0, 0), (0, 0), (0, width - flat.shape[2] + GRID_W)))
    return jnp.stack([even, odd], axis=1)


def _stack_heads(x, hd):
    lane = lax.broadcasted_iota(jnp.int32, x.shape, 1)
    zero = jnp.zeros_like(x)
    return jnp.concatenate([jnp.where(lane < hd, x, zero), jnp.where(lane >= hd, x, zero)], axis=0)


def _unstack_heads(o2, hd):
    n = o2.shape[0] // 2
    lane = lax.broadcasted_iota(jnp.int32, (n, o2.shape[1]), 1)
    return jnp.where(lane < hd, o2[:n], o2[n:])


def _na_kernel(q_ref, k_ref, v_ref, qg_ref, kg_ref, bias_ref, o_ref, qn_ref, kn_ref,
               s_ref, p_ref, den_ref, *, n_ctx, scale):
    t, w = q_ref.shape
    hd = w // 2
    rows_grid = (t - n_ctx) // GRID_W
    r_i = lax.broadcasted_iota(jnp.int32, (w, w), 0)
    c_i = lax.broadcasted_iota(jnp.int32, (w, w), 1)
    seg = jnp.where((r_i // hd) == (c_i // hd), 1.0 / hd, 0.0).astype(BF16)

    def normed(x, g):
        ms = _dot((x * x).astype(BF16), seg)
        return (x * lax.rsqrt(ms + EPS) * g).astype(BF16)

    qn_ref[...] = normed(q_ref[...].astype(F32), qg_ref[...] * scale)
    kn_ref[...] = normed(k_ref[...].astype(F32), kg_ref[...])
    vb_ref = v_ref

    kc = kn_ref[0:n_ctx, :]
    vc = vb_ref[0:n_ctx, :]

    s = _dot_nt(_stack_heads(qn_ref[0:n_ctx, :], hd), kc)
    p = jnp.exp(s - jnp.max(s, axis=-1, keepdims=True))
    o2 = _dot(p.astype(BF16), vc) / jnp.sum(p, axis=-1, keepdims=True)
    o_ref[0:n_ctx, :] = _unstack_heads(o2, hd).astype(o_ref.dtype)

    kr = NA_KR
    n_win = kr * GRID_W

    def window(r):
        r = jnp.clip(r, 0, rows_grid - 1)
        rs = jnp.clip(r - kr // 2, 0, rows_grid - kr)
        q0 = pl.multiple_of(n_ctx + r * GRID_W, GRID_W)
        k0 = pl.multiple_of(n_ctx + rs * GRID_W, GRID_W)
        return q0, k0, rs - r + (kr - 1)

    gap = NA_SLOTS // 2
    for slot in range(NA_SLOTS):
        s_ref[slot] = jnp.zeros(s_ref.shape[1:], F32)
        p_ref[slot] = jnp.zeros(p_ref.shape[1:], BF16)
        den_ref[slot] = jnp.ones(den_ref.shape[1:], F32)

    def step(i, slot):
        q0, k0, _ = window(i - 2 * gap)
        pr = p_ref[slot]
        o2 = (_dot(pr[:, :n_win], vb_ref[pl.ds(k0, n_win), :]) + _dot(pr[:, n_win:], vc)) / den_ref[slot]
        o_ref[pl.ds(q0, GRID_W), :] = _unstack_heads(o2, hd).astype(o_ref.dtype)

        mid = (slot + gap) % NA_SLOTS
        sc = s_ref[mid]
        pe = jnp.exp(sc - jnp.max(sc, axis=-1, keepdims=True))
        den_ref[mid] = jnp.sum(pe, axis=-1, keepdims=True)
        p_ref[mid] = pe.astype(BF16)

        q0, k0, off = window(i)
        q2 = _stack_heads(qn_ref[pl.ds(q0, GRID_W), :], hd)
        bias = bias_ref[off % 2, :, pl.ds(pl.multiple_of((off // 2) * (2 * GRID_W), 2 * GRID_W), n_win)]
        s_ref[slot, :, :n_win] = _dot_nt(q2, kn_ref[pl.ds(k0, n_win), :]) + bias
        s_ref[slot, :, n_win:] = _dot_nt(q2, kc)

    def steps(j, carry):
        for slot in range(NA_SLOTS):
            step(NA_SLOTS * j + slot, slot)
        return carry

    lax.fori_loop(0, (rows_grid + 2 * gap) // NA_SLOTS, steps, 0)


def _na_attention(qkv3, q_g, k_g, rpb, n_ctx):
    slabs, bsz, t, w = qkv3.shape
    pairs = slabs // 3
    d = pairs * w
    hd = d // NA_HEADS
    rows_grid = (t - n_ctx) // GRID_W
    assert w == LANES and rows_grid >= NA_KR and rows_grid % NA_SLOTS == 0
    n_keys = NA_KR * GRID_W + n_ctx
    bias = _na_bias_table(rpb)
    kern = functools.partial(_na_kernel, n_ctx=n_ctx, scale=hd ** -0.5)

    def col(group):
        return pl.BlockSpec((None, None, t, w), lambda hp, b: (group * pairs + hp, b, 0, 0))

    gain = lambda g: jnp.tile(g.astype(F32), 2).reshape(1, w)
    return pl.pallas_call(
        kern, grid=(pairs, bsz),
        in_specs=[col(0), col(1), col(2),
                  pl.BlockSpec((1, w), lambda hp, b: (0, 0)),
                  pl.BlockSpec((1, w), lambda hp, b: (0, 0)),
                  pl.BlockSpec((None,) + bias.shape[1:], lambda hp, b: (hp, 0, 0, 0))],
        out_specs=pl.BlockSpec((None, t, w), lambda hp, b: (b, 0, hp)),
        out_shape=jax.ShapeDtypeStruct((bsz, t, d), BF16),
        scratch_shapes=[pltpu.VMEM((t, w), BF16)] * 2
        + [pltpu.VMEM((NA_SLOTS, 2 * GRID_W, n_keys), F32), pltpu.VMEM((NA_SLOTS, 2 * GRID_W, n_keys), BF16),
           pltpu.VMEM((NA_SLOTS, 2 * GRID_W, 1), F32)],
        compiler_params=_cparams("parallel", "parallel"), name="na_attention",
    )(qkv3, qkv3, qkv3, gain(q_g), gain(k_g), bias)


FFN_ROWS = 384


def _swiglu_kernel(h_ref, g_ref, shc_ref, sh_ref, scc_ref, sc_ref, w1_ref, w3_ref, w2_ref, gtc_ref, gt_ref,
                   o_ref, *, n_ctx, tiles_per_batch):
    tile_in_batch = pl.program_id(0) % tiles_per_batch
    x = _modulated_rows(h_ref, g_ref, shc_ref, sh_ref, scc_ref, sc_ref, tile_in_batch, n_ctx).astype(BF16)
    hid = _silu(_dot(x, w1_ref[...])) * _dot(x, w3_ref[...])
    y = _dot(hid.astype(BF16), w2_ref[...])
    o_ref[...] = h_ref[...] + _row_gate(gtc_ref, gt_ref, y.shape, tile_in_batch, n_ctx) * y


def _resident(block_shape, index_map):
    return pl.BlockSpec(block_shape, index_map, pipeline_mode=pl.Buffered(1))


def _swiglu_residual(h, g, w1, w3, w2, j, mods, n_batch, t, n_ctx):
    n, d = h.shape
    dff = w1.shape[2]
    tm = FFN_ROWS
    tpb = t // tm
    assert t % tm == 0 and n_ctx <= tm
    kern = functools.partial(_swiglu_kernel, n_ctx=n_ctx, tiles_per_batch=tpb)
    return pl.pallas_call(
        kern, grid=(n // tm,),
        in_specs=[pl.BlockSpec((tm, d), lambda i: (i, 0)),
                  pl.BlockSpec((1, d), lambda i: (0, 0))]
        + _stream_specs(d, n_batch, tpb, 3, 1) + _stream_specs(d, n_batch, tpb, 4, 1)
        + [_resident((None, d, dff), lambda i: (j, 0, 0)),
           _resident((None, d, dff), lambda i: (j, 0, 0)),
           _resident((None, dff, d), lambda i: (j, 0, 0))] + _stream_specs(d, n_batch, tpb, 5, 1),
        out_specs=pl.BlockSpec((tm, d), lambda i: (i, 0)),
        out_shape=jax.ShapeDtypeStruct((n, d), F32),
        compiler_params=_cparams("parallel"), name="swiglu",
    )(h, g, mods, mods, mods, mods, w1, w3, w2, mods, mods)


MOE_ROWS = 512


def _route_tables(route_t, counts, tm, n_exp):
    r = route_t.shape[1]
    experts = route_t[0:2].astype(jnp.int32)
    ranks = route_t[4:6].astype(jnp.int32)
    padded = ((counts + tm - 1) // tm) * tm
    ends = jnp.cumsum(padded)
    starts = ends - padded
    slot = ranks
    for e in range(n_exp):
        slot = slot + jnp.where(experts == e, starts[e], 0)
    p_slots = 2 * r + n_exp * tm
    n_tiles = p_slots // tm
    n_used = ends[-1] // tm
    tile = jnp.arange(n_tiles, dtype=jnp.int32)
    te = jnp.sum(((tile * tm)[:, None] >= ends[None, :]).astype(jnp.int32), axis=1)
    te = jnp.where(tile < n_used, te, te[n_used - 1])
    return slot, p_slots, te, n_used.reshape(1)


def _gather_rows(table, idx):
    p = idx.shape[0]
    m = table.shape[1]
    mesh = plsc.VectorSubcoreMesh(core_axis_name="c", subcore_axis_name="s")

    @functools.partial(pl.kernel, out_type=jax.ShapeDtypeStruct((p, m), table.dtype), mesh=mesh)
    def gather(table_hbm, idx_hbm, out_hbm):
        def body(idx_vmem, out_vmem):
            pltpu.sync_copy(table_hbm.at[idx_vmem.at[0]], out_vmem)

        pltpu.emit_pipeline(
            body, grid=(p // SC_WINDOW,),
            in_specs=[pl.BlockSpec((1, SC_WINDOW), lambda i: (0, i))],
            out_specs=[pl.BlockSpec((SC_WINDOW, m), lambda i: (i, 0))],
            core_axis_name=("c", "s"), dimension_semantics=(pltpu.PARALLEL,),
        )(idx_hbm, out_hbm)

    return gather(table, idx.reshape(1, p))


def _scatter_rows(rows, idx_a, idx_b, n_out):
    q, m = rows.shape
    mesh = plsc.VectorSubcoreMesh(core_axis_name="c", subcore_axis_name="s")

    @functools.partial(pl.kernel, out_type=jax.ShapeDtypeStruct((n_out, m), rows.dtype), mesh=mesh)
    def scatter(rows_hbm, ia_hbm, ib_hbm, out_hbm):
        def body(rows_vmem, ia_vmem, ib_vmem):
            pltpu.sync_copy(rows_vmem, out_hbm.at[ia_vmem.at[0]])
            pltpu.sync_copy(rows_vmem, out_hbm.at[ib_vmem.at[0]])

        pltpu.emit_pipeline(
            body, grid=(q // SC_WINDOW,),
            in_specs=[pl.BlockSpec((SC_WINDOW, m), lambda i: (i, 0)),
                      pl.BlockSpec((1, SC_WINDOW), lambda i: (0, i)),
                      pl.BlockSpec((1, SC_WINDOW), lambda i: (0, i))],
            out_specs=[],
            core_axis_name=("c", "s"), dimension_semantics=(pltpu.PARALLEL,),
        )(rows_hbm, ia_hbm, ib_hbm)

    return scatter(rows, idx_a.reshape(1, q), idx_b.reshape(1, q))


def _scatter_pieces(pieces, slot, n_slots):
    npc, r, m = pieces.shape
    offs = jnp.arange(npc, dtype=jnp.int32)[:, None] * n_slots
    idx_a = (slot[0][None, :] + offs).reshape(-1)
    idx_b = (slot[1][None, :] + offs).reshape(-1)
    return _scatter_rows(pieces.reshape(npc * r, m), idx_a, idx_b, npc * n_slots).reshape(npc, n_slots, m)


def _gather_pieces(pieces, idx):
    npc, v, m = pieces.shape
    offs = (jnp.arange(npc, dtype=jnp.int32) * v).reshape((npc,) + (1,) * idx.ndim)
    flat = (idx[None] + offs).reshape(-1)
    return _gather_rows(pieces.reshape(npc * v, m), flat).reshape((npc,) + idx.shape + (m,))


def _moe_ffn_kernel(te_ref, nu_ref, x_ref, w1_ref, w3_ref, w2_ref, y_ref):
    @pl.when(pl.program_id(0) < nu_ref[0])
    def _():
        x = _load_unpacked(x_ref).astype(BF16)
        hid = _silu(_dot(x, w1_ref[...])) * _dot(x, w3_ref[...])
        _store_pieces(y_ref, _pack_halves(_dot(hid.astype(BF16), w2_ref[...])))


def _moe_ffn(xs, te, n_used, w1, w3, w2, j):
    npc, p, m = xs.shape
    d = 2 * npc * m
    dff = w1.shape[3]
    tm = MOE_ROWS

    def row(i, te_ref, nu_ref):
        return (0, jnp.minimum(i, nu_ref[0] - 1), 0)

    def expert(i, te_ref, nu_ref):
        return (j, te_ref[i], 0, 0)

    grid_spec = pltpu.PrefetchScalarGridSpec(
        num_scalar_prefetch=2, grid=(p // tm,),
        in_specs=[pl.BlockSpec((npc, tm, m), row),
                  pl.BlockSpec((None, None, d, dff), expert),
                  pl.BlockSpec((None, None, d, dff), expert),
                  pl.BlockSpec((None, None, dff, d), expert)],
        out_specs=pl.BlockSpec((npc, tm, m), row))
    return pl.pallas_call(
        _moe_ffn_kernel, grid_spec=grid_spec,
        out_shape=jax.ShapeDtypeStruct((npc, p, m), jnp.int32),
        compiler_params=_cparams("arbitrary"), name="moe_ffn",
    )(te, n_used, xs, w1, w3, w2)


def _moe_combine_kernel(ya_ref, yb_ref, route_ref, h_ref, gt_ref, o_ref):
    r = route_ref[...]
    y = r[:, 2:3] * _load_unpacked(ya_ref) + r[:, 3:4] * _load_unpacked(yb_ref)
    o_ref[...] = h_ref[...] + gt_ref[...] * y


def _moe_combine(yg, route, h, mods, n_batch, tiles_per_batch, ctx_tiles):
    d = h.shape[1]
    npc, _, r, m = yg.shape
    tm = TOKEN_TILE
    per = tiles_per_batch - ctx_tiles
    h_idx = lambda i: ((i // per) * tiles_per_batch + ctx_tiles + i % per, 0)
    if ctx_tiles == 0:
        mod_idx = _mod_row_spec(5, tiles_per_batch, n_batch)
    else:
        mod_idx = lambda i: (i // per, 5, 0, 0)
    return pl.pallas_call(
        _moe_combine_kernel, grid=(r // tm,),
        in_specs=[pl.BlockSpec((npc, None, tm, m), lambda i: (0, 0, i, 0)),
                  pl.BlockSpec((npc, None, tm, m), lambda i: (0, 1, i, 0)),
                  pl.BlockSpec((tm, LANES), lambda i: (i, 0)),
                  pl.BlockSpec((tm, d), h_idx),
                  pl.BlockSpec((None, None, 1, d), mod_idx)],
        out_specs=pl.BlockSpec((tm, d), lambda i: (i, 0)),
        out_shape=jax.ShapeDtypeStruct((r, d), F32),
        compiler_params=_cparams("parallel"), name="moe_combine",
    )(yg, yg, route, h, mods)


def _moe_residual(h, g, w_router, w1, w3, w2, j, mods, n_batch, t, n_ctx, latent_only):
    tpb = t // TOKEN_TILE
    ctx_tiles = n_ctx // TOKEN_TILE if latent_only else 0
    u_pieces, route, route_t, counts = _modulate_router(h, g, mods, 3, n_batch, tpb, ctx_tiles, w_router)
    if latent_only:
        npc, _, m = u_pieces.shape
        u_pieces = u_pieces.reshape(npc, n_batch, t, m)[:, :, n_ctx:].reshape(npc, -1, m)
        route = route.reshape(n_batch, t, LANES)[:, n_ctx:].reshape(-1, LANES)
        route_t = route_t.reshape(-1, n_batch, t)[:, :, n_ctx:].reshape(route_t.shape[0], -1)
    n_exp = w1.shape[1]
    slot, n_slots, te, n_used = _route_tables(route_t, counts[0, :n_exp].astype(jnp.int32), MOE_ROWS, n_exp)
    xs = _scatter_pieces(u_pieces, slot, n_slots)
    ys = _moe_ffn(xs, te, n_used, w1, w3, w2, j)
    yg = _gather_pieces(ys, slot)
    return _moe_combine(yg, route, h, mods, n_batch, tpb, ctx_tiles)


def kernel(x, c, ctx, c_ctx, w_mod, b_mod, norm_mix_g, norm_ffn_g, ev_w_in, ev_w_out, hg_lb_logits, hg_onorm_g, rg_conv_w, rg_conv_b, rg_wa, rg_ba, rg_wx, rg_bx, rg_lambda, na_w_qkv, na_w_o, na_q_g, na_k_g, na_rpb, ffn_w1, ffn_w3, ffn_w2, moe_router, moe_w1, moe_w3, moe_w2):
    bsz, seq, d = x.shape
    n_ctx = ctx.shape[1]
    t = n_ctx + seq
    depth = w_mod.shape[0]
    assert n_ctx % TOKEN_TILE == 0 and seq % TOKEN_TILE == 0 and t % ROW_TILE == 0 and n_ctx <= ROW_TILE

    rows = -(-(bsz + 1) // 8) * 8
    cond = jnp.zeros((rows, d), F32).at[:bsz].set(c).at[bsz].set(c_ctx)
    mods_all = _adaln(cond, w_mod, b_mod).reshape(depth, rows, N_MOD, 1, d)

    lb_p = jax.nn.softmax(hg_lb_logits.astype(F32), axis=0)
    lb_all = jnp.cumsum(lb_p, axis=0) - lb_p[0]

    bf = lambda w: w.astype(BF16)
    w_in, w_out, w_qkv, w_o = bf(ev_w_in), bf(ev_w_out), bf(na_w_qkv), bf(na_w_o)
    f_w1, f_w3, f_w2 = bf(ffn_w1), bf(ffn_w3), bf(ffn_w2)
    m_w1, m_w3, m_w2 = bf(moe_w1), bf(moe_w3), bf(moe_w2)

    h = jnp.concatenate([ctx, x], axis=1).reshape(bsz * t, d)
    for layer in range(depth):
        j = layer // 2
        mods = mods_all[layer]
        g_mix = norm_mix_g[layer].reshape(1, d)
        if layer % 2 == 0:
            half = hg_lb_logits.shape[-1]
            hg_slabs = 4 * half // LANES
            p_hg, p = _modulated_matmul(h, g_mix, mods, w_in, j, F32, hg_slabs, ROW_TILE // 2, bsz, t, n_ctx)
            hg = _hgrn2(p_hg.reshape(hg_slabs, bsz, t, LANES), lb_all[j], n_ctx)
            rg = _rglru(p.reshape(bsz, t, -1), half // (d - half), rg_conv_w[j], rg_conv_b[j], rg_wa[j], rg_ba[j],
                        rg_wx[j], rg_bx[j], rg_lambda[j], n_ctx)
            h = _merge_out(hg.reshape(bsz * t, half), rg.reshape(bsz * t, d - half), p, hg_onorm_g[j],
                           w_out, j, h, mods, bsz, t, n_ctx)
        else:
            qkv_slabs = 3 * d // LANES
            qkv, = _modulated_matmul(h, g_mix, mods, w_qkv, j, BF16, qkv_slabs, ROW_TILE, bsz, t, n_ctx)
            o = _na_attention(qkv.reshape(qkv_slabs, bsz, t, LANES), na_q_g[j], na_k_g[j], na_rpb[j], n_ctx)
            h = _matmul_residual(o.reshape(bsz * t, d), w_o, j, h, mods, 2, bsz, t, n_ctx)
        if layer % 2 == 0:
            h = _swiglu_residual(h, norm_ffn_g[layer].reshape(1, d), f_w1, f_w3, f_w2, j, mods, bsz, t, n_ctx)
        else:
            h = _moe_residual(h, norm_ffn_g[layer].reshape(1, d), moe_router[j], m_w1, m_w3, m_w2, j, mods,
                              bsz, t, n_ctx, layer == depth - 1)
    if depth % 2 == 0:
        return h.reshape(bsz, seq, d)
    return h.reshape(bsz, t, d)[:, n_ctx:]
```

```python
import functools

import numpy as np
import jax
import jax.numpy as jnp
from jax import lax
from jax.experimental import pallas as pl
from jax.experimental.pallas import tpu as pltpu
from jax.experimental.pallas import tpu_sc as plsc

F32 = jnp.float32
BF16 = jnp.bfloat16

EPS = 1e-6
NEG_BIG = -1e9
LB_MIN = 1e-30
LOG2_E = 1.4426950408889634
N_MOD = 6
GRID_W = 64
HG_HEAD_DIM = 128
RG_BLOCKS = 8
RG_CONV = 4
RG_C = 8.0
NA_HEADS = 16
NA_KR = 8
NA_KC = 16
N_EXPERTS = 8

LANES = 128
SC_WINDOW = 128
SC_PIECES = 2
TOKEN_TILE = 256
HG_CHUNK = 128
HG_LEVELS = 7
HG_SLOTS = 2
NA_SLOTS = 4
RG_SCAN_BLOCK = 256
RG_GROUP = 8
VMEM_LIMIT = 56 * 1024 * 1024


def _cparams(*sem):
    return pltpu.CompilerParams(dimension_semantics=sem, vmem_limit_bytes=VMEM_LIMIT)


def _dot(a, b):
    return jnp.dot(a, b, preferred_element_type=F32)


def _dot_nt(a, b):
    return lax.dot_general(a, b, (((1,), (1,)), ((), ())), preferred_element_type=F32)


def _dot_tn(a, b):
    return lax.dot_general(a, b, (((0,), (0,)), ((), ())), preferred_element_type=F32)


def _split2(x):
    hi = x.astype(BF16)
    lo = (x - hi.astype(F32)).astype(BF16)
    return hi, lo


def _sigmoid(x):
    return 0.5 * jnp.tanh(0.5 * x) + 0.5


def _silu(x):
    return x * _sigmoid(x)


def _softplus_neg_abs(x):
    return jnp.log(1.0 + jnp.exp(-jnp.abs(x)))


def _log_sigmoid(x):
    return jnp.minimum(x, 0.0) - _softplus_neg_abs(x)


def _gelu_tanh(x):
    return 0.5 * x * (1.0 + jnp.tanh(np.sqrt(2.0 / np.pi) * (x + 0.044715 * (x * x * x))))


def _adaln_kernel(c_ref, w_ref, b_ref, o_ref):
    a_hi, a_lo = _split2(_silu(c_ref[...]))
    w_hi, w_lo = _split2(w_ref[...])
    o_ref[...] = _dot(a_hi, w_hi) + _dot(a_hi, w_lo) + _dot(a_lo, w_hi) + b_ref[...]


def _adaln(cond, w_mod, b_mod):
    depth, d, n = w_mod.shape
    r = cond.shape[0]
    tn = 1536
    return pl.pallas_call(
        _adaln_kernel,
        grid=(depth, n // tn),
        in_specs=[pl.BlockSpec((r, d), lambda l, j: (0, 0)),
                  pl.BlockSpec((None, d, tn), lambda l, j: (l, 0, j)),
                  pl.BlockSpec((None, 1, tn), lambda l, j: (l, 0, j))],
        out_specs=pl.BlockSpec((None, r, tn), lambda l, j: (l, 0, j)),
        out_shape=jax.ShapeDtypeStruct((depth, r, n), F32),
        compiler_params=_cparams("parallel", "parallel"),
        name="adaln",
    )(cond, w_mod, b_mod.reshape(depth, 1, n))


def _mod_row_spec(k, tiles_per_batch, n_batch):
    def idx(i):
        row = jnp.where(i % tiles_per_batch == 0, n_batch, i // tiles_per_batch)
        return (row, k, 0, 0)
    return idx


def _modulated(h_ref, g_ref, sh_ref, sc_ref):
    x = h_ref[...]
    y = x * lax.rsqrt(jnp.mean(x * x, axis=-1, keepdims=True) + EPS)
    return (y * g_ref[...]) * (1.0 + sc_ref[...]) + sh_ref[...]


def _pack_halves(x):
    m = x.shape[1] // 2
    lo = lax.bitcast_convert_type(x[:, :m].astype(BF16).astype(F32), jnp.int32)
    hi = lax.bitcast_convert_type(x[:, m:].astype(BF16).astype(F32), jnp.int32)
    return lax.shift_right_logical(lo, 16) | (hi & jnp.int32(-65536))


def _unpack_halves(w):
    lo = lax.bitcast_convert_type(lax.shift_left(w, 16), F32)
    hi = lax.bitcast_convert_type(w & jnp.int32(-65536), F32)
    return lo, hi


def _store_pieces(ref, w):
    m = w.shape[1] // SC_PIECES
    for j in range(SC_PIECES):
        ref[j] = w[:, j * m:(j + 1) * m]


def _load_unpacked(ref):
    parts = [_unpack_halves(ref[j]) for j in range(SC_PIECES)]
    return jnp.concatenate([lo for lo, _ in parts] + [hi for _, hi in parts], axis=1)


def _modulate_router_kernel(h_ref, g_ref, sh_ref, sc_ref, wr_ref, u_ref, route_ref, route_t_ref, cnt_ref,
                            *, tiles_per_batch, skip_tiles):
    i = pl.program_id(0)

    @pl.when(i == 0)
    def _():
        cnt_ref[...] = jnp.zeros_like(cnt_ref)

    u = _modulated(h_ref, g_ref, sh_ref, sc_ref)
    _store_pieces(u_ref, _pack_halves(u))
    u0, u1 = _split2(u)
    w0, w1 = _split2(wr_ref[...])
    logits = _dot(u0, w0) + _dot(u0, w1) + _dot(u1, w0)
    lane = lax.broadcasted_iota(jnp.int32, logits.shape, 1).astype(F32)
    lg = jnp.where(lane < N_EXPERTS, logits, -jnp.inf)
    m1 = jnp.max(lg, axis=-1, keepdims=True)
    i1 = jnp.min(jnp.where(lg == m1, lane, float(LANES)), axis=-1, keepdims=True)
    lg2 = jnp.where(lane == i1, -jnp.inf, lg)
    m2 = jnp.max(lg2, axis=-1, keepdims=True)
    i2 = jnp.min(jnp.where(lg2 == m2, lane, float(LANES)), axis=-1, keepdims=True)
    e2 = jnp.exp(m2 - m1)
    den = 1.0 + e2

    routed = jnp.where(i % tiles_per_batch >= skip_tiles, 1.0, 0.0)
    pick_a = jnp.where(lane == i1, routed, 0.0)
    pick_b = jnp.where(lane == i2, routed, 0.0)
    tm = pick_a.shape[0]
    earlier = jnp.where(lax.broadcasted_iota(jnp.int32, (tm, tm), 1) < lax.broadcasted_iota(jnp.int32, (tm, tm), 0),
                        1.0, 0.0).astype(BF16)
    total_a = jnp.sum(pick_a, axis=0, keepdims=True)
    base = cnt_ref[0:1, :]
    rank_a = jnp.sum(pick_a * (base + _dot(earlier, pick_a.astype(BF16))), axis=-1, keepdims=True)
    rank_b = jnp.sum(pick_b * (base + total_a + _dot(earlier, pick_b.astype(BF16))), axis=-1, keepdims=True)
    cnt_ref[0:1, :] = base + total_a + jnp.sum(pick_b, axis=0, keepdims=True)

    route = jnp.where(lane == 0, i1, jnp.where(lane == 1, i2, jnp.where(lane == 2, 1.0 / den, jnp.where(
        lane == 3, e2 / den, jnp.where(lane == 4, rank_a, rank_b)))))
    route_ref[...] = route
    route_t_ref[...] = route.T[0:route_t_ref.shape[0], :]


def _modulate_router(h, g, mods, k_shift, n_batch, tiles_per_batch, skip_tiles, w_router):
    n, d = h.shape
    tm = TOKEN_TILE
    row = pl.BlockSpec((tm, d), lambda i: (i, 0))
    vec = pl.BlockSpec((1, d), lambda i: (0, 0))

    def mod_spec(k):
        return pl.BlockSpec((None, None, 1, d), _mod_row_spec(k, tiles_per_batch, n_batch))

    in_specs = [row, vec, mod_spec(k_shift), mod_spec(k_shift + 1)]
    wr = jnp.zeros((d, LANES), F32).at[:, :N_EXPERTS].set(w_router)
    kern = functools.partial(_modulate_router_kernel, tiles_per_batch=tiles_per_batch, skip_tiles=skip_tiles)
    return pl.pallas_call(
        kern, grid=(n // tm,),
        in_specs=in_specs + [pl.BlockSpec((d, LANES), lambda i: (0, 0))],
        out_specs=[pl.BlockSpec((SC_PIECES, tm, d // 2 // SC_PIECES), lambda i: (0, i, 0)),
                   pl.BlockSpec((tm, LANES), lambda i: (i, 0)),
                   pl.BlockSpec((8, tm), lambda i: (0, i)),
                   pl.BlockSpec((8, LANES), lambda i: (0, 0))],
        out_shape=[jax.ShapeDtypeStruct((SC_PIECES, n, d // 2 // SC_PIECES), jnp.int32),
                   jax.ShapeDtypeStruct((n, LANES), F32),
                   jax.ShapeDtypeStruct((8, n), F32),
                   jax.ShapeDtypeStruct((8, LANES), F32)],
        compiler_params=_cparams("arbitrary"), name="modulate_router",
    )(h, g, mods, mods, wr)


ROW_TILE = 768


def _row_gate(ctx_ref, own_ref, shape, tile_in_batch, n_ctx):
    rows = lax.broadcasted_iota(jnp.int32, shape, 0)
    is_ctx = (rows < n_ctx) & (tile_in_batch == 0)
    return jnp.where(is_ctx, ctx_ref[...], own_ref[...])


def _modulated_rows(h_ref, g_ref, shc_ref, sh_ref, scc_ref, sc_ref, tile_in_batch, n_ctx):
    x = h_ref[...]
    y = x * lax.rsqrt(jnp.mean(x * x, axis=-1, keepdims=True) + EPS)
    shift = _row_gate(shc_ref, sh_ref, x.shape, tile_in_batch, n_ctx)
    scale = _row_gate(scc_ref, sc_ref, x.shape, tile_in_batch, n_ctx)
    return (y * g_ref[...]) * (1.0 + scale) + shift


def _stream_specs(d, n_batch, tiles_per_batch, k, nargs):
    if nargs == 1:
        ctx = lambda i: (n_batch, k, 0, 0)
        own = lambda i: (i // tiles_per_batch, k, 0, 0)
    else:
        ctx = lambda i, f: (n_batch, k, 0, 0)
        own = lambda i, f: (i // tiles_per_batch, k, 0, 0)
    return [pl.BlockSpec((None, None, 1, d), ctx), pl.BlockSpec((None, None, 1, d), own)]


def _mod_mm_kernel(h_ref, g_ref, shc_ref, sh_ref, scc_ref, sc_ref, w_ref, o_ref, *, n_ctx, tiles_per_batch):
    x = _modulated_rows(h_ref, g_ref, shc_ref, sh_ref, scc_ref, sc_ref,
                        pl.program_id(0) % tiles_per_batch, n_ctx).astype(BF16)
    o_ref[...] = _dot(x, w_ref[...]).astype(o_ref.dtype)


def _modulated_matmul(h, g, mods, w, j, out_dtype, tm, n_batch, t, n_ctx):
    n, d = h.shape
    nout = w.shape[2]
    tpb = t // tm
    assert t % tm == 0 and n_ctx <= tm
    kern = functools.partial(_mod_mm_kernel, n_ctx=n_ctx, tiles_per_batch=tpb)
    return pl.pallas_call(
        kern, grid=(n // tm,),
        in_specs=[pl.BlockSpec((tm, d), lambda i: (i, 0)),
                  pl.BlockSpec((1, d), lambda i: (0, 0))]
        + _stream_specs(d, n_batch, tpb, 0, 1) + _stream_specs(d, n_batch, tpb, 1, 1)
        + [pl.BlockSpec((None, d, nout), lambda i: (j, 0, 0))],
        out_specs=pl.BlockSpec((tm, nout), lambda i: (i, 0)),
        out_shape=jax.ShapeDtypeStruct((n, nout), out_dtype),
        compiler_params=_cparams("parallel"), name="modulated_matmul",
    )(h, g, mods, mods, mods, mods, w)


def _mm_res_kernel(x_ref, w_ref, h_ref, gtc_ref, gt_ref, o_ref, *, n_ctx, tiles_per_batch):
    gate = _row_gate(gtc_ref, gt_ref, h_ref.shape, pl.program_id(0) % tiles_per_batch, n_ctx)
    o_ref[...] = h_ref[...] + gate * _dot(x_ref[...], w_ref[...])


def _matmul_residual(x, w, j, h, mods, k_gate, n_batch, t, n_ctx):
    n, d = h.shape
    kdim = x.shape[1]
    tm = ROW_TILE
    tpb = t // tm
    kern = functools.partial(_mm_res_kernel, n_ctx=n_ctx, tiles_per_batch=tpb)
    return pl.pallas_call(
        kern, grid=(n // tm,),
        in_specs=[pl.BlockSpec((tm, kdim), lambda i: (i, 0)),
                  pl.BlockSpec((None, kdim, d), lambda i: (j, 0, 0)),
                  pl.BlockSpec((tm, d), lambda i: (i, 0))] + _stream_specs(d, n_batch, tpb, k_gate, 1),
        out_specs=pl.BlockSpec((tm, d), lambda i: (i, 0)),
        out_shape=jax.ShapeDtypeStruct((n, d), F32),
        compiler_params=_cparams("parallel"), name="matmul_residual",
    )(x, w, h, mods, mods)


def _hg_tables():
    c = HG_CHUNK
    t = np.arange(c)[:, None]
    u = np.arange(c)[None, :]
    mats = [(u <= t), (u > t)]
    for k in range(1, HG_LEVELS + 1):
        m = ((t >> k) << k) + (1 << (k - 1)) - 1
        upper = t > m
        mats.append(np.where(upper, (u > m) & (u <= t), (u > t) & (u <= m)))
    fwd = np.concatenate(mats, axis=0).astype(np.float32)
    bwd = fwd.reshape(-1, c, c)[:, ::-1, ::-1].reshape(-1, c)
    x = t ^ u
    lvl = np.where(t > u, np.floor(np.log2(np.maximum(x, 1))).astype(np.int32) + 1, -1)
    lvl = np.where(t == u, 0, lvl).astype(np.int32)
    return np.stack([fwd, bwd]), np.stack([lvl, lvl.T])


def _hg_kernel(q_ref, ff_ref, fb_ref, v_ref, lb_ref, m_ref, lvl_ref, o_ref, of_ref, ob_ref,
               qk_ref, el_ref, io_ref, tot_ref, att_ref, *, n_ctx_chunks, n_chunks):
    c = HG_CHUNK
    gap = HG_SLOTS // 2
    t = o_ref.shape[0]
    z_refs = (ff_ref, fb_ref)
    out_refs = (of_ref, ob_ref)
    for ref in (qk_ref, el_ref, io_ref, tot_ref, att_ref):
        ref[...] = jnp.zeros(ref.shape, ref.dtype)

    def start_row(d, j):
        j = jnp.clip(j, 0, n_chunks - 1)
        if d == 1:
            j = jnp.where(j < n_ctx_chunks, n_ctx_chunks - 1 - j, n_chunks - 1 - (j - n_ctx_chunks))
        return pl.multiple_of(j * c, c)

    def stage1(d, j, slot):
        t0 = start_row(d, j)
        lb = lb_ref[d:d + 1, :]
        log_lb = jnp.log(jnp.maximum(lb, LB_MIN))
        q = _silu(q_ref[pl.ds(t0, c), :])
        z = z_refs[d][pl.ds(t0, c), :]
        k = (1.0 - lb) * _sigmoid(-z)
        b = jnp.log1p(-lb) + _log_sigmoid(z)
        logf = jnp.maximum(log_lb, b) + _softplus_neg_abs(log_lb - b)
        g_hi, g_lo = _split2(logf * LOG2_E)
        m_ends = m_ref[d, 0:2 * c, :]
        e_ends = jnp.exp2(_dot(m_ends, g_hi) + _dot(m_ends, g_lo))
        el_ref[d, slot] = jnp.exp2(_dot(m_ref[d, 2 * c:, :], g_hi).astype(BF16))
        qk_ref[d, slot, 0] = q.astype(BF16)
        qk_ref[d, slot, 1] = k.astype(BF16)
        io_ref[d, slot, 0] = (q * e_ends[0:c]).astype(BF16)
        io_ref[d, slot, 1] = (k * e_ends[c:]).astype(BF16)
        tot_row = c - 1 if d == 0 else 0
        tot_ref[d, slot, 0:1, :] = e_ends[tot_row:tot_row + 1, :]

    def stage2(d, slot):
        qb = qk_ref[d, slot, 0]
        kb = qk_ref[d, slot, 1]
        lvl = lvl_ref[d]
        att = jnp.where(lvl == 0, _dot_nt(qb, kb), 0.0)
        for lev in range(1, HG_LEVELS + 1):
            el = el_ref[d, slot, (lev - 1) * c:lev * c, :]
            att = jnp.where(lvl == lev, _dot_nt(qb * el, kb * el), att)
        att_ref[d, slot] = att.astype(BF16)

    def stage3(d, j, slot, st):
        t0 = start_row(d, j)
        v = v_ref[pl.ds(t0, c), :].astype(BF16)
        row = jnp.where(j < n_chunks, t0, t)
        out_refs[d][pl.ds(pl.multiple_of(row, c), c), :] = (
            _dot(att_ref[d, slot], v) + _dot_nt(io_ref[d, slot, 0], st.astype(BF16)))
        return st * tot_ref[d, slot, 0:1, :] + _dot_tn(v, io_ref[d, slot, 1])

    def steps(jj, carry):
        st = list(carry)
        for slot in range(HG_SLOTS):
            i = HG_SLOTS * jj + slot
            for d in range(2):
                st[d] = stage3(d, i - 2 * gap, slot, st[d])
            for d in range(2):
                stage2(d, (slot + gap) % HG_SLOTS)
            for d in range(2):
                stage1(d, i, slot)
        return tuple(st)

    zero = jnp.zeros((c, c), F32)
    n_steps = -(-(n_chunks + 2 * gap) // HG_SLOTS)
    lax.fori_loop(0, n_steps, steps, (zero, zero))
    o_ref[...] = (of_ref[0:t, :] + ob_ref[0:t, :]).astype(o_ref.dtype)


def _hgrn2(p3, lb, n_ctx):
    bsz, t, _ = p3.shape
    width = lb.shape[1]
    heads = width // HG_HEAD_DIM
    mats, lvl = _hg_tables()
    c = HG_CHUNK

    def col(group):
        return pl.BlockSpec((None, t, c), lambda b, h: (b, 0, group * heads + h))

    kern = functools.partial(_hg_kernel, n_ctx_chunks=n_ctx // c, n_chunks=t // c)
    return pl.pallas_call(
        kern, grid=(bsz, heads),
        in_specs=[col(0), col(1), col(2), col(3),
                  pl.BlockSpec((2, c), lambda b, h: (0, h)),
                  pl.BlockSpec(mats.shape, lambda b, h: (0, 0, 0)),
                  pl.BlockSpec(lvl.shape, lambda b, h: (0, 0, 0))],
        out_specs=pl.BlockSpec((None, t, c), lambda b, h: (b, 0, h)),
        out_shape=jax.ShapeDtypeStruct((bsz, t, width), F32),
        scratch_shapes=[pltpu.VMEM((t + c, c), F32), pltpu.VMEM((t + c, c), F32),
                        pltpu.VMEM((2, HG_SLOTS, 2, c, c), BF16), pltpu.VMEM((2, HG_SLOTS, HG_LEVELS * c, c), BF16),
                        pltpu.VMEM((2, HG_SLOTS, 2, c, c), BF16), pltpu.VMEM((2, HG_SLOTS, 8, c), F32),
                        pltpu.VMEM((2, HG_SLOTS, c, c), BF16)],
        compiler_params=_cparams("parallel", "parallel"), name="hgrn2",
    )(p3, p3, p3, p3, lb, jnp.asarray(mats, BF16), jnp.asarray(lvl))


PAD = 8


def _rg_kernel(x_ref, cw_ref, cb_ref, wa_ref, ba_ref, wx_ref, bx_ref, lam_ref, o_ref,
               xpad_ref, xc_ref, ob_ref, *, n_ctx):
    t, w = x_ref.shape
    n_lat = t - n_ctx
    lat0 = n_ctx + 2 * PAD
    xpad_ref[...] = jnp.zeros(xpad_ref.shape, F32)
    xpad_ref[PAD:PAD + n_ctx, :] = x_ref[0:n_ctx, :]
    xpad_ref[lat0:lat0 + n_lat, :] = x_ref[n_ctx:t, :]
    left = RG_CONV // 2
    for base, dst, n in ((PAD, 0, n_ctx), (lat0, n_ctx, n_lat)):
        acc = cb_ref[...] + cw_ref[0:1, :] * xpad_ref[base - left:base - left + n, :]
        for j in range(1, RG_CONV):
            acc = acc + cw_ref[j:j + 1, :] * xpad_ref[base - left + j:base - left + j + n, :]
        xc_ref[dst:dst + n, :] = acc

    blk = RG_SCAN_BLOCK
    n_blocks = t // blk
    n_ctx_blocks = n_ctx // blk
    sub = lax.broadcasted_iota(jnp.int32, (blk // RG_GROUP, RG_GROUP, w), 1)
    c_log_sig_lam = RG_C * _log_sigmoid(lam_ref[...])

    def scan_block(d, t0, hin, out_ref):
        xc = xc_ref[pl.ds(t0, blk), :]
        xb = xc.astype(BF16)
        r = _sigmoid(_dot(xb, wa_ref[d]) + ba_ref[d:d + 1, :])
        ig = _sigmoid(_dot(xb, wx_ref[d]) + bx_ref[d:d + 1, :])
        log_a = r * c_log_sig_lam[d:d + 1, :]
        a = jnp.exp(log_a)
        b = jnp.sqrt(jnp.maximum(1.0 - a * a, 0.0)) * (ig * xc)
        n_groups = blk // RG_GROUP
        a = a.reshape(n_groups, RG_GROUP, w)
        b = b.reshape(n_groups, RG_GROUP, w)
        s = 1
        while s < RG_GROUP:
            shift = s if d == 0 else RG_GROUP - s
            keep = (sub >= s) if d == 0 else (sub < RG_GROUP - s)
            b = jnp.where(keep, a * pltpu.roll(b, shift, 1), 0.0) + b
            a = jnp.where(keep, a * pltpu.roll(a, shift, 1), a)
            s *= 2
        for gi in range(n_groups):
            gj = gi if d == 0 else n_groups - 1 - gi
            hg = a[gj] * hin + b[gj]
            out_ref[pl.ds(t0 + gj * RG_GROUP, RG_GROUP), :] = hg
            hin = hg[RG_GROUP - 1:RG_GROUP, :] if d == 0 else hg[0:1, :]
        return hin

    def step(i, carry):
        h_f, h_b = carry
        bb = jnp.where(i < n_ctx_blocks, n_ctx_blocks - 1 - i, n_blocks - 1 - (i - n_ctx_blocks))
        h_f = scan_block(0, pl.multiple_of(i * blk, blk), h_f, o_ref)
        h_b = scan_block(1, pl.multiple_of(bb * blk, blk), h_b, ob_ref)
        return h_f, h_b

    zero = jnp.zeros((1, w), F32)
    lax.fori_loop(0, n_blocks, step, (zero, zero))
    o_ref[...] += ob_ref[...]


def _blockdiag_dense(w):
    nd, nb, k, _ = w.shape
    eye = jnp.eye(nb, dtype=w.dtype)
    return jnp.einsum('dnij,nm->dnimj', w, eye).reshape(nd, nb * k, nb * k)


def _rglru(p3, col_block, conv_w, conv_b, wa, ba, wx, bx, lam, n_ctx):
    bsz, t, _ = p3.shape
    w = conv_w.shape[1]
    full = lambda shape: pl.BlockSpec(shape, lambda b: (0,) * len(shape))
    kern = functools.partial(_rg_kernel, n_ctx=n_ctx)
    return pl.pallas_call(
        kern, grid=(bsz,),
        in_specs=[pl.BlockSpec((None, t, w), lambda b: (b, 0, col_block)),
                  full((RG_CONV, w)), full((1, w)),
                  full((2, w, w)), full((2, w)), full((2, w, w)), full((2, w)), full((2, w))],
        out_specs=pl.BlockSpec((None, t, w), lambda b: (b, 0, 0)),
        out_shape=jax.ShapeDtypeStruct((bsz, t, w), F32),
        scratch_shapes=[pltpu.VMEM((t + 3 * PAD, w), F32)] + [pltpu.VMEM((t, w), F32)] * 2,
        compiler_params=_cparams("parallel"), name="rglru",
    )(p3, conv_w, conv_b.reshape(1, w), _blockdiag_dense(wa).astype(BF16), ba,
      _blockdiag_dense(wx).astype(BF16), bx, lam)


def _merge_kernel(hg_ref, rg_ref, gate_ref, y_ref, og_ref, w_ref, h_ref, gtc_ref, gt_ref, o_ref,
                  *, n_ctx, tiles_per_batch):
    hg = hg_ref[...]
    half = hg.shape[1]
    hn = hg * lax.rsqrt(jnp.mean(hg * hg, axis=-1, keepdims=True) + EPS) * og_ref[...]
    a = (hn * _silu(gate_ref[...])).astype(BF16)
    b = (rg_ref[...] * _gelu_tanh(y_ref[...])).astype(BF16)
    o = _dot(a, w_ref[0:half, :]) + _dot(b, w_ref[half:, :])
    gate = _row_gate(gtc_ref, gt_ref, h_ref.shape, pl.program_id(0) % tiles_per_batch, n_ctx)
    o_ref[...] = h_ref[...] + gate * o


def _merge_out(hg, rg, p, onorm_g, w_out, j, h, mods, n_batch, t, n_ctx):
    n, d = h.shape
    half = hg.shape[1]
    tm = ROW_TILE
    tpb = t // tm
    blk = lambda c: pl.BlockSpec((tm, half), lambda i: (i, c))
    kern = functools.partial(_merge_kernel, n_ctx=n_ctx, tiles_per_batch=tpb)
    return pl.pallas_call(
        kern, grid=(n // tm,),
        in_specs=[blk(0), blk(0), blk(4), blk(6),
                  pl.BlockSpec((1, half), lambda i: (0, 0)),
                  pl.BlockSpec((None, d, d), lambda i: (j, 0, 0)),
                  pl.BlockSpec((tm, d), lambda i: (i, 0))] + _stream_specs(d, n_batch, tpb, 2, 1),
        out_specs=pl.BlockSpec((tm, d), lambda i: (i, 0)),
        out_shape=jax.ShapeDtypeStruct((n, d), F32),
        compiler_params=_cparams("parallel"), name="merge_out",
    )(hg, rg, p, p, onorm_g.reshape(1, half), w_out, h, mods, mods)


def _na_bias_table(rpb):
    q = np.arange(GRID_W)[:, None]
    kc = np.arange(GRID_W)[None, :]
    wstart = np.clip(q - NA_KC // 2, 0, GRID_W - NA_KC)
    mask = (kc >= wstart) & (kc < wstart + NA_KC)
    coff = np.clip(kc - q + NA_KC - 1, 0, 2 * NA_KC - 2)
    sel = (coff[None] == np.arange(2 * NA_KC - 1)[:, None, None]).astype(np.float32)
    toe = jnp.einsum('hdj,jqk->hqdk', rpb.astype(F32), sel, precision=lax.Precision.HIGHEST)
    toe = jnp.where(mask[None, :, None, :], toe, NEG_BIG)
    h = toe.shape[0]
    flat = toe.reshape(h // 2, 2 * GRID_W, -1)
    width = (2 * NA_KR) * GRID_W
    even = jnp.pad(flat, ((0, 0), (0, 0), (0, width - flat.shape[2])))
    odd = jnp.pad(flat[:, :, GRID_W:], ((0, 0), (0, 0), (0, width - flat.shape[2] + GRID_W)))
    return jnp.stack([even, odd], axis=1)


def _stack_heads(x, hd):
    lane = lax.broadcasted_iota(jnp.int32, x.shape, 1)
    zero = jnp.zeros_like(x)
    return jnp.concatenate([jnp.where(lane < hd, x, zero), jnp.where(lane >= hd, x, zero)], axis=0)


def _unstack_heads(o2, hd):
    n = o2.shape[0] // 2
    lane = lax.broadcasted_iota(jnp.int32, (n, o2.shape[1]), 1)
    return jnp.where(lane < hd, o2[:n], o2[n:])


def _na_kernel(q_ref, k_ref, v_ref, qg_ref, kg_ref, bias_ref, o_ref, qn_ref, kn_ref,
               s_ref, p_ref, den_ref, *, n_ctx, scale):
    t, w = q_ref.shape
    hd = w // 2
    rows_grid = (t - n_ctx) // GRID_W
    r_i = lax.broadcasted_iota(jnp.int32, (w, w), 0)
    c_i = lax.broadcasted_iota(jnp.int32, (w, w), 1)
    seg = jnp.where((r_i // hd) == (c_i // hd), 1.0 / hd, 0.0).astype(BF16)

    def normed(x, g):
        ms = _dot((x * x).astype(BF16), seg)
        return (x * lax.rsqrt(ms + EPS) * g).astype(BF16)

    qn_ref[...] = normed(q_ref[...].astype(F32), qg_ref[...] * scale)
    kn_ref[...] = normed(k_ref[...].astype(F32), kg_ref[...])
    vb_ref = v_ref

    kc = kn_ref[0:n_ctx, :]
    vc = vb_ref[0:n_ctx, :]

    s = _dot_nt(_stack_heads(qn_ref[0:n_ctx, :], hd), kc)
    p = jnp.exp(s - jnp.max(s, axis=-1, keepdims=True))
    o2 = _dot(p.astype(BF16), vc) / jnp.sum(p, axis=-1, keepdims=True)
    o_ref[0:n_ctx, :] = _unstack_heads(o2, hd).astype(o_ref.dtype)

    kr = NA_KR
    n_win = kr * GRID_W

    def window(r):
        r = jnp.clip(r, 0, rows_grid - 1)
        rs = jnp.clip(r - kr // 2, 0, rows_grid - kr)
        q0 = pl.multiple_of(n_ctx + r * GRID_W, GRID_W)
        k0 = pl.multiple_of(n_ctx + rs * GRID_W, GRID_W)
        return q0, k0, rs - r + (kr - 1)

    gap = NA_SLOTS // 2
    for slot in range(NA_SLOTS):
        s_ref[slot] = jnp.zeros(s_ref.shape[1:], F32)
        p_ref[slot] = jnp.zeros(p_ref.shape[1:], BF16)
        den_ref[slot] = jnp.ones(den_ref.shape[1:], F32)

    def step(i, slot):
        q0, k0, _ = window(i - 2 * gap)
        pr = p_ref[slot]
        o2 = (_dot(pr[:, :n_win], vb_ref[pl.ds(k0, n_win), :]) + _dot(pr[:, n_win:], vc)) / den_ref[slot]
        o_ref[pl.ds(q0, GRID_W), :] = _unstack_heads(o2, hd).astype(o_ref.dtype)

        mid = (slot + gap) % NA_SLOTS
        sc = s_ref[mid]
        pe = jnp.exp(sc - jnp.max(sc, axis=-1, keepdims=True))
        den_ref[mid] = jnp.sum(pe, axis=-1, keepdims=True)
        p_ref[mid] = pe.astype(BF16)

        q0, k0, off = window(i)
        q2 = _stack_heads(qn_ref[pl.ds(q0, GRID_W), :], hd)
        bias = bias_ref[off % 2, :, pl.ds(pl.multiple_of((off // 2) * (2 * GRID_W), 2 * GRID_W), n_win)]
        s_ref[slot, :, :n_win] = _dot_nt(q2, kn_ref[pl.ds(k0, n_win), :]) + bias
        s_ref[slot, :, n_win:] = _dot_nt(q2, kc)

    def steps(j, carry):
        for slot in range(NA_SLOTS):
            step(NA_SLOTS * j + slot, slot)
        return carry

    lax.fori_loop(0, (rows_grid + 2 * gap) // NA_SLOTS, steps, 0)


def _na_attention(qkv3, q_g, k_g, rpb, n_ctx):
    bsz, t, d3 = qkv3.shape
    d = d3 // 3
    hd = d // NA_HEADS
    w = 2 * hd
    pairs = d // w
    rows_grid = (t - n_ctx) // GRID_W
    assert w == LANES and rows_grid >= NA_KR and rows_grid % NA_SLOTS == 0
    n_keys = NA_KR * GRID_W + n_ctx
    bias = _na_bias_table(rpb)
    kern = functools.partial(_na_kernel, n_ctx=n_ctx, scale=hd ** -0.5)

    def col(group):
        return pl.BlockSpec((None, t, w), lambda hp, b: (b, 0, group * pairs + hp))

    gain = lambda g: jnp.tile(g.astype(F32), 2).reshape(1, w)
    return pl.pallas_call(
        kern, grid=(pairs, bsz),
        in_specs=[col(0), col(1), col(2),
                  pl.BlockSpec((1, w), lambda hp, b: (0, 0)),
                  pl.BlockSpec((1, w), lambda hp, b: (0, 0)),
                  pl.BlockSpec((None,) + bias.shape[1:], lambda hp, b: (hp, 0, 0, 0))],
        out_specs=pl.BlockSpec((None, t, w), lambda hp, b: (b, 0, hp)),
        out_shape=jax.ShapeDtypeStruct((bsz, t, d), BF16),
        scratch_shapes=[pltpu.VMEM((t, w), BF16)] * 2
        + [pltpu.VMEM((NA_SLOTS, 2 * GRID_W, n_keys), F32), pltpu.VMEM((NA_SLOTS, 2 * GRID_W, n_keys), BF16),
           pltpu.VMEM((NA_SLOTS, 2 * GRID_W, 1), F32)],
        compiler_params=_cparams("parallel", "parallel"), name="na_attention",
    )(qkv3, qkv3, qkv3, gain(q_g), gain(k_g), bias)


FFN_ROWS = 384


def _swiglu_kernel(h_ref, g_ref, shc_ref, sh_ref, scc_ref, sc_ref, w1_ref, w3_ref, w2_ref, gtc_ref, gt_ref,
                   o_ref, *, n_ctx, tiles_per_batch):
    tile_in_batch = pl.program_id(0) % tiles_per_batch
    x = _modulated_rows(h_ref, g_ref, shc_ref, sh_ref, scc_ref, sc_ref, tile_in_batch, n_ctx).astype(BF16)
    hid = _silu(_dot(x, w1_ref[...])) * _dot(x, w3_ref[...])
    y = _dot(hid.astype(BF16), w2_ref[...])
    o_ref[...] = h_ref[...] + _row_gate(gtc_ref, gt_ref, y.shape, tile_in_batch, n_ctx) * y


def _resident(block_shape, index_map):
    return pl.BlockSpec(block_shape, index_map, pipeline_mode=pl.Buffered(1))


def _swiglu_residual(h, g, w1, w3, w2, j, mods, n_batch, t, n_ctx):
    n, d = h.shape
    dff = w1.shape[2]
    tm = FFN_ROWS
    tpb = t // tm
    assert t % tm == 0 and n_ctx <= tm
    kern = functools.partial(_swiglu_kernel, n_ctx=n_ctx, tiles_per_batch=tpb)
    return pl.pallas_call(
        kern, grid=(n // tm,),
        in_specs=[pl.BlockSpec((tm, d), lambda i: (i, 0)),
                  pl.BlockSpec((1, d), lambda i: (0, 0))]
        + _stream_specs(d, n_batch, tpb, 3, 1) + _stream_specs(d, n_batch, tpb, 4, 1)
        + [_resident((None, d, dff), lambda i: (j, 0, 0)),
           _resident((None, d, dff), lambda i: (j, 0, 0)),
           _resident((None, dff, d), lambda i: (j, 0, 0))] + _stream_specs(d, n_batch, tpb, 5, 1),
        out_specs=pl.BlockSpec((tm, d), lambda i: (i, 0)),
        out_shape=jax.ShapeDtypeStruct((n, d), F32),
        compiler_params=_cparams("parallel"), name="swiglu",
    )(h, g, mods, mods, mods, mods, w1, w3, w2, mods, mods)


MOE_ROWS = 512


def _route_tables(route_t, counts, tm, n_exp):
    r = route_t.shape[1]
    experts = route_t[0:2].astype(jnp.int32)
    ranks = route_t[4:6].astype(jnp.int32)
    padded = ((counts + tm - 1) // tm) * tm
    ends = jnp.cumsum(padded)
    starts = ends - padded
    slot = ranks
    for e in range(n_exp):
        slot = slot + jnp.where(experts == e, starts[e], 0)
    p_slots = 2 * r + n_exp * tm
    n_tiles = p_slots // tm
    n_used = ends[-1] // tm
    tile = jnp.arange(n_tiles, dtype=jnp.int32)
    te = jnp.sum(((tile * tm)[:, None] >= ends[None, :]).astype(jnp.int32), axis=1)
    te = jnp.where(tile < n_used, te, te[n_used - 1])
    return slot, p_slots, te, n_used.reshape(1)


def _gather_rows(table, idx):
    p = idx.shape[0]
    m = table.shape[1]
    mesh = plsc.VectorSubcoreMesh(core_axis_name="c", subcore_axis_name="s")

    @functools.partial(pl.kernel, out_type=jax.ShapeDtypeStruct((p, m), table.dtype), mesh=mesh)
    def gather(table_hbm, idx_hbm, out_hbm):
        def body(idx_vmem, out_vmem):
            pltpu.sync_copy(table_hbm.at[idx_vmem.at[0]], out_vmem)

        pltpu.emit_pipeline(
            body, grid=(p // SC_WINDOW,),
            in_specs=[pl.BlockSpec((1, SC_WINDOW), lambda i: (0, i))],
            out_specs=[pl.BlockSpec((SC_WINDOW, m), lambda i: (i, 0))],
            core_axis_name=("c", "s"), dimension_semantics=(pltpu.PARALLEL,),
        )(idx_hbm, out_hbm)

    return gather(table, idx.reshape(1, p))


def _scatter_rows(rows, idx_a, idx_b, n_out):
    q, m = rows.shape
    mesh = plsc.VectorSubcoreMesh(core_axis_name="c", subcore_axis_name="s")

    @functools.partial(pl.kernel, out_type=jax.ShapeDtypeStruct((n_out, m), rows.dtype), mesh=mesh)
    def scatter(rows_hbm, ia_hbm, ib_hbm, out_hbm):
        def body(rows_vmem, ia_vmem, ib_vmem):
            pltpu.sync_copy(rows_vmem, out_hbm.at[ia_vmem.at[0]])
            pltpu.sync_copy(rows_vmem, out_hbm.at[ib_vmem.at[0]])

        pltpu.emit_pipeline(
            body, grid=(q // SC_WINDOW,),
            in_specs=[pl.BlockSpec((SC_WINDOW, m), lambda i: (i, 0)),
                      pl.BlockSpec((1, SC_WINDOW), lambda i: (0, i)),
                      pl.BlockSpec((1, SC_WINDOW), lambda i: (0, i))],
            out_specs=[],
            core_axis_name=("c", "s"), dimension_semantics=(pltpu.PARALLEL,),
        )(rows_hbm, ia_hbm, ib_hbm)

    return scatter(rows, idx_a.reshape(1, q), idx_b.reshape(1, q))


def _scatter_pieces(pieces, slot, n_slots):
    npc, r, m = pieces.shape
    offs = jnp.arange(npc, dtype=jnp.int32)[:, None] * n_slots
    idx_a = (slot[0][None, :] + offs).reshape(-1)
    idx_b = (slot[1][None, :] + offs).reshape(-1)
    return _scatter_rows(pieces.reshape(npc * r, m), idx_a, idx_b, npc * n_slots).reshape(npc, n_slots, m)


def _gather_pieces(pieces, idx):
    npc, v, m = pieces.shape
    offs = (jnp.arange(npc, dtype=jnp.int32) * v).reshape((npc,) + (1,) * idx.ndim)
    flat = (idx[None] + offs).reshape(-1)
    return _gather_rows(pieces.reshape(npc * v, m), flat).reshape((npc,) + idx.shape + (m,))


def _moe_ffn_kernel(te_ref, nu_ref, x_ref, w1_ref, w3_ref, w2_ref, y_ref):
    @pl.when(pl.program_id(0) < nu_ref[0])
    def _():
        x = _load_unpacked(x_ref).astype(BF16)
        hid = _silu(_dot(x, w1_ref[...].astype(BF16))) * _dot(x, w3_ref[...].astype(BF16))
        _store_pieces(y_ref, _pack_halves(_dot(hid.astype(BF16), w2_ref[...].astype(BF16))))


def _moe_ffn(xs, te, n_used, w1, w3, w2, j):
    npc, p, m = xs.shape
    d = 2 * npc * m
    dff = w1.shape[3]
    tm = MOE_ROWS

    def row(i, te_ref, nu_ref):
        return (0, jnp.minimum(i, nu_ref[0] - 1), 0)

    def expert(i, te_ref, nu_ref):
        return (j, te_ref[i], 0, 0)

    grid_spec = pltpu.PrefetchScalarGridSpec(
        num_scalar_prefetch=2, grid=(p // tm,),
        in_specs=[pl.BlockSpec((npc, tm, m), row),
                  _resident((None, None, d, dff), expert),
                  _resident((None, None, d, dff), expert),
                  _resident((None, None, dff, d), expert)],
        out_specs=pl.BlockSpec((npc, tm, m), row))
    return pl.pallas_call(
        _moe_ffn_kernel, grid_spec=grid_spec,
        out_shape=jax.ShapeDtypeStruct((npc, p, m), jnp.int32),
        compiler_params=_cparams("arbitrary"), name="moe_ffn",
    )(te, n_used, xs, w1, w3, w2)


def _moe_combine_kernel(ya_ref, yb_ref, route_ref, h_ref, gt_ref, o_ref):
    r = route_ref[...]
    y = r[:, 2:3] * _load_unpacked(ya_ref) + r[:, 3:4] * _load_unpacked(yb_ref)
    o_ref[...] = h_ref[...] + gt_ref[...] * y


def _moe_combine(yg, route, h, mods, n_batch, tiles_per_batch, ctx_tiles):
    d = h.shape[1]
    npc, _, r, m = yg.shape
    tm = TOKEN_TILE
    per = tiles_per_batch - ctx_tiles
    h_idx = lambda i: ((i // per) * tiles_per_batch + ctx_tiles + i % per, 0)
    if ctx_tiles == 0:
        mod_idx = _mod_row_spec(5, tiles_per_batch, n_batch)
    else:
        mod_idx = lambda i: (i // per, 5, 0, 0)
    return pl.pallas_call(
        _moe_combine_kernel, grid=(r // tm,),
        in_specs=[pl.BlockSpec((npc, None, tm, m), lambda i: (0, 0, i, 0)),
                  pl.BlockSpec((npc, None, tm, m), lambda i: (0, 1, i, 0)),
                  pl.BlockSpec((tm, LANES), lambda i: (i, 0)),
                  pl.BlockSpec((tm, d), h_idx),
                  pl.BlockSpec((None, None, 1, d), mod_idx)],
        out_specs=pl.BlockSpec((tm, d), lambda i: (i, 0)),
        out_shape=jax.ShapeDtypeStruct((r, d), F32),
        compiler_params=_cparams("parallel"), name="moe_combine",
    )(yg, yg, route, h, mods)


def _moe_residual(h, g, w_router, w1, w3, w2, j, mods, n_batch, t, n_ctx, latent_only):
    tpb = t // TOKEN_TILE
    ctx_tiles = n_ctx // TOKEN_TILE if latent_only else 0
    u_pieces, route, route_t, counts = _modulate_router(h, g, mods, 3, n_batch, tpb, ctx_tiles, w_router)
    if latent_only:
        npc, _, m = u_pieces.shape
        u_pieces = u_pieces.reshape(npc, n_batch, t, m)[:, :, n_ctx:].reshape(npc, -1, m)
        route = route.reshape(n_batch, t, LANES)[:, n_ctx:].reshape(-1, LANES)
        route_t = route_t.reshape(-1, n_batch, t)[:, :, n_ctx:].reshape(route_t.shape[0], -1)
    n_exp = w1.shape[1]
    slot, n_slots, te, n_used = _route_tables(route_t, counts[0, :n_exp].astype(jnp.int32), MOE_ROWS, n_exp)
    xs = _scatter_pieces(u_pieces, slot, n_slots)
    ys = _moe_ffn(xs, te, n_used, w1, w3, w2, j)
    yg = _gather_pieces(ys, slot)
    return _moe_combine(yg, route, h, mods, n_batch, tpb, ctx_tiles)


def kernel(x, c, ctx, c_ctx, w_mod, b_mod, norm_mix_g, norm_ffn_g, ev_w_in, ev_w_out, hg_lb_logits, hg_onorm_g, rg_conv_w, rg_conv_b, rg_wa, rg_ba, rg_wx, rg_bx, rg_lambda, na_w_qkv, na_w_o, na_q_g, na_k_g, na_rpb, ffn_w1, ffn_w3, ffn_w2, moe_router, moe_w1, moe_w3, moe_w2):
    bsz, seq, d = x.shape
    n_ctx = ctx.shape[1]
    t = n_ctx + seq
    depth = w_mod.shape[0]
    assert n_ctx % TOKEN_TILE == 0 and seq % TOKEN_TILE == 0 and t % ROW_TILE == 0 and n_ctx <= ROW_TILE

    rows = -(-(bsz + 1) // 8) * 8
    cond = jnp.zeros((rows, d), F32).at[:bsz].set(c).at[bsz].set(c_ctx)
    mods_all = _adaln(cond, w_mod, b_mod).reshape(depth, rows, N_MOD, 1, d)

    lb_p = jax.nn.softmax(hg_lb_logits.astype(F32), axis=0)
    lb_all = jnp.cumsum(lb_p, axis=0) - lb_p[0]

    bf = lambda w: w.astype(BF16)
    w_in, w_out, w_qkv, w_o = bf(ev_w_in), bf(ev_w_out), bf(na_w_qkv), bf(na_w_o)
    f_w1, f_w3, f_w2 = bf(ffn_w1), bf(ffn_w3), bf(ffn_w2)
    m_w1, m_w3, m_w2 = moe_w1, moe_w3, moe_w2

    h = jnp.concatenate([ctx, x], axis=1).reshape(bsz * t, d)
    for layer in range(depth):
        j = layer // 2
        mods = mods_all[layer]
        g_mix = norm_mix_g[layer].reshape(1, d)
        if layer % 2 == 0:
            p = _modulated_matmul(h, g_mix, mods, w_in, j, F32, ROW_TILE // 2, bsz, t, n_ctx)
            p3 = p.reshape(bsz, t, -1)
            hg = _hgrn2(p3, lb_all[j], n_ctx)
            half = hg.shape[-1]
            rg = _rglru(p3, 5 * half // (d - half), rg_conv_w[j], rg_conv_b[j], rg_wa[j], rg_ba[j],
                        rg_wx[j], rg_bx[j], rg_lambda[j], n_ctx)
            h = _merge_out(hg.reshape(bsz * t, half), rg.reshape(bsz * t, d - half), p, hg_onorm_g[j],
                           w_out, j, h, mods, bsz, t, n_ctx)
        else:
            qkv = _modulated_matmul(h, g_mix, mods, w_qkv, j, BF16, ROW_TILE, bsz, t, n_ctx)
            o = _na_attention(qkv.reshape(bsz, t, 3 * d), na_q_g[j], na_k_g[j], na_rpb[j], n_ctx)
            h = _matmul_residual(o.reshape(bsz * t, d), w_o, j, h, mods, 2, bsz, t, n_ctx)
        if layer % 2 == 0:
            h = _swiglu_residual(h, norm_ffn_g[layer].reshape(1, d), f_w1, f_w3, f_w2, j, mods, bsz, t, n_ctx)
        else:
            h = _moe_residual(h, norm_ffn_g[layer].reshape(1, d), moe_router[j], m_w1, m_w3, m_w2, j, mods,
                              bsz, t, n_ctx, layer == depth - 1)
    if depth % 2 == 0:
        return h.reshape(bsz, seq, d)
    return h.reshape(bsz, t, d)[:, n_ctx:]
```

```python
import functools

import numpy as np
import jax
import jax.numpy as jnp
from jax import lax
from jax.experimental import pallas as pl
from jax.experimental.pallas import tpu as pltpu
from jax.experimental.pallas import tpu_sc as plsc

F32 = jnp.float32
BF16 = jnp.bfloat16

EPS = 1e-6
NEG_BIG = -1e9
LB_MIN = 1e-30
LOG2_E = 1.4426950408889634
N_MOD = 6
GRID_W = 64
HG_HEAD_DIM = 128
RG_BLOCKS = 8
RG_CONV = 4
RG_C = 8.0
NA_HEADS = 16
NA_KR = 8
NA_KC = 16
N_EXPERTS = 8

LANES = 128
SC_WINDOW = 128
SC_PIECES = 2
TOKEN_TILE = 256
HG_CHUNK = 128
HG_LEVELS = 7
HG_SLOTS = 2
NA_SLOTS = 4
RG_SCAN_BLOCK = 256
RG_GROUP = 8
VMEM_LIMIT = 56 * 1024 * 1024


def _cparams(*sem):
    return pltpu.CompilerParams(dimension_semantics=sem, vmem_limit_bytes=VMEM_LIMIT)


def _resident(block_shape, index_map):
    return pl.BlockSpec(block_shape, index_map, pipeline_mode=pl.Buffered(1))


def _dot(a, b):
    return jnp.dot(a, b, preferred_element_type=F32)


def _dot_nt(a, b):
    return lax.dot_general(a, b, (((1,), (1,)), ((), ())), preferred_element_type=F32)


def _dot_tn(a, b):
    return lax.dot_general(a, b, (((0,), (0,)), ((), ())), preferred_element_type=F32)


def _split2(x):
    hi = x.astype(BF16)
    lo = (x - hi.astype(F32)).astype(BF16)
    return hi, lo


def _sigmoid(x):
    return 0.5 * jnp.tanh(0.5 * x) + 0.5


def _silu(x):
    return x * _sigmoid(x)


def _softplus_neg_abs(x):
    return jnp.log(1.0 + jnp.exp(-jnp.abs(x)))


def _log_sigmoid(x):
    return jnp.minimum(x, 0.0) - _softplus_neg_abs(x)


def _gelu_tanh(x):
    return 0.5 * x * (1.0 + jnp.tanh(np.sqrt(2.0 / np.pi) * (x + 0.044715 * (x * x * x))))


def _adaln_kernel(c_ref, w_ref, b_ref, o_ref):
    a_hi, a_lo = _split2(_silu(c_ref[...]))
    w_hi, w_lo = _split2(w_ref[...])
    o_ref[...] = _dot(a_hi, w_hi) + _dot(a_hi, w_lo) + _dot(a_lo, w_hi) + b_ref[...]


def _adaln(cond, w_mod, b_mod):
    depth, d, n = w_mod.shape
    r = cond.shape[0]
    tn = 1536
    return pl.pallas_call(
        _adaln_kernel,
        grid=(depth, n // tn),
        in_specs=[pl.BlockSpec((r, d), lambda l, j: (0, 0)),
                  pl.BlockSpec((None, d, tn), lambda l, j: (l, 0, j)),
                  pl.BlockSpec((None, 1, tn), lambda l, j: (l, 0, j))],
        out_specs=pl.BlockSpec((None, r, tn), lambda l, j: (l, 0, j)),
        out_shape=jax.ShapeDtypeStruct((depth, r, n), F32),
        compiler_params=_cparams("parallel", "parallel"),
        name="adaln",
    )(cond, w_mod, b_mod.reshape(depth, 1, n))


def _mod_row_spec(k, tiles_per_batch, n_batch):
    def idx(i):
        row = jnp.where(i % tiles_per_batch == 0, n_batch, i // tiles_per_batch)
        return (row, k, 0, 0)
    return idx


def _modulated(h_ref, g_ref, sh_ref, sc_ref):
    x = h_ref[...]
    y = x * lax.rsqrt(jnp.mean(x * x, axis=-1, keepdims=True) + EPS)
    return (y * g_ref[...]) * (1.0 + sc_ref[...]) + sh_ref[...]


def _pack_halves(x):
    m = x.shape[1] // 2
    lo = lax.bitcast_convert_type(x[:, :m].astype(BF16).astype(F32), jnp.int32)
    hi = lax.bitcast_convert_type(x[:, m:].astype(BF16).astype(F32), jnp.int32)
    return lax.shift_right_logical(lo, 16) | (hi & jnp.int32(-65536))


def _unpack_halves(w):
    lo = lax.bitcast_convert_type(lax.shift_left(w, 16), F32)
    hi = lax.bitcast_convert_type(w & jnp.int32(-65536), F32)
    return lo, hi


def _store_pieces(ref, w):
    m = w.shape[1] // SC_PIECES
    for j in range(SC_PIECES):
        ref[j] = w[:, j * m:(j + 1) * m]


def _load_unpacked(ref):
    parts = [_unpack_halves(ref[j]) for j in range(SC_PIECES)]
    return jnp.concatenate([lo for lo, _ in parts] + [hi for _, hi in parts], axis=1)


def _modulate_router_kernel(h_ref, g_ref, sh_ref, sc_ref, wr_ref, u_ref, route_ref, route_t_ref, cnt_ref,
                            *, tiles_per_batch, skip_tiles):
    i = pl.program_id(0)

    @pl.when(i == 0)
    def _():
        cnt_ref[...] = jnp.zeros_like(cnt_ref)

    u = _modulated(h_ref, g_ref, sh_ref, sc_ref)
    _store_pieces(u_ref, _pack_halves(u))
    u0, u1 = _split2(u)
    w0, w1 = _split2(wr_ref[...])
    logits = _dot(u0, w0) + _dot(u0, w1) + _dot(u1, w0)
    lane = lax.broadcasted_iota(jnp.int32, logits.shape, 1).astype(F32)
    lg = jnp.where(lane < N_EXPERTS, logits, -jnp.inf)
    m1 = jnp.max(lg, axis=-1, keepdims=True)
    i1 = jnp.min(jnp.where(lg == m1, lane, float(LANES)), axis=-1, keepdims=True)
    lg2 = jnp.where(lane == i1, -jnp.inf, lg)
    m2 = jnp.max(lg2, axis=-1, keepdims=True)
    i2 = jnp.min(jnp.where(lg2 == m2, lane, float(LANES)), axis=-1, keepdims=True)
    e2 = jnp.exp(m2 - m1)
    den = 1.0 + e2

    routed = jnp.where(i % tiles_per_batch >= skip_tiles, 1.0, 0.0)
    pick_a = jnp.where(lane == i1, routed, 0.0)
    pick_b = jnp.where(lane == i2, routed, 0.0)
    tm = pick_a.shape[0]
    earlier = jnp.where(lax.broadcasted_iota(jnp.int32, (tm, tm), 1) < lax.broadcasted_iota(jnp.int32, (tm, tm), 0),
                        1.0, 0.0).astype(BF16)
    total_a = jnp.sum(pick_a, axis=0, keepdims=True)
    base = cnt_ref[0:1, :]
    rank_a = jnp.sum(pick_a * (base + _dot(earlier, pick_a.astype(BF16))), axis=-1, keepdims=True)
    rank_b = jnp.sum(pick_b * (base + total_a + _dot(earlier, pick_b.astype(BF16))), axis=-1, keepdims=True)
    cnt_ref[0:1, :] = base + total_a + jnp.sum(pick_b, axis=0, keepdims=True)

    route = jnp.where(lane == 0, i1, jnp.where(lane == 1, i2, jnp.where(lane == 2, 1.0 / den, jnp.where(
        lane == 3, e2 / den, jnp.where(lane == 4, rank_a, rank_b)))))
    route_ref[...] = route
    route_t_ref[...] = route.T[0:route_t_ref.shape[0], :]


def _modulate_router(h, g, mods, k_shift, n_batch, tiles_per_batch, skip_tiles, w_router):
    n, d = h.shape
    tm = TOKEN_TILE
    row = pl.BlockSpec((tm, d), lambda i: (i, 0))
    vec = pl.BlockSpec((1, d), lambda i: (0, 0))

    def mod_spec(k):
        return pl.BlockSpec((None, None, 1, d), _mod_row_spec(k, tiles_per_batch, n_batch))

    in_specs = [row, vec, mod_spec(k_shift), mod_spec(k_shift + 1)]
    wr = jnp.zeros((d, LANES), F32).at[:, :N_EXPERTS].set(w_router)
    kern = functools.partial(_modulate_router_kernel, tiles_per_batch=tiles_per_batch, skip_tiles=skip_tiles)
    return pl.pallas_call(
        kern, grid=(n // tm,),
        in_specs=in_specs + [pl.BlockSpec((d, LANES), lambda i: (0, 0))],
        out_specs=[pl.BlockSpec((SC_PIECES, tm, d // 2 // SC_PIECES), lambda i: (0, i, 0)),
                   pl.BlockSpec((tm, LANES), lambda i: (i, 0)),
                   pl.BlockSpec((8, tm), lambda i: (0, i)),
                   pl.BlockSpec((8, LANES), lambda i: (0, 0))],
        out_shape=[jax.ShapeDtypeStruct((SC_PIECES, n, d // 2 // SC_PIECES), jnp.int32),
                   jax.ShapeDtypeStruct((n, LANES), F32),
                   jax.ShapeDtypeStruct((8, n), F32),
                   jax.ShapeDtypeStruct((8, LANES), F32)],
        compiler_params=_cparams("arbitrary"), name="modulate_router",
    )(h, g, mods, mods, wr)


ROW_TILE = 768


def _row_gate(ctx_ref, own_ref, shape, tile_in_batch, n_ctx):
    rows = lax.broadcasted_iota(jnp.int32, shape, 0)
    is_ctx = (rows < n_ctx) & (tile_in_batch == 0)
    return jnp.where(is_ctx, ctx_ref[...], own_ref[...])


def _modulated_rows(h_ref, g_ref, shc_ref, sh_ref, scc_ref, sc_ref, tile_in_batch, n_ctx):
    x = h_ref[...]
    y = x * lax.rsqrt(jnp.mean(x * x, axis=-1, keepdims=True) + EPS)
    shift = _row_gate(shc_ref, sh_ref, x.shape, tile_in_batch, n_ctx)
    scale = _row_gate(scc_ref, sc_ref, x.shape, tile_in_batch, n_ctx)
    return (y * g_ref[...]) * (1.0 + scale) + shift


def _stream_specs(d, n_batch, tiles_per_batch, k, nargs):
    if nargs == 1:
        ctx = lambda i: (n_batch, k, 0, 0)
        own = lambda i: (i // tiles_per_batch, k, 0, 0)
    else:
        ctx = lambda i, f: (n_batch, k, 0, 0)
        own = lambda i, f: (i // tiles_per_batch, k, 0, 0)
    return [pl.BlockSpec((None, None, 1, d), ctx), pl.BlockSpec((None, None, 1, d), own)]


def _mod_mm_kernel(h_ref, g_ref, shc_ref, sh_ref, scc_ref, sc_ref, w_ref, o_ref, *, n_ctx, tiles_per_batch):
    x = _modulated_rows(h_ref, g_ref, shc_ref, sh_ref, scc_ref, sc_ref,
                        pl.program_id(0) % tiles_per_batch, n_ctx).astype(BF16)
    o_ref[...] = _dot(x, w_ref[...].astype(BF16)).astype(o_ref.dtype)


def _modulated_matmul(h, g, mods, w, j, out_dtype, tm, n_batch, t, n_ctx):
    n, d = h.shape
    nout = w.shape[2]
    tpb = t // tm
    assert t % tm == 0 and n_ctx <= tm
    kern = functools.partial(_mod_mm_kernel, n_ctx=n_ctx, tiles_per_batch=tpb)
    return pl.pallas_call(
        kern, grid=(n // tm,),
        in_specs=[pl.BlockSpec((tm, d), lambda i: (i, 0)),
                  pl.BlockSpec((1, d), lambda i: (0, 0))]
        + _stream_specs(d, n_batch, tpb, 0, 1) + _stream_specs(d, n_batch, tpb, 1, 1)
        + [_resident((None, d, nout), lambda i: (j, 0, 0))],
        out_specs=pl.BlockSpec((tm, nout), lambda i: (i, 0)),
        out_shape=jax.ShapeDtypeStruct((n, nout), out_dtype),
        compiler_params=_cparams("parallel"), name="modulated_matmul",
    )(h, g, mods, mods, mods, mods, w)


def _mm_res_kernel(x_ref, w_ref, h_ref, gtc_ref, gt_ref, o_ref, *, n_ctx, tiles_per_batch):
    gate = _row_gate(gtc_ref, gt_ref, h_ref.shape, pl.program_id(0) % tiles_per_batch, n_ctx)
    o_ref[...] = h_ref[...] + gate * _dot(x_ref[...], w_ref[...].astype(BF16))


def _matmul_residual(x, w, j, h, mods, k_gate, n_batch, t, n_ctx):
    n, d = h.shape
    kdim = x.shape[1]
    tm = ROW_TILE
    tpb = t // tm
    kern = functools.partial(_mm_res_kernel, n_ctx=n_ctx, tiles_per_batch=tpb)
    return pl.pallas_call(
        kern, grid=(n // tm,),
        in_specs=[pl.BlockSpec((tm, kdim), lambda i: (i, 0)),
                  _resident((None, kdim, d), lambda i: (j, 0, 0)),
                  pl.BlockSpec((tm, d), lambda i: (i, 0))] + _stream_specs(d, n_batch, tpb, k_gate, 1),
        out_specs=pl.BlockSpec((tm, d), lambda i: (i, 0)),
        out_shape=jax.ShapeDtypeStruct((n, d), F32),
        compiler_params=_cparams("parallel"), name="matmul_residual",
    )(x, w, h, mods, mods)


def _hg_tables():
    c = HG_CHUNK
    t = np.arange(c)[:, None]
    u = np.arange(c)[None, :]
    mats = [(u <= t), (u > t)]
    for k in range(1, HG_LEVELS + 1):
        m = ((t >> k) << k) + (1 << (k - 1)) - 1
        upper = t > m
        mats.append(np.where(upper, (u > m) & (u <= t), (u > t) & (u <= m)))
    fwd = np.concatenate(mats, axis=0).astype(np.float32)
    bwd = fwd.reshape(-1, c, c)[:, ::-1, ::-1].reshape(-1, c)
    x = t ^ u
    lvl = np.where(t > u, np.floor(np.log2(np.maximum(x, 1))).astype(np.int32) + 1, -1)
    lvl = np.where(t == u, 0, lvl).astype(np.int32)
    return np.stack([fwd, bwd]), np.stack([lvl, lvl.T])


def _hg_kernel(q_ref, ff_ref, fb_ref, v_ref, lb_ref, m_ref, lvl_ref, o_ref, of_ref, ob_ref,
               qk_ref, el_ref, io_ref, tot_ref, att_ref, *, n_ctx_chunks, n_chunks):
    c = HG_CHUNK
    gap = HG_SLOTS // 2
    t = o_ref.shape[0]
    z_refs = (ff_ref, fb_ref)
    out_refs = (of_ref, ob_ref)
    for ref in (qk_ref, el_ref, io_ref, tot_ref, att_ref):
        ref[...] = jnp.zeros(ref.shape, ref.dtype)

    def start_row(d, j):
        j = jnp.clip(j, 0, n_chunks - 1)
        if d == 1:
            j = jnp.where(j < n_ctx_chunks, n_ctx_chunks - 1 - j, n_chunks - 1 - (j - n_ctx_chunks))
        return pl.multiple_of(j * c, c)

    def stage1(d, j, slot):
        t0 = start_row(d, j)
        lb = lb_ref[d:d + 1, :]
        log_lb = jnp.log(jnp.maximum(lb, LB_MIN))
        q = _silu(q_ref[pl.ds(t0, c), :])
        z = z_refs[d][pl.ds(t0, c), :]
        k = (1.0 - lb) * _sigmoid(-z)
        b = jnp.log1p(-lb) + _log_sigmoid(z)
        logf = jnp.maximum(log_lb, b) + _softplus_neg_abs(log_lb - b)
        g_hi, g_lo = _split2(logf * LOG2_E)
        m_ends = m_ref[d, 0:2 * c, :]
        e_ends = jnp.exp2(_dot(m_ends, g_hi) + _dot(m_ends, g_lo))
        el_ref[d, slot] = jnp.exp2(_dot(m_ref[d, 2 * c:, :], g_hi).astype(BF16))
        qk_ref[d, slot, 0] = q.astype(BF16)
        qk_ref[d, slot, 1] = k.astype(BF16)
        io_ref[d, slot, 0] = (q * e_ends[0:c]).astype(BF16)
        io_ref[d, slot, 1] = (k * e_ends[c:]).astype(BF16)
        tot_row = c - 1 if d == 0 else 0
        tot_ref[d, slot, 0:1, :] = e_ends[tot_row:tot_row + 1, :]

    def stage2(d, slot):
        qb = qk_ref[d, slot, 0]
        kb = qk_ref[d, slot, 1]
        lvl = lvl_ref[d]
        att = jnp.where(lvl == 0, _dot_nt(qb, kb), 0.0)
        for lev in range(1, HG_LEVELS + 1):
            el = el_ref[d, slot, (lev - 1) * c:lev * c, :]
            att = jnp.where(lvl == lev, _dot_nt(qb * el, kb * el), att)
        att_ref[d, slot] = att.astype(BF16)

    def stage3(d, j, slot, st):
        t0 = start_row(d, j)
        v = v_ref[pl.ds(t0, c), :].astype(BF16)
        row = jnp.where(j < n_chunks, t0, t)
        out_refs[d][pl.ds(pl.multiple_of(row, c), c), :] = (
            _dot(att_ref[d, slot], v) + _dot_nt(io_ref[d, slot, 0], st.astype(BF16)))
        return st * tot_ref[d, slot, 0:1, :] + _dot_tn(v, io_ref[d, slot, 1])

    def steps(jj, carry):
        st = list(carry)
        for slot in range(HG_SLOTS):
            i = HG_SLOTS * jj + slot
            for d in range(2):
                st[d] = stage3(d, i - 2 * gap, slot, st[d])
            for d in range(2):
                stage2(d, (slot + gap) % HG_SLOTS)
            for d in range(2):
                stage1(d, i, slot)
        return tuple(st)

    zero = jnp.zeros((c, c), F32)
    n_steps = -(-(n_chunks + 2 * gap) // HG_SLOTS)
    lax.fori_loop(0, n_steps, steps, (zero, zero))
    o_ref[...] = (of_ref[0:t, :] + ob_ref[0:t, :]).astype(o_ref.dtype)


def _hgrn2(p3, lb, n_ctx):
    bsz, t, _ = p3.shape
    width = lb.shape[1]
    heads = width // HG_HEAD_DIM
    mats, lvl = _hg_tables()
    c = HG_CHUNK

    def col(group):
        return pl.BlockSpec((None, t, c), lambda b, h: (b, 0, group * heads + h))

    kern = functools.partial(_hg_kernel, n_ctx_chunks=n_ctx // c, n_chunks=t // c)
    return pl.pallas_call(
        kern, grid=(bsz, heads),
        in_specs=[col(0), col(1), col(2), col(3),
                  pl.BlockSpec((2, c), lambda b, h: (0, h)),
                  pl.BlockSpec(mats.shape, lambda b, h: (0, 0, 0)),
                  pl.BlockSpec(lvl.shape, lambda b, h: (0, 0, 0))],
        out_specs=pl.BlockSpec((None, t, c), lambda b, h: (b, 0, h)),
        out_shape=jax.ShapeDtypeStruct((bsz, t, width), F32),
        scratch_shapes=[pltpu.VMEM((t + c, c), F32), pltpu.VMEM((t + c, c), F32),
                        pltpu.VMEM((2, HG_SLOTS, 2, c, c), BF16), pltpu.VMEM((2, HG_SLOTS, HG_LEVELS * c, c), BF16),
                        pltpu.VMEM((2, HG_SLOTS, 2, c, c), BF16), pltpu.VMEM((2, HG_SLOTS, 8, c), F32),
                        pltpu.VMEM((2, HG_SLOTS, c, c), BF16)],
        compiler_params=_cparams("parallel", "parallel"), name="hgrn2",
    )(p3, p3, p3, p3, lb, jnp.asarray(mats, BF16), jnp.asarray(lvl))


PAD = 8


def _rg_kernel(x_ref, cw_ref, cb_ref, wa_ref, ba_ref, wx_ref, bx_ref, lam_ref, o_ref,
               xpad_ref, xc_ref, ob_ref, *, n_ctx):
    t, w = x_ref.shape
    n_lat = t - n_ctx
    lat0 = n_ctx + 2 * PAD
    xpad_ref[...] = jnp.zeros(xpad_ref.shape, F32)
    xpad_ref[PAD:PAD + n_ctx, :] = x_ref[0:n_ctx, :]
    xpad_ref[lat0:lat0 + n_lat, :] = x_ref[n_ctx:t, :]
    left = RG_CONV // 2
    for base, dst, n in ((PAD, 0, n_ctx), (lat0, n_ctx, n_lat)):
        acc = cb_ref[...] + cw_ref[0:1, :] * xpad_ref[base - left:base - left + n, :]
        for j in range(1, RG_CONV):
            acc = acc + cw_ref[j:j + 1, :] * xpad_ref[base - left + j:base - left + j + n, :]
        xc_ref[dst:dst + n, :] = acc

    blk = RG_SCAN_BLOCK
    n_blocks = t // blk
    n_ctx_blocks = n_ctx // blk
    sub = lax.broadcasted_iota(jnp.int32, (blk // RG_GROUP, RG_GROUP, w), 1)
    c_log_sig_lam = RG_C * _log_sigmoid(lam_ref[...])

    def scan_block(d, t0, hin, out_ref):
        xc = xc_ref[pl.ds(t0, blk), :]
        xb = xc.astype(BF16)
        r = _sigmoid(_dot(xb, wa_ref[d]) + ba_ref[d:d + 1, :])
        ig = _sigmoid(_dot(xb, wx_ref[d]) + bx_ref[d:d + 1, :])
        log_a = r * c_log_sig_lam[d:d + 1, :]
        a = jnp.exp(log_a)
        b = jnp.sqrt(jnp.maximum(1.0 - a * a, 0.0)) * (ig * xc)
        n_groups = blk // RG_GROUP
        a = a.reshape(n_groups, RG_GROUP, w)
        b = b.reshape(n_groups, RG_GROUP, w)
        s = 1
        while s < RG_GROUP:
            shift = s if d == 0 else RG_GROUP - s
            keep = (sub >= s) if d == 0 else (sub < RG_GROUP - s)
            b = jnp.where(keep, a * pltpu.roll(b, shift, 1), 0.0) + b
            a = jnp.where(keep, a * pltpu.roll(a, shift, 1), a)
            s *= 2
        for gi in range(n_groups):
            gj = gi if d == 0 else n_groups - 1 - gi
            hg = a[gj] * hin + b[gj]
            out_ref[pl.ds(t0 + gj * RG_GROUP, RG_GROUP), :] = hg
            hin = hg[RG_GROUP - 1:RG_GROUP, :] if d == 0 else hg[0:1, :]
        return hin

    def step(i, carry):
        h_f, h_b = carry
        bb = jnp.where(i < n_ctx_blocks, n_ctx_blocks - 1 - i, n_blocks - 1 - (i - n_ctx_blocks))
        h_f = scan_block(0, pl.multiple_of(i * blk, blk), h_f, o_ref)
        h_b = scan_block(1, pl.multiple_of(bb * blk, blk), h_b, ob_ref)
        return h_f, h_b

    zero = jnp.zeros((1, w), F32)
    lax.fori_loop(0, n_blocks, step, (zero, zero))
    o_ref[...] += ob_ref[...]


def _blockdiag_dense(w):
    nd, nb, k, _ = w.shape
    eye = jnp.eye(nb, dtype=w.dtype)
    return jnp.einsum('dnij,nm->dnimj', w, eye).reshape(nd, nb * k, nb * k)


def _rglru(p3, col_block, conv_w, conv_b, wa, ba, wx, bx, lam, n_ctx):
    bsz, t, _ = p3.shape
    w = conv_w.shape[1]
    full = lambda shape: pl.BlockSpec(shape, lambda b: (0,) * len(shape))
    kern = functools.partial(_rg_kernel, n_ctx=n_ctx)
    return pl.pallas_call(
        kern, grid=(bsz,),
        in_specs=[pl.BlockSpec((None, t, w), lambda b: (b, 0, col_block)),
                  full((RG_CONV, w)), full((1, w)),
                  full((2, w, w)), full((2, w)), full((2, w, w)), full((2, w)), full((2, w))],
        out_specs=pl.BlockSpec((None, t, w), lambda b: (b, 0, 0)),
        out_shape=jax.ShapeDtypeStruct((bsz, t, w), F32),
        scratch_shapes=[pltpu.VMEM((t + 3 * PAD, w), F32)] + [pltpu.VMEM((t, w), F32)] * 2,
        compiler_params=_cparams("parallel"), name="rglru",
    )(p3, conv_w, conv_b.reshape(1, w), _blockdiag_dense(wa).astype(BF16), ba,
      _blockdiag_dense(wx).astype(BF16), bx, lam)


def _merge_kernel(hg_ref, rg_ref, gate_ref, y_ref, og_ref, w_ref, h_ref, gtc_ref, gt_ref, o_ref,
                  *, n_ctx, tiles_per_batch):
    hg = hg_ref[...]
    half = hg.shape[1]
    hn = hg * lax.rsqrt(jnp.mean(hg * hg, axis=-1, keepdims=True) + EPS) * og_ref[...]
    a = (hn * _silu(gate_ref[...])).astype(BF16)
    b = (rg_ref[...] * _gelu_tanh(y_ref[...])).astype(BF16)
    o = _dot(a, w_ref[0:half, :].astype(BF16)) + _dot(b, w_ref[half:, :].astype(BF16))
    gate = _row_gate(gtc_ref, gt_ref, h_ref.shape, pl.program_id(0) % tiles_per_batch, n_ctx)
    o_ref[...] = h_ref[...] + gate * o


def _merge_out(hg, rg, p, onorm_g, w_out, j, h, mods, n_batch, t, n_ctx):
    n, d = h.shape
    half = hg.shape[1]
    tm = ROW_TILE
    tpb = t // tm
    blk = lambda c: pl.BlockSpec((tm, half), lambda i: (i, c))
    kern = functools.partial(_merge_kernel, n_ctx=n_ctx, tiles_per_batch=tpb)
    return pl.pallas_call(
        kern, grid=(n // tm,),
        in_specs=[blk(0), blk(0), blk(4), blk(6),
                  pl.BlockSpec((1, half), lambda i: (0, 0)),
                  _resident((None, d, d), lambda i: (j, 0, 0)),
                  pl.BlockSpec((tm, d), lambda i: (i, 0))] + _stream_specs(d, n_batch, tpb, 2, 1),
        out_specs=pl.BlockSpec((tm, d), lambda i: (i, 0)),
        out_shape=jax.ShapeDtypeStruct((n, d), F32),
        compiler_params=_cparams("parallel"), name="merge_out",
    )(hg, rg, p, p, onorm_g.reshape(1, half), w_out, h, mods, mods)


def _na_bias_table(rpb):
    q = np.arange(GRID_W)[:, None]
    kc = np.arange(GRID_W)[None, :]
    wstart = np.clip(q - NA_KC // 2, 0, GRID_W - NA_KC)
    mask = (kc >= wstart) & (kc < wstart + NA_KC)
    coff = np.clip(kc - q + NA_KC - 1, 0, 2 * NA_KC - 2)
    sel = (coff[None] == np.arange(2 * NA_KC - 1)[:, None, None]).astype(np.float32)
    toe = jnp.einsum('hdj,jqk->hqdk', rpb.astype(F32), sel, precision=lax.Precision.HIGHEST)
    toe = jnp.where(mask[None, :, None, :], toe, NEG_BIG)
    h = toe.shape[0]
    flat = toe.reshape(h // 2, 2 * GRID_W, -1)
    width = (2 * NA_KR) * GRID_W
    even = jnp.pad(flat, ((0, 0), (0, 0), (0, width - flat.shape[2])))
    odd = jnp.pad(flat[:, :, GRID_W:], ((0, 0), (0, 0), (0, width - flat.shape[2] + GRID_W)))
    return jnp.stack([even, odd], axis=1)


def _stack_heads(x, hd):
    lane = lax.broadcasted_iota(jnp.int32, x.shape, 1)
    zero = jnp.zeros_like(x)
    return jnp.concatenate([jnp.where(lane < hd, x, zero), jnp.where(lane >= hd, x, zero)], axis=0)


def _unstack_heads(o2, hd):
    n = o2.shape[0] // 2
    lane = lax.broadcasted_iota(jnp.int32, (n, o2.shape[1]), 1)
    return jnp.where(lane < hd, o2[:n], o2[n:])


def _na_kernel(q_ref, k_ref, v_ref, qg_ref, kg_ref, bias_ref, o_ref, qn_ref, kn_ref,
               s_ref, p_ref, den_ref, *, n_ctx, scale):
    t, w = q_ref.shape
    hd = w // 2
    rows_grid = (t - n_ctx) // GRID_W
    r_i = lax.broadcasted_iota(jnp.int32, (w, w), 0)
    c_i = lax.broadcasted_iota(jnp.int32, (w, w), 1)
    seg = jnp.where((r_i // hd) == (c_i // hd), 1.0 / hd, 0.0).astype(BF16)

    def normed(x, g):
        ms = _dot((x * x).astype(BF16), seg)
        return (x * lax.rsqrt(ms + EPS) * g).astype(BF16)

    qn_ref[...] = normed(q_ref[...].astype(F32), qg_ref[...] * scale)
    kn_ref[...] = normed(k_ref[...].astype(F32), kg_ref[...])
    vb_ref = v_ref

    kc = kn_ref[0:n_ctx, :]
    vc = vb_ref[0:n_ctx, :]

    s = _dot_nt(_stack_heads(qn_ref[0:n_ctx, :], hd), kc)
    p = jnp.exp(s - jnp.max(s, axis=-1, keepdims=True))
    o2 = _dot(p.astype(BF16), vc) / jnp.sum(p, axis=-1, keepdims=True)
    o_ref[0:n_ctx, :] = _unstack_heads(o2, hd).astype(o_ref.dtype)

    kr = NA_KR
    n_win = kr * GRID_W

    def window(r):
        r = jnp.clip(r, 0, rows_grid - 1)
        rs = jnp.clip(r - kr // 2, 0, rows_grid - kr)
        q0 = pl.multiple_of(n_ctx + r * GRID_W, GRID_W)
        k0 = pl.multiple_of(n_ctx + rs * GRID_W, GRID_W)
        return q0, k0, rs - r + (kr - 1)

    gap = NA_SLOTS // 2
    for slot in range(NA_SLOTS):
        s_ref[slot] = jnp.zeros(s_ref.shape[1:], F32)
        p_ref[slot] = jnp.zeros(p_ref.shape[1:], BF16)
        den_ref[slot] = jnp.ones(den_ref.shape[1:], F32)

    def step(i, slot):
        q0, k0, _ = window(i - 2 * gap)
        pr = p_ref[slot]
        o2 = (_dot(pr[:, :n_win], vb_ref[pl.ds(k0, n_win), :]) + _dot(pr[:, n_win:], vc)) / den_ref[slot]
        o_ref[pl.ds(q0, GRID_W), :] = _unstack_heads(o2, hd).astype(o_ref.dtype)

        mid = (slot + gap) % NA_SLOTS
        sc = s_ref[mid]
        pe = jnp.exp(sc - jnp.max(sc, axis=-1, keepdims=True))
        den_ref[mid] = jnp.sum(pe, axis=-1, keepdims=True)
        p_ref[mid] = pe.astype(BF16)

        q0, k0, off = window(i)
        q2 = _stack_heads(qn_ref[pl.ds(q0, GRID_W), :], hd)
        bias = bias_ref[off % 2, :, pl.ds(pl.multiple_of((off // 2) * (2 * GRID_W), 2 * GRID_W), n_win)]
        s_ref[slot, :, :n_win] = _dot_nt(q2, kn_ref[pl.ds(k0, n_win), :]) + bias
        s_ref[slot, :, n_win:] = _dot_nt(q2, kc)

    def steps(j, carry):
        for slot in range(NA_SLOTS):
            step(NA_SLOTS * j + slot, slot)
        return carry

    lax.fori_loop(0, (rows_grid + 2 * gap) // NA_SLOTS, steps, 0)


def _na_attention(qkv3, q_g, k_g, rpb, n_ctx):
    bsz, t, d3 = qkv3.shape
    d = d3 // 3
    hd = d // NA_HEADS
    w = 2 * hd
    pairs = d // w
    rows_grid = (t - n_ctx) // GRID_W
    assert w == LANES and rows_grid >= NA_KR and rows_grid % NA_SLOTS == 0
    n_keys = NA_KR * GRID_W + n_ctx
    bias = _na_bias_table(rpb)
    kern = functools.partial(_na_kernel, n_ctx=n_ctx, scale=hd ** -0.5)

    def col(group):
        return pl.BlockSpec((None, t, w), lambda hp, b: (b, 0, group * pairs + hp))

    gain = lambda g: jnp.tile(g.astype(F32), 2).reshape(1, w)
    return pl.pallas_call(
        kern, grid=(pairs, bsz),
        in_specs=[col(0), col(1), col(2),
                  pl.BlockSpec((1, w), lambda hp, b: (0, 0)),
                  pl.BlockSpec((1, w), lambda hp, b: (0, 0)),
                  pl.BlockSpec((None,) + bias.shape[1:], lambda hp, b: (hp, 0, 0, 0))],
        out_specs=pl.BlockSpec((None, t, w), lambda hp, b: (b, 0, hp)),
        out_shape=jax.ShapeDtypeStruct((bsz, t, d), BF16),
        scratch_shapes=[pltpu.VMEM((t, w), BF16)] * 2
        + [pltpu.VMEM((NA_SLOTS, 2 * GRID_W, n_keys), F32), pltpu.VMEM((NA_SLOTS, 2 * GRID_W, n_keys), BF16),
           pltpu.VMEM((NA_SLOTS, 2 * GRID_W, 1), F32)],
        compiler_params=_cparams("parallel", "parallel"), name="na_attention",
    )(qkv3, qkv3, qkv3, gain(q_g), gain(k_g), bias)


FFN_ROWS = 384


def _swiglu_kernel(h_ref, g_ref, shc_ref, sh_ref, scc_ref, sc_ref, w1_ref, w3_ref, w2_ref, gtc_ref, gt_ref,
                   o_ref, *, n_ctx, tiles_per_batch):
    tile_in_batch = pl.program_id(0) % tiles_per_batch
    x = _modulated_rows(h_ref, g_ref, shc_ref, sh_ref, scc_ref, sc_ref, tile_in_batch, n_ctx).astype(BF16)
    hid = _silu(_dot(x, w1_ref[...].astype(BF16))) * _dot(x, w3_ref[...].astype(BF16))
    y = _dot(hid.astype(BF16), w2_ref[...].astype(BF16))
    o_ref[...] = h_ref[...] + _row_gate(gtc_ref, gt_ref, y.shape, tile_in_batch, n_ctx) * y


def _swiglu_residual(h, g, w1, w3, w2, j, mods, n_batch, t, n_ctx):
    n, d = h.shape
    dff = w1.shape[2]
    tm = FFN_ROWS
    tpb = t // tm
    assert t % tm == 0 and n_ctx <= tm
    kern = functools.partial(_swiglu_kernel, n_ctx=n_ctx, tiles_per_batch=tpb)
    return pl.pallas_call(
        kern, grid=(n // tm,),
        in_specs=[pl.BlockSpec((tm, d), lambda i: (i, 0)),
                  pl.BlockSpec((1, d), lambda i: (0, 0))]
        + _stream_specs(d, n_batch, tpb, 3, 1) + _stream_specs(d, n_batch, tpb, 4, 1)
        + [_resident((None, d, dff), lambda i: (j, 0, 0)),
           _resident((None, d, dff), lambda i: (j, 0, 0)),
           _resident((None, dff, d), lambda i: (j, 0, 0))] + _stream_specs(d, n_batch, tpb, 5, 1),
        out_specs=pl.BlockSpec((tm, d), lambda i: (i, 0)),
        out_shape=jax.ShapeDtypeStruct((n, d), F32),
        compiler_params=_cparams("parallel"), name="swiglu",
    )(h, g, mods, mods, mods, mods, w1, w3, w2, mods, mods)


MOE_ROWS = 512


def _route_tables(route_t, counts, tm, n_exp):
    r = route_t.shape[1]
    experts = route_t[0:2].astype(jnp.int32)
    ranks = route_t[4:6].astype(jnp.int32)
    padded = ((counts + tm - 1) // tm) * tm
    ends = jnp.cumsum(padded)
    starts = ends - padded
    slot = ranks
    for e in range(n_exp):
        slot = slot + jnp.where(experts == e, starts[e], 0)
    p_slots = 2 * r + n_exp * tm
    n_tiles = p_slots // tm
    n_used = ends[-1] // tm
    tile = jnp.arange(n_tiles, dtype=jnp.int32)
    te = jnp.sum(((tile * tm)[:, None] >= ends[None, :]).astype(jnp.int32), axis=1)
    te = jnp.where(tile < n_used, te, te[n_used - 1])
    return slot, p_slots, te, n_used.reshape(1)


def _gather_rows(table, idx):
    p = idx.shape[0]
    m = table.shape[1]
    mesh = plsc.VectorSubcoreMesh(core_axis_name="c", subcore_axis_name="s")

    @functools.partial(pl.kernel, out_type=jax.ShapeDtypeStruct((p, m), table.dtype), mesh=mesh)
    def gather(table_hbm, idx_hbm, out_hbm):
        def body(idx_vmem, out_vmem):
            pltpu.sync_copy(table_hbm.at[idx_vmem.at[0]], out_vmem)

        pltpu.emit_pipeline(
            body, grid=(p // SC_WINDOW,),
            in_specs=[pl.BlockSpec((1, SC_WINDOW), lambda i: (0, i))],
            out_specs=[pl.BlockSpec((SC_WINDOW, m), lambda i: (i, 0))],
            core_axis_name=("c", "s"), dimension_semantics=(pltpu.PARALLEL,),
        )(idx_hbm, out_hbm)

    return gather(table, idx.reshape(1, p))


def _scatter_rows(rows, idx_a, idx_b, n_out):
    q, m = rows.shape
    mesh = plsc.VectorSubcoreMesh(core_axis_name="c", subcore_axis_name="s")

    @functools.partial(pl.kernel, out_type=jax.ShapeDtypeStruct((n_out, m), rows.dtype), mesh=mesh)
    def scatter(rows_hbm, ia_hbm, ib_hbm, out_hbm):
        def body(rows_vmem, ia_vmem, ib_vmem):
            pltpu.sync_copy(rows_vmem, out_hbm.at[ia_vmem.at[0]])
            pltpu.sync_copy(rows_vmem, out_hbm.at[ib_vmem.at[0]])

        pltpu.emit_pipeline(
            body, grid=(q // SC_WINDOW,),
            in_specs=[pl.BlockSpec((SC_WINDOW, m), lambda i: (i, 0)),
                      pl.BlockSpec((1, SC_WINDOW), lambda i: (0, i)),
                      pl.BlockSpec((1, SC_WINDOW), lambda i: (0, i))],
            out_specs=[],
            core_axis_name=("c", "s"), dimension_semantics=(pltpu.PARALLEL,),
        )(rows_hbm, ia_hbm, ib_hbm)

    return scatter(rows, idx_a.reshape(1, q), idx_b.reshape(1, q))


def _scatter_pieces(pieces, slot, n_slots):
    npc, r, m = pieces.shape
    offs = jnp.arange(npc, dtype=jnp.int32)[:, None] * n_slots
    idx_a = (slot[0][None, :] + offs).reshape(-1)
    idx_b = (slot[1][None, :] + offs).reshape(-1)
    return _scatter_rows(pieces.reshape(npc * r, m), idx_a, idx_b, npc * n_slots).reshape(npc, n_slots, m)


def _gather_pieces(pieces, idx):
    npc, v, m = pieces.shape
    offs = (jnp.arange(npc, dtype=jnp.int32) * v).reshape((npc,) + (1,) * idx.ndim)
    flat = (idx[None] + offs).reshape(-1)
    return _gather_rows(pieces.reshape(npc * v, m), flat).reshape((npc,) + idx.shape + (m,))


def _moe_ffn_kernel(te_ref, nu_ref, x_ref, w1_ref, w3_ref, w2_ref, y_ref):
    @pl.when(pl.program_id(0) < nu_ref[0])
    def _():
        x = _load_unpacked(x_ref).astype(BF16)
        hid = _silu(_dot(x, w1_ref[...].astype(BF16))) * _dot(x, w3_ref[...].astype(BF16))
        _store_pieces(y_ref, _pack_halves(_dot(hid.astype(BF16), w2_ref[...].astype(BF16))))


def _moe_ffn(xs, te, n_used, w1, w3, w2, j):
    npc, p, m = xs.shape
    d = 2 * npc * m
    dff = w1.shape[3]
    tm = MOE_ROWS

    def row(i, te_ref, nu_ref):
        return (0, jnp.minimum(i, nu_ref[0] - 1), 0)

    def expert(i, te_ref, nu_ref):
        return (j, te_ref[i], 0, 0)

    grid_spec = pltpu.PrefetchScalarGridSpec(
        num_scalar_prefetch=2, grid=(p // tm,),
        in_specs=[pl.BlockSpec((npc, tm, m), row),
                  _resident((None, None, d, dff), expert),
                  _resident((None, None, d, dff), expert),
                  _resident((None, None, dff, d), expert)],
        out_specs=pl.BlockSpec((npc, tm, m), row))
    return pl.pallas_call(
        _moe_ffn_kernel, grid_spec=grid_spec,
        out_shape=jax.ShapeDtypeStruct((npc, p, m), jnp.int32),
        compiler_params=_cparams("arbitrary"), name="moe_ffn",
    )(te, n_used, xs, w1, w3, w2)


def _moe_combine_kernel(ya_ref, yb_ref, route_ref, h_ref, gt_ref, o_ref):
    r = route_ref[...]
    y = r[:, 2:3] * _load_unpacked(ya_ref) + r[:, 3:4] * _load_unpacked(yb_ref)
    o_ref[...] = h_ref[...] + gt_ref[...] * y


def _moe_combine(yg, route, h, mods, n_batch, tiles_per_batch, ctx_tiles):
    d = h.shape[1]
    npc, _, r, m = yg.shape
    tm = TOKEN_TILE
    per = tiles_per_batch - ctx_tiles
    h_idx = lambda i: ((i // per) * tiles_per_batch + ctx_tiles + i % per, 0)
    if ctx_tiles == 0:
        mod_idx = _mod_row_spec(5, tiles_per_batch, n_batch)
    else:
        mod_idx = lambda i: (i // per, 5, 0, 0)
    return pl.pallas_call(
        _moe_combine_kernel, grid=(r // tm,),
        in_specs=[pl.BlockSpec((npc, None, tm, m), lambda i: (0, 0, i, 0)),
                  pl.BlockSpec((npc, None, tm, m), lambda i: (0, 1, i, 0)),
                  pl.BlockSpec((tm, LANES), lambda i: (i, 0)),
                  pl.BlockSpec((tm, d), h_idx),
                  pl.BlockSpec((None, None, 1, d), mod_idx)],
        out_specs=pl.BlockSpec((tm, d), lambda i: (i, 0)),
        out_shape=jax.ShapeDtypeStruct((r, d), F32),
        compiler_params=_cparams("parallel"), name="moe_combine",
    )(yg, yg, route, h, mods)


def _moe_residual(h, g, w_router, w1, w3, w2, j, mods, n_batch, t, n_ctx, latent_only):
    tpb = t // TOKEN_TILE
    ctx_tiles = n_ctx // TOKEN_TILE if latent_only else 0
    u_pieces, route, route_t, counts = _modulate_router(h, g, mods, 3, n_batch, tpb, ctx_tiles, w_router)
    if latent_only:
        npc, _, m = u_pieces.shape
        u_pieces = u_pieces.reshape(npc, n_batch, t, m)[:, :, n_ctx:].reshape(npc, -1, m)
        route = route.reshape(n_batch, t, LANES)[:, n_ctx:].reshape(-1, LANES)
        route_t = route_t.reshape(-1, n_batch, t)[:, :, n_ctx:].reshape(route_t.shape[0], -1)
    n_exp = w1.shape[1]
    slot, n_slots, te, n_used = _route_tables(route_t, counts[0, :n_exp].astype(jnp.int32), MOE_ROWS, n_exp)
    xs = _scatter_pieces(u_pieces, slot, n_slots)
    ys = _moe_ffn(xs, te, n_used, w1, w3, w2, j)
    yg = _gather_pieces(ys, slot)
    return _moe_combine(yg, route, h, mods, n_batch, tpb, ctx_tiles)


def kernel(x, c, ctx, c_ctx, w_mod, b_mod, norm_mix_g, norm_ffn_g, ev_w_in, ev_w_out, hg_lb_logits, hg_onorm_g, rg_conv_w, rg_conv_b, rg_wa, rg_ba, rg_wx, rg_bx, rg_lambda, na_w_qkv, na_w_o, na_q_g, na_k_g, na_rpb, ffn_w1, ffn_w3, ffn_w2, moe_router, moe_w1, moe_w3, moe_w2):
    bsz, seq, d = x.shape
    n_ctx = ctx.shape[1]
    t = n_ctx + seq
    depth = w_mod.shape[0]
    assert n_ctx % TOKEN_TILE == 0 and seq % TOKEN_TILE == 0 and t % ROW_TILE == 0 and n_ctx <= ROW_TILE

    rows = -(-(bsz + 1) // 8) * 8
    cond = jnp.zeros((rows, d), F32).at[:bsz].set(c).at[bsz].set(c_ctx)
    mods_all = _adaln(cond, w_mod, b_mod).reshape(depth, rows, N_MOD, 1, d)

    lb_p = jax.nn.softmax(hg_lb_logits.astype(F32), axis=0)
    lb_all = jnp.cumsum(lb_p, axis=0) - lb_p[0]

    h = jnp.concatenate([ctx, x], axis=1).reshape(bsz * t, d)
    for layer in range(depth):
        j = layer // 2
        mods = mods_all[layer]
        g_mix = norm_mix_g[layer].reshape(1, d)
        if layer % 2 == 0:
            p = _modulated_matmul(h, g_mix, mods, ev_w_in, j, F32, ROW_TILE // 2, bsz, t, n_ctx)
            p3 = p.reshape(bsz, t, -1)
            hg = _hgrn2(p3, lb_all[j], n_ctx)
            half = hg.shape[-1]
            rg = _rglru(p3, 5 * half // (d - half), rg_conv_w[j], rg_conv_b[j], rg_wa[j], rg_ba[j],
                        rg_wx[j], rg_bx[j], rg_lambda[j], n_ctx)
            h = _merge_out(hg.reshape(bsz * t, half), rg.reshape(bsz * t, d - half), p, hg_onorm_g[j],
                           ev_w_out, j, h, mods, bsz, t, n_ctx)
        else:
            qkv = _modulated_matmul(h, g_mix, mods, na_w_qkv, j, BF16, ROW_TILE, bsz, t, n_ctx)
            o = _na_attention(qkv.reshape(bsz, t, 3 * d), na_q_g[j], na_k_g[j], na_rpb[j], n_ctx)
            h = _matmul_residual(o.reshape(bsz * t, d), na_w_o, j, h, mods, 2, bsz, t, n_ctx)
        if layer % 2 == 0:
            h = _swiglu_residual(h, norm_ffn_g[layer].reshape(1, d), ffn_w1, ffn_w3, ffn_w2, j, mods, bsz, t, n_ctx)
        else:
            h = _moe_residual(h, norm_ffn_g[layer].reshape(1, d), moe_router[j], moe_w1, moe_w3, moe_w2, j, mods,
                              bsz, t, n_ctx, layer == depth - 1)
    if depth % 2 == 0:
        return h.reshape(bsz, seq, d)
    return h.reshape(bsz, t, d)[:, n_ctx:]
```

```python
import functools

import numpy as np
import jax
import jax.numpy as jnp
from jax import lax
from jax.experimental import pallas as pl
from jax.experimental.pallas import tpu as pltpu
from jax.experimental.pallas import tpu_sc as plsc

F32 = jnp.float32
BF16 = jnp.bfloat16

EPS = 1e-6
NEG_BIG = -1e9
LB_MIN = 1e-30
LOG2_E = 1.4426950408889634
N_MOD = 6
GRID_W = 64
HG_HEAD_DIM = 128
RG_BLOCKS = 8
RG_CONV = 4
RG_C = 8.0
NA_HEADS = 16
NA_KR = 8
NA_KC = 16
N_EXPERTS = 8

LANES = 128
SC_WINDOW = 128
SC_PIECES = 2
TOKEN_TILE = 256
HG_CHUNK = 128
HG_LEVELS = 7
HG_SLOTS = 2
NA_SLOTS = 4
RG_SCAN_BLOCK = 256
RG_GROUP = 8
VMEM_LIMIT = 56 * 1024 * 1024


def _cparams(*sem):
    return pltpu.CompilerParams(dimension_semantics=sem, vmem_limit_bytes=VMEM_LIMIT)


def _resident(block_shape, index_map):
    return pl.BlockSpec(block_shape, index_map, pipeline_mode=pl.Buffered(1))


def _dot(a, b):
    return jnp.dot(a, b, preferred_element_type=F32)


def _dot_nt(a, b):
    return lax.dot_general(a, b, (((1,), (1,)), ((), ())), preferred_element_type=F32)


def _dot_tn(a, b):
    return lax.dot_general(a, b, (((0,), (0,)), ((), ())), preferred_element_type=F32)


def _split2(x):
    hi = x.astype(BF16)
    lo = (x - hi.astype(F32)).astype(BF16)
    return hi, lo


def _sigmoid(x):
    return 0.5 * jnp.tanh(0.5 * x) + 0.5


def _silu(x):
    return x * _sigmoid(x)


def _softplus_neg_abs(x):
    return jnp.log(1.0 + jnp.exp(-jnp.abs(x)))


def _log_sigmoid(x):
    return jnp.minimum(x, 0.0) - _softplus_neg_abs(x)


def _gelu_tanh(x):
    return 0.5 * x * (1.0 + jnp.tanh(np.sqrt(2.0 / np.pi) * (x + 0.044715 * (x * x * x))))


def _adaln_kernel(c_ref, w_ref, b_ref, o_ref):
    a_hi, a_lo = _split2(_silu(c_ref[...]))
    w_hi, w_lo = _split2(w_ref[...])
    o_ref[...] = _dot(a_hi, w_hi) + _dot(a_hi, w_lo) + _dot(a_lo, w_hi) + b_ref[...]


def _adaln(cond, w_mod, b_mod):
    depth, d, n = w_mod.shape
    r = cond.shape[0]
    tn = 1536
    return pl.pallas_call(
        _adaln_kernel,
        grid=(depth, n // tn),
        in_specs=[pl.BlockSpec((r, d), lambda l, j: (0, 0)),
                  pl.BlockSpec((None, d, tn), lambda l, j: (l, 0, j)),
                  pl.BlockSpec((None, 1, tn), lambda l, j: (l, 0, j))],
        out_specs=pl.BlockSpec((None, r, tn), lambda l, j: (l, 0, j)),
        out_shape=jax.ShapeDtypeStruct((depth, r, n), F32),
        compiler_params=_cparams("parallel", "parallel"),
        name="adaln",
    )(cond, w_mod, b_mod.reshape(depth, 1, n))


def _mod_row_spec(k, tiles_per_batch, n_batch):
    def idx(i):
        row = jnp.where(i % tiles_per_batch == 0, n_batch, i // tiles_per_batch)
        return (row, k, 0, 0)
    return idx


def _modulated(h_ref, g_ref, sh_ref, sc_ref):
    x = h_ref[...]
    y = x * lax.rsqrt(jnp.mean(x * x, axis=-1, keepdims=True) + EPS)
    return (y * g_ref[...]) * (1.0 + sc_ref[...]) + sh_ref[...]


def _pack_halves(x):
    m = x.shape[1] // 2
    lo = lax.bitcast_convert_type(x[:, :m].astype(BF16).astype(F32), jnp.int32)
    hi = lax.bitcast_convert_type(x[:, m:].astype(BF16).astype(F32), jnp.int32)
    return lax.shift_right_logical(lo, 16) | (hi & jnp.int32(-65536))


def _unpack_halves(w):
    lo = lax.bitcast_convert_type(lax.shift_left(w, 16), F32)
    hi = lax.bitcast_convert_type(w & jnp.int32(-65536), F32)
    return lo, hi


def _store_pieces(ref, w):
    m = w.shape[1] // SC_PIECES
    for j in range(SC_PIECES):
        ref[j] = w[:, j * m:(j + 1) * m]


def _load_unpacked(ref):
    parts = [_unpack_halves(ref[j]) for j in range(SC_PIECES)]
    return jnp.concatenate([lo for lo, _ in parts] + [hi for _, hi in parts], axis=1)


def _modulate_router_kernel(h_ref, g_ref, sh_ref, sc_ref, wr_ref, u_ref, route_ref, route_t_ref, cnt_ref,
                            *, tiles_per_batch, skip_tiles):
    i = pl.program_id(0)

    @pl.when(i == 0)
    def _():
        cnt_ref[...] = jnp.zeros_like(cnt_ref)

    u = _modulated(h_ref, g_ref, sh_ref, sc_ref)
    _store_pieces(u_ref, _pack_halves(u))
    u0, u1 = _split2(u)
    w0, w1 = _split2(wr_ref[...])
    logits = _dot(u0, w0) + _dot(u0, w1) + _dot(u1, w0)
    lane = lax.broadcasted_iota(jnp.int32, logits.shape, 1).astype(F32)
    lg = jnp.where(lane < N_EXPERTS, logits, -jnp.inf)
    m1 = jnp.max(lg, axis=-1, keepdims=True)
    i1 = jnp.min(jnp.where(lg == m1, lane, float(LANES)), axis=-1, keepdims=True)
    lg2 = jnp.where(lane == i1, -jnp.inf, lg)
    m2 = jnp.max(lg2, axis=-1, keepdims=True)
    i2 = jnp.min(jnp.where(lg2 == m2, lane, float(LANES)), axis=-1, keepdims=True)
    e2 = jnp.exp(m2 - m1)
    den = 1.0 + e2

    routed = jnp.where(i % tiles_per_batch >= skip_tiles, 1.0, 0.0)
    pick_a = jnp.where(lane == i1, routed, 0.0)
    pick_b = jnp.where(lane == i2, routed, 0.0)
    tm = pick_a.shape[0]
    earlier = jnp.where(lax.broadcasted_iota(jnp.int32, (tm, tm), 1) < lax.broadcasted_iota(jnp.int32, (tm, tm), 0),
                        1.0, 0.0).astype(BF16)
    total_a = jnp.sum(pick_a, axis=0, keepdims=True)
    base = cnt_ref[0:1, :]
    rank_a = jnp.sum(pick_a * (base + _dot(earlier, pick_a.astype(BF16))), axis=-1, keepdims=True)
    rank_b = jnp.sum(pick_b * (base + total_a + _dot(earlier, pick_b.astype(BF16))), axis=-1, keepdims=True)
    cnt_ref[0:1, :] = base + total_a + jnp.sum(pick_b, axis=0, keepdims=True)

    route = jnp.where(lane == 0, i1, jnp.where(lane == 1, i2, jnp.where(lane == 2, 1.0 / den, jnp.where(
        lane == 3, e2 / den, jnp.where(lane == 4, rank_a, rank_b)))))
    route_ref[...] = route
    route_t_ref[...] = route.T[0:route_t_ref.shape[0], :]


def _modulate_router(h, g, mods, k_shift, n_batch, tiles_per_batch, skip_tiles, w_router):
    n, d = h.shape
    tm = TOKEN_TILE
    row = pl.BlockSpec((tm, d), lambda i: (i, 0))
    vec = pl.BlockSpec((1, d), lambda i: (0, 0))

    def mod_spec(k):
        return pl.BlockSpec((None, None, 1, d), _mod_row_spec(k, tiles_per_batch, n_batch))

    in_specs = [row, vec, mod_spec(k_shift), mod_spec(k_shift + 1)]
    wr = jnp.zeros((d, LANES), F32).at[:, :N_EXPERTS].set(w_router)
    kern = functools.partial(_modulate_router_kernel, tiles_per_batch=tiles_per_batch, skip_tiles=skip_tiles)
    return pl.pallas_call(
        kern, grid=(n // tm,),
        in_specs=in_specs + [pl.BlockSpec((d, LANES), lambda i: (0, 0))],
        out_specs=[pl.BlockSpec((SC_PIECES, tm, d // 2 // SC_PIECES), lambda i: (0, i, 0)),
                   pl.BlockSpec((tm, LANES), lambda i: (i, 0)),
                   pl.BlockSpec((8, tm), lambda i: (0, i)),
                   pl.BlockSpec((8, LANES), lambda i: (0, 0))],
        out_shape=[jax.ShapeDtypeStruct((SC_PIECES, n, d // 2 // SC_PIECES), jnp.int32),
                   jax.ShapeDtypeStruct((n, LANES), F32),
                   jax.ShapeDtypeStruct((8, n), F32),
                   jax.ShapeDtypeStruct((8, LANES), F32)],
        compiler_params=_cparams("arbitrary"), name="modulate_router",
    )(h, g, mods, mods, wr)


ROW_TILE = 768


def _row_gate(ctx_ref, own_ref, shape, tile_in_batch, n_ctx):
    rows = lax.broadcasted_iota(jnp.int32, shape, 0)
    is_ctx = (rows < n_ctx) & (tile_in_batch == 0)
    return jnp.where(is_ctx, ctx_ref[...], own_ref[...])


def _modulated_rows(x, g_ref, shc_ref, sh_ref, scc_ref, sc_ref, tile_in_batch, n_ctx):
    y = x * lax.rsqrt(jnp.mean(x * x, axis=-1, keepdims=True) + EPS)
    shift = _row_gate(shc_ref, sh_ref, x.shape, tile_in_batch, n_ctx)
    scale = _row_gate(scc_ref, sc_ref, x.shape, tile_in_batch, n_ctx)
    return (y * g_ref[...]) * (1.0 + scale) + shift


def _stream_specs(d, n_batch, tiles_per_batch, k, nargs):
    if nargs == 1:
        ctx = lambda i: (n_batch, k, 0, 0)
        own = lambda i: (i // tiles_per_batch, k, 0, 0)
    else:
        ctx = lambda i, f: (n_batch, k, 0, 0)
        own = lambda i, f: (i // tiles_per_batch, k, 0, 0)
    return [pl.BlockSpec((None, None, 1, d), ctx), pl.BlockSpec((None, None, 1, d), own)]


def _mod_mm_kernel(h_ref, g_ref, shc_ref, sh_ref, scc_ref, sc_ref, w_ref, o_ref, *, n_ctx, tiles_per_batch):
    x = _modulated_rows(h_ref[...], g_ref, shc_ref, sh_ref, scc_ref, sc_ref,
                        pl.program_id(0) % tiles_per_batch, n_ctx).astype(BF16)
    o_ref[...] = _dot(x, w_ref[...].astype(BF16)).astype(o_ref.dtype)


def _modulated_matmul(h, g, mods, w, j, out_dtype, tm, n_batch, t, n_ctx):
    n, d = h.shape
    nout = w.shape[2]
    tpb = t // tm
    assert t % tm == 0 and n_ctx <= tm
    kern = functools.partial(_mod_mm_kernel, n_ctx=n_ctx, tiles_per_batch=tpb)
    return pl.pallas_call(
        kern, grid=(n // tm,),
        in_specs=[pl.BlockSpec((tm, d), lambda i: (i, 0)),
                  pl.BlockSpec((1, d), lambda i: (0, 0))]
        + _stream_specs(d, n_batch, tpb, 0, 1) + _stream_specs(d, n_batch, tpb, 1, 1)
        + [_resident((None, d, nout), lambda i: (j, 0, 0))],
        out_specs=pl.BlockSpec((tm, nout), lambda i: (i, 0)),
        out_shape=jax.ShapeDtypeStruct((n, nout), out_dtype),
        compiler_params=_cparams("parallel"), name="modulated_matmul",
    )(h, g, mods, mods, mods, mods, w)


def _mm_res_kernel(x_ref, w_ref, h_ref, gtc_ref, gt_ref, o_ref, *, n_ctx, tiles_per_batch):
    gate = _row_gate(gtc_ref, gt_ref, h_ref.shape, pl.program_id(0) % tiles_per_batch, n_ctx)
    o_ref[...] = h_ref[...] + gate * _dot(x_ref[...], w_ref[...].astype(BF16))


def _matmul_residual(x, w, j, h, mods, k_gate, n_batch, t, n_ctx):
    n, d = h.shape
    kdim = x.shape[1]
    tm = ROW_TILE
    tpb = t // tm
    kern = functools.partial(_mm_res_kernel, n_ctx=n_ctx, tiles_per_batch=tpb)
    return pl.pallas_call(
        kern, grid=(n // tm,),
        in_specs=[pl.BlockSpec((tm, kdim), lambda i: (i, 0)),
                  _resident((None, kdim, d), lambda i: (j, 0, 0)),
                  pl.BlockSpec((tm, d), lambda i: (i, 0))] + _stream_specs(d, n_batch, tpb, k_gate, 1),
        out_specs=pl.BlockSpec((tm, d), lambda i: (i, 0)),
        out_shape=jax.ShapeDtypeStruct((n, d), F32),
        compiler_params=_cparams("parallel"), name="matmul_residual",
    )(x, w, h, mods, mods)


def _hg_tables():
    c = HG_CHUNK
    t = np.arange(c)[:, None]
    u = np.arange(c)[None, :]
    mats = [(u <= t), (u > t)]
    for k in range(1, HG_LEVELS + 1):
        m = ((t >> k) << k) + (1 << (k - 1)) - 1
        upper = t > m
        mats.append(np.where(upper, (u > m) & (u <= t), (u > t) & (u <= m)))
    fwd = np.concatenate(mats, axis=0).astype(np.float32)
    bwd = fwd.reshape(-1, c, c)[:, ::-1, ::-1].reshape(-1, c)
    x = t ^ u
    lvl = np.where(t > u, np.floor(np.log2(np.maximum(x, 1))).astype(np.int32) + 1, -1)
    lvl = np.where(t == u, 0, lvl).astype(np.int32)
    return np.stack([fwd, bwd]), np.stack([lvl, lvl.T])


def _hg_kernel(q_ref, ff_ref, fb_ref, v_ref, lb_ref, m_ref, lvl_ref, o_ref, of_ref, ob_ref,
               qk_ref, el_ref, io_ref, tot_ref, att_ref, *, n_ctx_chunks, n_chunks):
    c = HG_CHUNK
    gap = HG_SLOTS // 2
    t = o_ref.shape[0]
    z_refs = (ff_ref, fb_ref)
    out_refs = (of_ref, ob_ref)
    for ref in (qk_ref, el_ref, io_ref, tot_ref, att_ref):
        ref[...] = jnp.zeros(ref.shape, ref.dtype)

    def start_row(d, j):
        j = jnp.clip(j, 0, n_chunks - 1)
        if d == 1:
            j = jnp.where(j < n_ctx_chunks, n_ctx_chunks - 1 - j, n_chunks - 1 - (j - n_ctx_chunks))
        return pl.multiple_of(j * c, c)

    def stage1(d, j, slot):
        t0 = start_row(d, j)
        lb = lb_ref[d:d + 1, :]
        log_lb = jnp.log(jnp.maximum(lb, LB_MIN))
        q = _silu(q_ref[pl.ds(t0, c), :])
        z = z_refs[d][pl.ds(t0, c), :]
        k = (1.0 - lb) * _sigmoid(-z)
        b = jnp.log1p(-lb) + _log_sigmoid(z)
        logf = jnp.maximum(log_lb, b) + _softplus_neg_abs(log_lb - b)
        g_hi, g_lo = _split2(logf * LOG2_E)
        m_ends = m_ref[d, 0:2 * c, :]
        e_ends = jnp.exp2(_dot(m_ends, g_hi) + _dot(m_ends, g_lo))
        el_ref[d, slot] = jnp.exp2(_dot(m_ref[d, 2 * c:, :], g_hi).astype(BF16))
        qk_ref[d, slot, 0] = q.astype(BF16)
        qk_ref[d, slot, 1] = k.astype(BF16)
        io_ref[d, slot, 0] = (q * e_ends[0:c]).astype(BF16)
        io_ref[d, slot, 1] = (k * e_ends[c:]).astype(BF16)
        tot_row = c - 1 if d == 0 else 0
        tot_ref[d, slot, 0:1, :] = e_ends[tot_row:tot_row + 1, :]

    def stage2(d, slot):
        qb = qk_ref[d, slot, 0]
        kb = qk_ref[d, slot, 1]
        lvl = lvl_ref[d]
        att = jnp.where(lvl == 0, _dot_nt(qb, kb), 0.0)
        for lev in range(1, HG_LEVELS + 1):
            el = el_ref[d, slot, (lev - 1) * c:lev * c, :]
            att = jnp.where(lvl == lev, _dot_nt(qb * el, kb * el), att)
        att_ref[d, slot] = att.astype(BF16)

    def stage3(d, j, slot, st):
        t0 = start_row(d, j)
        v = v_ref[pl.ds(t0, c), :].astype(BF16)
        row = jnp.where(j < n_chunks, t0, t)
        out_refs[d][pl.ds(pl.multiple_of(row, c), c), :] = (
            _dot(att_ref[d, slot], v) + _dot_nt(io_ref[d, slot, 0], st.astype(BF16)))
        return st * tot_ref[d, slot, 0:1, :] + _dot_tn(v, io_ref[d, slot, 1])

    def steps(jj, carry):
        st = list(carry)
        for slot in range(HG_SLOTS):
            i = HG_SLOTS * jj + slot
            for d in range(2):
                st[d] = stage3(d, i - 2 * gap, slot, st[d])
            for d in range(2):
                stage2(d, (slot + gap) % HG_SLOTS)
            for d in range(2):
                stage1(d, i, slot)
        return tuple(st)

    zero = jnp.zeros((c, c), F32)
    n_steps = -(-(n_chunks + 2 * gap) // HG_SLOTS)
    lax.fori_loop(0, n_steps, steps, (zero, zero))
    o_ref[...] = (of_ref[0:t, :] + ob_ref[0:t, :]).astype(o_ref.dtype)


def _hgrn2(p3, lb, n_ctx):
    bsz, t, _ = p3.shape
    width = lb.shape[1]
    heads = width // HG_HEAD_DIM
    mats, lvl = _hg_tables()
    c = HG_CHUNK

    def col(group):
        return pl.BlockSpec((None, t, c), lambda b, h: (b, 0, group * heads + h))

    kern = functools.partial(_hg_kernel, n_ctx_chunks=n_ctx // c, n_chunks=t // c)
    return pl.pallas_call(
        kern, grid=(bsz, heads),
        in_specs=[col(0), col(1), col(2), col(3),
                  pl.BlockSpec((2, c), lambda b, h: (0, h)),
                  pl.BlockSpec(mats.shape, lambda b, h: (0, 0, 0)),
                  pl.BlockSpec(lvl.shape, lambda b, h: (0, 0, 0))],
        out_specs=pl.BlockSpec((None, t, c), lambda b, h: (b, 0, h)),
        out_shape=jax.ShapeDtypeStruct((bsz, t, width), F32),
        scratch_shapes=[pltpu.VMEM((t + c, c), F32), pltpu.VMEM((t + c, c), F32),
                        pltpu.VMEM((2, HG_SLOTS, 2, c, c), BF16), pltpu.VMEM((2, HG_SLOTS, HG_LEVELS * c, c), BF16),
                        pltpu.VMEM((2, HG_SLOTS, 2, c, c), BF16), pltpu.VMEM((2, HG_SLOTS, 8, c), F32),
                        pltpu.VMEM((2, HG_SLOTS, c, c), BF16)],
        compiler_params=_cparams("parallel", "parallel"), name="hgrn2",
    )(p3, p3, p3, p3, lb, jnp.asarray(mats, BF16), jnp.asarray(lvl))


PAD = 8


def _rg_kernel(x_ref, cw_ref, cb_ref, wa_ref, ba_ref, wx_ref, bx_ref, lam_ref, o_ref,
               xpad_ref, xc_ref, ob_ref, *, n_ctx):
    t, w = x_ref.shape
    n_lat = t - n_ctx
    lat0 = n_ctx + 2 * PAD
    xpad_ref[...] = jnp.zeros(xpad_ref.shape, F32)
    xpad_ref[PAD:PAD + n_ctx, :] = x_ref[0:n_ctx, :]
    xpad_ref[lat0:lat0 + n_lat, :] = x_ref[n_ctx:t, :]
    left = RG_CONV // 2
    for base, dst, n in ((PAD, 0, n_ctx), (lat0, n_ctx, n_lat)):
        acc = cb_ref[...] + cw_ref[0:1, :] * xpad_ref[base - left:base - left + n, :]
        for j in range(1, RG_CONV):
            acc = acc + cw_ref[j:j + 1, :] * xpad_ref[base - left + j:base - left + j + n, :]
        xc_ref[dst:dst + n, :] = acc

    blk = RG_SCAN_BLOCK
    n_blocks = t // blk
    n_ctx_blocks = n_ctx // blk
    sub = lax.broadcasted_iota(jnp.int32, (blk // RG_GROUP, RG_GROUP, w), 1)
    c_log_sig_lam = RG_C * _log_sigmoid(lam_ref[...])

    def scan_block(d, t0, hin, out_ref):
        xc = xc_ref[pl.ds(t0, blk), :]
        xb = xc.astype(BF16)
        r = _sigmoid(_dot(xb, wa_ref[d]) + ba_ref[d:d + 1, :])
        ig = _sigmoid(_dot(xb, wx_ref[d]) + bx_ref[d:d + 1, :])
        log_a = r * c_log_sig_lam[d:d + 1, :]
        a = jnp.exp(log_a)
        b = jnp.sqrt(jnp.maximum(1.0 - a * a, 0.0)) * (ig * xc)
        n_groups = blk // RG_GROUP
        a = a.reshape(n_groups, RG_GROUP, w)
        b = b.reshape(n_groups, RG_GROUP, w)
        s = 1
        while s < RG_GROUP:
            shift = s if d == 0 else RG_GROUP - s
            keep = (sub >= s) if d == 0 else (sub < RG_GROUP - s)
            b = jnp.where(keep, a * pltpu.roll(b, shift, 1), 0.0) + b
            a = jnp.where(keep, a * pltpu.roll(a, shift, 1), a)
            s *= 2
        for gi in range(n_groups):
            gj = gi if d == 0 else n_groups - 1 - gi
            hg = a[gj] * hin + b[gj]
            out_ref[pl.ds(t0 + gj * RG_GROUP, RG_GROUP), :] = hg
            hin = hg[RG_GROUP - 1:RG_GROUP, :] if d == 0 else hg[0:1, :]
        return hin

    def step(i, carry):
        h_f, h_b = carry
        bb = jnp.where(i < n_ctx_blocks, n_ctx_blocks - 1 - i, n_blocks - 1 - (i - n_ctx_blocks))
        h_f = scan_block(0, pl.multiple_of(i * blk, blk), h_f, o_ref)
        h_b = scan_block(1, pl.multiple_of(bb * blk, blk), h_b, ob_ref)
        return h_f, h_b

    zero = jnp.zeros((1, w), F32)
    lax.fori_loop(0, n_blocks, step, (zero, zero))
    o_ref[...] += ob_ref[...]


def _blockdiag_dense(w):
    nd, nb, k, _ = w.shape
    eye = jnp.eye(nb, dtype=w.dtype)
    return jnp.einsum('dnij,nm->dnimj', w, eye).reshape(nd, nb * k, nb * k)


def _rglru(p3, col_block, conv_w, conv_b, wa, ba, wx, bx, lam, n_ctx):
    bsz, t, _ = p3.shape
    w = conv_w.shape[1]
    full = lambda shape: pl.BlockSpec(shape, lambda b: (0,) * len(shape))
    kern = functools.partial(_rg_kernel, n_ctx=n_ctx)
    return pl.pallas_call(
        kern, grid=(bsz,),
        in_specs=[pl.BlockSpec((None, t, w), lambda b: (b, 0, col_block)),
                  full((RG_CONV, w)), full((1, w)),
                  full((2, w, w)), full((2, w)), full((2, w, w)), full((2, w)), full((2, w))],
        out_specs=pl.BlockSpec((None, t, w), lambda b: (b, 0, 0)),
        out_shape=jax.ShapeDtypeStruct((bsz, t, w), F32),
        scratch_shapes=[pltpu.VMEM((t + 3 * PAD, w), F32)] + [pltpu.VMEM((t, w), F32)] * 2,
        compiler_params=_cparams("parallel"), name="rglru",
    )(p3, conv_w, conv_b.reshape(1, w), _blockdiag_dense(wa).astype(BF16), ba,
      _blockdiag_dense(wx).astype(BF16), bx, lam)


def _na_bias_table(rpb):
    q = np.arange(GRID_W)[:, None]
    kc = np.arange(GRID_W)[None, :]
    wstart = np.clip(q - NA_KC // 2, 0, GRID_W - NA_KC)
    mask = (kc >= wstart) & (kc < wstart + NA_KC)
    coff = np.clip(kc - q + NA_KC - 1, 0, 2 * NA_KC - 2)
    sel = (coff[None] == np.arange(2 * NA_KC - 1)[:, None, None]).astype(np.float32)
    toe = jnp.einsum('hdj,jqk->hqdk', rpb.astype(F32), sel, precision=lax.Precision.HIGHEST)
    toe = jnp.where(mask[None, :, None, :], toe, NEG_BIG)
    h = toe.shape[0]
    flat = toe.reshape(h // 2, 2 * GRID_W, -1)
    width = (2 * NA_KR) * GRID_W
    even = jnp.pad(flat, ((0, 0), (0, 0), (0, width - flat.shape[2])))
    odd = jnp.pad(flat[:, :, GRID_W:], ((0, 0), (0, 0), (0, width - flat.shape[2] + GRID_W)))
    return jnp.stack([even, odd], axis=1)


def _stack_heads(x, hd):
    lane = lax.broadcasted_iota(jnp.int32, x.shape, 1)
    zero = jnp.zeros_like(x)
    return jnp.concatenate([jnp.where(lane < hd, x, zero), jnp.where(lane >= hd, x, zero)], axis=0)


def _unstack_heads(o2, hd):
    n = o2.shape[0] // 2
    lane = lax.broadcasted_iota(jnp.int32, (n, o2.shape[1]), 1)
    return jnp.where(lane < hd, o2[:n], o2[n:])


def _na_kernel(q_ref, k_ref, v_ref, qg_ref, kg_ref, bias_ref, o_ref, qn_ref, kn_ref,
               s_ref, p_ref, den_ref, *, n_ctx, scale):
    t, w = q_ref.shape
    hd = w // 2
    rows_grid = (t - n_ctx) // GRID_W
    r_i = lax.broadcasted_iota(jnp.int32, (w, w), 0)
    c_i = lax.broadcasted_iota(jnp.int32, (w, w), 1)
    seg = jnp.where((r_i // hd) == (c_i // hd), 1.0 / hd, 0.0).astype(BF16)

    def normed(x, g):
        ms = _dot((x * x).astype(BF16), seg)
        return (x * lax.rsqrt(ms + EPS) * g).astype(BF16)

    qn_ref[...] = normed(q_ref[...].astype(F32), qg_ref[...] * scale)
    kn_ref[...] = normed(k_ref[...].astype(F32), kg_ref[...])
    vb_ref = v_ref

    kc = kn_ref[0:n_ctx, :]
    vc = vb_ref[0:n_ctx, :]

    s = _dot_nt(_stack_heads(qn_ref[0:n_ctx, :], hd), kc)
    p = jnp.exp(s - jnp.max(s, axis=-1, keepdims=True))
    o2 = _dot(p.astype(BF16), vc) / jnp.sum(p, axis=-1, keepdims=True)
    o_ref[0:n_ctx, :] = _unstack_heads(o2, hd).astype(o_ref.dtype)

    kr = NA_KR
    n_win = kr * GRID_W

    def window(r):
        r = jnp.clip(r, 0, rows_grid - 1)
        rs = jnp.clip(r - kr // 2, 0, rows_grid - kr)
        q0 = pl.multiple_of(n_ctx + r * GRID_W, GRID_W)
        k0 = pl.multiple_of(n_ctx + rs * GRID_W, GRID_W)
        return q0, k0, rs - r + (kr - 1)

    gap = NA_SLOTS // 2
    for slot in range(NA_SLOTS):
        s_ref[slot] = jnp.zeros(s_ref.shape[1:], F32)
        p_ref[slot] = jnp.zeros(p_ref.shape[1:], BF16)
        den_ref[slot] = jnp.ones(den_ref.shape[1:], F32)

    def step(i, slot):
        q0, k0, _ = window(i - 2 * gap)
        pr = p_ref[slot]
        o2 = (_dot(pr[:, :n_win], vb_ref[pl.ds(k0, n_win), :]) + _dot(pr[:, n_win:], vc)) / den_ref[slot]
        o_ref[pl.ds(q0, GRID_W), :] = _unstack_heads(o2, hd).astype(o_ref.dtype)

        mid = (slot + gap) % NA_SLOTS
        sc = s_ref[mid]
        pe = jnp.exp(sc - jnp.max(sc, axis=-1, keepdims=True))
        den_ref[mid] = jnp.sum(pe, axis=-1, keepdims=True)
        p_ref[mid] = pe.astype(BF16)

        q0, k0, off = window(i)
        q2 = _stack_heads(qn_ref[pl.ds(q0, GRID_W), :], hd)
        bias = bias_ref[off % 2, :, pl.ds(pl.multiple_of((off // 2) * (2 * GRID_W), 2 * GRID_W), n_win)]
        s_ref[slot, :, :n_win] = _dot_nt(q2, kn_ref[pl.ds(k0, n_win), :]) + bias
        s_ref[slot, :, n_win:] = _dot_nt(q2, kc)

    def steps(j, carry):
        for slot in range(NA_SLOTS):
            step(NA_SLOTS * j + slot, slot)
        return carry

    lax.fori_loop(0, (rows_grid + 2 * gap) // NA_SLOTS, steps, 0)


def _na_attention(qkv3, q_g, k_g, rpb, n_ctx):
    bsz, t, d3 = qkv3.shape
    d = d3 // 3
    hd = d // NA_HEADS
    w = 2 * hd
    pairs = d // w
    rows_grid = (t - n_ctx) // GRID_W
    assert w == LANES and rows_grid >= NA_KR and rows_grid % NA_SLOTS == 0
    n_keys = NA_KR * GRID_W + n_ctx
    bias = _na_bias_table(rpb)
    kern = functools.partial(_na_kernel, n_ctx=n_ctx, scale=hd ** -0.5)

    def col(group):
        return pl.BlockSpec((None, t, w), lambda hp, b: (b, 0, group * pairs + hp))

    gain = lambda g: jnp.tile(g.astype(F32), 2).reshape(1, w)
    return pl.pallas_call(
        kern, grid=(pairs, bsz),
        in_specs=[col(0), col(1), col(2),
                  pl.BlockSpec((1, w), lambda hp, b: (0, 0)),
                  pl.BlockSpec((1, w), lambda hp, b: (0, 0)),
                  pl.BlockSpec((None,) + bias.shape[1:], lambda hp, b: (hp, 0, 0, 0))],
        out_specs=pl.BlockSpec((None, t, w), lambda hp, b: (b, 0, hp)),
        out_shape=jax.ShapeDtypeStruct((bsz, t, d), BF16),
        scratch_shapes=[pltpu.VMEM((t, w), BF16)] * 2
        + [pltpu.VMEM((NA_SLOTS, 2 * GRID_W, n_keys), F32), pltpu.VMEM((NA_SLOTS, 2 * GRID_W, n_keys), BF16),
           pltpu.VMEM((NA_SLOTS, 2 * GRID_W, 1), F32)],
        compiler_params=_cparams("parallel", "parallel"), name="na_attention",
    )(qkv3, qkv3, qkv3, gain(q_g), gain(k_g), bias)


FFN_ROWS = 384


def _merge_swiglu_kernel(hg_ref, rg_ref, gate_ref, y_ref, og_ref, wo_ref, h_ref, g1c_ref, g1_ref,
                         g_ref, shc_ref, sh_ref, scc_ref, sc_ref, w1_ref, w3_ref, w2_ref, g2c_ref, g2_ref,
                         o_ref, *, n_ctx, tiles_per_batch):
    tile_in_batch = pl.program_id(0) % tiles_per_batch
    hg = hg_ref[...]
    half = hg.shape[1]
    hn = hg * lax.rsqrt(jnp.mean(hg * hg, axis=-1, keepdims=True) + EPS) * og_ref[...]
    a = (hn * _silu(gate_ref[...])).astype(BF16)
    b = (rg_ref[...] * _gelu_tanh(y_ref[...])).astype(BF16)
    mix = _dot(a, wo_ref[0:half, :].astype(BF16)) + _dot(b, wo_ref[half:, :].astype(BF16))
    h1 = h_ref[...] + _row_gate(g1c_ref, g1_ref, mix.shape, tile_in_batch, n_ctx) * mix
    x = _modulated_rows(h1, g_ref, shc_ref, sh_ref, scc_ref, sc_ref, tile_in_batch, n_ctx).astype(BF16)
    hid = _silu(_dot(x, w1_ref[...].astype(BF16))) * _dot(x, w3_ref[...].astype(BF16))
    y = _dot(hid.astype(BF16), w2_ref[...].astype(BF16))
    o_ref[...] = h1 + _row_gate(g2c_ref, g2_ref, y.shape, tile_in_batch, n_ctx) * y


def _merge_swiglu(hg, rg, p, onorm_g, w_out, h, g, w1, w3, w2, j, mods, n_batch, t, n_ctx):
    n, d = h.shape
    half = hg.shape[1]
    dff = w1.shape[2]
    tm = FFN_ROWS
    tpb = t // tm
    assert t % tm == 0 and n_ctx <= tm
    blk = lambda c: pl.BlockSpec((tm, half), lambda i: (i, c))
    vec = lambda width: pl.BlockSpec((1, width), lambda i: (0, 0))
    kern = functools.partial(_merge_swiglu_kernel, n_ctx=n_ctx, tiles_per_batch=tpb)
    return pl.pallas_call(
        kern, grid=(n // tm,),
        in_specs=[blk(0), blk(0), blk(4), blk(6), vec(half),
                  _resident((None, d, d), lambda i: (j, 0, 0)),
                  pl.BlockSpec((tm, d), lambda i: (i, 0))] + _stream_specs(d, n_batch, tpb, 2, 1)
        + [vec(d)] + _stream_specs(d, n_batch, tpb, 3, 1) + _stream_specs(d, n_batch, tpb, 4, 1)
        + [_resident((None, d, dff), lambda i: (j, 0, 0)),
           _resident((None, d, dff), lambda i: (j, 0, 0)),
           _resident((None, dff, d), lambda i: (j, 0, 0))] + _stream_specs(d, n_batch, tpb, 5, 1),
        out_specs=pl.BlockSpec((tm, d), lambda i: (i, 0)),
        out_shape=jax.ShapeDtypeStruct((n, d), F32),
        compiler_params=_cparams("parallel"), name="merge_swiglu",
    )(hg, rg, p, p, onorm_g.reshape(1, half), w_out, h, mods, mods,
      g, mods, mods, mods, mods, w1, w3, w2, mods, mods)


MOE_ROWS = 512


def _route_tables(route_t, counts, tm, n_exp):
    r = route_t.shape[1]
    experts = route_t[0:2].astype(jnp.int32)
    ranks = route_t[4:6].astype(jnp.int32)
    padded = ((counts + tm - 1) // tm) * tm
    ends = jnp.cumsum(padded)
    starts = ends - padded
    slot = ranks
    for e in range(n_exp):
        slot = slot + jnp.where(experts == e, starts[e], 0)
    p_slots = 2 * r + n_exp * tm
    n_tiles = p_slots // tm
    n_used = ends[-1] // tm
    tile = jnp.arange(n_tiles, dtype=jnp.int32)
    te = jnp.sum(((tile * tm)[:, None] >= ends[None, :]).astype(jnp.int32), axis=1)
    te = jnp.where(tile < n_used, te, te[n_used - 1])
    return slot, p_slots, te, n_used.reshape(1)


def _gather_rows(table, idx):
    p = idx.shape[0]
    m = table.shape[1]
    mesh = plsc.VectorSubcoreMesh(core_axis_name="c", subcore_axis_name="s")

    @functools.partial(pl.kernel, out_type=jax.ShapeDtypeStruct((p, m), table.dtype), mesh=mesh)
    def gather(table_hbm, idx_hbm, out_hbm):
        def body(idx_vmem, out_vmem):
            pltpu.sync_copy(table_hbm.at[idx_vmem.at[0]], out_vmem)

        pltpu.emit_pipeline(
            body, grid=(p // SC_WINDOW,),
            in_specs=[pl.BlockSpec((1, SC_WINDOW), lambda i: (0, i))],
            out_specs=[pl.BlockSpec((SC_WINDOW, m), lambda i: (i, 0))],
            core_axis_name=("c", "s"), dimension_semantics=(pltpu.PARALLEL,),
        )(idx_hbm, out_hbm)

    return gather(table, idx.reshape(1, p))


def _scatter_rows(rows, idx_a, idx_b, n_out):
    q, m = rows.shape
    mesh = plsc.VectorSubcoreMesh(core_axis_name="c", subcore_axis_name="s")

    @functools.partial(pl.kernel, out_type=jax.ShapeDtypeStruct((n_out, m), rows.dtype), mesh=mesh)
    def scatter(rows_hbm, ia_hbm, ib_hbm, out_hbm):
        def body(rows_vmem, ia_vmem, ib_vmem):
            pltpu.sync_copy(rows_vmem, out_hbm.at[ia_vmem.at[0]])
            pltpu.sync_copy(rows_vmem, out_hbm.at[ib_vmem.at[0]])

        pltpu.emit_pipeline(
            body, grid=(q // SC_WINDOW,),
            in_specs=[pl.BlockSpec((SC_WINDOW, m), lambda i: (i, 0)),
                      pl.BlockSpec((1, SC_WINDOW), lambda i: (0, i)),
                      pl.BlockSpec((1, SC_WINDOW), lambda i: (0, i))],
            out_specs=[],
            core_axis_name=("c", "s"), dimension_semantics=(pltpu.PARALLEL,),
        )(rows_hbm, ia_hbm, ib_hbm)

    return scatter(rows, idx_a.reshape(1, q), idx_b.reshape(1, q))


def _scatter_pieces(pieces, slot, n_slots):
    npc, r, m = pieces.shape
    offs = jnp.arange(npc, dtype=jnp.int32)[:, None] * n_slots
    idx_a = (slot[0][None, :] + offs).reshape(-1)
    idx_b = (slot[1][None, :] + offs).reshape(-1)
    return _scatter_rows(pieces.reshape(npc * r, m), idx_a, idx_b, npc * n_slots).reshape(npc, n_slots, m)


def _gather_pieces(pieces, idx):
    npc, v, m = pieces.shape
    offs = (jnp.arange(npc, dtype=jnp.int32) * v).reshape((npc,) + (1,) * idx.ndim)
    flat = (idx[None] + offs).reshape(-1)
    return _gather_rows(pieces.reshape(npc * v, m), flat).reshape((npc,) + idx.shape + (m,))


def _moe_ffn_kernel(te_ref, nu_ref, x_ref, w1_ref, w3_ref, w2_ref, y_ref):
    @pl.when(pl.program_id(0) < nu_ref[0])
    def _():
        x = _load_unpacked(x_ref).astype(BF16)
        hid = _silu(_dot(x, w1_ref[...].astype(BF16))) * _dot(x, w3_ref[...].astype(BF16))
        _store_pieces(y_ref, _pack_halves(_dot(hid.astype(BF16), w2_ref[...].astype(BF16))))


def _moe_ffn(xs, te, n_used, w1, w3, w2, j):
    npc, p, m = xs.shape
    d = 2 * npc * m
    dff = w1.shape[3]
    tm = MOE_ROWS

    def row(i, te_ref, nu_ref):
        return (0, jnp.minimum(i, nu_ref[0] - 1), 0)

    def expert(i, te_ref, nu_ref):
        return (j, te_ref[i], 0, 0)

    grid_spec = pltpu.PrefetchScalarGridSpec(
        num_scalar_prefetch=2, grid=(p // tm,),
        in_specs=[pl.BlockSpec((npc, tm, m), row),
                  _resident((None, None, d, dff), expert),
                  _resident((None, None, d, dff), expert),
                  _resident((None, None, dff, d), expert)],
        out_specs=pl.BlockSpec((npc, tm, m), row))
    return pl.pallas_call(
        _moe_ffn_kernel, grid_spec=grid_spec,
        out_shape=jax.ShapeDtypeStruct((npc, p, m), jnp.int32),
        compiler_params=_cparams("arbitrary"), name="moe_ffn",
    )(te, n_used, xs, w1, w3, w2)


def _moe_combine_kernel(ya_ref, yb_ref, route_ref, h_ref, gt_ref, o_ref):
    r = route_ref[...]
    y = r[:, 2:3] * _load_unpacked(ya_ref) + r[:, 3:4] * _load_unpacked(yb_ref)
    o_ref[...] = h_ref[...] + gt_ref[...] * y


def _moe_combine(yg, route, h, mods, n_batch, tiles_per_batch, ctx_tiles):
    d = h.shape[1]
    npc, _, r, m = yg.shape
    tm = TOKEN_TILE
    per = tiles_per_batch - ctx_tiles
    h_idx = lambda i: ((i // per) * tiles_per_batch + ctx_tiles + i % per, 0)
    if ctx_tiles == 0:
        mod_idx = _mod_row_spec(5, tiles_per_batch, n_batch)
    else:
        mod_idx = lambda i: (i // per, 5, 0, 0)
    return pl.pallas_call(
        _moe_combine_kernel, grid=(r // tm,),
        in_specs=[pl.BlockSpec((npc, None, tm, m), lambda i: (0, 0, i, 0)),
                  pl.BlockSpec((npc, None, tm, m), lambda i: (0, 1, i, 0)),
                  pl.BlockSpec((tm, LANES), lambda i: (i, 0)),
                  pl.BlockSpec((tm, d), h_idx),
                  pl.BlockSpec((None, None, 1, d), mod_idx)],
        out_specs=pl.BlockSpec((tm, d), lambda i: (i, 0)),
        out_shape=jax.ShapeDtypeStruct((r, d), F32),
        compiler_params=_cparams("parallel"), name="moe_combine",
    )(yg, yg, route, h, mods)


def _moe_residual(h, g, w_router, w1, w3, w2, j, mods, n_batch, t, n_ctx, latent_only):
    tpb = t // TOKEN_TILE
    ctx_tiles = n_ctx // TOKEN_TILE if latent_only else 0
    u_pieces, route, route_t, counts = _modulate_router(h, g, mods, 3, n_batch, tpb, ctx_tiles, w_router)
    if latent_only:
        npc, _, m = u_pieces.shape
        u_pieces = u_pieces.reshape(npc, n_batch, t, m)[:, :, n_ctx:].reshape(npc, -1, m)
        route = route.reshape(n_batch, t, LANES)[:, n_ctx:].reshape(-1, LANES)
        route_t = route_t.reshape(-1, n_batch, t)[:, :, n_ctx:].reshape(route_t.shape[0], -1)
    n_exp = w1.shape[1]
    slot, n_slots, te, n_used = _route_tables(route_t, counts[0, :n_exp].astype(jnp.int32), MOE_ROWS, n_exp)
    xs = _scatter_pieces(u_pieces, slot, n_slots)
    ys = _moe_ffn(xs, te, n_used, w1, w3, w2, j)
    yg = _gather_pieces(ys, slot)
    return _moe_combine(yg, route, h, mods, n_batch, tpb, ctx_tiles)


def kernel(x, c, ctx, c_ctx, w_mod, b_mod, norm_mix_g, norm_ffn_g, ev_w_in, ev_w_out, hg_lb_logits, hg_onorm_g, rg_conv_w, rg_conv_b, rg_wa, rg_ba, rg_wx, rg_bx, rg_lambda, na_w_qkv, na_w_o, na_q_g, na_k_g, na_rpb, ffn_w1, ffn_w3, ffn_w2, moe_router, moe_w1, moe_w3, moe_w2):
    bsz, seq, d = x.shape
    n_ctx = ctx.shape[1]
    t = n_ctx + seq
    depth = w_mod.shape[0]
    assert n_ctx % TOKEN_TILE == 0 and seq % TOKEN_TILE == 0 and t % ROW_TILE == 0 and n_ctx <= ROW_TILE

    rows = -(-(bsz + 1) // 8) * 8
    cond = jnp.zeros((rows, d), F32).at[:bsz].set(c).at[bsz].set(c_ctx)
    mods_all = _adaln(cond, w_mod, b_mod).reshape(depth, rows, N_MOD, 1, d)

    lb_p = jax.nn.softmax(hg_lb_logits.astype(F32), axis=0)
    lb_all = jnp.cumsum(lb_p, axis=0) - lb_p[0]

    h = jnp.concatenate([ctx, x], axis=1).reshape(bsz * t, d)
    for layer in range(depth):
        j = layer // 2
        mods = mods_all[layer]
        g_mix = norm_mix_g[layer].reshape(1, d)
        if layer % 2 == 0:
            p = _modulated_matmul(h, g_mix, mods, ev_w_in, j, F32, ROW_TILE // 2, bsz, t, n_ctx)
            p3 = p.reshape(bsz, t, -1)
            hg = _hgrn2(p3, lb_all[j], n_ctx)
            half = hg.shape[-1]
            rg = _rglru(p3, 5 * half // (d - half), rg_conv_w[j], rg_conv_b[j], rg_wa[j], rg_ba[j],
                        rg_wx[j], rg_bx[j], rg_lambda[j], n_ctx)
            h = _merge_swiglu(hg.reshape(bsz * t, half), rg.reshape(bsz * t, d - half), p, hg_onorm_g[j],
                              ev_w_out, h, norm_ffn_g[layer].reshape(1, d), ffn_w1, ffn_w3, ffn_w2, j,
                              mods, bsz, t, n_ctx)
        else:
            qkv = _modulated_matmul(h, g_mix, mods, na_w_qkv, j, BF16, ROW_TILE, bsz, t, n_ctx)
            o = _na_attention(qkv.reshape(bsz, t, 3 * d), na_q_g[j], na_k_g[j], na_rpb[j], n_ctx)
            h = _matmul_residual(o.reshape(bsz * t, d), na_w_o, j, h, mods, 2, bsz, t, n_ctx)
            h = _moe_residual(h, norm_ffn_g[layer].reshape(1, d), moe_router[j], moe_w1, moe_w3, moe_w2, j, mods,
                              bsz, t, n_ctx, layer == depth - 1)
    if depth % 2 == 0:
        return h.reshape(bsz, seq, d)
    return h.reshape(bsz, t, d)[:, n_ctx:]
```

```python
import functools

import numpy as np
import jax
import jax.numpy as jnp
from jax import lax
from jax.experimental import pallas as pl
from jax.experimental.pallas import tpu as pltpu
from jax.experimental.pallas import tpu_sc as plsc

F32 = jnp.float32
BF16 = jnp.bfloat16

EPS = 1e-6
NEG_BIG = -1e9
LB_MIN = 1e-30
LOG2_E = 1.4426950408889634
N_MOD = 6
GRID_W = 64
HG_HEAD_DIM = 128
RG_BLOCKS = 8
RG_CONV = 4
RG_C = 8.0
NA_HEADS = 16
NA_KR = 8
NA_KC = 16
N_EXPERTS = 8

LANES = 128
SC_WINDOW = 128
SC_PIECES = 2
TOKEN_TILE = 256
HG_CHUNK = 128
HG_LEVELS = 7
HG_SLOTS = 2
NA_SLOTS = 4
RG_SCAN_BLOCK = 256
RG_GROUP = 8
VMEM_LIMIT = 56 * 1024 * 1024


def _cparams(*sem):
    return pltpu.CompilerParams(dimension_semantics=sem, vmem_limit_bytes=VMEM_LIMIT)


def _resident(block_shape, index_map):
    return pl.BlockSpec(block_shape, index_map, pipeline_mode=pl.Buffered(1))


def _dot(a, b):
    return jnp.dot(a, b, preferred_element_type=F32)


def _dot_nt(a, b):
    return lax.dot_general(a, b, (((1,), (1,)), ((), ())), preferred_element_type=F32)


def _dot_tn(a, b):
    return lax.dot_general(a, b, (((0,), (0,)), ((), ())), preferred_element_type=F32)


def _split2(x):
    hi = x.astype(BF16)
    lo = (x - hi.astype(F32)).astype(BF16)
    return hi, lo


def _sigmoid(x):
    return 0.5 * jnp.tanh(0.5 * x) + 0.5


def _silu(x):
    return x * _sigmoid(x)


def _softplus_neg_abs(x):
    return jnp.log(1.0 + jnp.exp(-jnp.abs(x)))


def _log_sigmoid(x):
    return jnp.minimum(x, 0.0) - _softplus_neg_abs(x)


def _gelu_tanh(x):
    return 0.5 * x * (1.0 + jnp.tanh(np.sqrt(2.0 / np.pi) * (x + 0.044715 * (x * x * x))))


def _adaln_kernel(c_ref, w_ref, b_ref, o_ref):
    a_hi, a_lo = _split2(_silu(c_ref[...]))
    w_hi, w_lo = _split2(w_ref[...])
    o_ref[...] = _dot(a_hi, w_hi) + _dot(a_hi, w_lo) + _dot(a_lo, w_hi) + b_ref[...]


def _adaln(cond, w_mod, b_mod):
    depth, d, n = w_mod.shape
    r = cond.shape[0]
    tn = 1536
    return pl.pallas_call(
        _adaln_kernel,
        grid=(depth, n // tn),
        in_specs=[pl.BlockSpec((r, d), lambda l, j: (0, 0)),
                  pl.BlockSpec((None, d, tn), lambda l, j: (l, 0, j)),
                  pl.BlockSpec((None, 1, tn), lambda l, j: (l, 0, j))],
        out_specs=pl.BlockSpec((None, r, tn), lambda l, j: (l, 0, j)),
        out_shape=jax.ShapeDtypeStruct((depth, r, n), F32),
        compiler_params=_cparams("parallel", "parallel"),
        name="adaln",
    )(cond, w_mod, b_mod.reshape(depth, 1, n))


def _mod_row_spec(k, tiles_per_batch, n_batch):
    def idx(i):
        row = jnp.where(i % tiles_per_batch == 0, n_batch, i // tiles_per_batch)
        return (row, k, 0, 0)
    return idx


def _pack_halves(x):
    m = x.shape[1] // 2
    lo = lax.bitcast_convert_type(x[:, :m].astype(BF16).astype(F32), jnp.int32)
    hi = lax.bitcast_convert_type(x[:, m:].astype(BF16).astype(F32), jnp.int32)
    return lax.shift_right_logical(lo, 16) | (hi & jnp.int32(-65536))


def _unpack_halves(w):
    lo = lax.bitcast_convert_type(lax.shift_left(w, 16), F32)
    hi = lax.bitcast_convert_type(w & jnp.int32(-65536), F32)
    return lo, hi


def _store_pieces(ref, w):
    m = w.shape[1] // SC_PIECES
    for j in range(SC_PIECES):
        ref[j] = w[:, j * m:(j + 1) * m]


def _load_unpacked(ref):
    parts = [_unpack_halves(ref[j]) for j in range(SC_PIECES)]
    return jnp.concatenate([lo for lo, _ in parts] + [hi for _, hi in parts], axis=1)


def _proj_router_kernel(o_ref, wo_ref, h_ref, gt_ref, g_ref, sh_ref, sc_ref, wr_ref,
                        h1_ref, u_ref, route_ref, route_t_ref, cnt_ref, *, tiles_per_batch, skip_tiles):
    i = pl.program_id(0)

    @pl.when(i == 0)
    def _():
        cnt_ref[...] = jnp.zeros_like(cnt_ref)

    h1 = h_ref[...] + gt_ref[...] * _dot(o_ref[...], wo_ref[...].astype(BF16))
    h1_ref[...] = h1
    y = h1 * lax.rsqrt(jnp.mean(h1 * h1, axis=-1, keepdims=True) + EPS)
    u = (y * g_ref[...]) * (1.0 + sc_ref[...]) + sh_ref[...]
    _store_pieces(u_ref, _pack_halves(u))
    u0, u1 = _split2(u)
    w0, w1 = _split2(wr_ref[...])
    logits = _dot(u0, w0) + _dot(u0, w1) + _dot(u1, w0)
    lane = lax.broadcasted_iota(jnp.int32, logits.shape, 1).astype(F32)
    lg = jnp.where(lane < N_EXPERTS, logits, -jnp.inf)
    m1 = jnp.max(lg, axis=-1, keepdims=True)
    i1 = jnp.min(jnp.where(lg == m1, lane, float(LANES)), axis=-1, keepdims=True)
    lg2 = jnp.where(lane == i1, -jnp.inf, lg)
    m2 = jnp.max(lg2, axis=-1, keepdims=True)
    i2 = jnp.min(jnp.where(lg2 == m2, lane, float(LANES)), axis=-1, keepdims=True)
    e2 = jnp.exp(m2 - m1)
    den = 1.0 + e2

    routed = jnp.where(i % tiles_per_batch >= skip_tiles, 1.0, 0.0)
    pick_a = jnp.where(lane == i1, routed, 0.0)
    pick_b = jnp.where(lane == i2, routed, 0.0)
    tm = pick_a.shape[0]
    earlier = jnp.where(lax.broadcasted_iota(jnp.int32, (tm, tm), 1) < lax.broadcasted_iota(jnp.int32, (tm, tm), 0),
                        1.0, 0.0).astype(BF16)
    total_a = jnp.sum(pick_a, axis=0, keepdims=True)
    base = cnt_ref[0:1, :]
    rank_a = jnp.sum(pick_a * (base + _dot(earlier, pick_a.astype(BF16))), axis=-1, keepdims=True)
    rank_b = jnp.sum(pick_b * (base + total_a + _dot(earlier, pick_b.astype(BF16))), axis=-1, keepdims=True)
    cnt_ref[0:1, :] = base + total_a + jnp.sum(pick_b, axis=0, keepdims=True)

    route = jnp.where(lane == 0, i1, jnp.where(lane == 1, i2, jnp.where(lane == 2, 1.0 / den, jnp.where(
        lane == 3, e2 / den, jnp.where(lane == 4, rank_a, rank_b)))))
    route_ref[...] = route
    route_t_ref[...] = route.T[0:route_t_ref.shape[0], :]


def _proj_router(o, w_o, j, h, g, mods, n_batch, tiles_per_batch, skip_tiles, w_router):
    n, d = h.shape
    tm = TOKEN_TILE
    row = pl.BlockSpec((tm, d), lambda i: (i, 0))
    vec = pl.BlockSpec((1, d), lambda i: (0, 0))

    def mod_spec(k):
        return pl.BlockSpec((None, None, 1, d), _mod_row_spec(k, tiles_per_batch, n_batch))

    wr = jnp.zeros((d, LANES), F32).at[:, :N_EXPERTS].set(w_router)
    kern = functools.partial(_proj_router_kernel, tiles_per_batch=tiles_per_batch, skip_tiles=skip_tiles)
    return pl.pallas_call(
        kern, grid=(n // tm,),
        in_specs=[row, _resident((None, d, d), lambda i: (j, 0, 0)), row, mod_spec(2),
                  vec, mod_spec(3), mod_spec(4), pl.BlockSpec((d, LANES), lambda i: (0, 0))],
        out_specs=[row,
                   pl.BlockSpec((SC_PIECES, tm, d // 2 // SC_PIECES), lambda i: (0, i, 0)),
                   pl.BlockSpec((tm, LANES), lambda i: (i, 0)),
                   pl.BlockSpec((8, tm), lambda i: (0, i)),
                   pl.BlockSpec((8, LANES), lambda i: (0, 0))],
        out_shape=[jax.ShapeDtypeStruct((n, d), F32),
                   jax.ShapeDtypeStruct((SC_PIECES, n, d // 2 // SC_PIECES), jnp.int32),
                   jax.ShapeDtypeStruct((n, LANES), F32),
                   jax.ShapeDtypeStruct((8, n), F32),
                   jax.ShapeDtypeStruct((8, LANES), F32)],
        compiler_params=_cparams("arbitrary"), name="proj_router",
    )(o, w_o, h, mods, g, mods, mods, wr)


ROW_TILE = 768


def _row_gate(ctx_ref, own_ref, shape, tile_in_batch, n_ctx):
    rows = lax.broadcasted_iota(jnp.int32, shape, 0)
    is_ctx = (rows < n_ctx) & (tile_in_batch == 0)
    return jnp.where(is_ctx, ctx_ref[...], own_ref[...])


def _modulated_rows(x, g_ref, shc_ref, sh_ref, scc_ref, sc_ref, tile_in_batch, n_ctx):
    y = x * lax.rsqrt(jnp.mean(x * x, axis=-1, keepdims=True) + EPS)
    shift = _row_gate(shc_ref, sh_ref, x.shape, tile_in_batch, n_ctx)
    scale = _row_gate(scc_ref, sc_ref, x.shape, tile_in_batch, n_ctx)
    return (y * g_ref[...]) * (1.0 + scale) + shift


def _stream_specs(d, n_batch, tiles_per_batch, k, nargs):
    if nargs == 1:
        ctx = lambda i: (n_batch, k, 0, 0)
        own = lambda i: (i // tiles_per_batch, k, 0, 0)
    else:
        ctx = lambda i, f: (n_batch, k, 0, 0)
        own = lambda i, f: (i // tiles_per_batch, k, 0, 0)
    return [pl.BlockSpec((None, None, 1, d), ctx), pl.BlockSpec((None, None, 1, d), own)]


def _mod_mm_kernel(h_ref, g_ref, shc_ref, sh_ref, scc_ref, sc_ref, w_ref, o_ref, *, n_ctx, tiles_per_batch):
    x = _modulated_rows(h_ref[...], g_ref, shc_ref, sh_ref, scc_ref, sc_ref,
                        pl.program_id(0) % tiles_per_batch, n_ctx).astype(BF16)
    o_ref[...] = _dot(x, w_ref[...].astype(BF16)).astype(o_ref.dtype)


def _modulated_matmul(h, g, mods, w, j, out_dtype, tm, n_batch, t, n_ctx):
    n, d = h.shape
    nout = w.shape[2]
    tpb = t // tm
    assert t % tm == 0 and n_ctx <= tm
    kern = functools.partial(_mod_mm_kernel, n_ctx=n_ctx, tiles_per_batch=tpb)
    return pl.pallas_call(
        kern, grid=(n // tm,),
        in_specs=[pl.BlockSpec((tm, d), lambda i: (i, 0)),
                  pl.BlockSpec((1, d), lambda i: (0, 0))]
        + _stream_specs(d, n_batch, tpb, 0, 1) + _stream_specs(d, n_batch, tpb, 1, 1)
        + [_resident((None, d, nout), lambda i: (j, 0, 0))],
        out_specs=pl.BlockSpec((tm, nout), lambda i: (i, 0)),
        out_shape=jax.ShapeDtypeStruct((n, nout), out_dtype),
        compiler_params=_cparams("parallel"), name="modulated_matmul",
    )(h, g, mods, mods, mods, mods, w)


def _hg_tables():
    c = HG_CHUNK
    t = np.arange(c)[:, None]
    u = np.arange(c)[None, :]
    mats = [(u <= t), (u > t)]
    for k in range(1, HG_LEVELS + 1):
        m = ((t >> k) << k) + (1 << (k - 1)) - 1
        upper = t > m
        mats.append(np.where(upper, (u > m) & (u <= t), (u > t) & (u <= m)))
    fwd = np.concatenate(mats, axis=0).astype(np.float32)
    bwd = fwd.reshape(-1, c, c)[:, ::-1, ::-1].reshape(-1, c)
    x = t ^ u
    lvl = np.where(t > u, np.floor(np.log2(np.maximum(x, 1))).astype(np.int32) + 1, -1)
    lvl = np.where(t == u, 0, lvl).astype(np.int32)
    return np.stack([fwd, bwd]), np.stack([lvl, lvl.T])


def _hg_kernel(q_ref, ff_ref, fb_ref, v_ref, lb_ref, m_ref, lvl_ref, o_ref, of_ref, ob_ref,
               qk_ref, el_ref, io_ref, tot_ref, att_ref, *, n_ctx_chunks, n_chunks):
    c = HG_CHUNK
    gap = HG_SLOTS // 2
    t = o_ref.shape[0]
    z_refs = (ff_ref, fb_ref)
    out_refs = (of_ref, ob_ref)
    for ref in (qk_ref, el_ref, io_ref, tot_ref, att_ref):
        ref[...] = jnp.zeros(ref.shape, ref.dtype)

    def start_row(d, j):
        j = jnp.clip(j, 0, n_chunks - 1)
        if d == 1:
            j = jnp.where(j < n_ctx_chunks, n_ctx_chunks - 1 - j, n_chunks - 1 - (j - n_ctx_chunks))
        return pl.multiple_of(j * c, c)

    def stage1(d, j, slot):
        t0 = start_row(d, j)
        lb = lb_ref[d:d + 1, :]
        log_lb = jnp.log(jnp.maximum(lb, LB_MIN))
        q = _silu(q_ref[pl.ds(t0, c), :])
        z = z_refs[d][pl.ds(t0, c), :]
        k = (1.0 - lb) * _sigmoid(-z)
        b = jnp.log1p(-lb) + _log_sigmoid(z)
        logf = jnp.maximum(log_lb, b) + _softplus_neg_abs(log_lb - b)
        g_hi, g_lo = _split2(logf * LOG2_E)
        m_ends = m_ref[d, 0:2 * c, :]
        e_ends = jnp.exp2(_dot(m_ends, g_hi) + _dot(m_ends, g_lo))
        el_ref[d, slot] = jnp.exp2(_dot(m_ref[d, 2 * c:, :], g_hi).astype(BF16))
        qk_ref[d, slot, 0] = q.astype(BF16)
        qk_ref[d, slot, 1] = k.astype(BF16)
        io_ref[d, slot, 0] = (q * e_ends[0:c]).astype(BF16)
        io_ref[d, slot, 1] = (k * e_ends[c:]).astype(BF16)
        tot_row = c - 1 if d == 0 else 0
        tot_ref[d, slot, 0:1, :] = e_ends[tot_row:tot_row + 1, :]

    def stage2(d, slot):
        qb = qk_ref[d, slot, 0]
        kb = qk_ref[d, slot, 1]
        lvl = lvl_ref[d]
        att = jnp.where(lvl == 0, _dot_nt(qb, kb), 0.0)
        for lev in range(1, HG_LEVELS + 1):
            el = el_ref[d, slot, (lev - 1) * c:lev * c, :]
            att = jnp.where(lvl == lev, _dot_nt(qb * el, kb * el), att)
        att_ref[d, slot] = att.astype(BF16)

    def stage3(d, j, slot, st):
        t0 = start_row(d, j)
        v = v_ref[pl.ds(t0, c), :].astype(BF16)
        row = jnp.where(j < n_chunks, t0, t)
        out_refs[d][pl.ds(pl.multiple_of(row, c), c), :] = (
            _dot(att_ref[d, slot], v) + _dot_nt(io_ref[d, slot, 0], st.astype(BF16)))
        return st * tot_ref[d, slot, 0:1, :] + _dot_tn(v, io_ref[d, slot, 1])

    def steps(jj, carry):
        st = list(carry)
        for slot in range(HG_SLOTS):
            i = HG_SLOTS * jj + slot
            for d in range(2):
                st[d] = stage3(d, i - 2 * gap, slot, st[d])
            for d in range(2):
                stage2(d, (slot + gap) % HG_SLOTS)
            for d in range(2):
                stage1(d, i, slot)
        return tuple(st)

    zero = jnp.zeros((c, c), F32)
    n_steps = -(-(n_chunks + 2 * gap) // HG_SLOTS)
    lax.fori_loop(0, n_steps, steps, (zero, zero))
    o_ref[...] = (of_ref[0:t, :] + ob_ref[0:t, :]).astype(o_ref.dtype)


def _hgrn2(p3, lb, n_ctx):
    bsz, t, _ = p3.shape
    width = lb.shape[1]
    heads = width // HG_HEAD_DIM
    mats, lvl = _hg_tables()
    c = HG_CHUNK

    def col(group):
        return pl.BlockSpec((None, t, c), lambda b, h: (b, 0, group * heads + h))

    kern = functools.partial(_hg_kernel, n_ctx_chunks=n_ctx // c, n_chunks=t // c)
    return pl.pallas_call(
        kern, grid=(bsz, heads),
        in_specs=[col(0), col(1), col(2), col(3),
                  pl.BlockSpec((2, c), lambda b, h: (0, h)),
                  pl.BlockSpec(mats.shape, lambda b, h: (0, 0, 0)),
                  pl.BlockSpec(lvl.shape, lambda b, h: (0, 0, 0))],
        out_specs=pl.BlockSpec((None, t, c), lambda b, h: (b, 0, h)),
        out_shape=jax.ShapeDtypeStruct((bsz, t, width), F32),
        scratch_shapes=[pltpu.VMEM((t + c, c), F32), pltpu.VMEM((t + c, c), F32),
                        pltpu.VMEM((2, HG_SLOTS, 2, c, c), BF16), pltpu.VMEM((2, HG_SLOTS, HG_LEVELS * c, c), BF16),
                        pltpu.VMEM((2, HG_SLOTS, 2, c, c), BF16), pltpu.VMEM((2, HG_SLOTS, 8, c), F32),
                        pltpu.VMEM((2, HG_SLOTS, c, c), BF16)],
        compiler_params=_cparams("parallel", "parallel"), name="hgrn2",
    )(p3, p3, p3, p3, lb, jnp.asarray(mats, BF16), jnp.asarray(lvl))


PAD = 8


def _rg_kernel(x_ref, cw_ref, cb_ref, wa_ref, ba_ref, wx_ref, bx_ref, lam_ref, o_ref,
               xpad_ref, xc_ref, ob_ref, *, n_ctx):
    t, w = x_ref.shape
    n_lat = t - n_ctx
    lat0 = n_ctx + 2 * PAD
    xpad_ref[...] = jnp.zeros(xpad_ref.shape, F32)
    xpad_ref[PAD:PAD + n_ctx, :] = x_ref[0:n_ctx, :]
    xpad_ref[lat0:lat0 + n_lat, :] = x_ref[n_ctx:t, :]
    left = RG_CONV // 2
    for base, dst, n in ((PAD, 0, n_ctx), (lat0, n_ctx, n_lat)):
        acc = cb_ref[...] + cw_ref[0:1, :] * xpad_ref[base - left:base - left + n, :]
        for j in range(1, RG_CONV):
            acc = acc + cw_ref[j:j + 1, :] * xpad_ref[base - left + j:base - left + j + n, :]
        xc_ref[dst:dst + n, :] = acc

    blk = RG_SCAN_BLOCK
    n_blocks = t // blk
    n_ctx_blocks = n_ctx // blk
    sub = lax.broadcasted_iota(jnp.int32, (blk // RG_GROUP, RG_GROUP, w), 1)
    c_log_sig_lam = RG_C * _log_sigmoid(lam_ref[...])

    def scan_block(d, t0, hin, out_ref):
        xc = xc_ref[pl.ds(t0, blk), :]
        xb = xc.astype(BF16)
        r = _sigmoid(_dot(xb, wa_ref[d]) + ba_ref[d:d + 1, :])
        ig = _sigmoid(_dot(xb, wx_ref[d]) + bx_ref[d:d + 1, :])
        log_a = r * c_log_sig_lam[d:d + 1, :]
        a = jnp.exp(log_a)
        b = jnp.sqrt(jnp.maximum(1.0 - a * a, 0.0)) * (ig * xc)
        n_groups = blk // RG_GROUP
        a = a.reshape(n_groups, RG_GROUP, w)
        b = b.reshape(n_groups, RG_GROUP, w)
        s = 1
        while s < RG_GROUP:
            shift = s if d == 0 else RG_GROUP - s
            keep = (sub >= s) if d == 0 else (sub < RG_GROUP - s)
            b = jnp.where(keep, a * pltpu.roll(b, shift, 1), 0.0) + b
            a = jnp.where(keep, a * pltpu.roll(a, shift, 1), a)
            s *= 2
        for gi in range(n_groups):
            gj = gi if d == 0 else n_groups - 1 - gi
            hg = a[gj] * hin + b[gj]
            out_ref[pl.ds(t0 + gj * RG_GROUP, RG_GROUP), :] = hg
            hin = hg[RG_GROUP - 1:RG_GROUP, :] if d == 0 else hg[0:1, :]
        return hin

    def step(i, carry):
        h_f, h_b = carry
        bb = jnp.where(i < n_ctx_blocks, n_ctx_blocks - 1 - i, n_blocks - 1 - (i - n_ctx_blocks))
        h_f = scan_block(0, pl.multiple_of(i * blk, blk), h_f, o_ref)
        h_b = scan_block(1, pl.multiple_of(bb * blk, blk), h_b, ob_ref)
        return h_f, h_b

    zero = jnp.zeros((1, w), F32)
    lax.fori_loop(0, n_blocks, step, (zero, zero))
    o_ref[...] += ob_ref[...]


def _blockdiag_dense(w):
    nd, nb, k, _ = w.shape
    eye = jnp.eye(nb, dtype=w.dtype)
    return jnp.einsum('dnij,nm->dnimj', w, eye).reshape(nd, nb * k, nb * k)


def _rglru(p3, col_block, conv_w, conv_b, wa, ba, wx, bx, lam, n_ctx):
    bsz, t, _ = p3.shape
    w = conv_w.shape[1]
    full = lambda shape: pl.BlockSpec(shape, lambda b: (0,) * len(shape))
    kern = functools.partial(_rg_kernel, n_ctx=n_ctx)
    return pl.pallas_call(
        kern, grid=(bsz,),
        in_specs=[pl.BlockSpec((None, t, w), lambda b: (b, 0, col_block)),
                  full((RG_CONV, w)), full((1, w)),
                  full((2, w, w)), full((2, w)), full((2, w, w)), full((2, w)), full((2, w))],
        out_specs=pl.BlockSpec((None, t, w), lambda b: (b, 0, 0)),
        out_shape=jax.ShapeDtypeStruct((bsz, t, w), F32),
        scratch_shapes=[pltpu.VMEM((t + 3 * PAD, w), F32)] + [pltpu.VMEM((t, w), F32)] * 2,
        compiler_params=_cparams("parallel"), name="rglru",
    )(p3, conv_w, conv_b.reshape(1, w), _blockdiag_dense(wa).astype(BF16), ba,
      _blockdiag_dense(wx).astype(BF16), bx, lam)


def _na_bias_table(rpb):
    q = np.arange(GRID_W)[:, None]
    kc = np.arange(GRID_W)[None, :]
    wstart = np.clip(q - NA_KC // 2, 0, GRID_W - NA_KC)
    mask = (kc >= wstart) & (kc < wstart + NA_KC)
    coff = np.clip(kc - q + NA_KC - 1, 0, 2 * NA_KC - 2)
    sel = (coff[None] == np.arange(2 * NA_KC - 1)[:, None, None]).astype(np.float32)
    toe = jnp.einsum('hdj,jqk->hqdk', rpb.astype(F32), sel, precision=lax.Precision.HIGHEST)
    toe = jnp.where(mask[None, :, None, :], toe, NEG_BIG)
    h = toe.shape[0]
    flat = toe.reshape(h // 2, 2 * GRID_W, -1)
    width = (2 * NA_KR) * GRID_W
    even = jnp.pad(flat, ((0, 0), (0, 0), (0, width - flat.shape[2])))
    odd = jnp.pad(flat[:, :, GRID_W:], ((0, 0), (0, 0), (0, width - flat.shape[2] + GRID_W)))
    return jnp.stack([even, odd], axis=1)


def _stack_heads(x, hd):
    lane = lax.broadcasted_iota(jnp.int32, x.shape, 1)
    zero = jnp.zeros_like(x)
    return jnp.concatenate([jnp.where(lane < hd, x, zero), jnp.where(lane >= hd, x, zero)], axis=0)


def _unstack_heads(o2, hd):
    n = o2.shape[0] // 2
    lane = lax.broadcasted_iota(jnp.int32, (n, o2.shape[1]), 1)
    return jnp.where(lane < hd, o2[:n], o2[n:])


def _na_kernel(q_ref, k_ref, v_ref, qg_ref, kg_ref, bias_ref, o_ref, qn_ref, kn_ref,
               s_ref, p_ref, den_ref, *, n_ctx, scale):
    t, w = q_ref.shape
    hd = w // 2
    rows_grid = (t - n_ctx) // GRID_W
    r_i = lax.broadcasted_iota(jnp.int32, (w, w), 0)
    c_i = lax.broadcasted_iota(jnp.int32, (w, w), 1)
    seg = jnp.where((r_i // hd) == (c_i // hd), 1.0 / hd, 0.0).astype(BF16)

    def normed(x, g):
        ms = _dot((x * x).astype(BF16), seg)
        return (x * lax.rsqrt(ms + EPS) * g).astype(BF16)

    qn_ref[...] = normed(q_ref[...].astype(F32), qg_ref[...] * scale)
    kn_ref[...] = normed(k_ref[...].astype(F32), kg_ref[...])
    vb_ref = v_ref

    kc = kn_ref[0:n_ctx, :]
    vc = vb_ref[0:n_ctx, :]

    s = _dot_nt(_stack_heads(qn_ref[0:n_ctx, :], hd), kc)
    p = jnp.exp(s - jnp.max(s, axis=-1, keepdims=True))
    o2 = _dot(p.astype(BF16), vc) / jnp.sum(p, axis=-1, keepdims=True)
    o_ref[0:n_ctx, :] = _unstack_heads(o2, hd).astype(o_ref.dtype)

    kr = NA_KR
    n_win = kr * GRID_W

    def window(r):
        r = jnp.clip(r, 0, rows_grid - 1)
        rs = jnp.clip(r - kr // 2, 0, rows_grid - kr)
        q0 = pl.multiple_of(n_ctx + r * GRID_W, GRID_W)
        k0 = pl.multiple_of(n_ctx + rs * GRID_W, GRID_W)
        return q0, k0, rs - r + (kr - 1)

    gap = NA_SLOTS // 2
    for slot in range(NA_SLOTS):
        s_ref[slot] = jnp.zeros(s_ref.shape[1:], F32)
        p_ref[slot] = jnp.zeros(p_ref.shape[1:], BF16)
        den_ref[slot] = jnp.ones(den_ref.shape[1:], F32)

    def step(i, slot):
        q0, k0, _ = window(i - 2 * gap)
        pr = p_ref[slot]
        o2 = (_dot(pr[:, :n_win], vb_ref[pl.ds(k0, n_win), :]) + _dot(pr[:, n_win:], vc)) / den_ref[slot]
        o_ref[pl.ds(q0, GRID_W), :] = _unstack_heads(o2, hd).astype(o_ref.dtype)

        mid = (slot + gap) % NA_SLOTS
        sc = s_ref[mid]
        pe = jnp.exp(sc - jnp.max(sc, axis=-1, keepdims=True))
        den_ref[mid] = jnp.sum(pe, axis=-1, keepdims=True)
        p_ref[mid] = pe.astype(BF16)

        q0, k0, off = window(i)
        q2 = _stack_heads(qn_ref[pl.ds(q0, GRID_W), :], hd)
        bias = bias_ref[off % 2, :, pl.ds(pl.multiple_of((off // 2) * (2 * GRID_W), 2 * GRID_W), n_win)]
        s_ref[slot, :, :n_win] = _dot_nt(q2, kn_ref[pl.ds(k0, n_win), :]) + bias
        s_ref[slot, :, n_win:] = _dot_nt(q2, kc)

    def steps(j, carry):
        for slot in range(NA_SLOTS):
            step(NA_SLOTS * j + slot, slot)
        return carry

    lax.fori_loop(0, (rows_grid + 2 * gap) // NA_SLOTS, steps, 0)


def _na_attention(qkv3, q_g, k_g, rpb, n_ctx):
    bsz, t, d3 = qkv3.shape
    d = d3 // 3
    hd = d // NA_HEADS
    w = 2 * hd
    pairs = d // w
    rows_grid = (t - n_ctx) // GRID_W
    assert w == LANES and rows_grid >= NA_KR and rows_grid % NA_SLOTS == 0
    n_keys = NA_KR * GRID_W + n_ctx
    bias = _na_bias_table(rpb)
    kern = functools.partial(_na_kernel, n_ctx=n_ctx, scale=hd ** -0.5)

    def col(group):
        return pl.BlockSpec((None, t, w), lambda hp, b: (b, 0, group * pairs + hp))

    gain = lambda g: jnp.tile(g.astype(F32), 2).reshape(1, w)
    return pl.pallas_call(
        kern, grid=(pairs, bsz),
        in_specs=[col(0), col(1), col(2),
                  pl.BlockSpec((1, w), lambda hp, b: (0, 0)),
                  pl.BlockSpec((1, w), lambda hp, b: (0, 0)),
                  pl.BlockSpec((None,) + bias.shape[1:], lambda hp, b: (hp, 0, 0, 0))],
        out_specs=pl.BlockSpec((None, t, w), lambda hp, b: (b, 0, hp)),
        out_shape=jax.ShapeDtypeStruct((bsz, t, d), BF16),
        scratch_shapes=[pltpu.VMEM((t, w), BF16)] * 2
        + [pltpu.VMEM((NA_SLOTS, 2 * GRID_W, n_keys), F32), pltpu.VMEM((NA_SLOTS, 2 * GRID_W, n_keys), BF16),
           pltpu.VMEM((NA_SLOTS, 2 * GRID_W, 1), F32)],
        compiler_params=_cparams("parallel", "parallel"), name="na_attention",
    )(qkv3, qkv3, qkv3, gain(q_g), gain(k_g), bias)


FFN_ROWS = 384


def _merge_swiglu_kernel(hg_ref, rg_ref, gate_ref, y_ref, og_ref, wo_ref, h_ref, g1c_ref, g1_ref,
                         g_ref, shc_ref, sh_ref, scc_ref, sc_ref, w1_ref, w3_ref, w2_ref, g2c_ref, g2_ref,
                         o_ref, *, n_ctx, tiles_per_batch):
    tile_in_batch = pl.program_id(0) % tiles_per_batch
    hg = hg_ref[...]
    half = hg.shape[1]
    hn = hg * lax.rsqrt(jnp.mean(hg * hg, axis=-1, keepdims=True) + EPS) * og_ref[...]
    a = (hn * _silu(gate_ref[...])).astype(BF16)
    b = (rg_ref[...] * _gelu_tanh(y_ref[...])).astype(BF16)
    mix = _dot(a, wo_ref[0:half, :].astype(BF16)) + _dot(b, wo_ref[half:, :].astype(BF16))
    h1 = h_ref[...] + _row_gate(g1c_ref, g1_ref, mix.shape, tile_in_batch, n_ctx) * mix
    x = _modulated_rows(h1, g_ref, shc_ref, sh_ref, scc_ref, sc_ref, tile_in_batch, n_ctx).astype(BF16)
    hid = _silu(_dot(x, w1_ref[...].astype(BF16))) * _dot(x, w3_ref[...].astype(BF16))
    y = _dot(hid.astype(BF16), w2_ref[...].astype(BF16))
    o_ref[...] = h1 + _row_gate(g2c_ref, g2_ref, y.shape, tile_in_batch, n_ctx) * y


def _merge_swiglu(hg, rg, p, onorm_g, w_out, h, g, w1, w3, w2, j, mods, n_batch, t, n_ctx):
    n, d = h.shape
    half = hg.shape[1]
    dff = w1.shape[2]
    tm = FFN_ROWS
    tpb = t // tm
    assert t % tm == 0 and n_ctx <= tm
    blk = lambda c: pl.BlockSpec((tm, half), lambda i: (i, c))
    vec = lambda width: pl.BlockSpec((1, width), lambda i: (0, 0))
    kern = functools.partial(_merge_swiglu_kernel, n_ctx=n_ctx, tiles_per_batch=tpb)
    return pl.pallas_call(
        kern, grid=(n // tm,),
        in_specs=[blk(0), blk(0), blk(4), blk(6), vec(half),
                  _resident((None, d, d), lambda i: (j, 0, 0)),
                  pl.BlockSpec((tm, d), lambda i: (i, 0))] + _stream_specs(d, n_batch, tpb, 2, 1)
        + [vec(d)] + _stream_specs(d, n_batch, tpb, 3, 1) + _stream_specs(d, n_batch, tpb, 4, 1)
        + [_resident((None, d, dff), lambda i: (j, 0, 0)),
           _resident((None, d, dff), lambda i: (j, 0, 0)),
           _resident((None, dff, d), lambda i: (j, 0, 0))] + _stream_specs(d, n_batch, tpb, 5, 1),
        out_specs=pl.BlockSpec((tm, d), lambda i: (i, 0)),
        out_shape=jax.ShapeDtypeStruct((n, d), F32),
        compiler_params=_cparams("parallel"), name="merge_swiglu",
    )(hg, rg, p, p, onorm_g.reshape(1, half), w_out, h, mods, mods,
      g, mods, mods, mods, mods, w1, w3, w2, mods, mods)


MOE_ROWS = 512


def _route_tables(route_t, counts, tm, n_exp):
    r = route_t.shape[1]
    experts = route_t[0:2].astype(jnp.int32)
    ranks = route_t[4:6].astype(jnp.int32)
    padded = ((counts + tm - 1) // tm) * tm
    ends = jnp.cumsum(padded)
    starts = ends - padded
    slot = ranks
    for e in range(n_exp):
        slot = slot + jnp.where(experts == e, starts[e], 0)
    p_slots = 2 * r + n_exp * tm
    n_tiles = p_slots // tm
    n_used = ends[-1] // tm
    tile = jnp.arange(n_tiles, dtype=jnp.int32)
    te = jnp.sum(((tile * tm)[:, None] >= ends[None, :]).astype(jnp.int32), axis=1)
    te = jnp.where(tile < n_used, te, te[n_used - 1])
    return slot, p_slots, te, n_used.reshape(1)


def _gather_rows(table, idx):
    p = idx.shape[0]
    m = table.shape[1]
    mesh = plsc.VectorSubcoreMesh(core_axis_name="c", subcore_axis_name="s")

    @functools.partial(pl.kernel, out_type=jax.ShapeDtypeStruct((p, m), table.dtype), mesh=mesh)
    def gather(table_hbm, idx_hbm, out_hbm):
        def body(idx_vmem, out_vmem):
            pltpu.sync_copy(table_hbm.at[idx_vmem.at[0]], out_vmem)

        pltpu.emit_pipeline(
            body, grid=(p // SC_WINDOW,),
            in_specs=[pl.BlockSpec((1, SC_WINDOW), lambda i: (0, i))],
            out_specs=[pl.BlockSpec((SC_WINDOW, m), lambda i: (i, 0))],
            core_axis_name=("c", "s"), dimension_semantics=(pltpu.PARALLEL,),
        )(idx_hbm, out_hbm)

    return gather(table, idx.reshape(1, p))


def _scatter_rows(rows, idx_a, idx_b, n_out):
    q, m = rows.shape
    mesh = plsc.VectorSubcoreMesh(core_axis_name="c", subcore_axis_name="s")

    @functools.partial(pl.kernel, out_type=jax.ShapeDtypeStruct((n_out, m), rows.dtype), mesh=mesh)
    def scatter(rows_hbm, ia_hbm, ib_hbm, out_hbm):
        def body(rows_vmem, ia_vmem, ib_vmem):
            pltpu.sync_copy(rows_vmem, out_hbm.at[ia_vmem.at[0]])
            pltpu.sync_copy(rows_vmem, out_hbm.at[ib_vmem.at[0]])

        pltpu.emit_pipeline(
            body, grid=(q // SC_WINDOW,),
            in_specs=[pl.BlockSpec((SC_WINDOW, m), lambda i: (i, 0)),
                      pl.BlockSpec((1, SC_WINDOW), lambda i: (0, i)),
                      pl.BlockSpec((1, SC_WINDOW), lambda i: (0, i))],
            out_specs=[],
            core_axis_name=("c", "s"), dimension_semantics=(pltpu.PARALLEL,),
        )(rows_hbm, ia_hbm, ib_hbm)

    return scatter(rows, idx_a.reshape(1, q), idx_b.reshape(1, q))


def _scatter_pieces(pieces, slot, n_slots):
    npc, r, m = pieces.shape
    offs = jnp.arange(npc, dtype=jnp.int32)[:, None] * n_slots
    idx_a = (slot[0][None, :] + offs).reshape(-1)
    idx_b = (slot[1][None, :] + offs).reshape(-1)
    return _scatter_rows(pieces.reshape(npc * r, m), idx_a, idx_b, npc * n_slots).reshape(npc, n_slots, m)


def _gather_pieces(pieces, idx):
    npc, v, m = pieces.shape
    offs = (jnp.arange(npc, dtype=jnp.int32) * v).reshape((npc,) + (1,) * idx.ndim)
    flat = (idx[None] + offs).reshape(-1)
    return _gather_rows(pieces.reshape(npc * v, m), flat).reshape((npc,) + idx.shape + (m,))


def _moe_ffn_kernel(te_ref, nu_ref, x_ref, w1_ref, w3_ref, w2_ref, y_ref):
    @pl.when(pl.program_id(0) < nu_ref[0])
    def _():
        x = _load_unpacked(x_ref).astype(BF16)
        hid = _silu(_dot(x, w1_ref[...].astype(BF16))) * _dot(x, w3_ref[...].astype(BF16))
        _store_pieces(y_ref, _pack_halves(_dot(hid.astype(BF16), w2_ref[...].astype(BF16))))


def _moe_ffn(xs, te, n_used, w1, w3, w2, j):
    npc, p, m = xs.shape
    d = 2 * npc * m
    dff = w1.shape[3]
    tm = MOE_ROWS

    def row(i, te_ref, nu_ref):
        return (0, jnp.minimum(i, nu_ref[0] - 1), 0)

    def expert(i, te_ref, nu_ref):
        return (j, te_ref[i], 0, 0)

    grid_spec = pltpu.PrefetchScalarGridSpec(
        num_scalar_prefetch=2, grid=(p // tm,),
        in_specs=[pl.BlockSpec((npc, tm, m), row),
                  _resident((None, None, d, dff), expert),
                  _resident((None, None, d, dff), expert),
                  _resident((None, None, dff, d), expert)],
        out_specs=pl.BlockSpec((npc, tm, m), row))
    return pl.pallas_call(
        _moe_ffn_kernel, grid_spec=grid_spec,
        out_shape=jax.ShapeDtypeStruct((npc, p, m), jnp.int32),
        compiler_params=_cparams("arbitrary"), name="moe_ffn",
    )(te, n_used, xs, w1, w3, w2)


def _moe_combine_kernel(ya_ref, yb_ref, route_ref, h_ref, gt_ref, o_ref):
    r = route_ref[...]
    y = r[:, 2:3] * _load_unpacked(ya_ref) + r[:, 3:4] * _load_unpacked(yb_ref)
    o_ref[...] = h_ref[...] + gt_ref[...] * y


def _moe_combine(yg, route, h, mods, n_batch, tiles_per_batch, ctx_tiles):
    d = h.shape[1]
    npc, _, r, m = yg.shape
    tm = TOKEN_TILE
    per = tiles_per_batch - ctx_tiles
    h_idx = lambda i: ((i // per) * tiles_per_batch + ctx_tiles + i % per, 0)
    if ctx_tiles == 0:
        mod_idx = _mod_row_spec(5, tiles_per_batch, n_batch)
    else:
        mod_idx = lambda i: (i // per, 5, 0, 0)
    return pl.pallas_call(
        _moe_combine_kernel, grid=(r // tm,),
        in_specs=[pl.BlockSpec((npc, None, tm, m), lambda i: (0, 0, i, 0)),
                  pl.BlockSpec((npc, None, tm, m), lambda i: (0, 1, i, 0)),
                  pl.BlockSpec((tm, LANES), lambda i: (i, 0)),
                  pl.BlockSpec((tm, d), h_idx),
                  pl.BlockSpec((None, None, 1, d), mod_idx)],
        out_specs=pl.BlockSpec((tm, d), lambda i: (i, 0)),
        out_shape=jax.ShapeDtypeStruct((r, d), F32),
        compiler_params=_cparams("parallel"), name="moe_combine",
    )(yg, yg, route, h, mods)


def _proj_moe_residual(o, w_o, h, g, w_router, w1, w3, w2, j, mods, n_batch, t, n_ctx, latent_only):
    tpb = t // TOKEN_TILE
    ctx_tiles = n_ctx // TOKEN_TILE if latent_only else 0
    h, u_pieces, route, route_t, counts = _proj_router(o, w_o, j, h, g, mods, n_batch, tpb, ctx_tiles, w_router)
    if latent_only:
        npc, _, m = u_pieces.shape
        u_pieces = u_pieces.reshape(npc, n_batch, t, m)[:, :, n_ctx:].reshape(npc, -1, m)
        route = route.reshape(n_batch, t, LANES)[:, n_ctx:].reshape(-1, LANES)
        route_t = route_t.reshape(-1, n_batch, t)[:, :, n_ctx:].reshape(route_t.shape[0], -1)
    n_exp = w1.shape[1]
    slot, n_slots, te, n_used = _route_tables(route_t, counts[0, :n_exp].astype(jnp.int32), MOE_ROWS, n_exp)
    xs = _scatter_pieces(u_pieces, slot, n_slots)
    ys = _moe_ffn(xs, te, n_used, w1, w3, w2, j)
    yg = _gather_pieces(ys, slot)
    return _moe_combine(yg, route, h, mods, n_batch, tpb, ctx_tiles)


def kernel(x, c, ctx, c_ctx, w_mod, b_mod, norm_mix_g, norm_ffn_g, ev_w_in, ev_w_out, hg_lb_logits, hg_onorm_g, rg_conv_w, rg_conv_b, rg_wa, rg_ba, rg_wx, rg_bx, rg_lambda, na_w_qkv, na_w_o, na_q_g, na_k_g, na_rpb, ffn_w1, ffn_w3, ffn_w2, moe_router, moe_w1, moe_w3, moe_w2):
    bsz, seq, d = x.shape
    n_ctx = ctx.shape[1]
    t = n_ctx + seq
    depth = w_mod.shape[0]
    assert n_ctx % TOKEN_TILE == 0 and seq % TOKEN_TILE == 0 and t % ROW_TILE == 0 and n_ctx <= ROW_TILE

    rows = -(-(bsz + 1) // 8) * 8
    cond = jnp.zeros((rows, d), F32).at[:bsz].set(c).at[bsz].set(c_ctx)
    mods_all = _adaln(cond, w_mod, b_mod).reshape(depth, rows, N_MOD, 1, d)

    lb_p = jax.nn.softmax(hg_lb_logits.astype(F32), axis=0)
    lb_all = jnp.cumsum(lb_p, axis=0) - lb_p[0]

    h = jnp.concatenate([ctx, x], axis=1).reshape(bsz * t, d)
    for layer in range(depth):
        j = layer // 2
        mods = mods_all[layer]
        g_mix = norm_mix_g[layer].reshape(1, d)
        if layer % 2 == 0:
            p = _modulated_matmul(h, g_mix, mods, ev_w_in, j, F32, ROW_TILE // 2, bsz, t, n_ctx)
            p3 = p.reshape(bsz, t, -1)
            hg = _hgrn2(p3, lb_all[j], n_ctx)
            half = hg.shape[-1]
            rg = _rglru(p3, 5 * half // (d - half), rg_conv_w[j], rg_conv_b[j], rg_wa[j], rg_ba[j],
                        rg_wx[j], rg_bx[j], rg_lambda[j], n_ctx)
            h = _merge_swiglu(hg.reshape(bsz * t, half), rg.reshape(bsz * t, d - half), p, hg_onorm_g[j],
                              ev_w_out, h, norm_ffn_g[layer].reshape(1, d), ffn_w1, ffn_w3, ffn_w2, j,
                              mods, bsz, t, n_ctx)
        else:
            qkv = _modulated_matmul(h, g_mix, mods, na_w_qkv, j, BF16, ROW_TILE, bsz, t, n_ctx)
            o = _na_attention(qkv.reshape(bsz, t, 3 * d), na_q_g[j], na_k_g[j], na_rpb[j], n_ctx)
            h = _proj_moe_residual(o.reshape(bsz * t, d), na_w_o, h, norm_ffn_g[layer].reshape(1, d), moe_router[j],
                                   moe_w1, moe_w3, moe_w2, j, mods, bsz, t, n_ctx, layer == depth - 1)
    if depth % 2 == 0:
        return h.reshape(bsz, seq, d)
    return h.reshape(bsz, t, d)[:, n_ctx:]
```

```python
import functools

import numpy as np
import jax
import jax.numpy as jnp
from jax import lax
from jax.experimental import pallas as pl
from jax.experimental.pallas import tpu as pltpu
from jax.experimental.pallas import tpu_sc as plsc

F32 = jnp.float32
BF16 = jnp.bfloat16

EPS = 1e-6
NEG_BIG = -1e9
LB_MIN = 1e-30
LOG2_E = 1.4426950408889634
N_MOD = 6
GRID_W = 64
HG_HEAD_DIM = 128
RG_BLOCKS = 8
RG_CONV = 4
RG_C = 8.0
NA_HEADS = 16
NA_KR = 8
NA_KC = 16
N_EXPERTS = 8

LANES = 128
SC_WINDOW = 128
SC_PIECES = 2
TOKEN_TILE = 256
HG_CHUNK = 128
HG_LEVELS = 7
HG_SLOTS = 2
NA_SLOTS = 4
RG_SCAN_BLOCK = 256
RG_GROUP = 8
VMEM_LIMIT = 56 * 1024 * 1024


def _cparams(*sem):
    return pltpu.CompilerParams(dimension_semantics=sem, vmem_limit_bytes=VMEM_LIMIT)


def _resident(block_shape, index_map):
    return pl.BlockSpec(block_shape, index_map, pipeline_mode=pl.Buffered(1))


def _dot(a, b):
    return jnp.dot(a, b, preferred_element_type=F32)


def _dot_nt(a, b):
    return lax.dot_general(a, b, (((1,), (1,)), ((), ())), preferred_element_type=F32)


def _dot_tn(a, b):
    return lax.dot_general(a, b, (((0,), (0,)), ((), ())), preferred_element_type=F32)


def _split2(x):
    hi = x.astype(BF16)
    lo = (x - hi.astype(F32)).astype(BF16)
    return hi, lo


def _sigmoid(x):
    return 0.5 * jnp.tanh(0.5 * x) + 0.5


def _silu(x):
    return x * _sigmoid(x)


def _softplus_neg_abs(x):
    return jnp.log(1.0 + jnp.exp(-jnp.abs(x)))


def _log_sigmoid(x):
    return jnp.minimum(x, 0.0) - _softplus_neg_abs(x)


def _gelu_tanh(x):
    return 0.5 * x * (1.0 + jnp.tanh(np.sqrt(2.0 / np.pi) * (x + 0.044715 * (x * x * x))))


def _adaln_kernel(c_ref, w_ref, b_ref, o_ref):
    a_hi, a_lo = _split2(_silu(c_ref[...]))
    w_hi, w_lo = _split2(w_ref[...])
    o_ref[...] = _dot(a_hi, w_hi) + _dot(a_hi, w_lo) + _dot(a_lo, w_hi) + b_ref[...]


def _adaln(cond, w_mod, b_mod):
    depth, d, n = w_mod.shape
    r = cond.shape[0]
    tn = 1536
    return pl.pallas_call(
        _adaln_kernel,
        grid=(depth, n // tn),
        in_specs=[pl.BlockSpec((r, d), lambda l, j: (0, 0)),
                  pl.BlockSpec((None, d, tn), lambda l, j: (l, 0, j)),
                  pl.BlockSpec((None, 1, tn), lambda l, j: (l, 0, j))],
        out_specs=pl.BlockSpec((None, r, tn), lambda l, j: (l, 0, j)),
        out_shape=jax.ShapeDtypeStruct((depth, r, n), F32),
        compiler_params=_cparams("parallel", "parallel"),
        name="adaln",
    )(cond, w_mod, b_mod.reshape(depth, 1, n))


def _mod_row_spec(k, tiles_per_batch, n_batch):
    def idx(i):
        row = jnp.where(i % tiles_per_batch == 0, n_batch, i // tiles_per_batch)
        return (row, k, 0, 0)
    return idx


def _pack_halves(x):
    m = x.shape[1] // 2
    lo = lax.bitcast_convert_type(x[:, :m].astype(BF16).astype(F32), jnp.int32)
    hi = lax.bitcast_convert_type(x[:, m:].astype(BF16).astype(F32), jnp.int32)
    return lax.shift_right_logical(lo, 16) | (hi & jnp.int32(-65536))


def _unpack_halves(w):
    lo = lax.bitcast_convert_type(lax.shift_left(w, 16), F32)
    hi = lax.bitcast_convert_type(w & jnp.int32(-65536), F32)
    return lo, hi


def _store_pieces(ref, w):
    m = w.shape[1] // SC_PIECES
    for j in range(SC_PIECES):
        ref[j] = w[:, j * m:(j + 1) * m]


def _load_unpacked(ref):
    parts = [_unpack_halves(ref[j]) for j in range(SC_PIECES)]
    return jnp.concatenate([lo for lo, _ in parts] + [hi for _, hi in parts], axis=1)


def _proj_router_kernel(o_ref, wo_ref, h_ref, gt_ref, g_ref, sh_ref, sc_ref, wr_ref,
                        h1_ref, u_ref, route_ref, route_t_ref, cnt_ref, *, tiles_per_batch, skip_tiles):
    i = pl.program_id(0)

    @pl.when(i == 0)
    def _():
        cnt_ref[...] = jnp.zeros_like(cnt_ref)

    h1 = h_ref[...] + gt_ref[...] * _dot(o_ref[...], wo_ref[...].astype(BF16))
    h1_ref[...] = h1
    y = h1 * lax.rsqrt(jnp.mean(h1 * h1, axis=-1, keepdims=True) + EPS)
    u = (y * g_ref[...]) * (1.0 + sc_ref[...]) + sh_ref[...]
    _store_pieces(u_ref, _pack_halves(u))
    u0, u1 = _split2(u)
    w0, w1 = _split2(wr_ref[...])
    logits = _dot(u0, w0) + _dot(u0, w1) + _dot(u1, w0)
    lane = lax.broadcasted_iota(jnp.int32, logits.shape, 1).astype(F32)
    lg = jnp.where(lane < N_EXPERTS, logits, -jnp.inf)
    m1 = jnp.max(lg, axis=-1, keepdims=True)
    i1 = jnp.min(jnp.where(lg == m1, lane, float(LANES)), axis=-1, keepdims=True)
    lg2 = jnp.where(lane == i1, -jnp.inf, lg)
    m2 = jnp.max(lg2, axis=-1, keepdims=True)
    i2 = jnp.min(jnp.where(lg2 == m2, lane, float(LANES)), axis=-1, keepdims=True)
    e2 = jnp.exp(m2 - m1)
    den = 1.0 + e2

    routed = jnp.where(i % tiles_per_batch >= skip_tiles, 1.0, 0.0)
    pick_a = jnp.where(lane == i1, routed, 0.0)
    pick_b = jnp.where(lane == i2, routed, 0.0)
    tm = pick_a.shape[0]
    earlier = jnp.where(lax.broadcasted_iota(jnp.int32, (tm, tm), 1) < lax.broadcasted_iota(jnp.int32, (tm, tm), 0),
                        1.0, 0.0).astype(BF16)
    total_a = jnp.sum(pick_a, axis=0, keepdims=True)
    base = cnt_ref[0:1, :]
    rank_a = jnp.sum(pick_a * (base + _dot(earlier, pick_a.astype(BF16))), axis=-1, keepdims=True)
    rank_b = jnp.sum(pick_b * (base + total_a + _dot(earlier, pick_b.astype(BF16))), axis=-1, keepdims=True)
    cnt_ref[0:1, :] = base + total_a + jnp.sum(pick_b, axis=0, keepdims=True)

    route = jnp.where(lane == 0, i1, jnp.where(lane == 1, i2, jnp.where(lane == 2, 1.0 / den, jnp.where(
        lane == 3, e2 / den, jnp.where(lane == 4, rank_a, rank_b)))))
    route_ref[...] = route
    route_t_ref[...] = route.T[0:route_t_ref.shape[0], :]


def _proj_router(o, w_o, j, h, g, mods, n_batch, tiles_per_batch, skip_tiles, w_router):
    n, d = h.shape
    tm = TOKEN_TILE
    row = pl.BlockSpec((tm, d), lambda i: (i, 0))
    vec = pl.BlockSpec((1, d), lambda i: (0, 0))

    def mod_spec(k):
        return pl.BlockSpec((None, None, 1, d), _mod_row_spec(k, tiles_per_batch, n_batch))

    wr = jnp.zeros((d, LANES), F32).at[:, :N_EXPERTS].set(w_router)
    kern = functools.partial(_proj_router_kernel, tiles_per_batch=tiles_per_batch, skip_tiles=skip_tiles)
    return pl.pallas_call(
        kern, grid=(n // tm,),
        in_specs=[row, _resident((None, d, d), lambda i: (j, 0, 0)), row, mod_spec(2),
                  vec, mod_spec(3), mod_spec(4), pl.BlockSpec((d, LANES), lambda i: (0, 0))],
        out_specs=[row,
                   pl.BlockSpec((SC_PIECES, tm, d // 2 // SC_PIECES), lambda i: (0, i, 0)),
                   pl.BlockSpec((tm, LANES), lambda i: (i, 0)),
                   pl.BlockSpec((8, tm), lambda i: (0, i)),
                   pl.BlockSpec((8, LANES), lambda i: (0, 0))],
        out_shape=[jax.ShapeDtypeStruct((n, d), F32),
                   jax.ShapeDtypeStruct((SC_PIECES, n, d // 2 // SC_PIECES), jnp.int32),
                   jax.ShapeDtypeStruct((n, LANES), F32),
                   jax.ShapeDtypeStruct((8, n), F32),
                   jax.ShapeDtypeStruct((8, LANES), F32)],
        compiler_params=_cparams("arbitrary"), name="proj_router",
    )(o, w_o, h, mods, g, mods, mods, wr)


ROW_TILE = 768


def _row_gate(ctx_ref, own_ref, shape, tile_in_batch, n_ctx):
    rows = lax.broadcasted_iota(jnp.int32, shape, 0)
    is_ctx = (rows < n_ctx) & (tile_in_batch == 0)
    return jnp.where(is_ctx, ctx_ref[...], own_ref[...])


def _modulated_rows(x, g_ref, shc_ref, sh_ref, scc_ref, sc_ref, tile_in_batch, n_ctx):
    y = x * lax.rsqrt(jnp.mean(x * x, axis=-1, keepdims=True) + EPS)
    shift = _row_gate(shc_ref, sh_ref, x.shape, tile_in_batch, n_ctx)
    scale = _row_gate(scc_ref, sc_ref, x.shape, tile_in_batch, n_ctx)
    return (y * g_ref[...]) * (1.0 + scale) + shift


def _stream_specs(d, n_batch, tiles_per_batch, k, nargs):
    if nargs == 1:
        ctx = lambda i: (n_batch, k, 0, 0)
        own = lambda i: (i // tiles_per_batch, k, 0, 0)
    else:
        ctx = lambda i, f: (n_batch, k, 0, 0)
        own = lambda i, f: (i // tiles_per_batch, k, 0, 0)
    return [pl.BlockSpec((None, None, 1, d), ctx), pl.BlockSpec((None, None, 1, d), own)]


def _mod_mm_kernel(h_ref, g_ref, shc_ref, sh_ref, scc_ref, sc_ref, w_ref, o_ref, *, n_ctx, tiles_per_batch):
    x = _modulated_rows(h_ref[...], g_ref, shc_ref, sh_ref, scc_ref, sc_ref,
                        pl.program_id(0) % tiles_per_batch, n_ctx).astype(BF16)
    o_ref[...] = _dot(x, w_ref[...].astype(BF16)).astype(o_ref.dtype)


def _combine_mod_mm_kernel(ya_ref, yb_ref, route_ref, h_ref, g2c_ref, g2_ref, g_ref, shc_ref, sh_ref,
                           scc_ref, sc_ref, w_ref, h1_ref, o_ref, *, n_ctx, tiles_per_batch):
    tile_in_batch = pl.program_id(0) % tiles_per_batch
    r = route_ref[...]
    y = r[:, 2:3] * _load_unpacked(ya_ref) + r[:, 3:4] * _load_unpacked(yb_ref)
    h1 = h_ref[...] + _row_gate(g2c_ref, g2_ref, y.shape, tile_in_batch, n_ctx) * y
    h1_ref[...] = h1
    x = _modulated_rows(h1, g_ref, shc_ref, sh_ref, scc_ref, sc_ref, tile_in_batch, n_ctx).astype(BF16)
    o_ref[...] = _dot(x, w_ref[...].astype(BF16)).astype(o_ref.dtype)


def _combined_modulated_matmul(yg, route, h, mods_prev, g, mods, w, j, out_dtype, tm, n_batch, t, n_ctx):
    n, d = h.shape
    nout = w.shape[2]
    npc, _, _, m = yg.shape
    tpb = t // tm
    assert t % tm == 0 and n_ctx <= tm
    kern = functools.partial(_combine_mod_mm_kernel, n_ctx=n_ctx, tiles_per_batch=tpb)
    return pl.pallas_call(
        kern, grid=(n // tm,),
        in_specs=[pl.BlockSpec((npc, None, tm, m), lambda i: (0, 0, i, 0)),
                  pl.BlockSpec((npc, None, tm, m), lambda i: (0, 1, i, 0)),
                  pl.BlockSpec((tm, LANES), lambda i: (i, 0)),
                  pl.BlockSpec((tm, d), lambda i: (i, 0))] + _stream_specs(d, n_batch, tpb, 5, 1)
        + [pl.BlockSpec((1, d), lambda i: (0, 0))]
        + _stream_specs(d, n_batch, tpb, 0, 1) + _stream_specs(d, n_batch, tpb, 1, 1)
        + [_resident((None, d, nout), lambda i: (j, 0, 0))],
        out_specs=[pl.BlockSpec((tm, d), lambda i: (i, 0)), pl.BlockSpec((tm, nout), lambda i: (i, 0))],
        out_shape=[jax.ShapeDtypeStruct((n, d), F32), jax.ShapeDtypeStruct((n, nout), out_dtype)],
        compiler_params=_cparams("parallel"), name="combined_modulated_matmul",
    )(yg, yg, route, h, mods_prev, mods_prev, g, mods, mods, mods, mods, w)


def _modulated_matmul(h, g, mods, w, j, out_dtype, tm, n_batch, t, n_ctx):
    n, d = h.shape
    nout = w.shape[2]
    tpb = t // tm
    assert t % tm == 0 and n_ctx <= tm
    kern = functools.partial(_mod_mm_kernel, n_ctx=n_ctx, tiles_per_batch=tpb)
    return pl.pallas_call(
        kern, grid=(n // tm,),
        in_specs=[pl.BlockSpec((tm, d), lambda i: (i, 0)),
                  pl.BlockSpec((1, d), lambda i: (0, 0))]
        + _stream_specs(d, n_batch, tpb, 0, 1) + _stream_specs(d, n_batch, tpb, 1, 1)
        + [_resident((None, d, nout), lambda i: (j, 0, 0))],
        out_specs=pl.BlockSpec((tm, nout), lambda i: (i, 0)),
        out_shape=jax.ShapeDtypeStruct((n, nout), out_dtype),
        compiler_params=_cparams("parallel"), name="modulated_matmul",
    )(h, g, mods, mods, mods, mods, w)


def _hg_tables():
    c = HG_CHUNK
    t = np.arange(c)[:, None]
    u = np.arange(c)[None, :]
    mats = [(u <= t), (u > t)]
    for k in range(1, HG_LEVELS + 1):
        m = ((t >> k) << k) + (1 << (k - 1)) - 1
        upper = t > m
        mats.append(np.where(upper, (u > m) & (u <= t), (u > t) & (u <= m)))
    fwd = np.concatenate(mats, axis=0).astype(np.float32)
    bwd = fwd.reshape(-1, c, c)[:, ::-1, ::-1].reshape(-1, c)
    x = t ^ u
    lvl = np.where(t > u, np.floor(np.log2(np.maximum(x, 1))).astype(np.int32) + 1, -1)
    lvl = np.where(t == u, 0, lvl).astype(np.int32)
    return np.stack([fwd, bwd]), np.stack([lvl, lvl.T])


def _hg_kernel(q_ref, ff_ref, fb_ref, v_ref, lb_ref, m_ref, lvl_ref, o_ref, of_ref, ob_ref,
               qk_ref, el_ref, io_ref, tot_ref, att_ref, *, n_ctx_chunks, n_chunks):
    c = HG_CHUNK
    gap = HG_SLOTS // 2
    t = o_ref.shape[0]
    z_refs = (ff_ref, fb_ref)
    out_refs = (of_ref, ob_ref)
    for ref in (qk_ref, el_ref, io_ref, tot_ref, att_ref):
        ref[...] = jnp.zeros(ref.shape, ref.dtype)

    def start_row(d, j):
        j = jnp.clip(j, 0, n_chunks - 1)
        if d == 1:
            j = jnp.where(j < n_ctx_chunks, n_ctx_chunks - 1 - j, n_chunks - 1 - (j - n_ctx_chunks))
        return pl.multiple_of(j * c, c)

    def stage1(d, j, slot):
        t0 = start_row(d, j)
        lb = lb_ref[d:d + 1, :]
        log_lb = jnp.log(jnp.maximum(lb, LB_MIN))
        q = _silu(q_ref[pl.ds(t0, c), :])
        z = z_refs[d][pl.ds(t0, c), :]
        k = (1.0 - lb) * _sigmoid(-z)
        b = jnp.log1p(-lb) + _log_sigmoid(z)
        logf = jnp.maximum(log_lb, b) + _softplus_neg_abs(log_lb - b)
        g_hi, g_lo = _split2(logf * LOG2_E)
        m_ends = m_ref[d, 0:2 * c, :]
        e_ends = jnp.exp2(_dot(m_ends, g_hi) + _dot(m_ends, g_lo))
        el_ref[d, slot] = jnp.exp2(_dot(m_ref[d, 2 * c:, :], g_hi).astype(BF16))
        qk_ref[d, slot, 0] = q.astype(BF16)
        qk_ref[d, slot, 1] = k.astype(BF16)
        io_ref[d, slot, 0] = (q * e_ends[0:c]).astype(BF16)
        io_ref[d, slot, 1] = (k * e_ends[c:]).astype(BF16)
        tot_row = c - 1 if d == 0 else 0
        tot_ref[d, slot, 0:1, :] = e_ends[tot_row:tot_row + 1, :]

    def stage2(d, slot):
        qb = qk_ref[d, slot, 0]
        kb = qk_ref[d, slot, 1]
        lvl = lvl_ref[d]
        att = jnp.where(lvl == 0, _dot_nt(qb, kb), 0.0)
        for lev in range(1, HG_LEVELS + 1):
            el = el_ref[d, slot, (lev - 1) * c:lev * c, :]
            att = jnp.where(lvl == lev, _dot_nt(qb * el, kb * el), att)
        att_ref[d, slot] = att.astype(BF16)

    def stage3(d, j, slot, st):
        t0 = start_row(d, j)
        v = v_ref[pl.ds(t0, c), :].astype(BF16)
        row = jnp.where(j < n_chunks, t0, t)
        out_refs[d][pl.ds(pl.multiple_of(row, c), c), :] = (
            _dot(att_ref[d, slot], v) + _dot_nt(io_ref[d, slot, 0], st.astype(BF16)))
        return st * tot_ref[d, slot, 0:1, :] + _dot_tn(v, io_ref[d, slot, 1])

    def steps(jj, carry):
        st = list(carry)
        for slot in range(HG_SLOTS):
            i = HG_SLOTS * jj + slot
            for d in range(2):
                st[d] = stage3(d, i - 2 * gap, slot, st[d])
            for d in range(2):
                stage2(d, (slot + gap) % HG_SLOTS)
            for d in range(2):
                stage1(d, i, slot)
        return tuple(st)

    zero = jnp.zeros((c, c), F32)
    n_steps = -(-(n_chunks + 2 * gap) // HG_SLOTS)
    lax.fori_loop(0, n_steps, steps, (zero, zero))
    o_ref[...] = (of_ref[0:t, :] + ob_ref[0:t, :]).astype(o_ref.dtype)


def _hgrn2(p3, lb, n_ctx):
    bsz, t, _ = p3.shape
    width = lb.shape[1]
    heads = width // HG_HEAD_DIM
    mats, lvl = _hg_tables()
    c = HG_CHUNK

    def col(group):
        return pl.BlockSpec((None, t, c), lambda b, h: (b, 0, group * heads + h))

    kern = functools.partial(_hg_kernel, n_ctx_chunks=n_ctx // c, n_chunks=t // c)
    return pl.pallas_call(
        kern, grid=(bsz, heads),
        in_specs=[col(0), col(1), col(2), col(3),
                  pl.BlockSpec((2, c), lambda b, h: (0, h)),
                  pl.BlockSpec(mats.shape, lambda b, h: (0, 0, 0)),
                  pl.BlockSpec(lvl.shape, lambda b, h: (0, 0, 0))],
        out_specs=pl.BlockSpec((None, t, c), lambda b, h: (b, 0, h)),
        out_shape=jax.ShapeDtypeStruct((bsz, t, width), F32),
        scratch_shapes=[pltpu.VMEM((t + c, c), F32), pltpu.VMEM((t + c, c), F32),
                        pltpu.VMEM((2, HG_SLOTS, 2, c, c), BF16), pltpu.VMEM((2, HG_SLOTS, HG_LEVELS * c, c), BF16),
                        pltpu.VMEM((2, HG_SLOTS, 2, c, c), BF16), pltpu.VMEM((2, HG_SLOTS, 8, c), F32),
                        pltpu.VMEM((2, HG_SLOTS, c, c), BF16)],
        compiler_params=_cparams("parallel", "parallel"), name="hgrn2",
    )(p3, p3, p3, p3, lb, jnp.asarray(mats, BF16), jnp.asarray(lvl))


PAD = 8


def _rg_kernel(x_ref, cw_ref, cb_ref, wa_ref, ba_ref, wx_ref, bx_ref, lam_ref, o_ref,
               xpad_ref, xc_ref, ob_ref, *, n_ctx):
    t, w = x_ref.shape
    n_lat = t - n_ctx
    lat0 = n_ctx + 2 * PAD
    xpad_ref[...] = jnp.zeros(xpad_ref.shape, F32)
    xpad_ref[PAD:PAD + n_ctx, :] = x_ref[0:n_ctx, :]
    xpad_ref[lat0:lat0 + n_lat, :] = x_ref[n_ctx:t, :]
    left = RG_CONV // 2
    for base, dst, n in ((PAD, 0, n_ctx), (lat0, n_ctx, n_lat)):
        acc = cb_ref[...] + cw_ref[0:1, :] * xpad_ref[base - left:base - left + n, :]
        for j in range(1, RG_CONV):
            acc = acc + cw_ref[j:j + 1, :] * xpad_ref[base - left + j:base - left + j + n, :]
        xc_ref[dst:dst + n, :] = acc

    blk = RG_SCAN_BLOCK
    n_blocks = t // blk
    n_ctx_blocks = n_ctx // blk
    sub = lax.broadcasted_iota(jnp.int32, (blk // RG_GROUP, RG_GROUP, w), 1)
    c_log_sig_lam = RG_C * _log_sigmoid(lam_ref[...])

    def scan_block(d, t0, hin, out_ref):
        xc = xc_ref[pl.ds(t0, blk), :]
        xb = xc.astype(BF16)
        r = _sigmoid(_dot(xb, wa_ref[d]) + ba_ref[d:d + 1, :])
        ig = _sigmoid(_dot(xb, wx_ref[d]) + bx_ref[d:d + 1, :])
        log_a = r * c_log_sig_lam[d:d + 1, :]
        a = jnp.exp(log_a)
        b = jnp.sqrt(jnp.maximum(1.0 - a * a, 0.0)) * (ig * xc)
        n_groups = blk // RG_GROUP
        a = a.reshape(n_groups, RG_GROUP, w)
        b = b.reshape(n_groups, RG_GROUP, w)
        s = 1
        while s < RG_GROUP:
            shift = s if d == 0 else RG_GROUP - s
            keep = (sub >= s) if d == 0 else (sub < RG_GROUP - s)
            b = jnp.where(keep, a * pltpu.roll(b, shift, 1), 0.0) + b
            a = jnp.where(keep, a * pltpu.roll(a, shift, 1), a)
            s *= 2
        for gi in range(n_groups):
            gj = gi if d == 0 else n_groups - 1 - gi
            hg = a[gj] * hin + b[gj]
            out_ref[pl.ds(t0 + gj * RG_GROUP, RG_GROUP), :] = hg
            hin = hg[RG_GROUP - 1:RG_GROUP, :] if d == 0 else hg[0:1, :]
        return hin

    def step(i, carry):
        h_f, h_b = carry
        bb = jnp.where(i < n_ctx_blocks, n_ctx_blocks - 1 - i, n_blocks - 1 - (i - n_ctx_blocks))
        h_f = scan_block(0, pl.multiple_of(i * blk, blk), h_f, o_ref)
        h_b = scan_block(1, pl.multiple_of(bb * blk, blk), h_b, ob_ref)
        return h_f, h_b

    zero = jnp.zeros((1, w), F32)
    lax.fori_loop(0, n_blocks, step, (zero, zero))
    o_ref[...] += ob_ref[...]


def _blockdiag_dense(w):
    nd, nb, k, _ = w.shape
    eye = jnp.eye(nb, dtype=w.dtype)
    return jnp.einsum('dnij,nm->dnimj', w, eye).reshape(nd, nb * k, nb * k)


def _rglru(p3, col_block, conv_w, conv_b, wa, ba, wx, bx, lam, n_ctx):
    bsz, t, _ = p3.shape
    w = conv_w.shape[1]
    full = lambda shape: pl.BlockSpec(shape, lambda b: (0,) * len(shape))
    kern = functools.partial(_rg_kernel, n_ctx=n_ctx)
    return pl.pallas_call(
        kern, grid=(bsz,),
        in_specs=[pl.BlockSpec((None, t, w), lambda b: (b, 0, col_block)),
                  full((RG_CONV, w)), full((1, w)),
                  full((2, w, w)), full((2, w)), full((2, w, w)), full((2, w)), full((2, w))],
        out_specs=pl.BlockSpec((None, t, w), lambda b: (b, 0, 0)),
        out_shape=jax.ShapeDtypeStruct((bsz, t, w), F32),
        scratch_shapes=[pltpu.VMEM((t + 3 * PAD, w), F32)] + [pltpu.VMEM((t, w), F32)] * 2,
        compiler_params=_cparams("parallel"), name="rglru",
    )(p3, conv_w, conv_b.reshape(1, w), _blockdiag_dense(wa).astype(BF16), ba,
      _blockdiag_dense(wx).astype(BF16), bx, lam)


def _na_bias_table(rpb):
    q = np.arange(GRID_W)[:, None]
    kc = np.arange(GRID_W)[None, :]
    wstart = np.clip(q - NA_KC // 2, 0, GRID_W - NA_KC)
    mask = (kc >= wstart) & (kc < wstart + NA_KC)
    coff = np.clip(kc - q + NA_KC - 1, 0, 2 * NA_KC - 2)
    sel = (coff[None] == np.arange(2 * NA_KC - 1)[:, None, None]).astype(np.float32)
    toe = jnp.einsum('hdj,jqk->hqdk', rpb.astype(F32), sel, precision=lax.Precision.HIGHEST)
    toe = jnp.where(mask[None, :, None, :], toe, NEG_BIG)
    h = toe.shape[0]
    flat = toe.reshape(h // 2, 2 * GRID_W, -1)
    width = (2 * NA_KR) * GRID_W
    even = jnp.pad(flat, ((0, 0), (0, 0), (0, width - flat.shape[2])))
    odd = jnp.pad(flat[:, :, GRID_W:], ((0, 0), (0, 0), (0, width - flat.shape[2] + GRID_W)))
    return jnp.stack([even, odd], axis=1)


def _stack_heads(x, hd):
    lane = lax.broadcasted_iota(jnp.int32, x.shape, 1)
    zero = jnp.zeros_like(x)
    return jnp.concatenate([jnp.where(lane < hd, x, zero), jnp.where(lane >= hd, x, zero)], axis=0)


def _unstack_heads(o2, hd):
    n = o2.shape[0] // 2
    lane = lax.broadcasted_iota(jnp.int32, (n, o2.shape[1]), 1)
    return jnp.where(lane < hd, o2[:n], o2[n:])


def _na_kernel(q_ref, k_ref, v_ref, qg_ref, kg_ref, bias_ref, o_ref, qn_ref, kn_ref,
               s_ref, p_ref, den_ref, *, n_ctx, scale):
    t, w = q_ref.shape
    hd = w // 2
    rows_grid = (t - n_ctx) // GRID_W
    r_i = lax.broadcasted_iota(jnp.int32, (w, w), 0)
    c_i = lax.broadcasted_iota(jnp.int32, (w, w), 1)
    seg = jnp.where((r_i // hd) == (c_i // hd), 1.0 / hd, 0.0).astype(BF16)

    def normed(x, g):
        ms = _dot((x * x).astype(BF16), seg)
        return (x * lax.rsqrt(ms + EPS) * g).astype(BF16)

    qn_ref[...] = normed(q_ref[...].astype(F32), qg_ref[...] * scale)
    kn_ref[...] = normed(k_ref[...].astype(F32), kg_ref[...])
    vb_ref = v_ref

    kc = kn_ref[0:n_ctx, :]
    vc = vb_ref[0:n_ctx, :]

    s = _dot_nt(_stack_heads(qn_ref[0:n_ctx, :], hd), kc)
    p = jnp.exp(s - jnp.max(s, axis=-1, keepdims=True))
    o2 = _dot(p.astype(BF16), vc) / jnp.sum(p, axis=-1, keepdims=True)
    o_ref[0:n_ctx, :] = _unstack_heads(o2, hd).astype(o_ref.dtype)

    kr = NA_KR
    n_win = kr * GRID_W

    def window(r):
        r = jnp.clip(r, 0, rows_grid - 1)
        rs = jnp.clip(r - kr // 2, 0, rows_grid - kr)
        q0 = pl.multiple_of(n_ctx + r * GRID_W, GRID_W)
        k0 = pl.multiple_of(n_ctx + rs * GRID_W, GRID_W)
        return q0, k0, rs - r + (kr - 1)

    gap = NA_SLOTS // 2
    for slot in range(NA_SLOTS):
        s_ref[slot] = jnp.zeros(s_ref.shape[1:], F32)
        p_ref[slot] = jnp.zeros(p_ref.shape[1:], BF16)
        den_ref[slot] = jnp.ones(den_ref.shape[1:], F32)

    def step(i, slot):
        q0, k0, _ = window(i - 2 * gap)
        pr = p_ref[slot]
        o2 = (_dot(pr[:, :n_win], vb_ref[pl.ds(k0, n_win), :]) + _dot(pr[:, n_win:], vc)) / den_ref[slot]
        o_ref[pl.ds(q0, GRID_W), :] = _unstack_heads(o2, hd).astype(o_ref.dtype)

        mid = (slot + gap) % NA_SLOTS
        sc = s_ref[mid]
        pe = jnp.exp(sc - jnp.max(sc, axis=-1, keepdims=True))
        den_ref[mid] = jnp.sum(pe, axis=-1, keepdims=True)
        p_ref[mid] = pe.astype(BF16)

        q0, k0, off = window(i)
        q2 = _stack_heads(qn_ref[pl.ds(q0, GRID_W), :], hd)
        bias = bias_ref[off % 2, :, pl.ds(pl.multiple_of((off // 2) * (2 * GRID_W), 2 * GRID_W), n_win)]
        s_ref[slot, :, :n_win] = _dot_nt(q2, kn_ref[pl.ds(k0, n_win), :]) + bias
        s_ref[slot, :, n_win:] = _dot_nt(q2, kc)

    def steps(j, carry):
        for slot in range(NA_SLOTS):
            step(NA_SLOTS * j + slot, slot)
        return carry

    lax.fori_loop(0, (rows_grid + 2 * gap) // NA_SLOTS, steps, 0)


def _na_attention(qkv3, q_g, k_g, rpb, n_ctx):
    bsz, t, d3 = qkv3.shape
    d = d3 // 3
    hd = d // NA_HEADS
    w = 2 * hd
    pairs = d // w
    rows_grid = (t - n_ctx) // GRID_W
    assert w == LANES and rows_grid >= NA_KR and rows_grid % NA_SLOTS == 0
    n_keys = NA_KR * GRID_W + n_ctx
    bias = _na_bias_table(rpb)
    kern = functools.partial(_na_kernel, n_ctx=n_ctx, scale=hd ** -0.5)

    def col(group):
        return pl.BlockSpec((None, t, w), lambda hp, b: (b, 0, group * pairs + hp))

    gain = lambda g: jnp.tile(g.astype(F32), 2).reshape(1, w)
    return pl.pallas_call(
        kern, grid=(pairs, bsz),
        in_specs=[col(0), col(1), col(2),
                  pl.BlockSpec((1, w), lambda hp, b: (0, 0)),
                  pl.BlockSpec((1, w), lambda hp, b: (0, 0)),
                  pl.BlockSpec((None,) + bias.shape[1:], lambda hp, b: (hp, 0, 0, 0))],
        out_specs=pl.BlockSpec((None, t, w), lambda hp, b: (b, 0, hp)),
        out_shape=jax.ShapeDtypeStruct((bsz, t, d), BF16),
        scratch_shapes=[pltpu.VMEM((t, w), BF16)] * 2
        + [pltpu.VMEM((NA_SLOTS, 2 * GRID_W, n_keys), F32), pltpu.VMEM((NA_SLOTS, 2 * GRID_W, n_keys), BF16),
           pltpu.VMEM((NA_SLOTS, 2 * GRID_W, 1), F32)],
        compiler_params=_cparams("parallel", "parallel"), name="na_attention",
    )(qkv3, qkv3, qkv3, gain(q_g), gain(k_g), bias)


FFN_ROWS = 384


def _merge_swiglu_kernel(hg_ref, rg_ref, gate_ref, y_ref, og_ref, wo_ref, h_ref, g1c_ref, g1_ref,
                         g_ref, shc_ref, sh_ref, scc_ref, sc_ref, w1_ref, w3_ref, w2_ref, g2c_ref, g2_ref,
                         o_ref, *, n_ctx, tiles_per_batch):
    tile_in_batch = pl.program_id(0) % tiles_per_batch
    hg = hg_ref[...]
    half = hg.shape[1]
    hn = hg * lax.rsqrt(jnp.mean(hg * hg, axis=-1, keepdims=True) + EPS) * og_ref[...]
    a = (hn * _silu(gate_ref[...])).astype(BF16)
    b = (rg_ref[...] * _gelu_tanh(y_ref[...])).astype(BF16)
    mix = _dot(a, wo_ref[0:half, :].astype(BF16)) + _dot(b, wo_ref[half:, :].astype(BF16))
    h1 = h_ref[...] + _row_gate(g1c_ref, g1_ref, mix.shape, tile_in_batch, n_ctx) * mix
    x = _modulated_rows(h1, g_ref, shc_ref, sh_ref, scc_ref, sc_ref, tile_in_batch, n_ctx).astype(BF16)
    hid = _silu(_dot(x, w1_ref[...].astype(BF16))) * _dot(x, w3_ref[...].astype(BF16))
    y = _dot(hid.astype(BF16), w2_ref[...].astype(BF16))
    o_ref[...] = h1 + _row_gate(g2c_ref, g2_ref, y.shape, tile_in_batch, n_ctx) * y


def _merge_swiglu(hg, rg, p, onorm_g, w_out, h, g, w1, w3, w2, j, mods, n_batch, t, n_ctx):
    n, d = h.shape
    half = hg.shape[1]
    dff = w1.shape[2]
    tm = FFN_ROWS
    tpb = t // tm
    assert t % tm == 0 and n_ctx <= tm
    blk = lambda c: pl.BlockSpec((tm, half), lambda i: (i, c))
    vec = lambda width: pl.BlockSpec((1, width), lambda i: (0, 0))
    kern = functools.partial(_merge_swiglu_kernel, n_ctx=n_ctx, tiles_per_batch=tpb)
    return pl.pallas_call(
        kern, grid=(n // tm,),
        in_specs=[blk(0), blk(0), blk(4), blk(6), vec(half),
                  _resident((None, d, d), lambda i: (j, 0, 0)),
                  pl.BlockSpec((tm, d), lambda i: (i, 0))] + _stream_specs(d, n_batch, tpb, 2, 1)
        + [vec(d)] + _stream_specs(d, n_batch, tpb, 3, 1) + _stream_specs(d, n_batch, tpb, 4, 1)
        + [_resident((None, d, dff), lambda i: (j, 0, 0)),
           _resident((None, d, dff), lambda i: (j, 0, 0)),
           _resident((None, dff, d), lambda i: (j, 0, 0))] + _stream_specs(d, n_batch, tpb, 5, 1),
        out_specs=pl.BlockSpec((tm, d), lambda i: (i, 0)),
        out_shape=jax.ShapeDtypeStruct((n, d), F32),
        compiler_params=_cparams("parallel"), name="merge_swiglu",
    )(hg, rg, p, p, onorm_g.reshape(1, half), w_out, h, mods, mods,
      g, mods, mods, mods, mods, w1, w3, w2, mods, mods)


MOE_ROWS = 512


def _route_tables(route_t, counts, tm, n_exp):
    r = route_t.shape[1]
    experts = route_t[0:2].astype(jnp.int32)
    ranks = route_t[4:6].astype(jnp.int32)
    padded = ((counts + tm - 1) // tm) * tm
    ends = jnp.cumsum(padded)
    starts = ends - padded
    slot = ranks
    for e in range(n_exp):
        slot = slot + jnp.where(experts == e, starts[e], 0)
    p_slots = 2 * r + n_exp * tm
    n_tiles = p_slots // tm
    n_used = ends[-1] // tm
    tile = jnp.arange(n_tiles, dtype=jnp.int32)
    te = jnp.sum(((tile * tm)[:, None] >= ends[None, :]).astype(jnp.int32), axis=1)
    te = jnp.where(tile < n_used, te, te[n_used - 1])
    return slot, p_slots, te, n_used.reshape(1)


def _gather_rows(table, idx):
    p = idx.shape[0]
    m = table.shape[1]
    mesh = plsc.VectorSubcoreMesh(core_axis_name="c", subcore_axis_name="s")

    @functools.partial(pl.kernel, out_type=jax.ShapeDtypeStruct((p, m), table.dtype), mesh=mesh)
    def gather(table_hbm, idx_hbm, out_hbm):
        def body(idx_vmem, out_vmem):
            pltpu.sync_copy(table_hbm.at[idx_vmem.at[0]], out_vmem)

        pltpu.emit_pipeline(
            body, grid=(p // SC_WINDOW,),
            in_specs=[pl.BlockSpec((1, SC_WINDOW), lambda i: (0, i))],
            out_specs=[pl.BlockSpec((SC_WINDOW, m), lambda i: (i, 0))],
            core_axis_name=("c", "s"), dimension_semantics=(pltpu.PARALLEL,),
        )(idx_hbm, out_hbm)

    return gather(table, idx.reshape(1, p))


def _scatter_rows(rows, idx_a, idx_b, n_out):
    q, m = rows.shape
    mesh = plsc.VectorSubcoreMesh(core_axis_name="c", subcore_axis_name="s")

    @functools.partial(pl.kernel, out_type=jax.ShapeDtypeStruct((n_out, m), rows.dtype), mesh=mesh)
    def scatter(rows_hbm, ia_hbm, ib_hbm, out_hbm):
        def body(rows_vmem, ia_vmem, ib_vmem):
            pltpu.sync_copy(rows_vmem, out_hbm.at[ia_vmem.at[0]])
            pltpu.sync_copy(rows_vmem, out_hbm.at[ib_vmem.at[0]])

        pltpu.emit_pipeline(
            body, grid=(q // SC_WINDOW,),
            in_specs=[pl.BlockSpec((SC_WINDOW, m), lambda i: (i, 0)),
                      pl.BlockSpec((1, SC_WINDOW), lambda i: (0, i)),
                      pl.BlockSpec((1, SC_WINDOW), lambda i: (0, i))],
            out_specs=[],
            core_axis_name=("c", "s"), dimension_semantics=(pltpu.PARALLEL,),
        )(rows_hbm, ia_hbm, ib_hbm)

    return scatter(rows, idx_a.reshape(1, q), idx_b.reshape(1, q))


def _scatter_pieces(pieces, slot, n_slots):
    npc, r, m = pieces.shape
    offs = jnp.arange(npc, dtype=jnp.int32)[:, None] * n_slots
    idx_a = (slot[0][None, :] + offs).reshape(-1)
    idx_b = (slot[1][None, :] + offs).reshape(-1)
    return _scatter_rows(pieces.reshape(npc * r, m), idx_a, idx_b, npc * n_slots).reshape(npc, n_slots, m)


def _gather_pieces(pieces, idx):
    npc, v, m = pieces.shape
    offs = (jnp.arange(npc, dtype=jnp.int32) * v).reshape((npc,) + (1,) * idx.ndim)
    flat = (idx[None] + offs).reshape(-1)
    return _gather_rows(pieces.reshape(npc * v, m), flat).reshape((npc,) + idx.shape + (m,))


def _moe_ffn_kernel(te_ref, nu_ref, x_ref, w1_ref, w3_ref, w2_ref, y_ref):
    @pl.when(pl.program_id(0) < nu_ref[0])
    def _():
        x = _load_unpacked(x_ref).astype(BF16)
        hid = _silu(_dot(x, w1_ref[...].astype(BF16))) * _dot(x, w3_ref[...].astype(BF16))
        _store_pieces(y_ref, _pack_halves(_dot(hid.astype(BF16), w2_ref[...].astype(BF16))))


def _moe_ffn(xs, te, n_used, w1, w3, w2, j):
    npc, p, m = xs.shape
    d = 2 * npc * m
    dff = w1.shape[3]
    tm = MOE_ROWS

    def row(i, te_ref, nu_ref):
        return (0, jnp.minimum(i, nu_ref[0] - 1), 0)

    def expert(i, te_ref, nu_ref):
        return (j, te_ref[i], 0, 0)

    grid_spec = pltpu.PrefetchScalarGridSpec(
        num_scalar_prefetch=2, grid=(p // tm,),
        in_specs=[pl.BlockSpec((npc, tm, m), row),
                  _resident((None, None, d, dff), expert),
                  _resident((None, None, d, dff), expert),
                  _resident((None, None, dff, d), expert)],
        out_specs=pl.BlockSpec((npc, tm, m), row))
    return pl.pallas_call(
        _moe_ffn_kernel, grid_spec=grid_spec,
        out_shape=jax.ShapeDtypeStruct((npc, p, m), jnp.int32),
        compiler_params=_cparams("arbitrary"), name="moe_ffn",
    )(te, n_used, xs, w1, w3, w2)


def _moe_combine_kernel(ya_ref, yb_ref, route_ref, h_ref, gt_ref, o_ref):
    r = route_ref[...]
    y = r[:, 2:3] * _load_unpacked(ya_ref) + r[:, 3:4] * _load_unpacked(yb_ref)
    o_ref[...] = h_ref[...] + gt_ref[...] * y


def _moe_combine(yg, route, h, mods, n_batch, tiles_per_batch, ctx_tiles):
    d = h.shape[1]
    npc, _, r, m = yg.shape
    tm = TOKEN_TILE
    per = tiles_per_batch - ctx_tiles
    h_idx = lambda i: ((i // per) * tiles_per_batch + ctx_tiles + i % per, 0)
    if ctx_tiles == 0:
        mod_idx = _mod_row_spec(5, tiles_per_batch, n_batch)
    else:
        mod_idx = lambda i: (i // per, 5, 0, 0)
    return pl.pallas_call(
        _moe_combine_kernel, grid=(r // tm,),
        in_specs=[pl.BlockSpec((npc, None, tm, m), lambda i: (0, 0, i, 0)),
                  pl.BlockSpec((npc, None, tm, m), lambda i: (0, 1, i, 0)),
                  pl.BlockSpec((tm, LANES), lambda i: (i, 0)),
                  pl.BlockSpec((tm, d), h_idx),
                  pl.BlockSpec((None, None, 1, d), mod_idx)],
        out_specs=pl.BlockSpec((tm, d), lambda i: (i, 0)),
        out_shape=jax.ShapeDtypeStruct((r, d), F32),
        compiler_params=_cparams("parallel"), name="moe_combine",
    )(yg, yg, route, h, mods)


def _proj_moe_residual(o, w_o, h, g, w_router, w1, w3, w2, j, mods, n_batch, t, n_ctx, latent_only):
    tpb = t // TOKEN_TILE
    ctx_tiles = n_ctx // TOKEN_TILE if latent_only else 0
    h, u_pieces, route, route_t, counts = _proj_router(o, w_o, j, h, g, mods, n_batch, tpb, ctx_tiles, w_router)
    if latent_only:
        npc, _, m = u_pieces.shape
        u_pieces = u_pieces.reshape(npc, n_batch, t, m)[:, :, n_ctx:].reshape(npc, -1, m)
        route = route.reshape(n_batch, t, LANES)[:, n_ctx:].reshape(-1, LANES)
        route_t = route_t.reshape(-1, n_batch, t)[:, :, n_ctx:].reshape(route_t.shape[0], -1)
    n_exp = w1.shape[1]
    slot, n_slots, te, n_used = _route_tables(route_t, counts[0, :n_exp].astype(jnp.int32), MOE_ROWS, n_exp)
    xs = _scatter_pieces(u_pieces, slot, n_slots)
    ys = _moe_ffn(xs, te, n_used, w1, w3, w2, j)
    yg = _gather_pieces(ys, slot)
    if latent_only:
        return _moe_combine(yg, route, h, mods, n_batch, tpb, ctx_tiles)
    return yg, route, h, mods


def kernel(x, c, ctx, c_ctx, w_mod, b_mod, norm_mix_g, norm_ffn_g, ev_w_in, ev_w_out, hg_lb_logits, hg_onorm_g, rg_conv_w, rg_conv_b, rg_wa, rg_ba, rg_wx, rg_bx, rg_lambda, na_w_qkv, na_w_o, na_q_g, na_k_g, na_rpb, ffn_w1, ffn_w3, ffn_w2, moe_router, moe_w1, moe_w3, moe_w2):
    bsz, seq, d = x.shape
    n_ctx = ctx.shape[1]
    t = n_ctx + seq
    depth = w_mod.shape[0]
    assert n_ctx % TOKEN_TILE == 0 and seq % TOKEN_TILE == 0 and t % ROW_TILE == 0 and n_ctx <= ROW_TILE

    rows = -(-(bsz + 1) // 8) * 8
    cond = jnp.zeros((rows, d), F32).at[:bsz].set(c).at[bsz].set(c_ctx)
    mods_all = _adaln(cond, w_mod, b_mod).reshape(depth, rows, N_MOD, 1, d)

    lb_p = jax.nn.softmax(hg_lb_logits.astype(F32), axis=0)
    lb_all = jnp.cumsum(lb_p, axis=0) - lb_p[0]

    h = jnp.concatenate([ctx, x], axis=1).reshape(bsz * t, d)
    pending = None
    for layer in range(depth):
        j = layer // 2
        mods = mods_all[layer]
        g_mix = norm_mix_g[layer].reshape(1, d)
        if layer % 2 == 0:
            if pending is None:
                p = _modulated_matmul(h, g_mix, mods, ev_w_in, j, F32, ROW_TILE // 2, bsz, t, n_ctx)
            else:
                h, p = _combined_modulated_matmul(*pending, g_mix, mods, ev_w_in, j, F32, ROW_TILE // 2,
                                                  bsz, t, n_ctx)
                pending = None
            p3 = p.reshape(bsz, t, -1)
            hg = _hgrn2(p3, lb_all[j], n_ctx)
            half = hg.shape[-1]
            rg = _rglru(p3, 5 * half // (d - half), rg_conv_w[j], rg_conv_b[j], rg_wa[j], rg_ba[j],
                        rg_wx[j], rg_bx[j], rg_lambda[j], n_ctx)
            h = _merge_swiglu(hg.reshape(bsz * t, half), rg.reshape(bsz * t, d - half), p, hg_onorm_g[j],
                              ev_w_out, h, norm_ffn_g[layer].reshape(1, d), ffn_w1, ffn_w3, ffn_w2, j,
                              mods, bsz, t, n_ctx)
        else:
            qkv = _modulated_matmul(h, g_mix, mods, na_w_qkv, j, BF16, ROW_TILE, bsz, t, n_ctx)
            o = _na_attention(qkv.reshape(bsz, t, 3 * d), na_q_g[j], na_k_g[j], na_rpb[j], n_ctx)
            res = _proj_moe_residual(o.reshape(bsz * t, d), na_w_o, h, norm_ffn_g[layer].reshape(1, d), moe_router[j],
                                     moe_w1, moe_w3, moe_w2, j, mods, bsz, t, n_ctx, layer == depth - 1)
            if layer == depth - 1:
                h = res
            else:
                pending = res
    if depth % 2 == 0:
        return h.reshape(bsz, seq, d)
    return h.reshape(bsz, t, d)[:, n_ctx:]
```

```python
import functools

import numpy as np
import jax
import jax.numpy as jnp
from jax import lax
from jax.experimental import pallas as pl
from jax.experimental.pallas import tpu as pltpu
from jax.experimental.pallas import tpu_sc as plsc

F32 = jnp.float32
BF16 = jnp.bfloat16

EPS = 1e-6
NEG_BIG = -1e9
LB_MIN = 1e-30
LOG2_E = 1.4426950408889634
N_MOD = 6
GRID_W = 64
HG_HEAD_DIM = 128
RG_CONV = 4
RG_C = 8.0
NA_HEADS = 16
NA_KR = 8
NA_KC = 16
N_EXPERTS = 8

LANES = 128
SC_WINDOW = 128
SC_PIECES = 2
TOKEN_TILE = 256
HG_CHUNK = 128
HG_LEVELS = 7
HG_SLOTS = 2
NA_SLOTS = 4
RG_SCAN_BLOCK = 256
RG_GROUP = 8
VMEM_LIMIT = 56 * 1024 * 1024


def _cparams(*sem):
    return pltpu.CompilerParams(dimension_semantics=sem, vmem_limit_bytes=VMEM_LIMIT)


def _resident(block_shape, index_map):
    return pl.BlockSpec(block_shape, index_map, pipeline_mode=pl.Buffered(1))


def _dot(a, b):
    return jnp.dot(a, b, preferred_element_type=F32)


def _dot_nt(a, b):
    return lax.dot_general(a, b, (((1,), (1,)), ((), ())), preferred_element_type=F32)


def _dot_tn(a, b):
    return lax.dot_general(a, b, (((0,), (0,)), ((), ())), preferred_element_type=F32)


def _split2(x):
    hi = x.astype(BF16)
    lo = (x - hi.astype(F32)).astype(BF16)
    return hi, lo


def _sigmoid(x):
    return 0.5 * jnp.tanh(0.5 * x) + 0.5


def _silu(x):
    return x * _sigmoid(x)


def _softplus_neg_abs(x):
    return jnp.log(1.0 + jnp.exp(-jnp.abs(x)))


def _log_sigmoid(x):
    return jnp.minimum(x, 0.0) - _softplus_neg_abs(x)


def _gelu_tanh(x):
    return 0.5 * x * (1.0 + jnp.tanh(np.sqrt(2.0 / np.pi) * (x + 0.044715 * (x * x * x))))


def _adaln_kernel(c_ref, w_ref, b_ref, o_ref):
    a_hi, a_lo = _split2(_silu(c_ref[...]))
    w_hi, w_lo = _split2(w_ref[...])
    o_ref[...] = _dot(a_hi, w_hi) + _dot(a_hi, w_lo) + _dot(a_lo, w_hi) + b_ref[...]


def _adaln(cond, w_mod, b_mod):
    depth, d, n = w_mod.shape
    r = cond.shape[0]
    tn = 1536
    return pl.pallas_call(
        _adaln_kernel,
        grid=(depth, n // tn),
        in_specs=[pl.BlockSpec((r, d), lambda l, j: (0, 0)),
                  pl.BlockSpec((None, d, tn), lambda l, j: (l, 0, j)),
                  pl.BlockSpec((None, 1, tn), lambda l, j: (l, 0, j))],
        out_specs=pl.BlockSpec((None, r, tn), lambda l, j: (l, 0, j)),
        out_shape=jax.ShapeDtypeStruct((depth, r, n), F32),
        compiler_params=_cparams("parallel", "parallel"),
        name="adaln",
    )(cond, w_mod, b_mod.reshape(depth, 1, n))


def _mod_row_spec(k, tiles_per_batch, n_batch):
    def idx(i):
        row = jnp.where(i % tiles_per_batch == 0, n_batch, i // tiles_per_batch)
        return (row, k, 0, 0)
    return idx


def _pack_halves(x):
    m = x.shape[1] // 2
    lo = lax.bitcast_convert_type(x[:, :m].astype(BF16).astype(F32), jnp.int32)
    hi = lax.bitcast_convert_type(x[:, m:].astype(BF16).astype(F32), jnp.int32)
    return lax.shift_right_logical(lo, 16) | (hi & jnp.int32(-65536))


def _unpack_halves(w):
    lo = lax.bitcast_convert_type(lax.shift_left(w, 16), F32)
    hi = lax.bitcast_convert_type(w & jnp.int32(-65536), F32)
    return lo, hi


def _store_pieces(ref, w):
    m = w.shape[1] // SC_PIECES
    for j in range(SC_PIECES):
        ref[j] = w[:, j * m:(j + 1) * m]


def _load_unpacked(ref):
    parts = [_unpack_halves(ref[j]) for j in range(SC_PIECES)]
    return jnp.concatenate([lo for lo, _ in parts] + [hi for _, hi in parts], axis=1)


def _proj_router_kernel(o_ref, wo_ref, h_ref, gt_ref, g_ref, sh_ref, sc_ref, wr_ref,
                        h1_ref, u_ref, route_ref, route_t_ref, cnt_ref, *, tiles_per_batch, skip_tiles):
    i = pl.program_id(0)

    @pl.when(i == 0)
    def _():
        cnt_ref[...] = jnp.zeros_like(cnt_ref)

    h1 = h_ref[...] + gt_ref[...] * _dot(o_ref[...], wo_ref[...].astype(BF16))
    h1_ref[...] = h1
    y = h1 * lax.rsqrt(jnp.mean(h1 * h1, axis=-1, keepdims=True) + EPS)
    u = (y * g_ref[...]) * (1.0 + sc_ref[...]) + sh_ref[...]
    _store_pieces(u_ref, _pack_halves(u))
    u0, u1 = _split2(u)
    w0, w1 = _split2(wr_ref[...])
    logits = _dot(u0, w0) + _dot(u0, w1) + _dot(u1, w0)
    lane = lax.broadcasted_iota(jnp.int32, logits.shape, 1).astype(F32)
    lg = jnp.where(lane < N_EXPERTS, logits, -jnp.inf)
    m1 = jnp.max(lg, axis=-1, keepdims=True)
    i1 = jnp.min(jnp.where(lg == m1, lane, float(LANES)), axis=-1, keepdims=True)
    lg2 = jnp.where(lane == i1, -jnp.inf, lg)
    m2 = jnp.max(lg2, axis=-1, keepdims=True)
    i2 = jnp.min(jnp.where(lg2 == m2, lane, float(LANES)), axis=-1, keepdims=True)
    e2 = jnp.exp(m2 - m1)
    den = 1.0 + e2

    routed = jnp.where(i % tiles_per_batch >= skip_tiles, 1.0, 0.0)
    pick_a = jnp.where(lane == i1, routed, 0.0)
    pick_b = jnp.where(lane == i2, routed, 0.0)
    tm = pick_a.shape[0]
    earlier = jnp.where(lax.broadcasted_iota(jnp.int32, (tm, tm), 1) < lax.broadcasted_iota(jnp.int32, (tm, tm), 0),
                        1.0, 0.0).astype(BF16)
    total_a = jnp.sum(pick_a, axis=0, keepdims=True)
    base = cnt_ref[0:1, :]
    rank_a = jnp.sum(pick_a * (base + _dot(earlier, pick_a.astype(BF16))), axis=-1, keepdims=True)
    rank_b = jnp.sum(pick_b * (base + total_a + _dot(earlier, pick_b.astype(BF16))), axis=-1, keepdims=True)
    cnt_ref[0:1, :] = base + total_a + jnp.sum(pick_b, axis=0, keepdims=True)

    route = jnp.where(lane == 0, i1, jnp.where(lane == 1, i2, jnp.where(lane == 2, 1.0 / den, jnp.where(
        lane == 3, e2 / den, jnp.where(lane == 4, rank_a, rank_b)))))
    route_ref[...] = route
    route_t_ref[...] = route.T[0:route_t_ref.shape[0], :]


def _proj_router(o, w_o, j, h, g, mods, n_batch, tiles_per_batch, skip_tiles, w_router):
    n, d = h.shape
    tm = TOKEN_TILE
    row = pl.BlockSpec((tm, d), lambda i: (i, 0))
    vec = pl.BlockSpec((1, d), lambda i: (0, 0))

    def mod_spec(k):
        return pl.BlockSpec((None, None, 1, d), _mod_row_spec(k, tiles_per_batch, n_batch))

    wr = jnp.zeros((d, LANES), F32).at[:, :N_EXPERTS].set(w_router)
    kern = functools.partial(_proj_router_kernel, tiles_per_batch=tiles_per_batch, skip_tiles=skip_tiles)
    return pl.pallas_call(
        kern, grid=(n // tm,),
        in_specs=[row, _resident((None, d, d), lambda i: (j, 0, 0)), row, mod_spec(2),
                  vec, mod_spec(3), mod_spec(4), pl.BlockSpec((d, LANES), lambda i: (0, 0))],
        out_specs=[row,
                   pl.BlockSpec((SC_PIECES, tm, d // 2 // SC_PIECES), lambda i: (0, i, 0)),
                   pl.BlockSpec((tm, LANES), lambda i: (i, 0)),
                   pl.BlockSpec((8, tm), lambda i: (0, i)),
                   pl.BlockSpec((8, LANES), lambda i: (0, 0))],
        out_shape=[jax.ShapeDtypeStruct((n, d), F32),
                   jax.ShapeDtypeStruct((SC_PIECES, n, d // 2 // SC_PIECES), jnp.int32),
                   jax.ShapeDtypeStruct((n, LANES), F32),
                   jax.ShapeDtypeStruct((8, n), F32),
                   jax.ShapeDtypeStruct((8, LANES), F32)],
        compiler_params=_cparams("arbitrary"), name="proj_router",
    )(o, w_o, h, mods, g, mods, mods, wr)


ROW_TILE = 768


def _row_gate(ctx_ref, own_ref, shape, tile_in_batch, n_ctx):
    rows = lax.broadcasted_iota(jnp.int32, shape, 0)
    is_ctx = (rows < n_ctx) & (tile_in_batch == 0)
    return jnp.where(is_ctx, ctx_ref[...], own_ref[...])


def _modulated_rows(x, g_ref, shc_ref, sh_ref, scc_ref, sc_ref, tile_in_batch, n_ctx):
    y = x * lax.rsqrt(jnp.mean(x * x, axis=-1, keepdims=True) + EPS)
    shift = _row_gate(shc_ref, sh_ref, x.shape, tile_in_batch, n_ctx)
    scale = _row_gate(scc_ref, sc_ref, x.shape, tile_in_batch, n_ctx)
    return (y * g_ref[...]) * (1.0 + scale) + shift


def _stream_specs(d, n_batch, tiles_per_batch, k, nargs):
    if nargs == 1:
        ctx = lambda i: (n_batch, k, 0, 0)
        own = lambda i: (i // tiles_per_batch, k, 0, 0)
    else:
        ctx = lambda i, f: (n_batch, k, 0, 0)
        own = lambda i, f: (i // tiles_per_batch, k, 0, 0)
    return [pl.BlockSpec((None, None, 1, d), ctx), pl.BlockSpec((None, None, 1, d), own)]


def _mod_mm_kernel(h_ref, g_ref, shc_ref, sh_ref, scc_ref, sc_ref, w_ref, o_ref, *, n_ctx, tiles_per_batch):
    x = _modulated_rows(h_ref[...], g_ref, shc_ref, sh_ref, scc_ref, sc_ref,
                        pl.program_id(0) % tiles_per_batch, n_ctx).astype(BF16)
    o_ref[...] = _dot(x, w_ref[...].astype(BF16)).astype(o_ref.dtype)


def _concat_mod_mm_kernel(ctx_ref, x_ref, g_ref, sh_ref, sc_ref, w_ref, h_ref, o_ref, *, tiles_per_batch):
    h = jnp.where(pl.program_id(0) % tiles_per_batch == 0, ctx_ref[...], x_ref[...])
    h_ref[...] = h
    y = h * lax.rsqrt(jnp.mean(h * h, axis=-1, keepdims=True) + EPS)
    u = ((y * g_ref[...]) * (1.0 + sc_ref[...]) + sh_ref[...]).astype(BF16)
    o_ref[...] = _dot(u, w_ref[...].astype(BF16)).astype(o_ref.dtype)


def _concat_modulated_matmul(ctx, x, g, mods, w, j, out_dtype, n_batch):
    _, n_ctx, d = ctx.shape
    seq = x.shape[1]
    tm = TOKEN_TILE
    assert n_ctx == tm and seq % tm == 0
    lat = seq // tm
    tpb = lat + 1
    n = n_batch * (n_ctx + seq)
    nout = w.shape[2]

    def mod_spec(k):
        return pl.BlockSpec((None, None, 1, d), _mod_row_spec(k, tpb, n_batch))

    kern = functools.partial(_concat_mod_mm_kernel, tiles_per_batch=tpb)
    return pl.pallas_call(
        kern, grid=(n // tm,),
        in_specs=[pl.BlockSpec((tm, d), lambda i: (i // tpb, 0)),
                  pl.BlockSpec((tm, d), lambda i: ((i // tpb) * lat + jnp.maximum(i % tpb, 1) - 1, 0)),
                  pl.BlockSpec((1, d), lambda i: (0, 0)), mod_spec(0), mod_spec(1),
                  _resident((None, d, nout), lambda i: (j, 0, 0))],
        out_specs=[pl.BlockSpec((tm, d), lambda i: (i, 0)), pl.BlockSpec((tm, nout), lambda i: (i, 0))],
        out_shape=[jax.ShapeDtypeStruct((n, d), F32), jax.ShapeDtypeStruct((n, nout), out_dtype)],
        compiler_params=_cparams("parallel"), name="concat_modulated_matmul",
    )(ctx.reshape(n_batch * n_ctx, d), x.reshape(n_batch * seq, d), g, mods, mods, w)


def _combine_mod_mm_kernel(ya_ref, yb_ref, route_ref, h_ref, g2c_ref, g2_ref, g_ref, shc_ref, sh_ref,
                           scc_ref, sc_ref, w_ref, h1_ref, o_ref, *, n_ctx, tiles_per_batch):
    tile_in_batch = pl.program_id(0) % tiles_per_batch
    r = route_ref[...]
    y = r[:, 2:3] * _load_unpacked(ya_ref) + r[:, 3:4] * _load_unpacked(yb_ref)
    h1 = h_ref[...] + _row_gate(g2c_ref, g2_ref, y.shape, tile_in_batch, n_ctx) * y
    h1_ref[...] = h1
    x = _modulated_rows(h1, g_ref, shc_ref, sh_ref, scc_ref, sc_ref, tile_in_batch, n_ctx).astype(BF16)
    o_ref[...] = _dot(x, w_ref[...].astype(BF16)).astype(o_ref.dtype)


def _combined_modulated_matmul(yg, route, h, mods_prev, g, mods, w, j, out_dtype, tm, n_batch, t, n_ctx):
    n, d = h.shape
    nout = w.shape[2]
    npc, _, _, m = yg.shape
    tpb = t // tm
    assert t % tm == 0 and n_ctx <= tm
    kern = functools.partial(_combine_mod_mm_kernel, n_ctx=n_ctx, tiles_per_batch=tpb)
    return pl.pallas_call(
        kern, grid=(n // tm,),
        in_specs=[pl.BlockSpec((npc, None, tm, m), lambda i: (0, 0, i, 0)),
                  pl.BlockSpec((npc, None, tm, m), lambda i: (0, 1, i, 0)),
                  pl.BlockSpec((tm, LANES), lambda i: (i, 0)),
                  pl.BlockSpec((tm, d), lambda i: (i, 0))] + _stream_specs(d, n_batch, tpb, 5, 1)
        + [pl.BlockSpec((1, d), lambda i: (0, 0))]
        + _stream_specs(d, n_batch, tpb, 0, 1) + _stream_specs(d, n_batch, tpb, 1, 1)
        + [_resident((None, d, nout), lambda i: (j, 0, 0))],
        out_specs=[pl.BlockSpec((tm, d), lambda i: (i, 0)), pl.BlockSpec((tm, nout), lambda i: (i, 0))],
        out_shape=[jax.ShapeDtypeStruct((n, d), F32), jax.ShapeDtypeStruct((n, nout), out_dtype)],
        compiler_params=_cparams("parallel"), name="combined_modulated_matmul",
    )(yg, yg, route, h, mods_prev, mods_prev, g, mods, mods, mods, mods, w)


def _modulated_matmul(h, g, mods, w, j, out_dtype, tm, n_batch, t, n_ctx):
    n, d = h.shape
    nout = w.shape[2]
    tpb = t // tm
    assert t % tm == 0 and n_ctx <= tm
    kern = functools.partial(_mod_mm_kernel, n_ctx=n_ctx, tiles_per_batch=tpb)
    return pl.pallas_call(
        kern, grid=(n // tm,),
        in_specs=[pl.BlockSpec((tm, d), lambda i: (i, 0)),
                  pl.BlockSpec((1, d), lambda i: (0, 0))]
        + _stream_specs(d, n_batch, tpb, 0, 1) + _stream_specs(d, n_batch, tpb, 1, 1)
        + [_resident((None, d, nout), lambda i: (j, 0, 0))],
        out_specs=pl.BlockSpec((tm, nout), lambda i: (i, 0)),
        out_shape=jax.ShapeDtypeStruct((n, nout), out_dtype),
        compiler_params=_cparams("parallel"), name="modulated_matmul",
    )(h, g, mods, mods, mods, mods, w)


def _hg_tables():
    c = HG_CHUNK
    t = np.arange(c)[:, None]
    u = np.arange(c)[None, :]
    mats = [(u <= t), (u > t)]
    for k in range(1, HG_LEVELS + 1):
        m = ((t >> k) << k) + (1 << (k - 1)) - 1
        upper = t > m
        mats.append(np.where(upper, (u > m) & (u <= t), (u > t) & (u <= m)))
    fwd = np.concatenate(mats, axis=0).astype(np.float32)
    bwd = fwd.reshape(-1, c, c)[:, ::-1, ::-1].reshape(-1, c)
    x = t ^ u
    lvl = np.where(t > u, np.floor(np.log2(np.maximum(x, 1))).astype(np.int32) + 1, -1)
    lvl = np.where(t == u, 0, lvl).astype(np.int32)
    return np.stack([fwd, bwd]), np.stack([lvl, lvl.T])


def _hg_kernel(q_ref, ff_ref, fb_ref, v_ref, lb_ref, m_ref, lvl_ref, o_ref, of_ref, ob_ref,
               qk_ref, el_ref, io_ref, tot_ref, att_ref, *, n_ctx_chunks, n_chunks):
    c = HG_CHUNK
    gap = HG_SLOTS // 2
    t = o_ref.shape[0]
    z_refs = (ff_ref, fb_ref)
    out_refs = (of_ref, ob_ref)
    for ref in (qk_ref, el_ref, io_ref, tot_ref, att_ref):
        ref[...] = jnp.zeros(ref.shape, ref.dtype)

    def start_row(d, j):
        j = jnp.clip(j, 0, n_chunks - 1)
        if d == 1:
            j = jnp.where(j < n_ctx_chunks, n_ctx_chunks - 1 - j, n_chunks - 1 - (j - n_ctx_chunks))
        return pl.multiple_of(j * c, c)

    def stage1(d, j, slot):
        t0 = start_row(d, j)
        lb = lb_ref[d:d + 1, :]
        log_lb = jnp.log(jnp.maximum(lb, LB_MIN))
        q = _silu(q_ref[pl.ds(t0, c), :])
        z = z_refs[d][pl.ds(t0, c), :]
        k = (1.0 - lb) * _sigmoid(-z)
        b = jnp.log1p(-lb) + _log_sigmoid(z)
        logf = jnp.maximum(log_lb, b) + _softplus_neg_abs(log_lb - b)
        g_hi, g_lo = _split2(logf * LOG2_E)
        m_ends = m_ref[d, 0:2 * c, :]
        e_ends = jnp.exp2(_dot(m_ends, g_hi) + _dot(m_ends, g_lo))
        el_ref[d, slot] = jnp.exp2(_dot(m_ref[d, 2 * c:, :], g_hi).astype(BF16))
        qk_ref[d, slot, 0] = q.astype(BF16)
        qk_ref[d, slot, 1] = k.astype(BF16)
        io_ref[d, slot, 0] = (q * e_ends[0:c]).astype(BF16)
        io_ref[d, slot, 1] = (k * e_ends[c:]).astype(BF16)
        tot_row = c - 1 if d == 0 else 0
        tot_ref[d, slot, 0:1, :] = e_ends[tot_row:tot_row + 1, :]

    def stage2(d, slot):
        qb = qk_ref[d, slot, 0]
        kb = qk_ref[d, slot, 1]
        lvl = lvl_ref[d]
        att = jnp.where(lvl == 0, _dot_nt(qb, kb), 0.0)
        for lev in range(1, HG_LEVELS + 1):
            el = el_ref[d, slot, (lev - 1) * c:lev * c, :]
            att = jnp.where(lvl == lev, _dot_nt(qb * el, kb * el), att)
        att_ref[d, slot] = att.astype(BF16)

    def stage3(d, j, slot, st):
        t0 = start_row(d, j)
        v = v_ref[pl.ds(t0, c), :].astype(BF16)
        row = jnp.where(j < n_chunks, t0, t)
        out_refs[d][pl.ds(pl.multiple_of(row, c), c), :] = (
            _dot(att_ref[d, slot], v) + _dot_nt(io_ref[d, slot, 0], st.astype(BF16)))
        return st * tot_ref[d, slot, 0:1, :] + _dot_tn(v, io_ref[d, slot, 1])

    def steps(jj, carry):
        st = list(carry)
        for slot in range(HG_SLOTS):
            i = HG_SLOTS * jj + slot
            for d in range(2):
                st[d] = stage3(d, i - 2 * gap, slot, st[d])
            for d in range(2):
                stage2(d, (slot + gap) % HG_SLOTS)
            for d in range(2):
                stage1(d, i, slot)
        return tuple(st)

    zero = jnp.zeros((c, c), F32)
    n_steps = -(-(n_chunks + 2 * gap) // HG_SLOTS)
    lax.fori_loop(0, n_steps, steps, (zero, zero))
    o_ref[...] = (of_ref[0:t, :] + ob_ref[0:t, :]).astype(o_ref.dtype)


def _hgrn2(p3, lb, n_ctx):
    bsz, t, _ = p3.shape
    width = lb.shape[1]
    heads = width // HG_HEAD_DIM
    mats, lvl = _hg_tables()
    c = HG_CHUNK

    def col(group):
        return pl.BlockSpec((None, t, c), lambda b, h: (b, 0, group * heads + h))

    kern = functools.partial(_hg_kernel, n_ctx_chunks=n_ctx // c, n_chunks=t // c)
    return pl.pallas_call(
        kern, grid=(bsz, heads),
        in_specs=[col(0), col(1), col(2), col(3),
                  pl.BlockSpec((2, c), lambda b, h: (0, h)),
                  pl.BlockSpec(mats.shape, lambda b, h: (0, 0, 0)),
                  pl.BlockSpec(lvl.shape, lambda b, h: (0, 0, 0))],
        out_specs=pl.BlockSpec((None, t, c), lambda b, h: (b, 0, h)),
        out_shape=jax.ShapeDtypeStruct((bsz, t, width), F32),
        scratch_shapes=[pltpu.VMEM((t + c, c), F32), pltpu.VMEM((t + c, c), F32),
                        pltpu.VMEM((2, HG_SLOTS, 2, c, c), BF16), pltpu.VMEM((2, HG_SLOTS, HG_LEVELS * c, c), BF16),
                        pltpu.VMEM((2, HG_SLOTS, 2, c, c), BF16), pltpu.VMEM((2, HG_SLOTS, 8, c), F32),
                        pltpu.VMEM((2, HG_SLOTS, c, c), BF16)],
        compiler_params=_cparams("parallel", "parallel"), name="hgrn2",
    )(p3, p3, p3, p3, lb, jnp.asarray(mats, BF16), jnp.asarray(lvl))


PAD = 8


def _rg_kernel(x_ref, cw_ref, cb_ref, wa_ref, ba_ref, wx_ref, bx_ref, lam_ref, o_ref,
               xpad_ref, xc_ref, ob_ref, *, n_ctx):
    t, w = x_ref.shape
    n_lat = t - n_ctx
    lat0 = n_ctx + 2 * PAD
    xpad_ref[...] = jnp.zeros(xpad_ref.shape, F32)
    xpad_ref[PAD:PAD + n_ctx, :] = x_ref[0:n_ctx, :]
    xpad_ref[lat0:lat0 + n_lat, :] = x_ref[n_ctx:t, :]
    left = RG_CONV // 2
    for base, dst, n in ((PAD, 0, n_ctx), (lat0, n_ctx, n_lat)):
        acc = cb_ref[...] + cw_ref[0:1, :] * xpad_ref[base - left:base - left + n, :]
        for j in range(1, RG_CONV):
            acc = acc + cw_ref[j:j + 1, :] * xpad_ref[base - left + j:base - left + j + n, :]
        xc_ref[dst:dst + n, :] = acc

    blk = RG_SCAN_BLOCK
    n_blocks = t // blk
    n_ctx_blocks = n_ctx // blk
    sub = lax.broadcasted_iota(jnp.int32, (blk // RG_GROUP, RG_GROUP, w), 1)
    c_log_sig_lam = RG_C * _log_sigmoid(lam_ref[...])

    def scan_block(d, t0, hin, out_ref):
        xc = xc_ref[pl.ds(t0, blk), :]
        xb = xc.astype(BF16)
        r = _sigmoid(_dot(xb, wa_ref[d]) + ba_ref[d:d + 1, :])
        ig = _sigmoid(_dot(xb, wx_ref[d]) + bx_ref[d:d + 1, :])
        log_a = r * c_log_sig_lam[d:d + 1, :]
        a = jnp.exp(log_a)
        b = jnp.sqrt(jnp.maximum(1.0 - a * a, 0.0)) * (ig * xc)
        n_groups = blk // RG_GROUP
        a = a.reshape(n_groups, RG_GROUP, w)
        b = b.reshape(n_groups, RG_GROUP, w)
        s = 1
        while s < RG_GROUP:
            shift = s if d == 0 else RG_GROUP - s
            keep = (sub >= s) if d == 0 else (sub < RG_GROUP - s)
            b = jnp.where(keep, a * pltpu.roll(b, shift, 1), 0.0) + b
            a = jnp.where(keep, a * pltpu.roll(a, shift, 1), a)
            s *= 2
        for gi in range(n_groups):
            gj = gi if d == 0 else n_groups - 1 - gi
            hg = a[gj] * hin + b[gj]
            out_ref[pl.ds(t0 + gj * RG_GROUP, RG_GROUP), :] = hg
            hin = hg[RG_GROUP - 1:RG_GROUP, :] if d == 0 else hg[0:1, :]
        return hin

    def step(i, carry):
        h_f, h_b = carry
        bb = jnp.where(i < n_ctx_blocks, n_ctx_blocks - 1 - i, n_blocks - 1 - (i - n_ctx_blocks))
        h_f = scan_block(0, pl.multiple_of(i * blk, blk), h_f, o_ref)
        h_b = scan_block(1, pl.multiple_of(bb * blk, blk), h_b, ob_ref)
        return h_f, h_b

    zero = jnp.zeros((1, w), F32)
    lax.fori_loop(0, n_blocks, step, (zero, zero))
    o_ref[...] += ob_ref[...]


def _blockdiag_dense(w):
    nd, nb, k, _ = w.shape
    eye = jnp.eye(nb, dtype=w.dtype)
    return jnp.einsum('dnij,nm->dnimj', w, eye).reshape(nd, nb * k, nb * k)


def _rglru(p3, col_block, conv_w, conv_b, wa, ba, wx, bx, lam, n_ctx):
    bsz, t, _ = p3.shape
    w = conv_w.shape[1]
    full = lambda shape: pl.BlockSpec(shape, lambda b: (0,) * len(shape))
    kern = functools.partial(_rg_kernel, n_ctx=n_ctx)
    return pl.pallas_call(
        kern, grid=(bsz,),
        in_specs=[pl.BlockSpec((None, t, w), lambda b: (b, 0, col_block)),
                  full((RG_CONV, w)), full((1, w)),
                  full((2, w, w)), full((2, w)), full((2, w, w)), full((2, w)), full((2, w))],
        out_specs=pl.BlockSpec((None, t, w), lambda b: (b, 0, 0)),
        out_shape=jax.ShapeDtypeStruct((bsz, t, w), F32),
        scratch_shapes=[pltpu.VMEM((t + 3 * PAD, w), F32)] + [pltpu.VMEM((t, w), F32)] * 2,
        compiler_params=_cparams("parallel"), name="rglru",
    )(p3, conv_w, conv_b.reshape(1, w), _blockdiag_dense(wa).astype(BF16), ba,
      _blockdiag_dense(wx).astype(BF16), bx, lam)


def _na_bias_table(rpb):
    q = np.arange(GRID_W)[:, None]
    kc = np.arange(GRID_W)[None, :]
    wstart = np.clip(q - NA_KC // 2, 0, GRID_W - NA_KC)
    mask = (kc >= wstart) & (kc < wstart + NA_KC)
    coff = np.clip(kc - q + NA_KC - 1, 0, 2 * NA_KC - 2)
    sel = (coff[None] == np.arange(2 * NA_KC - 1)[:, None, None]).astype(np.float32)
    toe = jnp.einsum('hdj,jqk->hqdk', rpb.astype(F32), sel, precision=lax.Precision.HIGHEST)
    toe = jnp.where(mask[None, :, None, :], toe, NEG_BIG)
    h = toe.shape[0]
    flat = toe.reshape(h // 2, 2 * GRID_W, -1)
    width = (2 * NA_KR) * GRID_W
    even = jnp.pad(flat, ((0, 0), (0, 0), (0, width - flat.shape[2])))
    odd = jnp.pad(flat[:, :, GRID_W:], ((0, 0), (0, 0), (0, width - flat.shape[2] + GRID_W)))
    return jnp.stack([even, odd], axis=1)


def _stack_heads(x, hd):
    lane = lax.broadcasted_iota(jnp.int32, x.shape, 1)
    zero = jnp.zeros_like(x)
    return jnp.concatenate([jnp.where(lane < hd, x, zero), jnp.where(lane >= hd, x, zero)], axis=0)


def _unstack_heads(o2, hd):
    n = o2.shape[0] // 2
    lane = lax.broadcasted_iota(jnp.int32, (n, o2.shape[1]), 1)
    return jnp.where(lane < hd, o2[:n], o2[n:])


def _na_kernel(q_ref, k_ref, v_ref, qg_ref, kg_ref, bias_ref, o_ref, qn_ref, kn_ref,
               s_ref, p_ref, den_ref, *, n_ctx, scale):
    t, w = q_ref.shape
    hd = w // 2
    rows_grid = (t - n_ctx) // GRID_W
    r_i = lax.broadcasted_iota(jnp.int32, (w, w), 0)
    c_i = lax.broadcasted_iota(jnp.int32, (w, w), 1)
    seg = jnp.where((r_i // hd) == (c_i // hd), 1.0 / hd, 0.0).astype(BF16)

    def normed(x, g):
        ms = _dot((x * x).astype(BF16), seg)
        return (x * lax.rsqrt(ms + EPS) * g).astype(BF16)

    qn_ref[...] = normed(q_ref[...].astype(F32), qg_ref[...] * scale)
    kn_ref[...] = normed(k_ref[...].astype(F32), kg_ref[...])
    vb_ref = v_ref

    kc = kn_ref[0:n_ctx, :]
    vc = vb_ref[0:n_ctx, :]

    s = _dot_nt(_stack_heads(qn_ref[0:n_ctx, :], hd), kc)
    p = jnp.exp(s - jnp.max(s, axis=-1, keepdims=True))
    o2 = _dot(p.astype(BF16), vc) / jnp.sum(p, axis=-1, keepdims=True)
    o_ref[0:n_ctx, :] = _unstack_heads(o2, hd).astype(o_ref.dtype)

    kr = NA_KR
    n_win = kr * GRID_W

    def window(r):
        r = jnp.clip(r, 0, rows_grid - 1)
        rs = jnp.clip(r - kr // 2, 0, rows_grid - kr)
        q0 = pl.multiple_of(n_ctx + r * GRID_W, GRID_W)
        k0 = pl.multiple_of(n_ctx + rs * GRID_W, GRID_W)
        return q0, k0, rs - r + (kr - 1)

    gap = NA_SLOTS // 2
    for slot in range(NA_SLOTS):
        s_ref[slot] = jnp.zeros(s_ref.shape[1:], F32)
        p_ref[slot] = jnp.zeros(p_ref.shape[1:], BF16)
        den_ref[slot] = jnp.ones(den_ref.shape[1:], F32)

    def step(i, slot):
        q0, k0, _ = window(i - 2 * gap)
        pr = p_ref[slot]
        o2 = (_dot(pr[:, :n_win], vb_ref[pl.ds(k0, n_win), :]) + _dot(pr[:, n_win:], vc)) / den_ref[slot]
        o_ref[pl.ds(q0, GRID_W), :] = _unstack_heads(o2, hd).astype(o_ref.dtype)

        mid = (slot + gap) % NA_SLOTS
        sc = s_ref[mid]
        pe = jnp.exp(sc - jnp.max(sc, axis=-1, keepdims=True))
        den_ref[mid] = jnp.sum(pe, axis=-1, keepdims=True)
        p_ref[mid] = pe.astype(BF16)

        q0, k0, off = window(i)
        q2 = _stack_heads(qn_ref[pl.ds(q0, GRID_W), :], hd)
        bias = bias_ref[off % 2, :, pl.ds(pl.multiple_of((off // 2) * (2 * GRID_W), 2 * GRID_W), n_win)]
        s_ref[slot, :, :n_win] = _dot_nt(q2, kn_ref[pl.ds(k0, n_win), :]) + bias
        s_ref[slot, :, n_win:] = _dot_nt(q2, kc)

    def steps(j, carry):
        for slot in range(NA_SLOTS):
            step(NA_SLOTS * j + slot, slot)
        return carry

    lax.fori_loop(0, (rows_grid + 2 * gap) // NA_SLOTS, steps, 0)


def _na_attention(qkv3, q_g, k_g, rpb, n_ctx):
    bsz, t, d3 = qkv3.shape
    d = d3 // 3
    hd = d // NA_HEADS
    w = 2 * hd
    pairs = d // w
    rows_grid = (t - n_ctx) // GRID_W
    assert w == LANES and rows_grid >= NA_KR and rows_grid % NA_SLOTS == 0
    n_keys = NA_KR * GRID_W + n_ctx
    bias = _na_bias_table(rpb)
    kern = functools.partial(_na_kernel, n_ctx=n_ctx, scale=hd ** -0.5)

    def col(group):
        return pl.BlockSpec((None, t, w), lambda hp, b: (b, 0, group * pairs + hp))

    gain = lambda g: jnp.tile(g.astype(F32), 2).reshape(1, w)
    return pl.pallas_call(
        kern, grid=(pairs, bsz),
        in_specs=[col(0), col(1), col(2),
                  pl.BlockSpec((1, w), lambda hp, b: (0, 0)),
                  pl.BlockSpec((1, w), lambda hp, b: (0, 0)),
                  pl.BlockSpec((None,) + bias.shape[1:], lambda hp, b: (hp, 0, 0, 0))],
        out_specs=pl.BlockSpec((None, t, w), lambda hp, b: (b, 0, hp)),
        out_shape=jax.ShapeDtypeStruct((bsz, t, d), BF16),
        scratch_shapes=[pltpu.VMEM((t, w), BF16)] * 2
        + [pltpu.VMEM((NA_SLOTS, 2 * GRID_W, n_keys), F32), pltpu.VMEM((NA_SLOTS, 2 * GRID_W, n_keys), BF16),
           pltpu.VMEM((NA_SLOTS, 2 * GRID_W, 1), F32)],
        compiler_params=_cparams("parallel", "parallel"), name="na_attention",
    )(qkv3, qkv3, qkv3, gain(q_g), gain(k_g), bias)


FFN_ROWS = 384


def _merge_swiglu_kernel(hg_ref, rg_ref, gate_ref, y_ref, og_ref, wo_ref, h_ref, g1c_ref, g1_ref,
                         g_ref, shc_ref, sh_ref, scc_ref, sc_ref, w1_ref, w3_ref, w2_ref, g2c_ref, g2_ref,
                         o_ref, *, n_ctx, tiles_per_batch):
    tile_in_batch = pl.program_id(0) % tiles_per_batch
    hg = hg_ref[...]
    half = hg.shape[1]
    hn = hg * lax.rsqrt(jnp.mean(hg * hg, axis=-1, keepdims=True) + EPS) * og_ref[...]
    a = (hn * _silu(gate_ref[...])).astype(BF16)
    b = (rg_ref[...] * _gelu_tanh(y_ref[...])).astype(BF16)
    mix = _dot(a, wo_ref[0:half, :].astype(BF16)) + _dot(b, wo_ref[half:, :].astype(BF16))
    h1 = h_ref[...] + _row_gate(g1c_ref, g1_ref, mix.shape, tile_in_batch, n_ctx) * mix
    x = _modulated_rows(h1, g_ref, shc_ref, sh_ref, scc_ref, sc_ref, tile_in_batch, n_ctx).astype(BF16)
    hid = _silu(_dot(x, w1_ref[...].astype(BF16))) * _dot(x, w3_ref[...].astype(BF16))
    y = _dot(hid.astype(BF16), w2_ref[...].astype(BF16))
    o_ref[...] = h1 + _row_gate(g2c_ref, g2_ref, y.shape, tile_in_batch, n_ctx) * y


def _merge_swiglu(hg, rg, p, onorm_g, w_out, h, g, w1, w3, w2, j, mods, n_batch, t, n_ctx):
    n, d = h.shape
    half = hg.shape[1]
    dff = w1.shape[2]
    tm = FFN_ROWS
    tpb = t // tm
    assert t % tm == 0 and n_ctx <= tm
    blk = lambda c: pl.BlockSpec((tm, half), lambda i: (i, c))
    vec = lambda width: pl.BlockSpec((1, width), lambda i: (0, 0))
    kern = functools.partial(_merge_swiglu_kernel, n_ctx=n_ctx, tiles_per_batch=tpb)
    return pl.pallas_call(
        kern, grid=(n // tm,),
        in_specs=[blk(0), blk(0), blk(4), blk(6), vec(half),
                  _resident((None, d, d), lambda i: (j, 0, 0)),
                  pl.BlockSpec((tm, d), lambda i: (i, 0))] + _stream_specs(d, n_batch, tpb, 2, 1)
        + [vec(d)] + _stream_specs(d, n_batch, tpb, 3, 1) + _stream_specs(d, n_batch, tpb, 4, 1)
        + [_resident((None, d, dff), lambda i: (j, 0, 0)),
           _resident((None, d, dff), lambda i: (j, 0, 0)),
           _resident((None, dff, d), lambda i: (j, 0, 0))] + _stream_specs(d, n_batch, tpb, 5, 1),
        out_specs=pl.BlockSpec((tm, d), lambda i: (i, 0)),
        out_shape=jax.ShapeDtypeStruct((n, d), F32),
        compiler_params=_cparams("parallel"), name="merge_swiglu",
    )(hg, rg, p, p, onorm_g.reshape(1, half), w_out, h, mods, mods,
      g, mods, mods, mods, mods, w1, w3, w2, mods, mods)


MOE_ROWS = 512


def _route_tables(route_t, counts, tm, n_exp):
    r = route_t.shape[1]
    experts = route_t[0:2].astype(jnp.int32)
    ranks = route_t[4:6].astype(jnp.int32)
    padded = ((counts + tm - 1) // tm) * tm
    ends = jnp.cumsum(padded)
    starts = ends - padded
    slot = ranks
    for e in range(n_exp):
        slot = slot + jnp.where(experts == e, starts[e], 0)
    p_slots = 2 * r + n_exp * tm
    n_tiles = p_slots // tm
    n_used = ends[-1] // tm
    tile = jnp.arange(n_tiles, dtype=jnp.int32)
    te = jnp.sum(((tile * tm)[:, None] >= ends[None, :]).astype(jnp.int32), axis=1)
    te = jnp.where(tile < n_used, te, te[n_used - 1])
    return slot, p_slots, te, n_used.reshape(1)


def _gather_rows(table, idx):
    p = idx.shape[0]
    m = table.shape[1]
    mesh = plsc.VectorSubcoreMesh(core_axis_name="c", subcore_axis_name="s")

    @functools.partial(pl.kernel, out_type=jax.ShapeDtypeStruct((p, m), table.dtype), mesh=mesh)
    def gather(table_hbm, idx_hbm, out_hbm):
        def body(idx_vmem, out_vmem):
            pltpu.sync_copy(table_hbm.at[idx_vmem.at[0]], out_vmem)

        pltpu.emit_pipeline(
            body, grid=(p // SC_WINDOW,),
            in_specs=[pl.BlockSpec((1, SC_WINDOW), lambda i: (0, i))],
            out_specs=[pl.BlockSpec((SC_WINDOW, m), lambda i: (i, 0))],
            core_axis_name=("c", "s"), dimension_semantics=(pltpu.PARALLEL,),
        )(idx_hbm, out_hbm)

    return gather(table, idx.reshape(1, p))


def _scatter_rows(rows, idx_a, idx_b, n_out):
    q, m = rows.shape
    mesh = plsc.VectorSubcoreMesh(core_axis_name="c", subcore_axis_name="s")

    @functools.partial(pl.kernel, out_type=jax.ShapeDtypeStruct((n_out, m), rows.dtype), mesh=mesh)
    def scatter(rows_hbm, ia_hbm, ib_hbm, out_hbm):
        def body(rows_vmem, ia_vmem, ib_vmem):
            pltpu.sync_copy(rows_vmem, out_hbm.at[ia_vmem.at[0]])
            pltpu.sync_copy(rows_vmem, out_hbm.at[ib_vmem.at[0]])

        pltpu.emit_pipeline(
            body, grid=(q // SC_WINDOW,),
            in_specs=[pl.BlockSpec((SC_WINDOW, m), lambda i: (i, 0)),
                      pl.BlockSpec((1, SC_WINDOW), lambda i: (0, i)),
                      pl.BlockSpec((1, SC_WINDOW), lambda i: (0, i))],
            out_specs=[],
            core_axis_name=("c", "s"), dimension_semantics=(pltpu.PARALLEL,),
        )(rows_hbm, ia_hbm, ib_hbm)

    return scatter(rows, idx_a.reshape(1, q), idx_b.reshape(1, q))


def _scatter_pieces(pieces, slot, n_slots):
    npc, r, m = pieces.shape
    offs = jnp.arange(npc, dtype=jnp.int32)[:, None] * n_slots
    idx_a = (slot[0][None, :] + offs).reshape(-1)
    idx_b = (slot[1][None, :] + offs).reshape(-1)
    return _scatter_rows(pieces.reshape(npc * r, m), idx_a, idx_b, npc * n_slots).reshape(npc, n_slots, m)


def _gather_pieces(pieces, idx):
    npc, v, m = pieces.shape
    offs = (jnp.arange(npc, dtype=jnp.int32) * v).reshape((npc,) + (1,) * idx.ndim)
    flat = (idx[None] + offs).reshape(-1)
    return _gather_rows(pieces.reshape(npc * v, m), flat).reshape((npc,) + idx.shape + (m,))


def _moe_ffn_kernel(te_ref, nu_ref, x_ref, w1_ref, w3_ref, w2_ref, y_ref):
    @pl.when(pl.program_id(0) < nu_ref[0])
    def _():
        x = _load_unpacked(x_ref).astype(BF16)
        hid = _silu(_dot(x, w1_ref[...].astype(BF16))) * _dot(x, w3_ref[...].astype(BF16))
        _store_pieces(y_ref, _pack_halves(_dot(hid.astype(BF16), w2_ref[...].astype(BF16))))


def _moe_ffn(xs, te, n_used, w1, w3, w2, j):
    npc, p, m = xs.shape
    d = 2 * npc * m
    dff = w1.shape[3]
    tm = MOE_ROWS

    def row(i, te_ref, nu_ref):
        return (0, jnp.minimum(i, nu_ref[0] - 1), 0)

    def expert(i, te_ref, nu_ref):
        return (j, te_ref[i], 0, 0)

    grid_spec = pltpu.PrefetchScalarGridSpec(
        num_scalar_prefetch=2, grid=(p // tm,),
        in_specs=[pl.BlockSpec((npc, tm, m), row),
                  _resident((None, None, d, dff), expert),
                  _resident((None, None, d, dff), expert),
                  _resident((None, None, dff, d), expert)],
        out_specs=pl.BlockSpec((npc, tm, m), row))
    return pl.pallas_call(
        _moe_ffn_kernel, grid_spec=grid_spec,
        out_shape=jax.ShapeDtypeStruct((npc, p, m), jnp.int32),
        compiler_params=_cparams("arbitrary"), name="moe_ffn",
    )(te, n_used, xs, w1, w3, w2)


def _moe_combine_kernel(ya_ref, yb_ref, route_ref, h_ref, gt_ref, o_ref):
    r = route_ref[...]
    y = r[:, 2:3] * _load_unpacked(ya_ref) + r[:, 3:4] * _load_unpacked(yb_ref)
    o_ref[...] = h_ref[...] + gt_ref[...] * y


def _moe_combine(yg, route, h, mods, n_batch, tiles_per_batch, ctx_tiles):
    d = h.shape[1]
    npc, _, r, m = yg.shape
    tm = TOKEN_TILE
    per = tiles_per_batch - ctx_tiles
    h_idx = lambda i: ((i // per) * tiles_per_batch + ctx_tiles + i % per, 0)
    if ctx_tiles == 0:
        mod_idx = _mod_row_spec(5, tiles_per_batch, n_batch)
    else:
        mod_idx = lambda i: (i // per, 5, 0, 0)
    return pl.pallas_call(
        _moe_combine_kernel, grid=(r // tm,),
        in_specs=[pl.BlockSpec((npc, None, tm, m), lambda i: (0, 0, i, 0)),
                  pl.BlockSpec((npc, None, tm, m), lambda i: (0, 1, i, 0)),
                  pl.BlockSpec((tm, LANES), lambda i: (i, 0)),
                  pl.BlockSpec((tm, d), h_idx),
                  pl.BlockSpec((None, None, 1, d), mod_idx)],
        out_specs=pl.BlockSpec((tm, d), lambda i: (i, 0)),
        out_shape=jax.ShapeDtypeStruct((r, d), F32),
        compiler_params=_cparams("parallel"), name="moe_combine",
    )(yg, yg, route, h, mods)


def _proj_moe_residual(o, w_o, h, g, w_router, w1, w3, w2, j, mods, n_batch, t, n_ctx, latent_only):
    tpb = t // TOKEN_TILE
    ctx_tiles = n_ctx // TOKEN_TILE if latent_only else 0
    h, u_pieces, route, route_t, counts = _proj_router(o, w_o, j, h, g, mods, n_batch, tpb, ctx_tiles, w_router)
    if latent_only:
        npc, _, m = u_pieces.shape
        u_pieces = u_pieces.reshape(npc, n_batch, t, m)[:, :, n_ctx:].reshape(npc, -1, m)
        route = route.reshape(n_batch, t, LANES)[:, n_ctx:].reshape(-1, LANES)
        route_t = route_t.reshape(-1, n_batch, t)[:, :, n_ctx:].reshape(route_t.shape[0], -1)
    n_exp = w1.shape[1]
    slot, n_slots, te, n_used = _route_tables(route_t, counts[0, :n_exp].astype(jnp.int32), MOE_ROWS, n_exp)
    xs = _scatter_pieces(u_pieces, slot, n_slots)
    ys = _moe_ffn(xs, te, n_used, w1, w3, w2, j)
    yg = _gather_pieces(ys, slot)
    if latent_only:
        return _moe_combine(yg, route, h, mods, n_batch, tpb, ctx_tiles)
    return yg, route, h, mods


def kernel(x, c, ctx, c_ctx, w_mod, b_mod, norm_mix_g, norm_ffn_g, ev_w_in, ev_w_out, hg_lb_logits, hg_onorm_g, rg_conv_w, rg_conv_b, rg_wa, rg_ba, rg_wx, rg_bx, rg_lambda, na_w_qkv, na_w_o, na_q_g, na_k_g, na_rpb, ffn_w1, ffn_w3, ffn_w2, moe_router, moe_w1, moe_w3, moe_w2):
    bsz, seq, d = x.shape
    n_ctx = ctx.shape[1]
    t = n_ctx + seq
    depth = w_mod.shape[0]
    assert n_ctx % TOKEN_TILE == 0 and seq % TOKEN_TILE == 0 and t % ROW_TILE == 0 and n_ctx <= ROW_TILE

    rows = -(-(bsz + 1) // 8) * 8
    cond = jnp.zeros((rows, d), F32).at[:bsz].set(c).at[bsz].set(c_ctx)
    mods_all = _adaln(cond, w_mod, b_mod).reshape(depth, rows, N_MOD, 1, d)

    lb_p = jax.nn.softmax(hg_lb_logits.astype(F32), axis=0)
    lb_all = jnp.cumsum(lb_p, axis=0) - lb_p[0]

    h = None
    pending = None
    for layer in range(depth):
        j = layer // 2
        mods = mods_all[layer]
        g_mix = norm_mix_g[layer].reshape(1, d)
        if layer % 2 == 0:
            if layer == 0:
                h, p = _concat_modulated_matmul(ctx, x, g_mix, mods, ev_w_in, j, F32, bsz)
            elif pending is None:
                p = _modulated_matmul(h, g_mix, mods, ev_w_in, j, F32, ROW_TILE // 2, bsz, t, n_ctx)
            else:
                h, p = _combined_modulated_matmul(*pending, g_mix, mods, ev_w_in, j, F32, ROW_TILE // 2,
                                                  bsz, t, n_ctx)
                pending = None
            p3 = p.reshape(bsz, t, -1)
            hg = _hgrn2(p3, lb_all[j], n_ctx)
            half = hg.shape[-1]
            rg = _rglru(p3, 5 * half // (d - half), rg_conv_w[j], rg_conv_b[j], rg_wa[j], rg_ba[j],
                        rg_wx[j], rg_bx[j], rg_lambda[j], n_ctx)
            h = _merge_swiglu(hg.reshape(bsz * t, half), rg.reshape(bsz * t, d - half), p, hg_onorm_g[j],
                              ev_w_out, h, norm_ffn_g[layer].reshape(1, d), ffn_w1, ffn_w3, ffn_w2, j,
                              mods, bsz, t, n_ctx)
        else:
            qkv = _modulated_matmul(h, g_mix, mods, na_w_qkv, j, BF16, ROW_TILE, bsz, t, n_ctx)
            o = _na_attention(qkv.reshape(bsz, t, 3 * d), na_q_g[j], na_k_g[j], na_rpb[j], n_ctx)
            res = _proj_moe_residual(o.reshape(bsz * t, d), na_w_o, h, norm_ffn_g[layer].reshape(1, d), moe_router[j],
                                     moe_w1, moe_w3, moe_w2, j, mods, bsz, t, n_ctx, layer == depth - 1)
            if layer == depth - 1:
                h = res
            else:
                pending = res
    if depth % 2 == 0:
        return h.reshape(bsz, seq, d)
    return h.reshape(bsz, t, d)[:, n_ctx:]
```

```python
import functools

import numpy as np
import jax
import jax.numpy as jnp
from jax import lax
from jax.experimental import pallas as pl
from jax.experimental.pallas import tpu as pltpu
from jax.experimental.pallas import tpu_sc as plsc

F32 = jnp.float32
BF16 = jnp.bfloat16

EPS = 1e-6
NEG_BIG = -1e9
LB_MIN = 1e-30
LOG2_E = 1.4426950408889634
N_MOD = 6
GRID_W = 64
HG_HEAD_DIM = 128
RG_CONV = 4
RG_C = 8.0
NA_HEADS = 16
NA_KR = 8
NA_KC = 16
N_EXPERTS = 8

LANES = 128
SC_WINDOW = 128
SC_PIECES = 2
TOKEN_TILE = 256
HG_CHUNK = 128
HG_LEVELS = 7
HG_SLOTS = 2
NA_SLOTS = 4
RG_SCAN_BLOCK = 256
RG_GROUP = 8
VMEM_LIMIT = 56 * 1024 * 1024


def _cparams(*sem):
    return pltpu.CompilerParams(dimension_semantics=sem, vmem_limit_bytes=VMEM_LIMIT)


def _resident(block_shape, index_map):
    return pl.BlockSpec(block_shape, index_map, pipeline_mode=pl.Buffered(1))


def _dot(a, b):
    return jnp.dot(a, b, preferred_element_type=F32)


def _dot_nt(a, b):
    return lax.dot_general(a, b, (((1,), (1,)), ((), ())), preferred_element_type=F32)


def _dot_tn(a, b):
    return lax.dot_general(a, b, (((0,), (0,)), ((), ())), preferred_element_type=F32)


def _split2(x):
    hi = x.astype(BF16)
    lo = (x - hi.astype(F32)).astype(BF16)
    return hi, lo


def _sigmoid(x):
    return 0.5 * jnp.tanh(0.5 * x) + 0.5


def _silu(x):
    return x * _sigmoid(x)


def _softplus_neg_abs(x):
    return jnp.log(1.0 + jnp.exp(-jnp.abs(x)))


def _log_sigmoid(x):
    return jnp.minimum(x, 0.0) - _softplus_neg_abs(x)


def _gelu_tanh(x):
    return 0.5 * x * (1.0 + jnp.tanh(np.sqrt(2.0 / np.pi) * (x + 0.044715 * (x * x * x))))


def _adaln_kernel(c_ref, w_ref, b_ref, o_ref):
    a_hi, a_lo = _split2(_silu(c_ref[...]))
    w_hi, w_lo = _split2(w_ref[...])
    o_ref[...] = _dot(a_hi, w_hi) + _dot(a_hi, w_lo) + _dot(a_lo, w_hi) + b_ref[...]


def _adaln(cond, w_mod, b_mod):
    depth, d, n = w_mod.shape
    r = cond.shape[0]
    tn = 1536
    return pl.pallas_call(
        _adaln_kernel,
        grid=(depth, n // tn),
        in_specs=[pl.BlockSpec((r, d), lambda l, j: (0, 0)),
                  pl.BlockSpec((None, d, tn), lambda l, j: (l, 0, j)),
                  pl.BlockSpec((None, 1, tn), lambda l, j: (l, 0, j))],
        out_specs=pl.BlockSpec((None, r, tn), lambda l, j: (l, 0, j)),
        out_shape=jax.ShapeDtypeStruct((depth, r, n), F32),
        compiler_params=_cparams("parallel", "parallel"),
        name="adaln",
    )(cond, w_mod, b_mod.reshape(depth, 1, n))


def _mod_row_spec(k, tiles_per_batch, n_batch):
    def idx(i):
        row = jnp.where(i % tiles_per_batch == 0, n_batch, i // tiles_per_batch)
        return (row, k, 0, 0)
    return idx


def _pack_halves(x):
    m = x.shape[1] // 2
    lo = lax.bitcast_convert_type(x[:, :m].astype(BF16).astype(F32), jnp.int32)
    hi = lax.bitcast_convert_type(x[:, m:].astype(BF16).astype(F32), jnp.int32)
    return lax.shift_right_logical(lo, 16) | (hi & jnp.int32(-65536))


def _unpack_halves(w):
    lo = lax.bitcast_convert_type(lax.shift_left(w, 16), F32)
    hi = lax.bitcast_convert_type(w & jnp.int32(-65536), F32)
    return lo, hi


def _store_pieces(ref, w):
    m = w.shape[1] // SC_PIECES
    for j in range(SC_PIECES):
        ref[j] = w[:, j * m:(j + 1) * m]


def _load_unpacked(ref):
    parts = [_unpack_halves(ref[j]) for j in range(SC_PIECES)]
    return jnp.concatenate([lo for lo, _ in parts] + [hi for _, hi in parts], axis=1)


def _proj_router_kernel(o_ref, wo_ref, h_ref, gt_ref, g_ref, sh_ref, sc_ref, wr_ref,
                        h1_ref, u_ref, route_ref, route_t_ref, cnt_ref, *, tiles_per_batch, skip_tiles):
    i = pl.program_id(0)

    @pl.when(i == 0)
    def _():
        cnt_ref[...] = jnp.zeros_like(cnt_ref)

    h1 = h_ref[...] + gt_ref[...] * _dot(o_ref[...], wo_ref[...].astype(BF16))
    h1_ref[...] = h1
    y = h1 * lax.rsqrt(jnp.mean(h1 * h1, axis=-1, keepdims=True) + EPS)
    u = (y * g_ref[...]) * (1.0 + sc_ref[...]) + sh_ref[...]
    _store_pieces(u_ref, _pack_halves(u))
    u0, u1 = _split2(u)
    w0, w1 = _split2(wr_ref[...])
    logits = _dot(u0, w0) + _dot(u0, w1) + _dot(u1, w0)
    lane = lax.broadcasted_iota(jnp.int32, logits.shape, 1).astype(F32)
    lg = jnp.where(lane < N_EXPERTS, logits, -jnp.inf)
    m1 = jnp.max(lg, axis=-1, keepdims=True)
    i1 = jnp.min(jnp.where(lg == m1, lane, float(LANES)), axis=-1, keepdims=True)
    lg2 = jnp.where(lane == i1, -jnp.inf, lg)
    m2 = jnp.max(lg2, axis=-1, keepdims=True)
    i2 = jnp.min(jnp.where(lg2 == m2, lane, float(LANES)), axis=-1, keepdims=True)
    e2 = jnp.exp(m2 - m1)
    den = 1.0 + e2

    routed = jnp.where(i % tiles_per_batch >= skip_tiles, 1.0, 0.0)
    pick_a = jnp.where(lane == i1, routed, 0.0)
    pick_b = jnp.where(lane == i2, routed, 0.0)
    tm = pick_a.shape[0]
    earlier = jnp.where(lax.broadcasted_iota(jnp.int32, (tm, tm), 1) < lax.broadcasted_iota(jnp.int32, (tm, tm), 0),
                        1.0, 0.0).astype(BF16)
    total_a = jnp.sum(pick_a, axis=0, keepdims=True)
    base = cnt_ref[0:1, :]
    rank_a = jnp.sum(pick_a * (base + _dot(earlier, pick_a.astype(BF16))), axis=-1, keepdims=True)
    rank_b = jnp.sum(pick_b * (base + total_a + _dot(earlier, pick_b.astype(BF16))), axis=-1, keepdims=True)
    cnt_ref[0:1, :] = base + total_a + jnp.sum(pick_b, axis=0, keepdims=True)

    route = jnp.where(lane == 0, i1, jnp.where(lane == 1, i2, jnp.where(lane == 2, 1.0 / den, jnp.where(
        lane == 3, e2 / den, jnp.where(lane == 4, rank_a, rank_b)))))
    route_ref[...] = route
    route_t_ref[...] = route.T[0:route_t_ref.shape[0], :]


def _proj_router(o, w_o, j, h, g, mods, n_batch, tiles_per_batch, skip_tiles, w_router):
    n, d = h.shape
    tm = TOKEN_TILE
    row = pl.BlockSpec((tm, d), lambda i: (i, 0))
    vec = pl.BlockSpec((1, d), lambda i: (0, 0))

    def mod_spec(k):
        return pl.BlockSpec((None, None, 1, d), _mod_row_spec(k, tiles_per_batch, n_batch))

    wr = jnp.zeros((d, LANES), F32).at[:, :N_EXPERTS].set(w_router)
    kern = functools.partial(_proj_router_kernel, tiles_per_batch=tiles_per_batch, skip_tiles=skip_tiles)
    return pl.pallas_call(
        kern, grid=(n // tm,),
        in_specs=[row, _resident((None, d, d), lambda i: (j, 0, 0)), row, mod_spec(2),
                  vec, mod_spec(3), mod_spec(4), pl.BlockSpec((d, LANES), lambda i: (0, 0))],
        out_specs=[row,
                   pl.BlockSpec((SC_PIECES, tm, d // 2 // SC_PIECES), lambda i: (0, i, 0)),
                   pl.BlockSpec((tm, LANES), lambda i: (i, 0)),
                   pl.BlockSpec((8, tm), lambda i: (0, i)),
                   pl.BlockSpec((8, LANES), lambda i: (0, 0))],
        out_shape=[jax.ShapeDtypeStruct((n, d), F32),
                   jax.ShapeDtypeStruct((SC_PIECES, n, d // 2 // SC_PIECES), jnp.int32),
                   jax.ShapeDtypeStruct((n, LANES), F32),
                   jax.ShapeDtypeStruct((8, n), F32),
                   jax.ShapeDtypeStruct((8, LANES), F32)],
        compiler_params=_cparams("arbitrary"), name="proj_router",
    )(o, w_o, h, mods, g, mods, mods, wr)


ROW_TILE = 768


def _row_gate(ctx_ref, own_ref, shape, tile_in_batch, n_ctx):
    rows = lax.broadcasted_iota(jnp.int32, shape, 0)
    is_ctx = (rows < n_ctx) & (tile_in_batch == 0)
    return jnp.where(is_ctx, ctx_ref[...], own_ref[...])


def _modulated_rows(x, g_ref, shc_ref, sh_ref, scc_ref, sc_ref, tile_in_batch, n_ctx):
    y = x * lax.rsqrt(jnp.mean(x * x, axis=-1, keepdims=True) + EPS)
    shift = _row_gate(shc_ref, sh_ref, x.shape, tile_in_batch, n_ctx)
    scale = _row_gate(scc_ref, sc_ref, x.shape, tile_in_batch, n_ctx)
    return (y * g_ref[...]) * (1.0 + scale) + shift


def _stream_specs(d, n_batch, tiles_per_batch, k, nargs):
    if nargs == 1:
        ctx = lambda i: (n_batch, k, 0, 0)
        own = lambda i: (i // tiles_per_batch, k, 0, 0)
    else:
        ctx = lambda i, f: (n_batch, k, 0, 0)
        own = lambda i, f: (i // tiles_per_batch, k, 0, 0)
    return [pl.BlockSpec((None, None, 1, d), ctx), pl.BlockSpec((None, None, 1, d), own)]


def _mod_mm_kernel(h_ref, g_ref, shc_ref, sh_ref, scc_ref, sc_ref, w_ref, o_ref, *, n_ctx, tiles_per_batch):
    x = _modulated_rows(h_ref[...], g_ref, shc_ref, sh_ref, scc_ref, sc_ref,
                        pl.program_id(0) % tiles_per_batch, n_ctx).astype(BF16)
    o_ref[...] = _dot(x, w_ref[...].astype(BF16)).astype(o_ref.dtype)


def _concat_mod_mm_kernel(ctx_ref, x_ref, g_ref, sh_ref, sc_ref, w_ref, h_ref, o_ref, *, tiles_per_batch):
    h = jnp.where(pl.program_id(0) % tiles_per_batch == 0, ctx_ref[...], x_ref[...])
    h_ref[...] = h
    y = h * lax.rsqrt(jnp.mean(h * h, axis=-1, keepdims=True) + EPS)
    u = ((y * g_ref[...]) * (1.0 + sc_ref[...]) + sh_ref[...]).astype(BF16)
    o_ref[...] = _dot(u, w_ref[...].astype(BF16)).astype(o_ref.dtype)


def _concat_modulated_matmul(ctx, x, g, mods, w, j, out_dtype, n_batch):
    _, n_ctx, d = ctx.shape
    seq = x.shape[1]
    tm = TOKEN_TILE
    assert n_ctx == tm and seq % tm == 0
    lat = seq // tm
    tpb = lat + 1
    n = n_batch * (n_ctx + seq)
    nout = w.shape[2]

    def mod_spec(k):
        return pl.BlockSpec((None, None, 1, d), _mod_row_spec(k, tpb, n_batch))

    kern = functools.partial(_concat_mod_mm_kernel, tiles_per_batch=tpb)
    return pl.pallas_call(
        kern, grid=(n // tm,),
        in_specs=[pl.BlockSpec((tm, d), lambda i: (i // tpb, 0)),
                  pl.BlockSpec((tm, d), lambda i: ((i // tpb) * lat + jnp.maximum(i % tpb, 1) - 1, 0)),
                  pl.BlockSpec((1, d), lambda i: (0, 0)), mod_spec(0), mod_spec(1),
                  _resident((None, d, nout), lambda i: (j, 0, 0))],
        out_specs=[pl.BlockSpec((tm, d), lambda i: (i, 0)), pl.BlockSpec((tm, nout), lambda i: (i, 0))],
        out_shape=[jax.ShapeDtypeStruct((n, d), F32), jax.ShapeDtypeStruct((n, nout), out_dtype)],
        compiler_params=_cparams("parallel"), name="concat_modulated_matmul",
    )(ctx.reshape(n_batch * n_ctx, d), x.reshape(n_batch * seq, d), g, mods, mods, w)


def _combine_mod_mm_kernel(ya_ref, yb_ref, route_ref, h_ref, g2c_ref, g2_ref, g_ref, shc_ref, sh_ref,
                           scc_ref, sc_ref, w_ref, h1_ref, o_ref, *, n_ctx, tiles_per_batch):
    tile_in_batch = pl.program_id(0) % tiles_per_batch
    r = route_ref[...]
    y = r[:, 2:3] * _load_unpacked(ya_ref) + r[:, 3:4] * _load_unpacked(yb_ref)
    h1 = h_ref[...] + _row_gate(g2c_ref, g2_ref, y.shape, tile_in_batch, n_ctx) * y
    h1_ref[...] = h1
    x = _modulated_rows(h1, g_ref, shc_ref, sh_ref, scc_ref, sc_ref, tile_in_batch, n_ctx).astype(BF16)
    o_ref[...] = _dot(x, w_ref[...].astype(BF16)).astype(o_ref.dtype)


def _combined_modulated_matmul(yg, route, h, mods_prev, g, mods, w, j, out_dtype, tm, n_batch, t, n_ctx):
    n, d = h.shape
    nout = w.shape[2]
    npc, _, _, m = yg.shape
    tpb = t // tm
    assert t % tm == 0 and n_ctx <= tm
    kern = functools.partial(_combine_mod_mm_kernel, n_ctx=n_ctx, tiles_per_batch=tpb)
    return pl.pallas_call(
        kern, grid=(n // tm,),
        in_specs=[pl.BlockSpec((npc, None, tm, m), lambda i: (0, 0, i, 0)),
                  pl.BlockSpec((npc, None, tm, m), lambda i: (0, 1, i, 0)),
                  pl.BlockSpec((tm, LANES), lambda i: (i, 0)),
                  pl.BlockSpec((tm, d), lambda i: (i, 0))] + _stream_specs(d, n_batch, tpb, 5, 1)
        + [pl.BlockSpec((1, d), lambda i: (0, 0))]
        + _stream_specs(d, n_batch, tpb, 0, 1) + _stream_specs(d, n_batch, tpb, 1, 1)
        + [_resident((None, d, nout), lambda i: (j, 0, 0))],
        out_specs=[pl.BlockSpec((tm, d), lambda i: (i, 0)), pl.BlockSpec((tm, nout), lambda i: (i, 0))],
        out_shape=[jax.ShapeDtypeStruct((n, d), F32), jax.ShapeDtypeStruct((n, nout), out_dtype)],
        compiler_params=_cparams("parallel"), name="combined_modulated_matmul",
    )(yg, yg, route, h, mods_prev, mods_prev, g, mods, mods, mods, mods, w)


def _modulated_matmul(h, g, mods, w, j, out_dtype, tm, n_batch, t, n_ctx):
    n, d = h.shape
    nout = w.shape[2]
    tpb = t // tm
    assert t % tm == 0 and n_ctx <= tm
    kern = functools.partial(_mod_mm_kernel, n_ctx=n_ctx, tiles_per_batch=tpb)
    return pl.pallas_call(
        kern, grid=(n // tm,),
        in_specs=[pl.BlockSpec((tm, d), lambda i: (i, 0)),
                  pl.BlockSpec((1, d), lambda i: (0, 0))]
        + _stream_specs(d, n_batch, tpb, 0, 1) + _stream_specs(d, n_batch, tpb, 1, 1)
        + [_resident((None, d, nout), lambda i: (j, 0, 0))],
        out_specs=pl.BlockSpec((tm, nout), lambda i: (i, 0)),
        out_shape=jax.ShapeDtypeStruct((n, nout), out_dtype),
        compiler_params=_cparams("parallel"), name="modulated_matmul",
    )(h, g, mods, mods, mods, mods, w)


def _hg_tables():
    c = HG_CHUNK
    t = np.arange(c)[:, None]
    u = np.arange(c)[None, :]
    mats = [(u <= t), (u > t)]
    for k in range(1, HG_LEVELS + 1):
        m = ((t >> k) << k) + (1 << (k - 1)) - 1
        upper = t > m
        mats.append(np.where(upper, (u > m) & (u <= t), (u > t) & (u <= m)))
    fwd = np.concatenate(mats, axis=0).astype(np.float32)
    bwd = fwd.reshape(-1, c, c)[:, ::-1, ::-1].reshape(-1, c)
    x = t ^ u
    lvl = np.where(t > u, np.floor(np.log2(np.maximum(x, 1))).astype(np.int32) + 1, -1)
    lvl = np.where(t == u, 0, lvl).astype(np.int32)
    return np.stack([fwd, bwd]), np.stack([lvl, lvl.T])


def _hg_kernel(q_ref, ff_ref, fb_ref, v_ref, lb_ref, m_ref, lvl_ref, o_ref, of_ref, ob_ref,
               qk_ref, el_ref, io_ref, tot_ref, att_ref, *, n_ctx_chunks, n_chunks):
    c = HG_CHUNK
    gap = HG_SLOTS // 2
    t = o_ref.shape[0]
    z_refs = (ff_ref, fb_ref)
    out_refs = (of_ref, ob_ref)
    for ref in (qk_ref, el_ref, io_ref, tot_ref, att_ref):
        ref[...] = jnp.zeros(ref.shape, ref.dtype)

    def start_row(d, j):
        j = jnp.clip(j, 0, n_chunks - 1)
        if d == 1:
            j = jnp.where(j < n_ctx_chunks, n_ctx_chunks - 1 - j, n_chunks - 1 - (j - n_ctx_chunks))
        return pl.multiple_of(j * c, c)

    def stage1(d, j, slot):
        t0 = start_row(d, j)
        lb = lb_ref[d:d + 1, :]
        log_lb = jnp.log(jnp.maximum(lb, LB_MIN))
        q = _silu(q_ref[pl.ds(t0, c), :])
        z = z_refs[d][pl.ds(t0, c), :]
        k = (1.0 - lb) * _sigmoid(-z)
        b = jnp.log1p(-lb) + _log_sigmoid(z)
        logf = jnp.maximum(log_lb, b) + _softplus_neg_abs(log_lb - b)
        g_hi, g_lo = _split2(logf * LOG2_E)
        m_ends = m_ref[d, 0:2 * c, :]
        e_ends = jnp.exp2(_dot(m_ends, g_hi) + _dot(m_ends, g_lo))
        el_ref[d, slot] = jnp.exp2(_dot(m_ref[d, 2 * c:, :], g_hi).astype(BF16))
        qk_ref[d, slot, 0] = q.astype(BF16)
        qk_ref[d, slot, 1] = k.astype(BF16)
        io_ref[d, slot, 0] = (q * e_ends[0:c]).astype(BF16)
        io_ref[d, slot, 1] = (k * e_ends[c:]).astype(BF16)
        tot_row = c - 1 if d == 0 else 0
        tot_ref[d, slot, 0:1, :] = e_ends[tot_row:tot_row + 1, :]

    def stage2(d, slot):
        qb = qk_ref[d, slot, 0]
        kb = qk_ref[d, slot, 1]
        lvl = lvl_ref[d]
        att = jnp.where(lvl == 0, _dot_nt(qb, kb), 0.0)
        for lev in range(1, HG_LEVELS + 1):
            el = el_ref[d, slot, (lev - 1) * c:lev * c, :]
            att = jnp.where(lvl == lev, _dot_nt(qb * el, kb * el), att)
        att_ref[d, slot] = att.astype(BF16)

    def stage3(d, j, slot, st):
        t0 = start_row(d, j)
        v = v_ref[pl.ds(t0, c), :].astype(BF16)
        row = jnp.where(j < n_chunks, t0, t)
        out_refs[d][pl.ds(pl.multiple_of(row, c), c), :] = (
            _dot(att_ref[d, slot], v) + _dot_nt(io_ref[d, slot, 0], st.astype(BF16)))
        return st * tot_ref[d, slot, 0:1, :] + _dot_tn(v, io_ref[d, slot, 1])

    def steps(jj, carry):
        st = list(carry)
        for slot in range(HG_SLOTS):
            i = HG_SLOTS * jj + slot
            for d in range(2):
                st[d] = stage3(d, i - 2 * gap, slot, st[d])
            for d in range(2):
                stage2(d, (slot + gap) % HG_SLOTS)
            for d in range(2):
                stage1(d, i, slot)
        return tuple(st)

    zero = jnp.zeros((c, c), F32)
    n_steps = -(-(n_chunks + 2 * gap) // HG_SLOTS)
    lax.fori_loop(0, n_steps, steps, (zero, zero))
    o_ref[...] = (of_ref[0:t, :] + ob_ref[0:t, :]).astype(o_ref.dtype)


def _hgrn2(p3, lb, n_ctx):
    bsz, t, _ = p3.shape
    width = lb.shape[1]
    heads = width // HG_HEAD_DIM
    mats, lvl = _hg_tables()
    c = HG_CHUNK

    def col(group):
        return pl.BlockSpec((None, t, c), lambda b, h: (b, 0, group * heads + h))

    kern = functools.partial(_hg_kernel, n_ctx_chunks=n_ctx // c, n_chunks=t // c)
    return pl.pallas_call(
        kern, grid=(bsz, heads),
        in_specs=[col(0), col(1), col(2), col(3),
                  pl.BlockSpec((2, c), lambda b, h: (0, h)),
                  pl.BlockSpec(mats.shape, lambda b, h: (0, 0, 0)),
                  pl.BlockSpec(lvl.shape, lambda b, h: (0, 0, 0))],
        out_specs=pl.BlockSpec((None, t, c), lambda b, h: (b, 0, h)),
        out_shape=jax.ShapeDtypeStruct((bsz, t, width), F32),
        scratch_shapes=[pltpu.VMEM((t + c, c), F32), pltpu.VMEM((t + c, c), F32),
                        pltpu.VMEM((2, HG_SLOTS, 2, c, c), BF16), pltpu.VMEM((2, HG_SLOTS, HG_LEVELS * c, c), BF16),
                        pltpu.VMEM((2, HG_SLOTS, 2, c, c), BF16), pltpu.VMEM((2, HG_SLOTS, 8, c), F32),
                        pltpu.VMEM((2, HG_SLOTS, c, c), BF16)],
        compiler_params=_cparams("parallel", "parallel"), name="hgrn2",
    )(p3, p3, p3, p3, lb, jnp.asarray(mats, BF16), jnp.asarray(lvl))


PAD = 8


def _rg_kernel(x_ref, cw_ref, cb_ref, wa_ref, ba_ref, wx_ref, bx_ref, lam_ref, o_ref,
               xpad_ref, xc_ref, ob_ref, *, n_ctx):
    t, w = x_ref.shape
    n_lat = t - n_ctx
    lat0 = n_ctx + 2 * PAD
    xpad_ref[...] = jnp.zeros(xpad_ref.shape, F32)
    xpad_ref[PAD:PAD + n_ctx, :] = x_ref[0:n_ctx, :]
    xpad_ref[lat0:lat0 + n_lat, :] = x_ref[n_ctx:t, :]
    left = RG_CONV // 2
    for base, dst, n in ((PAD, 0, n_ctx), (lat0, n_ctx, n_lat)):
        acc = cb_ref[...] + cw_ref[0:1, :] * xpad_ref[base - left:base - left + n, :]
        for j in range(1, RG_CONV):
            acc = acc + cw_ref[j:j + 1, :] * xpad_ref[base - left + j:base - left + j + n, :]
        xc_ref[dst:dst + n, :] = acc

    blk = RG_SCAN_BLOCK
    n_blocks = t // blk
    n_ctx_blocks = n_ctx // blk
    sub = lax.broadcasted_iota(jnp.int32, (blk // RG_GROUP, RG_GROUP, w), 1)
    c_log_sig_lam = RG_C * _log_sigmoid(lam_ref[...])

    def scan_block(d, t0, hin, out_ref):
        xc = xc_ref[pl.ds(t0, blk), :]
        xb = xc.astype(BF16)
        r = _sigmoid(_dot(xb, wa_ref[d]) + ba_ref[d:d + 1, :])
        ig = _sigmoid(_dot(xb, wx_ref[d]) + bx_ref[d:d + 1, :])
        log_a = r * c_log_sig_lam[d:d + 1, :]
        a = jnp.exp(log_a)
        b = jnp.sqrt(jnp.maximum(1.0 - a * a, 0.0)) * (ig * xc)
        n_groups = blk // RG_GROUP
        a = a.reshape(n_groups, RG_GROUP, w)
        b = b.reshape(n_groups, RG_GROUP, w)
        s = 1
        while s < RG_GROUP:
            shift = s if d == 0 else RG_GROUP - s
            keep = (sub >= s) if d == 0 else (sub < RG_GROUP - s)
            b = jnp.where(keep, a * pltpu.roll(b, shift, 1), 0.0) + b
            a = jnp.where(keep, a * pltpu.roll(a, shift, 1), a)
            s *= 2
        for gi in range(n_groups):
            gj = gi if d == 0 else n_groups - 1 - gi
            hg = a[gj] * hin + b[gj]
            out_ref[pl.ds(t0 + gj * RG_GROUP, RG_GROUP), :] = hg
            hin = hg[RG_GROUP - 1:RG_GROUP, :] if d == 0 else hg[0:1, :]
        return hin

    def step(i, carry):
        h_f, h_b = carry
        bb = jnp.where(i < n_ctx_blocks, n_ctx_blocks - 1 - i, n_blocks - 1 - (i - n_ctx_blocks))
        h_f = scan_block(0, pl.multiple_of(i * blk, blk), h_f, o_ref)
        h_b = scan_block(1, pl.multiple_of(bb * blk, blk), h_b, ob_ref)
        return h_f, h_b

    zero = jnp.zeros((1, w), F32)
    lax.fori_loop(0, n_blocks, step, (zero, zero))
    o_ref[...] += ob_ref[...]


def _blockdiag_dense(w):
    nd, nb, k, _ = w.shape
    eye = jnp.eye(nb, dtype=w.dtype)
    return jnp.einsum('dnij,nm->dnimj', w, eye).reshape(nd, nb * k, nb * k)


def _rglru(p3, col_block, conv_w, conv_b, wa, ba, wx, bx, lam, n_ctx):
    bsz, t, _ = p3.shape
    w = conv_w.shape[1]
    full = lambda shape: pl.BlockSpec(shape, lambda b: (0,) * len(shape))
    kern = functools.partial(_rg_kernel, n_ctx=n_ctx)
    return pl.pallas_call(
        kern, grid=(bsz,),
        in_specs=[pl.BlockSpec((None, t, w), lambda b: (b, 0, col_block)),
                  full((RG_CONV, w)), full((1, w)),
                  full((2, w, w)), full((2, w)), full((2, w, w)), full((2, w)), full((2, w))],
        out_specs=pl.BlockSpec((None, t, w), lambda b: (b, 0, 0)),
        out_shape=jax.ShapeDtypeStruct((bsz, t, w), F32),
        scratch_shapes=[pltpu.VMEM((t + 3 * PAD, w), F32)] + [pltpu.VMEM((t, w), F32)] * 2,
        compiler_params=_cparams("parallel"), name="rglru",
    )(p3, conv_w, conv_b.reshape(1, w), _blockdiag_dense(wa).astype(BF16), ba,
      _blockdiag_dense(wx).astype(BF16), bx, lam)


def _na_bias_table(rpb):
    q = np.arange(GRID_W)[:, None]
    kc = np.arange(GRID_W)[None, :]
    wstart = np.clip(q - NA_KC // 2, 0, GRID_W - NA_KC)
    mask = (kc >= wstart) & (kc < wstart + NA_KC)
    coff = np.clip(kc - q + NA_KC - 1, 0, 2 * NA_KC - 2)
    sel = (coff[None] == np.arange(2 * NA_KC - 1)[:, None, None]).astype(np.float32)
    toe = jnp.einsum('hdj,jqk->hqdk', rpb.astype(F32), sel, precision=lax.Precision.HIGHEST)
    toe = jnp.where(mask[None, :, None, :], toe, NEG_BIG)
    h = toe.shape[0]
    flat = toe.reshape(h // 2, 2 * GRID_W, -1)
    width = (2 * NA_KR) * GRID_W
    even = jnp.pad(flat, ((0, 0), (0, 0), (0, width - flat.shape[2])))
    odd = jnp.pad(flat[:, :, GRID_W:], ((0, 0), (0, 0), (0, width - flat.shape[2] + GRID_W)))
    return jnp.stack([even, odd], axis=1)


def _stack_heads(x, hd):
    lane = lax.broadcasted_iota(jnp.int32, x.shape, 1)
    zero = jnp.zeros_like(x)
    return jnp.concatenate([jnp.where(lane < hd, x, zero), jnp.where(lane >= hd, x, zero)], axis=0)


def _unstack_heads(o2, hd):
    n = o2.shape[0] // 2
    lane = lax.broadcasted_iota(jnp.int32, (n, o2.shape[1]), 1)
    return jnp.where(lane < hd, o2[:n], o2[n:])


def _na_kernel(q_ref, k_ref, v_ref, qg_ref, kg_ref, bias_ref, o_ref, qn_ref, kn_ref,
               s_ref, p_ref, den_ref, *, n_ctx, scale):
    t, w = q_ref.shape
    hd = w // 2
    rows_grid = (t - n_ctx) // GRID_W
    r_i = lax.broadcasted_iota(jnp.int32, (w, w), 0)
    c_i = lax.broadcasted_iota(jnp.int32, (w, w), 1)
    seg = jnp.where((r_i // hd) == (c_i // hd), 1.0 / hd, 0.0).astype(BF16)

    def normed(x, g):
        ms = _dot((x * x).astype(BF16), seg)
        return (x * lax.rsqrt(ms + EPS) * g).astype(BF16)

    qn_ref[...] = normed(q_ref[...].astype(F32), qg_ref[...] * scale)
    kn_ref[...] = normed(k_ref[...].astype(F32), kg_ref[...])
    vb_ref = v_ref

    kc = kn_ref[0:n_ctx, :]
    vc = vb_ref[0:n_ctx, :]

    s = _dot_nt(_stack_heads(qn_ref[0:n_ctx, :], hd), kc)
    p = jnp.exp(s - jnp.max(s, axis=-1, keepdims=True))
    o2 = _dot(p.astype(BF16), vc) / jnp.sum(p, axis=-1, keepdims=True)
    o_ref[0:n_ctx, :] = _unstack_heads(o2, hd).astype(o_ref.dtype)

    kr = NA_KR
    n_win = kr * GRID_W

    def window(r):
        r = jnp.clip(r, 0, rows_grid - 1)
        rs = jnp.clip(r - kr // 2, 0, rows_grid - kr)
        q0 = pl.multiple_of(n_ctx + r * GRID_W, GRID_W)
        k0 = pl.multiple_of(n_ctx + rs * GRID_W, GRID_W)
        return q0, k0, rs - r + (kr - 1)

    gap = NA_SLOTS // 2
    for slot in range(gap):
        s_ref[gap + slot] = jnp.zeros(s_ref.shape[1:], F32)
        p_ref[slot] = jnp.zeros(p_ref.shape[1:], BF16)
        den_ref[slot] = jnp.ones(den_ref.shape[1:], F32)

    def step(i, slot):
        q0, k0, _ = window(i - 2 * gap)
        pr = p_ref[slot]
        o2 = (_dot(pr[:, :n_win], vb_ref[pl.ds(k0, n_win), :]) + _dot(pr[:, n_win:], vc)) / den_ref[slot]
        o_ref[pl.ds(q0, GRID_W), :] = _unstack_heads(o2, hd).astype(o_ref.dtype)

        mid = (slot + gap) % NA_SLOTS
        sc = s_ref[mid]
        pe = jnp.exp(sc - jnp.max(sc, axis=-1, keepdims=True))
        den_ref[mid] = jnp.sum(pe, axis=-1, keepdims=True)
        p_ref[mid] = pe.astype(BF16)

        q0, k0, off = window(i)
        q2 = _stack_heads(qn_ref[pl.ds(q0, GRID_W), :], hd)
        bias = bias_ref[off % 2, :, pl.ds(pl.multiple_of((off // 2) * (2 * GRID_W), 2 * GRID_W), n_win)]
        s_ref[slot, :, :n_win] = _dot_nt(q2, kn_ref[pl.ds(k0, n_win), :]) + bias
        s_ref[slot, :, n_win:] = _dot_nt(q2, kc)

    def steps(j, carry):
        for slot in range(NA_SLOTS):
            step(NA_SLOTS * j + slot, slot)
        return carry

    lax.fori_loop(0, (rows_grid + 2 * gap) // NA_SLOTS, steps, 0)


def _na_attention(qkv3, q_g, k_g, rpb, n_ctx):
    bsz, t, d3 = qkv3.shape
    d = d3 // 3
    hd = d // NA_HEADS
    w = 2 * hd
    pairs = d // w
    rows_grid = (t - n_ctx) // GRID_W
    assert w == LANES and rows_grid >= NA_KR and rows_grid % NA_SLOTS == 0
    n_keys = NA_KR * GRID_W + n_ctx
    bias = _na_bias_table(rpb)
    kern = functools.partial(_na_kernel, n_ctx=n_ctx, scale=hd ** -0.5)

    def col(group):
        return pl.BlockSpec((None, t, w), lambda hp, b: (b, 0, group * pairs + hp))

    gain = lambda g: jnp.tile(g.astype(F32), 2).reshape(1, w)
    return pl.pallas_call(
        kern, grid=(pairs, bsz),
        in_specs=[col(0), col(1), col(2),
                  pl.BlockSpec((1, w), lambda hp, b: (0, 0)),
                  pl.BlockSpec((1, w), lambda hp, b: (0, 0)),
                  pl.BlockSpec((None,) + bias.shape[1:], lambda hp, b: (hp, 0, 0, 0))],
        out_specs=pl.BlockSpec((None, t, w), lambda hp, b: (b, 0, hp)),
        out_shape=jax.ShapeDtypeStruct((bsz, t, d), BF16),
        scratch_shapes=[pltpu.VMEM((t, w), BF16)] * 2
        + [pltpu.VMEM((NA_SLOTS, 2 * GRID_W, n_keys), F32), pltpu.VMEM((NA_SLOTS, 2 * GRID_W, n_keys), BF16),
           pltpu.VMEM((NA_SLOTS, 2 * GRID_W, 1), F32)],
        compiler_params=_cparams("parallel", "parallel"), name="na_attention",
    )(qkv3, qkv3, qkv3, gain(q_g), gain(k_g), bias)


FFN_ROWS = 384


def _merge_swiglu_kernel(hg_ref, rg_ref, gate_ref, y_ref, og_ref, wo_ref, h_ref, g1c_ref, g1_ref,
                         g_ref, shc_ref, sh_ref, scc_ref, sc_ref, w1_ref, w3_ref, w2_ref, g2c_ref, g2_ref,
                         o_ref, *, n_ctx, tiles_per_batch):
    tile_in_batch = pl.program_id(0) % tiles_per_batch
    hg = hg_ref[...]
    half = hg.shape[1]
    hn = hg * lax.rsqrt(jnp.mean(hg * hg, axis=-1, keepdims=True) + EPS) * og_ref[...]
    a = (hn * _silu(gate_ref[...])).astype(BF16)
    b = (rg_ref[...] * _gelu_tanh(y_ref[...])).astype(BF16)
    mix = _dot(a, wo_ref[0:half, :].astype(BF16)) + _dot(b, wo_ref[half:, :].astype(BF16))
    h1 = h_ref[...] + _row_gate(g1c_ref, g1_ref, mix.shape, tile_in_batch, n_ctx) * mix
    x = _modulated_rows(h1, g_ref, shc_ref, sh_ref, scc_ref, sc_ref, tile_in_batch, n_ctx).astype(BF16)
    hid = _silu(_dot(x, w1_ref[...].astype(BF16))) * _dot(x, w3_ref[...].astype(BF16))
    y = _dot(hid.astype(BF16), w2_ref[...].astype(BF16))
    o_ref[...] = h1 + _row_gate(g2c_ref, g2_ref, y.shape, tile_in_batch, n_ctx) * y


def _merge_swiglu(hg, rg, p, onorm_g, w_out, h, g, w1, w3, w2, j, mods, n_batch, t, n_ctx):
    n, d = h.shape
    half = hg.shape[1]
    dff = w1.shape[2]
    tm = FFN_ROWS
    tpb = t // tm
    assert t % tm == 0 and n_ctx <= tm
    blk = lambda c: pl.BlockSpec((tm, half), lambda i: (i, c))
    vec = lambda width: pl.BlockSpec((1, width), lambda i: (0, 0))
    kern = functools.partial(_merge_swiglu_kernel, n_ctx=n_ctx, tiles_per_batch=tpb)
    return pl.pallas_call(
        kern, grid=(n // tm,),
        in_specs=[blk(0), blk(0), blk(4), blk(6), vec(half),
                  _resident((None, d, d), lambda i: (j, 0, 0)),
                  pl.BlockSpec((tm, d), lambda i: (i, 0))] + _stream_specs(d, n_batch, tpb, 2, 1)
        + [vec(d)] + _stream_specs(d, n_batch, tpb, 3, 1) + _stream_specs(d, n_batch, tpb, 4, 1)
        + [_resident((None, d, dff), lambda i: (j, 0, 0)),
           _resident((None, d, dff), lambda i: (j, 0, 0)),
           _resident((None, dff, d), lambda i: (j, 0, 0))] + _stream_specs(d, n_batch, tpb, 5, 1),
        out_specs=pl.BlockSpec((tm, d), lambda i: (i, 0)),
        out_shape=jax.ShapeDtypeStruct((n, d), F32),
        compiler_params=_cparams("parallel"), name="merge_swiglu",
    )(hg, rg, p, p, onorm_g.reshape(1, half), w_out, h, mods, mods,
      g, mods, mods, mods, mods, w1, w3, w2, mods, mods)


MOE_ROWS = 512


def _route_tables(route_t, counts, tm, n_exp):
    r = route_t.shape[1]
    experts = route_t[0:2].astype(jnp.int32)
    ranks = route_t[4:6].astype(jnp.int32)
    padded = ((counts + tm - 1) // tm) * tm
    ends = jnp.cumsum(padded)
    starts = ends - padded
    slot = ranks
    for e in range(n_exp):
        slot = slot + jnp.where(experts == e, starts[e], 0)
    p_slots = 2 * r + n_exp * tm
    n_tiles = p_slots // tm
    n_used = ends[-1] // tm
    tile = jnp.arange(n_tiles, dtype=jnp.int32)
    te = jnp.sum(((tile * tm)[:, None] >= ends[None, :]).astype(jnp.int32), axis=1)
    te = jnp.where(tile < n_used, te, te[n_used - 1])
    return slot, p_slots, te, n_used.reshape(1)


def _gather_rows(table, idx):
    p = idx.shape[0]
    m = table.shape[1]
    mesh = plsc.VectorSubcoreMesh(core_axis_name="c", subcore_axis_name="s")

    @functools.partial(pl.kernel, out_type=jax.ShapeDtypeStruct((p, m), table.dtype), mesh=mesh)
    def gather(table_hbm, idx_hbm, out_hbm):
        def body(idx_vmem, out_vmem):
            pltpu.sync_copy(table_hbm.at[idx_vmem.at[0]], out_vmem)

        pltpu.emit_pipeline(
            body, grid=(p // SC_WINDOW,),
            in_specs=[pl.BlockSpec((1, SC_WINDOW), lambda i: (0, i))],
            out_specs=[pl.BlockSpec((SC_WINDOW, m), lambda i: (i, 0))],
            core_axis_name=("c", "s"), dimension_semantics=(pltpu.PARALLEL,),
        )(idx_hbm, out_hbm)

    return gather(table, idx.reshape(1, p))


def _scatter_rows(rows, idx_a, idx_b, n_out):
    q, m = rows.shape
    mesh = plsc.VectorSubcoreMesh(core_axis_name="c", subcore_axis_name="s")

    @functools.partial(pl.kernel, out_type=jax.ShapeDtypeStruct((n_out, m), rows.dtype), mesh=mesh)
    def scatter(rows_hbm, ia_hbm, ib_hbm, out_hbm):
        def body(rows_vmem, ia_vmem, ib_vmem):
            pltpu.sync_copy(rows_vmem, out_hbm.at[ia_vmem.at[0]])
            pltpu.sync_copy(rows_vmem, out_hbm.at[ib_vmem.at[0]])

        pltpu.emit_pipeline(
            body, grid=(q // SC_WINDOW,),
            in_specs=[pl.BlockSpec((SC_WINDOW, m), lambda i: (i, 0)),
                      pl.BlockSpec((1, SC_WINDOW), lambda i: (0, i)),
                      pl.BlockSpec((1, SC_WINDOW), lambda i: (0, i))],
            out_specs=[],
            core_axis_name=("c", "s"), dimension_semantics=(pltpu.PARALLEL,),
        )(rows_hbm, ia_hbm, ib_hbm)

    return scatter(rows, idx_a.reshape(1, q), idx_b.reshape(1, q))


def _scatter_pieces(pieces, slot, n_slots):
    npc, r, m = pieces.shape
    offs = jnp.arange(npc, dtype=jnp.int32)[:, None] * n_slots
    idx_a = (slot[0][None, :] + offs).reshape(-1)
    idx_b = (slot[1][None, :] + offs).reshape(-1)
    return _scatter_rows(pieces.reshape(npc * r, m), idx_a, idx_b, npc * n_slots).reshape(npc, n_slots, m)


def _gather_pieces(pieces, idx):
    npc, v, m = pieces.shape
    offs = (jnp.arange(npc, dtype=jnp.int32) * v).reshape((npc,) + (1,) * idx.ndim)
    flat = (idx[None] + offs).reshape(-1)
    return _gather_rows(pieces.reshape(npc * v, m), flat).reshape((npc,) + idx.shape + (m,))


def _moe_ffn_kernel(te_ref, nu_ref, x_ref, w1_ref, w3_ref, w2_ref, y_ref):
    @pl.when(pl.program_id(0) < nu_ref[0])
    def _():
        x = _load_unpacked(x_ref).astype(BF16)
        hid = _silu(_dot(x, w1_ref[...].astype(BF16))) * _dot(x, w3_ref[...].astype(BF16))
        _store_pieces(y_ref, _pack_halves(_dot(hid.astype(BF16), w2_ref[...].astype(BF16))))


def _moe_ffn(xs, te, n_used, w1, w3, w2, j):
    npc, p, m = xs.shape
    d = 2 * npc * m
    dff = w1.shape[3]
    tm = MOE_ROWS

    def row(i, te_ref, nu_ref):
        return (0, jnp.minimum(i, nu_ref[0] - 1), 0)

    def expert(i, te_ref, nu_ref):
        return (j, te_ref[i], 0, 0)

    grid_spec = pltpu.PrefetchScalarGridSpec(
        num_scalar_prefetch=2, grid=(p // tm,),
        in_specs=[pl.BlockSpec((npc, tm, m), row),
                  _resident((None, None, d, dff), expert),
                  _resident((None, None, d, dff), expert),
                  _resident((None, None, dff, d), expert)],
        out_specs=pl.BlockSpec((npc, tm, m), row))
    return pl.pallas_call(
        _moe_ffn_kernel, grid_spec=grid_spec,
        out_shape=jax.ShapeDtypeStruct((npc, p, m), jnp.int32),
        compiler_params=_cparams("arbitrary"), name="moe_ffn",
    )(te, n_used, xs, w1, w3, w2)


def _moe_combine_kernel(ya_ref, yb_ref, route_ref, h_ref, gt_ref, o_ref):
    r = route_ref[...]
    y = r[:, 2:3] * _load_unpacked(ya_ref) + r[:, 3:4] * _load_unpacked(yb_ref)
    o_ref[...] = h_ref[...] + gt_ref[...] * y


def _moe_combine(yg, route, h, mods, n_batch, tiles_per_batch, ctx_tiles):
    d = h.shape[1]
    npc, _, r, m = yg.shape
    tm = TOKEN_TILE
    per = tiles_per_batch - ctx_tiles
    h_idx = lambda i: ((i // per) * tiles_per_batch + ctx_tiles + i % per, 0)
    if ctx_tiles == 0:
        mod_idx = _mod_row_spec(5, tiles_per_batch, n_batch)
    else:
        mod_idx = lambda i: (i // per, 5, 0, 0)
    return pl.pallas_call(
        _moe_combine_kernel, grid=(r // tm,),
        in_specs=[pl.BlockSpec((npc, None, tm, m), lambda i: (0, 0, i, 0)),
                  pl.BlockSpec((npc, None, tm, m), lambda i: (0, 1, i, 0)),
                  pl.BlockSpec((tm, LANES), lambda i: (i, 0)),
                  pl.BlockSpec((tm, d), h_idx),
                  pl.BlockSpec((None, None, 1, d), mod_idx)],
        out_specs=pl.BlockSpec((tm, d), lambda i: (i, 0)),
        out_shape=jax.ShapeDtypeStruct((r, d), F32),
        compiler_params=_cparams("parallel"), name="moe_combine",
    )(yg, yg, route, h, mods)


def _proj_moe_residual(o, w_o, h, g, w_router, w1, w3, w2, j, mods, n_batch, t, n_ctx, latent_only):
    tpb = t // TOKEN_TILE
    ctx_tiles = n_ctx // TOKEN_TILE if latent_only else 0
    h, u_pieces, route, route_t, counts = _proj_router(o, w_o, j, h, g, mods, n_batch, tpb, ctx_tiles, w_router)
    if latent_only:
        npc, _, m = u_pieces.shape
        u_pieces = u_pieces.reshape(npc, n_batch, t, m)[:, :, n_ctx:].reshape(npc, -1, m)
        route = route.reshape(n_batch, t, LANES)[:, n_ctx:].reshape(-1, LANES)
        route_t = route_t.reshape(-1, n_batch, t)[:, :, n_ctx:].reshape(route_t.shape[0], -1)
    n_exp = w1.shape[1]
    slot, n_slots, te, n_used = _route_tables(route_t, counts[0, :n_exp].astype(jnp.int32), MOE_ROWS, n_exp)
    xs = _scatter_pieces(u_pieces, slot, n_slots)
    ys = _moe_ffn(xs, te, n_used, w1, w3, w2, j)
    yg = _gather_pieces(ys, slot)
    if latent_only:
        return _moe_combine(yg, route, h, mods, n_batch, tpb, ctx_tiles)
    return yg, route, h, mods


def kernel(x, c, ctx, c_ctx, w_mod, b_mod, norm_mix_g, norm_ffn_g, ev_w_in, ev_w_out, hg_lb_logits, hg_onorm_g, rg_conv_w, rg_conv_b, rg_wa, rg_ba, rg_wx, rg_bx, rg_lambda, na_w_qkv, na_w_o, na_q_g, na_k_g, na_rpb, ffn_w1, ffn_w3, ffn_w2, moe_router, moe_w1, moe_w3, moe_w2):
    bsz, seq, d = x.shape
    n_ctx = ctx.shape[1]
    t = n_ctx + seq
    depth = w_mod.shape[0]
    assert n_ctx % TOKEN_TILE == 0 and seq % TOKEN_TILE == 0 and t % ROW_TILE == 0 and n_ctx <= ROW_TILE

    rows = -(-(bsz + 1) // 8) * 8
    cond = jnp.zeros((rows, d), F32).at[:bsz].set(c).at[bsz].set(c_ctx)
    mods_all = _adaln(cond, w_mod, b_mod).reshape(depth, rows, N_MOD, 1, d)

    lb_p = jax.nn.softmax(hg_lb_logits.astype(F32), axis=0)
    lb_all = jnp.cumsum(lb_p, axis=0) - lb_p[0]

    h = None
    pending = None
    for layer in range(depth):
        j = layer // 2
        mods = mods_all[layer]
        g_mix = norm_mix_g[layer].reshape(1, d)
        if layer % 2 == 0:
            if layer == 0:
                h, p = _concat_modulated_matmul(ctx, x, g_mix, mods, ev_w_in, j, F32, bsz)
            elif pending is None:
                p = _modulated_matmul(h, g_mix, mods, ev_w_in, j, F32, ROW_TILE // 2, bsz, t, n_ctx)
            else:
                h, p = _combined_modulated_matmul(*pending, g_mix, mods, ev_w_in, j, F32, ROW_TILE // 2,
                                                  bsz, t, n_ctx)
                pending = None
            p3 = p.reshape(bsz, t, -1)
            hg = _hgrn2(p3, lb_all[j], n_ctx)
            half = hg.shape[-1]
            rg = _rglru(p3, 5 * half // (d - half), rg_conv_w[j], rg_conv_b[j], rg_wa[j], rg_ba[j],
                        rg_wx[j], rg_bx[j], rg_lambda[j], n_ctx)
            h = _merge_swiglu(hg.reshape(bsz * t, half), rg.reshape(bsz * t, d - half), p, hg_onorm_g[j],
                              ev_w_out, h, norm_ffn_g[layer].reshape(1, d), ffn_w1, ffn_w3, ffn_w2, j,
                              mods, bsz, t, n_ctx)
        else:
            qkv = _modulated_matmul(h, g_mix, mods, na_w_qkv, j, BF16, ROW_TILE, bsz, t, n_ctx)
            o = _na_attention(qkv.reshape(bsz, t, 3 * d), na_q_g[j], na_k_g[j], na_rpb[j], n_ctx)
            res = _proj_moe_residual(o.reshape(bsz * t, d), na_w_o, h, norm_ffn_g[layer].reshape(1, d), moe_router[j],
                                     moe_w1, moe_w3, moe_w2, j, mods, bsz, t, n_ctx, layer == depth - 1)
            if layer == depth - 1:
                h = res
            else:
                pending = res
    if depth % 2 == 0:
        return h.reshape(bsz, seq, d)
    return h.reshape(bsz, t, d)[:, n_ctx:]
```
